```python
import math
import jax, jax.numpy as jnp
from jax import lax
import numpy as np

D_MODEL = 1024
BATCH = 16
SEQ = 4096
DEPTH = 1

CHUNK = 64
D_MIX = D_MODEL
POOL_WIDTH = D_MIX // 2
POOL_WINDOWS = (2, 4, 8, 16)
POOL_GROUPS = len(POOL_WINDOWS)
POOL_GW = POOL_WIDTH // POOL_GROUPS
HGRN_WIDTH = D_MIX - POOL_WIDTH
HGRN_HEADS = 4
HGRN_DK = HGRN_WIDTH // HGRN_HEADS
HGRN_DV = HGRN_WIDTH // HGRN_HEADS
IN_COLS = POOL_WIDTH + 2 * HGRN_HEADS * HGRN_DK + 2 * HGRN_HEADS * HGRN_DV
N_MEM = 256
XATTN_HEADS = 4
XATTN_HD = D_MODEL // XATTN_HEADS
D_FF = 4 * D_MODEL
EPS = 1e-6

kernel_name = "hybrid_pool_hgrn2_xattn_block"


def rmsnorm(x, gain):
    xf = x.astype(jnp.float32)
    y = xf * lax.rsqrt(jnp.mean(xf * xf, axis=-1, keepdims=True) + EPS)
    return (y * gain.astype(jnp.float32)).astype(x.dtype)


def pool_mixer(u, w_grp, scale):
    B, S, P = u.shape
    uf = u.astype(jnp.float32)
    cs = jnp.concatenate([jnp.zeros((B, 1, P), jnp.float32), jnp.cumsum(uf, axis=1)], axis=1)
    t = jnp.arange(S)
    outs = []
    for g, w in enumerate(POOL_WINDOWS):
        sl = slice(g * POOL_GW, (g + 1) * POOL_GW)
        lo = jnp.maximum(t + 1 - w, 0)
        win_sum = cs[:, 1:, sl] - cs[:, lo, sl]
        cnt = jnp.minimum(t + 1, w).astype(jnp.float32)[None, :, None]
        outs.append(win_sum / cnt - uf[..., sl])
    y = jnp.stack(outs, axis=2).astype(u.dtype)
    y = jnp.einsum('bsgc,gcd->bsgd', y, w_grp).reshape(B, S, P)
    return y * scale


def hgrn2_chunkwise(q, log_f, k, v):
    B, S, H, DK = q.shape
    DV = v.shape[-1]
    n = S // CHUNK

    def to_chunks(a):
        return a.reshape(B, n, CHUNK, H, a.shape[-1]).transpose(0, 1, 3, 2, 4)

    q, log_f, k, v = map(to_chunks, (q, log_f, k, v))
    G = jnp.cumsum(log_f, axis=3)
    G_last = G[:, :, :, -1:]
    G_mid = G[:, :, :, CHUNK // 2 - 1:CHUNK // 2]

    q_rel = q * jnp.exp(G - G_mid)
    k_rel = k * jnp.exp(G_mid - G)
    scores = jnp.einsum('bnhtk,bnhsk->bnhts', q_rel, k_rel)
    causal = jnp.tril(jnp.ones((CHUNK, CHUNK), dtype=bool))
    scores = jnp.where(causal, scores, jnp.zeros_like(scores))
    o_intra = jnp.einsum('bnhts,bnhsv->bnhtv', scores, v)

    k_end = k * jnp.exp(G_last - G)
    dS = jnp.einsum('bnhsk,bnhsv->bnhkv', k_end, v)
    decay = jnp.exp(G_last[:, :, :, 0, :])

    def step(state, inp):
        dS_c, d_c = inp
        return d_c[..., None] * state + dS_c, state

    S0 = jnp.zeros((B, H, DK, DV), q.dtype)
    _, S_prev = lax.scan(step, S0, (dS.transpose(1, 0, 2, 3, 4), decay.transpose(1, 0, 2, 3)))
    S_prev = S_prev.transpose(1, 0, 2, 3, 4)
    o_inter = jnp.einsum('bnhtk,bnhkv->bnhtv', q * jnp.exp(G), S_prev)
    o = o_intra + o_inter
    return o.transpose(0, 1, 3, 2, 4).reshape(B, S, H, DV)


def hgrn2_mixer(z, lb_theta, layer, o_norm):
    B, S, _ = z.shape
    hk = HGRN_HEADS * HGRN_DK
    hv = HGRN_HEADS * HGRN_DV
    zq, zf, zi, zg = jnp.split(z, [hk, 2 * hk, 2 * hk + hv], axis=-1)
    p = jax.nn.softmax(lb_theta.astype(jnp.float32), axis=0)
    lb = jnp.cumsum(p, axis=0)[layer]
    f = lb + (1.0 - lb) * jax.nn.sigmoid(zf.astype(jnp.float32))
    log_f = jnp.log(f).astype(z.dtype)
    k = (1.0 - f).astype(z.dtype)
    q = jax.nn.silu(zq)
    shp = (B, S, HGRN_HEADS, -1)
    o = hgrn2_chunkwise(q.reshape(shp), log_f.reshape(shp), k.reshape(shp), zi.reshape(shp))
    o = rmsnorm(o, o_norm.reshape(HGRN_HEADS, HGRN_DV))
    return o.reshape(B, S, hv) * jax.nn.silu(zg)


def cross_attention(h, mem, wq, wkv, wo):
    B, S, _ = h.shape
    q = (h @ wq).reshape(B, S, XATTN_HEADS, XATTN_HD)
    kv = mem @ wkv
    k, v = jnp.split(kv, 2, axis=-1)
    k = k.reshape(B, N_MEM, XATTN_HEADS, XATTN_HD)
    v = v.reshape(B, N_MEM, XATTN_HEADS, XATTN_HD)
    s = jnp.einsum('bshd,bmhd->bhsm', q, k).astype(jnp.float32) / math.sqrt(XATTN_HD)
    p = jax.nn.softmax(s, axis=-1).astype(h.dtype)
    o = jnp.einsum('bhsm,bmhd->bshd', p, v).reshape(B, S, D_MODEL)
    return o @ wo


def _fwd_setup_inputs(seed: int = 0) -> dict:
    key = jax.random.key(seed)
    ks = jax.random.split(key, 24)
    n = jax.random.normal
    L = DEPTH
    return {
        "x": n(ks[0], (BATCH, SEQ, D_MODEL), jnp.float32),
        "mem": n(ks[1], (BATCH, N_MEM, D_MODEL), jnp.float32),
        "norm_mix": 1.0 + 0.02 * n(ks[2], (L, D_MODEL), jnp.float32),
        "w_in": n(ks[3], (L, D_MODEL, IN_COLS), jnp.float32) * D_MODEL ** -0.5,
        "pool_w": n(ks[4], (L, POOL_GROUPS, POOL_GW, POOL_GW), jnp.float32) * POOL_GW ** -0.5,
        "pool_scale": 1.0 + 0.02 * n(ks[5], (L, POOL_WIDTH), jnp.float32),
        "lb_theta": 0.1 * n(ks[6], (L + 1, HGRN_HEADS * HGRN_DK), jnp.float32),
        "hgrn_norm": 1.0 + 0.02 * n(ks[7], (L, HGRN_HEADS * HGRN_DV), jnp.float32),
        "w_out": n(ks[8], (L, D_MIX, D_MODEL), jnp.float32) * D_MIX ** -0.5,
        "norm_xq": 1.0 + 0.02 * n(ks[9], (L, D_MODEL), jnp.float32),
        "norm_mem": 1.0 + 0.02 * n(ks[10], (L, D_MODEL), jnp.float32),
        "xw_q": n(ks[11], (L, D_MODEL, D_MODEL), jnp.float32) * D_MODEL ** -0.5,
        "xw_kv": n(ks[12], (L, D_MODEL, 2 * D_MODEL), jnp.float32) * D_MODEL ** -0.5,
        "xw_o": n(ks[13], (L, D_MODEL, D_MODEL), jnp.float32) * D_MODEL ** -0.5,
        "norm_mlp": 1.0 + 0.02 * n(ks[14], (L, D_MODEL), jnp.float32),
        "w_up": n(ks[15], (L, D_MODEL, D_FF), jnp.float32) * D_MODEL ** -0.5,
        "w_down": n(ks[16], (L, D_FF, D_MODEL), jnp.float32) * D_FF ** -0.5,
        "norm_final": 1.0 + 0.02 * n(ks[17], (D_MODEL,), jnp.float32),
    }


def _fwd_reference(x, mem, norm_mix, w_in, pool_w, pool_scale, lb_theta, hgrn_norm, w_out,
              norm_xq, norm_mem, xw_q, xw_kv, xw_o, norm_mlp, w_up, w_down, norm_final):
    h = x
    for l in range(DEPTH):
        u = rmsnorm(h, norm_mix[l]) @ w_in[l]
        u_pool, u_hgrn = u[..., :POOL_WIDTH], u[..., POOL_WIDTH:]
        y_pool = pool_mixer(u_pool, pool_w[l], pool_scale[l])
        y_hgrn = hgrn2_mixer(u_hgrn, lb_theta, l, hgrn_norm[l])
        h = h + jnp.concatenate([y_pool, y_hgrn], axis=-1) @ w_out[l]
        h = h + cross_attention(rmsnorm(h, norm_xq[l]), rmsnorm(mem, norm_mem[l]),
                                xw_q[l], xw_kv[l], xw_o[l])
        a = jax.nn.relu(rmsnorm(h, norm_mlp[l]) @ w_up[l])
        h = h + (a * a) @ w_down[l]
    return rmsnorm(h, norm_final)


import jax as _jax
import jax.numpy as _jnp

TWIN_FORMAT = 'train_step'
FWD_PARAMS = ['x', 'mem', 'norm_mix', 'w_in', 'pool_w', 'pool_scale', 'lb_theta', 'hgrn_norm', 'w_out', 'norm_xq', 'norm_mem', 'xw_q', 'xw_kv', 'xw_o', 'norm_mlp', 'w_up', 'w_down', 'norm_final']
TWIN_WEIGHTS = ['norm_mix', 'w_in', 'pool_w', 'pool_scale', 'lb_theta', 'hgrn_norm', 'w_out', 'norm_xq', 'norm_mem', 'xw_q', 'xw_kv', 'xw_o', 'norm_mlp', 'w_up', 'w_down', 'norm_final']
TWIN_DIFF_INPUT = 'x'
TWIN_INPUTS = ['x', 'mem', 'norm_mix', 'w_in', 'pool_w', 'pool_scale', 'lb_theta', 'hgrn_norm', 'w_out', 'norm_xq', 'norm_mem', 'xw_q', 'xw_kv', 'xw_o', 'norm_mlp', 'w_up', 'w_down', 'norm_final', 'loss_target', 'm_norm_mix', 'm_w_in', 'm_pool_w', 'm_pool_scale', 'm_lb_theta', 'm_hgrn_norm', 'm_w_out', 'm_norm_xq', 'm_norm_mem', 'm_xw_q', 'm_xw_kv', 'm_xw_o', 'm_norm_mlp', 'm_w_up', 'm_w_down', 'm_norm_final', 'v_norm_mix', 'v_w_in', 'v_pool_w', 'v_pool_scale', 'v_lb_theta', 'v_hgrn_norm', 'v_w_out', 'v_norm_xq', 'v_norm_mem', 'v_xw_q', 'v_xw_kv', 'v_xw_o', 'v_norm_mlp', 'v_w_up', 'v_w_down', 'v_norm_final']
TWIN_OUTPUTS = ['loss', 'grad_x', 'grad_norm_mix', 'grad_w_in', 'grad_pool_w', 'grad_pool_scale', 'grad_lb_theta', 'grad_hgrn_norm', 'grad_w_out', 'grad_norm_xq', 'grad_norm_mem', 'grad_xw_q', 'grad_xw_kv', 'grad_xw_o', 'grad_norm_mlp', 'grad_w_up', 'grad_w_down', 'grad_norm_final', 'delta_norm_mix', 'delta_w_in', 'delta_pool_w', 'delta_pool_scale', 'delta_lb_theta', 'delta_hgrn_norm', 'delta_w_out', 'delta_norm_xq', 'delta_norm_mem', 'delta_xw_q', 'delta_xw_kv', 'delta_xw_o', 'delta_norm_mlp', 'delta_w_up', 'delta_w_down', 'delta_norm_final', 'new_m_norm_mix', 'new_m_w_in', 'new_m_pool_w', 'new_m_pool_scale', 'new_m_lb_theta', 'new_m_hgrn_norm', 'new_m_w_out', 'new_m_norm_xq', 'new_m_norm_mem', 'new_m_xw_q', 'new_m_xw_kv', 'new_m_xw_o', 'new_m_norm_mlp', 'new_m_w_up', 'new_m_w_down', 'new_m_norm_final', 'new_v_norm_mix', 'new_v_w_in', 'new_v_pool_w', 'new_v_pool_scale', 'new_v_lb_theta', 'new_v_hgrn_norm', 'new_v_w_out', 'new_v_norm_xq', 'new_v_norm_mem', 'new_v_xw_q', 'new_v_xw_kv', 'new_v_xw_o', 'new_v_norm_mlp', 'new_v_w_up', 'new_v_w_down', 'new_v_norm_final']
TWIN_LEAF_KINDS = {'loss': 'loss', 'grad_x': 'grad_x', 'grad_norm_mix': 'grad_w', 'grad_w_in': 'grad_w', 'grad_pool_w': 'grad_w', 'grad_pool_scale': 'grad_w', 'grad_lb_theta': 'grad_w', 'grad_hgrn_norm': 'grad_w', 'grad_w_out': 'grad_w', 'grad_norm_xq': 'grad_w', 'grad_norm_mem': 'grad_w', 'grad_xw_q': 'grad_w', 'grad_xw_kv': 'grad_w', 'grad_xw_o': 'grad_w', 'grad_norm_mlp': 'grad_w', 'grad_w_up': 'grad_w', 'grad_w_down': 'grad_w', 'grad_norm_final': 'grad_w', 'delta_norm_mix': 'delta_w', 'delta_w_in': 'delta_w', 'delta_pool_w': 'delta_w', 'delta_pool_scale': 'delta_w', 'delta_lb_theta': 'delta_w', 'delta_hgrn_norm': 'delta_w', 'delta_w_out': 'delta_w', 'delta_norm_xq': 'delta_w', 'delta_norm_mem': 'delta_w', 'delta_xw_q': 'delta_w', 'delta_xw_kv': 'delta_w', 'delta_xw_o': 'delta_w', 'delta_norm_mlp': 'delta_w', 'delta_w_up': 'delta_w', 'delta_w_down': 'delta_w', 'delta_norm_final': 'delta_w', 'new_m_norm_mix': 'new_m', 'new_m_w_in': 'new_m', 'new_m_pool_w': 'new_m', 'new_m_pool_scale': 'new_m', 'new_m_lb_theta': 'new_m', 'new_m_hgrn_norm': 'new_m', 'new_m_w_out': 'new_m', 'new_m_norm_xq': 'new_m', 'new_m_norm_mem': 'new_m', 'new_m_xw_q': 'new_m', 'new_m_xw_kv': 'new_m', 'new_m_xw_o': 'new_m', 'new_m_norm_mlp': 'new_m', 'new_m_w_up': 'new_m', 'new_m_w_down': 'new_m', 'new_m_norm_final': 'new_m', 'new_v_norm_mix': 'new_v', 'new_v_w_in': 'new_v', 'new_v_pool_w': 'new_v', 'new_v_pool_scale': 'new_v', 'new_v_lb_theta': 'new_v', 'new_v_hgrn_norm': 'new_v', 'new_v_w_out': 'new_v', 'new_v_norm_xq': 'new_v', 'new_v_norm_mem': 'new_v', 'new_v_xw_q': 'new_v', 'new_v_xw_kv': 'new_v', 'new_v_xw_o': 'new_v', 'new_v_norm_mlp': 'new_v', 'new_v_w_up': 'new_v', 'new_v_w_down': 'new_v', 'new_v_norm_final': 'new_v'}


def _forward(args):
    return _fwd_reference(*[args[k] for k in FWD_PARAMS])


def _output_shape():
    out = _jax.eval_shape(lambda: _forward(_fwd_setup_inputs(0)))
    return out.shape, out.dtype

N_MICROBATCH = 1
ADAM_LR = 0.001
ADAM_B1 = 0.9
ADAM_B2 = 0.999
ADAM_EPS = 1e-08
ADAM_WD = 0.01
ADAM_STEP = 10
PER_EXAMPLE_BATCH_AXIS = {'x': 0, 'mem': 0, 'loss_target': 0}
SHARED_INPUTS = []
_WEIGHT_DTYPES = {'norm_mix': _jnp.float32, 'w_in': _jnp.float32, 'pool_w': _jnp.float32, 'pool_scale': _jnp.float32, 'lb_theta': _jnp.float32, 'hgrn_norm': _jnp.float32, 'w_out': _jnp.float32, 'norm_xq': _jnp.float32, 'norm_mem': _jnp.float32, 'xw_q': _jnp.float32, 'xw_kv': _jnp.float32, 'xw_o': _jnp.float32, 'norm_mlp': _jnp.float32, 'w_up': _jnp.float32, 'w_down': _jnp.float32, 'norm_final': _jnp.float32}
MOMENT_SCALE = {'norm_mix': 2.137260e-01, 'w_in': 1.194475e-01, 'pool_w': 1.927298e-01, 'pool_scale': 1.913133e-01, 'lb_theta': 1.212665e-02, 'hgrn_norm': 1.370304e-01, 'w_out': 1.627545e-01, 'norm_xq': 2.385400e-02, 'norm_mem': 3.355198e-02, 'xw_q': 2.263426e-02, 'xw_kv': 2.360524e-02, 'xw_o': 2.414516e-02, 'norm_mlp': 2.104684e-01, 'w_up': 1.011025e-01, 'w_down': 1.948603e-01, 'norm_final': 6.430796e+01}


def _to_microbatches(a, axis):
    t = _jnp.moveaxis(a, axis, 0)
    t = t.reshape((N_MICROBATCH, t.shape[0] // N_MICROBATCH) + t.shape[1:])
    return _jnp.moveaxis(t, 1, axis + 1)


def setup_inputs(seed: int = 0) -> dict:
    inp = _fwd_setup_inputs(seed)
    key = _jax.random.fold_in(_jax.random.key(seed), 7919)
    shape, _ = _output_shape()
    out = dict(inp)
    out["loss_target"] = _jax.random.normal(_jax.random.fold_in(key, 0), shape, _jnp.float32)
    for i, name in enumerate(TWIN_WEIGHTS):
        w = inp[name].astype(_jnp.float32)
        if MOMENT_SCALE is None:
            s = _jnp.sqrt(_jnp.mean(_jnp.square(w)) + 1e-30)
        else:
            s = MOMENT_SCALE[name]
        km, kv = _jax.random.split(_jax.random.fold_in(key, i + 1))
        out[name] = w
        out["m_" + name] = s * _jax.random.normal(km, w.shape, _jnp.float32)
        out["v_" + name] = (s * s) * _jax.random.uniform(kv, w.shape, _jnp.float32, 0.5, 1.5)
    if N_MICROBATCH > 1:
        for name, axis in PER_EXAMPLE_BATCH_AXIS.items():
            out[name] = _to_microbatches(out[name], axis)
    return {'x': out['x'], 'mem': out['mem'], 'norm_mix': out['norm_mix'], 'w_in': out['w_in'], 'pool_w': out['pool_w'], 'pool_scale': out['pool_scale'], 'lb_theta': out['lb_theta'], 'hgrn_norm': out['hgrn_norm'], 'w_out': out['w_out'], 'norm_xq': out['norm_xq'], 'norm_mem': out['norm_mem'], 'xw_q': out['xw_q'], 'xw_kv': out['xw_kv'], 'xw_o': out['xw_o'], 'norm_mlp': out['norm_mlp'], 'w_up': out['w_up'], 'w_down': out['w_down'], 'norm_final': out['norm_final'], 'loss_target': out['loss_target'], 'm_norm_mix': out['m_norm_mix'], 'm_w_in': out['m_w_in'], 'm_pool_w': out['m_pool_w'], 'm_pool_scale': out['m_pool_scale'], 'm_lb_theta': out['m_lb_theta'], 'm_hgrn_norm': out['m_hgrn_norm'], 'm_w_out': out['m_w_out'], 'm_norm_xq': out['m_norm_xq'], 'm_norm_mem': out['m_norm_mem'], 'm_xw_q': out['m_xw_q'], 'm_xw_kv': out['m_xw_kv'], 'm_xw_o': out['m_xw_o'], 'm_norm_mlp': out['m_norm_mlp'], 'm_w_up': out['m_w_up'], 'm_w_down': out['m_w_down'], 'm_norm_final': out['m_norm_final'], 'v_norm_mix': out['v_norm_mix'], 'v_w_in': out['v_w_in'], 'v_pool_w': out['v_pool_w'], 'v_pool_scale': out['v_pool_scale'], 'v_lb_theta': out['v_lb_theta'], 'v_hgrn_norm': out['v_hgrn_norm'], 'v_w_out': out['v_w_out'], 'v_norm_xq': out['v_norm_xq'], 'v_norm_mem': out['v_norm_mem'], 'v_xw_q': out['v_xw_q'], 'v_xw_kv': out['v_xw_kv'], 'v_xw_o': out['v_xw_o'], 'v_norm_mlp': out['v_norm_mlp'], 'v_w_up': out['v_w_up'], 'v_w_down': out['v_w_down'], 'v_norm_final': out['v_norm_final']}


def _loss(weights, diff, rest, loss_target):
    with _jax.named_scope("forward"):
        args = {**rest, TWIN_DIFF_INPUT: diff, **{k: w.astype(_WEIGHT_DTYPES[k]) for k, w in weights.items()}}
        y = _forward(args)
    with _jax.named_scope("loss_head"):
        err = _jnp.square(y.astype(_jnp.float32) - loss_target)
        return 0.5 * _jnp.sum(_jnp.mean(err, axis=-1)) if err.ndim else 0.5 * err


def _adamw(w, g, m, v):
    m = ADAM_B1 * m + (1.0 - ADAM_B1) * g
    v = ADAM_B2 * v + (1.0 - ADAM_B2) * _jnp.square(g)
    m_hat = m / (1.0 - ADAM_B1 ** ADAM_STEP)
    v_hat = v / (1.0 - ADAM_B2 ** ADAM_STEP)
    delta = -ADAM_LR * (m_hat / (_jnp.sqrt(v_hat) + ADAM_EPS) + ADAM_WD * w)
    return delta, m, v


def reference(x, mem, norm_mix, w_in, pool_w, pool_scale, lb_theta, hgrn_norm, w_out, norm_xq, norm_mem, xw_q, xw_kv, xw_o, norm_mlp, w_up, w_down, norm_final, loss_target, m_norm_mix, m_w_in, m_pool_w, m_pool_scale, m_lb_theta, m_hgrn_norm, m_w_out, m_norm_xq, m_norm_mem, m_xw_q, m_xw_kv, m_xw_o, m_norm_mlp, m_w_up, m_w_down, m_norm_final, v_norm_mix, v_w_in, v_pool_w, v_pool_scale, v_lb_theta, v_hgrn_norm, v_w_out, v_norm_xq, v_norm_mem, v_xw_q, v_xw_kv, v_xw_o, v_norm_mlp, v_w_up, v_w_down, v_norm_final):
    given = dict(x=x, mem=mem, norm_mix=norm_mix, w_in=w_in, pool_w=pool_w, pool_scale=pool_scale, lb_theta=lb_theta, hgrn_norm=hgrn_norm, w_out=w_out, norm_xq=norm_xq, norm_mem=norm_mem, xw_q=xw_q, xw_kv=xw_kv, xw_o=xw_o, norm_mlp=norm_mlp, w_up=w_up, w_down=w_down, norm_final=norm_final, loss_target=loss_target, m_norm_mix=m_norm_mix, m_w_in=m_w_in, m_pool_w=m_pool_w, m_pool_scale=m_pool_scale, m_lb_theta=m_lb_theta, m_hgrn_norm=m_hgrn_norm, m_w_out=m_w_out, m_norm_xq=m_norm_xq, m_norm_mem=m_norm_mem, m_xw_q=m_xw_q, m_xw_kv=m_xw_kv, m_xw_o=m_xw_o, m_norm_mlp=m_norm_mlp, m_w_up=m_w_up, m_w_down=m_w_down, m_norm_final=m_norm_final, v_norm_mix=v_norm_mix, v_w_in=v_w_in, v_pool_w=v_pool_w, v_pool_scale=v_pool_scale, v_lb_theta=v_lb_theta, v_hgrn_norm=v_hgrn_norm, v_w_out=v_w_out, v_norm_xq=v_norm_xq, v_norm_mem=v_norm_mem, v_xw_q=v_xw_q, v_xw_kv=v_xw_kv, v_xw_o=v_xw_o, v_norm_mlp=v_norm_mlp, v_w_up=v_w_up, v_w_down=v_w_down, v_norm_final=v_norm_final)
    weights = {n: given[n] for n in TWIN_WEIGHTS}
    shared = {n: given[n] for n in SHARED_INPUTS}
    per_example = {n: given[n] for n in ['x', 'mem']}
    grad_fn = _jax.value_and_grad(_loss, argnums=(0, 1))

    def one_microbatch(ex, loss_target):
        ex = dict(ex)
        diff = ex.pop(TWIN_DIFF_INPUT)
        return grad_fn(weights, diff, {**shared, **ex}, loss_target)

    if N_MICROBATCH == 1:
        loss, (grad_w, grad_x) = one_microbatch(per_example, given["loss_target"])
    else:
        def body(carry, xs):
            loss_sum, grad_sum = carry
            l_k, (gw_k, gx_k) = one_microbatch(xs[0], xs[1])
            with _jax.named_scope("update"):
                return (loss_sum + l_k, _jax.tree.map(_jnp.add, grad_sum, gw_k)), gx_k

        init = (_jnp.zeros((), _jnp.float32), _jax.tree.map(_jnp.zeros_like, weights))
        (loss, grad_w), grad_x = _jax.lax.scan(body, init, (per_example, given["loss_target"]))
    with _jax.named_scope("update"):
        delta_w, new_m, new_v = {}, {}, {}
        for n in TWIN_WEIGHTS:
            delta_w[n], new_m[n], new_v[n] = _adamw(weights[n], grad_w[n], given["m_" + n], given["v_" + n])
    return (loss, grad_x, *[grad_w[n] for n in TWIN_WEIGHTS], *[delta_w[n] for n in TWIN_WEIGHTS],
            *[new_m[n] for n in TWIN_WEIGHTS], *[new_v[n] for n in TWIN_WEIGHTS])
```

```python
import functools

import jax
import jax.numpy as jnp
from jax import lax
from jax.experimental import pallas as pl
from jax.experimental.pallas import tpu as pltpu

F32 = jnp.float32
BF16 = jnp.bfloat16
EPS = 1e-6
CHUNK = 64
POOL_WINDOWS = (2, 4, 8, 16)
POOL_HALO = 16
HEAD_W = 128
XATTN_HEADS = 4
N_DEV = 8
ADAM_LR = 0.001
ADAM_B1 = 0.9
ADAM_B2 = 0.999
ADAM_EPS = 1e-08
ADAM_WD = 0.01
ADAM_STEP = 10
V7X_VMEM_LIMIT = 52 * 1024 * 1024
MESH = pl.DeviceIdType.MESH


def _cparams(dims):
    return pltpu.CompilerParams(dimension_semantics=dims, vmem_limit_bytes=V7X_VMEM_LIMIT)


def _sigmoid(v):
    return 1.0 / (1.0 + jnp.exp(-v))


def _dot(a, b):
    return jnp.dot(a, b, preferred_element_type=F32)


def _dot_nt(a, b):
    return lax.dot_general(a, b, (((1,), (1,)), ((), ())), preferred_element_type=F32)


def _dot_tn(a, b):
    return lax.dot_general(a, b, (((0,), (0,)), ((), ())), preferred_element_type=F32)


def _split3(v):
    hi = v.astype(BF16)
    r1 = v - hi.astype(F32)
    mid = r1.astype(BF16)
    lo = (r1 - mid.astype(F32)).astype(BF16)
    return hi, mid, lo


def _tri_apply(tri, v):
    hi, mid, lo = _split3(v)
    return _dot(tri, hi) + _dot(tri, mid) + _dot(tri, lo)


def norm_mm(h, g, w, *, name, tm, tn, out_dtype, out3d=False):
    T, D = h.shape
    N = w.shape[1]
    nj = N // tn

    def body(h_ref, g_ref, w_ref, o_ref, n_ref):
        @pl.when(pl.program_id(1) == 0)
        def _():
            x = h_ref[...]
            r = lax.rsqrt(jnp.mean(x * x, axis=-1, keepdims=True) + EPS)
            n_ref[...] = (x * r * g_ref[...]).astype(BF16)

        o_ref[...] = _dot(n_ref[...], w_ref[...]).astype(o_ref.dtype)

    if out3d:
        o_shape = jax.ShapeDtypeStruct((nj, T, tn), out_dtype)
        o_spec = pl.BlockSpec((None, tm, tn), lambda i, j: (j, i, 0))
    else:
        o_shape = jax.ShapeDtypeStruct((T, N), out_dtype)
        o_spec = pl.BlockSpec((tm, tn), lambda i, j: (i, j))
    return pl.pallas_call(
        body, name=name, grid=(T // tm, nj),
        in_specs=[pl.BlockSpec((tm, D), lambda i, j: (i, 0)),
                  pl.BlockSpec((1, D), lambda i, j: (0, 0)),
                  pl.BlockSpec((D, tn), lambda i, j: (0, j))],
        out_specs=[o_spec, pl.BlockSpec((tm, D), lambda i, j: (i, 0))],
        out_shape=[o_shape, jax.ShapeDtypeStruct((T, D), BF16)],
        compiler_params=_cparams(("parallel", "arbitrary")),
    )(h, g, w)


def mm_nn(a, w, res, *, name, tm, tn, tk, a3d=False, relu2=False):
    if a3d:
        nk, T, _ = a.shape
        a_spec = pl.BlockSpec((None, tm, tk), lambda i, j, k: (k, i, 0))
    else:
        T, K = a.shape
        nk = K // tk
        a_spec = pl.BlockSpec((tm, tk), lambda i, j, k: (i, k))
    N = w.shape[1]

    def body(a_ref, w_ref, r_ref, o_ref, acc_ref):
        k = pl.program_id(2)
        av = a_ref[...]
        if relu2:
            av = jnp.maximum(av, 0.0)
            av = av * av
        part = _dot(av.astype(BF16), w_ref[...])

        @pl.when(k == 0)
        def _():
            acc_ref[...] = part

        @pl.when(k > 0)
        def _():
            acc_ref[...] += part

        @pl.when(k == nk - 1)
        def _():
            o_ref[...] = r_ref[...] + acc_ref[...]

    return pl.pallas_call(
        body, name=name, grid=(T // tm, N // tn, nk),
        in_specs=[a_spec,
                  pl.BlockSpec((tk, tn), lambda i, j, k: (k, j)),
                  pl.BlockSpec((tm, tn), lambda i, j, k: (i, j))],
        out_specs=pl.BlockSpec((tm, tn), lambda i, j, k: (i, j)),
        out_shape=jax.ShapeDtypeStruct((T, N), F32),
        scratch_shapes=[pltpu.VMEM((tm, tn), F32)],
        compiler_params=_cparams(("parallel", "parallel", "arbitrary")),
    )(a, w, res)


def mm_nt(a, w, *, name, tm, tn, tk, out_dtype, a3d=False, out3d=False, relu2_of=None):
    if a3d:
        nk, T, _ = a.shape
        a_spec = pl.BlockSpec((None, tm, tk), lambda i, j, k: (k, i, 0))
    else:
        T, K = a.shape
        nk = K // tk
        a_spec = pl.BlockSpec((tm, tk), lambda i, j, k: (i, k))
    N = w.shape[0]
    nj = N // tn
    has_z = relu2_of is not None

    def body(*refs):
        if has_z:
            a_ref, w_ref, z_ref, o_ref, acc_ref = refs
        else:
            a_ref, w_ref, o_ref, acc_ref = refs
        k = pl.program_id(2)
        part = _dot_nt(a_ref[...].astype(BF16), w_ref[...])

        @pl.when(k == 0)
        def _():
            acc_ref[...] = part

        @pl.when(k > 0)
        def _():
            acc_ref[...] += part

        @pl.when(k == nk - 1)
        def _():
            out = acc_ref[...]
            if has_z:
                out = out * (2.0 * jnp.maximum(z_ref[...], 0.0))
            o_ref[...] = out.astype(o_ref.dtype)

    in_specs = [a_spec, pl.BlockSpec((tn, tk), lambda i, j, k: (j, k))]
    args = [a, w]
    if has_z:
        in_specs.append(pl.BlockSpec((tm, tn), lambda i, j, k: (i, j)))
        args.append(relu2_of)
    if out3d:
        o_shape = jax.ShapeDtypeStruct((nj, T, tn), out_dtype)
        o_spec = pl.BlockSpec((None, tm, tn), lambda i, j, k: (j, i, 0))
    else:
        o_shape = jax.ShapeDtypeStruct((T, N), out_dtype)
        o_spec = pl.BlockSpec((tm, tn), lambda i, j, k: (i, j))
    return pl.pallas_call(
        body, name=name, grid=(T // tm, nj, nk),
        in_specs=in_specs, out_specs=o_spec, out_shape=o_shape,
        scratch_shapes=[pltpu.VMEM((tm, tn), F32)],
        compiler_params=_cparams(("parallel", "parallel", "arbitrary")),
    )(*args)


def mm_nt_normbwd(a, w, h, g, dres, *, name, tm, tk, a3d=False):
    if a3d:
        nk, T, _ = a.shape
        a_spec = pl.BlockSpec((None, tm, tk), lambda i, k: (k, i, 0))
    else:
        T, K = a.shape
        nk = K // tk
        a_spec = pl.BlockSpec((tm, tk), lambda i, k: (i, k))
    D = w.shape[0]
    with_dh = dres is not None

    def body(*refs):
        if with_dh:
            a_ref, w_ref, h_ref, g_ref, r_ref, dh_ref, dhb_ref, dg_ref, acc_ref = refs
        else:
            a_ref, w_ref, h_ref, g_ref, dg_ref, acc_ref = refs
        i = pl.program_id(0)
        k = pl.program_id(1)
        part = _dot_nt(a_ref[...].astype(BF16), w_ref[...])

        @pl.when(k == 0)
        def _():
            acc_ref[...] = part

        @pl.when(k > 0)
        def _():
            acc_ref[...] += part

        @pl.when(k == nk - 1)
        def _():
            dn = acc_ref[...]
            x = h_ref[...]
            r = lax.rsqrt(jnp.mean(x * x, axis=-1, keepdims=True) + EPS)
            xr = x * r
            dgp = jnp.sum(dn * xr, axis=0, keepdims=True)

            @pl.when(i == 0)
            def _():
                dg_ref[...] = dgp

            @pl.when(i > 0)
            def _():
                dg_ref[...] += dgp

            if with_dh:
                dyg = dn * g_ref[...]
                dx = r * (dyg - xr * jnp.mean(dyg * xr, axis=-1, keepdims=True))
                out = r_ref[...] + dx
                dh_ref[...] = out
                dhb_ref[...] = out.astype(BF16)

    row = pl.BlockSpec((tm, D), lambda i, k: (i, 0))
    vec = pl.BlockSpec((1, D), lambda i, k: (0, 0))
    in_specs = [a_spec, pl.BlockSpec((D, tk), lambda i, k: (0, k)), row, vec]
    args = [a, w, h, g]
    if with_dh:
        in_specs.append(row)
        args.append(dres)
        out_specs = [row, row, vec]
        out_shape = [jax.ShapeDtypeStruct((T, D), F32), jax.ShapeDtypeStruct((T, D), BF16),
                     jax.ShapeDtypeStruct((1, D), F32)]
    else:
        out_specs = vec
        out_shape = jax.ShapeDtypeStruct((1, D), F32)
    return pl.pallas_call(
        body, name=name, grid=(T // tm, nk),
        in_specs=in_specs, out_specs=out_specs, out_shape=out_shape,
        scratch_shapes=[pltpu.VMEM((tm, D), F32)],
        compiler_params=_cparams(("arbitrary", "arbitrary")),
    )(*args)


def mm_tn(a, b, *, name, tt, tko, tn, a3d=False, b3d=False, relu2=False):
    if a3d:
        nko, T, _ = a.shape
        a_spec = pl.BlockSpec((None, tt, tko), lambda kk, j, t: (kk, t, 0))
    else:
        T, K = a.shape
        nko = K // tko
        a_spec = pl.BlockSpec((tt, tko), lambda kk, j, t: (t, kk))
    if b3d:
        nj = b.shape[0]
        b_spec = pl.BlockSpec((None, tt, tn), lambda kk, j, t: (j, t, 0))
    else:
        nj = b.shape[1] // tn
        b_spec = pl.BlockSpec((tt, tn), lambda kk, j, t: (t, j))
    nt = T // tt

    def body(a_ref, b_ref, o_ref, acc_ref):
        t = pl.program_id(2)
        av = a_ref[...]
        if relu2:
            av = jnp.maximum(av, 0.0)
            av = av * av
        part = _dot_tn(av.astype(BF16), b_ref[...].astype(BF16))

        @pl.when(t == 0)
        def _():
            acc_ref[...] = part

        @pl.when(t > 0)
        def _():
            acc_ref[...] += part

        @pl.when(t == nt - 1)
        def _():
            o_ref[...] = acc_ref[...].astype(BF16)

    return pl.pallas_call(
        body, name=name, grid=(nko, nj, nt),
        in_specs=[a_spec, b_spec],
        out_specs=pl.BlockSpec((tko, tn), lambda kk, j, t: (kk, j)),
        out_shape=jax.ShapeDtypeStruct((nko * tko, nj * tn), BF16),
        scratch_shapes=[pltpu.VMEM((tko, tn), F32)],
        compiler_params=_cparams(("parallel", "parallel", "arbitrary")),
    )(a, b)


def _chunk_tri(tm, upper):
    r = lax.broadcasted_iota(jnp.int32, (tm, tm), 0)
    c = lax.broadcasted_iota(jnp.int32, (tm, tm), 1)
    same = (r // CHUNK) == (c // CHUNK)
    keep = (c >= r) if upper else (c <= r)
    return jnp.where(same & keep, 1.0, 0.0).astype(BF16)


def _pool_windows_back(ext_ref, e0_rows, tm):
    n = tm + 32
    ext_ref[1, 8:n] = ext_ref[0, 8:n] + ext_ref[0, 7:n - 1]
    ext_ref[2, 16:n] = ext_ref[1, 16:n] + ext_ref[1, 14:n - 2]
    ext_ref[3, 24:n] = ext_ref[2, 24:n] + ext_ref[2, 20:n - 4]
    s2 = ext_ref[1, 32:n]
    s4 = ext_ref[2, 32:n]
    s8 = ext_ref[3, 32:n]
    s16 = s8 + ext_ref[3, 24:n - 8]
    return s2, s4, s8, s16


def _pool_windows_fwd(ext_ref, tm):
    n = tm + 32
    ext_ref[1, 0:n - 8] = ext_ref[0, 0:n - 8] + ext_ref[0, 1:n - 7]
    ext_ref[2, 0:n - 16] = ext_ref[1, 0:n - 16] + ext_ref[1, 2:n - 14]
    ext_ref[3, 0:n - 24] = ext_ref[2, 0:n - 24] + ext_ref[2, 4:n - 20]
    s2 = ext_ref[1, 0:tm]
    s4 = ext_ref[2, 0:tm]
    s8 = ext_ref[3, 0:tm]
    s16 = s8 + ext_ref[3, 8:tm + 8]
    return s2, s4, s8, s16


def _select_window(g, s2, s4, s8, s16):
    return jnp.where(g == 0, s2, jnp.where(g == 1, s4, jnp.where(g == 2, s8, s16)))


def _pool_count(g, pos):
    width = lax.shift_left(jnp.int32(2), g)
    return jnp.minimum(pos + 1, width).astype(F32)


def _hgrn_gates(zq, zf, th):
    lb = _sigmoid(th[0:1, :] - th[1:2, :])
    sig = _sigmoid(zf)
    f = lb + (1.0 - lb) * sig
    sq = _sigmoid(zq)
    return lb, sig, f, sq


def mixer_fwd(u5, pool_w_bf, scale4, theta4, gn4, *, seqs, seq_len, tm):
    T = u5.shape[1]
    tps = seq_len // tm
    nc = tm // CHUNK
    W = HEAD_W

    def body(u_ref, pw_ref, sc_ref, th_ref, gn_ref, y_ref, o_ref, st_ref, halo_ref, ext_ref, s_ref):
        g = pl.program_id(0)
        i = pl.program_id(2)

        @pl.when(i == 0)
        def _():
            halo_ref[...] = jnp.zeros_like(halo_ref)
            s_ref[...] = jnp.zeros_like(s_ref)

        row = lax.broadcasted_iota(jnp.int32, (tm, 1), 0)

        up = u_ref[0]
        ext_ref[0, 0:16] = jnp.zeros((16, W), F32)
        ext_ref[0, 16:32] = halo_ref[...]
        ext_ref[0, 32:32 + tm] = up
        win = _select_window(g, *_pool_windows_back(ext_ref, None, tm))
        p = win / _pool_count(g, i * tm + row) - up
        halo_ref[...] = up[tm - POOL_HALO:tm]
        y_ref[0] = (_dot(p.astype(BF16), pw_ref[...]) * sc_ref[...]).astype(BF16)

        zq, zf, zi, zg = u_ref[1], u_ref[2], u_ref[3], u_ref[4]
        lb, sig, f, sq = _hgrn_gates(zq, zf, th_ref[...])
        logf = jnp.log(f)
        kk = 1.0 - f
        q = zq * sq
        G = _tri_apply(_chunk_tri(tm, False), logf)
        causal = (lax.broadcasted_iota(jnp.int32, (CHUNK, CHUNK), 1)
                  <= lax.broadcasted_iota(jnp.int32, (CHUNK, CHUNK), 0))
        st = s_ref[...]
        outs = []
        for c in range(nc):
            sl = slice(c * CHUNK, (c + 1) * CHUNK)
            Gc, qc, kc = G[sl], q[sl], kk[sl]
            vb = zi[sl].astype(BF16)
            Gm = Gc[CHUNK // 2 - 1:CHUNK // 2]
            Gl = Gc[CHUNK - 1:CHUNK]
            a = _dot_nt((qc * jnp.exp(Gc - Gm)).astype(BF16), (kc * jnp.exp(Gm - Gc)).astype(BF16))
            a = jnp.where(causal, a, 0.0)
            o_intra = _dot(a.astype(BF16), vb)
            st_ref[c] = st
            o_inter = _dot_nt((qc * jnp.exp(Gc)).astype(BF16), st.astype(BF16))
            outs.append(o_intra + o_inter)
            d_st = _dot_tn(vb, (kc * jnp.exp(Gl - Gc)).astype(BF16))
            st = st * jnp.exp(Gl) + d_st
        s_ref[...] = st
        o = jnp.concatenate(outs, axis=0)
        o_ref[...] = o
        r = lax.rsqrt(jnp.mean(o * o, axis=-1, keepdims=True) + EPS)
        y_ref[1] = (o * r * gn_ref[...] * (zg * _sigmoid(zg))).astype(BF16)

    def rb(s, i):
        return s * tps + i

    return pl.pallas_call(
        body, name="mixer_fwd", grid=(4, seqs, tps),
        in_specs=[pl.BlockSpec((5, tm, W), lambda g, s, i: (0, rb(s, i), g)),
                  pl.BlockSpec((None, W, W), lambda g, s, i: (g, 0, 0)),
                  pl.BlockSpec((None, 1, W), lambda g, s, i: (g, 0, 0)),
                  pl.BlockSpec((None, 2, W), lambda g, s, i: (g, 0, 0)),
                  pl.BlockSpec((None, 1, W), lambda g, s, i: (g, 0, 0))],
        out_specs=[pl.BlockSpec((2, tm, W), lambda g, s, i: (0, rb(s, i), g)),
                   pl.BlockSpec((tm, W), lambda g, s, i: (rb(s, i), g)),
                   pl.BlockSpec((nc, None, W, W), lambda g, s, i: (rb(s, i), g, 0, 0))],
        out_shape=[jax.ShapeDtypeStruct((2, T, 4 * W), BF16),
                   jax.ShapeDtypeStruct((T, 4 * W), F32),
                   jax.ShapeDtypeStruct((T // CHUNK, 4, W, W), F32)],
        scratch_shapes=[pltpu.VMEM((POOL_HALO, W), F32),
                        pltpu.VMEM((4, tm + 32, W), F32),
                        pltpu.VMEM((W, W), F32)],
        compiler_params=_cparams(("arbitrary", "arbitrary", "arbitrary")),
    )(u5, pool_w_bf, scale4, theta4, gn4)


def mixer_bwd(u5, dy2, o_pre, st_prev, pool_w_bf, scale4, theta4, gn4, *, seqs, seq_len, tm):
    T = u5.shape[1]
    tps = seq_len // tm
    nc = tm // CHUNK
    W = HEAD_W
    hb = tm // POOL_HALO

    def body(u_ref, uh_ref, dy_ref, o_ref, st_ref, pw_ref, sc_ref, th_ref, gn_ref,
             du_ref, dpw_ref, dsc_ref, dlb_ref, dgn_ref, nxt_ref, ext_ref, ds_ref):
        g = pl.program_id(0)
        s = pl.program_id(1)
        i = pl.program_id(2)
        tile = tps - 1 - i
        first = (s == 0) & (i == 0)

        @pl.when(i == 0)
        def _():
            nxt_ref[...] = jnp.zeros_like(nxt_ref)
            ds_ref[...] = jnp.zeros_like(ds_ref)

        row = lax.broadcasted_iota(jnp.int32, (tm, 1), 0)
        cnt = _pool_count(g, tile * tm + row)

        def accumulate(ref, val):
            @pl.when(first)
            def _():
                ref[...] = val

            @pl.when(jnp.logical_not(first))
            def _():
                ref[...] += val

        up = u_ref[0]
        ext_ref[0, 0:16] = jnp.zeros((16, W), F32)
        ext_ref[0, 16:32] = jnp.where(tile == 0, 0.0, uh_ref[...])
        ext_ref[0, 32:32 + tm] = up
        win = _select_window(g, *_pool_windows_back(ext_ref, None, tm))
        pb = (win / cnt - up).astype(BF16)
        dyp = dy_ref[0]
        z = _dot(pb, pw_ref[...])
        accumulate(dsc_ref, jnp.sum(dyp * z, axis=0, keepdims=True))
        dz = (dyp * sc_ref[...]).astype(BF16)
        accumulate(dpw_ref, _dot_tn(pb, dz))
        dp = _dot_nt(dz, pw_ref[...])
        e = dp / cnt
        ext_ref[0, 0:tm] = e
        ext_ref[0, tm:tm + 16] = nxt_ref[...]
        ext_ref[0, tm + 16:tm + 32] = jnp.zeros((16, W), F32)
        lead = _select_window(g, *_pool_windows_fwd(ext_ref, tm))
        nxt_ref[...] = e[0:POOL_HALO]
        du_ref[0] = (lead - dp).astype(BF16)

        zq, zf, zi, zg = u_ref[1], u_ref[2], u_ref[3], u_ref[4]
        lb, sig, f, sq = _hgrn_gates(zq, zf, th_ref[...])
        logf = jnp.log(f)
        kk = 1.0 - f
        q = zq * sq
        G = _tri_apply(_chunk_tri(tm, False), logf)

        dyh = dy_ref[1]
        o = o_ref[...]
        sg = _sigmoid(zg)
        r = lax.rsqrt(jnp.mean(o * o, axis=-1, keepdims=True) + EPS)
        orr = o * r
        gn = gn_ref[...]
        du_ref[4] = (dyh * (orr * gn) * (sg * (1.0 + zg * (1.0 - sg)))).astype(BF16)
        don = dyh * (zg * sg)
        accumulate(dgn_ref, jnp.sum(don * orr, axis=0, keepdims=True))
        dog = don * gn
        do = r * (dog - orr * jnp.mean(dog * orr, axis=-1, keepdims=True))

        causal = (lax.broadcasted_iota(jnp.int32, (CHUNK, CHUNK), 1)
                  <= lax.broadcasted_iota(jnp.int32, (CHUNK, CHUNK), 0))
        crow = lax.broadcasted_iota(jnp.int32, (CHUNK, 1), 0)
        dsn = ds_ref[...]
        dq_parts, dk_parts, dv_parts, dg_parts = [None] * nc, [None] * nc, [None] * nc, [None] * nc
        for c in reversed(range(nc)):
            sl = slice(c * CHUNK, (c + 1) * CHUNK)
            Gc, qc, kc = G[sl], q[sl], kk[sl]
            vb = zi[sl].astype(BF16)
            dob = do[sl].astype(BF16)
            Gm = Gc[CHUNK // 2 - 1:CHUNK // 2]
            Gl = Gc[CHUNK - 1:CHUNK]
            e_q, e_k, e_e, e_g = jnp.exp(Gc - Gm), jnp.exp(Gm - Gc), jnp.exp(Gl - Gc), jnp.exp(Gc)
            decay = jnp.exp(Gl)
            qr, kr, ke, qg = qc * e_q, kc * e_k, kc * e_e, qc * e_g
            qrb, krb, keb, qgb = qr.astype(BF16), kr.astype(BF16), ke.astype(BF16), qg.astype(BF16)
            st = st_ref[c]
            dsnb = dsn.astype(BF16)
            a = jnp.where(causal, _dot_nt(qrb, krb), 0.0).astype(BF16)
            da = jnp.where(causal, _dot_nt(dob, vb), 0.0).astype(BF16)
            dv_parts[c] = _dot_tn(a, dob) + _dot_nt(keb, dsnb)
            dqr = _dot(da, krb)
            dkr = _dot_tn(da, qrb)
            dqg = _dot(dob, st.astype(BF16))
            dke = _dot(vb, dsnb)
            ddecay = jnp.sum(dsn * st, axis=0, keepdims=True)
            dsn = _dot_tn(dob, qgb) + dsn * decay
            t_qr, t_kr, t_qg, t_ke = dqr * qr, dkr * kr, dqg * qg, dke * ke
            dq_parts[c] = dqr * e_q + dqg * e_g
            dk_parts[c] = dkr * e_k + dke * e_e
            dgm = jnp.sum(t_kr - t_qr, axis=0, keepdims=True)
            dgl = jnp.sum(t_ke, axis=0, keepdims=True) + ddecay * decay
            dg_parts[c] = (t_qr - t_kr + t_qg - t_ke
                           + jnp.where(crow == CHUNK // 2 - 1, dgm, 0.0)
                           + jnp.where(crow == CHUNK - 1, dgl, 0.0))
        ds_ref[...] = dsn
        dq = jnp.concatenate(dq_parts, axis=0)
        dk = jnp.concatenate(dk_parts, axis=0)
        dv = jnp.concatenate(dv_parts, axis=0)
        dG = jnp.concatenate(dg_parts, axis=0)
        dlogf = _tri_apply(_chunk_tri(tm, True), dG)
        df = dlogf / f - dk
        du_ref[1] = (dq * (sq * (1.0 + zq * (1.0 - sq)))).astype(BF16)
        du_ref[2] = (df * (1.0 - lb) * (sig * (1.0 - sig))).astype(BF16)
        du_ref[3] = dv.astype(BF16)
        accumulate(dlb_ref, jnp.sum(df * (1.0 - sig), axis=0, keepdims=True) * (lb * (1.0 - lb)))

    def rb(s, i):
        return s * tps + (tps - 1 - i)

    vec = pl.BlockSpec((None, 1, W), lambda g, s, i: (g, 0, 0))
    mat = pl.BlockSpec((None, W, W), lambda g, s, i: (g, 0, 0))
    return pl.pallas_call(
        body, name="mixer_bwd", grid=(4, seqs, tps),
        in_specs=[pl.BlockSpec((5, tm, W), lambda g, s, i: (0, rb(s, i), g)),
                  pl.BlockSpec((None, POOL_HALO, W), lambda g, s, i: (0, jnp.maximum(rb(s, i) * hb - 1, 0), g)),
                  pl.BlockSpec((2, tm, W), lambda g, s, i: (0, rb(s, i), g)),
                  pl.BlockSpec((tm, W), lambda g, s, i: (rb(s, i), g)),
                  pl.BlockSpec((nc, None, W, W), lambda g, s, i: (rb(s, i), g, 0, 0)),
                  mat, vec,
                  pl.BlockSpec((None, 2, W), lambda g, s, i: (g, 0, 0)),
                  vec],
        out_specs=[pl.BlockSpec((5, tm, W), lambda g, s, i: (0, rb(s, i), g)), mat, vec, vec, vec],
        out_shape=[jax.ShapeDtypeStruct((5, T, 4 * W), BF16),
                   jax.ShapeDtypeStruct((4, W, W), F32),
                   jax.ShapeDtypeStruct((4, 1, W), F32),
                   jax.ShapeDtypeStruct((4, 1, W), F32),
                   jax.ShapeDtypeStruct((4, 1, W), F32)],
        scratch_shapes=[pltpu.VMEM((POOL_HALO, W), F32),
                        pltpu.VMEM((4, tm + 32, W), F32),
                        pltpu.VMEM((W, W), F32)],
        compiler_params=_cparams(("arbitrary", "arbitrary", "arbitrary")),
    )(u5, u5, dy2, o_pre, st_prev, pool_w_bf, scale4, theta4, gn4)


def _attn_probs(q, k, hd):
    s = _dot_nt(q, k) * (1.0 / (hd ** 0.5))
    e = jnp.exp(s - jnp.max(s, axis=-1, keepdims=True))
    return e / jnp.sum(e, axis=-1, keepdims=True)


def attn_fwd(q, kv3, *, seqs, seq_len, n_mem, tm):
    T, D = q.shape
    hd = D // XATTN_HEADS
    tps = seq_len // tm

    def body(q_ref, kv_ref, o_ref):
        p = _attn_probs(q_ref[...], kv_ref[0], hd)
        o_ref[...] = _dot(p.astype(BF16), kv_ref[1]).astype(BF16)

    return pl.pallas_call(
        body, name="attn_fwd", grid=(seqs, XATTN_HEADS, tps),
        in_specs=[pl.BlockSpec((tm, hd), lambda b, h, i: (b * tps + i, h)),
                  pl.BlockSpec((2, n_mem, hd), lambda b, h, i: (0, b, h))],
        out_specs=pl.BlockSpec((tm, hd), lambda b, h, i: (b * tps + i, h)),
        out_shape=jax.ShapeDtypeStruct((T, D), BF16),
        compiler_params=_cparams(("parallel", "parallel", "arbitrary")),
    )(q, kv3)


def attn_bwd(q, kv3, do, *, seqs, seq_len, n_mem, tm):
    T, D = q.shape
    hd = D // XATTN_HEADS
    tps = seq_len // tm

    def body(q_ref, kv_ref, do_ref, dq_ref, dkv_ref):
        i = pl.program_id(2)
        qv, k, v, dov = q_ref[...], kv_ref[0], kv_ref[1], do_ref[...]
        p = _attn_probs(qv, k, hd)
        dp = _dot_nt(dov, v)
        ds = (p * (dp - jnp.sum(dp * p, axis=-1, keepdims=True)) * (1.0 / (hd ** 0.5))).astype(BF16)
        dq_ref[...] = _dot(ds, k).astype(BF16)
        dk = _dot_tn(ds, qv)
        dv = _dot_tn(p.astype(BF16), dov)

        @pl.when(i == 0)
        def _():
            dkv_ref[0] = dk
            dkv_ref[1] = dv

        @pl.when(i > 0)
        def _():
            dkv_ref[0] += dk
            dkv_ref[1] += dv

    qspec = pl.BlockSpec((tm, hd), lambda b, h, i: (b * tps + i, h))
    kvspec = pl.BlockSpec((2, n_mem, hd), lambda b, h, i: (0, b, h))
    return pl.pallas_call(
        body, name="attn_bwd", grid=(seqs, XATTN_HEADS, tps),
        in_specs=[qspec, kvspec, qspec],
        out_specs=[qspec, kvspec],
        out_shape=[jax.ShapeDtypeStruct((T, D), BF16), jax.ShapeDtypeStruct((2, seqs * n_mem, D), F32)],
        compiler_params=_cparams(("parallel", "parallel", "arbitrary")),
    )(q, kv3, do)


def final_loss(h, g, target, *, tm):
    T, D = h.shape

    def body(h_ref, g_ref, t_ref, dh_ref, dhb_ref, ls_ref, dg_ref):
        i = pl.program_id(0)
        x = h_ref[...]
        gv = g_ref[...]
        r = lax.rsqrt(jnp.mean(x * x, axis=-1, keepdims=True) + EPS)
        xr = x * r
        d = xr * gv - t_ref[...]
        dy = d * (1.0 / D)
        dyg = dy * gv
        dx = r * (dyg - xr * jnp.mean(dyg * xr, axis=-1, keepdims=True))
        dh_ref[...] = dx
        dhb_ref[...] = dx.astype(BF16)
        ls = jnp.sum(d * d, axis=0, keepdims=True)
        dg = jnp.sum(dy * xr, axis=0, keepdims=True)

        @pl.when(i == 0)
        def _():
            ls_ref[...] = ls
            dg_ref[...] = dg

        @pl.when(i > 0)
        def _():
            ls_ref[...] += ls
            dg_ref[...] += dg

    row = pl.BlockSpec((tm, D), lambda i: (i, 0))
    vec = pl.BlockSpec((1, D), lambda i: (0, 0))
    return pl.pallas_call(
        body, name="final_loss", grid=(T // tm,),
        in_specs=[row, vec, row], out_specs=[row, row, vec, vec],
        out_shape=[jax.ShapeDtypeStruct((T, D), F32), jax.ShapeDtypeStruct((T, D), BF16),
                   jax.ShapeDtypeStruct((1, D), F32), jax.ShapeDtypeStruct((1, D), F32)],
        compiler_params=_cparams(("arbitrary",)),
    )(h, g, target)


def _my_place():
    return lax.axis_index("x"), lax.axis_index("y"), lax.axis_index("c")


def _slot_of(px, py, pc):
    return 4 * px + 2 * py + pc


def allgather_shards(shard):
    R, C = shard.shape

    def body(x_ref, out_ref, send_sems, recv_sems, local_sem):
        x, y, c = _my_place()
        me, sibling = (x, y, c), (x, y, 1 - c)
        chips = [(1 - x, y), (x, 1 - y), (1 - x, 1 - y)]

        def copy(k, block, to, src=None):
            dst = out_ref.at[_slot_of(*block)]
            return pltpu.make_async_remote_copy(
                src_ref=dst if src is None else src, dst_ref=dst,
                send_sem=send_sems.at[k], recv_sem=recv_sems.at[k],
                device_id=to, device_id_type=MESH)

        mine = pltpu.make_async_copy(x_ref, out_ref.at[_slot_of(*me)], local_sem)
        mine.start()
        first = [copy(0, me, sibling, src=x_ref)]
        first += [copy(1 + j, me, (*chip, c), src=x_ref) for j, chip in enumerate(chips)]
        for cp in first:
            cp.start()
        passed = [copy(4 + j, (*chip, c), sibling) for j, chip in enumerate(chips)]
        for j, chip in enumerate(chips):
            copy(1 + j, (*chip, c), me).wait_recv()
            passed[j].start()
        copy(0, sibling, me).wait_recv()
        for j, chip in enumerate(chips):
            copy(4 + j, (*chip, 1 - c), me).wait_recv()
        for cp in first + passed:
            cp.wait_send()
        mine.wait()

    return pl.pallas_call(
        body, name="allgather_shards",
        out_shape=jax.ShapeDtypeStruct((N_DEV, R, C), shard.dtype),
        in_specs=[pl.BlockSpec(memory_space=pltpu.HBM)],
        out_specs=pl.BlockSpec(memory_space=pltpu.HBM),
        scratch_shapes=[pltpu.SemaphoreType.DMA((7,)), pltpu.SemaphoreType.DMA((7,)), pltpu.SemaphoreType.DMA],
    )(shard)


def exchange_partials(big, small):
    _, R, C = big.shape
    Rs = small.shape[0]

    def body(b_ref, s_ref, br_ref, sr_ref, bsend, brecv, ssend, srecv, lsems):
        x, y, c = _my_place()
        mine = _slot_of(x, y, c)
        loc_b = pltpu.make_async_copy(b_ref.at[mine], br_ref.at[mine], lsems.at[0])
        loc_s = pltpu.make_async_copy(s_ref, sr_ref.at[mine], lsems.at[1])
        loc_b.start()
        loc_s.start()
        copies = []
        for k in range(1, N_DEV):
            px = 1 - x if (k >> 2) & 1 else x
            py = 1 - y if (k >> 1) & 1 else y
            pc = 1 - c if k & 1 else c
            copies.append(pltpu.make_async_remote_copy(
                src_ref=b_ref.at[_slot_of(px, py, pc)], dst_ref=br_ref.at[mine],
                send_sem=bsend.at[k - 1], recv_sem=brecv.at[k - 1],
                device_id=(px, py, pc), device_id_type=MESH))
            copies.append(pltpu.make_async_remote_copy(
                src_ref=s_ref, dst_ref=sr_ref.at[mine],
                send_sem=ssend.at[k - 1], recv_sem=srecv.at[k - 1],
                device_id=(px, py, pc), device_id_type=MESH))
        for cp in copies:
            cp.start()
        for cp in copies:
            cp.wait()
        loc_b.wait()
        loc_s.wait()

    hbm = pl.BlockSpec(memory_space=pltpu.HBM)
    return pl.pallas_call(
        body, name="exchange_partials",
        out_shape=[jax.ShapeDtypeStruct((N_DEV, R, C), big.dtype), jax.ShapeDtypeStruct((N_DEV, Rs, C), small.dtype)],
        in_specs=[hbm, hbm], out_specs=[hbm, hbm],
        scratch_shapes=[pltpu.SemaphoreType.DMA((7,)), pltpu.SemaphoreType.DMA((7,)),
                        pltpu.SemaphoreType.DMA((7,)), pltpu.SemaphoreType.DMA((7,)),
                        pltpu.SemaphoreType.DMA((2,))],
    )(big, small)


def adamw_packed(parts, w, m, v, *, name, tr):
    R, C = w.shape
    c1 = 1.0 - ADAM_B1 ** ADAM_STEP
    c2 = 1.0 - ADAM_B2 ** ADAM_STEP

    def body(p_ref, w_ref, m_ref, v_ref, g_ref, d_ref, nm_ref, nv_ref):
        g = p_ref[0].astype(F32)
        for s in range(1, N_DEV):
            g = g + p_ref[s].astype(F32)
        nm = ADAM_B1 * m_ref[...] + (1.0 - ADAM_B1) * g
        nv = ADAM_B2 * v_ref[...] + (1.0 - ADAM_B2) * (g * g)
        g_ref[...] = g
        nm_ref[...] = nm
        nv_ref[...] = nv
        d_ref[...] = -ADAM_LR * ((nm / c1) / (jnp.sqrt(nv / c2) + ADAM_EPS) + ADAM_WD * w_ref[...])

    blk = pl.BlockSpec((tr, C), lambda i: (i, 0))
    out = jax.ShapeDtypeStruct((R, C), F32)
    return pl.pallas_call(
        body, name=name, grid=(R // tr,),
        in_specs=[pl.BlockSpec((N_DEV, tr, C), lambda i: (0, i, 0)), blk, blk, blk],
        out_specs=[blk, blk, blk, blk], out_shape=[out, out, out, out],
        compiler_params=_cparams(("parallel",)),
    )(parts, w, m, v)


BIG = ("w_in", "w_out", "xw_q", "xw_kv", "xw_o", "w_up", "w_down")
COL_SHARDED = ("w_in", "xw_kv", "w_up")
SMALL = ("pool_w", "norm_mix", "norm_xq", "norm_mem", "norm_mlp", "norm_final",
         "pool_scale", "hgrn_norm", "lb_theta")
WEIGHTS = ("norm_mix", "w_in", "pool_w", "pool_scale", "lb_theta", "hgrn_norm", "w_out", "norm_xq",
           "norm_mem", "xw_q", "xw_kv", "xw_o", "norm_mlp", "w_up", "w_down", "norm_final")


def _rows(a, C):
    return a.reshape(-1, C)


def _size(shp):
    n = 1
    for d in shp:
        n *= d
    return n


def _pack(arrs, C, pad_to=None):
    if all(a.size % C == 0 for a in arrs):
        out = jnp.concatenate([_rows(a, C) for a in arrs], axis=0)
    else:
        out = _rows(jnp.concatenate([a.reshape(-1) for a in arrs]), C)
    if pad_to is not None and out.shape[0] < pad_to:
        out = jnp.concatenate([out, jnp.zeros((pad_to - out.shape[0], C), out.dtype)], axis=0)
    return out


def _unpack(packed, shapes, C):
    flat = any(_size(shp) % C for shp in shapes)
    src = packed.reshape(-1) if flat else packed
    out, r = [], 0
    for shp in shapes:
        n = _size(shp) if flat else _size(shp) // C
        out.append(src[r:r + n].reshape(shp))
        r += n
    return out


def kernel(x, mem, norm_mix, w_in, pool_w, pool_scale, lb_theta, hgrn_norm, w_out, norm_xq, norm_mem, xw_q, xw_kv, xw_o, norm_mlp, w_up, w_down, norm_final, loss_target, m_norm_mix, m_w_in, m_pool_w, m_pool_scale, m_lb_theta, m_hgrn_norm, m_w_out, m_norm_xq, m_norm_mem, m_xw_q, m_xw_kv, m_xw_o, m_norm_mlp, m_w_up, m_w_down, m_norm_final, v_norm_mix, v_w_in, v_pool_w, v_pool_scale, v_lb_theta, v_hgrn_norm, v_w_out, v_norm_xq, v_norm_mem, v_xw_q, v_xw_kv, v_xw_o, v_norm_mlp, v_w_up, v_w_down, v_norm_final):
    w = dict(norm_mix=norm_mix, w_in=w_in, pool_w=pool_w, pool_scale=pool_scale, lb_theta=lb_theta,
             hgrn_norm=hgrn_norm, w_out=w_out, norm_xq=norm_xq, norm_mem=norm_mem, xw_q=xw_q, xw_kv=xw_kv,
             xw_o=xw_o, norm_mlp=norm_mlp, w_up=w_up, w_down=w_down, norm_final=norm_final)
    mom = dict(norm_mix=m_norm_mix, w_in=m_w_in, pool_w=m_pool_w, pool_scale=m_pool_scale, lb_theta=m_lb_theta,
               hgrn_norm=m_hgrn_norm, w_out=m_w_out, norm_xq=m_norm_xq, norm_mem=m_norm_mem, xw_q=m_xw_q,
               xw_kv=m_xw_kv, xw_o=m_xw_o, norm_mlp=m_norm_mlp, w_up=m_w_up, w_down=m_w_down,
               norm_final=m_norm_final)
    var = dict(norm_mix=v_norm_mix, w_in=v_w_in, pool_w=v_pool_w, pool_scale=v_pool_scale, lb_theta=v_lb_theta,
               hgrn_norm=v_hgrn_norm, w_out=v_w_out, norm_xq=v_norm_xq, norm_mem=v_norm_mem, xw_q=v_xw_q,
               xw_kv=v_xw_kv, xw_o=v_xw_o, norm_mlp=v_norm_mlp, w_up=v_w_up, w_down=v_w_down,
               norm_final=v_norm_final)

    seqs, seq_len, D = x.shape
    n_mem = mem.shape[1]
    T = seqs * seq_len
    C = D
    W = HEAD_W
    x2 = x.reshape(T, D)
    mem2 = mem.reshape(seqs * n_mem, D)
    tgt2 = loss_target.reshape(T, D)
    tm_big = min(1024, T)
    tm_mid = min(512, T)
    tm_mix = min(256, seq_len)
    tm_att = min(1024, seq_len)

    shard_bf = _pack([w[n][0].astype(BF16) for n in BIG], C)
    gathered = allgather_shards(shard_bf)
    full = {}
    r = 0
    for n in BIG:
        shp = w[n].shape[1:]
        rows = shp[0] * shp[1] // C
        blk = gathered[:, r:r + rows].reshape((N_DEV,) + shp)
        r += rows
        if n in COL_SHARDED:
            full[n] = blk.transpose(1, 0, 2).reshape(shp[0], N_DEV * shp[1])
        else:
            full[n] = blk.reshape(N_DEV * shp[0], shp[1])

    pool_w_bf = pool_w[0].astype(BF16)
    scale4 = pool_scale.reshape(4, 1, W)
    gn4 = hgrn_norm.reshape(4, 1, W)
    theta4 = lb_theta.reshape(2, 4, W).transpose(1, 0, 2)
    g_final = norm_final.reshape(1, D)

    u5, n1 = norm_mm(x2, norm_mix, full["w_in"], name="in_proj", tm=tm_big, tn=4 * W, out_dtype=F32, out3d=True)
    y2, o_pre, st_prev = mixer_fwd(u5, pool_w_bf, scale4, theta4, gn4, seqs=seqs, seq_len=seq_len, tm=tm_mix)
    h1 = mm_nn(y2, full["w_out"], x2, name="out_proj", tm=tm_big, tn=D, tk=4 * W, a3d=True)
    q, n2 = norm_mm(h1, norm_xq, full["xw_q"], name="q_proj", tm=tm_big, tn=D, out_dtype=BF16)
    kv3, memn = norm_mm(mem2, norm_mem, full["xw_kv"], name="kv_proj", tm=min(512, seqs * n_mem), tn=D,
                        out_dtype=BF16, out3d=True)
    o_att = attn_fwd(q, kv3, seqs=seqs, seq_len=seq_len, n_mem=n_mem, tm=tm_att)
    h2 = mm_nn(o_att, full["xw_o"], h1, name="attn_out_proj", tm=tm_big, tn=D, tk=D)
    a_pre, n3 = norm_mm(h2, norm_mlp, full["w_up"], name="up_proj", tm=tm_big, tn=D, out_dtype=F32)
    h3 = mm_nn(a_pre, full["w_down"], h2, name="down_proj", tm=tm_big, tn=D, tk=D, relu2=True)
    dh3, dh3b, sq_err, dg_final = final_loss(h3, g_final, tgt2, tm=tm_mid)
    loss = lax.psum(0.5 * jnp.sum(sq_err) / D, ("x", "y", "c"))

    dap = mm_nt(dh3b, full["w_down"], name="down_proj_bwd", tm=tm_big, tn=D, tk=D, out_dtype=BF16, relu2_of=a_pre)
    gw_down = mm_tn(a_pre, dh3b, name="down_proj_wgrad", tt=tm_mid, tko=D, tn=D, relu2=True)
    dh2, dh2b, dg_mlp = mm_nt_normbwd(dap, full["w_up"], h2, norm_mlp, dh3, name="up_proj_bwd", tm=tm_mid, tk=D)
    gw_up = mm_tn(n3, dap, name="up_proj_wgrad", tt=tm_mid, tko=D, tn=D)
    do_att = mm_nt(dh2b, full["xw_o"], name="attn_out_proj_bwd", tm=tm_big, tn=D, tk=D, out_dtype=BF16)
    gxw_o = mm_tn(o_att, dh2b, name="attn_out_proj_wgrad", tt=tm_mid, tko=D, tn=D)
    dq, dkv3 = attn_bwd(q, kv3, do_att, seqs=seqs, seq_len=seq_len, n_mem=n_mem, tm=tm_att)
    dh1, dh1b, dg_xq = mm_nt_normbwd(dq, full["xw_q"], h1, norm_xq, dh2, name="q_proj_bwd", tm=tm_mid, tk=D)
    gxw_q = mm_tn(n2, dq, name="q_proj_wgrad", tt=tm_mid, tko=D, tn=D)
    tkv = min(512, seqs * n_mem)
    gxw_kv = mm_tn(memn, dkv3, name="kv_proj_wgrad", tt=tkv, tko=D, tn=D, b3d=True)
    dg_mem = mm_nt_normbwd(dkv3, full["xw_kv"], mem2, norm_mem, None, name="kv_proj_bwd", tm=tkv, tk=D, a3d=True)
    dy2 = mm_nt(dh1b, full["w_out"], name="out_proj_bwd", tm=tm_big, tn=4 * W, tk=D, out_dtype=F32, out3d=True)
    gw_out = mm_tn(y2, dh1b, name="out_proj_wgrad", tt=tm_mid, tko=4 * W, tn=D, a3d=True)
    du5, dpw, dsc, dlb, dgn = mixer_bwd(u5, dy2, o_pre, st_prev, pool_w_bf, scale4, theta4, gn4,
                                        seqs=seqs, seq_len=seq_len, tm=tm_mix)
    dx, _, dg_mix = mm_nt_normbwd(du5, full["w_in"], x2, norm_mix, dh1, name="in_proj_bwd", tm=tm_mid, tk=4 * W,
                                  a3d=True)
    gw_in = mm_tn(n1, du5, name="in_proj_wgrad", tt=tm_mid, tko=D, tn=4 * W, b3d=True)

    gfull = dict(w_in=gw_in, w_out=gw_out, xw_q=gxw_q, xw_kv=gxw_kv, xw_o=gxw_o, w_up=gw_up, w_down=gw_down)
    slots = []
    for n in BIG:
        shp = w[n].shape[1:]
        if n in COL_SHARDED:
            blk = gfull[n].reshape(shp[0], N_DEV, shp[1]).transpose(1, 0, 2)
        else:
            blk = gfull[n].reshape(N_DEV, shp[0], shp[1])
        slots.append(blk.reshape(N_DEV, -1, C))
    big_send = jnp.concatenate(slots, axis=1)
    dlb_row = dlb.reshape(1, 4 * W)
    small_parts = dict(pool_w=dpw, norm_mix=dg_mix, norm_xq=dg_xq, norm_mem=dg_mem, norm_mlp=dg_mlp,
                       norm_final=dg_final, pool_scale=dsc, hgrn_norm=dgn,
                       lb_theta=jnp.concatenate([dlb_row, -dlb_row], axis=0))
    rs_small = -(-sum(w[n].size for n in SMALL) // C // 8) * 8
    small_send = _pack([small_parts[n] for n in SMALL], C, pad_to=rs_small)
    big_recv, small_recv = exchange_partials(big_send, small_send)

    gb, db, mb, vb = adamw_packed(big_recv, _pack([w[n] for n in BIG], C), _pack([mom[n] for n in BIG], C),
                                  _pack([var[n] for n in BIG], C), name="adamw_sharded", tr=64)
    gs, ds, ms, vs = adamw_packed(small_recv, _pack([w[n] for n in SMALL], C, pad_to=rs_small),
                                  _pack([mom[n] for n in SMALL], C, pad_to=rs_small),
                                  _pack([var[n] for n in SMALL], C, pad_to=rs_small), name="adamw_replicated", tr=8)
    res = {}
    for kind, pb, ps in (("g", gb, gs), ("d", db, ds), ("m", mb, ms), ("v", vb, vs)):
        for n, a in zip(BIG, _unpack(pb, [w[n].shape for n in BIG], C)):
            res[kind, n] = a
        for n, a in zip(SMALL, _unpack(ps, [w[n].shape for n in SMALL], C)):
            res[kind, n] = a

    out = [loss, dx.reshape(x.shape)]
    for kind in ("g", "d", "m", "v"):
        out += [res[kind, n] for n in WEIGHTS]
    return tuple(out)
```

```python
import jax
import jax.numpy as jnp
from jax import lax
from jax.experimental import pallas as pl
from jax.experimental.pallas import tpu as pltpu

F32 = jnp.float32
BF16 = jnp.bfloat16
EPS = 1e-6
CHUNK = 64
POOL_HALO = 16
HEAD_W = 128
XATTN_HEADS = 4
N_DEV = 8
N_PEERS = N_DEV - 1
ADAM_LR = 0.001
ADAM_B1 = 0.9
ADAM_B2 = 0.999
ADAM_EPS = 1e-08
ADAM_WD = 0.01
ADAM_STEP = 10
V7X_VMEM_LIMIT = 52 * 1024 * 1024
MESH = pl.DeviceIdType.MESH
HBM = pl.BlockSpec(memory_space=pltpu.HBM)
SEM = pl.BlockSpec(memory_space=pltpu.SEMAPHORE)


def _cparams(dims):
    return pltpu.CompilerParams(dimension_semantics=dims, vmem_limit_bytes=V7X_VMEM_LIMIT)


def _sigmoid(v):
    return 1.0 / (1.0 + jnp.exp(-v))


def _dot(a, b):
    return jnp.dot(a, b, preferred_element_type=F32)


def _dot_nt(a, b):
    return lax.dot_general(a, b, (((1,), (1,)), ((), ())), preferred_element_type=F32)


def _dot_tn(a, b):
    return lax.dot_general(a, b, (((0,), (0,)), ((), ())), preferred_element_type=F32)


def _split3(v):
    hi = v.astype(BF16)
    r1 = v - hi.astype(F32)
    mid = r1.astype(BF16)
    lo = (r1 - mid.astype(F32)).astype(BF16)
    return hi, mid, lo


def _tri_apply(tri, v):
    hi, mid, lo = _split3(v)
    return _dot(tri, hi) + _dot(tri, mid) + _dot(tri, lo)


def _mat_shape(a):
    return a.shape if a.ndim == 2 else (a.shape[1], a.shape[0] * a.shape[2])


def _tile_spec(a, rows, cols, row_of, col_of):
    if a.ndim == 2:
        return pl.BlockSpec((rows, cols), lambda *g: (row_of(*g), col_of(*g)))
    per = a.shape[2] // cols
    return pl.BlockSpec((None, rows, cols), lambda *g: (col_of(*g) // per, row_of(*g), col_of(*g) % per))


def _out_struct(rows, n, slabs, dtype):
    return jax.ShapeDtypeStruct((rows, n) if slabs is None else (slabs, rows, n // slabs), dtype)


def norm_mm(h, g, w, *, name, tm, tn, out_dtype, out_slabs=None):
    T, D = h.shape
    N = _mat_shape(w)[1]
    o_shape = _out_struct(T, N, out_slabs, out_dtype)

    def body(h_ref, g_ref, w_ref, o_ref, n_ref):
        @pl.when(pl.program_id(1) == 0)
        def _():
            x = h_ref[...]
            r = lax.rsqrt(jnp.mean(x * x, axis=-1, keepdims=True) + EPS)
            n_ref[...] = (x * r * g_ref[...]).astype(BF16)

        o_ref[...] = _dot(n_ref[...], w_ref[...]).astype(o_ref.dtype)

    return pl.pallas_call(
        body, name=name, grid=(T // tm, N // tn),
        in_specs=[pl.BlockSpec((tm, D), lambda i, j: (i, 0)),
                  pl.BlockSpec((1, D), lambda i, j: (0, 0)),
                  _tile_spec(w, D, tn, lambda i, j: 0, lambda i, j: j)],
        out_specs=[_tile_spec(o_shape, tm, tn, lambda i, j: i, lambda i, j: j),
                   pl.BlockSpec((tm, D), lambda i, j: (i, 0))],
        out_shape=[o_shape, jax.ShapeDtypeStruct((T, D), BF16)],
        compiler_params=_cparams(("parallel", "arbitrary")),
    )(h, g, w)


def mm_nn(a, w, res, *, name, tm, tn, tk, relu2=False):
    T, K = _mat_shape(a)
    N = w.shape[1]
    nk = K // tk

    def body(a_ref, w_ref, r_ref, o_ref, acc_ref):
        k = pl.program_id(2)
        av = a_ref[...]
        if relu2:
            av = jnp.maximum(av, 0.0)
            av = av * av
        part = _dot(av.astype(BF16), w_ref[...])

        @pl.when(k == 0)
        def _():
            acc_ref[...] = part

        @pl.when(k > 0)
        def _():
            acc_ref[...] += part

        @pl.when(k == nk - 1)
        def _():
            o_ref[...] = r_ref[...] + acc_ref[...]

    return pl.pallas_call(
        body, name=name, grid=(T // tm, N // tn, nk),
        in_specs=[_tile_spec(a, tm, tk, lambda i, j, k: i, lambda i, j, k: k),
                  pl.BlockSpec((tk, tn), lambda i, j, k: (k, j)),
                  pl.BlockSpec((tm, tn), lambda i, j, k: (i, j))],
        out_specs=pl.BlockSpec((tm, tn), lambda i, j, k: (i, j)),
        out_shape=jax.ShapeDtypeStruct((T, N), F32),
        scratch_shapes=[pltpu.VMEM((tm, tn), F32)],
        compiler_params=_cparams(("parallel", "parallel", "arbitrary")),
    )(a, w, res)


def mm_nt(a, w, *, name, tm, tn, tk, out_dtype, out_slabs=None, relu2_of=None, after=None):
    T, K = _mat_shape(a)
    nk = K // tk
    N = w.shape[0]
    has_z = relu2_of is not None
    o_shape = _out_struct(T, N, out_slabs, out_dtype)

    def body(*refs):
        a_ref, w_ref = refs[0], refs[1]
        z_ref = refs[2] if has_z else None
        o_ref, acc_ref = refs[-2], refs[-1]
        k = pl.program_id(2)
        part = _dot_nt(a_ref[...].astype(BF16), w_ref[...])

        @pl.when(k == 0)
        def _():
            acc_ref[...] = part

        @pl.when(k > 0)
        def _():
            acc_ref[...] += part

        @pl.when(k == nk - 1)
        def _():
            out = acc_ref[...]
            if has_z:
                out = out * (2.0 * jnp.maximum(z_ref[...], 0.0))
            o_ref[...] = out.astype(o_ref.dtype)

    in_specs = [_tile_spec(a, tm, tk, lambda i, j, k: i, lambda i, j, k: k),
                pl.BlockSpec((tn, tk), lambda i, j, k: (j, k))]
    args = [a, w]
    if has_z:
        in_specs.append(pl.BlockSpec((tm, tn), lambda i, j, k: (i, j)))
        args.append(relu2_of)
    if after is not None:
        in_specs.append(pl.BlockSpec(after.shape, lambda i, j, k: (0, 0)))
        args.append(after)
    return pl.pallas_call(
        body, name=name, grid=(T // tm, N // tn, nk),
        in_specs=in_specs,
        out_specs=_tile_spec(o_shape, tm, tn, lambda i, j, k: i, lambda i, j, k: j),
        out_shape=o_shape,
        scratch_shapes=[pltpu.VMEM((tm, tn), F32)],
        compiler_params=_cparams(("parallel", "parallel", "arbitrary")),
    )(*args)


def mm_nt_normbwd(a, w, h, g, dres, *, name, tm, tk, after=None):
    T, K = _mat_shape(a)
    nk = K // tk
    D = h.shape[1]
    with_dh = dres is not None

    def body(*refs):
        a_ref, w_ref, h_ref, g_ref = refs[:4]
        if with_dh:
            r_ref = refs[4]
            dh_ref, dhb_ref, dg_ref, acc_ref = refs[-4:]
        else:
            dg_ref, acc_ref = refs[-2:]
        i = pl.program_id(0)
        k = pl.program_id(1)
        part = _dot_nt(a_ref[...].astype(BF16), w_ref[...])

        @pl.when(k == 0)
        def _():
            acc_ref[...] = part

        @pl.when(k > 0)
        def _():
            acc_ref[...] += part

        @pl.when(k == nk - 1)
        def _():
            dn = acc_ref[...]
            x = h_ref[...]
            r = lax.rsqrt(jnp.mean(x * x, axis=-1, keepdims=True) + EPS)
            xr = x * r
            dgp = jnp.sum(dn * xr, axis=0, keepdims=True)

            @pl.when(i == 0)
            def _():
                dg_ref[...] = dgp

            @pl.when(i > 0)
            def _():
                dg_ref[...] += dgp

            if with_dh:
                dyg = dn * g_ref[...]
                dx = r * (dyg - xr * jnp.mean(dyg * xr, axis=-1, keepdims=True))
                out = r_ref[...] + dx
                dh_ref[...] = out
                dhb_ref[...] = out.astype(BF16)

    row = pl.BlockSpec((tm, D), lambda i, k: (i, 0))
    vec = pl.BlockSpec((1, D), lambda i, k: (0, 0))
    in_specs = [_tile_spec(a, tm, tk, lambda i, k: i, lambda i, k: k),
                _tile_spec(w, D, tk, lambda i, k: 0, lambda i, k: k), row, vec]
    args = [a, w, h, g]
    if with_dh:
        in_specs.append(row)
        args.append(dres)
        out_specs = [row, row, vec]
        out_shape = [jax.ShapeDtypeStruct((T, D), F32), jax.ShapeDtypeStruct((T, D), BF16),
                     jax.ShapeDtypeStruct((1, D), F32)]
    else:
        out_specs = vec
        out_shape = jax.ShapeDtypeStruct((1, D), F32)
    if after is not None:
        in_specs.append(pl.BlockSpec(after.shape, lambda i, k: (0, 0)))
        args.append(after)
    return pl.pallas_call(
        body, name=name, grid=(T // tm, nk),
        in_specs=in_specs, out_specs=out_specs, out_shape=out_shape,
        scratch_shapes=[pltpu.VMEM((tm, D), F32)],
        compiler_params=_cparams(("arbitrary", "arbitrary")),
    )(*args)


def mm_tn(a, b, *, name, tt, tko, tn, relu2=False, out_slabs=None):
    T, K = _mat_shape(a)
    N = _mat_shape(b)[1]
    nt = T // tt
    o_shape = _out_struct(K, N, out_slabs, BF16)

    def body(a_ref, b_ref, o_ref, acc_ref):
        t = pl.program_id(2)
        av = a_ref[...]
        if relu2:
            av = jnp.maximum(av, 0.0)
            av = av * av
        part = _dot_tn(av.astype(BF16), b_ref[...].astype(BF16))

        @pl.when(t == 0)
        def _():
            acc_ref[...] = part

        @pl.when(t > 0)
        def _():
            acc_ref[...] += part

        @pl.when(t == nt - 1)
        def _():
            o_ref[...] = acc_ref[...].astype(BF16)

    return pl.pallas_call(
        body, name=name, grid=(K // tko, N // tn, nt),
        in_specs=[_tile_spec(a, tt, tko, lambda kk, j, t: t, lambda kk, j, t: kk),
                  _tile_spec(b, tt, tn, lambda kk, j, t: t, lambda kk, j, t: j)],
        out_specs=_tile_spec(o_shape, tko, tn, lambda kk, j, t: kk, lambda kk, j, t: j),
        out_shape=o_shape,
        scratch_shapes=[pltpu.VMEM((tko, tn), F32)],
        compiler_params=_cparams(("parallel", "parallel", "arbitrary")),
    )(a, b)


def _chunk_tri(tm, upper):
    r = lax.broadcasted_iota(jnp.int32, (tm, tm), 0)
    c = lax.broadcasted_iota(jnp.int32, (tm, tm), 1)
    same = (r // CHUNK) == (c // CHUNK)
    keep = (c >= r) if upper else (c <= r)
    return jnp.where(same & keep, 1.0, 0.0).astype(BF16)


def _pool_windows_back(ext_ref, tm):
    n = tm + 32
    ext_ref[1, 8:n] = ext_ref[0, 8:n] + ext_ref[0, 7:n - 1]
    ext_ref[2, 16:n] = ext_ref[1, 16:n] + ext_ref[1, 14:n - 2]
    ext_ref[3, 24:n] = ext_ref[2, 24:n] + ext_ref[2, 20:n - 4]
    s2 = ext_ref[1, 32:n]
    s4 = ext_ref[2, 32:n]
    s8 = ext_ref[3, 32:n]
    s16 = s8 + ext_ref[3, 24:n - 8]
    return s2, s4, s8, s16


def _pool_windows_fwd(ext_ref, tm):
    n = tm + 32
    ext_ref[1, 0:n - 8] = ext_ref[0, 0:n - 8] + ext_ref[0, 1:n - 7]
    ext_ref[2, 0:n - 16] = ext_ref[1, 0:n - 16] + ext_ref[1, 2:n - 14]
    ext_ref[3, 0:n - 24] = ext_ref[2, 0:n - 24] + ext_ref[2, 4:n - 20]
    s2 = ext_ref[1, 0:tm]
    s4 = ext_ref[2, 0:tm]
    s8 = ext_ref[3, 0:tm]
    s16 = s8 + ext_ref[3, 8:tm + 8]
    return s2, s4, s8, s16


def _select_window(g, s2, s4, s8, s16):
    return jnp.where(g == 0, s2, jnp.where(g == 1, s4, jnp.where(g == 2, s8, s16)))


def _pool_count(g, pos):
    width = lax.shift_left(jnp.int32(2), g)
    return jnp.minimum(pos + 1, width).astype(F32)


def _hgrn_gates(zq, zf, th):
    lb = _sigmoid(th[0:1, :] - th[1:2, :])
    sig = _sigmoid(zf)
    f = lb + (1.0 - lb) * sig
    sq = _sigmoid(zq)
    return lb, sig, f, sq


def mixer_fwd(u5, pool_w_bf, scale4, theta4, gn4, *, seqs, seq_len, tm):
    T = u5.shape[1]
    tps = seq_len // tm
    nc = tm // CHUNK
    W = HEAD_W

    def body(u_ref, pw_ref, sc_ref, th_ref, gn_ref, y_ref, o_ref, st_ref, halo_ref, ext_ref, s_ref):
        g = pl.program_id(0)
        i = pl.program_id(2)

        @pl.when(i == 0)
        def _():
            halo_ref[...] = jnp.zeros_like(halo_ref)
            s_ref[...] = jnp.zeros_like(s_ref)

        row = lax.broadcasted_iota(jnp.int32, (tm, 1), 0)

        up = u_ref[0]
        ext_ref[0, 0:16] = jnp.zeros((16, W), F32)
        ext_ref[0, 16:32] = halo_ref[...]
        ext_ref[0, 32:32 + tm] = up
        win = _select_window(g, *_pool_windows_back(ext_ref, tm))
        p = win / _pool_count(g, i * tm + row) - up
        halo_ref[...] = up[tm - POOL_HALO:tm]
        y_ref[0] = (_dot(p.astype(BF16), pw_ref[...]) * sc_ref[...]).astype(BF16)

        zq, zf, zi, zg = u_ref[1], u_ref[2], u_ref[3], u_ref[4]
        lb, sig, f, sq = _hgrn_gates(zq, zf, th_ref[...])
        logf = jnp.log(f)
        kk = 1.0 - f
        q = zq * sq
        G = _tri_apply(_chunk_tri(tm, False), logf)
        causal = (lax.broadcasted_iota(jnp.int32, (CHUNK, CHUNK), 1)
                  <= lax.broadcasted_iota(jnp.int32, (CHUNK, CHUNK), 0))
        st = s_ref[...]
        outs = []
        for c in range(nc):
            sl = slice(c * CHUNK, (c + 1) * CHUNK)
            Gc, qc, kc = G[sl], q[sl], kk[sl]
            vb = zi[sl].astype(BF16)
            Gm = Gc[CHUNK // 2 - 1:CHUNK // 2]
            Gl = Gc[CHUNK - 1:CHUNK]
            a = _dot_nt((qc * jnp.exp(Gc - Gm)).astype(BF16), (kc * jnp.exp(Gm - Gc)).astype(BF16))
            a = jnp.where(causal, a, 0.0)
            o_intra = _dot(a.astype(BF16), vb)
            st_ref[c] = st
            o_inter = _dot_nt((qc * jnp.exp(Gc)).astype(BF16), st.astype(BF16))
            outs.append(o_intra + o_inter)
            d_st = _dot_tn(vb, (kc * jnp.exp(Gl - Gc)).astype(BF16))
            st = st * jnp.exp(Gl) + d_st
        s_ref[...] = st
        o = jnp.concatenate(outs, axis=0)
        o_ref[...] = o
        r = lax.rsqrt(jnp.mean(o * o, axis=-1, keepdims=True) + EPS)
        y_ref[1] = (o * r * gn_ref[...] * (zg * _sigmoid(zg))).astype(BF16)

    def rb(s, i):
        return s * tps + i

    return pl.pallas_call(
        body, name="mixer_fwd", grid=(4, seqs, tps),
        in_specs=[pl.BlockSpec((5, tm, W), lambda g, s, i: (0, rb(s, i), g)),
                  pl.BlockSpec((None, W, W), lambda g, s, i: (g, 0, 0)),
                  pl.BlockSpec((None, 1, W), lambda g, s, i: (g, 0, 0)),
                  pl.BlockSpec((None, 2, W), lambda g, s, i: (g, 0, 0)),
                  pl.BlockSpec((None, 1, W), lambda g, s, i: (g, 0, 0))],
        out_specs=[pl.BlockSpec((2, tm, W), lambda g, s, i: (0, rb(s, i), g)),
                   pl.BlockSpec((tm, W), lambda g, s, i: (rb(s, i), g)),
                   pl.BlockSpec((nc, None, W, W), lambda g, s, i: (rb(s, i), g, 0, 0))],
        out_shape=[jax.ShapeDtypeStruct((2, T, 4 * W), BF16),
                   jax.ShapeDtypeStruct((T, 4 * W), F32),
                   jax.ShapeDtypeStruct((T // CHUNK, 4, W, W), F32)],
        scratch_shapes=[pltpu.VMEM((POOL_HALO, W), F32),
                        pltpu.VMEM((4, tm + 32, W), F32),
                        pltpu.VMEM((W, W), F32)],
        compiler_params=_cparams(("arbitrary", "arbitrary", "arbitrary")),
    )(u5, pool_w_bf, scale4, theta4, gn4)


def mixer_bwd(u5, dy2, o_pre, st_prev, pool_w_bf, scale4, theta4, gn4, *, seqs, seq_len, tm):
    T = u5.shape[1]
    tps = seq_len // tm
    nc = tm // CHUNK
    W = HEAD_W
    hb = tm // POOL_HALO

    def body(u_ref, uh_ref, dy_ref, o_ref, st_ref, pw_ref, sc_ref, th_ref, gn_ref,
             du_ref, dpw_ref, dsc_ref, dlb_ref, dgn_ref, nxt_ref, ext_ref, ds_ref):
        g = pl.program_id(0)
        s = pl.program_id(1)
        i = pl.program_id(2)
        tile = tps - 1 - i
        first = (s == 0) & (i == 0)

        @pl.when(i == 0)
        def _():
            nxt_ref[...] = jnp.zeros_like(nxt_ref)
            ds_ref[...] = jnp.zeros_like(ds_ref)

        row = lax.broadcasted_iota(jnp.int32, (tm, 1), 0)
        cnt = _pool_count(g, tile * tm + row)

        def accumulate(ref, val):
            @pl.when(first)
            def _():
                ref[...] = val

            @pl.when(jnp.logical_not(first))
            def _():
                ref[...] += val

        up = u_ref[0]
        ext_ref[0, 0:16] = jnp.zeros((16, W), F32)
        ext_ref[0, 16:32] = jnp.where(tile == 0, 0.0, uh_ref[...])
        ext_ref[0, 32:32 + tm] = up
        win = _select_window(g, *_pool_windows_back(ext_ref, tm))
        pb = (win / cnt - up).astype(BF16)
        dyp = dy_ref[0]
        z = _dot(pb, pw_ref[...])
        accumulate(dsc_ref, jnp.sum(dyp * z, axis=0, keepdims=True))
        dz = (dyp * sc_ref[...]).astype(BF16)
        accumulate(dpw_ref, _dot_tn(pb, dz))
        dp = _dot_nt(dz, pw_ref[...])
        e = dp / cnt
        ext_ref[0, 0:tm] = e
        ext_ref[0, tm:tm + 16] = nxt_ref[...]
        ext_ref[0, tm + 16:tm + 32] = jnp.zeros((16, W), F32)
        lead = _select_window(g, *_pool_windows_fwd(ext_ref, tm))
        nxt_ref[...] = e[0:POOL_HALO]
        du_ref[0] = (lead - dp).astype(BF16)

        zq, zf, zi, zg = u_ref[1], u_ref[2], u_ref[3], u_ref[4]
        lb, sig, f, sq = _hgrn_gates(zq, zf, th_ref[...])
        logf = jnp.log(f)
        kk = 1.0 - f
        q = zq * sq
        G = _tri_apply(_chunk_tri(tm, False), logf)

        dyh = dy_ref[1]
        o = o_ref[...]
        sg = _sigmoid(zg)
        r = lax.rsqrt(jnp.mean(o * o, axis=-1, keepdims=True) + EPS)
        orr = o * r
        gn = gn_ref[...]
        du_ref[4] = (dyh * (orr * gn) * (sg * (1.0 + zg * (1.0 - sg)))).astype(BF16)
        don = dyh * (zg * sg)
        accumulate(dgn_ref, jnp.sum(don * orr, axis=0, keepdims=True))
        dog = don * gn
        do = r * (dog - orr * jnp.mean(dog * orr, axis=-1, keepdims=True))

        causal = (lax.broadcasted_iota(jnp.int32, (CHUNK, CHUNK), 1)
                  <= lax.broadcasted_iota(jnp.int32, (CHUNK, CHUNK), 0))
        crow = lax.broadcasted_iota(jnp.int32, (CHUNK, 1), 0)
        dsn = ds_ref[...]
        dq_parts, dk_parts, dv_parts, dg_parts = [None] * nc, [None] * nc, [None] * nc, [None] * nc
        for c in reversed(range(nc)):
            sl = slice(c * CHUNK, (c + 1) * CHUNK)
            Gc, qc, kc = G[sl], q[sl], kk[sl]
            vb = zi[sl].astype(BF16)
            dob = do[sl].astype(BF16)
            Gm = Gc[CHUNK // 2 - 1:CHUNK // 2]
            Gl = Gc[CHUNK - 1:CHUNK]
            e_q, e_k, e_e, e_g = jnp.exp(Gc - Gm), jnp.exp(Gm - Gc), jnp.exp(Gl - Gc), jnp.exp(Gc)
            decay = jnp.exp(Gl)
            qr, kr, ke, qg = qc * e_q, kc * e_k, kc * e_e, qc * e_g
            qrb, krb, keb, qgb = qr.astype(BF16), kr.astype(BF16), ke.astype(BF16), qg.astype(BF16)
            st = st_ref[c]
            dsnb = dsn.astype(BF16)
            a = jnp.where(causal, _dot_nt(qrb, krb), 0.0).astype(BF16)
            da = jnp.where(causal, _dot_nt(dob, vb), 0.0).astype(BF16)
            dv_parts[c] = _dot_tn(a, dob) + _dot_nt(keb, dsnb)
            dqr = _dot(da, krb)
            dkr = _dot_tn(da, qrb)
            dqg = _dot(dob, st.astype(BF16))
            dke = _dot(vb, dsnb)
            ddecay = jnp.sum(dsn * st, axis=0, keepdims=True)
            dsn = _dot_tn(dob, qgb) + dsn * decay
            t_qr, t_kr, t_qg, t_ke = dqr * qr, dkr * kr, dqg * qg, dke * ke
            dq_parts[c] = dqr * e_q + dqg * e_g
            dk_parts[c] = dkr * e_k + dke * e_e
            dgm = jnp.sum(t_kr - t_qr, axis=0, keepdims=True)
            dgl = jnp.sum(t_ke, axis=0, keepdims=True) + ddecay * decay
            dg_parts[c] = (t_qr - t_kr + t_qg - t_ke
                           + jnp.where(crow == CHUNK // 2 - 1, dgm, 0.0)
                           + jnp.where(crow == CHUNK - 1, dgl, 0.0))
        ds_ref[...] = dsn
        dq = jnp.concatenate(dq_parts, axis=0)
        dk = jnp.concatenate(dk_parts, axis=0)
        dv = jnp.concatenate(dv_parts, axis=0)
        dG = jnp.concatenate(dg_parts, axis=0)
        dlogf = _tri_apply(_chunk_tri(tm, True), dG)
        df = dlogf / f - dk
        du_ref[1] = (dq * (sq * (1.0 + zq * (1.0 - sq)))).astype(BF16)
        du_ref[2] = (df * (1.0 - lb) * (sig * (1.0 - sig))).astype(BF16)
        du_ref[3] = dv.astype(BF16)
        accumulate(dlb_ref, jnp.sum(df * (1.0 - sig), axis=0, keepdims=True) * (lb * (1.0 - lb)))

    def rb(s, i):
        return s * tps + (tps - 1 - i)

    vec = pl.BlockSpec((None, 1, W), lambda g, s, i: (g, 0, 0))
    mat = pl.BlockSpec((None, W, W), lambda g, s, i: (g, 0, 0))
    return pl.pallas_call(
        body, name="mixer_bwd", grid=(4, seqs, tps),
        in_specs=[pl.BlockSpec((5, tm, W), lambda g, s, i: (0, rb(s, i), g)),
                  pl.BlockSpec((None, POOL_HALO, W), lambda g, s, i: (0, jnp.maximum(rb(s, i) * hb - 1, 0), g)),
                  pl.BlockSpec((2, tm, W), lambda g, s, i: (0, rb(s, i), g)),
                  pl.BlockSpec((tm, W), lambda g, s, i: (rb(s, i), g)),
                  pl.BlockSpec((nc, None, W, W), lambda g, s, i: (rb(s, i), g, 0, 0)),
                  mat, vec,
                  pl.BlockSpec((None, 2, W), lambda g, s, i: (g, 0, 0)),
                  vec],
        out_specs=[pl.BlockSpec((5, tm, W), lambda g, s, i: (0, rb(s, i), g)), mat, vec, vec, vec],
        out_shape=[jax.ShapeDtypeStruct((5, T, 4 * W), BF16),
                   jax.ShapeDtypeStruct((4, W, W), F32),
                   jax.ShapeDtypeStruct((4, 1, W), F32),
                   jax.ShapeDtypeStruct((4, 1, W), F32),
                   jax.ShapeDtypeStruct((4, 1, W), F32)],
        scratch_shapes=[pltpu.VMEM((POOL_HALO, W), F32),
                        pltpu.VMEM((4, tm + 32, W), F32),
                        pltpu.VMEM((W, W), F32)],
        compiler_params=_cparams(("arbitrary", "arbitrary", "arbitrary")),
    )(u5, u5, dy2, o_pre, st_prev, pool_w_bf, scale4, theta4, gn4)


def _attn_probs(q, k, hd):
    s = _dot_nt(q, k) * (1.0 / (hd ** 0.5))
    e = jnp.exp(s - jnp.max(s, axis=-1, keepdims=True))
    return e / jnp.sum(e, axis=-1, keepdims=True)


def attn_fwd(q, kv3, *, seqs, seq_len, n_mem, tm):
    T, D = q.shape
    hd = D // XATTN_HEADS
    tps = seq_len // tm

    def body(q_ref, kv_ref, o_ref):
        p = _attn_probs(q_ref[...], kv_ref[0], hd)
        o_ref[...] = _dot(p.astype(BF16), kv_ref[1]).astype(BF16)

    return pl.pallas_call(
        body, name="attn_fwd", grid=(seqs, XATTN_HEADS, tps),
        in_specs=[pl.BlockSpec((tm, hd), lambda b, h, i: (b * tps + i, h)),
                  pl.BlockSpec((2, n_mem, hd), lambda b, h, i: (0, b, h))],
        out_specs=pl.BlockSpec((tm, hd), lambda b, h, i: (b * tps + i, h)),
        out_shape=jax.ShapeDtypeStruct((T, D), BF16),
        compiler_params=_cparams(("parallel", "parallel", "arbitrary")),
    )(q, kv3)


def attn_bwd(q, kv3, do, *, seqs, seq_len, n_mem, tm):
    T, D = q.shape
    hd = D // XATTN_HEADS
    tps = seq_len // tm

    def body(q_ref, kv_ref, do_ref, dq_ref, dkv_ref):
        i = pl.program_id(2)
        qv, k, v, dov = q_ref[...], kv_ref[0], kv_ref[1], do_ref[...]
        p = _attn_probs(qv, k, hd)
        dp = _dot_nt(dov, v)
        ds = (p * (dp - jnp.sum(dp * p, axis=-1, keepdims=True)) * (1.0 / (hd ** 0.5))).astype(BF16)
        dq_ref[...] = _dot(ds, k).astype(BF16)
        dk = _dot_tn(ds, qv)
        dv = _dot_tn(p.astype(BF16), dov)

        @pl.when(i == 0)
        def _():
            dkv_ref[0] = dk
            dkv_ref[1] = dv

        @pl.when(i > 0)
        def _():
            dkv_ref[0] += dk
            dkv_ref[1] += dv

    qspec = pl.BlockSpec((tm, hd), lambda b, h, i: (b * tps + i, h))
    kvspec = pl.BlockSpec((2, n_mem, hd), lambda b, h, i: (0, b, h))
    return pl.pallas_call(
        body, name="attn_bwd", grid=(seqs, XATTN_HEADS, tps),
        in_specs=[qspec, kvspec, qspec],
        out_specs=[qspec, kvspec],
        out_shape=[jax.ShapeDtypeStruct((T, D), BF16), jax.ShapeDtypeStruct((2, seqs * n_mem, D), F32)],
        compiler_params=_cparams(("parallel", "parallel", "arbitrary")),
    )(q, kv3, do)


def final_loss(h, g, target, *, tm):
    T, D = h.shape

    def body(h_ref, g_ref, t_ref, dh_ref, dhb_ref, ls_ref, dg_ref):
        i = pl.program_id(0)
        x = h_ref[...]
        gv = g_ref[...]
        r = lax.rsqrt(jnp.mean(x * x, axis=-1, keepdims=True) + EPS)
        xr = x * r
        d = xr * gv - t_ref[...]
        dy = d * (1.0 / D)
        dyg = dy * gv
        dx = r * (dyg - xr * jnp.mean(dyg * xr, axis=-1, keepdims=True))
        dh_ref[...] = dx
        dhb_ref[...] = dx.astype(BF16)
        ls = jnp.sum(d * d, axis=0, keepdims=True)
        dg = jnp.sum(dy * xr, axis=0, keepdims=True)

        @pl.when(i == 0)
        def _():
            ls_ref[...] = ls
            dg_ref[...] = dg

        @pl.when(i > 0)
        def _():
            ls_ref[...] += ls
            dg_ref[...] += dg

    row = pl.BlockSpec((tm, D), lambda i: (i, 0))
    vec = pl.BlockSpec((1, D), lambda i: (0, 0))
    return pl.pallas_call(
        body, name="final_loss", grid=(T // tm,),
        in_specs=[row, vec, row], out_specs=[row, row, vec, vec],
        out_shape=[jax.ShapeDtypeStruct((T, D), F32), jax.ShapeDtypeStruct((T, D), BF16),
                   jax.ShapeDtypeStruct((1, D), F32), jax.ShapeDtypeStruct((1, D), F32)],
        compiler_params=_cparams(("arbitrary",)),
    )(h, g, target)


def _my_place():
    return lax.axis_index("x"), lax.axis_index("y"), lax.axis_index("c")


def _slot_of(px, py, pc):
    return 4 * px + 2 * py + pc


def _peer(k, x, y, c):
    return (1 - x if (k >> 2) & 1 else x, 1 - y if (k >> 1) & 1 else y, 1 - c if k & 1 else c)


def _split_copies(src_refs, land_refs, send_sems, recv_sems, scatter):
    x, y, c = _my_place()
    mine = _slot_of(x, y, c)
    copies = []
    for a, (src, land) in enumerate(zip(src_refs, land_refs)):
        for k in range(1, N_DEV):
            peer = _peer(k, x, y, c)
            copies.append(pltpu.make_async_remote_copy(
                src_ref=src.at[_slot_of(*peer)] if scatter else src, dst_ref=land.at[mine],
                send_sem=send_sems.at[a * N_PEERS + k - 1], recv_sem=recv_sems.at[a * N_PEERS + k - 1],
                device_id=peer, device_id_type=MESH))
    return copies


def split_start(groups, *, name, scatter):
    sizes = [len(srcs) for srcs, _ in groups]
    n_arr = sum(sizes)
    flat = [a for srcs, lands in groups for a in list(srcs) + list(lands)]

    def body(*refs):
        ins = refs[:2 * n_arr]
        sems = refs[4 * n_arr:4 * n_arr + 2 * len(groups)]
        token = refs[-1]
        at = 0
        for gi, n in enumerate(sizes):
            for cp in _split_copies(ins[at:at + n], ins[at + n:at + 2 * n], sems[2 * gi], sems[2 * gi + 1], scatter):
                cp.start()
            at += 2 * n
        token[...] = jnp.zeros_like(token)

    sem_shapes = []
    for n in sizes:
        sem_shapes += [pltpu.SemaphoreType.DMA((n * N_PEERS,))] * 2
    outs = pl.pallas_call(
        body, name=name,
        out_shape=tuple(pltpu.HBM(a.shape, a.dtype) for a in flat) + tuple(sem_shapes)
        + (jax.ShapeDtypeStruct((8, 128), F32),),
        in_specs=(HBM,) * len(flat),
        out_specs=(HBM,) * len(flat) + (SEM,) * len(sem_shapes) + (pl.BlockSpec(memory_space=pltpu.VMEM),),
        input_output_aliases={i: i for i in range(len(flat))},
        compiler_params=pltpu.CompilerParams(has_side_effects=pltpu.SideEffectType.DATAFLOW_SIDE_EFFECTING),
    )(*[pltpu.with_memory_space_constraint(a, pltpu.HBM) for a in flat])
    thru, sems, token = outs[:len(flat)], outs[len(flat):-1], outs[-1]
    started, at = [], 0
    for gi, n in enumerate(sizes):
        started.append((sems[2 * gi], sems[2 * gi + 1], thru[at:at + n], thru[at + n:at + 2 * n]))
        at += 2 * n
    return started, token


def split_wait(started, after, *, name, scatter):
    sizes = [len(g[2]) for g in started]
    n_arr = sum(sizes)
    flat = [a for g in started for a in list(g[2]) + list(g[3])]
    sems = [s for g in started for s in g[:2]]

    def body(*refs):
        ins = refs[:2 * n_arr]
        sem_refs = refs[2 * n_arr:2 * n_arr + len(sems)]
        at = 0
        for gi, n in enumerate(sizes):
            for cp in _split_copies(ins[at:at + n], ins[at + n:at + 2 * n], sem_refs[2 * gi], sem_refs[2 * gi + 1], scatter):
                cp.wait_send()
                cp.wait_recv()
            at += 2 * n

    outs = pl.pallas_call(
        body, name=name,
        out_shape=tuple(pltpu.HBM(a.shape, a.dtype) for a in flat),
        in_specs=(HBM,) * len(flat) + (SEM,) * len(sems) + (pl.BlockSpec(memory_space=pl.ANY),),
        out_specs=(HBM,) * len(flat),
        input_output_aliases={i: i for i in range(len(flat))},
        compiler_params=pltpu.CompilerParams(has_side_effects=pltpu.SideEffectType.DATAFLOW_SIDE_EFFECTING),
    )(*flat, *sems, after)
    lands, at = [], 0
    for n in sizes:
        lands.append(outs[at + n:at + 2 * n])
        at += 2 * n
    return lands


def allgather_small(bufs):
    n = len(bufs)

    def body(*refs):
        srcs, outs = refs[:n], refs[n:2 * n]
        send_sems, recv_sems, local_sems = refs[2 * n:]
        x, y, c = _my_place()
        mine = _slot_of(x, y, c)
        local = [pltpu.make_async_copy(s, o.at[mine], local_sems.at[a]) for a, (s, o) in enumerate(zip(srcs, outs))]
        for cp in local:
            cp.start()
        copies = _split_copies(srcs, outs, send_sems, recv_sems, False)
        for cp in copies:
            cp.start()
        for cp in copies:
            cp.wait()
        for cp in local:
            cp.wait()

    return pl.pallas_call(
        body, name="allgather_small",
        out_shape=[jax.ShapeDtypeStruct((N_DEV,) + b.shape, b.dtype) for b in bufs],
        in_specs=[HBM] * n, out_specs=[HBM] * n,
        scratch_shapes=[pltpu.SemaphoreType.DMA((n * N_PEERS,)), pltpu.SemaphoreType.DMA((n * N_PEERS,)),
                        pltpu.SemaphoreType.DMA((n,))],
    )(*bufs)


def _adamw_math(g, w, m, v):
    c1 = 1.0 - ADAM_B1 ** ADAM_STEP
    c2 = 1.0 - ADAM_B2 ** ADAM_STEP
    nm = ADAM_B1 * m + (1.0 - ADAM_B1) * g
    nv = ADAM_B2 * v + (1.0 - ADAM_B2) * (g * g)
    delta = -ADAM_LR * ((nm / c1) / (jnp.sqrt(nv / c2) + ADAM_EPS) + ADAM_WD * w)
    return delta, nm, nv


def adamw_sharded(me, own, recv, w, m, v, *, name, tr):
    R, C = w.shape

    def body(me_ref, *refs):
        parts = refs[:N_DEV]
        w_ref, m_ref, v_ref, g_ref, d_ref, nm_ref, nv_ref = refs[N_DEV:]
        g = parts[0][...].astype(F32)
        for p in parts[1:]:
            g = g + p[...].astype(F32)
        g_ref[...] = g
        d_ref[...], nm_ref[...], nv_ref[...] = _adamw_math(g, w_ref[...], m_ref[...], v_ref[...])

    def slab(k):
        return pl.BlockSpec((None, tr, C), lambda i, me_ref: (me_ref[0] ^ k, i, 0))

    blk = pl.BlockSpec((tr, C), lambda i, me_ref: (i, 0))
    out = jax.ShapeDtypeStruct((R, C), F32)
    return pl.pallas_call(
        body, name=name,
        grid_spec=pltpu.PrefetchScalarGridSpec(
            num_scalar_prefetch=1, grid=(R // tr,),
            in_specs=[slab(k) for k in range(N_DEV)] + [blk, blk, blk],
            out_specs=[blk, blk, blk, blk]),
        out_shape=[out, out, out, out],
        compiler_params=_cparams(("parallel",)),
    )(me, own, *([recv] * N_PEERS), w, m, v)


def adamw_replicated(parts, ws, ms, vs, rows):
    n_buf, n_par = len(parts), len(ws)

    def body(*refs):
        p_refs = refs[:n_buf]
        w_refs = refs[n_buf:n_buf + n_par]
        m_refs = refs[n_buf + n_par:n_buf + 2 * n_par]
        v_refs = refs[n_buf + 2 * n_par:n_buf + 3 * n_par]
        outs = refs[n_buf + 3 * n_par:]
        sums = []
        for p in p_refs:
            g = p[0]
            for s in range(1, N_DEV):
                g = g + p[s]
            sums.append(g)
        for j, (b, r0, nr) in enumerate(rows):
            g = sums[b][r0:r0 + nr]
            delta, nm, nv = _adamw_math(g, w_refs[j][...], m_refs[j][...], v_refs[j][...])
            outs[j][...] = g
            outs[n_par + j][...] = delta
            outs[2 * n_par + j][...] = nm
            outs[3 * n_par + j][...] = nv

    shapes = [jax.ShapeDtypeStruct(w.shape, F32) for w in ws]
    outs = pl.pallas_call(
        body, name="adamw_replicated", out_shape=shapes * 4,
        compiler_params=pltpu.CompilerParams(vmem_limit_bytes=V7X_VMEM_LIMIT),
    )(*parts, *ws, *ms, *vs)
    return outs[:n_par], outs[n_par:2 * n_par], outs[2 * n_par:3 * n_par], outs[3 * n_par:]


BIG = ("w_in", "w_out", "xw_q", "xw_kv", "xw_o", "w_up", "w_down")
COL_SHARDED = ("w_in", "xw_kv", "w_up")
WEIGHTS = ("norm_mix", "w_in", "pool_w", "pool_scale", "lb_theta", "hgrn_norm", "w_out", "norm_xq",
           "norm_mem", "xw_q", "xw_kv", "xw_o", "norm_mlp", "w_up", "w_down", "norm_final")
SMALL = (("pool_w", (4 * HEAD_W, HEAD_W), 0, 0),
         ("norm_mix", (1, 1024), 1, 0), ("norm_xq", (1, 1024), 1, 1), ("norm_mem", (1, 1024), 1, 2),
         ("norm_mlp", (1, 1024), 1, 3), ("norm_final", (1, 1024), 1, 4),
         ("pool_scale", (1, 512), 2, 0), ("hgrn_norm", (1, 512), 2, 1), ("lb_theta", (2, 512), 2, 2))


def _pad_rows(a, rows):
    return jnp.concatenate([a, jnp.zeros((rows - a.shape[0], a.shape[1]), a.dtype)], axis=0)


def kernel(x, mem, norm_mix, w_in, pool_w, pool_scale, lb_theta, hgrn_norm, w_out, norm_xq, norm_mem, xw_q, xw_kv, xw_o, norm_mlp, w_up, w_down, norm_final, loss_target, m_norm_mix, m_w_in, m_pool_w, m_pool_scale, m_lb_theta, m_hgrn_norm, m_w_out, m_norm_xq, m_norm_mem, m_xw_q, m_xw_kv, m_xw_o, m_norm_mlp, m_w_up, m_w_down, m_norm_final, v_norm_mix, v_w_in, v_pool_w, v_pool_scale, v_lb_theta, v_hgrn_norm, v_w_out, v_norm_xq, v_norm_mem, v_xw_q, v_xw_kv, v_xw_o, v_norm_mlp, v_w_up, v_w_down, v_norm_final):
    w = dict(norm_mix=norm_mix, w_in=w_in, pool_w=pool_w, pool_scale=pool_scale, lb_theta=lb_theta,
             hgrn_norm=hgrn_norm, w_out=w_out, norm_xq=norm_xq, norm_mem=norm_mem, xw_q=xw_q, xw_kv=xw_kv,
             xw_o=xw_o, norm_mlp=norm_mlp, w_up=w_up, w_down=w_down, norm_final=norm_final)
    mom = dict(norm_mix=m_norm_mix, w_in=m_w_in, pool_w=m_pool_w, pool_scale=m_pool_scale, lb_theta=m_lb_theta,
               hgrn_norm=m_hgrn_norm, w_out=m_w_out, norm_xq=m_norm_xq, norm_mem=m_norm_mem, xw_q=m_xw_q,
               xw_kv=m_xw_kv, xw_o=m_xw_o, norm_mlp=m_norm_mlp, w_up=m_w_up, w_down=m_w_down,
               norm_final=m_norm_final)
    var = dict(norm_mix=v_norm_mix, w_in=v_w_in, pool_w=v_pool_w, pool_scale=v_pool_scale, lb_theta=v_lb_theta,
               hgrn_norm=v_hgrn_norm, w_out=v_w_out, norm_xq=v_norm_xq, norm_mem=v_norm_mem, xw_q=v_xw_q,
               xw_kv=v_xw_kv, xw_o=v_xw_o, norm_mlp=v_norm_mlp, w_up=v_w_up, w_down=v_w_down,
               norm_final=v_norm_final)

    seqs, seq_len, D = x.shape
    n_mem = mem.shape[1]
    T = seqs * seq_len
    W = HEAD_W
    x2 = x.reshape(T, D)
    mem2 = mem.reshape(seqs * n_mem, D)
    tgt2 = loss_target.reshape(T, D)
    tm_big = min(1024, T)
    tm_mid = min(512, T)
    tm_mix = min(256, seq_len)
    tm_att = min(1024, seq_len)
    tkv = min(512, seqs * n_mem)
    px, py, pc = _my_place()
    me = _slot_of(px, py, pc).astype(jnp.int32)
    me1 = me.reshape(1)

    shard_bf = {n: w[n][0].astype(BF16) for n in BIG}

    def landing(n):
        zone = lax.empty((N_DEV,) + shard_bf[n].shape, BF16)
        return lax.dynamic_update_slice(zone, shard_bf[n][None], (me, 0, 0))

    ag_groups = (("w_in",), ("w_out", "xw_q", "xw_kv", "xw_o"), ("w_up", "w_down"))
    ag_started, _ = split_start([([shard_bf[n] for n in grp], [landing(n) for n in grp]) for grp in ag_groups],
                                name="weights_gather_start", scatter=False)

    pool_w_bf = pool_w[0].astype(BF16)
    scale4 = pool_scale.reshape(4, 1, W)
    gn4 = hgrn_norm.reshape(4, 1, W)
    theta4 = lb_theta.reshape(2, 4, W).transpose(1, 0, 2)
    g_final = norm_final.reshape(1, D)

    (wi3,), = split_wait(ag_started[:1], x2, name="weights_gather_wait_in", scatter=False)
    full_w_in = wi3.transpose(1, 0, 2).reshape(D, -1)
    u5, n1 = norm_mm(x2, norm_mix, full_w_in, name="in_proj", tm=tm_big, tn=4 * W, out_dtype=F32, out_slabs=5)
    y2, o_pre, st_prev = mixer_fwd(u5, pool_w_bf, scale4, theta4, gn4, seqs=seqs, seq_len=seq_len, tm=tm_mix)
    (wo3, wq3, wkv3, wao3), = split_wait(ag_started[1:2], y2, name="weights_gather_wait_attn", scatter=False)
    full_w_out, full_xw_q, full_xw_o = wo3.reshape(D, D), wq3.reshape(D, D), wao3.reshape(D, D)
    h1 = mm_nn(y2, full_w_out, x2, name="out_proj", tm=tm_big, tn=D, tk=4 * W)
    q, n2 = norm_mm(h1, norm_xq, full_xw_q, name="q_proj", tm=tm_big, tn=D, out_dtype=BF16)
    kv3, memn = norm_mm(mem2, norm_mem, wkv3, name="kv_proj", tm=tkv, tn=wkv3.shape[2], out_dtype=BF16, out_slabs=2)
    o_att = attn_fwd(q, kv3, seqs=seqs, seq_len=seq_len, n_mem=n_mem, tm=tm_att)
    h2 = mm_nn(o_att, full_xw_o, h1, name="attn_out_proj", tm=tm_big, tn=D, tk=D)
    (wup3, wdn3), = split_wait(ag_started[2:3], h2, name="weights_gather_wait_mlp", scatter=False)
    full_w_down = wdn3.reshape(-1, D)
    tn_up = wup3.shape[2]
    a_pre, n3 = norm_mm(h2, norm_mlp, wup3, name="up_proj", tm=tm_big, tn=tn_up, out_dtype=F32)
    h3 = mm_nn(a_pre, full_w_down, h2, name="down_proj", tm=tm_big, tn=D, tk=D, relu2=True)
    dh3, dh3b, sq_err, dg_final = final_loss(h3, g_final, tgt2, tm=tm_mid)
    loss = lax.psum(0.5 * jnp.sum(sq_err) / D, ("x", "y", "c"))

    def send(parts, name):
        srcs = [p.reshape((N_DEV, -1, p.shape[-1])) for p in parts]
        lands = [lax.empty(s.shape, BF16) for s in srcs]
        started, token = split_start([(srcs, lands)], name=name, scatter=True)
        return started[0], token

    gw_down = mm_tn(a_pre, dh3b, name="down_proj_wgrad", tt=tm_mid, tko=D, tn=D, relu2=True)
    sent_down, tok = send([gw_down], "grads_send_down")
    dap = mm_nt(dh3b, full_w_down, name="down_proj_bwd", tm=tm_big, tn=D, tk=D, out_dtype=BF16, relu2_of=a_pre,
                after=tok)
    gw_up = mm_tn(n3, dap, name="up_proj_wgrad", tt=tm_mid, tko=D, tn=tn_up, out_slabs=N_DEV)
    sent_up, tok = send([gw_up], "grads_send_up")
    dh2, dh2b, dg_mlp = mm_nt_normbwd(dap, wup3, h2, norm_mlp, dh3, name="up_proj_bwd", tm=tm_mid, tk=tn_up,
                                      after=tok)
    do_att = mm_nt(dh2b, full_xw_o, name="attn_out_proj_bwd", tm=tm_big, tn=D, tk=D, out_dtype=BF16)
    gxw_o = mm_tn(o_att, dh2b, name="attn_out_proj_wgrad", tt=tm_mid, tko=D, tn=D)
    dq, dkv3 = attn_bwd(q, kv3, do_att, seqs=seqs, seq_len=seq_len, n_mem=n_mem, tm=tm_att)
    gxw_q = mm_tn(n2, dq, name="q_proj_wgrad", tt=tm_mid, tko=D, tn=D)
    gxw_kv = mm_tn(memn, dkv3, name="kv_proj_wgrad", tt=tkv, tko=D, tn=wkv3.shape[2], out_slabs=N_DEV)
    sent_attn, tok = send([gxw_o, gxw_q, gxw_kv], "grads_send_attn")
    dg_mem = mm_nt_normbwd(dkv3, wkv3, mem2, norm_mem, None, name="kv_proj_bwd", tm=tkv, tk=wkv3.shape[2])
    dh1, dh1b, dg_xq = mm_nt_normbwd(dq, full_xw_q, h1, norm_xq, dh2, name="q_proj_bwd", tm=tm_mid, tk=D, after=tok)
    gw_out = mm_tn(y2, dh1b, name="out_proj_wgrad", tt=tm_mid, tko=4 * W, tn=D)
    sent_out, tok = send([gw_out], "grads_send_out")
    dy2 = mm_nt(dh1b, full_w_out, name="out_proj_bwd", tm=tm_big, tn=4 * W, tk=D, out_dtype=F32, out_slabs=2,
                after=tok)
    du5, dpw, dsc, dlb, dgn = mixer_bwd(u5, dy2, o_pre, st_prev, pool_w_bf, scale4, theta4, gn4,
                                        seqs=seqs, seq_len=seq_len, tm=tm_mix)
    gw_in = mm_tn(n1, du5, name="in_proj_wgrad", tt=tm_mid, tko=D, tn=4 * W)
    gw_in_slots = gw_in.reshape(D, N_DEV, -1).transpose(1, 0, 2)
    sent_in, tok = send([gw_in_slots], "grads_send_in")
    dx, _, dg_mix = mm_nt_normbwd(du5, full_w_in, x2, norm_mix, dh1, name="in_proj_bwd", tm=tm_mid, tk=4 * W,
                                  after=tok)

    dlb_row = dlb.reshape(1, 4 * W)
    buf_vec = _pad_rows(jnp.concatenate([dg_mix, dg_xq, dg_mem, dg_mlp, dg_final], axis=0), 8)
    buf_half = _pad_rows(jnp.concatenate([dsc.reshape(1, 4 * W), dgn.reshape(1, 4 * W), dlb_row, -dlb_row], axis=0), 8)
    small_parts = allgather_small([dpw.reshape(4 * W, W), buf_vec, buf_half])

    sent = [sent_down, sent_up, sent_attn, sent_out, sent_in]
    recv = split_wait(sent, small_parts[1], name="grads_wait", scatter=True)
    own = dict(w_down=sent_down[2][0], w_up=sent_up[2][0], xw_o=sent_attn[2][0], xw_q=sent_attn[2][1],
               xw_kv=sent_attn[2][2], w_out=sent_out[2][0], w_in=sent_in[2][0])
    got = dict(w_down=recv[0][0], w_up=recv[1][0], xw_o=recv[2][0], xw_q=recv[2][1], xw_kv=recv[2][2],
               w_out=recv[3][0], w_in=recv[4][0])
    res = {}
    for n in BIG:
        shp = w[n].shape
        r = adamw_sharded(me1, own[n], got[n], w[n][0], mom[n][0], var[n][0], name="adamw_" + n,
                          tr=min(256, shp[1]))
        for kind, a in zip("gdmv", r):
            res[kind, n] = a.reshape(shp)
    r = adamw_replicated(small_parts, [w[n].reshape(v2) for n, v2, _, _ in SMALL],
                         [mom[n].reshape(v2) for n, v2, _, _ in SMALL],
                         [var[n].reshape(v2) for n, v2, _, _ in SMALL],
                         [(b, r0, v2[0]) for _, v2, b, r0 in SMALL])
    for kind, arrs in zip("gdmv", r):
        for (n, _, _, _), a in zip(SMALL, arrs):
            res[kind, n] = a.reshape(w[n].shape)

    out = [loss, dx.reshape(x.shape)]
    for kind in "gdmv":
        out += [res[kind, n] for n in WEIGHTS]
    return tuple(out)
```

```python
import jax
import jax.numpy as jnp
from jax import lax
from jax.experimental import pallas as pl
from jax.experimental.pallas import tpu as pltpu

F32 = jnp.float32
BF16 = jnp.bfloat16
EPS = 1e-6
CHUNK = 64
POOL_HALO = 16
HEAD_W = 128
XATTN_HEADS = 4
N_DEV = 8
N_PEERS = N_DEV - 1
ADAM_LR = 0.001
ADAM_B1 = 0.9
ADAM_B2 = 0.999
ADAM_EPS = 1e-08
ADAM_WD = 0.01
ADAM_STEP = 10
V7X_VMEM_LIMIT = 52 * 1024 * 1024
MESH = pl.DeviceIdType.MESH
HBM = pl.BlockSpec(memory_space=pltpu.HBM)
SEM = pl.BlockSpec(memory_space=pltpu.SEMAPHORE)


def _cparams(dims):
    return pltpu.CompilerParams(dimension_semantics=dims, vmem_limit_bytes=V7X_VMEM_LIMIT)


def _sigmoid(v):
    return 1.0 / (1.0 + jnp.exp(-v))


def _dot(a, b):
    return jnp.dot(a, b, preferred_element_type=F32)


def _dot_nt(a, b):
    return lax.dot_general(a, b, (((1,), (1,)), ((), ())), preferred_element_type=F32)


def _dot_tn(a, b):
    return lax.dot_general(a, b, (((0,), (0,)), ((), ())), preferred_element_type=F32)


def _split3(v):
    hi = v.astype(BF16)
    r1 = v - hi.astype(F32)
    mid = r1.astype(BF16)
    lo = (r1 - mid.astype(F32)).astype(BF16)
    return hi, mid, lo


def _tri_apply(tri, v):
    hi, mid, lo = _split3(v)
    return _dot(tri, hi) + _dot(tri, mid) + _dot(tri, lo)


def _mat_shape(a):
    return a.shape if a.ndim == 2 else (a.shape[1], a.shape[0] * a.shape[2])


def _tile_spec(a, rows, cols, row_of, col_of):
    if a.ndim == 2:
        return pl.BlockSpec((rows, cols), lambda *g: (row_of(*g), col_of(*g)))
    per = a.shape[2] // cols
    return pl.BlockSpec((None, rows, cols), lambda *g: (col_of(*g) // per, row_of(*g), col_of(*g) % per))


def _out_struct(rows, n, slabs, dtype):
    return jax.ShapeDtypeStruct((rows, n) if slabs is None else (slabs, rows, n // slabs), dtype)


def norm_mm(h, g, w, *, name, tm, tn, out_dtype, out_slabs=None):
    T, D = h.shape
    N = _mat_shape(w)[1]
    o_shape = _out_struct(T, N, out_slabs, out_dtype)

    def body(h_ref, g_ref, w_ref, o_ref, n_ref):
        @pl.when(pl.program_id(1) == 0)
        def _():
            x = h_ref[...]
            r = lax.rsqrt(jnp.mean(x * x, axis=-1, keepdims=True) + EPS)
            n_ref[...] = (x * r * g_ref[...]).astype(BF16)

        o_ref[...] = _dot(n_ref[...], w_ref[...]).astype(o_ref.dtype)

    return pl.pallas_call(
        body, name=name, grid=(T // tm, N // tn),
        in_specs=[pl.BlockSpec((tm, D), lambda i, j: (i, 0)),
                  pl.BlockSpec((1, D), lambda i, j: (0, 0)),
                  _tile_spec(w, D, tn, lambda i, j: 0, lambda i, j: j)],
        out_specs=[_tile_spec(o_shape, tm, tn, lambda i, j: i, lambda i, j: j),
                   pl.BlockSpec((tm, D), lambda i, j: (i, 0))],
        out_shape=[o_shape, jax.ShapeDtypeStruct((T, D), BF16)],
        compiler_params=_cparams(("parallel", "arbitrary")),
    )(h, g, w)


def mm_nn(a, w, res, *, name, tm, tn, tk, relu2=False):
    T, K = _mat_shape(a)
    N = w.shape[1]
    nk = K // tk

    def body(a_ref, w_ref, r_ref, o_ref, acc_ref):
        k = pl.program_id(2)
        av = a_ref[...]
        if relu2:
            av = jnp.maximum(av, 0.0)
            av = av * av
        part = _dot(av.astype(BF16), w_ref[...])

        @pl.when(k == 0)
        def _():
            acc_ref[...] = part

        @pl.when(k > 0)
        def _():
            acc_ref[...] += part

        @pl.when(k == nk - 1)
        def _():
            o_ref[...] = r_ref[...] + acc_ref[...]

    return pl.pallas_call(
        body, name=name, grid=(T // tm, N // tn, nk),
        in_specs=[_tile_spec(a, tm, tk, lambda i, j, k: i, lambda i, j, k: k),
                  pl.BlockSpec((tk, tn), lambda i, j, k: (k, j)),
                  pl.BlockSpec((tm, tn), lambda i, j, k: (i, j))],
        out_specs=pl.BlockSpec((tm, tn), lambda i, j, k: (i, j)),
        out_shape=jax.ShapeDtypeStruct((T, N), F32),
        scratch_shapes=[pltpu.VMEM((tm, tn), F32)],
        compiler_params=_cparams(("parallel", "parallel", "arbitrary")),
    )(a, w, res)


def mm_nt(a, w, *, name, tm, tn, tk, out_dtype, out_slabs=None, relu2_of=None, after=None):
    T, K = _mat_shape(a)
    nk = K // tk
    N = w.shape[0]
    has_z = relu2_of is not None
    o_shape = _out_struct(T, N, out_slabs, out_dtype)

    def body(*refs):
        a_ref, w_ref = refs[0], refs[1]
        z_ref = refs[2] if has_z else None
        o_ref, acc_ref = refs[-2], refs[-1]
        k = pl.program_id(2)
        part = _dot_nt(a_ref[...].astype(BF16), w_ref[...])

        @pl.when(k == 0)
        def _():
            acc_ref[...] = part

        @pl.when(k > 0)
        def _():
            acc_ref[...] += part

        @pl.when(k == nk - 1)
        def _():
            out = acc_ref[...]
            if has_z:
                out = out * (2.0 * jnp.maximum(z_ref[...], 0.0))
            o_ref[...] = out.astype(o_ref.dtype)

    in_specs = [_tile_spec(a, tm, tk, lambda i, j, k: i, lambda i, j, k: k),
                pl.BlockSpec((tn, tk), lambda i, j, k: (j, k))]
    args = [a, w]
    if has_z:
        in_specs.append(pl.BlockSpec((tm, tn), lambda i, j, k: (i, j)))
        args.append(relu2_of)
    if after is not None:
        in_specs.append(pl.BlockSpec(after.shape, lambda i, j, k: (0, 0)))
        args.append(after)
    return pl.pallas_call(
        body, name=name, grid=(T // tm, N // tn, nk),
        in_specs=in_specs,
        out_specs=_tile_spec(o_shape, tm, tn, lambda i, j, k: i, lambda i, j, k: j),
        out_shape=o_shape,
        scratch_shapes=[pltpu.VMEM((tm, tn), F32)],
        compiler_params=_cparams(("parallel", "parallel", "arbitrary")),
    )(*args)


def mm_nt_normbwd(a, w, h, g, dres, *, name, tm, tk, after=None):
    T, K = _mat_shape(a)
    nk = K // tk
    D = h.shape[1]
    with_dh = dres is not None

    def body(*refs):
        a_ref, w_ref, h_ref, g_ref = refs[:4]
        if with_dh:
            r_ref = refs[4]
            dh_ref, dhb_ref, dg_ref, acc_ref = refs[-4:]
        else:
            dg_ref, acc_ref = refs[-2:]
        i = pl.program_id(0)
        k = pl.program_id(1)
        part = _dot_nt(a_ref[...].astype(BF16), w_ref[...])

        @pl.when(k == 0)
        def _():
            acc_ref[...] = part

        @pl.when(k > 0)
        def _():
            acc_ref[...] += part

        @pl.when(k == nk - 1)
        def _():
            dn = acc_ref[...]
            x = h_ref[...]
            r = lax.rsqrt(jnp.mean(x * x, axis=-1, keepdims=True) + EPS)
            xr = x * r
            dgp = jnp.sum(dn * xr, axis=0, keepdims=True)

            @pl.when(i == 0)
            def _():
                dg_ref[...] = dgp

            @pl.when(i > 0)
            def _():
                dg_ref[...] += dgp

            if with_dh:
                dyg = dn * g_ref[...]
                dx = r * (dyg - xr * jnp.mean(dyg * xr, axis=-1, keepdims=True))
                out = r_ref[...] + dx
                dh_ref[...] = out
                dhb_ref[...] = out.astype(BF16)

    row = pl.BlockSpec((tm, D), lambda i, k: (i, 0))
    vec = pl.BlockSpec((1, D), lambda i, k: (0, 0))
    in_specs = [_tile_spec(a, tm, tk, lambda i, k: i, lambda i, k: k),
                _tile_spec(w, D, tk, lambda i, k: 0, lambda i, k: k), row, vec]
    args = [a, w, h, g]
    if with_dh:
        in_specs.append(row)
        args.append(dres)
        out_specs = [row, row, vec]
        out_shape = [jax.ShapeDtypeStruct((T, D), F32), jax.ShapeDtypeStruct((T, D), BF16),
                     jax.ShapeDtypeStruct((1, D), F32)]
    else:
        out_specs = vec
        out_shape = jax.ShapeDtypeStruct((1, D), F32)
    if after is not None:
        in_specs.append(pl.BlockSpec(after.shape, lambda i, k: (0, 0)))
        args.append(after)
    return pl.pallas_call(
        body, name=name, grid=(T // tm, nk),
        in_specs=in_specs, out_specs=out_specs, out_shape=out_shape,
        scratch_shapes=[pltpu.VMEM((tm, D), F32)],
        compiler_params=_cparams(("arbitrary", "arbitrary")),
    )(*args)


def mm_tn(a, b, *, name, tt, tko, tn, relu2=False, out_slabs=None):
    T, K = _mat_shape(a)
    N = _mat_shape(b)[1]
    nt = T // tt
    o_shape = _out_struct(K, N, out_slabs, BF16)

    def body(a_ref, b_ref, o_ref, acc_ref):
        t = pl.program_id(2)
        av = a_ref[...]
        if relu2:
            av = jnp.maximum(av, 0.0)
            av = av * av
        part = _dot_tn(av.astype(BF16), b_ref[...].astype(BF16))

        @pl.when(t == 0)
        def _():
            acc_ref[...] = part

        @pl.when(t > 0)
        def _():
            acc_ref[...] += part

        @pl.when(t == nt - 1)
        def _():
            o_ref[...] = acc_ref[...].astype(BF16)

    return pl.pallas_call(
        body, name=name, grid=(K // tko, N // tn, nt),
        in_specs=[_tile_spec(a, tt, tko, lambda kk, j, t: t, lambda kk, j, t: kk),
                  _tile_spec(b, tt, tn, lambda kk, j, t: t, lambda kk, j, t: j)],
        out_specs=_tile_spec(o_shape, tko, tn, lambda kk, j, t: kk, lambda kk, j, t: j),
        out_shape=o_shape,
        scratch_shapes=[pltpu.VMEM((tko, tn), F32)],
        compiler_params=_cparams(("parallel", "parallel", "arbitrary")),
    )(a, b)


def _resident(a):
    nd = a.ndim
    return pl.BlockSpec(a.shape, lambda i: (0,) * nd, pipeline_mode=pl.Buffered(1))


def _row_block(a, tm):
    if a.ndim == 2:
        return pl.BlockSpec((tm, a.shape[1]), lambda i: (i, 0))
    return pl.BlockSpec((a.shape[0], tm, a.shape[2]), lambda i: (0, i, 0))


def _cols(ref, c, width):
    if len(ref.shape) == 2:
        return ref[:, c * width:(c + 1) * width]
    per = ref.shape[2] // width
    if per == 1:
        return ref[c]
    return ref[c // per, :, (c % per) * width:(c % per + 1) * width]


def _set_cols(ref, c, width, val):
    if len(ref.shape) == 2:
        ref[:, c * width:(c + 1) * width] = val
        return
    per = ref.shape[2] // width
    if per == 1:
        ref[c] = val
    else:
        ref[c // per, :, (c % per) * width:(c % per + 1) * width] = val


def _all_cols(ref):
    if len(ref.shape) == 2:
        return ref[...]
    return jnp.concatenate([ref[s] for s in range(ref.shape[0])], axis=1)


def _rms(x):
    return lax.rsqrt(jnp.mean(x * x, axis=-1, keepdims=True) + EPS)


def _row_params():
    return _cparams(("arbitrary",))


def proj_norm(h, g, w, *, name, tm, tn, out_dtype, out_slabs=None):
    T, D = h.shape
    N = _mat_shape(w)[1]
    o_shape = _out_struct(T, N, out_slabs, out_dtype)

    def body(h_ref, g_ref, w_ref, o_ref, n_ref):
        x = h_ref[...]
        n = (x * _rms(x) * g_ref[...]).astype(BF16)
        n_ref[...] = n
        for c in range(N // tn):
            _set_cols(o_ref, c, tn, _dot(n, _cols(w_ref, c, tn)).astype(out_dtype))

    return pl.pallas_call(
        body, name=name, grid=(T // tm,),
        in_specs=[_row_block(h, tm), pl.BlockSpec((1, D), lambda i: (0, 0)), _resident(w)],
        out_specs=[_row_block(o_shape, tm), pl.BlockSpec((tm, D), lambda i: (i, 0))],
        out_shape=[o_shape, jax.ShapeDtypeStruct((T, D), BF16)],
        compiler_params=_row_params(),
    )(h, g, w)


def proj_plain(a, w, *, name, tm, tn, out_dtype, relu2_out=False):
    T = a.shape[0]
    N = _mat_shape(w)[1]

    def body(a_ref, w_ref, o_ref, *sq_ref):
        av = a_ref[...]
        for c in range(N // tn):
            z = _dot(av, _cols(w_ref, c, tn))
            _set_cols(o_ref, c, tn, z.astype(out_dtype))
            if relu2_out:
                zp = jnp.maximum(z, 0.0)
                _set_cols(sq_ref[0], c, tn, (zp * zp).astype(BF16))

    o_shape = jax.ShapeDtypeStruct((T, N), out_dtype)
    out_specs, out_shape = [_row_block(o_shape, tm)], [o_shape]
    if relu2_out:
        out_specs.append(_row_block(o_shape, tm))
        out_shape.append(jax.ShapeDtypeStruct((T, N), BF16))
    outs = pl.pallas_call(
        body, name=name, grid=(T // tm,),
        in_specs=[_row_block(a, tm), _resident(w)],
        out_specs=out_specs, out_shape=out_shape,
        compiler_params=_row_params(),
    )(a, w)
    return outs if relu2_out else outs[0]


def proj_res_norm(a, w, res, g, *, name, tm, tn):
    T = res.shape[0]
    D = w.shape[1]

    def body(a_ref, w_ref, r_ref, g_ref, h_ref, n_ref):
        av = _all_cols(a_ref)
        for c in range(D // tn):
            sl = slice(c * tn, (c + 1) * tn)
            h_ref[:, sl] = r_ref[:, sl] + _dot(av, w_ref[:, sl])
        hv = h_ref[...]
        n_ref[...] = (hv * _rms(hv) * g_ref[...]).astype(BF16)

    row = pl.BlockSpec((tm, D), lambda i: (i, 0))
    return pl.pallas_call(
        body, name=name, grid=(T // tm,),
        in_specs=[_row_block(a, tm), _resident(w), row, pl.BlockSpec((1, D), lambda i: (0, 0))],
        out_specs=[row, row],
        out_shape=[jax.ShapeDtypeStruct((T, D), F32), jax.ShapeDtypeStruct((T, D), BF16)],
        compiler_params=_row_params(),
    )(a, w, res, g)


def proj_res_loss(a, w, res, g, target, *, name, tm, tn):
    T = res.shape[0]
    D = w.shape[1]

    def body(a_ref, w_ref, r_ref, g_ref, t_ref, dh_ref, dhb_ref, ls_ref, dg_ref):
        i = pl.program_id(0)
        av = a_ref[...]
        for c in range(D // tn):
            sl = slice(c * tn, (c + 1) * tn)
            dh_ref[:, sl] = r_ref[:, sl] + _dot(av, w_ref[:, sl])
        x = dh_ref[...]
        gv = g_ref[...]
        r = _rms(x)
        xr = x * r
        d = xr * gv - t_ref[...]
        dy = d * (1.0 / D)
        dyg = dy * gv
        dx = r * (dyg - xr * jnp.mean(dyg * xr, axis=-1, keepdims=True))
        dh_ref[...] = dx
        dhb_ref[...] = dx.astype(BF16)
        ls = jnp.sum(d * d, axis=0, keepdims=True)
        dg = jnp.sum(dy * xr, axis=0, keepdims=True)

        @pl.when(i == 0)
        def _():
            ls_ref[...] = ls
            dg_ref[...] = dg

        @pl.when(i > 0)
        def _():
            ls_ref[...] += ls
            dg_ref[...] += dg

    row = pl.BlockSpec((tm, D), lambda i: (i, 0))
    vec = pl.BlockSpec((1, D), lambda i: (0, 0))
    return pl.pallas_call(
        body, name=name, grid=(T // tm,),
        in_specs=[_row_block(a, tm), _resident(w), row, vec, row],
        out_specs=[row, row, vec, vec],
        out_shape=[jax.ShapeDtypeStruct((T, D), F32), jax.ShapeDtypeStruct((T, D), BF16),
                   jax.ShapeDtypeStruct((1, D), F32), jax.ShapeDtypeStruct((1, D), F32)],
        compiler_params=_row_params(),
    )(a, w, res, g, target)


def _anchor_spec(after):
    return pl.BlockSpec(after.shape, lambda i: (0, 0))


def back_plain(a, w, *, name, tm, tn, out_dtype, out_slabs=None, relu2_of=None, after=None):
    T = a.shape[0]
    N = w.shape[0]
    has_z = relu2_of is not None
    o_shape = _out_struct(T, N, out_slabs, out_dtype)

    def body(*refs):
        a_ref, w_ref = refs[0], refs[1]
        o_ref = refs[-1]
        av = a_ref[...]
        for c in range(N // tn):
            out = _dot_nt(av, w_ref[c * tn:(c + 1) * tn, :])
            if has_z:
                out = out * (2.0 * jnp.maximum(refs[2][:, c * tn:(c + 1) * tn], 0.0))
            _set_cols(o_ref, c, tn, out.astype(out_dtype))

    in_specs, args = [_row_block(a, tm), _resident(w)], [a, w]
    if has_z:
        in_specs.append(_row_block(relu2_of, tm))
        args.append(relu2_of)
    if after is not None:
        in_specs.append(_anchor_spec(after))
        args.append(after)
    return pl.pallas_call(
        body, name=name, grid=(T // tm,),
        in_specs=in_specs, out_specs=_row_block(o_shape, tm), out_shape=o_shape,
        compiler_params=_row_params(),
    )(*args)


def back_norm(a, w, h, g, dres, *, name, tm, tk, after=None):
    T, K = _mat_shape(a)
    D = h.shape[1]
    with_dh = dres is not None

    def body(*refs):
        a_ref, w_ref, h_ref, g_ref = refs[:4]
        i = pl.program_id(0)
        dn = None
        for kc in range(K // tk):
            part = _dot_nt(_cols(a_ref, kc, tk).astype(BF16), _cols(w_ref, kc, tk))
            dn = part if dn is None else dn + part
        x = h_ref[...]
        r = _rms(x)
        xr = x * r
        dgp = jnp.sum(dn * xr, axis=0, keepdims=True)
        dg_ref = refs[-1]

        @pl.when(i == 0)
        def _():
            dg_ref[...] = dgp

        @pl.when(i > 0)
        def _():
            dg_ref[...] += dgp

        if with_dh:
            dh_ref, dhb_ref = refs[-3], refs[-2]
            dyg = dn * g_ref[...]
            out = refs[4][...] + r * (dyg - xr * jnp.mean(dyg * xr, axis=-1, keepdims=True))
            dh_ref[...] = out
            dhb_ref[...] = out.astype(BF16)

    row = pl.BlockSpec((tm, D), lambda i: (i, 0))
    vec = pl.BlockSpec((1, D), lambda i: (0, 0))
    in_specs, args = [_row_block(a, tm), _resident(w), row, vec], [a, w, h, g]
    if with_dh:
        in_specs.append(row)
        args.append(dres)
        out_specs = [row, row, vec]
        out_shape = [jax.ShapeDtypeStruct((T, D), F32), jax.ShapeDtypeStruct((T, D), BF16),
                     jax.ShapeDtypeStruct((1, D), F32)]
    else:
        out_specs = vec
        out_shape = jax.ShapeDtypeStruct((1, D), F32)
    if after is not None:
        in_specs.append(_anchor_spec(after))
        args.append(after)
    return pl.pallas_call(
        body, name=name, grid=(T // tm,),
        in_specs=in_specs, out_specs=out_specs, out_shape=out_shape,
        compiler_params=_row_params(),
    )(*args)


def wgrad(a, b, *, name, tt, tn, out_slabs=None):
    T, K = _mat_shape(a)
    N = _mat_shape(b)[1]
    nt = T // tt
    o_shape = _out_struct(K, N, out_slabs, BF16)

    def body(a_ref, b_ref, o_ref, acc_ref):
        t = pl.program_id(0)
        at = _all_cols(a_ref).astype(BF16).T

        @pl.when(t == 0)
        def _():
            acc_ref[...] = jnp.zeros_like(acc_ref)

        for c in range(N // tn):
            acc_ref[:, c * tn:(c + 1) * tn] += _dot(at, _cols(b_ref, c, tn).astype(BF16))

        @pl.when(t == nt - 1)
        def _():
            for c in range(N // tn):
                _set_cols(o_ref, c, tn, acc_ref[:, c * tn:(c + 1) * tn].astype(BF16))

    return pl.pallas_call(
        body, name=name, grid=(nt,),
        in_specs=[_row_block(a, tt), _row_block(b, tt)],
        out_specs=_resident(o_shape), out_shape=o_shape,
        scratch_shapes=[pltpu.VMEM((K, N), F32)],
        compiler_params=_row_params(),
    )(a, b)


def _chunk_tri(tm, upper):
    r = lax.broadcasted_iota(jnp.int32, (tm, tm), 0)
    c = lax.broadcasted_iota(jnp.int32, (tm, tm), 1)
    same = (r // CHUNK) == (c // CHUNK)
    keep = (c >= r) if upper else (c <= r)
    return jnp.where(same & keep, 1.0, 0.0).astype(BF16)


def _pool_windows_back(ext_ref, tm):
    n = tm + 32
    ext_ref[1, 8:n] = ext_ref[0, 8:n] + ext_ref[0, 7:n - 1]
    ext_ref[2, 16:n] = ext_ref[1, 16:n] + ext_ref[1, 14:n - 2]
    ext_ref[3, 24:n] = ext_ref[2, 24:n] + ext_ref[2, 20:n - 4]
    s2 = ext_ref[1, 32:n]
    s4 = ext_ref[2, 32:n]
    s8 = ext_ref[3, 32:n]
    s16 = s8 + ext_ref[3, 24:n - 8]
    return s2, s4, s8, s16


def _pool_windows_fwd(ext_ref, tm):
    n = tm + 32
    ext_ref[1, 0:n - 8] = ext_ref[0, 0:n - 8] + ext_ref[0, 1:n - 7]
    ext_ref[2, 0:n - 16] = ext_ref[1, 0:n - 16] + ext_ref[1, 2:n - 14]
    ext_ref[3, 0:n - 24] = ext_ref[2, 0:n - 24] + ext_ref[2, 4:n - 20]
    s2 = ext_ref[1, 0:tm]
    s4 = ext_ref[2, 0:tm]
    s8 = ext_ref[3, 0:tm]
    s16 = s8 + ext_ref[3, 8:tm + 8]
    return s2, s4, s8, s16


def _select_window(g, s2, s4, s8, s16):
    return jnp.where(g == 0, s2, jnp.where(g == 1, s4, jnp.where(g == 2, s8, s16)))


def _pool_count(g, pos):
    width = lax.shift_left(jnp.int32(2), g)
    return jnp.minimum(pos + 1, width).astype(F32)


def _hgrn_gates(zq, zf, th):
    lb = _sigmoid(th[0:1, :] - th[1:2, :])
    sig = _sigmoid(zf)
    f = lb + (1.0 - lb) * sig
    sq = _sigmoid(zq)
    return lb, sig, f, sq


def mixer_fwd(u5, pool_w_bf, scale4, theta4, gn4, *, seqs, seq_len, tm):
    T = u5.shape[1]
    tps = seq_len // tm
    nc = tm // CHUNK
    W = HEAD_W

    def body(u_ref, pw_ref, sc_ref, th_ref, gn_ref, y_ref, o_ref, st_ref, halo_ref, ext_ref, s_ref):
        g = pl.program_id(0)
        i = pl.program_id(2)

        @pl.when(i == 0)
        def _():
            halo_ref[...] = jnp.zeros_like(halo_ref)
            s_ref[...] = jnp.zeros_like(s_ref)

        row = lax.broadcasted_iota(jnp.int32, (tm, 1), 0)

        up = u_ref[0]
        ext_ref[0, 0:16] = jnp.zeros((16, W), F32)
        ext_ref[0, 16:32] = halo_ref[...]
        ext_ref[0, 32:32 + tm] = up
        win = _select_window(g, *_pool_windows_back(ext_ref, tm))
        p = win / _pool_count(g, i * tm + row) - up
        halo_ref[...] = up[tm - POOL_HALO:tm]
        y_ref[0] = (_dot(p.astype(BF16), pw_ref[...]) * sc_ref[...]).astype(BF16)

        zq, zf, zi, zg = u_ref[1], u_ref[2], u_ref[3], u_ref[4]
        lb, sig, f, sq = _hgrn_gates(zq, zf, th_ref[...])
        logf = jnp.log(f)
        kk = 1.0 - f
        q = zq * sq
        G = _tri_apply(_chunk_tri(tm, False), logf)
        causal = (lax.broadcasted_iota(jnp.int32, (CHUNK, CHUNK), 1)
                  <= lax.broadcasted_iota(jnp.int32, (CHUNK, CHUNK), 0))
        st = s_ref[...]
        outs = []
        for c in range(nc):
            sl = slice(c * CHUNK, (c + 1) * CHUNK)
            Gc, qc, kc = G[sl], q[sl], kk[sl]
            vb = zi[sl].astype(BF16)
            Gm = Gc[CHUNK // 2 - 1:CHUNK // 2]
            Gl = Gc[CHUNK - 1:CHUNK]
            a = _dot_nt((qc * jnp.exp(Gc - Gm)).astype(BF16), (kc * jnp.exp(Gm - Gc)).astype(BF16))
            a = jnp.where(causal, a, 0.0)
            o_intra = _dot(a.astype(BF16), vb)
            st_ref[c] = st
            o_inter = _dot_nt((qc * jnp.exp(Gc)).astype(BF16), st.astype(BF16))
            outs.append(o_intra + o_inter)
            d_st = _dot_tn(vb, (kc * jnp.exp(Gl - Gc)).astype(BF16))
            st = st * jnp.exp(Gl) + d_st
        s_ref[...] = st
        o = jnp.concatenate(outs, axis=0)
        o_ref[...] = o
        r = lax.rsqrt(jnp.mean(o * o, axis=-1, keepdims=True) + EPS)
        y_ref[1] = (o * r * gn_ref[...] * (zg * _sigmoid(zg))).astype(BF16)

    def rb(s, i):
        return s * tps + i

    return pl.pallas_call(
        body, name="mixer_fwd", grid=(4, seqs, tps),
        in_specs=[pl.BlockSpec((5, tm, W), lambda g, s, i: (0, rb(s, i), g)),
                  pl.BlockSpec((None, W, W), lambda g, s, i: (g, 0, 0)),
                  pl.BlockSpec((None, 1, W), lambda g, s, i: (g, 0, 0)),
                  pl.BlockSpec((None, 2, W), lambda g, s, i: (g, 0, 0)),
                  pl.BlockSpec((None, 1, W), lambda g, s, i: (g, 0, 0))],
        out_specs=[pl.BlockSpec((2, tm, W), lambda g, s, i: (0, rb(s, i), g)),
                   pl.BlockSpec((tm, W), lambda g, s, i: (rb(s, i), g)),
                   pl.BlockSpec((nc, None, W, W), lambda g, s, i: (rb(s, i), g, 0, 0))],
        out_shape=[jax.ShapeDtypeStruct((2, T, 4 * W), BF16),
                   jax.ShapeDtypeStruct((T, 4 * W), F32),
                   jax.ShapeDtypeStruct((T // CHUNK, 4, W, W), F32)],
        scratch_shapes=[pltpu.VMEM((POOL_HALO, W), F32),
                        pltpu.VMEM((4, tm + 32, W), F32),
                        pltpu.VMEM((W, W), F32)],
        compiler_params=_cparams(("arbitrary", "arbitrary", "arbitrary")),
    )(u5, pool_w_bf, scale4, theta4, gn4)


def mixer_bwd(u5, dy2, o_pre, st_prev, pool_w_bf, scale4, theta4, gn4, *, seqs, seq_len, tm):
    T = u5.shape[1]
    tps = seq_len // tm
    nc = tm // CHUNK
    W = HEAD_W
    hb = tm // POOL_HALO

    def body(u_ref, uh_ref, dy_ref, o_ref, st_ref, pw_ref, sc_ref, th_ref, gn_ref,
             du_ref, dpw_ref, dsc_ref, dlb_ref, dgn_ref, nxt_ref, ext_ref, ds_ref):
        g = pl.program_id(0)
        s = pl.program_id(1)
        i = pl.program_id(2)
        tile = tps - 1 - i
        first = (s == 0) & (i == 0)

        @pl.when(i == 0)
        def _():
            nxt_ref[...] = jnp.zeros_like(nxt_ref)
            ds_ref[...] = jnp.zeros_like(ds_ref)

        row = lax.broadcasted_iota(jnp.int32, (tm, 1), 0)
        cnt = _pool_count(g, tile * tm + row)

        def accumulate(ref, val):
            @pl.when(first)
            def _():
                ref[...] = val

            @pl.when(jnp.logical_not(first))
            def _():
                ref[...] += val

        up = u_ref[0]
        ext_ref[0, 0:16] = jnp.zeros((16, W), F32)
        ext_ref[0, 16:32] = jnp.where(tile == 0, 0.0, uh_ref[...])
        ext_ref[0, 32:32 + tm] = up
        win = _select_window(g, *_pool_windows_back(ext_ref, tm))
        pb = (win / cnt - up).astype(BF16)
        dyp = dy_ref[0]
        z = _dot(pb, pw_ref[...])
        accumulate(dsc_ref, jnp.sum(dyp * z, axis=0, keepdims=True))
        dz = (dyp * sc_ref[...]).astype(BF16)
        accumulate(dpw_ref, _dot_tn(pb, dz))
        dp = _dot_nt(dz, pw_ref[...])
        e = dp / cnt
        ext_ref[0, 0:tm] = e
        ext_ref[0, tm:tm + 16] = nxt_ref[...]
        ext_ref[0, tm + 16:tm + 32] = jnp.zeros((16, W), F32)
        lead = _select_window(g, *_pool_windows_fwd(ext_ref, tm))
        nxt_ref[...] = e[0:POOL_HALO]
        du_ref[0] = (lead - dp).astype(BF16)

        zq, zf, zi, zg = u_ref[1], u_ref[2], u_ref[3], u_ref[4]
        lb, sig, f, sq = _hgrn_gates(zq, zf, th_ref[...])
        logf = jnp.log(f)
        kk = 1.0 - f
        q = zq * sq
        G = _tri_apply(_chunk_tri(tm, False), logf)

        dyh = dy_ref[1]
        o = o_ref[...]
        sg = _sigmoid(zg)
        r = lax.rsqrt(jnp.mean(o * o, axis=-1, keepdims=True) + EPS)
        orr = o * r
        gn = gn_ref[...]
        du_ref[4] = (dyh * (orr * gn) * (sg * (1.0 + zg * (1.0 - sg)))).astype(BF16)
        don = dyh * (zg * sg)
        accumulate(dgn_ref, jnp.sum(don * orr, axis=0, keepdims=True))
        dog = don * gn
        do = r * (dog - orr * jnp.mean(dog * orr, axis=-1, keepdims=True))

        causal = (lax.broadcasted_iota(jnp.int32, (CHUNK, CHUNK), 1)
                  <= lax.broadcasted_iota(jnp.int32, (CHUNK, CHUNK), 0))
        crow = lax.broadcasted_iota(jnp.int32, (CHUNK, 1), 0)
        dsn = ds_ref[...]
        dq_parts, dk_parts, dv_parts, dg_parts = [None] * nc, [None] * nc, [None] * nc, [None] * nc
        for c in reversed(range(nc)):
            sl = slice(c * CHUNK, (c + 1) * CHUNK)
            Gc, qc, kc = G[sl], q[sl], kk[sl]
            vb = zi[sl].astype(BF16)
            dob = do[sl].astype(BF16)
            Gm = Gc[CHUNK // 2 - 1:CHUNK // 2]
            Gl = Gc[CHUNK - 1:CHUNK]
            e_q, e_k, e_e, e_g = jnp.exp(Gc - Gm), jnp.exp(Gm - Gc), jnp.exp(Gl - Gc), jnp.exp(Gc)
            decay = jnp.exp(Gl)
            qr, kr, ke, qg = qc * e_q, kc * e_k, kc * e_e, qc * e_g
            qrb, krb, keb, qgb = qr.astype(BF16), kr.astype(BF16), ke.astype(BF16), qg.astype(BF16)
            st = st_ref[c]
            dsnb = dsn.astype(BF16)
            a = jnp.where(causal, _dot_nt(qrb, krb), 0.0).astype(BF16)
            da = jnp.where(causal, _dot_nt(dob, vb), 0.0).astype(BF16)
            dv_parts[c] = _dot_tn(a, dob) + _dot_nt(keb, dsnb)
            dqr = _dot(da, krb)
            dkr = _dot_tn(da, qrb)
            dqg = _dot(dob, st.astype(BF16))
            dke = _dot(vb, dsnb)
            ddecay = jnp.sum(dsn * st, axis=0, keepdims=True)
            dsn = _dot_tn(dob, qgb) + dsn * decay
            t_qr, t_kr, t_qg, t_ke = dqr * qr, dkr * kr, dqg * qg, dke * ke
            dq_parts[c] = dqr * e_q + dqg * e_g
            dk_parts[c] = dkr * e_k + dke * e_e
            dgm = jnp.sum(t_kr - t_qr, axis=0, keepdims=True)
            dgl = jnp.sum(t_ke, axis=0, keepdims=True) + ddecay * decay
            dg_parts[c] = (t_qr - t_kr + t_qg - t_ke
                           + jnp.where(crow == CHUNK // 2 - 1, dgm, 0.0)
                           + jnp.where(crow == CHUNK - 1, dgl, 0.0))
        ds_ref[...] = dsn
        dq = jnp.concatenate(dq_parts, axis=0)
        dk = jnp.concatenate(dk_parts, axis=0)
        dv = jnp.concatenate(dv_parts, axis=0)
        dG = jnp.concatenate(dg_parts, axis=0)
        dlogf = _tri_apply(_chunk_tri(tm, True), dG)
        df = dlogf / f - dk
        du_ref[1] = (dq * (sq * (1.0 + zq * (1.0 - sq)))).astype(BF16)
        du_ref[2] = (df * (1.0 - lb) * (sig * (1.0 - sig))).astype(BF16)
        du_ref[3] = dv.astype(BF16)
        accumulate(dlb_ref, jnp.sum(df * (1.0 - sig), axis=0, keepdims=True) * (lb * (1.0 - lb)))

    def rb(s, i):
        return s * tps + (tps - 1 - i)

    vec = pl.BlockSpec((None, 1, W), lambda g, s, i: (g, 0, 0))
    mat = pl.BlockSpec((None, W, W), lambda g, s, i: (g, 0, 0))
    return pl.pallas_call(
        body, name="mixer_bwd", grid=(4, seqs, tps),
        in_specs=[pl.BlockSpec((5, tm, W), lambda g, s, i: (0, rb(s, i), g)),
                  pl.BlockSpec((None, POOL_HALO, W), lambda g, s, i: (0, jnp.maximum(rb(s, i) * hb - 1, 0), g)),
                  pl.BlockSpec((2, tm, W), lambda g, s, i: (0, rb(s, i), g)),
                  pl.BlockSpec((tm, W), lambda g, s, i: (rb(s, i), g)),
                  pl.BlockSpec((nc, None, W, W), lambda g, s, i: (rb(s, i), g, 0, 0)),
                  mat, vec,
                  pl.BlockSpec((None, 2, W), lambda g, s, i: (g, 0, 0)),
                  vec],
        out_specs=[pl.BlockSpec((5, tm, W), lambda g, s, i: (0, rb(s, i), g)), mat, vec, vec, vec],
        out_shape=[jax.ShapeDtypeStruct((5, T, 4 * W), BF16),
                   jax.ShapeDtypeStruct((4, W, W), F32),
                   jax.ShapeDtypeStruct((4, 1, W), F32),
                   jax.ShapeDtypeStruct((4, 1, W), F32),
                   jax.ShapeDtypeStruct((4, 1, W), F32)],
        scratch_shapes=[pltpu.VMEM((POOL_HALO, W), F32),
                        pltpu.VMEM((4, tm + 32, W), F32),
                        pltpu.VMEM((W, W), F32)],
        compiler_params=_cparams(("arbitrary", "arbitrary", "arbitrary")),
    )(u5, u5, dy2, o_pre, st_prev, pool_w_bf, scale4, theta4, gn4)


def _attn_probs(q, k, hd):
    s = _dot_nt(q, k) * (1.0 / (hd ** 0.5))
    e = jnp.exp(s - jnp.max(s, axis=-1, keepdims=True))
    return e / jnp.sum(e, axis=-1, keepdims=True)


def attn_fwd(q, kv3, *, seqs, seq_len, n_mem, tm):
    T, D = q.shape
    hd = D // XATTN_HEADS
    tps = seq_len // tm

    def body(q_ref, kv_ref, o_ref):
        p = _attn_probs(q_ref[...], kv_ref[0], hd)
        o_ref[...] = _dot(p.astype(BF16), kv_ref[1]).astype(BF16)

    return pl.pallas_call(
        body, name="attn_fwd", grid=(seqs, XATTN_HEADS, tps),
        in_specs=[pl.BlockSpec((tm, hd), lambda b, h, i: (b * tps + i, h)),
                  pl.BlockSpec((2, n_mem, hd), lambda b, h, i: (0, b, h))],
        out_specs=pl.BlockSpec((tm, hd), lambda b, h, i: (b * tps + i, h)),
        out_shape=jax.ShapeDtypeStruct((T, D), BF16),
        compiler_params=_cparams(("parallel", "parallel", "arbitrary")),
    )(q, kv3)


def attn_bwd(q, kv3, do, *, seqs, seq_len, n_mem, tm):
    T, D = q.shape
    hd = D // XATTN_HEADS
    tps = seq_len // tm

    def body(q_ref, kv_ref, do_ref, dq_ref, dkv_ref):
        i = pl.program_id(2)
        qv, k, v, dov = q_ref[...], kv_ref[0], kv_ref[1], do_ref[...]
        p = _attn_probs(qv, k, hd)
        dp = _dot_nt(dov, v)
        ds = (p * (dp - jnp.sum(dp * p, axis=-1, keepdims=True)) * (1.0 / (hd ** 0.5))).astype(BF16)
        dq_ref[...] = _dot(ds, k).astype(BF16)
        dk = _dot_tn(ds, qv)
        dv = _dot_tn(p.astype(BF16), dov)

        @pl.when(i == 0)
        def _():
            dkv_ref[0] = dk
            dkv_ref[1] = dv

        @pl.when(i > 0)
        def _():
            dkv_ref[0] += dk
            dkv_ref[1] += dv

    qspec = pl.BlockSpec((tm, hd), lambda b, h, i: (b * tps + i, h))
    kvspec = pl.BlockSpec((2, n_mem, hd), lambda b, h, i: (0, b, h))
    return pl.pallas_call(
        body, name="attn_bwd", grid=(seqs, XATTN_HEADS, tps),
        in_specs=[qspec, kvspec, qspec],
        out_specs=[qspec, kvspec],
        out_shape=[jax.ShapeDtypeStruct((T, D), BF16), jax.ShapeDtypeStruct((2, seqs * n_mem, D), F32)],
        compiler_params=_cparams(("parallel", "parallel", "arbitrary")),
    )(q, kv3, do)


def final_loss(h, g, target, *, tm):
    T, D = h.shape

    def body(h_ref, g_ref, t_ref, dh_ref, dhb_ref, ls_ref, dg_ref):
        i = pl.program_id(0)
        x = h_ref[...]
        gv = g_ref[...]
        r = lax.rsqrt(jnp.mean(x * x, axis=-1, keepdims=True) + EPS)
        xr = x * r
        d = xr * gv - t_ref[...]
        dy = d * (1.0 / D)
        dyg = dy * gv
        dx = r * (dyg - xr * jnp.mean(dyg * xr, axis=-1, keepdims=True))
        dh_ref[...] = dx
        dhb_ref[...] = dx.astype(BF16)
        ls = jnp.sum(d * d, axis=0, keepdims=True)
        dg = jnp.sum(dy * xr, axis=0, keepdims=True)

        @pl.when(i == 0)
        def _():
            ls_ref[...] = ls
            dg_ref[...] = dg

        @pl.when(i > 0)
        def _():
            ls_ref[...] += ls
            dg_ref[...] += dg

    row = pl.BlockSpec((tm, D), lambda i: (i, 0))
    vec = pl.BlockSpec((1, D), lambda i: (0, 0))
    return pl.pallas_call(
        body, name="final_loss", grid=(T // tm,),
        in_specs=[row, vec, row], out_specs=[row, row, vec, vec],
        out_shape=[jax.ShapeDtypeStruct((T, D), F32), jax.ShapeDtypeStruct((T, D), BF16),
                   jax.ShapeDtypeStruct((1, D), F32), jax.ShapeDtypeStruct((1, D), F32)],
        compiler_params=_cparams(("arbitrary",)),
    )(h, g, target)


def _my_place():
    return lax.axis_index("x"), lax.axis_index("y"), lax.axis_index("c")


def _slot_of(px, py, pc):
    return 4 * px + 2 * py + pc


def _peer(k, x, y, c):
    return (1 - x if (k >> 2) & 1 else x, 1 - y if (k >> 1) & 1 else y, 1 - c if k & 1 else c)


def _split_copies(src_refs, land_refs, send_sems, recv_sems, scatter):
    x, y, c = _my_place()
    mine = _slot_of(x, y, c)
    copies = []
    for a, (src, land) in enumerate(zip(src_refs, land_refs)):
        for k in range(1, N_DEV):
            peer = _peer(k, x, y, c)
            copies.append(pltpu.make_async_remote_copy(
                src_ref=src.at[_slot_of(*peer)] if scatter else src, dst_ref=land.at[mine],
                send_sem=send_sems.at[a * N_PEERS + k - 1], recv_sem=recv_sems.at[a * N_PEERS + k - 1],
                device_id=peer, device_id_type=MESH))
    return copies


def split_start(groups, *, name, scatter):
    sizes = [len(srcs) for srcs, _ in groups]
    n_arr = sum(sizes)
    flat = [a for srcs, lands in groups for a in list(srcs) + list(lands)]

    def body(*refs):
        ins = refs[:2 * n_arr]
        sems = refs[4 * n_arr:4 * n_arr + 2 * len(groups)]
        token = refs[-1]
        at = 0
        for gi, n in enumerate(sizes):
            for cp in _split_copies(ins[at:at + n], ins[at + n:at + 2 * n], sems[2 * gi], sems[2 * gi + 1], scatter):
                cp.start()
            at += 2 * n
        token[...] = jnp.zeros_like(token)

    sem_shapes = []
    for n in sizes:
        sem_shapes += [pltpu.SemaphoreType.DMA((n * N_PEERS,))] * 2
    outs = pl.pallas_call(
        body, name=name,
        out_shape=tuple(pltpu.HBM(a.shape, a.dtype) for a in flat) + tuple(sem_shapes)
        + (jax.ShapeDtypeStruct((8, 128), F32),),
        in_specs=(HBM,) * len(flat),
        out_specs=(HBM,) * len(flat) + (SEM,) * len(sem_shapes) + (pl.BlockSpec(memory_space=pltpu.VMEM),),
        input_output_aliases={i: i for i in range(len(flat))},
        compiler_params=pltpu.CompilerParams(has_side_effects=pltpu.SideEffectType.DATAFLOW_SIDE_EFFECTING),
    )(*[pltpu.with_memory_space_constraint(a, pltpu.HBM) for a in flat])
    thru, sems, token = outs[:len(flat)], outs[len(flat):-1], outs[-1]
    started, at = [], 0
    for gi, n in enumerate(sizes):
        started.append((sems[2 * gi], sems[2 * gi + 1], thru[at:at + n], thru[at + n:at + 2 * n]))
        at += 2 * n
    return started, token


def split_wait(started, after, *, name, scatter):
    sizes = [len(g[2]) for g in started]
    n_arr = sum(sizes)
    flat = [a for g in started for a in list(g[2]) + list(g[3])]
    sems = [s for g in started for s in g[:2]]

    def body(*refs):
        ins = refs[:2 * n_arr]
        sem_refs = refs[2 * n_arr:2 * n_arr + len(sems)]
        at = 0
        for gi, n in enumerate(sizes):
            for cp in _split_copies(ins[at:at + n], ins[at + n:at + 2 * n], sem_refs[2 * gi], sem_refs[2 * gi + 1], scatter):
                cp.wait_send()
                cp.wait_recv()
            at += 2 * n

    outs = pl.pallas_call(
        body, name=name,
        out_shape=tuple(pltpu.HBM(a.shape, a.dtype) for a in flat),
        in_specs=(HBM,) * len(flat) + (SEM,) * len(sems) + (pl.BlockSpec(memory_space=pl.ANY),),
        out_specs=(HBM,) * len(flat),
        input_output_aliases={i: i for i in range(len(flat))},
        compiler_params=pltpu.CompilerParams(has_side_effects=pltpu.SideEffectType.DATAFLOW_SIDE_EFFECTING),
    )(*flat, *sems, after)
    lands, at = [], 0
    for n in sizes:
        lands.append(outs[at + n:at + 2 * n])
        at += 2 * n
    return lands


def allgather_small(bufs):
    n = len(bufs)

    def body(*refs):
        srcs, outs = refs[:n], refs[n:2 * n]
        send_sems, recv_sems, local_sems = refs[2 * n:]
        x, y, c = _my_place()
        mine = _slot_of(x, y, c)
        local = [pltpu.make_async_copy(s, o.at[mine], local_sems.at[a]) for a, (s, o) in enumerate(zip(srcs, outs))]
        for cp in local:
            cp.start()
        copies = _split_copies(srcs, outs, send_sems, recv_sems, False)
        for cp in copies:
            cp.start()
        for cp in copies:
            cp.wait()
        for cp in local:
            cp.wait()

    return pl.pallas_call(
        body, name="allgather_small",
        out_shape=[jax.ShapeDtypeStruct((N_DEV,) + b.shape, b.dtype) for b in bufs],
        in_specs=[HBM] * n, out_specs=[HBM] * n,
        scratch_shapes=[pltpu.SemaphoreType.DMA((n * N_PEERS,)), pltpu.SemaphoreType.DMA((n * N_PEERS,)),
                        pltpu.SemaphoreType.DMA((n,))],
    )(*bufs)


def _adamw_math(g, w, m, v):
    c1 = 1.0 - ADAM_B1 ** ADAM_STEP
    c2 = 1.0 - ADAM_B2 ** ADAM_STEP
    nm = ADAM_B1 * m + (1.0 - ADAM_B1) * g
    nv = ADAM_B2 * v + (1.0 - ADAM_B2) * (g * g)
    delta = -ADAM_LR * ((nm / c1) / (jnp.sqrt(nv / c2) + ADAM_EPS) + ADAM_WD * w)
    return delta, nm, nv


def adamw_sharded(me, own, recv, w, m, v, *, name, tr):
    R, C = w.shape

    def body(me_ref, *refs):
        parts = refs[:N_DEV]
        w_ref, m_ref, v_ref, g_ref, d_ref, nm_ref, nv_ref = refs[N_DEV:]
        g = parts[0][...].astype(F32)
        for p in parts[1:]:
            g = g + p[...].astype(F32)
        g_ref[...] = g
        d_ref[...], nm_ref[...], nv_ref[...] = _adamw_math(g, w_ref[...], m_ref[...], v_ref[...])

    def slab(k):
        return pl.BlockSpec((None, tr, C), lambda i, me_ref: (me_ref[0] ^ k, i, 0))

    blk = pl.BlockSpec((tr, C), lambda i, me_ref: (i, 0))
    out = jax.ShapeDtypeStruct((R, C), F32)
    return pl.pallas_call(
        body, name=name,
        grid_spec=pltpu.PrefetchScalarGridSpec(
            num_scalar_prefetch=1, grid=(R // tr,),
            in_specs=[slab(k) for k in range(N_DEV)] + [blk, blk, blk],
            out_specs=[blk, blk, blk, blk]),
        out_shape=[out, out, out, out],
        compiler_params=_cparams(("parallel",)),
    )(me, own, *([recv] * N_PEERS), w, m, v)


def adamw_replicated(parts, ws, ms, vs, rows):
    n_buf, n_par = len(parts), len(ws)

    def body(*refs):
        p_refs = refs[:n_buf]
        w_refs = refs[n_buf:n_buf + n_par]
        m_refs = refs[n_buf + n_par:n_buf + 2 * n_par]
        v_refs = refs[n_buf + 2 * n_par:n_buf + 3 * n_par]
        outs = refs[n_buf + 3 * n_par:]
        sums = []
        for p in p_refs:
            g = p[0]
            for s in range(1, N_DEV):
                g = g + p[s]
            sums.append(g)
        for j, (b, r0, nr) in enumerate(rows):
            g = sums[b][r0:r0 + nr]
            delta, nm, nv = _adamw_math(g, w_refs[j][...], m_refs[j][...], v_refs[j][...])
            outs[j][...] = g
            outs[n_par + j][...] = delta
            outs[2 * n_par + j][...] = nm
            outs[3 * n_par + j][...] = nv

    shapes = [jax.ShapeDtypeStruct(w.shape, F32) for w in ws]
    outs = pl.pallas_call(
        body, name="adamw_replicated", out_shape=shapes * 4,
        compiler_params=pltpu.CompilerParams(vmem_limit_bytes=V7X_VMEM_LIMIT),
    )(*parts, *ws, *ms, *vs)
    return outs[:n_par], outs[n_par:2 * n_par], outs[2 * n_par:3 * n_par], outs[3 * n_par:]


BIG = ("w_in", "w_out", "xw_q", "xw_kv", "xw_o", "w_up", "w_down")
COL_SHARDED = ("w_in", "xw_kv", "w_up")
WEIGHTS = ("norm_mix", "w_in", "pool_w", "pool_scale", "lb_theta", "hgrn_norm", "w_out", "norm_xq",
           "norm_mem", "xw_q", "xw_kv", "xw_o", "norm_mlp", "w_up", "w_down", "norm_final")
SMALL = (("pool_w", (4 * HEAD_W, HEAD_W), 0, 0),
         ("norm_mix", (1, 1024), 1, 0), ("norm_xq", (1, 1024), 1, 1), ("norm_mem", (1, 1024), 1, 2),
         ("norm_mlp", (1, 1024), 1, 3), ("norm_final", (1, 1024), 1, 4),
         ("pool_scale", (1, 512), 2, 0), ("hgrn_norm", (1, 512), 2, 1), ("lb_theta", (2, 512), 2, 2))


def _pad_rows(a, rows):
    return jnp.concatenate([a, jnp.zeros((rows - a.shape[0], a.shape[1]), a.dtype)], axis=0)


def kernel(x, mem, norm_mix, w_in, pool_w, pool_scale, lb_theta, hgrn_norm, w_out, norm_xq, norm_mem, xw_q, xw_kv, xw_o, norm_mlp, w_up, w_down, norm_final, loss_target, m_norm_mix, m_w_in, m_pool_w, m_pool_scale, m_lb_theta, m_hgrn_norm, m_w_out, m_norm_xq, m_norm_mem, m_xw_q, m_xw_kv, m_xw_o, m_norm_mlp, m_w_up, m_w_down, m_norm_final, v_norm_mix, v_w_in, v_pool_w, v_pool_scale, v_lb_theta, v_hgrn_norm, v_w_out, v_norm_xq, v_norm_mem, v_xw_q, v_xw_kv, v_xw_o, v_norm_mlp, v_w_up, v_w_down, v_norm_final):
    w = dict(norm_mix=norm_mix, w_in=w_in, pool_w=pool_w, pool_scale=pool_scale, lb_theta=lb_theta,
             hgrn_norm=hgrn_norm, w_out=w_out, norm_xq=norm_xq, norm_mem=norm_mem, xw_q=xw_q, xw_kv=xw_kv,
             xw_o=xw_o, norm_mlp=norm_mlp, w_up=w_up, w_down=w_down, norm_final=norm_final)
    mom = dict(norm_mix=m_norm_mix, w_in=m_w_in, pool_w=m_pool_w, pool_scale=m_pool_scale, lb_theta=m_lb_theta,
               hgrn_norm=m_hgrn_norm, w_out=m_w_out, norm_xq=m_norm_xq, norm_mem=m_norm_mem, xw_q=m_xw_q,
               xw_kv=m_xw_kv, xw_o=m_xw_o, norm_mlp=m_norm_mlp, w_up=m_w_up, w_down=m_w_down,
               norm_final=m_norm_final)
    var = dict(norm_mix=v_norm_mix, w_in=v_w_in, pool_w=v_pool_w, pool_scale=v_pool_scale, lb_theta=v_lb_theta,
               hgrn_norm=v_hgrn_norm, w_out=v_w_out, norm_xq=v_norm_xq, norm_mem=v_norm_mem, xw_q=v_xw_q,
               xw_kv=v_xw_kv, xw_o=v_xw_o, norm_mlp=v_norm_mlp, w_up=v_w_up, w_down=v_w_down,
               norm_final=v_norm_final)

    seqs, seq_len, D = x.shape
    n_mem = mem.shape[1]
    T = seqs * seq_len
    W = HEAD_W
    x2 = x.reshape(T, D)
    mem2 = mem.reshape(seqs * n_mem, D)
    tgt2 = loss_target.reshape(T, D)
    tm_big = min(1024, T)
    tm_mid = min(512, T)
    tm_mix = min(256, seq_len)
    tm_att = min(1024, seq_len)
    tkv = min(512, seqs * n_mem)
    px, py, pc = _my_place()
    me = _slot_of(px, py, pc).astype(jnp.int32)
    me1 = me.reshape(1)

    shard_bf = {n: w[n][0].astype(BF16) for n in BIG}

    def landing(n):
        zone = lax.empty((N_DEV,) + shard_bf[n].shape, BF16)
        return lax.dynamic_update_slice(zone, shard_bf[n][None], (me, 0, 0))

    ag_groups = (("w_in",), ("w_out", "xw_q", "xw_kv", "xw_o"), ("w_up", "w_down"))
    ag_started, _ = split_start([([shard_bf[n] for n in grp], [landing(n) for n in grp]) for grp in ag_groups],
                                name="weights_gather_start", scatter=False)

    pool_w_bf = pool_w[0].astype(BF16)
    scale4 = pool_scale.reshape(4, 1, W)
    gn4 = hgrn_norm.reshape(4, 1, W)
    theta4 = lb_theta.reshape(2, 4, W).transpose(1, 0, 2)
    g_final = norm_final.reshape(1, D)

    (wi3,), = split_wait(ag_started[:1], x2, name="weights_gather_wait_in", scatter=False)
    full_w_in = wi3.transpose(1, 0, 2).reshape(D, -1)
    u5, n1 = proj_norm(x2, norm_mix, full_w_in, name="in_proj", tm=tm_mid, tn=4 * W, out_dtype=F32, out_slabs=5)
    y2, o_pre, st_prev = mixer_fwd(u5, pool_w_bf, scale4, theta4, gn4, seqs=seqs, seq_len=seq_len, tm=tm_mix)
    (wo3, wq3, wkv3, wao3), = split_wait(ag_started[1:2], y2, name="weights_gather_wait_attn", scatter=False)
    full_w_out, full_xw_q, full_xw_o = wo3.reshape(D, D), wq3.reshape(D, D), wao3.reshape(D, D)
    tn = 4 * W
    h1, n2 = proj_res_norm(y2, full_w_out, x2, norm_xq, name="out_proj", tm=tm_mid, tn=tn)
    q = proj_plain(n2, full_xw_q, name="q_proj", tm=tm_mid, tn=tn, out_dtype=BF16)
    kv3, memn = proj_norm(mem2, norm_mem, wkv3, name="kv_proj", tm=tkv, tn=wkv3.shape[2], out_dtype=BF16,
                          out_slabs=2)
    o_att = attn_fwd(q, kv3, seqs=seqs, seq_len=seq_len, n_mem=n_mem, tm=tm_att)
    h2, n3 = proj_res_norm(o_att, full_xw_o, h1, norm_mlp, name="attn_out_proj", tm=tm_mid, tn=tn)
    (wup3, wdn3), = split_wait(ag_started[2:3], h2, name="weights_gather_wait_mlp", scatter=False)
    full_w_down = wdn3.reshape(-1, D)
    tn_up = wup3.shape[2]
    a_pre, aa = proj_plain(n3, wup3, name="up_proj", tm=tm_mid, tn=tn_up, out_dtype=F32, relu2_out=True)
    dh3, dh3b, sq_err, dg_final = proj_res_loss(aa, full_w_down, h2, g_final, tgt2, name="down_proj_loss",
                                                tm=tm_mid, tn=tn)
    loss = lax.psum(0.5 * jnp.sum(sq_err) / D, ("x", "y", "c"))

    def send(parts, name):
        srcs = [p.reshape((N_DEV, -1, p.shape[-1])) for p in parts]
        lands = [lax.empty(s.shape, BF16) for s in srcs]
        started, token = split_start([(srcs, lands)], name=name, scatter=True)
        return started[0], token

    gw_down = wgrad(aa, dh3b, name="down_proj_wgrad", tt=tm_mid, tn=tn)
    sent_down, tok = send([gw_down], "grads_send_down")
    dap = back_plain(dh3b, full_w_down, name="down_proj_bwd", tm=tm_mid, tn=tn, out_dtype=BF16, relu2_of=a_pre,
                     after=tok)
    gw_up = wgrad(n3, dap, name="up_proj_wgrad", tt=tm_mid, tn=tn_up, out_slabs=N_DEV)
    sent_up, tok = send([gw_up], "grads_send_up")
    dh2, dh2b, dg_mlp = back_norm(dap, wup3, h2, norm_mlp, dh3, name="up_proj_bwd", tm=tm_mid, tk=tn_up, after=tok)
    do_att = back_plain(dh2b, full_xw_o, name="attn_out_proj_bwd", tm=tm_mid, tn=tn, out_dtype=BF16)
    gxw_o = wgrad(o_att, dh2b, name="attn_out_proj_wgrad", tt=tm_mid, tn=tn)
    dq, dkv3 = attn_bwd(q, kv3, do_att, seqs=seqs, seq_len=seq_len, n_mem=n_mem, tm=tm_att)
    gxw_q = wgrad(n2, dq, name="q_proj_wgrad", tt=tm_mid, tn=tn)
    gxw_kv = wgrad(memn, dkv3, name="kv_proj_wgrad", tt=tkv, tn=wkv3.shape[2], out_slabs=N_DEV)
    sent_attn, tok = send([gxw_o, gxw_q, gxw_kv], "grads_send_attn")
    dg_mem = back_norm(dkv3, wkv3, mem2, norm_mem, None, name="kv_proj_bwd", tm=tkv, tk=wkv3.shape[2])
    dh1, dh1b, dg_xq = back_norm(dq, full_xw_q, h1, norm_xq, dh2, name="q_proj_bwd", tm=tm_mid, tk=D, after=tok)
    gw_out = wgrad(y2, dh1b, name="out_proj_wgrad", tt=tm_mid, tn=tn)
    sent_out, tok = send([gw_out], "grads_send_out")
    dy2 = back_plain(dh1b, full_w_out, name="out_proj_bwd", tm=tm_mid, tn=tn, out_dtype=F32, out_slabs=2, after=tok)
    du5, dpw, dsc, dlb, dgn = mixer_bwd(u5, dy2, o_pre, st_prev, pool_w_bf, scale4, theta4, gn4,
                                        seqs=seqs, seq_len=seq_len, tm=tm_mix)
    gw_in = wgrad(n1, du5, name="in_proj_wgrad", tt=tm_mid, tn=tn)
    gw_in_slots = gw_in.reshape(D, N_DEV, -1).transpose(1, 0, 2)
    sent_in, tok = send([gw_in_slots], "grads_send_in")
    dx, _, dg_mix = back_norm(du5, full_w_in, x2, norm_mix, dh1, name="in_proj_bwd", tm=tm_mid, tk=tn, after=tok)

    dlb_row = dlb.reshape(1, 4 * W)
    buf_vec = _pad_rows(jnp.concatenate([dg_mix, dg_xq, dg_mem, dg_mlp, dg_final], axis=0), 8)
    buf_half = _pad_rows(jnp.concatenate([dsc.reshape(1, 4 * W), dgn.reshape(1, 4 * W), dlb_row, -dlb_row], axis=0), 8)
    small_parts = allgather_small([dpw.reshape(4 * W, W), buf_vec, buf_half])

    sent = [sent_down, sent_up, sent_attn, sent_out, sent_in]
    recv = split_wait(sent, small_parts[1], name="grads_wait", scatter=True)
    own = dict(w_down=sent_down[2][0], w_up=sent_up[2][0], xw_o=sent_attn[2][0], xw_q=sent_attn[2][1],
               xw_kv=sent_attn[2][2], w_out=sent_out[2][0], w_in=sent_in[2][0])
    got = dict(w_down=recv[0][0], w_up=recv[1][0], xw_o=recv[2][0], xw_q=recv[2][1], xw_kv=recv[2][2],
               w_out=recv[3][0], w_in=recv[4][0])
    res = {}
    for n in BIG:
        shp = w[n].shape
        r = adamw_sharded(me1, own[n], got[n], w[n][0], mom[n][0], var[n][0], name="adamw_" + n,
                          tr=min(256, shp[1]))
        for kind, a in zip("gdmv", r):
            res[kind, n] = a.reshape(shp)
    r = adamw_replicated(small_parts, [w[n].reshape(v2) for n, v2, _, _ in SMALL],
                         [mom[n].reshape(v2) for n, v2, _, _ in SMALL],
                         [var[n].reshape(v2) for n, v2, _, _ in SMALL],
                         [(b, r0, v2[0]) for _, v2, b, r0 in SMALL])
    for kind, arrs in zip("gdmv", r):
        for (n, _, _, _), a in zip(SMALL, arrs):
            res[kind, n] = a.reshape(w[n].shape)

    out = [loss, dx.reshape(x.shape)]
    for kind in "gdmv":
        out += [res[kind, n] for n in WEIGHTS]
    return tuple(out)
```

```python
import jax
import jax.numpy as jnp
from jax import lax
from jax.experimental import pallas as pl
from jax.experimental.pallas import tpu as pltpu

F32 = jnp.float32
BF16 = jnp.bfloat16
EPS = 1e-6
CHUNK = 64
POOL_HALO = 16
HEAD_W = 128
XATTN_HEADS = 4
N_DEV = 8
N_PEERS = N_DEV - 1
ADAM_LR = 0.001
ADAM_B1 = 0.9
ADAM_B2 = 0.999
ADAM_EPS = 1e-08
ADAM_WD = 0.01
ADAM_STEP = 10
V7X_VMEM_LIMIT = 52 * 1024 * 1024
MESH = pl.DeviceIdType.MESH
HBM = pl.BlockSpec(memory_space=pltpu.HBM)
SEM = pl.BlockSpec(memory_space=pltpu.SEMAPHORE)


def _cparams(dims):
    return pltpu.CompilerParams(dimension_semantics=dims, vmem_limit_bytes=V7X_VMEM_LIMIT)


def _sigmoid(v):
    return 1.0 / (1.0 + jnp.exp(-v))


def _dot(a, b):
    return jnp.dot(a, b, preferred_element_type=F32)


def _dot_nt(a, b):
    return lax.dot_general(a, b, (((1,), (1,)), ((), ())), preferred_element_type=F32)


def _dot_tn(a, b):
    return lax.dot_general(a, b, (((0,), (0,)), ((), ())), preferred_element_type=F32)


def _split3(v):
    hi = v.astype(BF16)
    r1 = v - hi.astype(F32)
    mid = r1.astype(BF16)
    lo = (r1 - mid.astype(F32)).astype(BF16)
    return hi, mid, lo


def _tri_apply(tri, v):
    hi, mid, lo = _split3(v)
    return _dot(tri, hi) + _dot(tri, mid) + _dot(tri, lo)


def _mat_shape(a):
    return a.shape if a.ndim == 2 else (a.shape[1], a.shape[0] * a.shape[2])


def _tile_spec(a, rows, cols, row_of, col_of):
    if a.ndim == 2:
        return pl.BlockSpec((rows, cols), lambda *g: (row_of(*g), col_of(*g)))
    per = a.shape[2] // cols
    return pl.BlockSpec((None, rows, cols), lambda *g: (col_of(*g) // per, row_of(*g), col_of(*g) % per))


def _out_struct(rows, n, slabs, dtype):
    return jax.ShapeDtypeStruct((rows, n) if slabs is None else (slabs, rows, n // slabs), dtype)


def norm_mm(h, g, w, *, name, tm, tn, out_dtype, out_slabs=None):
    T, D = h.shape
    N = _mat_shape(w)[1]
    o_shape = _out_struct(T, N, out_slabs, out_dtype)

    def body(h_ref, g_ref, w_ref, o_ref, n_ref):
        @pl.when(pl.program_id(1) == 0)
        def _():
            x = h_ref[...]
            r = lax.rsqrt(jnp.mean(x * x, axis=-1, keepdims=True) + EPS)
            n_ref[...] = (x * r * g_ref[...]).astype(BF16)

        o_ref[...] = _dot(n_ref[...], w_ref[...]).astype(o_ref.dtype)

    return pl.pallas_call(
        body, name=name, grid=(T // tm, N // tn),
        in_specs=[pl.BlockSpec((tm, D), lambda i, j: (i, 0)),
                  pl.BlockSpec((1, D), lambda i, j: (0, 0)),
                  _tile_spec(w, D, tn, lambda i, j: 0, lambda i, j: j)],
        out_specs=[_tile_spec(o_shape, tm, tn, lambda i, j: i, lambda i, j: j),
                   pl.BlockSpec((tm, D), lambda i, j: (i, 0))],
        out_shape=[o_shape, jax.ShapeDtypeStruct((T, D), BF16)],
        compiler_params=_cparams(("parallel", "arbitrary")),
    )(h, g, w)


def mm_nn(a, w, res, *, name, tm, tn, tk, relu2=False):
    T, K = _mat_shape(a)
    N = w.shape[1]
    nk = K // tk

    def body(a_ref, w_ref, r_ref, o_ref, acc_ref):
        k = pl.program_id(2)
        av = a_ref[...]
        if relu2:
            av = jnp.maximum(av, 0.0)
            av = av * av
        part = _dot(av.astype(BF16), w_ref[...])

        @pl.when(k == 0)
        def _():
            acc_ref[...] = part

        @pl.when(k > 0)
        def _():
            acc_ref[...] += part

        @pl.when(k == nk - 1)
        def _():
            o_ref[...] = r_ref[...] + acc_ref[...]

    return pl.pallas_call(
        body, name=name, grid=(T // tm, N // tn, nk),
        in_specs=[_tile_spec(a, tm, tk, lambda i, j, k: i, lambda i, j, k: k),
                  pl.BlockSpec((tk, tn), lambda i, j, k: (k, j)),
                  pl.BlockSpec((tm, tn), lambda i, j, k: (i, j))],
        out_specs=pl.BlockSpec((tm, tn), lambda i, j, k: (i, j)),
        out_shape=jax.ShapeDtypeStruct((T, N), F32),
        scratch_shapes=[pltpu.VMEM((tm, tn), F32)],
        compiler_params=_cparams(("parallel", "parallel", "arbitrary")),
    )(a, w, res)


def mm_nt(a, w, *, name, tm, tn, tk, out_dtype, out_slabs=None, relu2_of=None, after=None):
    T, K = _mat_shape(a)
    nk = K // tk
    N = w.shape[0]
    has_z = relu2_of is not None
    o_shape = _out_struct(T, N, out_slabs, out_dtype)

    def body(*refs):
        a_ref, w_ref = refs[0], refs[1]
        z_ref = refs[2] if has_z else None
        o_ref, acc_ref = refs[-2], refs[-1]
        k = pl.program_id(2)
        part = _dot_nt(a_ref[...].astype(BF16), w_ref[...])

        @pl.when(k == 0)
        def _():
            acc_ref[...] = part

        @pl.when(k > 0)
        def _():
            acc_ref[...] += part

        @pl.when(k == nk - 1)
        def _():
            out = acc_ref[...]
            if has_z:
                out = out * (2.0 * jnp.maximum(z_ref[...], 0.0))
            o_ref[...] = out.astype(o_ref.dtype)

    in_specs = [_tile_spec(a, tm, tk, lambda i, j, k: i, lambda i, j, k: k),
                pl.BlockSpec((tn, tk), lambda i, j, k: (j, k))]
    args = [a, w]
    if has_z:
        in_specs.append(pl.BlockSpec((tm, tn), lambda i, j, k: (i, j)))
        args.append(relu2_of)
    if after is not None:
        in_specs.append(pl.BlockSpec(after.shape, lambda i, j, k: (0, 0)))
        args.append(after)
    return pl.pallas_call(
        body, name=name, grid=(T // tm, N // tn, nk),
        in_specs=in_specs,
        out_specs=_tile_spec(o_shape, tm, tn, lambda i, j, k: i, lambda i, j, k: j),
        out_shape=o_shape,
        scratch_shapes=[pltpu.VMEM((tm, tn), F32)],
        compiler_params=_cparams(("parallel", "parallel", "arbitrary")),
    )(*args)


def mm_nt_normbwd(a, w, h, g, dres, *, name, tm, tk, after=None):
    T, K = _mat_shape(a)
    nk = K // tk
    D = h.shape[1]
    with_dh = dres is not None

    def body(*refs):
        a_ref, w_ref, h_ref, g_ref = refs[:4]
        if with_dh:
            r_ref = refs[4]
            dh_ref, dhb_ref, dg_ref, acc_ref = refs[-4:]
        else:
            dg_ref, acc_ref = refs[-2:]
        i = pl.program_id(0)
        k = pl.program_id(1)
        part = _dot_nt(a_ref[...].astype(BF16), w_ref[...])

        @pl.when(k == 0)
        def _():
            acc_ref[...] = part

        @pl.when(k > 0)
        def _():
            acc_ref[...] += part

        @pl.when(k == nk - 1)
        def _():
            dn = acc_ref[...]
            x = h_ref[...]
            r = lax.rsqrt(jnp.mean(x * x, axis=-1, keepdims=True) + EPS)
            xr = x * r
            dgp = jnp.sum(dn * xr, axis=0, keepdims=True)

            @pl.when(i == 0)
            def _():
                dg_ref[...] = dgp

            @pl.when(i > 0)
            def _():
                dg_ref[...] += dgp

            if with_dh:
                dyg = dn * g_ref[...]
                dx = r * (dyg - xr * jnp.mean(dyg * xr, axis=-1, keepdims=True))
                out = r_ref[...] + dx
                dh_ref[...] = out
                dhb_ref[...] = out.astype(BF16)

    row = pl.BlockSpec((tm, D), lambda i, k: (i, 0))
    vec = pl.BlockSpec((1, D), lambda i, k: (0, 0))
    in_specs = [_tile_spec(a, tm, tk, lambda i, k: i, lambda i, k: k),
                _tile_spec(w, D, tk, lambda i, k: 0, lambda i, k: k), row, vec]
    args = [a, w, h, g]
    if with_dh:
        in_specs.append(row)
        args.append(dres)
        out_specs = [row, row, vec]
        out_shape = [jax.ShapeDtypeStruct((T, D), F32), jax.ShapeDtypeStruct((T, D), BF16),
                     jax.ShapeDtypeStruct((1, D), F32)]
    else:
        out_specs = vec
        out_shape = jax.ShapeDtypeStruct((1, D), F32)
    if after is not None:
        in_specs.append(pl.BlockSpec(after.shape, lambda i, k: (0, 0)))
        args.append(after)
    return pl.pallas_call(
        body, name=name, grid=(T // tm, nk),
        in_specs=in_specs, out_specs=out_specs, out_shape=out_shape,
        scratch_shapes=[pltpu.VMEM((tm, D), F32)],
        compiler_params=_cparams(("arbitrary", "arbitrary")),
    )(*args)


def mm_tn(a, b, *, name, tt, tko, tn, relu2=False, out_slabs=None):
    T, K = _mat_shape(a)
    N = _mat_shape(b)[1]
    nt = T // tt
    o_shape = _out_struct(K, N, out_slabs, BF16)

    def body(a_ref, b_ref, o_ref, acc_ref):
        t = pl.program_id(2)
        av = a_ref[...]
        if relu2:
            av = jnp.maximum(av, 0.0)
            av = av * av
        part = _dot_tn(av.astype(BF16), b_ref[...].astype(BF16))

        @pl.when(t == 0)
        def _():
            acc_ref[...] = part

        @pl.when(t > 0)
        def _():
            acc_ref[...] += part

        @pl.when(t == nt - 1)
        def _():
            o_ref[...] = acc_ref[...].astype(BF16)

    return pl.pallas_call(
        body, name=name, grid=(K // tko, N // tn, nt),
        in_specs=[_tile_spec(a, tt, tko, lambda kk, j, t: t, lambda kk, j, t: kk),
                  _tile_spec(b, tt, tn, lambda kk, j, t: t, lambda kk, j, t: j)],
        out_specs=_tile_spec(o_shape, tko, tn, lambda kk, j, t: kk, lambda kk, j, t: j),
        out_shape=o_shape,
        scratch_shapes=[pltpu.VMEM((tko, tn), F32)],
        compiler_params=_cparams(("parallel", "parallel", "arbitrary")),
    )(a, b)


def _resident(a):
    nd = a.ndim
    return pl.BlockSpec(a.shape, lambda i: (0,) * nd, pipeline_mode=pl.Buffered(1))


def _row_block(a, tm):
    if a.ndim == 2:
        return pl.BlockSpec((tm, a.shape[1]), lambda i: (i, 0))
    return pl.BlockSpec((a.shape[0], tm, a.shape[2]), lambda i: (0, i, 0))


def _cols(ref, c, width):
    if len(ref.shape) == 2:
        return ref[:, c * width:(c + 1) * width]
    per = ref.shape[2] // width
    if per == 1:
        return ref[c]
    return ref[c // per, :, (c % per) * width:(c % per + 1) * width]


def _set_cols(ref, c, width, val):
    if len(ref.shape) == 2:
        ref[:, c * width:(c + 1) * width] = val
        return
    per = ref.shape[2] // width
    if per == 1:
        ref[c] = val
    else:
        ref[c // per, :, (c % per) * width:(c % per + 1) * width] = val


def _all_cols(ref):
    if len(ref.shape) == 2:
        return ref[...]
    return jnp.concatenate([ref[s] for s in range(ref.shape[0])], axis=1)


def _rms(x):
    return lax.rsqrt(jnp.mean(x * x, axis=-1, keepdims=True) + EPS)


def _row_params():
    return _cparams(("arbitrary",))


def proj_norm(h, g, w, *, name, tm, tn, out_dtype, out_slabs=None):
    T, D = h.shape
    N = _mat_shape(w)[1]
    o_shape = _out_struct(T, N, out_slabs, out_dtype)

    def body(h_ref, g_ref, w_ref, o_ref, n_ref):
        x = h_ref[...]
        n = (x * _rms(x) * g_ref[...]).astype(BF16)
        n_ref[...] = n
        for c in range(N // tn):
            _set_cols(o_ref, c, tn, _dot(n, _cols(w_ref, c, tn)).astype(out_dtype))

    return pl.pallas_call(
        body, name=name, grid=(T // tm,),
        in_specs=[_row_block(h, tm), pl.BlockSpec((1, D), lambda i: (0, 0)), _resident(w)],
        out_specs=[_row_block(o_shape, tm), pl.BlockSpec((tm, D), lambda i: (i, 0))],
        out_shape=[o_shape, jax.ShapeDtypeStruct((T, D), BF16)],
        compiler_params=_row_params(),
    )(h, g, w)


def proj_plain(a, w, *, name, tm, tn, relu2=False):
    T = a.shape[0]
    N = _mat_shape(w)[1]

    def body(a_ref, w_ref, o_ref):
        av = a_ref[...]
        for c in range(N // tn):
            z = _dot(av, _cols(w_ref, c, tn))
            if relu2:
                z = jnp.maximum(z, 0.0)
                z = z * z
            _set_cols(o_ref, c, tn, z.astype(BF16))

    o_shape = jax.ShapeDtypeStruct((T, N), BF16)
    return pl.pallas_call(
        body, name=name, grid=(T // tm,),
        in_specs=[_row_block(a, tm), _resident(w)],
        out_specs=_row_block(o_shape, tm), out_shape=o_shape,
        compiler_params=_row_params(),
    )(a, w)


def proj_res_norm(a, w, res, g, *, name, tm, tn):
    T = res.shape[0]
    D = w.shape[1]

    def body(a_ref, w_ref, r_ref, g_ref, h_ref, n_ref):
        av = _all_cols(a_ref)
        for c in range(D // tn):
            sl = slice(c * tn, (c + 1) * tn)
            h_ref[:, sl] = r_ref[:, sl] + _dot(av, w_ref[:, sl])
        hv = h_ref[...]
        n_ref[...] = (hv * _rms(hv) * g_ref[...]).astype(BF16)

    row = pl.BlockSpec((tm, D), lambda i: (i, 0))
    return pl.pallas_call(
        body, name=name, grid=(T // tm,),
        in_specs=[_row_block(a, tm), _resident(w), row, pl.BlockSpec((1, D), lambda i: (0, 0))],
        out_specs=[row, row],
        out_shape=[jax.ShapeDtypeStruct((T, D), F32), jax.ShapeDtypeStruct((T, D), BF16)],
        compiler_params=_row_params(),
    )(a, w, res, g)


def proj_res_loss(a, w, res, g, target, *, name, tm, tn):
    T = res.shape[0]
    D = w.shape[1]

    def body(a_ref, w_ref, r_ref, g_ref, t_ref, dh_ref, dhb_ref, ls_ref, dg_ref):
        i = pl.program_id(0)
        av = a_ref[...]
        for c in range(D // tn):
            sl = slice(c * tn, (c + 1) * tn)
            dh_ref[:, sl] = r_ref[:, sl] + _dot(av, w_ref[:, sl])
        x = dh_ref[...]
        gv = g_ref[...]
        r = _rms(x)
        xr = x * r
        d = xr * gv - t_ref[...]
        dy = d * (1.0 / D)
        dyg = dy * gv
        dx = r * (dyg - xr * jnp.mean(dyg * xr, axis=-1, keepdims=True))
        dh_ref[...] = dx
        dhb_ref[...] = dx.astype(BF16)
        ls = jnp.sum(d * d, axis=0, keepdims=True)
        dg = jnp.sum(dy * xr, axis=0, keepdims=True)

        @pl.when(i == 0)
        def _():
            ls_ref[...] = ls
            dg_ref[...] = dg

        @pl.when(i > 0)
        def _():
            ls_ref[...] += ls
            dg_ref[...] += dg

    row = pl.BlockSpec((tm, D), lambda i: (i, 0))
    vec = pl.BlockSpec((1, D), lambda i: (0, 0))
    return pl.pallas_call(
        body, name=name, grid=(T // tm,),
        in_specs=[_row_block(a, tm), _resident(w), row, vec, row],
        out_specs=[row, row, vec, vec],
        out_shape=[jax.ShapeDtypeStruct((T, D), F32), jax.ShapeDtypeStruct((T, D), BF16),
                   jax.ShapeDtypeStruct((1, D), F32), jax.ShapeDtypeStruct((1, D), F32)],
        compiler_params=_row_params(),
    )(a, w, res, g, target)


def _anchor_spec(after):
    return pl.BlockSpec(after.shape, lambda i: (0, 0))


def back_plain(a, w, *, name, tm, tn, out_dtype, out_slabs=None, relu2_value=None, after=None):
    T = a.shape[0]
    N = w.shape[0]
    has_z = relu2_value is not None
    o_shape = _out_struct(T, N, out_slabs, out_dtype)

    def body(*refs):
        a_ref, w_ref = refs[0], refs[1]
        o_ref = refs[-1]
        av = a_ref[...]
        for c in range(N // tn):
            out = _dot_nt(av, w_ref[c * tn:(c + 1) * tn, :])
            if has_z:
                out = out * (2.0 * jnp.sqrt(refs[2][:, c * tn:(c + 1) * tn]).astype(F32))
            _set_cols(o_ref, c, tn, out.astype(out_dtype))

    in_specs, args = [_row_block(a, tm), _resident(w)], [a, w]
    if has_z:
        in_specs.append(_row_block(relu2_value, tm))
        args.append(relu2_value)
    if after is not None:
        in_specs.append(_anchor_spec(after))
        args.append(after)
    return pl.pallas_call(
        body, name=name, grid=(T // tm,),
        in_specs=in_specs, out_specs=_row_block(o_shape, tm), out_shape=o_shape,
        compiler_params=_row_params(),
    )(*args)


def back_norm(a, w, h, g, dres, *, name, tm, tk, bf16_copy=True, after=None):
    T, K = _mat_shape(a)
    D = h.shape[1]
    with_dh = dres is not None

    def body(*refs):
        a_ref, w_ref, h_ref, g_ref = refs[:4]
        i = pl.program_id(0)
        dn = None
        for kc in range(K // tk):
            part = _dot_nt(_cols(a_ref, kc, tk).astype(BF16), _cols(w_ref, kc, tk))
            dn = part if dn is None else dn + part
        x = h_ref[...]
        r = _rms(x)
        xr = x * r
        dgp = jnp.sum(dn * xr, axis=0, keepdims=True)
        dg_ref = refs[-1]

        @pl.when(i == 0)
        def _():
            dg_ref[...] = dgp

        @pl.when(i > 0)
        def _():
            dg_ref[...] += dgp

        if with_dh:
            dyg = dn * g_ref[...]
            out = refs[4][...] + r * (dyg - xr * jnp.mean(dyg * xr, axis=-1, keepdims=True))
            if bf16_copy:
                refs[-3][...] = out
                refs[-2][...] = out.astype(BF16)
            else:
                refs[-2][...] = out

    row = pl.BlockSpec((tm, D), lambda i: (i, 0))
    vec = pl.BlockSpec((1, D), lambda i: (0, 0))
    in_specs, args = [_row_block(a, tm), _resident(w), row, vec], [a, w, h, g]
    if with_dh:
        in_specs.append(row)
        args.append(dres)
        out_specs = [row, row, vec] if bf16_copy else [row, vec]
        out_shape = [jax.ShapeDtypeStruct((T, D), F32)]
        if bf16_copy:
            out_shape.append(jax.ShapeDtypeStruct((T, D), BF16))
        out_shape.append(jax.ShapeDtypeStruct((1, D), F32))
    else:
        out_specs = vec
        out_shape = jax.ShapeDtypeStruct((1, D), F32)
    if after is not None:
        in_specs.append(_anchor_spec(after))
        args.append(after)
    return pl.pallas_call(
        body, name=name, grid=(T // tm,),
        in_specs=in_specs, out_specs=out_specs, out_shape=out_shape,
        compiler_params=_row_params(),
    )(*args)


def wgrad(a, b, *, name, tt, tn, out_slabs=None):
    T, K = _mat_shape(a)
    N = _mat_shape(b)[1]
    nt = T // tt
    o_shape = _out_struct(K, N, out_slabs, BF16)

    def body(a_ref, b_ref, o_ref, acc_ref):
        t = pl.program_id(0)
        at = _all_cols(a_ref).astype(BF16).T

        @pl.when(t == 0)
        def _():
            acc_ref[...] = jnp.zeros_like(acc_ref)

        for c in range(N // tn):
            acc_ref[:, c * tn:(c + 1) * tn] += _dot(at, _cols(b_ref, c, tn).astype(BF16))

        @pl.when(t == nt - 1)
        def _():
            for c in range(N // tn):
                _set_cols(o_ref, c, tn, acc_ref[:, c * tn:(c + 1) * tn].astype(BF16))

    return pl.pallas_call(
        body, name=name, grid=(nt,),
        in_specs=[_row_block(a, tt), _row_block(b, tt)],
        out_specs=_resident(o_shape), out_shape=o_shape,
        scratch_shapes=[pltpu.VMEM((K, N), F32)],
        compiler_params=_row_params(),
    )(a, b)


def _chunk_tri(tm, upper):
    r = lax.broadcasted_iota(jnp.int32, (tm, tm), 0)
    c = lax.broadcasted_iota(jnp.int32, (tm, tm), 1)
    same = (r // CHUNK) == (c // CHUNK)
    keep = (c >= r) if upper else (c <= r)
    return jnp.where(same & keep, 1.0, 0.0).astype(BF16)


def _chunk_mask(tm, upper):
    r = lax.broadcasted_iota(jnp.int32, (tm, tm), 0)
    c = lax.broadcasted_iota(jnp.int32, (tm, tm), 1)
    return ((r // CHUNK) == (c // CHUNK)) & ((c >= r) if upper else (c <= r))


def _chunk_row(v, r, nc):
    return jnp.concatenate([jnp.broadcast_to(v[c * CHUNK + r:c * CHUNK + r + 1], (CHUNK, v.shape[1]))
                            for c in range(nc)], axis=0)


def _block_diag(v, nc):
    chunk = lax.broadcasted_iota(jnp.int32, (v.shape[0], 1), 0) // CHUNK
    return jnp.concatenate([jnp.where(chunk == c, v, jnp.zeros_like(v)) for c in range(nc)], axis=1)


def _pool_windows_back(ext_ref, tm):
    n = tm + 32
    ext_ref[1, 8:n] = ext_ref[0, 8:n] + ext_ref[0, 7:n - 1]
    ext_ref[2, 16:n] = ext_ref[1, 16:n] + ext_ref[1, 14:n - 2]
    ext_ref[3, 24:n] = ext_ref[2, 24:n] + ext_ref[2, 20:n - 4]
    s2 = ext_ref[1, 32:n]
    s4 = ext_ref[2, 32:n]
    s8 = ext_ref[3, 32:n]
    s16 = s8 + ext_ref[3, 24:n - 8]
    return s2, s4, s8, s16


def _pool_windows_fwd(ext_ref, tm):
    n = tm + 32
    ext_ref[1, 0:n - 8] = ext_ref[0, 0:n - 8] + ext_ref[0, 1:n - 7]
    ext_ref[2, 0:n - 16] = ext_ref[1, 0:n - 16] + ext_ref[1, 2:n - 14]
    ext_ref[3, 0:n - 24] = ext_ref[2, 0:n - 24] + ext_ref[2, 4:n - 20]
    s2 = ext_ref[1, 0:tm]
    s4 = ext_ref[2, 0:tm]
    s8 = ext_ref[3, 0:tm]
    s16 = s8 + ext_ref[3, 8:tm + 8]
    return s2, s4, s8, s16


def _select_window(g, s2, s4, s8, s16):
    return jnp.where(g == 0, s2, jnp.where(g == 1, s4, jnp.where(g == 2, s8, s16)))


def _pool_count(g, pos):
    width = lax.shift_left(jnp.int32(2), g)
    return jnp.minimum(pos + 1, width).astype(F32)


def _hgrn_gates(zq, zf, th):
    lb = _sigmoid(th[0:1, :] - th[1:2, :])
    sig = _sigmoid(zf)
    f = lb + (1.0 - lb) * sig
    sq = _sigmoid(zq)
    return lb, sig, f, sq


def mixer_fwd(u5, pool_w_bf, scale4, theta4, gn4, *, seqs, seq_len, tm):
    T = u5.shape[1]
    tps = seq_len // tm
    nc = tm // CHUNK
    W = HEAD_W

    def body(u_ref, pw_ref, sc_ref, th_ref, gn_ref, y_ref, o_ref, st_ref, halo_ref, ext_ref, s_ref):
        g = pl.program_id(0)
        i = pl.program_id(2)

        @pl.when(i == 0)
        def _():
            halo_ref[...] = jnp.zeros_like(halo_ref)
            s_ref[...] = jnp.zeros_like(s_ref)

        row = lax.broadcasted_iota(jnp.int32, (tm, 1), 0)

        up = u_ref[0]
        ext_ref[0, 0:16] = jnp.zeros((16, W), F32)
        ext_ref[0, 16:32] = halo_ref[...]
        ext_ref[0, 32:32 + tm] = up
        win = _select_window(g, *_pool_windows_back(ext_ref, tm))
        p = win / _pool_count(g, i * tm + row) - up
        halo_ref[...] = up[tm - POOL_HALO:tm]
        y_ref[0] = (_dot(p.astype(BF16), pw_ref[...]) * sc_ref[...]).astype(BF16)

        zq, zf, zi, zg = u_ref[1], u_ref[2], u_ref[3], u_ref[4]
        lb, sig, f, sq = _hgrn_gates(zq, zf, th_ref[...])
        logf = jnp.log(f)
        kk = 1.0 - f
        q = zq * sq
        G = _tri_apply(_chunk_tri(tm, False), logf)
        Gm, Gl = _chunk_row(G, CHUNK // 2 - 1, nc), _chunk_row(G, CHUNK - 1, nc)
        vb = zi.astype(BF16)
        qrb = (q * jnp.exp(G - Gm)).astype(BF16)
        krb = (kk * jnp.exp(Gm - G)).astype(BF16)
        keb = (kk * jnp.exp(Gl - G)).astype(BF16)
        qgb = (q * jnp.exp(G)).astype(BF16)
        a = jnp.where(_chunk_mask(tm, False), _dot_nt(qrb, krb), 0.0).astype(BF16)
        d_st = _dot_tn(vb, _block_diag(keb, nc))
        o_intra = _dot(a, vb)
        st = s_ref[...]
        states = []
        for c in range(nc):
            st_ref[c] = st
            states.append(st.astype(BF16))
            st = st * jnp.exp(G[(c + 1) * CHUNK - 1:(c + 1) * CHUNK]) + d_st[:, c * W:(c + 1) * W]
        s_ref[...] = st
        o = o_intra + _dot_nt(_block_diag(qgb, nc), jnp.concatenate(states, axis=1))
        o_ref[...] = o
        r = lax.rsqrt(jnp.mean(o * o, axis=-1, keepdims=True) + EPS)
        y_ref[1] = (o * r * gn_ref[...] * (zg * _sigmoid(zg))).astype(BF16)

    def rb(s, i):
        return s * tps + i

    return pl.pallas_call(
        body, name="mixer_fwd", grid=(4, seqs, tps),
        in_specs=[pl.BlockSpec((5, tm, W), lambda g, s, i: (0, rb(s, i), g)),
                  pl.BlockSpec((None, W, W), lambda g, s, i: (g, 0, 0)),
                  pl.BlockSpec((None, 1, W), lambda g, s, i: (g, 0, 0)),
                  pl.BlockSpec((None, 2, W), lambda g, s, i: (g, 0, 0)),
                  pl.BlockSpec((None, 1, W), lambda g, s, i: (g, 0, 0))],
        out_specs=[pl.BlockSpec((2, tm, W), lambda g, s, i: (0, rb(s, i), g)),
                   pl.BlockSpec((tm, W), lambda g, s, i: (rb(s, i), g)),
                   pl.BlockSpec((nc, None, W, W), lambda g, s, i: (rb(s, i), g, 0, 0))],
        out_shape=[jax.ShapeDtypeStruct((2, T, 4 * W), BF16),
                   jax.ShapeDtypeStruct((T, 4 * W), F32),
                   jax.ShapeDtypeStruct((T // CHUNK, 4, W, W), F32)],
        scratch_shapes=[pltpu.VMEM((POOL_HALO, W), F32),
                        pltpu.VMEM((4, tm + 32, W), F32),
                        pltpu.VMEM((W, W), F32)],
        compiler_params=_cparams(("arbitrary", "arbitrary", "arbitrary")),
    )(u5, pool_w_bf, scale4, theta4, gn4)


def mixer_bwd(u5, dy2, o_pre, st_prev, pool_w_bf, scale4, theta4, gn4, *, seqs, seq_len, tm):
    T = u5.shape[1]
    tps = seq_len // tm
    nc = tm // CHUNK
    W = HEAD_W
    hb = tm // POOL_HALO

    def body(u_ref, uh_ref, dy_ref, o_ref, st_ref, pw_ref, sc_ref, th_ref, gn_ref,
             du_ref, dpw_ref, dsc_ref, dlb_ref, dgn_ref, nxt_ref, ext_ref, ds_ref):
        g = pl.program_id(0)
        s = pl.program_id(1)
        i = pl.program_id(2)
        tile = tps - 1 - i
        first = (s == 0) & (i == 0)

        @pl.when(i == 0)
        def _():
            nxt_ref[...] = jnp.zeros_like(nxt_ref)
            ds_ref[...] = jnp.zeros_like(ds_ref)

        row = lax.broadcasted_iota(jnp.int32, (tm, 1), 0)
        cnt = _pool_count(g, tile * tm + row)

        def accumulate(ref, val):
            @pl.when(first)
            def _():
                ref[...] = val

            @pl.when(jnp.logical_not(first))
            def _():
                ref[...] += val

        up = u_ref[0]
        ext_ref[0, 0:16] = jnp.zeros((16, W), F32)
        ext_ref[0, 16:32] = jnp.where(tile == 0, 0.0, uh_ref[...])
        ext_ref[0, 32:32 + tm] = up
        win = _select_window(g, *_pool_windows_back(ext_ref, tm))
        pb = (win / cnt - up).astype(BF16)
        dyp = dy_ref[0]
        z = _dot(pb, pw_ref[...])
        accumulate(dsc_ref, jnp.sum(dyp * z, axis=0, keepdims=True))
        dz = (dyp * sc_ref[...]).astype(BF16)
        accumulate(dpw_ref, _dot_tn(pb, dz))
        dp = _dot_nt(dz, pw_ref[...])
        e = dp / cnt
        ext_ref[0, 0:tm] = e
        ext_ref[0, tm:tm + 16] = nxt_ref[...]
        ext_ref[0, tm + 16:tm + 32] = jnp.zeros((16, W), F32)
        lead = _select_window(g, *_pool_windows_fwd(ext_ref, tm))
        nxt_ref[...] = e[0:POOL_HALO]
        du_ref[0] = (lead - dp).astype(BF16)

        zq, zf, zi, zg = u_ref[1], u_ref[2], u_ref[3], u_ref[4]
        lb, sig, f, sq = _hgrn_gates(zq, zf, th_ref[...])
        logf = jnp.log(f)
        kk = 1.0 - f
        q = zq * sq
        G = _tri_apply(_chunk_tri(tm, False), logf)

        dyh = dy_ref[1]
        o = o_ref[...]
        sg = _sigmoid(zg)
        r = lax.rsqrt(jnp.mean(o * o, axis=-1, keepdims=True) + EPS)
        orr = o * r
        gn = gn_ref[...]
        du_ref[4] = (dyh * (orr * gn) * (sg * (1.0 + zg * (1.0 - sg)))).astype(BF16)
        don = dyh * (zg * sg)
        accumulate(dgn_ref, jnp.sum(don * orr, axis=0, keepdims=True))
        dog = don * gn
        do = r * (dog - orr * jnp.mean(dog * orr, axis=-1, keepdims=True))

        Gm, Gl = _chunk_row(G, CHUNK // 2 - 1, nc), _chunk_row(G, CHUNK - 1, nc)
        e_q, e_k, e_e, e_g = jnp.exp(G - Gm), jnp.exp(Gm - G), jnp.exp(Gl - G), jnp.exp(G)
        qr, kr, ke, qg = q * e_q, kk * e_k, kk * e_e, q * e_g
        qrb, krb, keb, qgb = qr.astype(BF16), kr.astype(BF16), ke.astype(BF16), qg.astype(BF16)
        vb = zi.astype(BF16)
        dob = do.astype(BF16)
        lower, upper = _chunk_mask(tm, False), _chunk_mask(tm, True)
        da = jnp.where(lower, _dot_nt(dob, vb), 0.0).astype(BF16)
        a_t = jnp.where(upper, _dot_nt(krb, qrb), 0.0).astype(BF16)
        da_t = jnp.where(upper, _dot_nt(vb, dob), 0.0).astype(BF16)
        u_cat = _dot_tn(dob, _block_diag(qgb, nc))
        dqr = _dot(da, krb)
        dkr = _dot(da_t, qrb)
        dv = _dot(a_t, dob)
        dsn = ds_ref[...]
        dsn_b, ddecay = [None] * nc, [None] * nc
        for c in reversed(range(nc)):
            decay = jnp.exp(G[(c + 1) * CHUNK - 1:(c + 1) * CHUNK])
            dsn_b[c] = dsn.astype(BF16)
            ddecay[c] = jnp.sum(dsn * st_ref[c], axis=0, keepdims=True) * decay
            dsn = u_cat[:, c * W:(c + 1) * W] + dsn * decay
        ds_ref[...] = dsn
        st_rows = jnp.concatenate([st_ref[c].astype(BF16) for c in range(nc)], axis=0)
        dqg = _dot(_block_diag(dob, nc), st_rows)
        dke = _dot(_block_diag(vb, nc), jnp.concatenate(dsn_b, axis=0))
        dv = dv + _dot_nt(_block_diag(keb, nc), jnp.concatenate(dsn_b, axis=1))
        t_qr, t_kr, t_qg, t_ke = dqr * qr, dkr * kr, dqg * qg, dke * ke
        dq = dqr * e_q + dqg * e_g
        dk = dkr * e_k + dke * e_e
        crow = lax.broadcasted_iota(jnp.int32, (CHUNK, 1), 0)
        t_mid = t_kr - t_qr
        ends = []
        for c in range(nc):
            sl = slice(c * CHUNK, (c + 1) * CHUNK)
            dgm = jnp.sum(t_mid[sl], axis=0, keepdims=True)
            dgl = jnp.sum(t_ke[sl], axis=0, keepdims=True) + ddecay[c]
            ends.append(jnp.where(crow == CHUNK // 2 - 1, dgm, 0.0) + jnp.where(crow == CHUNK - 1, dgl, 0.0))
        dG = t_qg - t_ke - t_mid + jnp.concatenate(ends, axis=0)
        dlogf = _tri_apply(_chunk_tri(tm, True), dG)
        df = dlogf / f - dk
        du_ref[1] = (dq * (sq * (1.0 + zq * (1.0 - sq)))).astype(BF16)
        du_ref[2] = (df * (1.0 - lb) * (sig * (1.0 - sig))).astype(BF16)
        du_ref[3] = dv.astype(BF16)
        accumulate(dlb_ref, jnp.sum(df * (1.0 - sig), axis=0, keepdims=True) * (lb * (1.0 - lb)))

    def rb(s, i):
        return s * tps + (tps - 1 - i)

    vec = pl.BlockSpec((None, 1, W), lambda g, s, i: (g, 0, 0))
    mat = pl.BlockSpec((None, W, W), lambda g, s, i: (g, 0, 0))
    return pl.pallas_call(
        body, name="mixer_bwd", grid=(4, seqs, tps),
        in_specs=[pl.BlockSpec((5, tm, W), lambda g, s, i: (0, rb(s, i), g)),
                  pl.BlockSpec((None, POOL_HALO, W), lambda g, s, i: (0, jnp.maximum(rb(s, i) * hb - 1, 0), g)),
                  pl.BlockSpec((2, tm, W), lambda g, s, i: (0, rb(s, i), g)),
                  pl.BlockSpec((tm, W), lambda g, s, i: (rb(s, i), g)),
                  pl.BlockSpec((nc, None, W, W), lambda g, s, i: (rb(s, i), g, 0, 0)),
                  mat, vec,
                  pl.BlockSpec((None, 2, W), lambda g, s, i: (g, 0, 0)),
                  vec],
        out_specs=[pl.BlockSpec((5, tm, W), lambda g, s, i: (0, rb(s, i), g)), mat, vec, vec, vec],
        out_shape=[jax.ShapeDtypeStruct((5, T, 4 * W), BF16),
                   jax.ShapeDtypeStruct((4, W, W), F32),
                   jax.ShapeDtypeStruct((4, 1, W), F32),
                   jax.ShapeDtypeStruct((4, 1, W), F32),
                   jax.ShapeDtypeStruct((4, 1, W), F32)],
        scratch_shapes=[pltpu.VMEM((POOL_HALO, W), F32),
                        pltpu.VMEM((4, tm + 32, W), F32),
                        pltpu.VMEM((W, W), F32)],
        compiler_params=_cparams(("arbitrary", "arbitrary", "arbitrary")),
    )(u5, u5, dy2, o_pre, st_prev, pool_w_bf, scale4, theta4, gn4)


def _attn_probs(q, k, hd):
    s = _dot_nt(q, k) * (1.0 / (hd ** 0.5))
    e = jnp.exp(s - jnp.max(s, axis=-1, keepdims=True))
    return e / jnp.sum(e, axis=-1, keepdims=True)


def attn_fwd(q, kv3, *, seqs, seq_len, n_mem, tm):
    T, D = q.shape
    hd = D // XATTN_HEADS
    tps = seq_len // tm

    def body(q_ref, kv_ref, o_ref):
        p = _attn_probs(q_ref[...], kv_ref[0], hd)
        o_ref[...] = _dot(p.astype(BF16), kv_ref[1]).astype(BF16)

    return pl.pallas_call(
        body, name="attn_fwd", grid=(seqs, XATTN_HEADS, tps),
        in_specs=[pl.BlockSpec((tm, hd), lambda b, h, i: (b * tps + i, h)),
                  pl.BlockSpec((2, n_mem, hd), lambda b, h, i: (0, b, h))],
        out_specs=pl.BlockSpec((tm, hd), lambda b, h, i: (b * tps + i, h)),
        out_shape=jax.ShapeDtypeStruct((T, D), BF16),
        compiler_params=_cparams(("parallel", "parallel", "arbitrary")),
    )(q, kv3)


def attn_bwd(q, kv3, do, *, seqs, seq_len, n_mem, tm):
    T, D = q.shape
    hd = D // XATTN_HEADS
    tps = seq_len // tm

    def body(q_ref, kv_ref, do_ref, dq_ref, dkv_ref):
        i = pl.program_id(2)
        qv, k, v, dov = q_ref[...], kv_ref[0], kv_ref[1], do_ref[...]
        p = _attn_probs(qv, k, hd)
        dp = _dot_nt(dov, v)
        ds = (p * (dp - jnp.sum(dp * p, axis=-1, keepdims=True)) * (1.0 / (hd ** 0.5))).astype(BF16)
        dq_ref[...] = _dot(ds, k).astype(BF16)
        dk = _dot_tn(ds, qv)
        dv = _dot_tn(p.astype(BF16), dov)

        @pl.when(i == 0)
        def _():
            dkv_ref[0] = dk
            dkv_ref[1] = dv

        @pl.when(i > 0)
        def _():
            dkv_ref[0] += dk
            dkv_ref[1] += dv

    qspec = pl.BlockSpec((tm, hd), lambda b, h, i: (b * tps + i, h))
    kvspec = pl.BlockSpec((2, n_mem, hd), lambda b, h, i: (0, b, h))
    return pl.pallas_call(
        body, name="attn_bwd", grid=(seqs, XATTN_HEADS, tps),
        in_specs=[qspec, kvspec, qspec],
        out_specs=[qspec, kvspec],
        out_shape=[jax.ShapeDtypeStruct((T, D), BF16), jax.ShapeDtypeStruct((2, seqs * n_mem, D), F32)],
        compiler_params=_cparams(("parallel", "parallel", "arbitrary")),
    )(q, kv3, do)


def final_loss(h, g, target, *, tm):
    T, D = h.shape

    def body(h_ref, g_ref, t_ref, dh_ref, dhb_ref, ls_ref, dg_ref):
        i = pl.program_id(0)
        x = h_ref[...]
        gv = g_ref[...]
        r = lax.rsqrt(jnp.mean(x * x, axis=-1, keepdims=True) + EPS)
        xr = x * r
        d = xr * gv - t_ref[...]
        dy = d * (1.0 / D)
        dyg = dy * gv
        dx = r * (dyg - xr * jnp.mean(dyg * xr, axis=-1, keepdims=True))
        dh_ref[...] = dx
        dhb_ref[...] = dx.astype(BF16)
        ls = jnp.sum(d * d, axis=0, keepdims=True)
        dg = jnp.sum(dy * xr, axis=0, keepdims=True)

        @pl.when(i == 0)
        def _():
            ls_ref[...] = ls
            dg_ref[...] = dg

        @pl.when(i > 0)
        def _():
            ls_ref[...] += ls
            dg_ref[...] += dg

    row = pl.BlockSpec((tm, D), lambda i: (i, 0))
    vec = pl.BlockSpec((1, D), lambda i: (0, 0))
    return pl.pallas_call(
        body, name="final_loss", grid=(T // tm,),
        in_specs=[row, vec, row], out_specs=[row, row, vec, vec],
        out_shape=[jax.ShapeDtypeStruct((T, D), F32), jax.ShapeDtypeStruct((T, D), BF16),
                   jax.ShapeDtypeStruct((1, D), F32), jax.ShapeDtypeStruct((1, D), F32)],
        compiler_params=_cparams(("arbitrary",)),
    )(h, g, target)


def _my_place():
    return lax.axis_index("x"), lax.axis_index("y"), lax.axis_index("c")


def _slot_of(px, py, pc):
    return 4 * px + 2 * py + pc


def _peer(k, x, y, c):
    return (1 - x if (k >> 2) & 1 else x, 1 - y if (k >> 1) & 1 else y, 1 - c if k & 1 else c)


def _split_copies(src_refs, land_refs, send_sems, recv_sems, scatter):
    x, y, c = _my_place()
    mine = _slot_of(x, y, c)
    copies = []
    for a, (src, land) in enumerate(zip(src_refs, land_refs)):
        for k in range(1, N_DEV):
            peer = _peer(k, x, y, c)
            copies.append(pltpu.make_async_remote_copy(
                src_ref=src.at[_slot_of(*peer)] if scatter else src, dst_ref=land.at[mine],
                send_sem=send_sems.at[a * N_PEERS + k - 1], recv_sem=recv_sems.at[a * N_PEERS + k - 1],
                device_id=peer, device_id_type=MESH))
    return copies


def split_start(groups, *, name, scatter):
    sizes = [len(srcs) for srcs, _ in groups]
    n_arr = sum(sizes)
    flat = [a for srcs, lands in groups for a in list(srcs) + list(lands)]

    def body(*refs):
        ins = refs[:2 * n_arr]
        sems = refs[4 * n_arr:4 * n_arr + 2 * len(groups)]
        token = refs[-1]
        at = 0
        for gi, n in enumerate(sizes):
            for cp in _split_copies(ins[at:at + n], ins[at + n:at + 2 * n], sems[2 * gi], sems[2 * gi + 1], scatter):
                cp.start()
            at += 2 * n
        token[...] = jnp.zeros_like(token)

    sem_shapes = []
    for n in sizes:
        sem_shapes += [pltpu.SemaphoreType.DMA((n * N_PEERS,))] * 2
    outs = pl.pallas_call(
        body, name=name,
        out_shape=tuple(pltpu.HBM(a.shape, a.dtype) for a in flat) + tuple(sem_shapes)
        + (jax.ShapeDtypeStruct((8, 128), F32),),
        in_specs=(HBM,) * len(flat),
        out_specs=(HBM,) * len(flat) + (SEM,) * len(sem_shapes) + (pl.BlockSpec(memory_space=pltpu.VMEM),),
        input_output_aliases={i: i for i in range(len(flat))},
        compiler_params=pltpu.CompilerParams(has_side_effects=pltpu.SideEffectType.DATAFLOW_SIDE_EFFECTING),
    )(*[pltpu.with_memory_space_constraint(a, pltpu.HBM) for a in flat])
    thru, sems, token = outs[:len(flat)], outs[len(flat):-1], outs[-1]
    started, at = [], 0
    for gi, n in enumerate(sizes):
        started.append((sems[2 * gi], sems[2 * gi + 1], thru[at:at + n], thru[at + n:at + 2 * n]))
        at += 2 * n
    return started, token


def split_wait(started, after, *, name, scatter):
    sizes = [len(g[2]) for g in started]
    n_arr = sum(sizes)
    flat = [a for g in started for a in list(g[2]) + list(g[3])]
    sems = [s for g in started for s in g[:2]]

    def body(*refs):
        ins = refs[:2 * n_arr]
        sem_refs = refs[2 * n_arr:2 * n_arr + len(sems)]
        at = 0
        for gi, n in enumerate(sizes):
            for cp in _split_copies(ins[at:at + n], ins[at + n:at + 2 * n], sem_refs[2 * gi], sem_refs[2 * gi + 1], scatter):
                cp.wait_send()
                cp.wait_recv()
            at += 2 * n

    outs = pl.pallas_call(
        body, name=name,
        out_shape=tuple(pltpu.HBM(a.shape, a.dtype) for a in flat),
        in_specs=(HBM,) * len(flat) + (SEM,) * len(sems) + (pl.BlockSpec(memory_space=pl.ANY),),
        out_specs=(HBM,) * len(flat),
        input_output_aliases={i: i for i in range(len(flat))},
        compiler_params=pltpu.CompilerParams(has_side_effects=pltpu.SideEffectType.DATAFLOW_SIDE_EFFECTING),
    )(*flat, *sems, after)
    lands, at = [], 0
    for n in sizes:
        lands.append(outs[at + n:at + 2 * n])
        at += 2 * n
    return lands


def allgather_small(bufs):
    n = len(bufs)

    def body(*refs):
        srcs, outs = refs[:n], refs[n:2 * n]
        send_sems, recv_sems, local_sems = refs[2 * n:]
        x, y, c = _my_place()
        mine = _slot_of(x, y, c)
        local = [pltpu.make_async_copy(s, o.at[mine], local_sems.at[a]) for a, (s, o) in enumerate(zip(srcs, outs))]
        for cp in local:
            cp.start()
        copies = _split_copies(srcs, outs, send_sems, recv_sems, False)
        for cp in copies:
            cp.start()
        for cp in copies:
            cp.wait()
        for cp in local:
            cp.wait()

    return pl.pallas_call(
        body, name="allgather_small",
        out_shape=[jax.ShapeDtypeStruct((N_DEV,) + b.shape, b.dtype) for b in bufs],
        in_specs=[HBM] * n, out_specs=[HBM] * n,
        scratch_shapes=[pltpu.SemaphoreType.DMA((n * N_PEERS,)), pltpu.SemaphoreType.DMA((n * N_PEERS,)),
                        pltpu.SemaphoreType.DMA((n,))],
    )(*bufs)


def _adamw_math(g, w, m, v):
    c1 = 1.0 - ADAM_B1 ** ADAM_STEP
    c2 = 1.0 - ADAM_B2 ** ADAM_STEP
    nm = ADAM_B1 * m + (1.0 - ADAM_B1) * g
    nv = ADAM_B2 * v + (1.0 - ADAM_B2) * (g * g)
    delta = -ADAM_LR * ((nm / c1) / (jnp.sqrt(nv / c2) + ADAM_EPS) + ADAM_WD * w)
    return delta, nm, nv


def adamw_sharded(me, own, recv, w, m, v, *, name, tr):
    R, C = w.shape

    def body(me_ref, *refs):
        parts = refs[:N_DEV]
        w_ref, m_ref, v_ref, g_ref, d_ref, nm_ref, nv_ref = refs[N_DEV:]
        g = parts[0][...].astype(F32)
        for p in parts[1:]:
            g = g + p[...].astype(F32)
        g_ref[...] = g
        d_ref[...], nm_ref[...], nv_ref[...] = _adamw_math(g, w_ref[...], m_ref[...], v_ref[...])

    def slab(k):
        return pl.BlockSpec((None, tr, C), lambda i, me_ref: (me_ref[0] ^ k, i, 0))

    blk = pl.BlockSpec((tr, C), lambda i, me_ref: (i, 0))
    out = jax.ShapeDtypeStruct((R, C), F32)
    return pl.pallas_call(
        body, name=name,
        grid_spec=pltpu.PrefetchScalarGridSpec(
            num_scalar_prefetch=1, grid=(R // tr,),
            in_specs=[slab(k) for k in range(N_DEV)] + [blk, blk, blk],
            out_specs=[blk, blk, blk, blk]),
        out_shape=[out, out, out, out],
        compiler_params=_cparams(("parallel",)),
    )(me, own, *([recv] * N_PEERS), w, m, v)


def adamw_replicated(parts, ws, ms, vs, rows):
    n_buf, n_par = len(parts), len(ws)

    def body(*refs):
        p_refs = refs[:n_buf]
        w_refs = refs[n_buf:n_buf + n_par]
        m_refs = refs[n_buf + n_par:n_buf + 2 * n_par]
        v_refs = refs[n_buf + 2 * n_par:n_buf + 3 * n_par]
        outs = refs[n_buf + 3 * n_par:]
        sums = []
        for p in p_refs:
            g = p[0]
            for s in range(1, N_DEV):
                g = g + p[s]
            sums.append(g)
        for j, (b, r0, nr) in enumerate(rows):
            g = sums[b][r0:r0 + nr]
            delta, nm, nv = _adamw_math(g, w_refs[j][...], m_refs[j][...], v_refs[j][...])
            outs[j][...] = g
            outs[n_par + j][...] = delta
            outs[2 * n_par + j][...] = nm
            outs[3 * n_par + j][...] = nv

    shapes = [jax.ShapeDtypeStruct(w.shape, F32) for w in ws]
    outs = pl.pallas_call(
        body, name="adamw_replicated", out_shape=shapes * 4,
        compiler_params=pltpu.CompilerParams(vmem_limit_bytes=V7X_VMEM_LIMIT),
    )(*parts, *ws, *ms, *vs)
    return outs[:n_par], outs[n_par:2 * n_par], outs[2 * n_par:3 * n_par], outs[3 * n_par:]


BIG = ("w_in", "w_out", "xw_q", "xw_kv", "xw_o", "w_up", "w_down")
COL_SHARDED = ("w_in", "xw_kv", "w_up")
WEIGHTS = ("norm_mix", "w_in", "pool_w", "pool_scale", "lb_theta", "hgrn_norm", "w_out", "norm_xq",
           "norm_mem", "xw_q", "xw_kv", "xw_o", "norm_mlp", "w_up", "w_down", "norm_final")
SMALL = (("pool_w", (4 * HEAD_W, HEAD_W), 0, 0),
         ("norm_mix", (1, 1024), 1, 0), ("norm_xq", (1, 1024), 1, 1), ("norm_mem", (1, 1024), 1, 2),
         ("norm_mlp", (1, 1024), 1, 3), ("norm_final", (1, 1024), 1, 4),
         ("pool_scale", (1, 512), 2, 0), ("hgrn_norm", (1, 512), 2, 1), ("lb_theta", (2, 512), 2, 2))


def _pad_rows(a, rows):
    return jnp.concatenate([a, jnp.zeros((rows - a.shape[0], a.shape[1]), a.dtype)], axis=0)


def kernel(x, mem, norm_mix, w_in, pool_w, pool_scale, lb_theta, hgrn_norm, w_out, norm_xq, norm_mem, xw_q, xw_kv, xw_o, norm_mlp, w_up, w_down, norm_final, loss_target, m_norm_mix, m_w_in, m_pool_w, m_pool_scale, m_lb_theta, m_hgrn_norm, m_w_out, m_norm_xq, m_norm_mem, m_xw_q, m_xw_kv, m_xw_o, m_norm_mlp, m_w_up, m_w_down, m_norm_final, v_norm_mix, v_w_in, v_pool_w, v_pool_scale, v_lb_theta, v_hgrn_norm, v_w_out, v_norm_xq, v_norm_mem, v_xw_q, v_xw_kv, v_xw_o, v_norm_mlp, v_w_up, v_w_down, v_norm_final):
    w = dict(norm_mix=norm_mix, w_in=w_in, pool_w=pool_w, pool_scale=pool_scale, lb_theta=lb_theta,
             hgrn_norm=hgrn_norm, w_out=w_out, norm_xq=norm_xq, norm_mem=norm_mem, xw_q=xw_q, xw_kv=xw_kv,
             xw_o=xw_o, norm_mlp=norm_mlp, w_up=w_up, w_down=w_down, norm_final=norm_final)
    mom = dict(norm_mix=m_norm_mix, w_in=m_w_in, pool_w=m_pool_w, pool_scale=m_pool_scale, lb_theta=m_lb_theta,
               hgrn_norm=m_hgrn_norm, w_out=m_w_out, norm_xq=m_norm_xq, norm_mem=m_norm_mem, xw_q=m_xw_q,
               xw_kv=m_xw_kv, xw_o=m_xw_o, norm_mlp=m_norm_mlp, w_up=m_w_up, w_down=m_w_down,
               norm_final=m_norm_final)
    var = dict(norm_mix=v_norm_mix, w_in=v_w_in, pool_w=v_pool_w, pool_scale=v_pool_scale, lb_theta=v_lb_theta,
               hgrn_norm=v_hgrn_norm, w_out=v_w_out, norm_xq=v_norm_xq, norm_mem=v_norm_mem, xw_q=v_xw_q,
               xw_kv=v_xw_kv, xw_o=v_xw_o, norm_mlp=v_norm_mlp, w_up=v_w_up, w_down=v_w_down,
               norm_final=v_norm_final)

    seqs, seq_len, D = x.shape
    n_mem = mem.shape[1]
    T = seqs * seq_len
    W = HEAD_W
    x2 = x.reshape(T, D)
    mem2 = mem.reshape(seqs * n_mem, D)
    tgt2 = loss_target.reshape(T, D)
    tm_big = min(1024, T)
    tm_mid = min(512, T)
    tm_sq = min(1024, T)
    tm_mix = min(256, seq_len)
    tm_att = min(1024, seq_len)
    tkv = min(512, seqs * n_mem)
    px, py, pc = _my_place()
    me = _slot_of(px, py, pc).astype(jnp.int32)
    me1 = me.reshape(1)

    shard_bf = {n: w[n][0].astype(BF16) for n in BIG}

    def landing(n):
        zone = lax.empty((N_DEV,) + shard_bf[n].shape, BF16)
        return lax.dynamic_update_slice(zone, shard_bf[n][None], (me, 0, 0))

    ag_groups = (("w_in",), ("w_out", "xw_q", "xw_kv", "xw_o"), ("w_up", "w_down"))
    ag_started, _ = split_start([([shard_bf[n] for n in grp], [landing(n) for n in grp]) for grp in ag_groups],
                                name="weights_gather_start", scatter=False)

    pool_w_bf = pool_w[0].astype(BF16)
    scale4 = pool_scale.reshape(4, 1, W)
    gn4 = hgrn_norm.reshape(4, 1, W)
    theta4 = lb_theta.reshape(2, 4, W).transpose(1, 0, 2)
    g_final = norm_final.reshape(1, D)

    (wi3,), = split_wait(ag_started[:1], x2, name="weights_gather_wait_in", scatter=False)
    full_w_in = wi3.transpose(1, 0, 2).reshape(D, -1)
    u5, n1 = proj_norm(x2, norm_mix, full_w_in, name="in_proj", tm=tm_mid, tn=4 * W, out_dtype=F32, out_slabs=5)
    y2, o_pre, st_prev = mixer_fwd(u5, pool_w_bf, scale4, theta4, gn4, seqs=seqs, seq_len=seq_len, tm=tm_mix)
    (wo3, wq3, wkv3, wao3), = split_wait(ag_started[1:2], y2, name="weights_gather_wait_attn", scatter=False)
    full_w_out, full_xw_q, full_xw_o = wo3.reshape(D, D), wq3.reshape(D, D), wao3.reshape(D, D)
    tn = 4 * W
    h1, n2 = proj_res_norm(y2, full_w_out, x2, norm_xq, name="out_proj", tm=tm_sq, tn=tn)
    q = proj_plain(n2, full_xw_q, name="q_proj", tm=tm_sq, tn=tn)
    kv3, memn = proj_norm(mem2, norm_mem, wkv3, name="kv_proj", tm=tkv, tn=wkv3.shape[2], out_dtype=BF16,
                          out_slabs=2)
    o_att = attn_fwd(q, kv3, seqs=seqs, seq_len=seq_len, n_mem=n_mem, tm=tm_att)
    h2, n3 = proj_res_norm(o_att, full_xw_o, h1, norm_mlp, name="attn_out_proj", tm=tm_sq, tn=tn)
    (wup3, wdn3), = split_wait(ag_started[2:3], h2, name="weights_gather_wait_mlp", scatter=False)
    full_w_down = wdn3.reshape(-1, D)
    tn_up = wup3.shape[2]
    aa = proj_plain(n3, wup3, name="up_proj", tm=tm_mid, tn=tn_up, relu2=True)
    dh3, dh3b, sq_err, dg_final = proj_res_loss(aa, full_w_down, h2, g_final, tgt2, name="down_proj_loss",
                                                tm=tm_mid, tn=tn)
    loss = lax.psum(0.5 * jnp.sum(sq_err) / D, ("x", "y", "c"))

    def send(parts, name):
        srcs = [p.reshape((N_DEV, -1, p.shape[-1])) for p in parts]
        lands = [lax.empty(s.shape, BF16) for s in srcs]
        started, token = split_start([(srcs, lands)], name=name, scatter=True)
        return started[0], token

    gw_down = wgrad(aa, dh3b, name="down_proj_wgrad", tt=tm_mid, tn=tn)
    sent_down, tok = send([gw_down], "grads_send_down")
    dap = back_plain(dh3b, full_w_down, name="down_proj_bwd", tm=tm_mid, tn=tn, out_dtype=BF16, relu2_value=aa,
                     after=tok)
    gw_up = wgrad(n3, dap, name="up_proj_wgrad", tt=tm_mid, tn=tn_up, out_slabs=N_DEV)
    sent_up, tok = send([gw_up], "grads_send_up")
    dh2, dh2b, dg_mlp = back_norm(dap, wup3, h2, norm_mlp, dh3, name="up_proj_bwd", tm=tm_mid, tk=tn_up, after=tok)
    do_att = back_plain(dh2b, full_xw_o, name="attn_out_proj_bwd", tm=tm_sq, tn=tn, out_dtype=BF16)
    gxw_o = wgrad(o_att, dh2b, name="attn_out_proj_wgrad", tt=tm_sq, tn=tn)
    dq, dkv3 = attn_bwd(q, kv3, do_att, seqs=seqs, seq_len=seq_len, n_mem=n_mem, tm=tm_att)
    gxw_q = wgrad(n2, dq, name="q_proj_wgrad", tt=tm_sq, tn=tn)
    gxw_kv = wgrad(memn, dkv3, name="kv_proj_wgrad", tt=tkv, tn=wkv3.shape[2], out_slabs=N_DEV)
    sent_attn, tok = send([gxw_o, gxw_q, gxw_kv], "grads_send_attn")
    dg_mem = back_norm(dkv3, wkv3, mem2, norm_mem, None, name="kv_proj_bwd", tm=tkv, tk=wkv3.shape[2])
    dh1, dh1b, dg_xq = back_norm(dq, full_xw_q, h1, norm_xq, dh2, name="q_proj_bwd", tm=tm_sq, tk=D, after=tok)
    gw_out = wgrad(y2, dh1b, name="out_proj_wgrad", tt=tm_sq, tn=tn)
    sent_out, tok = send([gw_out], "grads_send_out")
    dy2 = back_plain(dh1b, full_w_out, name="out_proj_bwd", tm=tm_sq, tn=tn, out_dtype=F32, out_slabs=2, after=tok)
    du5, dpw, dsc, dlb, dgn = mixer_bwd(u5, dy2, o_pre, st_prev, pool_w_bf, scale4, theta4, gn4,
                                        seqs=seqs, seq_len=seq_len, tm=tm_mix)
    gw_in = wgrad(n1, du5, name="in_proj_wgrad", tt=tm_mid, tn=tn)
    gw_in_slots = gw_in.reshape(D, N_DEV, -1).transpose(1, 0, 2)
    sent_in, tok = send([gw_in_slots], "grads_send_in")
    dx, dg_mix = back_norm(du5, full_w_in, x2, norm_mix, dh1, name="in_proj_bwd", tm=tm_mid, tk=tn, bf16_copy=False,
                           after=tok)

    dlb_row = dlb.reshape(1, 4 * W)
    buf_vec = _pad_rows(jnp.concatenate([dg_mix, dg_xq, dg_mem, dg_mlp, dg_final], axis=0), 8)
    buf_half = _pad_rows(jnp.concatenate([dsc.reshape(1, 4 * W), dgn.reshape(1, 4 * W), dlb_row, -dlb_row], axis=0), 8)
    small_parts = allgather_small([dpw.reshape(4 * W, W), buf_vec, buf_half])

    sent = [sent_down, sent_up, sent_attn, sent_out, sent_in]
    recv = split_wait(sent, small_parts[1], name="grads_wait", scatter=True)
    own = dict(w_down=sent_down[2][0], w_up=sent_up[2][0], xw_o=sent_attn[2][0], xw_q=sent_attn[2][1],
               xw_kv=sent_attn[2][2], w_out=sent_out[2][0], w_in=sent_in[2][0])
    got = dict(w_down=recv[0][0], w_up=recv[1][0], xw_o=recv[2][0], xw_q=recv[2][1], xw_kv=recv[2][2],
               w_out=recv[3][0], w_in=recv[4][0])
    res = {}
    for n in BIG:
        shp = w[n].shape
        r = adamw_sharded(me1, own[n], got[n], w[n][0], mom[n][0], var[n][0], name="adamw_" + n,
                          tr=min(256, shp[1]))
        for kind, a in zip("gdmv", r):
            res[kind, n] = a.reshape(shp)
    r = adamw_replicated(small_parts, [w[n].reshape(v2) for n, v2, _, _ in SMALL],
                         [mom[n].reshape(v2) for n, v2, _, _ in SMALL],
                         [var[n].reshape(v2) for n, v2, _, _ in SMALL],
                         [(b, r0, v2[0]) for _, v2, b, r0 in SMALL])
    for kind, arrs in zip("gdmv", r):
        for (n, _, _, _), a in zip(SMALL, arrs):
            res[kind, n] = a.reshape(w[n].shape)

    out = [loss, dx.reshape(x.shape)]
    for kind in "gdmv":
        out += [res[kind, n] for n in WEIGHTS]
    return tuple(out)
```

```python
import jax
import jax.numpy as jnp
from jax import lax
from jax.experimental import pallas as pl
from jax.experimental.pallas import tpu as pltpu

F32 = jnp.float32
BF16 = jnp.bfloat16
EPS = 1e-6
CHUNK = 64
POOL_HALO = 16
HEAD_W = 128
XATTN_HEADS = 4
N_DEV = 8
N_PEERS = N_DEV - 1
ADAM_LR = 0.001
ADAM_B1 = 0.9
ADAM_B2 = 0.999
ADAM_EPS = 1e-08
ADAM_WD = 0.01
ADAM_STEP = 10
V7X_VMEM_LIMIT = 52 * 1024 * 1024
MESH = pl.DeviceIdType.MESH
HBM = pl.BlockSpec(memory_space=pltpu.HBM)
SEM = pl.BlockSpec(memory_space=pltpu.SEMAPHORE)


def _cparams(dims):
    return pltpu.CompilerParams(dimension_semantics=dims, vmem_limit_bytes=V7X_VMEM_LIMIT)


def _sigmoid(v):
    return 0.5 * jnp.tanh(0.5 * v) + 0.5


def _dot(a, b):
    return jnp.dot(a, b, preferred_element_type=F32)


def _dot_nt(a, b):
    return lax.dot_general(a, b, (((1,), (1,)), ((), ())), preferred_element_type=F32)


def _dot_tn(a, b):
    return lax.dot_general(a, b, (((0,), (0,)), ((), ())), preferred_element_type=F32)


def _split3(v):
    hi = v.astype(BF16)
    r1 = v - hi.astype(F32)
    mid = r1.astype(BF16)
    lo = (r1 - mid.astype(F32)).astype(BF16)
    return hi, mid, lo


def _tri_apply(tri, v):
    hi, mid, lo = _split3(v)
    return _dot(tri, hi) + _dot(tri, mid) + _dot(tri, lo)


def _mat_shape(a):
    return a.shape if a.ndim == 2 else (a.shape[1], a.shape[0] * a.shape[2])


def _tile_spec(a, rows, cols, row_of, col_of):
    if a.ndim == 2:
        return pl.BlockSpec((rows, cols), lambda *g: (row_of(*g), col_of(*g)))
    per = a.shape[2] // cols
    return pl.BlockSpec((None, rows, cols), lambda *g: (col_of(*g) // per, row_of(*g), col_of(*g) % per))


def _out_struct(rows, n, slabs, dtype):
    return jax.ShapeDtypeStruct((rows, n) if slabs is None else (slabs, rows, n // slabs), dtype)


def norm_mm(h, g, w, *, name, tm, tn, out_dtype, out_slabs=None):
    T, D = h.shape
    N = _mat_shape(w)[1]
    o_shape = _out_struct(T, N, out_slabs, out_dtype)

    def body(h_ref, g_ref, w_ref, o_ref, n_ref):
        @pl.when(pl.program_id(1) == 0)
        def _():
            x = h_ref[...]
            r = lax.rsqrt(jnp.mean(x * x, axis=-1, keepdims=True) + EPS)
            n_ref[...] = (x * r * g_ref[...]).astype(BF16)

        o_ref[...] = _dot(n_ref[...], w_ref[...]).astype(o_ref.dtype)

    return pl.pallas_call(
        body, name=name, grid=(T // tm, N // tn),
        in_specs=[pl.BlockSpec((tm, D), lambda i, j: (i, 0)),
                  pl.BlockSpec((1, D), lambda i, j: (0, 0)),
                  _tile_spec(w, D, tn, lambda i, j: 0, lambda i, j: j)],
        out_specs=[_tile_spec(o_shape, tm, tn, lambda i, j: i, lambda i, j: j),
                   pl.BlockSpec((tm, D), lambda i, j: (i, 0))],
        out_shape=[o_shape, jax.ShapeDtypeStruct((T, D), BF16)],
        compiler_params=_cparams(("parallel", "arbitrary")),
    )(h, g, w)


def mm_nn(a, w, res, *, name, tm, tn, tk, relu2=False):
    T, K = _mat_shape(a)
    N = w.shape[1]
    nk = K // tk

    def body(a_ref, w_ref, r_ref, o_ref, acc_ref):
        k = pl.program_id(2)
        av = a_ref[...]
        if relu2:
            av = jnp.maximum(av, 0.0)
            av = av * av
        part = _dot(av.astype(BF16), w_ref[...])

        @pl.when(k == 0)
        def _():
            acc_ref[...] = part

        @pl.when(k > 0)
        def _():
            acc_ref[...] += part

        @pl.when(k == nk - 1)
        def _():
            o_ref[...] = r_ref[...] + acc_ref[...]

    return pl.pallas_call(
        body, name=name, grid=(T // tm, N // tn, nk),
        in_specs=[_tile_spec(a, tm, tk, lambda i, j, k: i, lambda i, j, k: k),
                  pl.BlockSpec((tk, tn), lambda i, j, k: (k, j)),
                  pl.BlockSpec((tm, tn), lambda i, j, k: (i, j))],
        out_specs=pl.BlockSpec((tm, tn), lambda i, j, k: (i, j)),
        out_shape=jax.ShapeDtypeStruct((T, N), F32),
        scratch_shapes=[pltpu.VMEM((tm, tn), F32)],
        compiler_params=_cparams(("parallel", "parallel", "arbitrary")),
    )(a, w, res)


def mm_nt(a, w, *, name, tm, tn, tk, out_dtype, out_slabs=None, relu2_of=None, after=None):
    T, K = _mat_shape(a)
    nk = K // tk
    N = w.shape[0]
    has_z = relu2_of is not None
    o_shape = _out_struct(T, N, out_slabs, out_dtype)

    def body(*refs):
        a_ref, w_ref = refs[0], refs[1]
        z_ref = refs[2] if has_z else None
        o_ref, acc_ref = refs[-2], refs[-1]
        k = pl.program_id(2)
        part = _dot_nt(a_ref[...].astype(BF16), w_ref[...])

        @pl.when(k == 0)
        def _():
            acc_ref[...] = part

        @pl.when(k > 0)
        def _():
            acc_ref[...] += part

        @pl.when(k == nk - 1)
        def _():
            out = acc_ref[...]
            if has_z:
                out = out * (2.0 * jnp.maximum(z_ref[...], 0.0))
            o_ref[...] = out.astype(o_ref.dtype)

    in_specs = [_tile_spec(a, tm, tk, lambda i, j, k: i, lambda i, j, k: k),
                pl.BlockSpec((tn, tk), lambda i, j, k: (j, k))]
    args = [a, w]
    if has_z:
        in_specs.append(pl.BlockSpec((tm, tn), lambda i, j, k: (i, j)))
        args.append(relu2_of)
    if after is not None:
        in_specs.append(pl.BlockSpec(after.shape, lambda i, j, k: (0, 0)))
        args.append(after)
    return pl.pallas_call(
        body, name=name, grid=(T // tm, N // tn, nk),
        in_specs=in_specs,
        out_specs=_tile_spec(o_shape, tm, tn, lambda i, j, k: i, lambda i, j, k: j),
        out_shape=o_shape,
        scratch_shapes=[pltpu.VMEM((tm, tn), F32)],
        compiler_params=_cparams(("parallel", "parallel", "arbitrary")),
    )(*args)


def mm_nt_normbwd(a, w, h, g, dres, *, name, tm, tk, after=None):
    T, K = _mat_shape(a)
    nk = K // tk
    D = h.shape[1]
    with_dh = dres is not None

    def body(*refs):
        a_ref, w_ref, h_ref, g_ref = refs[:4]
        if with_dh:
            r_ref = refs[4]
            dh_ref, dhb_ref, dg_ref, acc_ref = refs[-4:]
        else:
            dg_ref, acc_ref = refs[-2:]
        i = pl.program_id(0)
        k = pl.program_id(1)
        part = _dot_nt(a_ref[...].astype(BF16), w_ref[...])

        @pl.when(k == 0)
        def _():
            acc_ref[...] = part

        @pl.when(k > 0)
        def _():
            acc_ref[...] += part

        @pl.when(k == nk - 1)
        def _():
            dn = acc_ref[...]
            x = h_ref[...]
            r = lax.rsqrt(jnp.mean(x * x, axis=-1, keepdims=True) + EPS)
            xr = x * r
            dgp = jnp.sum(dn * xr, axis=0, keepdims=True)

            @pl.when(i == 0)
            def _():
                dg_ref[...] = dgp

            @pl.when(i > 0)
            def _():
                dg_ref[...] += dgp

            if with_dh:
                dyg = dn * g_ref[...]
                dx = r * (dyg - xr * jnp.mean(dyg * xr, axis=-1, keepdims=True))
                out = r_ref[...] + dx
                dh_ref[...] = out
                dhb_ref[...] = out.astype(BF16)

    row = pl.BlockSpec((tm, D), lambda i, k: (i, 0))
    vec = pl.BlockSpec((1, D), lambda i, k: (0, 0))
    in_specs = [_tile_spec(a, tm, tk, lambda i, k: i, lambda i, k: k),
                _tile_spec(w, D, tk, lambda i, k: 0, lambda i, k: k), row, vec]
    args = [a, w, h, g]
    if with_dh:
        in_specs.append(row)
        args.append(dres)
        out_specs = [row, row, vec]
        out_shape = [jax.ShapeDtypeStruct((T, D), F32), jax.ShapeDtypeStruct((T, D), BF16),
                     jax.ShapeDtypeStruct((1, D), F32)]
    else:
        out_specs = vec
        out_shape = jax.ShapeDtypeStruct((1, D), F32)
    if after is not None:
        in_specs.append(pl.BlockSpec(after.shape, lambda i, k: (0, 0)))
        args.append(after)
    return pl.pallas_call(
        body, name=name, grid=(T // tm, nk),
        in_specs=in_specs, out_specs=out_specs, out_shape=out_shape,
        scratch_shapes=[pltpu.VMEM((tm, D), F32)],
        compiler_params=_cparams(("arbitrary", "arbitrary")),
    )(*args)


def mm_tn(a, b, *, name, tt, tko, tn, relu2=False, out_slabs=None):
    T, K = _mat_shape(a)
    N = _mat_shape(b)[1]
    nt = T // tt
    o_shape = _out_struct(K, N, out_slabs, BF16)

    def body(a_ref, b_ref, o_ref, acc_ref):
        t = pl.program_id(2)
        av = a_ref[...]
        if relu2:
            av = jnp.maximum(av, 0.0)
            av = av * av
        part = _dot_tn(av.astype(BF16), b_ref[...].astype(BF16))

        @pl.when(t == 0)
        def _():
            acc_ref[...] = part

        @pl.when(t > 0)
        def _():
            acc_ref[...] += part

        @pl.when(t == nt - 1)
        def _():
            o_ref[...] = acc_ref[...].astype(BF16)

    return pl.pallas_call(
        body, name=name, grid=(K // tko, N // tn, nt),
        in_specs=[_tile_spec(a, tt, tko, lambda kk, j, t: t, lambda kk, j, t: kk),
                  _tile_spec(b, tt, tn, lambda kk, j, t: t, lambda kk, j, t: j)],
        out_specs=_tile_spec(o_shape, tko, tn, lambda kk, j, t: kk, lambda kk, j, t: j),
        out_shape=o_shape,
        scratch_shapes=[pltpu.VMEM((tko, tn), F32)],
        compiler_params=_cparams(("parallel", "parallel", "arbitrary")),
    )(a, b)


def _resident(a):
    nd = a.ndim
    return pl.BlockSpec(a.shape, lambda i: (0,) * nd, pipeline_mode=pl.Buffered(1))


def _row_block(a, tm):
    if a.ndim == 2:
        return pl.BlockSpec((tm, a.shape[1]), lambda i: (i, 0))
    return pl.BlockSpec((a.shape[0], tm, a.shape[2]), lambda i: (0, i, 0))


def _cols(ref, c, width):
    if len(ref.shape) == 2:
        return ref[:, c * width:(c + 1) * width]
    per = ref.shape[2] // width
    if per == 1:
        return ref[c]
    return ref[c // per, :, (c % per) * width:(c % per + 1) * width]


def _set_cols(ref, c, width, val):
    if len(ref.shape) == 2:
        ref[:, c * width:(c + 1) * width] = val
        return
    per = ref.shape[2] // width
    if per == 1:
        ref[c] = val
    else:
        ref[c // per, :, (c % per) * width:(c % per + 1) * width] = val


def _all_cols(ref):
    if len(ref.shape) == 2:
        return ref[...]
    return jnp.concatenate([ref[s] for s in range(ref.shape[0])], axis=1)


def _rms(x):
    return lax.rsqrt(jnp.mean(x * x, axis=-1, keepdims=True) + EPS)


def _row_params():
    return _cparams(("arbitrary",))


def proj_norm(h, g, w, *, name, tm, tn, out_dtype, out_slabs=None):
    T, D = h.shape
    N = _mat_shape(w)[1]
    o_shape = _out_struct(T, N, out_slabs, out_dtype)

    def body(h_ref, g_ref, w_ref, o_ref, n_ref):
        x = h_ref[...]
        n = (x * _rms(x) * g_ref[...]).astype(BF16)
        n_ref[...] = n
        for c in range(N // tn):
            _set_cols(o_ref, c, tn, _dot(n, _cols(w_ref, c, tn)).astype(out_dtype))

    return pl.pallas_call(
        body, name=name, grid=(T // tm,),
        in_specs=[_row_block(h, tm), pl.BlockSpec((1, D), lambda i: (0, 0)), _resident(w)],
        out_specs=[_row_block(o_shape, tm), pl.BlockSpec((tm, D), lambda i: (i, 0))],
        out_shape=[o_shape, jax.ShapeDtypeStruct((T, D), BF16)],
        compiler_params=_row_params(),
    )(h, g, w)


def proj_plain(a, w, *, name, tm, tn, relu2=False):
    T = a.shape[0]
    N = _mat_shape(w)[1]

    def body(a_ref, w_ref, o_ref):
        av = a_ref[...]
        for c in range(N // tn):
            z = _dot(av, _cols(w_ref, c, tn))
            if relu2:
                z = jnp.maximum(z, 0.0)
                z = z * z
            _set_cols(o_ref, c, tn, z.astype(BF16))

    o_shape = jax.ShapeDtypeStruct((T, N), BF16)
    return pl.pallas_call(
        body, name=name, grid=(T // tm,),
        in_specs=[_row_block(a, tm), _resident(w)],
        out_specs=_row_block(o_shape, tm), out_shape=o_shape,
        compiler_params=_row_params(),
    )(a, w)


def proj_res_norm(a, w, res, g, *, name, tm, tn):
    T = res.shape[0]
    D = w.shape[1]

    def body(a_ref, w_ref, r_ref, g_ref, h_ref, n_ref):
        av = _all_cols(a_ref)
        for c in range(D // tn):
            sl = slice(c * tn, (c + 1) * tn)
            h_ref[:, sl] = r_ref[:, sl] + _dot(av, w_ref[:, sl])
        hv = h_ref[...]
        n_ref[...] = (hv * _rms(hv) * g_ref[...]).astype(BF16)

    row = pl.BlockSpec((tm, D), lambda i: (i, 0))
    return pl.pallas_call(
        body, name=name, grid=(T // tm,),
        in_specs=[_row_block(a, tm), _resident(w), row, pl.BlockSpec((1, D), lambda i: (0, 0))],
        out_specs=[row, row],
        out_shape=[jax.ShapeDtypeStruct((T, D), F32), jax.ShapeDtypeStruct((T, D), BF16)],
        compiler_params=_row_params(),
    )(a, w, res, g)


def proj_res_loss(a, w, res, g, target, *, name, tm, tn):
    T = res.shape[0]
    D = w.shape[1]

    def body(a_ref, w_ref, r_ref, g_ref, t_ref, dh_ref, dhb_ref, ls_ref, dg_ref):
        i = pl.program_id(0)
        av = a_ref[...]
        for c in range(D // tn):
            sl = slice(c * tn, (c + 1) * tn)
            dh_ref[:, sl] = r_ref[:, sl] + _dot(av, w_ref[:, sl])
        x = dh_ref[...]
        gv = g_ref[...]
        r = _rms(x)
        xr = x * r
        d = xr * gv - t_ref[...]
        dy = d * (1.0 / D)
        dyg = dy * gv
        dx = r * (dyg - xr * jnp.mean(dyg * xr, axis=-1, keepdims=True))
        dh_ref[...] = dx
        dhb_ref[...] = dx.astype(BF16)
        ls = jnp.sum(d * d, axis=0, keepdims=True)
        dg = jnp.sum(dy * xr, axis=0, keepdims=True)

        @pl.when(i == 0)
        def _():
            ls_ref[...] = ls
            dg_ref[...] = dg

        @pl.when(i > 0)
        def _():
            ls_ref[...] += ls
            dg_ref[...] += dg

    row = pl.BlockSpec((tm, D), lambda i: (i, 0))
    vec = pl.BlockSpec((1, D), lambda i: (0, 0))
    return pl.pallas_call(
        body, name=name, grid=(T // tm,),
        in_specs=[_row_block(a, tm), _resident(w), row, vec, row],
        out_specs=[row, row, vec, vec],
        out_shape=[jax.ShapeDtypeStruct((T, D), F32), jax.ShapeDtypeStruct((T, D), BF16),
                   jax.ShapeDtypeStruct((1, D), F32), jax.ShapeDtypeStruct((1, D), F32)],
        compiler_params=_row_params(),
    )(a, w, res, g, target)


def _anchor_spec(after):
    return pl.BlockSpec(after.shape, lambda i: (0, 0))


def back_plain(a, w, *, name, tm, tn, out_dtype, out_slabs=None, relu2_value=None, after=None):
    T = a.shape[0]
    N = w.shape[0]
    has_z = relu2_value is not None
    o_shape = _out_struct(T, N, out_slabs, out_dtype)

    def body(*refs):
        a_ref, w_ref = refs[0], refs[1]
        o_ref = refs[-1]
        av = a_ref[...]
        for c in range(N // tn):
            out = _dot_nt(av, w_ref[c * tn:(c + 1) * tn, :])
            if has_z:
                out = out * (2.0 * jnp.sqrt(refs[2][:, c * tn:(c + 1) * tn]).astype(F32))
            _set_cols(o_ref, c, tn, out.astype(out_dtype))

    in_specs, args = [_row_block(a, tm), _resident(w)], [a, w]
    if has_z:
        in_specs.append(_row_block(relu2_value, tm))
        args.append(relu2_value)
    if after is not None:
        in_specs.append(_anchor_spec(after))
        args.append(after)
    return pl.pallas_call(
        body, name=name, grid=(T // tm,),
        in_specs=in_specs, out_specs=_row_block(o_shape, tm), out_shape=o_shape,
        compiler_params=_row_params(),
    )(*args)


def back_norm(a, w, h, g, dres, *, name, tm, tk, bf16_copy=True, after=None):
    T, K = _mat_shape(a)
    D = h.shape[1]
    with_dh = dres is not None

    def body(*refs):
        a_ref, w_ref, h_ref, g_ref = refs[:4]
        i = pl.program_id(0)
        dn = None
        for kc in range(K // tk):
            part = _dot_nt(_cols(a_ref, kc, tk).astype(BF16), _cols(w_ref, kc, tk))
            dn = part if dn is None else dn + part
        x = h_ref[...]
        r = _rms(x)
        xr = x * r
        dgp = jnp.sum(dn * xr, axis=0, keepdims=True)
        dg_ref = refs[-1]

        @pl.when(i == 0)
        def _():
            dg_ref[...] = dgp

        @pl.when(i > 0)
        def _():
            dg_ref[...] += dgp

        if with_dh:
            dyg = dn * g_ref[...]
            out = refs[4][...] + r * (dyg - xr * jnp.mean(dyg * xr, axis=-1, keepdims=True))
            if bf16_copy:
                refs[-3][...] = out
                refs[-2][...] = out.astype(BF16)
            else:
                refs[-2][...] = out

    row = pl.BlockSpec((tm, D), lambda i: (i, 0))
    vec = pl.BlockSpec((1, D), lambda i: (0, 0))
    in_specs, args = [_row_block(a, tm), _resident(w), row, vec], [a, w, h, g]
    if with_dh:
        in_specs.append(row)
        args.append(dres)
        out_specs = [row, row, vec] if bf16_copy else [row, vec]
        out_shape = [jax.ShapeDtypeStruct((T, D), F32)]
        if bf16_copy:
            out_shape.append(jax.ShapeDtypeStruct((T, D), BF16))
        out_shape.append(jax.ShapeDtypeStruct((1, D), F32))
    else:
        out_specs = vec
        out_shape = jax.ShapeDtypeStruct((1, D), F32)
    if after is not None:
        in_specs.append(_anchor_spec(after))
        args.append(after)
    return pl.pallas_call(
        body, name=name, grid=(T // tm,),
        in_specs=in_specs, out_specs=out_specs, out_shape=out_shape,
        compiler_params=_row_params(),
    )(*args)


def wgrad(a, b, *, name, tt, tn, out_slabs=None):
    T, K = _mat_shape(a)
    N = _mat_shape(b)[1]
    nt = T // tt
    o_shape = _out_struct(K, N, out_slabs, BF16)

    def body(a_ref, b_ref, o_ref, acc_ref):
        t = pl.program_id(0)
        at = _all_cols(a_ref).astype(BF16).T

        @pl.when(t == 0)
        def _():
            acc_ref[...] = jnp.zeros_like(acc_ref)

        for c in range(N // tn):
            acc_ref[:, c * tn:(c + 1) * tn] += _dot(at, _cols(b_ref, c, tn).astype(BF16))

        @pl.when(t == nt - 1)
        def _():
            for c in range(N // tn):
                _set_cols(o_ref, c, tn, acc_ref[:, c * tn:(c + 1) * tn].astype(BF16))

    return pl.pallas_call(
        body, name=name, grid=(nt,),
        in_specs=[_row_block(a, tt), _row_block(b, tt)],
        out_specs=_resident(o_shape), out_shape=o_shape,
        scratch_shapes=[pltpu.VMEM((K, N), F32)],
        compiler_params=_row_params(),
    )(a, b)


def chunk_triangles(tm):
    r = lax.broadcasted_iota(jnp.int32, (tm, tm), 0)
    c = lax.broadcasted_iota(jnp.int32, (tm, tm), 1)
    same = (r // CHUNK) == (c // CHUNK)
    tri = jnp.stack([same & (c <= r), same & (c >= r)]).astype(F32)
    return tri.astype(BF16), tri


def _tri_spec(tm):
    return pl.BlockSpec((2, tm, tm), lambda g, s, i: (0, 0, 0))


def _chunk_row(v, r, nc):
    return jnp.concatenate([jnp.broadcast_to(v[c * CHUNK + r:c * CHUNK + r + 1], (CHUNK, v.shape[1]))
                            for c in range(nc)], axis=0)


def _block_diag(v, nc):
    chunk = lax.broadcasted_iota(jnp.int32, (v.shape[0], 1), 0) // CHUNK
    return jnp.concatenate([jnp.where(chunk == c, v, jnp.zeros_like(v)) for c in range(nc)], axis=1)


def _pool_windows_back(ext_ref, tm):
    n = tm + 32
    ext_ref[1, 8:n] = ext_ref[0, 8:n] + ext_ref[0, 7:n - 1]
    ext_ref[2, 16:n] = ext_ref[1, 16:n] + ext_ref[1, 14:n - 2]
    ext_ref[3, 24:n] = ext_ref[2, 24:n] + ext_ref[2, 20:n - 4]
    s2 = ext_ref[1, 32:n]
    s4 = ext_ref[2, 32:n]
    s8 = ext_ref[3, 32:n]
    s16 = s8 + ext_ref[3, 24:n - 8]
    return s2, s4, s8, s16


def _pool_windows_fwd(ext_ref, tm):
    n = tm + 32
    ext_ref[1, 0:n - 8] = ext_ref[0, 0:n - 8] + ext_ref[0, 1:n - 7]
    ext_ref[2, 0:n - 16] = ext_ref[1, 0:n - 16] + ext_ref[1, 2:n - 14]
    ext_ref[3, 0:n - 24] = ext_ref[2, 0:n - 24] + ext_ref[2, 4:n - 20]
    s2 = ext_ref[1, 0:tm]
    s4 = ext_ref[2, 0:tm]
    s8 = ext_ref[3, 0:tm]
    s16 = s8 + ext_ref[3, 8:tm + 8]
    return s2, s4, s8, s16


def _select_window(g, s2, s4, s8, s16):
    return jnp.where(g == 0, s2, jnp.where(g == 1, s4, jnp.where(g == 2, s8, s16)))


def _pool_count(g, pos):
    width = lax.shift_left(jnp.int32(2), g)
    return jnp.minimum(pos + 1, width).astype(F32)


def _hgrn_gates(zq, zf, th):
    lb = _sigmoid(th[0:1, :] - th[1:2, :])
    sig = _sigmoid(zf)
    f = lb + (1.0 - lb) * sig
    sq = _sigmoid(zq)
    return lb, sig, f, sq


def mixer_fwd(u5, pool_w_bf, scale4, theta4, gn4, tri_bf, tri_f, *, seqs, seq_len, tm):
    T = u5.shape[1]
    tps = seq_len // tm
    nc = tm // CHUNK
    W = HEAD_W

    def body(u_ref, pw_ref, sc_ref, th_ref, gn_ref, tri_ref, msk_ref, y_ref, o_ref, st_ref, halo_ref, ext_ref, s_ref):
        g = pl.program_id(0)
        i = pl.program_id(2)

        @pl.when(i == 0)
        def _():
            halo_ref[...] = jnp.zeros_like(halo_ref)
            s_ref[...] = jnp.zeros_like(s_ref)

        row = lax.broadcasted_iota(jnp.int32, (tm, 1), 0)

        up = u_ref[0]
        ext_ref[0, 0:16] = jnp.zeros((16, W), F32)
        ext_ref[0, 16:32] = halo_ref[...]
        ext_ref[0, 32:32 + tm] = up
        win = _select_window(g, *_pool_windows_back(ext_ref, tm))
        p = win * (1.0 / _pool_count(g, i * tm + row)) - up
        halo_ref[...] = up[tm - POOL_HALO:tm]
        y_ref[0] = (_dot(p.astype(BF16), pw_ref[...]) * sc_ref[...]).astype(BF16)

        zq, zf, zi, zg = u_ref[1], u_ref[2], u_ref[3], u_ref[4]
        lb, sig, f, sq = _hgrn_gates(zq, zf, th_ref[...])
        logf = jnp.log(f)
        kk = 1.0 - f
        q = zq * sq
        G = _tri_apply(tri_ref[0], logf)
        Gm, Gl = _chunk_row(G, CHUNK // 2 - 1, nc), _chunk_row(G, CHUNK - 1, nc)
        vb = zi.astype(BF16)
        qrb = (q * jnp.exp(G - Gm)).astype(BF16)
        krb = (kk * jnp.exp(Gm - G)).astype(BF16)
        keb = (kk * jnp.exp(Gl - G)).astype(BF16)
        qgb = (q * jnp.exp(G)).astype(BF16)
        a = jnp.where(msk_ref[0] > 0.5, _dot_nt(qrb, krb), 0.0).astype(BF16)
        d_st = _dot_tn(vb, _block_diag(keb, nc))
        o_intra = _dot(a, vb)
        st = s_ref[...]
        states = []
        for c in range(nc):
            st_ref[c] = st
            states.append(st.astype(BF16))
            st = st * jnp.exp(G[(c + 1) * CHUNK - 1:(c + 1) * CHUNK]) + d_st[:, c * W:(c + 1) * W]
        s_ref[...] = st
        o = o_intra + _dot_nt(_block_diag(qgb, nc), jnp.concatenate(states, axis=1))
        o_ref[...] = o
        r = lax.rsqrt(jnp.mean(o * o, axis=-1, keepdims=True) + EPS)
        y_ref[1] = (o * r * gn_ref[...] * (zg * _sigmoid(zg))).astype(BF16)

    def rb(s, i):
        return s * tps + i

    return pl.pallas_call(
        body, name="mixer_fwd", grid=(4, seqs, tps),
        in_specs=[pl.BlockSpec((5, tm, W), lambda g, s, i: (0, rb(s, i), g)),
                  pl.BlockSpec((None, W, W), lambda g, s, i: (g, 0, 0)),
                  pl.BlockSpec((None, 1, W), lambda g, s, i: (g, 0, 0)),
                  pl.BlockSpec((None, 2, W), lambda g, s, i: (g, 0, 0)),
                  pl.BlockSpec((None, 1, W), lambda g, s, i: (g, 0, 0)),
                  _tri_spec(tm), _tri_spec(tm)],
        out_specs=[pl.BlockSpec((2, tm, W), lambda g, s, i: (0, rb(s, i), g)),
                   pl.BlockSpec((tm, W), lambda g, s, i: (rb(s, i), g)),
                   pl.BlockSpec((nc, None, W, W), lambda g, s, i: (rb(s, i), g, 0, 0))],
        out_shape=[jax.ShapeDtypeStruct((2, T, 4 * W), BF16),
                   jax.ShapeDtypeStruct((T, 4 * W), F32),
                   jax.ShapeDtypeStruct((T // CHUNK, 4, W, W), F32)],
        scratch_shapes=[pltpu.VMEM((POOL_HALO, W), F32),
                        pltpu.VMEM((4, tm + 32, W), F32),
                        pltpu.VMEM((W, W), F32)],
        compiler_params=_cparams(("arbitrary", "arbitrary", "arbitrary")),
    )(u5, pool_w_bf, scale4, theta4, gn4, tri_bf, tri_f)


def mixer_bwd(u5, dy2, o_pre, st_prev, pool_w_bf, scale4, theta4, gn4, tri_bf, tri_f, *, seqs, seq_len, tm):
    T = u5.shape[1]
    tps = seq_len // tm
    nc = tm // CHUNK
    W = HEAD_W
    hb = tm // POOL_HALO

    def body(u_ref, uh_ref, dy_ref, o_ref, st_ref, pw_ref, sc_ref, th_ref, gn_ref, tri_ref, msk_ref,
             du_ref, dpw_ref, dsc_ref, dlb_ref, dgn_ref, nxt_ref, ext_ref, ds_ref):
        g = pl.program_id(0)
        s = pl.program_id(1)
        i = pl.program_id(2)
        tile = tps - 1 - i
        first = (s == 0) & (i == 0)

        @pl.when(i == 0)
        def _():
            nxt_ref[...] = jnp.zeros_like(nxt_ref)
            ds_ref[...] = jnp.zeros_like(ds_ref)

        row = lax.broadcasted_iota(jnp.int32, (tm, 1), 0)
        cnt = _pool_count(g, tile * tm + row)

        def accumulate(ref, val):
            @pl.when(first)
            def _():
                ref[...] = val

            @pl.when(jnp.logical_not(first))
            def _():
                ref[...] += val

        up = u_ref[0]
        ext_ref[0, 0:16] = jnp.zeros((16, W), F32)
        ext_ref[0, 16:32] = jnp.where(tile == 0, 0.0, uh_ref[...])
        ext_ref[0, 32:32 + tm] = up
        win = _select_window(g, *_pool_windows_back(ext_ref, tm))
        inv_cnt = 1.0 / cnt
        pb = (win * inv_cnt - up).astype(BF16)
        dyp = dy_ref[0]
        z = _dot(pb, pw_ref[...])
        accumulate(dsc_ref, jnp.sum(dyp * z, axis=0, keepdims=True))
        dz = (dyp * sc_ref[...]).astype(BF16)
        accumulate(dpw_ref, _dot_tn(pb, dz))
        dp = _dot_nt(dz, pw_ref[...])
        e = dp * inv_cnt
        ext_ref[0, 0:tm] = e
        ext_ref[0, tm:tm + 16] = nxt_ref[...]
        ext_ref[0, tm + 16:tm + 32] = jnp.zeros((16, W), F32)
        lead = _select_window(g, *_pool_windows_fwd(ext_ref, tm))
        nxt_ref[...] = e[0:POOL_HALO]
        du_ref[0] = (lead - dp).astype(BF16)

        zq, zf, zi, zg = u_ref[1], u_ref[2], u_ref[3], u_ref[4]
        lb, sig, f, sq = _hgrn_gates(zq, zf, th_ref[...])
        logf = jnp.log(f)
        kk = 1.0 - f
        q = zq * sq
        G = _tri_apply(tri_ref[0], logf)

        dyh = dy_ref[1]
        o = o_ref[...]
        sg = _sigmoid(zg)
        r = lax.rsqrt(jnp.mean(o * o, axis=-1, keepdims=True) + EPS)
        orr = o * r
        gn = gn_ref[...]
        du_ref[4] = (dyh * (orr * gn) * (sg * (1.0 + zg * (1.0 - sg)))).astype(BF16)
        don = dyh * (zg * sg)
        accumulate(dgn_ref, jnp.sum(don * orr, axis=0, keepdims=True))
        dog = don * gn
        do = r * (dog - orr * jnp.mean(dog * orr, axis=-1, keepdims=True))

        Gm, Gl = _chunk_row(G, CHUNK // 2 - 1, nc), _chunk_row(G, CHUNK - 1, nc)
        e_q, e_k, e_e, e_g = jnp.exp(G - Gm), jnp.exp(Gm - G), jnp.exp(Gl - G), jnp.exp(G)
        qr, kr, ke, qg = q * e_q, kk * e_k, kk * e_e, q * e_g
        qrb, krb, keb, qgb = qr.astype(BF16), kr.astype(BF16), ke.astype(BF16), qg.astype(BF16)
        vb = zi.astype(BF16)
        dob = do.astype(BF16)
        lower, upper = msk_ref[0] > 0.5, msk_ref[1] > 0.5
        da = jnp.where(lower, _dot_nt(dob, vb), 0.0).astype(BF16)
        a_t = jnp.where(upper, _dot_nt(krb, qrb), 0.0).astype(BF16)
        da_t = jnp.where(upper, _dot_nt(vb, dob), 0.0).astype(BF16)
        u_cat = _dot_tn(dob, _block_diag(qgb, nc))
        dqr = _dot(da, krb)
        dkr = _dot(da_t, qrb)
        dv = _dot(a_t, dob)
        dsn = ds_ref[...]
        dsn_b, ddecay = [None] * nc, [None] * nc
        for c in reversed(range(nc)):
            decay = jnp.exp(G[(c + 1) * CHUNK - 1:(c + 1) * CHUNK])
            dsn_b[c] = dsn.astype(BF16)
            ddecay[c] = jnp.sum(dsn * st_ref[c], axis=0, keepdims=True) * decay
            dsn = u_cat[:, c * W:(c + 1) * W] + dsn * decay
        ds_ref[...] = dsn
        st_rows = jnp.concatenate([st_ref[c].astype(BF16) for c in range(nc)], axis=0)
        dqg = _dot(_block_diag(dob, nc), st_rows)
        dke = _dot(_block_diag(vb, nc), jnp.concatenate(dsn_b, axis=0))
        dv = dv + _dot_nt(_block_diag(keb, nc), jnp.concatenate(dsn_b, axis=1))
        t_qr, t_kr, t_qg, t_ke = dqr * qr, dkr * kr, dqg * qg, dke * ke
        dq = dqr * e_q + dqg * e_g
        dk = dkr * e_k + dke * e_e
        crow = lax.broadcasted_iota(jnp.int32, (CHUNK, 1), 0)
        t_mid = t_kr - t_qr
        ends = []
        for c in range(nc):
            sl = slice(c * CHUNK, (c + 1) * CHUNK)
            dgm = jnp.sum(t_mid[sl], axis=0, keepdims=True)
            dgl = jnp.sum(t_ke[sl], axis=0, keepdims=True) + ddecay[c]
            ends.append(jnp.where(crow == CHUNK // 2 - 1, dgm, 0.0) + jnp.where(crow == CHUNK - 1, dgl, 0.0))
        dG = t_qg - t_ke - t_mid + jnp.concatenate(ends, axis=0)
        dlogf = _tri_apply(tri_ref[1], dG)
        df = dlogf / f - dk
        du_ref[1] = (dq * (sq * (1.0 + zq * (1.0 - sq)))).astype(BF16)
        du_ref[2] = (df * (1.0 - lb) * (sig * (1.0 - sig))).astype(BF16)
        du_ref[3] = dv.astype(BF16)
        accumulate(dlb_ref, jnp.sum(df * (1.0 - sig), axis=0, keepdims=True) * (lb * (1.0 - lb)))

    def rb(s, i):
        return s * tps + (tps - 1 - i)

    vec = pl.BlockSpec((None, 1, W), lambda g, s, i: (g, 0, 0))
    mat = pl.BlockSpec((None, W, W), lambda g, s, i: (g, 0, 0))
    return pl.pallas_call(
        body, name="mixer_bwd", grid=(4, seqs, tps),
        in_specs=[pl.BlockSpec((5, tm, W), lambda g, s, i: (0, rb(s, i), g)),
                  pl.BlockSpec((None, POOL_HALO, W), lambda g, s, i: (0, jnp.maximum(rb(s, i) * hb - 1, 0), g)),
                  pl.BlockSpec((2, tm, W), lambda g, s, i: (0, rb(s, i), g)),
                  pl.BlockSpec((tm, W), lambda g, s, i: (rb(s, i), g)),
                  pl.BlockSpec((nc, None, W, W), lambda g, s, i: (rb(s, i), g, 0, 0)),
                  mat, vec,
                  pl.BlockSpec((None, 2, W), lambda g, s, i: (g, 0, 0)),
                  vec, _tri_spec(tm), _tri_spec(tm)],
        out_specs=[pl.BlockSpec((5, tm, W), lambda g, s, i: (0, rb(s, i), g)), mat, vec, vec, vec],
        out_shape=[jax.ShapeDtypeStruct((5, T, 4 * W), BF16),
                   jax.ShapeDtypeStruct((4, W, W), F32),
                   jax.ShapeDtypeStruct((4, 1, W), F32),
                   jax.ShapeDtypeStruct((4, 1, W), F32),
                   jax.ShapeDtypeStruct((4, 1, W), F32)],
        scratch_shapes=[pltpu.VMEM((POOL_HALO, W), F32),
                        pltpu.VMEM((4, tm + 32, W), F32),
                        pltpu.VMEM((W, W), F32)],
        compiler_params=_cparams(("arbitrary", "arbitrary", "arbitrary")),
    )(u5, u5, dy2, o_pre, st_prev, pool_w_bf, scale4, theta4, gn4, tri_bf, tri_f)


def _attn_probs(q, k, hd):
    s = _dot_nt(q, k) * (1.0 / (hd ** 0.5))
    e = jnp.exp(s - jnp.max(s, axis=-1, keepdims=True))
    return e * (1.0 / jnp.sum(e, axis=-1, keepdims=True))


def attn_fwd(q, kv3, *, seqs, seq_len, n_mem, tm):
    T, D = q.shape
    hd = D // XATTN_HEADS
    tps = seq_len // tm

    def body(q_ref, kv_ref, o_ref):
        p = _attn_probs(q_ref[...], kv_ref[0], hd)
        o_ref[...] = _dot(p.astype(BF16), kv_ref[1]).astype(BF16)

    return pl.pallas_call(
        body, name="attn_fwd", grid=(seqs, XATTN_HEADS, tps),
        in_specs=[pl.BlockSpec((tm, hd), lambda b, h, i: (b * tps + i, h)),
                  pl.BlockSpec((2, n_mem, hd), lambda b, h, i: (0, b, h))],
        out_specs=pl.BlockSpec((tm, hd), lambda b, h, i: (b * tps + i, h)),
        out_shape=jax.ShapeDtypeStruct((T, D), BF16),
        compiler_params=_cparams(("parallel", "parallel", "arbitrary")),
    )(q, kv3)


def attn_bwd(q, kv3, do, *, seqs, seq_len, n_mem, tm):
    T, D = q.shape
    hd = D // XATTN_HEADS
    tps = seq_len // tm

    def body(q_ref, kv_ref, do_ref, dq_ref, dkv_ref):
        i = pl.program_id(2)
        qv, k, v, dov = q_ref[...], kv_ref[0], kv_ref[1], do_ref[...]
        p = _attn_probs(qv, k, hd)
        dp = _dot_nt(dov, v)
        ds = (p * (dp - jnp.sum(dp * p, axis=-1, keepdims=True)) * (1.0 / (hd ** 0.5))).astype(BF16)
        dq_ref[...] = _dot(ds, k).astype(BF16)
        dk = _dot_tn(ds, qv)
        dv = _dot_tn(p.astype(BF16), dov)

        @pl.when(i == 0)
        def _():
            dkv_ref[0] = dk
            dkv_ref[1] = dv

        @pl.when(i > 0)
        def _():
            dkv_ref[0] += dk
            dkv_ref[1] += dv

    qspec = pl.BlockSpec((tm, hd), lambda b, h, i: (b * tps + i, h))
    kvspec = pl.BlockSpec((2, n_mem, hd), lambda b, h, i: (0, b, h))
    return pl.pallas_call(
        body, name="attn_bwd", grid=(seqs, XATTN_HEADS, tps),
        in_specs=[qspec, kvspec, qspec],
        out_specs=[qspec, kvspec],
        out_shape=[jax.ShapeDtypeStruct((T, D), BF16), jax.ShapeDtypeStruct((2, seqs * n_mem, D), F32)],
        compiler_params=_cparams(("parallel", "parallel", "arbitrary")),
    )(q, kv3, do)


def final_loss(h, g, target, *, tm):
    T, D = h.shape

    def body(h_ref, g_ref, t_ref, dh_ref, dhb_ref, ls_ref, dg_ref):
        i = pl.program_id(0)
        x = h_ref[...]
        gv = g_ref[...]
        r = lax.rsqrt(jnp.mean(x * x, axis=-1, keepdims=True) + EPS)
        xr = x * r
        d = xr * gv - t_ref[...]
        dy = d * (1.0 / D)
        dyg = dy * gv
        dx = r * (dyg - xr * jnp.mean(dyg * xr, axis=-1, keepdims=True))
        dh_ref[...] = dx
        dhb_ref[...] = dx.astype(BF16)
        ls = jnp.sum(d * d, axis=0, keepdims=True)
        dg = jnp.sum(dy * xr, axis=0, keepdims=True)

        @pl.when(i == 0)
        def _():
            ls_ref[...] = ls
            dg_ref[...] = dg

        @pl.when(i > 0)
        def _():
            ls_ref[...] += ls
            dg_ref[...] += dg

    row = pl.BlockSpec((tm, D), lambda i: (i, 0))
    vec = pl.BlockSpec((1, D), lambda i: (0, 0))
    return pl.pallas_call(
        body, name="final_loss", grid=(T // tm,),
        in_specs=[row, vec, row], out_specs=[row, row, vec, vec],
        out_shape=[jax.ShapeDtypeStruct((T, D), F32), jax.ShapeDtypeStruct((T, D), BF16),
                   jax.ShapeDtypeStruct((1, D), F32), jax.ShapeDtypeStruct((1, D), F32)],
        compiler_params=_cparams(("arbitrary",)),
    )(h, g, target)


def _my_place():
    return lax.axis_index("x"), lax.axis_index("y"), lax.axis_index("c")


def _slot_of(px, py, pc):
    return 4 * px + 2 * py + pc


def _peer(k, x, y, c):
    return (1 - x if (k >> 2) & 1 else x, 1 - y if (k >> 1) & 1 else y, 1 - c if k & 1 else c)


def _split_copies(src_refs, land_refs, send_sems, recv_sems, scatter):
    x, y, c = _my_place()
    mine = _slot_of(x, y, c)
    copies = []
    for a, (src, land) in enumerate(zip(src_refs, land_refs)):
        for k in range(1, N_DEV):
            peer = _peer(k, x, y, c)
            copies.append(pltpu.make_async_remote_copy(
                src_ref=src.at[_slot_of(*peer)] if scatter else src, dst_ref=land.at[mine],
                send_sem=send_sems.at[a * N_PEERS + k - 1], recv_sem=recv_sems.at[a * N_PEERS + k - 1],
                device_id=peer, device_id_type=MESH))
    return copies


def split_start(groups, *, name, scatter):
    sizes = [len(srcs) for srcs, _ in groups]
    n_arr = sum(sizes)
    flat = [a for srcs, lands in groups for a in list(srcs) + list(lands)]

    def body(*refs):
        ins = refs[:2 * n_arr]
        sems = refs[4 * n_arr:4 * n_arr + 2 * len(groups)]
        token = refs[-1]
        at = 0
        for gi, n in enumerate(sizes):
            for cp in _split_copies(ins[at:at + n], ins[at + n:at + 2 * n], sems[2 * gi], sems[2 * gi + 1], scatter):
                cp.start()
            at += 2 * n
        token[...] = jnp.zeros_like(token)

    sem_shapes = []
    for n in sizes:
        sem_shapes += [pltpu.SemaphoreType.DMA((n * N_PEERS,))] * 2
    outs = pl.pallas_call(
        body, name=name,
        out_shape=tuple(pltpu.HBM(a.shape, a.dtype) for a in flat) + tuple(sem_shapes)
        + (jax.ShapeDtypeStruct((8, 128), F32),),
        in_specs=(HBM,) * len(flat),
        out_specs=(HBM,) * len(flat) + (SEM,) * len(sem_shapes) + (pl.BlockSpec(memory_space=pltpu.VMEM),),
        input_output_aliases={i: i for i in range(len(flat))},
        compiler_params=pltpu.CompilerParams(has_side_effects=pltpu.SideEffectType.DATAFLOW_SIDE_EFFECTING),
    )(*[pltpu.with_memory_space_constraint(a, pltpu.HBM) for a in flat])
    thru, sems, token = outs[:len(flat)], outs[len(flat):-1], outs[-1]
    started, at = [], 0
    for gi, n in enumerate(sizes):
        started.append((sems[2 * gi], sems[2 * gi + 1], thru[at:at + n], thru[at + n:at + 2 * n]))
        at += 2 * n
    return started, token


def split_wait(started, after, *, name, scatter):
    sizes = [len(g[2]) for g in started]
    n_arr = sum(sizes)
    flat = [a for g in started for a in list(g[2]) + list(g[3])]
    sems = [s for g in started for s in g[:2]]

    def body(*refs):
        ins = refs[:2 * n_arr]
        sem_refs = refs[2 * n_arr:2 * n_arr + len(sems)]
        at = 0
        for gi, n in enumerate(sizes):
            for cp in _split_copies(ins[at:at + n], ins[at + n:at + 2 * n], sem_refs[2 * gi], sem_refs[2 * gi + 1], scatter):
                cp.wait_send()
                cp.wait_recv()
            at += 2 * n

    outs = pl.pallas_call(
        body, name=name,
        out_shape=tuple(pltpu.HBM(a.shape, a.dtype) for a in flat),
        in_specs=(HBM,) * len(flat) + (SEM,) * len(sems) + (pl.BlockSpec(memory_space=pl.ANY),),
        out_specs=(HBM,) * len(flat),
        input_output_aliases={i: i for i in range(len(flat))},
        compiler_params=pltpu.CompilerParams(has_side_effects=pltpu.SideEffectType.DATAFLOW_SIDE_EFFECTING),
    )(*flat, *sems, after)
    done, at = [], 0
    for n in sizes:
        done.append((outs[at:at + n], outs[at + n:at + 2 * n]))
        at += 2 * n
    return done


def allgather_small(bufs):
    n = len(bufs)

    def body(*refs):
        srcs, outs = refs[:n], refs[n:2 * n]
        send_sems, recv_sems, local_sems = refs[2 * n:]
        x, y, c = _my_place()
        mine = _slot_of(x, y, c)
        local = [pltpu.make_async_copy(s, o.at[mine], local_sems.at[a]) for a, (s, o) in enumerate(zip(srcs, outs))]
        for cp in local:
            cp.start()
        copies = _split_copies(srcs, outs, send_sems, recv_sems, False)
        for cp in copies:
            cp.start()
        for cp in copies:
            cp.wait()
        for cp in local:
            cp.wait()

    return pl.pallas_call(
        body, name="allgather_small",
        out_shape=[jax.ShapeDtypeStruct((N_DEV,) + b.shape, b.dtype) for b in bufs],
        in_specs=[HBM] * n, out_specs=[HBM] * n,
        scratch_shapes=[pltpu.SemaphoreType.DMA((n * N_PEERS,)), pltpu.SemaphoreType.DMA((n * N_PEERS,)),
                        pltpu.SemaphoreType.DMA((n,))],
    )(*bufs)


def _adamw_math(g, w, m, v):
    c1 = 1.0 - ADAM_B1 ** ADAM_STEP
    c2 = 1.0 - ADAM_B2 ** ADAM_STEP
    nm = ADAM_B1 * m + (1.0 - ADAM_B1) * g
    nv = ADAM_B2 * v + (1.0 - ADAM_B2) * (g * g)
    delta = -ADAM_LR * ((nm / c1) / (jnp.sqrt(nv / c2) + ADAM_EPS) + ADAM_WD * w)
    return delta, nm, nv


def adamw_sharded(me, own, recv, w, m, v, *, name, tr):
    R, C = w.shape

    def body(me_ref, *refs):
        parts = refs[:N_DEV]
        w_ref, m_ref, v_ref, g_ref, d_ref, nm_ref, nv_ref = refs[N_DEV:]
        g = parts[0][...].astype(F32)
        for p in parts[1:]:
            g = g + p[...].astype(F32)
        g_ref[...] = g
        d_ref[...], nm_ref[...], nv_ref[...] = _adamw_math(g, w_ref[...], m_ref[...], v_ref[...])

    def slab(k):
        return pl.BlockSpec((None, tr, C), lambda i, me_ref: (me_ref[0] ^ k, i, 0))

    blk = pl.BlockSpec((tr, C), lambda i, me_ref: (i, 0))
    out = jax.ShapeDtypeStruct((R, C), F32)
    return pl.pallas_call(
        body, name=name,
        grid_spec=pltpu.PrefetchScalarGridSpec(
            num_scalar_prefetch=1, grid=(R // tr,),
            in_specs=[slab(k) for k in range(N_DEV)] + [blk, blk, blk],
            out_specs=[blk, blk, blk, blk]),
        out_shape=[out, out, out, out],
        compiler_params=_cparams(("parallel",)),
    )(me, own, *([recv] * N_PEERS), w, m, v)


def adamw_replicated(parts, ws, ms, vs, rows):
    n_buf, n_par = len(parts), len(ws)

    def body(*refs):
        p_refs = refs[:n_buf]
        w_refs = refs[n_buf:n_buf + n_par]
        m_refs = refs[n_buf + n_par:n_buf + 2 * n_par]
        v_refs = refs[n_buf + 2 * n_par:n_buf + 3 * n_par]
        outs = refs[n_buf + 3 * n_par:]
        sums = []
        for p in p_refs:
            g = p[0]
            for s in range(1, N_DEV):
                g = g + p[s]
            sums.append(g)
        for j, (b, r0, nr) in enumerate(rows):
            g = sums[b][r0:r0 + nr]
            delta, nm, nv = _adamw_math(g, w_refs[j][...], m_refs[j][...], v_refs[j][...])
            outs[j][...] = g
            outs[n_par + j][...] = delta
            outs[2 * n_par + j][...] = nm
            outs[3 * n_par + j][...] = nv

    shapes = [jax.ShapeDtypeStruct(w.shape, F32) for w in ws]
    outs = pl.pallas_call(
        body, name="adamw_replicated", out_shape=shapes * 4,
        compiler_params=pltpu.CompilerParams(vmem_limit_bytes=V7X_VMEM_LIMIT),
    )(*parts, *ws, *ms, *vs)
    return outs[:n_par], outs[n_par:2 * n_par], outs[2 * n_par:3 * n_par], outs[3 * n_par:]


BIG = ("w_in", "w_out", "xw_q", "xw_kv", "xw_o", "w_up", "w_down")
COL_SHARDED = ("w_in", "xw_kv", "w_up")
WEIGHTS = ("norm_mix", "w_in", "pool_w", "pool_scale", "lb_theta", "hgrn_norm", "w_out", "norm_xq",
           "norm_mem", "xw_q", "xw_kv", "xw_o", "norm_mlp", "w_up", "w_down", "norm_final")
SMALL = (("pool_w", (4 * HEAD_W, HEAD_W), 0, 0),
         ("norm_mix", (1, 1024), 1, 0), ("norm_xq", (1, 1024), 1, 1), ("norm_mem", (1, 1024), 1, 2),
         ("norm_mlp", (1, 1024), 1, 3), ("norm_final", (1, 1024), 1, 4),
         ("pool_scale", (1, 512), 2, 0), ("hgrn_norm", (1, 512), 2, 1), ("lb_theta", (2, 512), 2, 2))


def _pad_rows(a, rows):
    return jnp.concatenate([a, jnp.zeros((rows - a.shape[0], a.shape[1]), a.dtype)], axis=0)


def kernel(x, mem, norm_mix, w_in, pool_w, pool_scale, lb_theta, hgrn_norm, w_out, norm_xq, norm_mem, xw_q, xw_kv, xw_o, norm_mlp, w_up, w_down, norm_final, loss_target, m_norm_mix, m_w_in, m_pool_w, m_pool_scale, m_lb_theta, m_hgrn_norm, m_w_out, m_norm_xq, m_norm_mem, m_xw_q, m_xw_kv, m_xw_o, m_norm_mlp, m_w_up, m_w_down, m_norm_final, v_norm_mix, v_w_in, v_pool_w, v_pool_scale, v_lb_theta, v_hgrn_norm, v_w_out, v_norm_xq, v_norm_mem, v_xw_q, v_xw_kv, v_xw_o, v_norm_mlp, v_w_up, v_w_down, v_norm_final):
    w = dict(norm_mix=norm_mix, w_in=w_in, pool_w=pool_w, pool_scale=pool_scale, lb_theta=lb_theta,
             hgrn_norm=hgrn_norm, w_out=w_out, norm_xq=norm_xq, norm_mem=norm_mem, xw_q=xw_q, xw_kv=xw_kv,
             xw_o=xw_o, norm_mlp=norm_mlp, w_up=w_up, w_down=w_down, norm_final=norm_final)
    mom = dict(norm_mix=m_norm_mix, w_in=m_w_in, pool_w=m_pool_w, pool_scale=m_pool_scale, lb_theta=m_lb_theta,
               hgrn_norm=m_hgrn_norm, w_out=m_w_out, norm_xq=m_norm_xq, norm_mem=m_norm_mem, xw_q=m_xw_q,
               xw_kv=m_xw_kv, xw_o=m_xw_o, norm_mlp=m_norm_mlp, w_up=m_w_up, w_down=m_w_down,
               norm_final=m_norm_final)
    var = dict(norm_mix=v_norm_mix, w_in=v_w_in, pool_w=v_pool_w, pool_scale=v_pool_scale, lb_theta=v_lb_theta,
               hgrn_norm=v_hgrn_norm, w_out=v_w_out, norm_xq=v_norm_xq, norm_mem=v_norm_mem, xw_q=v_xw_q,
               xw_kv=v_xw_kv, xw_o=v_xw_o, norm_mlp=v_norm_mlp, w_up=v_w_up, w_down=v_w_down,
               norm_final=v_norm_final)

    seqs, seq_len, D = x.shape
    n_mem = mem.shape[1]
    T = seqs * seq_len
    W = HEAD_W
    x2 = x.reshape(T, D)
    mem2 = mem.reshape(seqs * n_mem, D)
    tgt2 = loss_target.reshape(T, D)
    tm_big = min(1024, T)
    tm_mid = min(512, T)
    tm_sq = min(1024, T)
    tm_mix = min(256, seq_len)
    tm_att = min(1024, seq_len)
    tkv = min(512, seqs * n_mem)
    px, py, pc = _my_place()
    me = _slot_of(px, py, pc).astype(jnp.int32)
    me1 = me.reshape(1)

    shard_bf = {n: w[n][0].astype(BF16) for n in BIG}

    def landing(n):
        zone = lax.empty((N_DEV,) + shard_bf[n].shape, BF16)
        return lax.dynamic_update_slice(zone, shard_bf[n][None], (me, 0, 0))

    ag_groups = (("w_in",), ("w_out", "xw_q", "xw_kv", "xw_o"), ("w_up", "w_down"))
    ag_started, _ = split_start([([shard_bf[n] for n in grp], [landing(n) for n in grp]) for grp in ag_groups],
                                name="weights_gather_start", scatter=False)

    pool_w_bf = pool_w[0].astype(BF16)
    scale4 = pool_scale.reshape(4, 1, W)
    gn4 = hgrn_norm.reshape(4, 1, W)
    theta4 = lb_theta.reshape(2, 4, W).transpose(1, 0, 2)
    g_final = norm_final.reshape(1, D)

    (_, (wi3,)), = split_wait(ag_started[:1], x2, name="weights_gather_wait_in", scatter=False)
    full_w_in = wi3.transpose(1, 0, 2).reshape(D, -1)
    u5, n1 = proj_norm(x2, norm_mix, full_w_in, name="in_proj", tm=tm_mid, tn=4 * W, out_dtype=F32, out_slabs=5)
    tri_bf, tri_f = chunk_triangles(tm_mix)
    y2, o_pre, st_prev = mixer_fwd(u5, pool_w_bf, scale4, theta4, gn4, tri_bf, tri_f, seqs=seqs, seq_len=seq_len,
                                   tm=tm_mix)
    (_, (wo3, wq3, wkv3, wao3)), = split_wait(ag_started[1:2], y2, name="weights_gather_wait_attn", scatter=False)
    full_w_out, full_xw_q, full_xw_o = wo3.reshape(D, D), wq3.reshape(D, D), wao3.reshape(D, D)
    tn = 4 * W
    h1, n2 = proj_res_norm(y2, full_w_out, x2, norm_xq, name="out_proj", tm=tm_sq, tn=tn)
    q = proj_plain(n2, full_xw_q, name="q_proj", tm=tm_sq, tn=tn)
    kv3, memn = proj_norm(mem2, norm_mem, wkv3, name="kv_proj", tm=tkv, tn=wkv3.shape[2], out_dtype=BF16,
                          out_slabs=2)
    o_att = attn_fwd(q, kv3, seqs=seqs, seq_len=seq_len, n_mem=n_mem, tm=tm_att)
    h2, n3 = proj_res_norm(o_att, full_xw_o, h1, norm_mlp, name="attn_out_proj", tm=tm_sq, tn=tn)
    (_, (wup3, wdn3)), = split_wait(ag_started[2:3], h2, name="weights_gather_wait_mlp", scatter=False)
    full_w_down = wdn3.reshape(-1, D)
    tn_up = wup3.shape[2]
    aa = proj_plain(n3, wup3, name="up_proj", tm=tm_mid, tn=tn_up, relu2=True)
    dh3, dh3b, sq_err, dg_final = proj_res_loss(aa, full_w_down, h2, g_final, tgt2, name="down_proj_loss",
                                                tm=tm_mid, tn=tn)

    def send(parts, name):
        srcs = [p.reshape((N_DEV, -1, p.shape[-1])) for p in parts]
        lands = [lax.empty(s.shape, BF16) for s in srcs]
        started, token = split_start([(srcs, lands)], name=name, scatter=True)
        return started[0], token

    gw_down = wgrad(aa, dh3b, name="down_proj_wgrad", tt=tm_mid, tn=tn)
    sent_down, tok = send([gw_down], "grads_send_down")
    dap = back_plain(dh3b, full_w_down, name="down_proj_bwd", tm=tm_mid, tn=tn, out_dtype=BF16, relu2_value=aa,
                     after=tok)
    gw_up = wgrad(n3, dap, name="up_proj_wgrad", tt=tm_mid, tn=tn_up, out_slabs=N_DEV)
    sent_up, tok = send([gw_up], "grads_send_up")
    dh2, dh2b, dg_mlp = back_norm(dap, wup3, h2, norm_mlp, dh3, name="up_proj_bwd", tm=tm_mid, tk=tn_up, after=tok)
    do_att = back_plain(dh2b, full_xw_o, name="attn_out_proj_bwd", tm=tm_sq, tn=tn, out_dtype=BF16)
    gxw_o = wgrad(o_att, dh2b, name="attn_out_proj_wgrad", tt=tm_sq, tn=tn)
    dq, dkv3 = attn_bwd(q, kv3, do_att, seqs=seqs, seq_len=seq_len, n_mem=n_mem, tm=tm_att)
    gxw_q = wgrad(n2, dq, name="q_proj_wgrad", tt=tm_sq, tn=tn)
    gxw_kv = wgrad(memn, dkv3, name="kv_proj_wgrad", tt=tkv, tn=wkv3.shape[2], out_slabs=N_DEV)
    sent_attn, tok = send([gxw_o, gxw_q, gxw_kv], "grads_send_attn")
    dg_mem = back_norm(dkv3, wkv3, mem2, norm_mem, None, name="kv_proj_bwd", tm=tkv, tk=wkv3.shape[2])
    dh1, dh1b, dg_xq = back_norm(dq, full_xw_q, h1, norm_xq, dh2, name="q_proj_bwd", tm=tm_sq, tk=D, after=tok)
    gw_out = wgrad(y2, dh1b, name="out_proj_wgrad", tt=tm_sq, tn=tn)
    sent_out, tok = send([gw_out], "grads_send_out")
    dy2 = back_plain(dh1b, full_w_out, name="out_proj_bwd", tm=tm_sq, tn=tn, out_dtype=F32, out_slabs=2, after=tok)
    du5, dpw, dsc, dlb, dgn = mixer_bwd(u5, dy2, o_pre, st_prev, pool_w_bf, scale4, theta4, gn4, tri_bf, tri_f,
                                        seqs=seqs, seq_len=seq_len, tm=tm_mix)
    gw_in = wgrad(n1, du5, name="in_proj_wgrad", tt=tm_mid, tn=tn)
    gw_in_slots = gw_in.reshape(D, N_DEV, -1).transpose(1, 0, 2)
    sent_in, tok = send([gw_in_slots], "grads_send_in")
    dx, dg_mix = back_norm(du5, full_w_in, x2, norm_mix, dh1, name="in_proj_bwd", tm=tm_mid, tk=tn, bf16_copy=False,
                           after=tok)

    dlb_row = dlb.reshape(1, 4 * W)
    buf_vec = _pad_rows(jnp.concatenate([dg_mix, dg_xq, dg_mem, dg_mlp, dg_final, sq_err], axis=0), 8)
    buf_half = _pad_rows(jnp.concatenate([dsc.reshape(1, 4 * W), dgn.reshape(1, 4 * W), dlb_row, -dlb_row], axis=0), 8)
    small_src = [dpw.reshape(4 * W, W), buf_vec, buf_half]
    small_land = [lax.dynamic_update_slice(lax.empty((N_DEV,) + b.shape, F32), b[None], (me, 0, 0))
                  for b in small_src]
    small_started, tok = split_start([(small_src, small_land)], name="small_grads_start", scatter=False)

    done = split_wait([sent_down, sent_up, sent_attn, sent_out, sent_in], tok, name="grads_wait", scatter=True)
    slots = dict(w_down=(0, 0), w_up=(1, 0), xw_o=(2, 0), xw_q=(2, 1), xw_kv=(2, 2), w_out=(3, 0), w_in=(4, 0))
    own = {n: done[gi][0][ai] for n, (gi, ai) in slots.items()}
    got = {n: done[gi][1][ai] for n, (gi, ai) in slots.items()}
    res = {}
    for n in BIG:
        shp = w[n].shape
        r = adamw_sharded(me1, own[n], got[n], w[n][0], mom[n][0], var[n][0], name="adamw_" + n,
                          tr=min(256, shp[1]))
        for kind, a in zip("gdmv", r):
            res[kind, n] = a.reshape(shp)
    (_, small_parts), = split_wait(small_started, res["g", BIG[-1]], name="small_grads_wait", scatter=False)
    loss = 0.5 * jnp.sum(small_parts[1][:, 5, :]) / D
    r = adamw_replicated(small_parts, [w[n].reshape(v2) for n, v2, _, _ in SMALL],
                         [mom[n].reshape(v2) for n, v2, _, _ in SMALL],
                         [var[n].reshape(v2) for n, v2, _, _ in SMALL],
                         [(b, r0, v2[0]) for _, v2, b, r0 in SMALL])
    for kind, arrs in zip("gdmv", r):
        for (n, _, _, _), a in zip(SMALL, arrs):
            res[kind, n] = a.reshape(w[n].shape)

    out = [loss, dx.reshape(x.shape)]
    for kind in "gdmv":
        out += [res[kind, n] for n in WEIGHTS]
    return tuple(out)
```

```python
import jax
import jax.numpy as jnp
from jax import lax
from jax.experimental import pallas as pl
from jax.experimental.pallas import tpu as pltpu

F32 = jnp.float32
BF16 = jnp.bfloat16
EPS = 1e-6
CHUNK = 64
POOL_HALO = 16
HEAD_W = 128
HEADS_PER_STEP = 4
XATTN_HEADS = 4
N_DEV = 8
N_PEERS = N_DEV - 1
ADAM_LR = 0.001
ADAM_B1 = 0.9
ADAM_B2 = 0.999
ADAM_EPS = 1e-08
ADAM_WD = 0.01
ADAM_STEP = 10
V7X_VMEM_LIMIT = 52 * 1024 * 1024
MESH = pl.DeviceIdType.MESH
HBM = pl.BlockSpec(memory_space=pltpu.HBM)
SEM = pl.BlockSpec(memory_space=pltpu.SEMAPHORE)


def _cparams(dims):
    return pltpu.CompilerParams(dimension_semantics=dims, vmem_limit_bytes=V7X_VMEM_LIMIT)


def _sigmoid(v):
    return 0.5 * jnp.tanh(0.5 * v) + 0.5


def _dot(a, b):
    return jnp.dot(a, b, preferred_element_type=F32)


def _dot_nt(a, b):
    return lax.dot_general(a, b, (((1,), (1,)), ((), ())), preferred_element_type=F32)


def _dot_tn(a, b):
    return lax.dot_general(a, b, (((0,), (0,)), ((), ())), preferred_element_type=F32)


def _split3(v):
    hi = v.astype(BF16)
    r1 = v - hi.astype(F32)
    mid = r1.astype(BF16)
    lo = (r1 - mid.astype(F32)).astype(BF16)
    return hi, mid, lo


def _tri_apply(tri, v):
    hi, mid, lo = _split3(v)
    return _dot(tri, hi) + _dot(tri, mid) + _dot(tri, lo)


def _mat_shape(a):
    return a.shape if a.ndim == 2 else (a.shape[1], a.shape[0] * a.shape[2])


def _tile_spec(a, rows, cols, row_of, col_of):
    if a.ndim == 2:
        return pl.BlockSpec((rows, cols), lambda *g: (row_of(*g), col_of(*g)))
    per = a.shape[2] // cols
    return pl.BlockSpec((None, rows, cols), lambda *g: (col_of(*g) // per, row_of(*g), col_of(*g) % per))


def _out_struct(rows, n, slabs, dtype):
    return jax.ShapeDtypeStruct((rows, n) if slabs is None else (slabs, rows, n // slabs), dtype)


def norm_mm(h, g, w, *, name, tm, tn, out_dtype, out_slabs=None):
    T, D = h.shape
    N = _mat_shape(w)[1]
    o_shape = _out_struct(T, N, out_slabs, out_dtype)

    def body(h_ref, g_ref, w_ref, o_ref, n_ref):
        @pl.when(pl.program_id(1) == 0)
        def _():
            x = h_ref[...]
            r = lax.rsqrt(jnp.mean(x * x, axis=-1, keepdims=True) + EPS)
            n_ref[...] = (x * r * g_ref[...]).astype(BF16)

        o_ref[...] = _dot(n_ref[...], w_ref[...]).astype(o_ref.dtype)

    return pl.pallas_call(
        body, name=name, grid=(T // tm, N // tn),
        in_specs=[pl.BlockSpec((tm, D), lambda i, j: (i, 0)),
                  pl.BlockSpec((1, D), lambda i, j: (0, 0)),
                  _tile_spec(w, D, tn, lambda i, j: 0, lambda i, j: j)],
        out_specs=[_tile_spec(o_shape, tm, tn, lambda i, j: i, lambda i, j: j),
                   pl.BlockSpec((tm, D), lambda i, j: (i, 0))],
        out_shape=[o_shape, jax.ShapeDtypeStruct((T, D), BF16)],
        compiler_params=_cparams(("parallel", "arbitrary")),
    )(h, g, w)


def mm_nn(a, w, res, *, name, tm, tn, tk, relu2=False):
    T, K = _mat_shape(a)
    N = w.shape[1]
    nk = K // tk

    def body(a_ref, w_ref, r_ref, o_ref, acc_ref):
        k = pl.program_id(2)
        av = a_ref[...]
        if relu2:
            av = jnp.maximum(av, 0.0)
            av = av * av
        part = _dot(av.astype(BF16), w_ref[...])

        @pl.when(k == 0)
        def _():
            acc_ref[...] = part

        @pl.when(k > 0)
        def _():
            acc_ref[...] += part

        @pl.when(k == nk - 1)
        def _():
            o_ref[...] = r_ref[...] + acc_ref[...]

    return pl.pallas_call(
        body, name=name, grid=(T // tm, N // tn, nk),
        in_specs=[_tile_spec(a, tm, tk, lambda i, j, k: i, lambda i, j, k: k),
                  pl.BlockSpec((tk, tn), lambda i, j, k: (k, j)),
                  pl.BlockSpec((tm, tn), lambda i, j, k: (i, j))],
        out_specs=pl.BlockSpec((tm, tn), lambda i, j, k: (i, j)),
        out_shape=jax.ShapeDtypeStruct((T, N), F32),
        scratch_shapes=[pltpu.VMEM((tm, tn), F32)],
        compiler_params=_cparams(("parallel", "parallel", "arbitrary")),
    )(a, w, res)


def mm_nt(a, w, *, name, tm, tn, tk, out_dtype, out_slabs=None, relu2_of=None, after=None):
    T, K = _mat_shape(a)
    nk = K // tk
    N = w.shape[0]
    has_z = relu2_of is not None
    o_shape = _out_struct(T, N, out_slabs, out_dtype)

    def body(*refs):
        a_ref, w_ref = refs[0], refs[1]
        z_ref = refs[2] if has_z else None
        o_ref, acc_ref = refs[-2], refs[-1]
        k = pl.program_id(2)
        part = _dot_nt(a_ref[...].astype(BF16), w_ref[...])

        @pl.when(k == 0)
        def _():
            acc_ref[...] = part

        @pl.when(k > 0)
        def _():
            acc_ref[...] += part

        @pl.when(k == nk - 1)
        def _():
            out = acc_ref[...]
            if has_z:
                out = out * (2.0 * jnp.maximum(z_ref[...], 0.0))
            o_ref[...] = out.astype(o_ref.dtype)

    in_specs = [_tile_spec(a, tm, tk, lambda i, j, k: i, lambda i, j, k: k),
                pl.BlockSpec((tn, tk), lambda i, j, k: (j, k))]
    args = [a, w]
    if has_z:
        in_specs.append(pl.BlockSpec((tm, tn), lambda i, j, k: (i, j)))
        args.append(relu2_of)
    if after is not None:
        in_specs.append(pl.BlockSpec(after.shape, lambda i, j, k: (0, 0)))
        args.append(after)
    return pl.pallas_call(
        body, name=name, grid=(T // tm, N // tn, nk),
        in_specs=in_specs,
        out_specs=_tile_spec(o_shape, tm, tn, lambda i, j, k: i, lambda i, j, k: j),
        out_shape=o_shape,
        scratch_shapes=[pltpu.VMEM((tm, tn), F32)],
        compiler_params=_cparams(("parallel", "parallel", "arbitrary")),
    )(*args)


def mm_nt_normbwd(a, w, h, g, dres, *, name, tm, tk, after=None):
    T, K = _mat_shape(a)
    nk = K // tk
    D = h.shape[1]
    with_dh = dres is not None

    def body(*refs):
        a_ref, w_ref, h_ref, g_ref = refs[:4]
        if with_dh:
            r_ref = refs[4]
            dh_ref, dhb_ref, dg_ref, acc_ref = refs[-4:]
        else:
            dg_ref, acc_ref = refs[-2:]
        i = pl.program_id(0)
        k = pl.program_id(1)
        part = _dot_nt(a_ref[...].astype(BF16), w_ref[...])

        @pl.when(k == 0)
        def _():
            acc_ref[...] = part

        @pl.when(k > 0)
        def _():
            acc_ref[...] += part

        @pl.when(k == nk - 1)
        def _():
            dn = acc_ref[...]
            x = h_ref[...]
            r = lax.rsqrt(jnp.mean(x * x, axis=-1, keepdims=True) + EPS)
            xr = x * r
            dgp = jnp.sum(dn * xr, axis=0, keepdims=True)

            @pl.when(i == 0)
            def _():
                dg_ref[...] = dgp

            @pl.when(i > 0)
            def _():
                dg_ref[...] += dgp

            if with_dh:
                dyg = dn * g_ref[...]
                dx = r * (dyg - xr * jnp.mean(dyg * xr, axis=-1, keepdims=True))
                out = r_ref[...] + dx
                dh_ref[...] = out
                dhb_ref[...] = out.astype(BF16)

    row = pl.BlockSpec((tm, D), lambda i, k: (i, 0))
    vec = pl.BlockSpec((1, D), lambda i, k: (0, 0))
    in_specs = [_tile_spec(a, tm, tk, lambda i, k: i, lambda i, k: k),
                _tile_spec(w, D, tk, lambda i, k: 0, lambda i, k: k), row, vec]
    args = [a, w, h, g]
    if with_dh:
        in_specs.append(row)
        args.append(dres)
        out_specs = [row, row, vec]
        out_shape = [jax.ShapeDtypeStruct((T, D), F32), jax.ShapeDtypeStruct((T, D), BF16),
                     jax.ShapeDtypeStruct((1, D), F32)]
    else:
        out_specs = vec
        out_shape = jax.ShapeDtypeStruct((1, D), F32)
    if after is not None:
        in_specs.append(pl.BlockSpec(after.shape, lambda i, k: (0, 0)))
        args.append(after)
    return pl.pallas_call(
        body, name=name, grid=(T // tm, nk),
        in_specs=in_specs, out_specs=out_specs, out_shape=out_shape,
        scratch_shapes=[pltpu.VMEM((tm, D), F32)],
        compiler_params=_cparams(("arbitrary", "arbitrary")),
    )(*args)


def mm_tn(a, b, *, name, tt, tko, tn, relu2=False, out_slabs=None):
    T, K = _mat_shape(a)
    N = _mat_shape(b)[1]
    nt = T // tt
    o_shape = _out_struct(K, N, out_slabs, BF16)

    def body(a_ref, b_ref, o_ref, acc_ref):
        t = pl.program_id(2)
        av = a_ref[...]
        if relu2:
            av = jnp.maximum(av, 0.0)
            av = av * av
        part = _dot_tn(av.astype(BF16), b_ref[...].astype(BF16))

        @pl.when(t == 0)
        def _():
            acc_ref[...] = part

        @pl.when(t > 0)
        def _():
            acc_ref[...] += part

        @pl.when(t == nt - 1)
        def _():
            o_ref[...] = acc_ref[...].astype(BF16)

    return pl.pallas_call(
        body, name=name, grid=(K // tko, N // tn, nt),
        in_specs=[_tile_spec(a, tt, tko, lambda kk, j, t: t, lambda kk, j, t: kk),
                  _tile_spec(b, tt, tn, lambda kk, j, t: t, lambda kk, j, t: j)],
        out_specs=_tile_spec(o_shape, tko, tn, lambda kk, j, t: kk, lambda kk, j, t: j),
        out_shape=o_shape,
        scratch_shapes=[pltpu.VMEM((tko, tn), F32)],
        compiler_params=_cparams(("parallel", "parallel", "arbitrary")),
    )(a, b)


def _resident(a):
    nd = a.ndim
    return pl.BlockSpec(a.shape, lambda i: (0,) * nd, pipeline_mode=pl.Buffered(1))


def _row_block(a, tm):
    if a.ndim == 2:
        return pl.BlockSpec((tm, a.shape[1]), lambda i: (i, 0))
    return pl.BlockSpec((a.shape[0], tm, a.shape[2]), lambda i: (0, i, 0))


def _cols(ref, c, width):
    if len(ref.shape) == 2:
        return ref[:, c * width:(c + 1) * width]
    per = ref.shape[2] // width
    if per == 1:
        return ref[c]
    return ref[c // per, :, (c % per) * width:(c % per + 1) * width]


def _set_cols(ref, c, width, val):
    if len(ref.shape) == 2:
        ref[:, c * width:(c + 1) * width] = val
        return
    per = ref.shape[2] // width
    if per == 1:
        ref[c] = val
    else:
        ref[c // per, :, (c % per) * width:(c % per + 1) * width] = val


def _all_cols(ref):
    if len(ref.shape) == 2:
        return ref[...]
    return jnp.concatenate([ref[s] for s in range(ref.shape[0])], axis=1)


def _rms(x):
    return lax.rsqrt(jnp.mean(x * x, axis=-1, keepdims=True) + EPS)


def _row_params():
    return _cparams(("arbitrary",))


def proj_norm(h, g, w, *, name, tm, tn, out_dtype, out_slabs=None):
    T, D = h.shape
    N = _mat_shape(w)[1]
    o_shape = _out_struct(T, N, out_slabs, out_dtype)

    def body(h_ref, g_ref, w_ref, o_ref, n_ref):
        x = h_ref[...]
        n = (x * _rms(x) * g_ref[...]).astype(BF16)
        n_ref[...] = n
        for c in range(N // tn):
            _set_cols(o_ref, c, tn, _dot(n, _cols(w_ref, c, tn)).astype(out_dtype))

    return pl.pallas_call(
        body, name=name, grid=(T // tm,),
        in_specs=[_row_block(h, tm), pl.BlockSpec((1, D), lambda i: (0, 0)), _resident(w)],
        out_specs=[_row_block(o_shape, tm), pl.BlockSpec((tm, D), lambda i: (i, 0))],
        out_shape=[o_shape, jax.ShapeDtypeStruct((T, D), BF16)],
        compiler_params=_row_params(),
    )(h, g, w)


def proj_plain(a, w, *, name, tm, tn, relu2=False):
    T = a.shape[0]
    N = _mat_shape(w)[1]

    def body(a_ref, w_ref, o_ref):
        av = a_ref[...]
        for c in range(N // tn):
            z = _dot(av, _cols(w_ref, c, tn))
            if relu2:
                z = jnp.maximum(z, 0.0)
                z = z * z
            _set_cols(o_ref, c, tn, z.astype(BF16))

    o_shape = jax.ShapeDtypeStruct((T, N), BF16)
    return pl.pallas_call(
        body, name=name, grid=(T // tm,),
        in_specs=[_row_block(a, tm), _resident(w)],
        out_specs=_row_block(o_shape, tm), out_shape=o_shape,
        compiler_params=_row_params(),
    )(a, w)


def proj_res_norm(a, w, res, g, *, name, tm, tn):
    T = res.shape[0]
    D = w.shape[1]

    def body(a_ref, w_ref, r_ref, g_ref, h_ref, n_ref):
        av = _all_cols(a_ref)
        for c in range(D // tn):
            sl = slice(c * tn, (c + 1) * tn)
            h_ref[:, sl] = r_ref[:, sl] + _dot(av, w_ref[:, sl])
        hv = h_ref[...]
        n_ref[...] = (hv * _rms(hv) * g_ref[...]).astype(BF16)

    row = pl.BlockSpec((tm, D), lambda i: (i, 0))
    return pl.pallas_call(
        body, name=name, grid=(T // tm,),
        in_specs=[_row_block(a, tm), _resident(w), row, pl.BlockSpec((1, D), lambda i: (0, 0))],
        out_specs=[row, row],
        out_shape=[jax.ShapeDtypeStruct((T, D), F32), jax.ShapeDtypeStruct((T, D), BF16)],
        compiler_params=_row_params(),
    )(a, w, res, g)


def proj_res_loss(a, w, res, g, target, *, name, tm, tn):
    T = res.shape[0]
    D = w.shape[1]

    def body(a_ref, w_ref, r_ref, g_ref, t_ref, dh_ref, dhb_ref, ls_ref, dg_ref):
        i = pl.program_id(0)
        av = a_ref[...]
        for c in range(D // tn):
            sl = slice(c * tn, (c + 1) * tn)
            dh_ref[:, sl] = r_ref[:, sl] + _dot(av, w_ref[:, sl])
        x = dh_ref[...]
        gv = g_ref[...]
        r = _rms(x)
        xr = x * r
        d = xr * gv - t_ref[...]
        dy = d * (1.0 / D)
        dyg = dy * gv
        dx = r * (dyg - xr * jnp.mean(dyg * xr, axis=-1, keepdims=True))
        dh_ref[...] = dx
        dhb_ref[...] = dx.astype(BF16)
        ls = jnp.sum(d * d, axis=0, keepdims=True)
        dg = jnp.sum(dy * xr, axis=0, keepdims=True)

        @pl.when(i == 0)
        def _():
            ls_ref[...] = ls
            dg_ref[...] = dg

        @pl.when(i > 0)
        def _():
            ls_ref[...] += ls
            dg_ref[...] += dg

    row = pl.BlockSpec((tm, D), lambda i: (i, 0))
    vec = pl.BlockSpec((1, D), lambda i: (0, 0))
    return pl.pallas_call(
        body, name=name, grid=(T // tm,),
        in_specs=[_row_block(a, tm), _resident(w), row, vec, row],
        out_specs=[row, row, vec, vec],
        out_shape=[jax.ShapeDtypeStruct((T, D), F32), jax.ShapeDtypeStruct((T, D), BF16),
                   jax.ShapeDtypeStruct((1, D), F32), jax.ShapeDtypeStruct((1, D), F32)],
        compiler_params=_row_params(),
    )(a, w, res, g, target)


def _anchor_spec(after):
    return pl.BlockSpec(after.shape, lambda i: (0, 0))


def back_plain(a, w, *, name, tm, tn, out_dtype, out_slabs=None, relu2_value=None, after=None):
    T = a.shape[0]
    N = w.shape[0]
    has_z = relu2_value is not None
    o_shape = _out_struct(T, N, out_slabs, out_dtype)

    def body(*refs):
        a_ref, w_ref = refs[0], refs[1]
        o_ref = refs[-1]
        av = a_ref[...]
        for c in range(N // tn):
            out = _dot_nt(av, w_ref[c * tn:(c + 1) * tn, :])
            if has_z:
                out = out * (2.0 * jnp.sqrt(refs[2][:, c * tn:(c + 1) * tn]).astype(F32))
            _set_cols(o_ref, c, tn, out.astype(out_dtype))

    in_specs, args = [_row_block(a, tm), _resident(w)], [a, w]
    if has_z:
        in_specs.append(_row_block(relu2_value, tm))
        args.append(relu2_value)
    if after is not None:
        in_specs.append(_anchor_spec(after))
        args.append(after)
    return pl.pallas_call(
        body, name=name, grid=(T // tm,),
        in_specs=in_specs, out_specs=_row_block(o_shape, tm), out_shape=o_shape,
        compiler_params=_row_params(),
    )(*args)


def back_norm(a, w, h, g, dres, *, name, tm, tk, bf16_copy=True, after=None):
    T, K = _mat_shape(a)
    D = h.shape[1]
    with_dh = dres is not None

    def body(*refs):
        a_ref, w_ref, h_ref, g_ref = refs[:4]
        i = pl.program_id(0)
        dn = None
        for kc in range(K // tk):
            part = _dot_nt(_cols(a_ref, kc, tk).astype(BF16), _cols(w_ref, kc, tk))
            dn = part if dn is None else dn + part
        x = h_ref[...]
        r = _rms(x)
        xr = x * r
        dgp = jnp.sum(dn * xr, axis=0, keepdims=True)
        dg_ref = refs[-1]

        @pl.when(i == 0)
        def _():
            dg_ref[...] = dgp

        @pl.when(i > 0)
        def _():
            dg_ref[...] += dgp

        if with_dh:
            dyg = dn * g_ref[...]
            out = refs[4][...] + r * (dyg - xr * jnp.mean(dyg * xr, axis=-1, keepdims=True))
            if bf16_copy:
                refs[-3][...] = out
                refs[-2][...] = out.astype(BF16)
            else:
                refs[-2][...] = out

    row = pl.BlockSpec((tm, D), lambda i: (i, 0))
    vec = pl.BlockSpec((1, D), lambda i: (0, 0))
    in_specs, args = [_row_block(a, tm), _resident(w), row, vec], [a, w, h, g]
    if with_dh:
        in_specs.append(row)
        args.append(dres)
        out_specs = [row, row, vec] if bf16_copy else [row, vec]
        out_shape = [jax.ShapeDtypeStruct((T, D), F32)]
        if bf16_copy:
            out_shape.append(jax.ShapeDtypeStruct((T, D), BF16))
        out_shape.append(jax.ShapeDtypeStruct((1, D), F32))
    else:
        out_specs = vec
        out_shape = jax.ShapeDtypeStruct((1, D), F32)
    if after is not None:
        in_specs.append(_anchor_spec(after))
        args.append(after)
    return pl.pallas_call(
        body, name=name, grid=(T // tm,),
        in_specs=in_specs, out_specs=out_specs, out_shape=out_shape,
        compiler_params=_row_params(),
    )(*args)


def wgrad(a, b, *, name, tt, tn, out_slabs=None):
    T, K = _mat_shape(a)
    N = _mat_shape(b)[1]
    nt = T // tt
    o_shape = _out_struct(K, N, out_slabs, BF16)

    def body(a_ref, b_ref, o_ref, acc_ref):
        t = pl.program_id(0)
        at = _all_cols(a_ref).astype(BF16).T

        @pl.when(t == 0)
        def _():
            acc_ref[...] = jnp.zeros_like(acc_ref)

        for c in range(N // tn):
            acc_ref[:, c * tn:(c + 1) * tn] += _dot(at, _cols(b_ref, c, tn).astype(BF16))

        @pl.when(t == nt - 1)
        def _():
            for c in range(N // tn):
                _set_cols(o_ref, c, tn, acc_ref[:, c * tn:(c + 1) * tn].astype(BF16))

    return pl.pallas_call(
        body, name=name, grid=(nt,),
        in_specs=[_row_block(a, tt), _row_block(b, tt)],
        out_specs=_resident(o_shape), out_shape=o_shape,
        scratch_shapes=[pltpu.VMEM((K, N), F32)],
        compiler_params=_row_params(),
    )(a, b)


def chunk_triangles(tm):
    r = lax.broadcasted_iota(jnp.int32, (tm, tm), 0)
    c = lax.broadcasted_iota(jnp.int32, (tm, tm), 1)
    same = (r // CHUNK) == (c // CHUNK)
    tri = jnp.stack([same & (c <= r), same & (c >= r)]).astype(F32)
    return tri.astype(BF16), tri


def _tri_spec(tm):
    return pl.BlockSpec((2, tm, tm), lambda g, s, i: (0, 0, 0))


def _chunk_row(v, r, nc):
    return jnp.concatenate([jnp.broadcast_to(v[c * CHUNK + r:c * CHUNK + r + 1], (CHUNK, v.shape[1]))
                            for c in range(nc)], axis=0)


def _block_diag(v, nc):
    chunk = lax.broadcasted_iota(jnp.int32, (v.shape[0], 1), 0) // CHUNK
    return jnp.concatenate([jnp.where(chunk == c, v, jnp.zeros_like(v)) for c in range(nc)], axis=1)


def _pool_windows_back(ext_ref, tm):
    n = tm + 32
    ext_ref[1, 8:n] = ext_ref[0, 8:n] + ext_ref[0, 7:n - 1]
    ext_ref[2, 16:n] = ext_ref[1, 16:n] + ext_ref[1, 14:n - 2]
    ext_ref[3, 24:n] = ext_ref[2, 24:n] + ext_ref[2, 20:n - 4]
    s2 = ext_ref[1, 32:n]
    s4 = ext_ref[2, 32:n]
    s8 = ext_ref[3, 32:n]
    s16 = s8 + ext_ref[3, 24:n - 8]
    return s2, s4, s8, s16


def _pool_windows_fwd(ext_ref, tm):
    n = tm + 32
    ext_ref[1, 0:n - 8] = ext_ref[0, 0:n - 8] + ext_ref[0, 1:n - 7]
    ext_ref[2, 0:n - 16] = ext_ref[1, 0:n - 16] + ext_ref[1, 2:n - 14]
    ext_ref[3, 0:n - 24] = ext_ref[2, 0:n - 24] + ext_ref[2, 4:n - 20]
    s2 = ext_ref[1, 0:tm]
    s4 = ext_ref[2, 0:tm]
    s8 = ext_ref[3, 0:tm]
    s16 = s8 + ext_ref[3, 8:tm + 8]
    return s2, s4, s8, s16


def _select_window(g, s2, s4, s8, s16):
    return jnp.where(g == 0, s2, jnp.where(g == 1, s4, jnp.where(g == 2, s8, s16)))


def _pool_count(g, pos):
    width = lax.shift_left(jnp.int32(2), g)
    return jnp.minimum(pos + 1, width).astype(F32)


def _hgrn_gates(zq, zf, th):
    lb = _sigmoid(th[0:1, :] - th[1:2, :])
    sig = _sigmoid(zf)
    f = lb + (1.0 - lb) * sig
    sq = _sigmoid(zq)
    return lb, sig, f, sq


def mixer_fwd(u5, pool_w_bf, scale4, theta4, gn4, tri_bf, tri_f, *, seqs, seq_len, tm):
    T = u5.shape[1]
    tps = seq_len // tm
    nc = tm // CHUNK
    W = HEAD_W

    H = HEADS_PER_STEP
    heads = range(H)

    def body(u_ref, pw_ref, sc_ref, th_ref, gn_ref, tri_ref, msk_ref, y_ref, o_ref, st_ref, halo_ref, ext_ref, s_ref):
        g = pl.program_id(0)
        i = pl.program_id(2)

        @pl.when(i == 0)
        def _():
            halo_ref[...] = jnp.zeros_like(halo_ref)
            s_ref[...] = jnp.zeros_like(s_ref)

        row = lax.broadcasted_iota(jnp.int32, (tm, 1), 0)
        cols = [slice(h * W, (h + 1) * W) for h in heads]

        for h in heads:
            grp = g * H + h
            up = u_ref[0, :, cols[h]]
            ext_ref[h, 0, 0:16] = jnp.zeros((16, W), F32)
            ext_ref[h, 0, 16:32] = halo_ref[h]
            ext_ref[h, 0, 32:32 + tm] = up
            win = _select_window(grp, *_pool_windows_back(ext_ref.at[h], tm))
            p = win * (1.0 / _pool_count(grp, i * tm + row)) - up
            halo_ref[h] = up[tm - POOL_HALO:tm]
            y_ref[0, :, cols[h]] = (_dot(p.astype(BF16), pw_ref[h]) * sc_ref[h]).astype(BF16)

        zq, zf, zi, zg = u_ref[1], u_ref[2], u_ref[3], u_ref[4]
        th = [th_ref[h] for h in heads]
        lb = jnp.concatenate([_sigmoid(t[0:1, :] - t[1:2, :]) for t in th], axis=1)
        f = lb + (1.0 - lb) * _sigmoid(zf)
        kk = 1.0 - f
        q = zq * _sigmoid(zq)
        G = _tri_apply(tri_ref[0], jnp.log(f))
        Gm, Gl = _chunk_row(G, CHUNK // 2 - 1, nc), _chunk_row(G, CHUNK - 1, nc)
        vb = zi.astype(BF16)
        qrb = (q * jnp.exp(G - Gm)).astype(BF16)
        krb = (kk * jnp.exp(Gm - G)).astype(BF16)
        keb = (kk * jnp.exp(Gl - G)).astype(BF16)
        qgb = (q * jnp.exp(G)).astype(BF16)
        mask = msk_ref[0] > 0.5
        a = [jnp.where(mask, _dot_nt(qrb[:, cols[h]], krb[:, cols[h]]), 0.0).astype(BF16) for h in heads]
        d_st = [_dot_tn(vb[:, cols[h]], _block_diag(keb[:, cols[h]], nc)) for h in heads]
        o_intra = [_dot(a[h], vb[:, cols[h]]) for h in heads]
        st_cat = []
        for h in heads:
            st = s_ref[h]
            states = []
            for c in range(nc):
                st_ref[c, h] = st
                states.append(st.astype(BF16))
                st = st * jnp.exp(G[(c + 1) * CHUNK - 1:(c + 1) * CHUNK, cols[h]]) + d_st[h][:, c * W:(c + 1) * W]
            s_ref[h] = st
            st_cat.append(jnp.concatenate(states, axis=1))
        o = [o_intra[h] + _dot_nt(_block_diag(qgb[:, cols[h]], nc), st_cat[h]) for h in heads]
        gate = zg * _sigmoid(zg)
        for h in heads:
            o_ref[:, cols[h]] = o[h]
            r = lax.rsqrt(jnp.mean(o[h] * o[h], axis=-1, keepdims=True) + EPS)
            y_ref[1, :, cols[h]] = (o[h] * r * gn_ref[h] * gate[:, cols[h]]).astype(BF16)

    def rb(s, i):
        return s * tps + i

    def per_head(*shape):
        return pl.BlockSpec((H,) + shape, lambda g, s, i: (g,) + (0,) * len(shape))

    return pl.pallas_call(
        body, name="mixer_fwd", grid=(4 // H, seqs, tps),
        in_specs=[pl.BlockSpec((5, tm, H * W), lambda g, s, i: (0, rb(s, i), g)),
                  per_head(W, W), per_head(1, W), per_head(2, W), per_head(1, W),
                  _tri_spec(tm), _tri_spec(tm)],
        out_specs=[pl.BlockSpec((2, tm, H * W), lambda g, s, i: (0, rb(s, i), g)),
                   pl.BlockSpec((tm, H * W), lambda g, s, i: (rb(s, i), g)),
                   pl.BlockSpec((nc, H, W, W), lambda g, s, i: (rb(s, i), g, 0, 0))],
        out_shape=[jax.ShapeDtypeStruct((2, T, 4 * W), BF16),
                   jax.ShapeDtypeStruct((T, 4 * W), F32),
                   jax.ShapeDtypeStruct((T // CHUNK, 4, W, W), F32)],
        scratch_shapes=[pltpu.VMEM((H, POOL_HALO, W), F32),
                        pltpu.VMEM((H, 4, tm + 32, W), F32),
                        pltpu.VMEM((H, W, W), F32)],
        compiler_params=_cparams(("arbitrary", "arbitrary", "arbitrary")),
    )(u5, pool_w_bf, scale4, theta4, gn4, tri_bf, tri_f)


def mixer_bwd(u5, dy2, o_pre, st_prev, pool_w_bf, scale4, theta4, gn4, tri_bf, tri_f, *, seqs, seq_len, tm):
    T = u5.shape[1]
    tps = seq_len // tm
    nc = tm // CHUNK
    W = HEAD_W
    hb = tm // POOL_HALO

    H = HEADS_PER_STEP
    heads = range(H)

    def body(u_ref, uh_ref, dy_ref, o_ref, st_ref, pw_ref, sc_ref, th_ref, gn_ref, tri_ref, msk_ref,
             du_ref, dpw_ref, dsc_ref, dlb_ref, dgn_ref, nxt_ref, ext_ref, ds_ref):
        g = pl.program_id(0)
        s = pl.program_id(1)
        i = pl.program_id(2)
        tile = tps - 1 - i
        first = (s == 0) & (i == 0)

        @pl.when(i == 0)
        def _():
            nxt_ref[...] = jnp.zeros_like(nxt_ref)
            ds_ref[...] = jnp.zeros_like(ds_ref)

        row = lax.broadcasted_iota(jnp.int32, (tm, 1), 0)
        cols = [slice(h * W, (h + 1) * W) for h in heads]

        def accumulate(ref, h, val):
            @pl.when(first)
            def _():
                ref[h] = val

            @pl.when(jnp.logical_not(first))
            def _():
                ref[h] += val

        def per_head(fn):
            return jnp.concatenate([jnp.broadcast_to(fn(cols[h]), (tm, W)) for h in heads], axis=1)

        for h in heads:
            grp = g * H + h
            inv_cnt = 1.0 / _pool_count(grp, tile * tm + row)
            ext = ext_ref.at[h]
            up = u_ref[0, :, cols[h]]
            ext[0, 0:16] = jnp.zeros((16, W), F32)
            ext[0, 16:32] = jnp.where(tile == 0, 0.0, uh_ref[:, cols[h]])
            ext[0, 32:32 + tm] = up
            win = _select_window(grp, *_pool_windows_back(ext, tm))
            pb = (win * inv_cnt - up).astype(BF16)
            dyp = dy_ref[0, :, cols[h]]
            z = _dot(pb, pw_ref[h])
            accumulate(dsc_ref, h, jnp.sum(dyp * z, axis=0, keepdims=True))
            dz = (dyp * sc_ref[h]).astype(BF16)
            accumulate(dpw_ref, h, _dot_tn(pb, dz))
            dp = _dot_nt(dz, pw_ref[h])
            e = dp * inv_cnt
            ext[0, 0:tm] = e
            ext[0, tm:tm + 16] = nxt_ref[h]
            ext[0, tm + 16:tm + 32] = jnp.zeros((16, W), F32)
            lead = _select_window(grp, *_pool_windows_fwd(ext, tm))
            nxt_ref[h] = e[0:POOL_HALO]
            du_ref[0, :, cols[h]] = (lead - dp).astype(BF16)

        zq, zf, zi, zg = u_ref[1], u_ref[2], u_ref[3], u_ref[4]
        lb = jnp.concatenate([_sigmoid(th_ref[h][0:1, :] - th_ref[h][1:2, :]) for h in heads], axis=1)
        gn = jnp.concatenate([gn_ref[h] for h in heads], axis=1)
        sig, sq, sg = _sigmoid(zf), _sigmoid(zq), _sigmoid(zg)
        f = lb + (1.0 - lb) * sig
        kk = 1.0 - f
        q = zq * sq
        G = _tri_apply(tri_ref[0], jnp.log(f))

        dyh = dy_ref[1]
        o = o_ref[...]
        sqr = o * o
        r = per_head(lambda cs: lax.rsqrt(jnp.mean(sqr[:, cs], axis=-1, keepdims=True) + EPS))
        orr = o * r
        du_ref[4] = (dyh * (orr * gn) * (sg * (1.0 + zg * (1.0 - sg)))).astype(BF16)
        don = dyh * (zg * sg)
        dgn = jnp.sum(don * orr, axis=0, keepdims=True)
        dog = don * gn
        dog_orr = dog * orr
        do = r * (dog - orr * per_head(lambda cs: jnp.mean(dog_orr[:, cs], axis=-1, keepdims=True)))

        Gm, Gl = _chunk_row(G, CHUNK // 2 - 1, nc), _chunk_row(G, CHUNK - 1, nc)
        e_q, e_k, e_e, e_g = jnp.exp(G - Gm), jnp.exp(Gm - G), jnp.exp(Gl - G), jnp.exp(G)
        qr, kr, ke, qg = q * e_q, kk * e_k, kk * e_e, q * e_g
        qrb, krb, keb, qgb = qr.astype(BF16), kr.astype(BF16), ke.astype(BF16), qg.astype(BF16)
        vb = zi.astype(BF16)
        dob = do.astype(BF16)
        lower, upper = msk_ref[0] > 0.5, msk_ref[1] > 0.5
        da = [jnp.where(lower, _dot_nt(dob[:, cs], vb[:, cs]), 0.0).astype(BF16) for cs in cols]
        a_t = [jnp.where(upper, _dot_nt(krb[:, cs], qrb[:, cs]), 0.0).astype(BF16) for cs in cols]
        da_t = [jnp.where(upper, _dot_nt(vb[:, cs], dob[:, cs]), 0.0).astype(BF16) for cs in cols]
        u_cat = [_dot_tn(dob[:, cs], _block_diag(qgb[:, cs], nc)) for cs in cols]
        dqr = [_dot(da[h], krb[:, cols[h]]) for h in heads]
        dkr = [_dot(da_t[h], qrb[:, cols[h]]) for h in heads]
        dv = [_dot(a_t[h], dob[:, cols[h]]) for h in heads]
        dsn_rows, dsn_cols, ddecay = [], [], [[None] * H for _ in range(nc)]
        for h in heads:
            dsn = ds_ref[h]
            dsn_b = [None] * nc
            for c in reversed(range(nc)):
                decay = jnp.exp(G[(c + 1) * CHUNK - 1:(c + 1) * CHUNK, cols[h]])
                dsn_b[c] = dsn.astype(BF16)
                ddecay[c][h] = jnp.sum(dsn * st_ref[c, h], axis=0, keepdims=True) * decay
                dsn = u_cat[h][:, c * W:(c + 1) * W] + dsn * decay
            ds_ref[h] = dsn
            dsn_rows.append(jnp.concatenate(dsn_b, axis=0))
            dsn_cols.append(jnp.concatenate(dsn_b, axis=1))
        st_rows = [jnp.concatenate([st_ref[c, h].astype(BF16) for c in range(nc)], axis=0) for h in heads]
        dqg = [_dot(_block_diag(dob[:, cols[h]], nc), st_rows[h]) for h in heads]
        dke = [_dot(_block_diag(vb[:, cols[h]], nc), dsn_rows[h]) for h in heads]
        dv = [dv[h] + _dot_nt(_block_diag(keb[:, cols[h]], nc), dsn_cols[h]) for h in heads]
        dqr, dkr, dqg, dke, dv = (jnp.concatenate(parts, axis=1) for parts in (dqr, dkr, dqg, dke, dv))
        t_mid, t_qg, t_ke = dkr * kr - dqr * qr, dqg * qg, dke * ke
        dq = dqr * e_q + dqg * e_g
        dk = dkr * e_k + dke * e_e
        crow = lax.broadcasted_iota(jnp.int32, (CHUNK, 1), 0)
        ends = []
        for c in range(nc):
            sl = slice(c * CHUNK, (c + 1) * CHUNK)
            dgm = jnp.sum(t_mid[sl], axis=0, keepdims=True)
            dgl = jnp.sum(t_ke[sl], axis=0, keepdims=True) + jnp.concatenate(ddecay[c], axis=1)
            ends.append(jnp.where(crow == CHUNK // 2 - 1, dgm, 0.0) + jnp.where(crow == CHUNK - 1, dgl, 0.0))
        dG = t_qg - t_ke - t_mid + jnp.concatenate(ends, axis=0)
        dlogf = _tri_apply(tri_ref[1], dG)
        df = dlogf / f - dk
        du_ref[1] = (dq * (sq * (1.0 + zq * (1.0 - sq)))).astype(BF16)
        du_ref[2] = (df * (1.0 - lb) * (sig * (1.0 - sig))).astype(BF16)
        du_ref[3] = dv.astype(BF16)
        dlb = jnp.sum(df * (1.0 - sig), axis=0, keepdims=True) * (lb * (1.0 - lb))
        for h in heads:
            accumulate(dgn_ref, h, dgn[:, cols[h]])
            accumulate(dlb_ref, h, dlb[:, cols[h]])

    def rb(s, i):
        return s * tps + (tps - 1 - i)

    def per_head_spec(*shape):
        return pl.BlockSpec((H,) + shape, lambda g, s, i: (g,) + (0,) * len(shape))

    vec, mat = per_head_spec(1, W), per_head_spec(W, W)
    return pl.pallas_call(
        body, name="mixer_bwd", grid=(4 // H, seqs, tps),
        in_specs=[pl.BlockSpec((5, tm, H * W), lambda g, s, i: (0, rb(s, i), g)),
                  pl.BlockSpec((None, POOL_HALO, H * W), lambda g, s, i: (0, jnp.maximum(rb(s, i) * hb - 1, 0), g)),
                  pl.BlockSpec((2, tm, H * W), lambda g, s, i: (0, rb(s, i), g)),
                  pl.BlockSpec((tm, H * W), lambda g, s, i: (rb(s, i), g)),
                  pl.BlockSpec((nc, H, W, W), lambda g, s, i: (rb(s, i), g, 0, 0)),
                  mat, vec, per_head_spec(2, W), vec, _tri_spec(tm), _tri_spec(tm)],
        out_specs=[pl.BlockSpec((5, tm, H * W), lambda g, s, i: (0, rb(s, i), g)), mat, vec, vec, vec],
        out_shape=[jax.ShapeDtypeStruct((5, T, 4 * W), BF16),
                   jax.ShapeDtypeStruct((4, W, W), F32),
                   jax.ShapeDtypeStruct((4, 1, W), F32),
                   jax.ShapeDtypeStruct((4, 1, W), F32),
                   jax.ShapeDtypeStruct((4, 1, W), F32)],
        scratch_shapes=[pltpu.VMEM((H, POOL_HALO, W), F32),
                        pltpu.VMEM((H, 4, tm + 32, W), F32),
                        pltpu.VMEM((H, W, W), F32)],
        compiler_params=_cparams(("arbitrary", "arbitrary", "arbitrary")),
    )(u5, u5, dy2, o_pre, st_prev, pool_w_bf, scale4, theta4, gn4, tri_bf, tri_f)


def _attn_probs(q, k, hd):
    s = _dot_nt(q, k) * (1.0 / (hd ** 0.5))
    e = jnp.exp(s - jnp.max(s, axis=-1, keepdims=True))
    return e * (1.0 / jnp.sum(e, axis=-1, keepdims=True))


def attn_fwd(q, kv3, *, seqs, seq_len, n_mem, tm):
    T, D = q.shape
    hd = D // XATTN_HEADS
    tps = seq_len // tm

    cols = [slice(h * hd, (h + 1) * hd) for h in range(XATTN_HEADS)]

    def body(q_ref, kv_ref, o_ref):
        p = [_attn_probs(q_ref[:, cs], kv_ref[0, :, cs], hd) for cs in cols]
        for h, cs in enumerate(cols):
            o_ref[:, cs] = _dot(p[h].astype(BF16), kv_ref[1, :, cs]).astype(BF16)

    return pl.pallas_call(
        body, name="attn_fwd", grid=(seqs, tps),
        in_specs=[pl.BlockSpec((tm, D), lambda b, i: (b * tps + i, 0)),
                  pl.BlockSpec((2, n_mem, D), lambda b, i: (0, b, 0))],
        out_specs=pl.BlockSpec((tm, D), lambda b, i: (b * tps + i, 0)),
        out_shape=jax.ShapeDtypeStruct((T, D), BF16),
        compiler_params=_cparams(("parallel", "arbitrary")),
    )(q, kv3)


def attn_bwd(q, kv3, do, *, seqs, seq_len, n_mem, tm):
    T, D = q.shape
    hd = D // XATTN_HEADS
    tps = seq_len // tm

    cols = [slice(h * hd, (h + 1) * hd) for h in range(XATTN_HEADS)]

    def body(q_ref, kv_ref, do_ref, dq_ref, dkv_ref):
        i = pl.program_id(1)

        @pl.when(i == 0)
        def _():
            dkv_ref[...] = jnp.zeros_like(dkv_ref)

        p = [_attn_probs(q_ref[:, cs], kv_ref[0, :, cs], hd) for cs in cols]
        dp = [_dot_nt(do_ref[:, cs], kv_ref[1, :, cs]) for cs in cols]
        ds = [(p[h] * (dp[h] - jnp.sum(dp[h] * p[h], axis=-1, keepdims=True)) * (1.0 / (hd ** 0.5))).astype(BF16)
              for h in range(XATTN_HEADS)]
        for h, cs in enumerate(cols):
            dq_ref[:, cs] = _dot(ds[h], kv_ref[0, :, cs]).astype(BF16)
            dkv_ref[0, :, cs] += _dot_tn(ds[h], q_ref[:, cs])
            dkv_ref[1, :, cs] += _dot_tn(p[h].astype(BF16), do_ref[:, cs])

    qspec = pl.BlockSpec((tm, D), lambda b, i: (b * tps + i, 0))
    kvspec = pl.BlockSpec((2, n_mem, D), lambda b, i: (0, b, 0))
    return pl.pallas_call(
        body, name="attn_bwd", grid=(seqs, tps),
        in_specs=[qspec, kvspec, qspec],
        out_specs=[qspec, kvspec],
        out_shape=[jax.ShapeDtypeStruct((T, D), BF16), jax.ShapeDtypeStruct((2, seqs * n_mem, D), F32)],
        compiler_params=_cparams(("parallel", "arbitrary")),
    )(q, kv3, do)


def final_loss(h, g, target, *, tm):
    T, D = h.shape

    def body(h_ref, g_ref, t_ref, dh_ref, dhb_ref, ls_ref, dg_ref):
        i = pl.program_id(0)
        x = h_ref[...]
        gv = g_ref[...]
        r = lax.rsqrt(jnp.mean(x * x, axis=-1, keepdims=True) + EPS)
        xr = x * r
        d = xr * gv - t_ref[...]
        dy = d * (1.0 / D)
        dyg = dy * gv
        dx = r * (dyg - xr * jnp.mean(dyg * xr, axis=-1, keepdims=True))
        dh_ref[...] = dx
        dhb_ref[...] = dx.astype(BF16)
        ls = jnp.sum(d * d, axis=0, keepdims=True)
        dg = jnp.sum(dy * xr, axis=0, keepdims=True)

        @pl.when(i == 0)
        def _():
            ls_ref[...] = ls
            dg_ref[...] = dg

        @pl.when(i > 0)
        def _():
            ls_ref[...] += ls
            dg_ref[...] += dg

    row = pl.BlockSpec((tm, D), lambda i: (i, 0))
    vec = pl.BlockSpec((1, D), lambda i: (0, 0))
    return pl.pallas_call(
        body, name="final_loss", grid=(T // tm,),
        in_specs=[row, vec, row], out_specs=[row, row, vec, vec],
        out_shape=[jax.ShapeDtypeStruct((T, D), F32), jax.ShapeDtypeStruct((T, D), BF16),
                   jax.ShapeDtypeStruct((1, D), F32), jax.ShapeDtypeStruct((1, D), F32)],
        compiler_params=_cparams(("arbitrary",)),
    )(h, g, target)


def _my_place():
    return lax.axis_index("x"), lax.axis_index("y"), lax.axis_index("c")


def _slot_of(px, py, pc):
    return 4 * px + 2 * py + pc


def _peer(k, x, y, c):
    return (1 - x if (k >> 2) & 1 else x, 1 - y if (k >> 1) & 1 else y, 1 - c if k & 1 else c)


def _split_copies(src_refs, land_refs, send_sems, recv_sems, scatter):
    x, y, c = _my_place()
    mine = _slot_of(x, y, c)
    copies = []
    for a, (src, land) in enumerate(zip(src_refs, land_refs)):
        for k in range(1, N_DEV):
            peer = _peer(k, x, y, c)
            copies.append(pltpu.make_async_remote_copy(
                src_ref=src.at[_slot_of(*peer)] if scatter else src, dst_ref=land.at[mine],
                send_sem=send_sems.at[a * N_PEERS + k - 1], recv_sem=recv_sems.at[a * N_PEERS + k - 1],
                device_id=peer, device_id_type=MESH))
    return copies


def split_start(groups, *, name, scatter):
    sizes = [len(srcs) for srcs, _ in groups]
    n_arr = sum(sizes)
    flat = [a for srcs, lands in groups for a in list(srcs) + list(lands)]

    def body(*refs):
        ins = refs[:2 * n_arr]
        sems = refs[4 * n_arr:4 * n_arr + 2 * len(groups)]
        token = refs[-1]
        at = 0
        for gi, n in enumerate(sizes):
            for cp in _split_copies(ins[at:at + n], ins[at + n:at + 2 * n], sems[2 * gi], sems[2 * gi + 1], scatter):
                cp.start()
            at += 2 * n
        token[...] = jnp.zeros_like(token)

    sem_shapes = []
    for n in sizes:
        sem_shapes += [pltpu.SemaphoreType.DMA((n * N_PEERS,))] * 2
    outs = pl.pallas_call(
        body, name=name,
        out_shape=tuple(pltpu.HBM(a.shape, a.dtype) for a in flat) + tuple(sem_shapes)
        + (jax.ShapeDtypeStruct((8, 128), F32),),
        in_specs=(HBM,) * len(flat),
        out_specs=(HBM,) * len(flat) + (SEM,) * len(sem_shapes) + (pl.BlockSpec(memory_space=pltpu.VMEM),),
        input_output_aliases={i: i for i in range(len(flat))},
        compiler_params=pltpu.CompilerParams(has_side_effects=pltpu.SideEffectType.DATAFLOW_SIDE_EFFECTING),
    )(*[pltpu.with_memory_space_constraint(a, pltpu.HBM) for a in flat])
    thru, sems, token = outs[:len(flat)], outs[len(flat):-1], outs[-1]
    started, at = [], 0
    for gi, n in enumerate(sizes):
        started.append((sems[2 * gi], sems[2 * gi + 1], thru[at:at + n], thru[at + n:at + 2 * n]))
        at += 2 * n
    return started, token


def split_wait(started, after, *, name, scatter):
    sizes = [len(g[2]) for g in started]
    n_arr = sum(sizes)
    flat = [a for g in started for a in list(g[2]) + list(g[3])]
    sems = [s for g in started for s in g[:2]]

    def body(*refs):
        ins = refs[:2 * n_arr]
        sem_refs = refs[2 * n_arr:2 * n_arr + len(sems)]
        at = 0
        for gi, n in enumerate(sizes):
            for cp in _split_copies(ins[at:at + n], ins[at + n:at + 2 * n], sem_refs[2 * gi], sem_refs[2 * gi + 1], scatter):
                cp.wait_send()
                cp.wait_recv()
            at += 2 * n

    outs = pl.pallas_call(
        body, name=name,
        out_shape=tuple(pltpu.HBM(a.shape, a.dtype) for a in flat),
        in_specs=(HBM,) * len(flat) + (SEM,) * len(sems) + (pl.BlockSpec(memory_space=pl.ANY),),
        out_specs=(HBM,) * len(flat),
        input_output_aliases={i: i for i in range(len(flat))},
        compiler_params=pltpu.CompilerParams(has_side_effects=pltpu.SideEffectType.DATAFLOW_SIDE_EFFECTING),
    )(*flat, *sems, after)
    done, at = [], 0
    for n in sizes:
        done.append((outs[at:at + n], outs[at + n:at + 2 * n]))
        at += 2 * n
    return done


def allgather_small(bufs):
    n = len(bufs)

    def body(*refs):
        srcs, outs = refs[:n], refs[n:2 * n]
        send_sems, recv_sems, local_sems = refs[2 * n:]
        x, y, c = _my_place()
        mine = _slot_of(x, y, c)
        local = [pltpu.make_async_copy(s, o.at[mine], local_sems.at[a]) for a, (s, o) in enumerate(zip(srcs, outs))]
        for cp in local:
            cp.start()
        copies = _split_copies(srcs, outs, send_sems, recv_sems, False)
        for cp in copies:
            cp.start()
        for cp in copies:
            cp.wait()
        for cp in local:
            cp.wait()

    return pl.pallas_call(
        body, name="allgather_small",
        out_shape=[jax.ShapeDtypeStruct((N_DEV,) + b.shape, b.dtype) for b in bufs],
        in_specs=[HBM] * n, out_specs=[HBM] * n,
        scratch_shapes=[pltpu.SemaphoreType.DMA((n * N_PEERS,)), pltpu.SemaphoreType.DMA((n * N_PEERS,)),
                        pltpu.SemaphoreType.DMA((n,))],
    )(*bufs)


def _adamw_math(g, w, m, v):
    c1 = 1.0 - ADAM_B1 ** ADAM_STEP
    c2 = 1.0 - ADAM_B2 ** ADAM_STEP
    nm = ADAM_B1 * m + (1.0 - ADAM_B1) * g
    nv = ADAM_B2 * v + (1.0 - ADAM_B2) * (g * g)
    delta = -ADAM_LR * ((nm / c1) / (jnp.sqrt(nv / c2) + ADAM_EPS) + ADAM_WD * w)
    return delta, nm, nv


def adamw_sharded(me, own, recv, w, m, v, *, name, tr):
    R, C = w.shape

    def body(me_ref, *refs):
        parts = refs[:N_DEV]
        w_ref, m_ref, v_ref, g_ref, d_ref, nm_ref, nv_ref = refs[N_DEV:]
        g = parts[0][...].astype(F32)
        for p in parts[1:]:
            g = g + p[...].astype(F32)
        g_ref[...] = g
        d_ref[...], nm_ref[...], nv_ref[...] = _adamw_math(g, w_ref[...], m_ref[...], v_ref[...])

    def slab(k):
        return pl.BlockSpec((None, tr, C), lambda i, me_ref: (me_ref[0] ^ k, i, 0))

    blk = pl.BlockSpec((tr, C), lambda i, me_ref: (i, 0))
    out = jax.ShapeDtypeStruct((R, C), F32)
    return pl.pallas_call(
        body, name=name,
        grid_spec=pltpu.PrefetchScalarGridSpec(
            num_scalar_prefetch=1, grid=(R // tr,),
            in_specs=[slab(k) for k in range(N_DEV)] + [blk, blk, blk],
            out_specs=[blk, blk, blk, blk]),
        out_shape=[out, out, out, out],
        compiler_params=_cparams(("parallel",)),
    )(me, own, *([recv] * N_PEERS), w, m, v)


def adamw_replicated(parts, ws, ms, vs, rows):
    n_buf, n_par = len(parts), len(ws)

    def body(*refs):
        p_refs = refs[:n_buf]
        w_refs = refs[n_buf:n_buf + n_par]
        m_refs = refs[n_buf + n_par:n_buf + 2 * n_par]
        v_refs = refs[n_buf + 2 * n_par:n_buf + 3 * n_par]
        outs = refs[n_buf + 3 * n_par:]
        sums = []
        for p in p_refs:
            g = p[0]
            for s in range(1, N_DEV):
                g = g + p[s]
            sums.append(g)
        for j, (b, r0, nr) in enumerate(rows):
            g = sums[b][r0:r0 + nr]
            delta, nm, nv = _adamw_math(g, w_refs[j][...], m_refs[j][...], v_refs[j][...])
            outs[j][...] = g
            outs[n_par + j][...] = delta
            outs[2 * n_par + j][...] = nm
            outs[3 * n_par + j][...] = nv

    shapes = [jax.ShapeDtypeStruct(w.shape, F32) for w in ws]
    outs = pl.pallas_call(
        body, name="adamw_replicated", out_shape=shapes * 4,
        compiler_params=pltpu.CompilerParams(vmem_limit_bytes=V7X_VMEM_LIMIT),
    )(*parts, *ws, *ms, *vs)
    return outs[:n_par], outs[n_par:2 * n_par], outs[2 * n_par:3 * n_par], outs[3 * n_par:]


BIG = ("w_in", "w_out", "xw_q", "xw_kv", "xw_o", "w_up", "w_down")
COL_SHARDED = ("w_in", "xw_kv", "w_up")
WEIGHTS = ("norm_mix", "w_in", "pool_w", "pool_scale", "lb_theta", "hgrn_norm", "w_out", "norm_xq",
           "norm_mem", "xw_q", "xw_kv", "xw_o", "norm_mlp", "w_up", "w_down", "norm_final")
SMALL = (("pool_w", (4 * HEAD_W, HEAD_W), 0, 0),
         ("norm_mix", (1, 1024), 1, 0), ("norm_xq", (1, 1024), 1, 1), ("norm_mem", (1, 1024), 1, 2),
         ("norm_mlp", (1, 1024), 1, 3), ("norm_final", (1, 1024), 1, 4),
         ("pool_scale", (1, 512), 2, 0), ("hgrn_norm", (1, 512), 2, 1), ("lb_theta", (2, 512), 2, 2))


def _pad_rows(a, rows):
    return jnp.concatenate([a, jnp.zeros((rows - a.shape[0], a.shape[1]), a.dtype)], axis=0)


def kernel(x, mem, norm_mix, w_in, pool_w, pool_scale, lb_theta, hgrn_norm, w_out, norm_xq, norm_mem, xw_q, xw_kv, xw_o, norm_mlp, w_up, w_down, norm_final, loss_target, m_norm_mix, m_w_in, m_pool_w, m_pool_scale, m_lb_theta, m_hgrn_norm, m_w_out, m_norm_xq, m_norm_mem, m_xw_q, m_xw_kv, m_xw_o, m_norm_mlp, m_w_up, m_w_down, m_norm_final, v_norm_mix, v_w_in, v_pool_w, v_pool_scale, v_lb_theta, v_hgrn_norm, v_w_out, v_norm_xq, v_norm_mem, v_xw_q, v_xw_kv, v_xw_o, v_norm_mlp, v_w_up, v_w_down, v_norm_final):
    w = dict(norm_mix=norm_mix, w_in=w_in, pool_w=pool_w, pool_scale=pool_scale, lb_theta=lb_theta,
             hgrn_norm=hgrn_norm, w_out=w_out, norm_xq=norm_xq, norm_mem=norm_mem, xw_q=xw_q, xw_kv=xw_kv,
             xw_o=xw_o, norm_mlp=norm_mlp, w_up=w_up, w_down=w_down, norm_final=norm_final)
    mom = dict(norm_mix=m_norm_mix, w_in=m_w_in, pool_w=m_pool_w, pool_scale=m_pool_scale, lb_theta=m_lb_theta,
               hgrn_norm=m_hgrn_norm, w_out=m_w_out, norm_xq=m_norm_xq, norm_mem=m_norm_mem, xw_q=m_xw_q,
               xw_kv=m_xw_kv, xw_o=m_xw_o, norm_mlp=m_norm_mlp, w_up=m_w_up, w_down=m_w_down,
               norm_final=m_norm_final)
    var = dict(norm_mix=v_norm_mix, w_in=v_w_in, pool_w=v_pool_w, pool_scale=v_pool_scale, lb_theta=v_lb_theta,
               hgrn_norm=v_hgrn_norm, w_out=v_w_out, norm_xq=v_norm_xq, norm_mem=v_norm_mem, xw_q=v_xw_q,
               xw_kv=v_xw_kv, xw_o=v_xw_o, norm_mlp=v_norm_mlp, w_up=v_w_up, w_down=v_w_down,
               norm_final=v_norm_final)

    seqs, seq_len, D = x.shape
    n_mem = mem.shape[1]
    T = seqs * seq_len
    W = HEAD_W
    x2 = x.reshape(T, D)
    mem2 = mem.reshape(seqs * n_mem, D)
    tgt2 = loss_target.reshape(T, D)
    tm_big = min(1024, T)
    tm_mid = min(512, T)
    tm_sq = min(1024, T)
    tm_mix = min(256, seq_len)
    tm_att = min(1024, seq_len)
    tkv = min(512, seqs * n_mem)
    px, py, pc = _my_place()
    me = _slot_of(px, py, pc).astype(jnp.int32)
    me1 = me.reshape(1)

    shard_bf = {n: w[n][0].astype(BF16) for n in BIG}

    def landing(n):
        zone = lax.empty((N_DEV,) + shard_bf[n].shape, BF16)
        return lax.dynamic_update_slice(zone, shard_bf[n][None], (me, 0, 0))

    ag_groups = (("w_in",), ("w_out", "xw_q", "xw_kv", "xw_o"), ("w_up", "w_down"))
    ag_started, _ = split_start([([shard_bf[n] for n in grp], [landing(n) for n in grp]) for grp in ag_groups],
                                name="weights_gather_start", scatter=False)

    pool_w_bf = pool_w[0].astype(BF16)
    scale4 = pool_scale.reshape(4, 1, W)
    gn4 = hgrn_norm.reshape(4, 1, W)
    theta4 = lb_theta.reshape(2, 4, W).transpose(1, 0, 2)
    g_final = norm_final.reshape(1, D)

    (_, (wi3,)), = split_wait(ag_started[:1], x2, name="weights_gather_wait_in", scatter=False)
    full_w_in = wi3.transpose(1, 0, 2).reshape(D, -1)
    u5, n1 = proj_norm(x2, norm_mix, full_w_in, name="in_proj", tm=tm_mid, tn=4 * W, out_dtype=F32, out_slabs=5)
    tri_bf, tri_f = chunk_triangles(tm_mix)
    y2, o_pre, st_prev = mixer_fwd(u5, pool_w_bf, scale4, theta4, gn4, tri_bf, tri_f, seqs=seqs, seq_len=seq_len,
                                   tm=tm_mix)
    (_, (wo3, wq3, wkv3, wao3)), = split_wait(ag_started[1:2], y2, name="weights_gather_wait_attn", scatter=False)
    full_w_out, full_xw_q, full_xw_o = wo3.reshape(D, D), wq3.reshape(D, D), wao3.reshape(D, D)
    tn = 4 * W
    h1, n2 = proj_res_norm(y2, full_w_out, x2, norm_xq, name="out_proj", tm=tm_sq, tn=tn)
    q = proj_plain(n2, full_xw_q, name="q_proj", tm=tm_sq, tn=tn)
    kv3, memn = proj_norm(mem2, norm_mem, wkv3, name="kv_proj", tm=tkv, tn=wkv3.shape[2], out_dtype=BF16,
                          out_slabs=2)
    o_att = attn_fwd(q, kv3, seqs=seqs, seq_len=seq_len, n_mem=n_mem, tm=tm_att)
    h2, n3 = proj_res_norm(o_att, full_xw_o, h1, norm_mlp, name="attn_out_proj", tm=tm_sq, tn=tn)
    (_, (wup3, wdn3)), = split_wait(ag_started[2:3], h2, name="weights_gather_wait_mlp", scatter=False)
    full_w_down = wdn3.reshape(-1, D)
    tn_up = wup3.shape[2]
    aa = proj_plain(n3, wup3, name="up_proj", tm=tm_mid, tn=tn_up, relu2=True)
    dh3, dh3b, sq_err, dg_final = proj_res_loss(aa, full_w_down, h2, g_final, tgt2, name="down_proj_loss",
                                                tm=tm_mid, tn=tn)

    def send(parts, name):
        srcs = [p.reshape((N_DEV, -1, p.shape[-1])) for p in parts]
        lands = [lax.empty(s.shape, BF16) for s in srcs]
        started, token = split_start([(srcs, lands)], name=name, scatter=True)
        return started[0], token

    gw_down = wgrad(aa, dh3b, name="down_proj_wgrad", tt=tm_mid, tn=tn)
    sent_down, tok = send([gw_down], "grads_send_down")
    dap = back_plain(dh3b, full_w_down, name="down_proj_bwd", tm=tm_mid, tn=tn, out_dtype=BF16, relu2_value=aa,
                     after=tok)
    gw_up = wgrad(n3, dap, name="up_proj_wgrad", tt=tm_mid, tn=tn_up, out_slabs=N_DEV)
    sent_up, tok = send([gw_up], "grads_send_up")
    dh2, dh2b, dg_mlp = back_norm(dap, wup3, h2, norm_mlp, dh3, name="up_proj_bwd", tm=tm_mid, tk=tn_up, after=tok)
    do_att = back_plain(dh2b, full_xw_o, name="attn_out_proj_bwd", tm=tm_sq, tn=tn, out_dtype=BF16)
    gxw_o = wgrad(o_att, dh2b, name="attn_out_proj_wgrad", tt=tm_sq, tn=tn)
    dq, dkv3 = attn_bwd(q, kv3, do_att, seqs=seqs, seq_len=seq_len, n_mem=n_mem, tm=tm_att)
    gxw_q = wgrad(n2, dq, name="q_proj_wgrad", tt=tm_sq, tn=tn)
    gxw_kv = wgrad(memn, dkv3, name="kv_proj_wgrad", tt=tkv, tn=wkv3.shape[2], out_slabs=N_DEV)
    sent_attn, tok = send([gxw_o, gxw_q, gxw_kv], "grads_send_attn")
    dg_mem = back_norm(dkv3, wkv3, mem2, norm_mem, None, name="kv_proj_bwd", tm=tkv, tk=wkv3.shape[2])
    dh1, dh1b, dg_xq = back_norm(dq, full_xw_q, h1, norm_xq, dh2, name="q_proj_bwd", tm=tm_sq, tk=D, after=tok)
    gw_out = wgrad(y2, dh1b, name="out_proj_wgrad", tt=tm_sq, tn=tn)
    sent_out, tok = send([gw_out], "grads_send_out")
    dy2 = back_plain(dh1b, full_w_out, name="out_proj_bwd", tm=tm_sq, tn=tn, out_dtype=F32, out_slabs=2, after=tok)
    du5, dpw, dsc, dlb, dgn = mixer_bwd(u5, dy2, o_pre, st_prev, pool_w_bf, scale4, theta4, gn4, tri_bf, tri_f,
                                        seqs=seqs, seq_len=seq_len, tm=tm_mix)
    gw_in = wgrad(n1, du5, name="in_proj_wgrad", tt=tm_mid, tn=tn)
    gw_in_slots = gw_in.reshape(D, N_DEV, -1).transpose(1, 0, 2)
    sent_in, tok = send([gw_in_slots], "grads_send_in")
    dx, dg_mix = back_norm(du5, full_w_in, x2, norm_mix, dh1, name="in_proj_bwd", tm=tm_mid, tk=tn, bf16_copy=False,
                           after=tok)

    dlb_row = dlb.reshape(1, 4 * W)
    buf_vec = _pad_rows(jnp.concatenate([dg_mix, dg_xq, dg_mem, dg_mlp, dg_final, sq_err], axis=0), 8)
    buf_half = _pad_rows(jnp.concatenate([dsc.reshape(1, 4 * W), dgn.reshape(1, 4 * W), dlb_row, -dlb_row], axis=0), 8)
    small_src = [dpw.reshape(4 * W, W), buf_vec, buf_half]
    small_land = [lax.dynamic_update_slice(lax.empty((N_DEV,) + b.shape, F32), b[None], (me, 0, 0))
                  for b in small_src]
    small_started, tok = split_start([(small_src, small_land)], name="small_grads_start", scatter=False)

    done = split_wait([sent_down, sent_up, sent_attn, sent_out, sent_in], tok, name="grads_wait", scatter=True)
    slots = dict(w_down=(0, 0), w_up=(1, 0), xw_o=(2, 0), xw_q=(2, 1), xw_kv=(2, 2), w_out=(3, 0), w_in=(4, 0))
    own = {n: done[gi][0][ai] for n, (gi, ai) in slots.items()}
    got = {n: done[gi][1][ai] for n, (gi, ai) in slots.items()}
    res = {}
    for n in BIG:
        shp = w[n].shape
        r = adamw_sharded(me1, own[n], got[n], w[n][0], mom[n][0], var[n][0], name="adamw_" + n,
                          tr=min(256, shp[1]))
        for kind, a in zip("gdmv", r):
            res[kind, n] = a.reshape(shp)
    (_, small_parts), = split_wait(small_started, res["g", BIG[-1]], name="small_grads_wait", scatter=False)
    loss = 0.5 * jnp.sum(small_parts[1][:, 5, :]) / D
    r = adamw_replicated(small_parts, [w[n].reshape(v2) for n, v2, _, _ in SMALL],
                         [mom[n].reshape(v2) for n, v2, _, _ in SMALL],
                         [var[n].reshape(v2) for n, v2, _, _ in SMALL],
                         [(b, r0, v2[0]) for _, v2, b, r0 in SMALL])
    for kind, arrs in zip("gdmv", r):
        for (n, _, _, _), a in zip(SMALL, arrs):
            res[kind, n] = a.reshape(w[n].shape)

    out = [loss, dx.reshape(x.shape)]
    for kind in "gdmv":
        out += [res[kind, n] for n in WEIGHTS]
    return tuple(out)
```

```python
import jax
import jax.numpy as jnp
from jax import lax
from jax.experimental import pallas as pl
from jax.experimental.pallas import tpu as pltpu

F32 = jnp.float32
BF16 = jnp.bfloat16
EPS = 1e-6
CHUNK = 64
POOL_HALO = 16
HEAD_W = 128
HEADS_PER_STEP = 4
XATTN_HEADS = 4
N_DEV = 8
N_PEERS = N_DEV - 1
ADAM_LR = 0.001
ADAM_B1 = 0.9
ADAM_B2 = 0.999
ADAM_EPS = 1e-08
ADAM_WD = 0.01
ADAM_STEP = 10
V7X_VMEM_LIMIT = 52 * 1024 * 1024
MESH = pl.DeviceIdType.MESH
HBM = pl.BlockSpec(memory_space=pltpu.HBM)
SEM = pl.BlockSpec(memory_space=pltpu.SEMAPHORE)


def _cparams(dims):
    return pltpu.CompilerParams(dimension_semantics=dims, vmem_limit_bytes=V7X_VMEM_LIMIT)


def _sigmoid(v):
    return 0.5 * jnp.tanh(0.5 * v) + 0.5


def _dot(a, b):
    return jnp.dot(a, b, preferred_element_type=F32)


def _dot_nt(a, b):
    return lax.dot_general(a, b, (((1,), (1,)), ((), ())), preferred_element_type=F32)


def _dot_tn(a, b):
    return lax.dot_general(a, b, (((0,), (0,)), ((), ())), preferred_element_type=F32)


def _split3(v):
    hi = v.astype(BF16)
    r1 = v - hi.astype(F32)
    mid = r1.astype(BF16)
    lo = (r1 - mid.astype(F32)).astype(BF16)
    return hi, mid, lo


def _tri_apply(tri, v):
    hi, mid, lo = _split3(v)
    return _dot(tri, hi) + _dot(tri, mid) + _dot(tri, lo)


def _mat_shape(a):
    return a.shape if a.ndim == 2 else (a.shape[1], a.shape[0] * a.shape[2])


def _tile_spec(a, rows, cols, row_of, col_of):
    if a.ndim == 2:
        return pl.BlockSpec((rows, cols), lambda *g: (row_of(*g), col_of(*g)))
    per = a.shape[2] // cols
    return pl.BlockSpec((None, rows, cols), lambda *g: (col_of(*g) // per, row_of(*g), col_of(*g) % per))


def _out_struct(rows, n, slabs, dtype):
    return jax.ShapeDtypeStruct((rows, n) if slabs is None else (slabs, rows, n // slabs), dtype)


def norm_mm(h, g, w, *, name, tm, tn, out_dtype, out_slabs=None):
    T, D = h.shape
    N = _mat_shape(w)[1]
    o_shape = _out_struct(T, N, out_slabs, out_dtype)

    def body(h_ref, g_ref, w_ref, o_ref, n_ref):
        @pl.when(pl.program_id(1) == 0)
        def _():
            x = h_ref[...]
            r = lax.rsqrt(jnp.mean(x * x, axis=-1, keepdims=True) + EPS)
            n_ref[...] = (x * r * g_ref[...]).astype(BF16)

        o_ref[...] = _dot(n_ref[...], w_ref[...]).astype(o_ref.dtype)

    return pl.pallas_call(
        body, name=name, grid=(T // tm, N // tn),
        in_specs=[pl.BlockSpec((tm, D), lambda i, j: (i, 0)),
                  pl.BlockSpec((1, D), lambda i, j: (0, 0)),
                  _tile_spec(w, D, tn, lambda i, j: 0, lambda i, j: j)],
        out_specs=[_tile_spec(o_shape, tm, tn, lambda i, j: i, lambda i, j: j),
                   pl.BlockSpec((tm, D), lambda i, j: (i, 0))],
        out_shape=[o_shape, jax.ShapeDtypeStruct((T, D), BF16)],
        compiler_params=_cparams(("parallel", "arbitrary")),
    )(h, g, w)


def mm_nn(a, w, res, *, name, tm, tn, tk, relu2=False):
    T, K = _mat_shape(a)
    N = w.shape[1]
    nk = K // tk

    def body(a_ref, w_ref, r_ref, o_ref, acc_ref):
        k = pl.program_id(2)
        av = a_ref[...]
        if relu2:
            av = jnp.maximum(av, 0.0)
            av = av * av
        part = _dot(av.astype(BF16), w_ref[...])

        @pl.when(k == 0)
        def _():
            acc_ref[...] = part

        @pl.when(k > 0)
        def _():
            acc_ref[...] += part

        @pl.when(k == nk - 1)
        def _():
            o_ref[...] = r_ref[...] + acc_ref[...]

    return pl.pallas_call(
        body, name=name, grid=(T // tm, N // tn, nk),
        in_specs=[_tile_spec(a, tm, tk, lambda i, j, k: i, lambda i, j, k: k),
                  pl.BlockSpec((tk, tn), lambda i, j, k: (k, j)),
                  pl.BlockSpec((tm, tn), lambda i, j, k: (i, j))],
        out_specs=pl.BlockSpec((tm, tn), lambda i, j, k: (i, j)),
        out_shape=jax.ShapeDtypeStruct((T, N), F32),
        scratch_shapes=[pltpu.VMEM((tm, tn), F32)],
        compiler_params=_cparams(("parallel", "parallel", "arbitrary")),
    )(a, w, res)


def mm_nt(a, w, *, name, tm, tn, tk, out_dtype, out_slabs=None, relu2_of=None, after=None):
    T, K = _mat_shape(a)
    nk = K // tk
    N = w.shape[0]
    has_z = relu2_of is not None
    o_shape = _out_struct(T, N, out_slabs, out_dtype)

    def body(*refs):
        a_ref, w_ref = refs[0], refs[1]
        z_ref = refs[2] if has_z else None
        o_ref, acc_ref = refs[-2], refs[-1]
        k = pl.program_id(2)
        part = _dot_nt(a_ref[...].astype(BF16), w_ref[...])

        @pl.when(k == 0)
        def _():
            acc_ref[...] = part

        @pl.when(k > 0)
        def _():
            acc_ref[...] += part

        @pl.when(k == nk - 1)
        def _():
            out = acc_ref[...]
            if has_z:
                out = out * (2.0 * jnp.maximum(z_ref[...], 0.0))
            o_ref[...] = out.astype(o_ref.dtype)

    in_specs = [_tile_spec(a, tm, tk, lambda i, j, k: i, lambda i, j, k: k),
                pl.BlockSpec((tn, tk), lambda i, j, k: (j, k))]
    args = [a, w]
    if has_z:
        in_specs.append(pl.BlockSpec((tm, tn), lambda i, j, k: (i, j)))
        args.append(relu2_of)
    if after is not None:
        in_specs.append(pl.BlockSpec(after.shape, lambda i, j, k: (0, 0)))
        args.append(after)
    return pl.pallas_call(
        body, name=name, grid=(T // tm, N // tn, nk),
        in_specs=in_specs,
        out_specs=_tile_spec(o_shape, tm, tn, lambda i, j, k: i, lambda i, j, k: j),
        out_shape=o_shape,
        scratch_shapes=[pltpu.VMEM((tm, tn), F32)],
        compiler_params=_cparams(("parallel", "parallel", "arbitrary")),
    )(*args)


def mm_nt_normbwd(a, w, h, g, dres, *, name, tm, tk, after=None):
    T, K = _mat_shape(a)
    nk = K // tk
    D = h.shape[1]
    with_dh = dres is not None

    def body(*refs):
        a_ref, w_ref, h_ref, g_ref = refs[:4]
        if with_dh:
            r_ref = refs[4]
            dh_ref, dhb_ref, dg_ref, acc_ref = refs[-4:]
        else:
            dg_ref, acc_ref = refs[-2:]
        i = pl.program_id(0)
        k = pl.program_id(1)
        part = _dot_nt(a_ref[...].astype(BF16), w_ref[...])

        @pl.when(k == 0)
        def _():
            acc_ref[...] = part

        @pl.when(k > 0)
        def _():
            acc_ref[...] += part

        @pl.when(k == nk - 1)
        def _():
            dn = acc_ref[...]
            x = h_ref[...]
            r = lax.rsqrt(jnp.mean(x * x, axis=-1, keepdims=True) + EPS)
            xr = x * r
            dgp = jnp.sum(dn * xr, axis=0, keepdims=True)

            @pl.when(i == 0)
            def _():
                dg_ref[...] = dgp

            @pl.when(i > 0)
            def _():
                dg_ref[...] += dgp

            if with_dh:
                dyg = dn * g_ref[...]
                dx = r * (dyg - xr * jnp.mean(dyg * xr, axis=-1, keepdims=True))
                out = r_ref[...] + dx
                dh_ref[...] = out
                dhb_ref[...] = out.astype(BF16)

    row = pl.BlockSpec((tm, D), lambda i, k: (i, 0))
    vec = pl.BlockSpec((1, D), lambda i, k: (0, 0))
    in_specs = [_tile_spec(a, tm, tk, lambda i, k: i, lambda i, k: k),
                _tile_spec(w, D, tk, lambda i, k: 0, lambda i, k: k), row, vec]
    args = [a, w, h, g]
    if with_dh:
        in_specs.append(row)
        args.append(dres)
        out_specs = [row, row, vec]
        out_shape = [jax.ShapeDtypeStruct((T, D), F32), jax.ShapeDtypeStruct((T, D), BF16),
                     jax.ShapeDtypeStruct((1, D), F32)]
    else:
        out_specs = vec
        out_shape = jax.ShapeDtypeStruct((1, D), F32)
    if after is not None:
        in_specs.append(pl.BlockSpec(after.shape, lambda i, k: (0, 0)))
        args.append(after)
    return pl.pallas_call(
        body, name=name, grid=(T // tm, nk),
        in_specs=in_specs, out_specs=out_specs, out_shape=out_shape,
        scratch_shapes=[pltpu.VMEM((tm, D), F32)],
        compiler_params=_cparams(("arbitrary", "arbitrary")),
    )(*args)


def mm_tn(a, b, *, name, tt, tko, tn, relu2=False, out_slabs=None):
    T, K = _mat_shape(a)
    N = _mat_shape(b)[1]
    nt = T // tt
    o_shape = _out_struct(K, N, out_slabs, BF16)

    def body(a_ref, b_ref, o_ref, acc_ref):
        t = pl.program_id(2)
        av = a_ref[...]
        if relu2:
            av = jnp.maximum(av, 0.0)
            av = av * av
        part = _dot_tn(av.astype(BF16), b_ref[...].astype(BF16))

        @pl.when(t == 0)
        def _():
            acc_ref[...] = part

        @pl.when(t > 0)
        def _():
            acc_ref[...] += part

        @pl.when(t == nt - 1)
        def _():
            o_ref[...] = acc_ref[...].astype(BF16)

    return pl.pallas_call(
        body, name=name, grid=(K // tko, N // tn, nt),
        in_specs=[_tile_spec(a, tt, tko, lambda kk, j, t: t, lambda kk, j, t: kk),
                  _tile_spec(b, tt, tn, lambda kk, j, t: t, lambda kk, j, t: j)],
        out_specs=_tile_spec(o_shape, tko, tn, lambda kk, j, t: kk, lambda kk, j, t: j),
        out_shape=o_shape,
        scratch_shapes=[pltpu.VMEM((tko, tn), F32)],
        compiler_params=_cparams(("parallel", "parallel", "arbitrary")),
    )(a, b)


def _resident(a):
    nd = a.ndim
    return pl.BlockSpec(a.shape, lambda i: (0,) * nd, pipeline_mode=pl.Buffered(1))


def _row_block(a, tm):
    if a.ndim == 2:
        return pl.BlockSpec((tm, a.shape[1]), lambda i: (i, 0))
    return pl.BlockSpec((a.shape[0], tm, a.shape[2]), lambda i: (0, i, 0))


def _cols(ref, c, width):
    if len(ref.shape) == 2:
        return ref[:, c * width:(c + 1) * width]
    per = ref.shape[2] // width
    if per == 1:
        return ref[c]
    return ref[c // per, :, (c % per) * width:(c % per + 1) * width]


def _set_cols(ref, c, width, val):
    if len(ref.shape) == 2:
        ref[:, c * width:(c + 1) * width] = val
        return
    per = ref.shape[2] // width
    if per == 1:
        ref[c] = val
    else:
        ref[c // per, :, (c % per) * width:(c % per + 1) * width] = val


def _all_cols(ref):
    if len(ref.shape) == 2:
        return ref[...]
    return jnp.concatenate([ref[s] for s in range(ref.shape[0])], axis=1)


def _rms(x):
    return lax.rsqrt(jnp.mean(x * x, axis=-1, keepdims=True) + EPS)


def _row_params():
    return _cparams(("arbitrary",))


def proj_norm(h, g, w, *, name, tm, tn, out_dtype, out_slabs=None):
    T, D = h.shape
    N = _mat_shape(w)[1]
    o_shape = _out_struct(T, N, out_slabs, out_dtype)

    def body(h_ref, g_ref, w_ref, o_ref, n_ref):
        x = h_ref[...]
        n = (x * _rms(x) * g_ref[...]).astype(BF16)
        n_ref[...] = n
        for c in range(N // tn):
            _set_cols(o_ref, c, tn, _dot(n, _cols(w_ref, c, tn)).astype(out_dtype))

    return pl.pallas_call(
        body, name=name, grid=(T // tm,),
        in_specs=[_row_block(h, tm), pl.BlockSpec((1, D), lambda i: (0, 0)), _resident(w)],
        out_specs=[_row_block(o_shape, tm), pl.BlockSpec((tm, D), lambda i: (i, 0))],
        out_shape=[o_shape, jax.ShapeDtypeStruct((T, D), BF16)],
        compiler_params=_row_params(),
    )(h, g, w)


def prenorm(h, g, *, tm):
    T, D = h.shape

    def body(h_ref, g_ref, n_ref):
        x = h_ref[...]
        n_ref[...] = (x * _rms(x) * g_ref[...]).astype(BF16)

    row = pl.BlockSpec((tm, D), lambda i: (i, 0))
    return pl.pallas_call(
        body, name="prenorm", grid=(T // tm,),
        in_specs=[row, pl.BlockSpec((1, D), lambda i: (0, 0))],
        out_specs=row, out_shape=jax.ShapeDtypeStruct((T, D), BF16),
        compiler_params=_row_params(),
    )(h, g)


def proj_plain(a, w, *, name, tm, tn, out_dtype=BF16, out_slabs=None, relu2=False):
    T = a.shape[0]
    N = _mat_shape(w)[1]

    def body(a_ref, w_ref, o_ref):
        av = a_ref[...]
        for c in range(N // tn):
            z = _dot(av, _cols(w_ref, c, tn))
            if relu2:
                z = jnp.maximum(z, 0.0)
                z = z * z
            _set_cols(o_ref, c, tn, z.astype(out_dtype))

    o_shape = _out_struct(T, N, out_slabs, out_dtype)
    return pl.pallas_call(
        body, name=name, grid=(T // tm,),
        in_specs=[_row_block(a, tm), _resident(w)],
        out_specs=_row_block(o_shape, tm), out_shape=o_shape,
        compiler_params=_row_params(),
    )(a, w)


def proj_res_norm(a, w, res, g, w_next=None, *, name, tm, tn):
    T = res.shape[0]
    D = w.shape[1]
    chained = w_next is not None

    def body(*refs):
        a_ref, w_ref, r_ref, g_ref = refs[:4]
        h_ref, n_ref = refs[4 + chained], refs[5 + chained]
        av = _all_cols(a_ref)
        for c in range(D // tn):
            sl = slice(c * tn, (c + 1) * tn)
            h_ref[:, sl] = r_ref[:, sl] + _dot(av, w_ref[:, sl])
        hv = h_ref[...]
        n = (hv * _rms(hv) * g_ref[...]).astype(BF16)
        n_ref[...] = n
        if chained:
            for c in range(D // tn):
                sl = slice(c * tn, (c + 1) * tn)
                refs[-1][:, sl] = _dot(n, refs[4][:, sl]).astype(BF16)

    row = pl.BlockSpec((tm, D), lambda i: (i, 0))
    half = jax.ShapeDtypeStruct((T, D), BF16)
    return pl.pallas_call(
        body, name=name, grid=(T // tm,),
        in_specs=[_row_block(a, tm), _resident(w), row, pl.BlockSpec((1, D), lambda i: (0, 0))]
        + ([_resident(w_next)] if chained else []),
        out_specs=[row, row] + ([row] if chained else []),
        out_shape=[jax.ShapeDtypeStruct((T, D), F32), half] + ([half] if chained else []),
        compiler_params=_row_params(),
    )(*([a, w, res, g] + ([w_next] if chained else [])))


def proj_res_loss(a, w, res, g, target, *, name, tm, tn):
    T = res.shape[0]
    D = w.shape[1]

    def body(a_ref, w_ref, r_ref, g_ref, t_ref, dh_ref, dhb_ref, ls_ref, dg_ref):
        i = pl.program_id(0)
        gv = g_ref[...]
        ls, dg = 0.0, 0.0
        halves = [slice(s * (tm // 2), (s + 1) * (tm // 2)) for s in range(2)]
        for rows in halves:
            av = a_ref[rows, :]
            for c in range(D // tn):
                sl = slice(c * tn, (c + 1) * tn)
                dh_ref[rows, sl] = r_ref[rows, sl] + _dot(av, w_ref[:, sl])
        for rows in halves:
            x = dh_ref[rows, :]
            r = _rms(x)
            xr = x * r
            d = xr * gv - t_ref[rows, :]
            dy = d * (1.0 / D)
            dyg = dy * gv
            dx = r * (dyg - xr * jnp.mean(dyg * xr, axis=-1, keepdims=True))
            dh_ref[rows, :] = dx
            dhb_ref[rows, :] = dx.astype(BF16)
            ls = ls + jnp.sum(d * d, axis=0, keepdims=True)
            dg = dg + jnp.sum(dy * xr, axis=0, keepdims=True)

        @pl.when(i == 0)
        def _():
            ls_ref[...] = ls
            dg_ref[...] = dg

        @pl.when(i > 0)
        def _():
            ls_ref[...] += ls
            dg_ref[...] += dg

    row = pl.BlockSpec((tm, D), lambda i: (i, 0))
    vec = pl.BlockSpec((1, D), lambda i: (0, 0))
    return pl.pallas_call(
        body, name=name, grid=(T // tm,),
        in_specs=[_row_block(a, tm), _resident(w), row, vec, row],
        out_specs=[row, row, vec, vec],
        out_shape=[jax.ShapeDtypeStruct((T, D), F32), jax.ShapeDtypeStruct((T, D), BF16),
                   jax.ShapeDtypeStruct((1, D), F32), jax.ShapeDtypeStruct((1, D), F32)],
        compiler_params=_row_params(),
    )(a, w, res, g, target)


def _anchor_spec(after):
    return pl.BlockSpec(after.shape, lambda i: (0, 0))


def back_plain(a, w, *, name, tm, tn, out_dtype, out_slabs=None, relu2_value=None, after=None):
    T = a.shape[0]
    N = w.shape[0]
    has_z = relu2_value is not None
    o_shape = _out_struct(T, N, out_slabs, out_dtype)

    def body(*refs):
        a_ref, w_ref = refs[0], refs[1]
        o_ref = refs[-1]
        av = a_ref[...]
        for c in range(N // tn):
            out = _dot_nt(av, w_ref[c * tn:(c + 1) * tn, :])
            if has_z:
                out = out * (2.0 * jnp.sqrt(refs[2][:, c * tn:(c + 1) * tn]).astype(F32))
            _set_cols(o_ref, c, tn, out.astype(out_dtype))

    in_specs, args = [_row_block(a, tm), _resident(w)], [a, w]
    if has_z:
        in_specs.append(_row_block(relu2_value, tm))
        args.append(relu2_value)
    if after is not None:
        in_specs.append(_anchor_spec(after))
        args.append(after)
    return pl.pallas_call(
        body, name=name, grid=(T // tm,),
        in_specs=in_specs, out_specs=_row_block(o_shape, tm), out_shape=o_shape,
        compiler_params=_row_params(),
    )(*args)


def back_norm(a, w, h, g, dres, *, name, tm, tk, bf16_copy=True, after=None):
    T, K = _mat_shape(a)
    D = h.shape[1]
    with_dh = dres is not None

    def body(*refs):
        a_ref, w_ref, h_ref, g_ref = refs[:4]
        i = pl.program_id(0)
        dn = None
        for kc in range(K // tk):
            part = _dot_nt(_cols(a_ref, kc, tk).astype(BF16), _cols(w_ref, kc, tk))
            dn = part if dn is None else dn + part
        x = h_ref[...]
        r = _rms(x)
        xr = x * r
        dgp = jnp.sum(dn * xr, axis=0, keepdims=True)
        dg_ref = refs[-1]

        @pl.when(i == 0)
        def _():
            dg_ref[...] = dgp

        @pl.when(i > 0)
        def _():
            dg_ref[...] += dgp

        if with_dh:
            dyg = dn * g_ref[...]
            out = refs[4][...] + r * (dyg - xr * jnp.mean(dyg * xr, axis=-1, keepdims=True))
            if bf16_copy:
                refs[-3][...] = out
                refs[-2][...] = out.astype(BF16)
            else:
                refs[-2][...] = out

    row = pl.BlockSpec((tm, D), lambda i: (i, 0))
    vec = pl.BlockSpec((1, D), lambda i: (0, 0))
    in_specs, args = [_row_block(a, tm), _resident(w), row, vec], [a, w, h, g]
    if with_dh:
        in_specs.append(row)
        args.append(dres)
        out_specs = [row, row, vec] if bf16_copy else [row, vec]
        out_shape = [jax.ShapeDtypeStruct((T, D), F32)]
        if bf16_copy:
            out_shape.append(jax.ShapeDtypeStruct((T, D), BF16))
        out_shape.append(jax.ShapeDtypeStruct((1, D), F32))
    else:
        out_specs = vec
        out_shape = jax.ShapeDtypeStruct((1, D), F32)
    if after is not None:
        in_specs.append(_anchor_spec(after))
        args.append(after)
    return pl.pallas_call(
        body, name=name, grid=(T // tm,),
        in_specs=in_specs, out_specs=out_specs, out_shape=out_shape,
        compiler_params=_row_params(),
    )(*args)


def wgrad(a, b, *, name, tt, tn, out_slabs=None):
    T, K = _mat_shape(a)
    N = _mat_shape(b)[1]
    nt = T // tt
    o_shape = _out_struct(K, N, out_slabs, BF16)

    flipped = K > N and out_slabs is None

    def body(a_ref, b_ref, o_ref, acc_ref):
        t = pl.program_id(0)

        @pl.when(t == 0)
        def _():
            acc_ref[...] = jnp.zeros_like(acc_ref)

        if flipped:
            bt = _all_cols(b_ref).astype(BF16).T
            for c in range(K // tn):
                acc_ref[:, c * tn:(c + 1) * tn] += _dot(bt, _cols(a_ref, c, tn).astype(BF16))
        else:
            at = _all_cols(a_ref).astype(BF16).T
            for c in range(N // tn):
                acc_ref[:, c * tn:(c + 1) * tn] += _dot(at, _cols(b_ref, c, tn).astype(BF16))

        @pl.when(t == nt - 1)
        def _():
            if flipped:
                for c in range(K // tn):
                    o_ref[c * tn:(c + 1) * tn, :] = acc_ref[:, c * tn:(c + 1) * tn].T.astype(BF16)
            else:
                for c in range(N // tn):
                    _set_cols(o_ref, c, tn, acc_ref[:, c * tn:(c + 1) * tn].astype(BF16))

    return pl.pallas_call(
        body, name=name, grid=(nt,),
        in_specs=[_row_block(a, tt), _row_block(b, tt)],
        out_specs=_resident(o_shape), out_shape=o_shape,
        scratch_shapes=[pltpu.VMEM((N, K) if flipped else (K, N), F32)],
        compiler_params=_row_params(),
    )(a, b)


def chunk_triangles(tm):
    r = lax.broadcasted_iota(jnp.int32, (tm, tm), 0)
    c = lax.broadcasted_iota(jnp.int32, (tm, tm), 1)
    same = (r // CHUNK) == (c // CHUNK)
    tri = jnp.stack([same & (c <= r), same & (c >= r)]).astype(F32)
    return tri.astype(BF16), tri


def _tri_spec(tm):
    return pl.BlockSpec((2, tm, tm), lambda g, s, i: (0, 0, 0))


def _chunk_row(v, r, nc):
    return jnp.concatenate([jnp.broadcast_to(v[c * CHUNK + r:c * CHUNK + r + 1], (CHUNK, v.shape[1]))
                            for c in range(nc)], axis=0)


def _block_diag(v, nc):
    chunk = lax.broadcasted_iota(jnp.int32, (v.shape[0], 1), 0) // CHUNK
    return jnp.concatenate([jnp.where(chunk == c, v, jnp.zeros_like(v)) for c in range(nc)], axis=1)


def _pool_windows_back(ext_ref, tm):
    n = tm + 32
    ext_ref[1, 8:n] = ext_ref[0, 8:n] + ext_ref[0, 7:n - 1]
    ext_ref[2, 16:n] = ext_ref[1, 16:n] + ext_ref[1, 14:n - 2]
    ext_ref[3, 24:n] = ext_ref[2, 24:n] + ext_ref[2, 20:n - 4]
    s2 = ext_ref[1, 32:n]
    s4 = ext_ref[2, 32:n]
    s8 = ext_ref[3, 32:n]
    s16 = s8 + ext_ref[3, 24:n - 8]
    return s2, s4, s8, s16


def _pool_windows_fwd(ext_ref, tm):
    n = tm + 32
    ext_ref[1, 0:n - 8] = ext_ref[0, 0:n - 8] + ext_ref[0, 1:n - 7]
    ext_ref[2, 0:n - 16] = ext_ref[1, 0:n - 16] + ext_ref[1, 2:n - 14]
    ext_ref[3, 0:n - 24] = ext_ref[2, 0:n - 24] + ext_ref[2, 4:n - 20]
    s2 = ext_ref[1, 0:tm]
    s4 = ext_ref[2, 0:tm]
    s8 = ext_ref[3, 0:tm]
    s16 = s8 + ext_ref[3, 8:tm + 8]
    return s2, s4, s8, s16


def _select_window(g, s2, s4, s8, s16):
    return jnp.where(g == 0, s2, jnp.where(g == 1, s4, jnp.where(g == 2, s8, s16)))


def _pool_count(g, pos):
    width = lax.shift_left(jnp.int32(2), g)
    return jnp.minimum(pos + 1, width).astype(F32)


def _hgrn_gates(zq, zf, th):
    lb = _sigmoid(th[0:1, :] - th[1:2, :])
    sig = _sigmoid(zf)
    f = lb + (1.0 - lb) * sig
    sq = _sigmoid(zq)
    return lb, sig, f, sq


def mixer_fwd(u5, pool_w_bf, scale4, theta4, gn4, tri_bf, tri_f, *, seqs, seq_len, tm):
    T = u5.shape[1]
    tps = seq_len // tm
    nc = tm // CHUNK
    W = HEAD_W

    H = HEADS_PER_STEP
    heads = range(H)

    def body(u_ref, pw_ref, sc_ref, th_ref, gn_ref, tri_ref, msk_ref, y_ref, o_ref, st_ref, halo_ref, ext_ref, s_ref):
        g = pl.program_id(0)
        i = pl.program_id(2)

        @pl.when(i == 0)
        def _():
            halo_ref[...] = jnp.zeros_like(halo_ref)
            s_ref[...] = jnp.zeros_like(s_ref)

        row = lax.broadcasted_iota(jnp.int32, (tm, 1), 0)
        cols = [slice(h * W, (h + 1) * W) for h in heads]

        for h in heads:
            grp = g * H + h
            up = u_ref[0, :, cols[h]]
            ext_ref[h, 0, 0:16] = jnp.zeros((16, W), F32)
            ext_ref[h, 0, 16:32] = halo_ref[h]
            ext_ref[h, 0, 32:32 + tm] = up
            win = _select_window(grp, *_pool_windows_back(ext_ref.at[h], tm))
            p = win * (1.0 / _pool_count(grp, i * tm + row)) - up
            halo_ref[h] = up[tm - POOL_HALO:tm]
            y_ref[0, :, cols[h]] = (_dot(p.astype(BF16), pw_ref[h]) * sc_ref[h]).astype(BF16)

        zq, zf, zi, zg = u_ref[1], u_ref[2], u_ref[3], u_ref[4]
        th = [th_ref[h] for h in heads]
        lb = jnp.concatenate([_sigmoid(t[0:1, :] - t[1:2, :]) for t in th], axis=1)
        f = lb + (1.0 - lb) * _sigmoid(zf)
        kk = 1.0 - f
        q = zq * _sigmoid(zq)
        G = _tri_apply(tri_ref[0], jnp.log(f))
        Gm, Gl = _chunk_row(G, CHUNK // 2 - 1, nc), _chunk_row(G, CHUNK - 1, nc)
        vb = zi.astype(BF16)
        qrb = (q * jnp.exp(G - Gm)).astype(BF16)
        krb = (kk * jnp.exp(Gm - G)).astype(BF16)
        keb = (kk * jnp.exp(Gl - G)).astype(BF16)
        qgb = (q * jnp.exp(G)).astype(BF16)
        mask = msk_ref[0] > 0.5
        a = [jnp.where(mask, _dot_nt(qrb[:, cols[h]], krb[:, cols[h]]), 0.0).astype(BF16) for h in heads]
        d_st = [_dot_tn(vb[:, cols[h]], _block_diag(keb[:, cols[h]], nc)) for h in heads]
        o_intra = [_dot(a[h], vb[:, cols[h]]) for h in heads]
        st_cat = []
        for h in heads:
            st = s_ref[h]
            states = []
            for c in range(nc):
                st_ref[c, h] = st
                states.append(st.astype(BF16))
                st = st * jnp.exp(G[(c + 1) * CHUNK - 1:(c + 1) * CHUNK, cols[h]]) + d_st[h][:, c * W:(c + 1) * W]
            s_ref[h] = st
            st_cat.append(jnp.concatenate(states, axis=1))
        o = [o_intra[h] + _dot_nt(_block_diag(qgb[:, cols[h]], nc), st_cat[h]) for h in heads]
        gate = zg * _sigmoid(zg)
        for h in heads:
            o_ref[:, cols[h]] = o[h]
            r = lax.rsqrt(jnp.mean(o[h] * o[h], axis=-1, keepdims=True) + EPS)
            y_ref[1, :, cols[h]] = (o[h] * r * gn_ref[h] * gate[:, cols[h]]).astype(BF16)

    def rb(s, i):
        return s * tps + i

    def per_head(*shape):
        return pl.BlockSpec((H,) + shape, lambda g, s, i: (g,) + (0,) * len(shape))

    return pl.pallas_call(
        body, name="mixer_fwd", grid=(4 // H, seqs, tps),
        in_specs=[pl.BlockSpec((5, tm, H * W), lambda g, s, i: (0, rb(s, i), g)),
                  per_head(W, W), per_head(1, W), per_head(2, W), per_head(1, W),
                  _tri_spec(tm), _tri_spec(tm)],
        out_specs=[pl.BlockSpec((2, tm, H * W), lambda g, s, i: (0, rb(s, i), g)),
                   pl.BlockSpec((tm, H * W), lambda g, s, i: (rb(s, i), g)),
                   pl.BlockSpec((nc, H, W, W), lambda g, s, i: (rb(s, i), g, 0, 0))],
        out_shape=[jax.ShapeDtypeStruct((2, T, 4 * W), BF16),
                   jax.ShapeDtypeStruct((T, 4 * W), F32),
                   jax.ShapeDtypeStruct((T // CHUNK, 4, W, W), F32)],
        scratch_shapes=[pltpu.VMEM((H, POOL_HALO, W), F32),
                        pltpu.VMEM((H, 4, tm + 32, W), F32),
                        pltpu.VMEM((H, W, W), F32)],
        compiler_params=_cparams(("arbitrary", "arbitrary", "arbitrary")),
    )(u5, pool_w_bf, scale4, theta4, gn4, tri_bf, tri_f)


def mixer_bwd(u5, dy2, o_pre, st_prev, pool_w_bf, scale4, theta4, gn4, tri_bf, tri_f, *, seqs, seq_len, tm):
    T = u5.shape[1]
    tps = seq_len // tm
    nc = tm // CHUNK
    W = HEAD_W
    hb = tm // POOL_HALO

    H = HEADS_PER_STEP
    heads = range(H)

    def body(u_ref, uh_ref, dy_ref, o_ref, st_ref, pw_ref, sc_ref, th_ref, gn_ref, tri_ref, msk_ref,
             du_ref, dpw_ref, dsc_ref, dlb_ref, dgn_ref, nxt_ref, ext_ref, ds_ref):
        g = pl.program_id(0)
        s = pl.program_id(1)
        i = pl.program_id(2)
        tile = tps - 1 - i
        first = (s == 0) & (i == 0)

        @pl.when(i == 0)
        def _():
            nxt_ref[...] = jnp.zeros_like(nxt_ref)
            ds_ref[...] = jnp.zeros_like(ds_ref)

        row = lax.broadcasted_iota(jnp.int32, (tm, 1), 0)
        cols = [slice(h * W, (h + 1) * W) for h in heads]

        def accumulate(ref, h, val):
            @pl.when(first)
            def _():
                ref[h] = val

            @pl.when(jnp.logical_not(first))
            def _():
                ref[h] += val

        def per_head(fn):
            return jnp.concatenate([jnp.broadcast_to(fn(cols[h]), (tm, W)) for h in heads], axis=1)

        for h in heads:
            grp = g * H + h
            inv_cnt = 1.0 / _pool_count(grp, tile * tm + row)
            ext = ext_ref.at[h]
            up = u_ref[0, :, cols[h]]
            ext[0, 0:16] = jnp.zeros((16, W), F32)
            ext[0, 16:32] = jnp.where(tile == 0, 0.0, uh_ref[:, cols[h]])
            ext[0, 32:32 + tm] = up
            win = _select_window(grp, *_pool_windows_back(ext, tm))
            pb = (win * inv_cnt - up).astype(BF16)
            dyp = dy_ref[0, :, cols[h]]
            z = _dot(pb, pw_ref[h])
            accumulate(dsc_ref, h, jnp.sum(dyp * z, axis=0, keepdims=True))
            dz = (dyp * sc_ref[h]).astype(BF16)
            accumulate(dpw_ref, h, _dot_tn(pb, dz))
            dp = _dot_nt(dz, pw_ref[h])
            e = dp * inv_cnt
            ext[0, 0:tm] = e
            ext[0, tm:tm + 16] = nxt_ref[h]
            ext[0, tm + 16:tm + 32] = jnp.zeros((16, W), F32)
            lead = _select_window(grp, *_pool_windows_fwd(ext, tm))
            nxt_ref[h] = e[0:POOL_HALO]
            du_ref[0, :, cols[h]] = (lead - dp).astype(BF16)

        zq, zf, zi, zg = u_ref[1], u_ref[2], u_ref[3], u_ref[4]
        lb = jnp.concatenate([_sigmoid(th_ref[h][0:1, :] - th_ref[h][1:2, :]) for h in heads], axis=1)
        gn = jnp.concatenate([gn_ref[h] for h in heads], axis=1)
        sig, sq, sg = _sigmoid(zf), _sigmoid(zq), _sigmoid(zg)
        f = lb + (1.0 - lb) * sig
        kk = 1.0 - f
        q = zq * sq
        G = _tri_apply(tri_ref[0], jnp.log(f))

        dyh = dy_ref[1]
        o = o_ref[...]
        sqr = o * o
        r = per_head(lambda cs: lax.rsqrt(jnp.mean(sqr[:, cs], axis=-1, keepdims=True) + EPS))
        orr = o * r
        du_ref[4] = (dyh * (orr * gn) * (sg * (1.0 + zg * (1.0 - sg)))).astype(BF16)
        don = dyh * (zg * sg)
        dgn = jnp.sum(don * orr, axis=0, keepdims=True)
        dog = don * gn
        dog_orr = dog * orr
        do = r * (dog - orr * per_head(lambda cs: jnp.mean(dog_orr[:, cs], axis=-1, keepdims=True)))

        Gm, Gl = _chunk_row(G, CHUNK // 2 - 1, nc), _chunk_row(G, CHUNK - 1, nc)
        e_q, e_k, e_e, e_g = jnp.exp(G - Gm), jnp.exp(Gm - G), jnp.exp(Gl - G), jnp.exp(G)
        qr, kr, ke, qg = q * e_q, kk * e_k, kk * e_e, q * e_g
        qrb, krb, keb, qgb = qr.astype(BF16), kr.astype(BF16), ke.astype(BF16), qg.astype(BF16)
        vb = zi.astype(BF16)
        dob = do.astype(BF16)
        lower, upper = msk_ref[0] > 0.5, msk_ref[1] > 0.5
        da = [jnp.where(lower, _dot_nt(dob[:, cs], vb[:, cs]), 0.0).astype(BF16) for cs in cols]
        a_t = [jnp.where(upper, _dot_nt(krb[:, cs], qrb[:, cs]), 0.0).astype(BF16) for cs in cols]
        da_t = [jnp.where(upper, _dot_nt(vb[:, cs], dob[:, cs]), 0.0).astype(BF16) for cs in cols]
        u_cat = [_dot_tn(dob[:, cs], _block_diag(qgb[:, cs], nc)) for cs in cols]
        dqr = [_dot(da[h], krb[:, cols[h]]) for h in heads]
        dkr = [_dot(da_t[h], qrb[:, cols[h]]) for h in heads]
        dv = [_dot(a_t[h], dob[:, cols[h]]) for h in heads]
        dsn_rows, dsn_cols, ddecay = [], [], [[None] * H for _ in range(nc)]
        for h in heads:
            dsn = ds_ref[h]
            dsn_b = [None] * nc
            for c in reversed(range(nc)):
                decay = jnp.exp(G[(c + 1) * CHUNK - 1:(c + 1) * CHUNK, cols[h]])
                dsn_b[c] = dsn.astype(BF16)
                ddecay[c][h] = jnp.sum(dsn * st_ref[c, h], axis=0, keepdims=True) * decay
                dsn = u_cat[h][:, c * W:(c + 1) * W] + dsn * decay
            ds_ref[h] = dsn
            dsn_rows.append(jnp.concatenate(dsn_b, axis=0))
            dsn_cols.append(jnp.concatenate(dsn_b, axis=1))
        st_rows = [jnp.concatenate([st_ref[c, h].astype(BF16) for c in range(nc)], axis=0) for h in heads]
        dqg = [_dot(_block_diag(dob[:, cols[h]], nc), st_rows[h]) for h in heads]
        dke = [_dot(_block_diag(vb[:, cols[h]], nc), dsn_rows[h]) for h in heads]
        dv = [dv[h] + _dot_nt(_block_diag(keb[:, cols[h]], nc), dsn_cols[h]) for h in heads]
        dqr, dkr, dqg, dke, dv = (jnp.concatenate(parts, axis=1) for parts in (dqr, dkr, dqg, dke, dv))
        t_mid, t_qg, t_ke = dkr * kr - dqr * qr, dqg * qg, dke * ke
        dq = dqr * e_q + dqg * e_g
        dk = dkr * e_k + dke * e_e
        crow = lax.broadcasted_iota(jnp.int32, (CHUNK, 1), 0)
        ends = []
        for c in range(nc):
            sl = slice(c * CHUNK, (c + 1) * CHUNK)
            dgm = jnp.sum(t_mid[sl], axis=0, keepdims=True)
            dgl = jnp.sum(t_ke[sl], axis=0, keepdims=True) + jnp.concatenate(ddecay[c], axis=1)
            ends.append(jnp.where(crow == CHUNK // 2 - 1, dgm, 0.0) + jnp.where(crow == CHUNK - 1, dgl, 0.0))
        dG = t_qg - t_ke - t_mid + jnp.concatenate(ends, axis=0)
        dlogf = _tri_apply(tri_ref[1], dG)
        df = dlogf / f - dk
        du_ref[1] = (dq * (sq * (1.0 + zq * (1.0 - sq)))).astype(BF16)
        du_ref[2] = (df * (1.0 - lb) * (sig * (1.0 - sig))).astype(BF16)
        du_ref[3] = dv.astype(BF16)
        dlb = jnp.sum(df * (1.0 - sig), axis=0, keepdims=True) * (lb * (1.0 - lb))
        for h in heads:
            accumulate(dgn_ref, h, dgn[:, cols[h]])
            accumulate(dlb_ref, h, dlb[:, cols[h]])

    def rb(s, i):
        return s * tps + (tps - 1 - i)

    def per_head_spec(*shape):
        return pl.BlockSpec((H,) + shape, lambda g, s, i: (g,) + (0,) * len(shape))

    vec, mat = per_head_spec(1, W), per_head_spec(W, W)
    return pl.pallas_call(
        body, name="mixer_bwd", grid=(4 // H, seqs, tps),
        in_specs=[pl.BlockSpec((5, tm, H * W), lambda g, s, i: (0, rb(s, i), g)),
                  pl.BlockSpec((None, POOL_HALO, H * W), lambda g, s, i: (0, jnp.maximum(rb(s, i) * hb - 1, 0), g)),
                  pl.BlockSpec((2, tm, H * W), lambda g, s, i: (0, rb(s, i), g)),
                  pl.BlockSpec((tm, H * W), lambda g, s, i: (rb(s, i), g)),
                  pl.BlockSpec((nc, H, W, W), lambda g, s, i: (rb(s, i), g, 0, 0)),
                  mat, vec, per_head_spec(2, W), vec, _tri_spec(tm), _tri_spec(tm)],
        out_specs=[pl.BlockSpec((5, tm, H * W), lambda g, s, i: (0, rb(s, i), g)), mat, vec, vec, vec],
        out_shape=[jax.ShapeDtypeStruct((5, T, 4 * W), BF16),
                   jax.ShapeDtypeStruct((4, W, W), F32),
                   jax.ShapeDtypeStruct((4, 1, W), F32),
                   jax.ShapeDtypeStruct((4, 1, W), F32),
                   jax.ShapeDtypeStruct((4, 1, W), F32)],
        scratch_shapes=[pltpu.VMEM((H, POOL_HALO, W), F32),
                        pltpu.VMEM((H, 4, tm + 32, W), F32),
                        pltpu.VMEM((H, W, W), F32)],
        compiler_params=_cparams(("arbitrary", "arbitrary", "arbitrary")),
    )(u5, u5, dy2, o_pre, st_prev, pool_w_bf, scale4, theta4, gn4, tri_bf, tri_f)


def _attn_probs(q, k, hd):
    s = _dot_nt(q, k) * (1.0 / (hd ** 0.5))
    e = jnp.exp(s - jnp.max(s, axis=-1, keepdims=True))
    return e * (1.0 / jnp.sum(e, axis=-1, keepdims=True))


def attn_fwd(q, kv3, *, seqs, seq_len, n_mem, tm):
    T, D = q.shape
    hd = D // XATTN_HEADS
    tps = seq_len // tm

    cols = [slice(h * hd, (h + 1) * hd) for h in range(XATTN_HEADS)]

    def body(q_ref, kv_ref, o_ref):
        p = [_attn_probs(q_ref[:, cs], kv_ref[0, :, cs], hd) for cs in cols]
        for h, cs in enumerate(cols):
            o_ref[:, cs] = _dot(p[h].astype(BF16), kv_ref[1, :, cs]).astype(BF16)

    return pl.pallas_call(
        body, name="attn_fwd", grid=(seqs, tps),
        in_specs=[pl.BlockSpec((tm, D), lambda b, i: (b * tps + i, 0)),
                  pl.BlockSpec((2, n_mem, D), lambda b, i: (0, b, 0))],
        out_specs=pl.BlockSpec((tm, D), lambda b, i: (b * tps + i, 0)),
        out_shape=jax.ShapeDtypeStruct((T, D), BF16),
        compiler_params=_cparams(("parallel", "arbitrary")),
    )(q, kv3)


def attn_bwd(q, kv3, do, *, seqs, seq_len, n_mem, tm):
    T, D = q.shape
    hd = D // XATTN_HEADS
    tps = seq_len // tm

    cols = [slice(h * hd, (h + 1) * hd) for h in range(XATTN_HEADS)]

    def body(q_ref, kv_ref, do_ref, dq_ref, dkv_ref):
        i = pl.program_id(1)

        @pl.when(i == 0)
        def _():
            dkv_ref[...] = jnp.zeros_like(dkv_ref)

        p = [_attn_probs(q_ref[:, cs], kv_ref[0, :, cs], hd) for cs in cols]
        dp = [_dot_nt(do_ref[:, cs], kv_ref[1, :, cs]) for cs in cols]
        ds = [(p[h] * (dp[h] - jnp.sum(dp[h] * p[h], axis=-1, keepdims=True)) * (1.0 / (hd ** 0.5))).astype(BF16)
              for h in range(XATTN_HEADS)]
        for h, cs in enumerate(cols):
            dq_ref[:, cs] = _dot(ds[h], kv_ref[0, :, cs]).astype(BF16)
            dkv_ref[0, :, cs] += _dot_tn(ds[h], q_ref[:, cs])
            dkv_ref[1, :, cs] += _dot_tn(p[h].astype(BF16), do_ref[:, cs])

    qspec = pl.BlockSpec((tm, D), lambda b, i: (b * tps + i, 0))
    kvspec = pl.BlockSpec((2, n_mem, D), lambda b, i: (0, b, 0))
    return pl.pallas_call(
        body, name="attn_bwd", grid=(seqs, tps),
        in_specs=[qspec, kvspec, qspec],
        out_specs=[qspec, kvspec],
        out_shape=[jax.ShapeDtypeStruct((T, D), BF16), jax.ShapeDtypeStruct((2, seqs * n_mem, D), F32)],
        compiler_params=_cparams(("parallel", "arbitrary")),
    )(q, kv3, do)


def final_loss(h, g, target, *, tm):
    T, D = h.shape

    def body(h_ref, g_ref, t_ref, dh_ref, dhb_ref, ls_ref, dg_ref):
        i = pl.program_id(0)
        x = h_ref[...]
        gv = g_ref[...]
        r = lax.rsqrt(jnp.mean(x * x, axis=-1, keepdims=True) + EPS)
        xr = x * r
        d = xr * gv - t_ref[...]
        dy = d * (1.0 / D)
        dyg = dy * gv
        dx = r * (dyg - xr * jnp.mean(dyg * xr, axis=-1, keepdims=True))
        dh_ref[...] = dx
        dhb_ref[...] = dx.astype(BF16)
        ls = jnp.sum(d * d, axis=0, keepdims=True)
        dg = jnp.sum(dy * xr, axis=0, keepdims=True)

        @pl.when(i == 0)
        def _():
            ls_ref[...] = ls
            dg_ref[...] = dg

        @pl.when(i > 0)
        def _():
            ls_ref[...] += ls
            dg_ref[...] += dg

    row = pl.BlockSpec((tm, D), lambda i: (i, 0))
    vec = pl.BlockSpec((1, D), lambda i: (0, 0))
    return pl.pallas_call(
        body, name="final_loss", grid=(T // tm,),
        in_specs=[row, vec, row], out_specs=[row, row, vec, vec],
        out_shape=[jax.ShapeDtypeStruct((T, D), F32), jax.ShapeDtypeStruct((T, D), BF16),
                   jax.ShapeDtypeStruct((1, D), F32), jax.ShapeDtypeStruct((1, D), F32)],
        compiler_params=_cparams(("arbitrary",)),
    )(h, g, target)


def _my_place():
    return lax.axis_index("x"), lax.axis_index("y"), lax.axis_index("c")


def _slot_of(px, py, pc):
    return 4 * px + 2 * py + pc


def _peer(k, x, y, c):
    return (1 - x if (k >> 2) & 1 else x, 1 - y if (k >> 1) & 1 else y, 1 - c if k & 1 else c)


def _split_copies(src_refs, land_refs, send_sems, recv_sems, scatter):
    x, y, c = _my_place()
    mine = _slot_of(x, y, c)
    copies = []
    for a, (src, land) in enumerate(zip(src_refs, land_refs)):
        for k in range(1, N_DEV):
            peer = _peer(k, x, y, c)
            copies.append(pltpu.make_async_remote_copy(
                src_ref=src.at[_slot_of(*peer)] if scatter else src, dst_ref=land.at[mine],
                send_sem=send_sems.at[a * N_PEERS + k - 1], recv_sem=recv_sems.at[a * N_PEERS + k - 1],
                device_id=peer, device_id_type=MESH))
    return copies


def split_start(groups, *, name, scatter):
    sizes = [len(srcs) for srcs, _ in groups]
    n_arr = sum(sizes)
    flat = [a for srcs, lands in groups for a in list(srcs) + list(lands)]

    def body(*refs):
        ins = refs[:2 * n_arr]
        sems = refs[4 * n_arr:4 * n_arr + 2 * len(groups)]
        token = refs[-1]
        at = 0
        for gi, n in enumerate(sizes):
            for cp in _split_copies(ins[at:at + n], ins[at + n:at + 2 * n], sems[2 * gi], sems[2 * gi + 1], scatter):
                cp.start()
            at += 2 * n
        token[...] = jnp.zeros_like(token)

    sem_shapes = []
    for n in sizes:
        sem_shapes += [pltpu.SemaphoreType.DMA((n * N_PEERS,))] * 2
    outs = pl.pallas_call(
        body, name=name,
        out_shape=tuple(pltpu.HBM(a.shape, a.dtype) for a in flat) + tuple(sem_shapes)
        + (jax.ShapeDtypeStruct((8, 128), F32),),
        in_specs=(HBM,) * len(flat),
        out_specs=(HBM,) * len(flat) + (SEM,) * len(sem_shapes) + (pl.BlockSpec(memory_space=pltpu.VMEM),),
        input_output_aliases={i: i for i in range(len(flat))},
        compiler_params=pltpu.CompilerParams(has_side_effects=pltpu.SideEffectType.DATAFLOW_SIDE_EFFECTING),
    )(*[pltpu.with_memory_space_constraint(a, pltpu.HBM) for a in flat])
    thru, sems, token = outs[:len(flat)], outs[len(flat):-1], outs[-1]
    started, at = [], 0
    for gi, n in enumerate(sizes):
        started.append((sems[2 * gi], sems[2 * gi + 1], thru[at:at + n], thru[at + n:at + 2 * n]))
        at += 2 * n
    return started, token


def split_wait(started, after, *, name, scatter):
    sizes = [len(g[2]) for g in started]
    n_arr = sum(sizes)
    flat = [a for g in started for a in list(g[2]) + list(g[3])]
    sems = [s for g in started for s in g[:2]]

    def body(*refs):
        ins = refs[:2 * n_arr]
        sem_refs = refs[2 * n_arr:2 * n_arr + len(sems)]
        at = 0
        for gi, n in enumerate(sizes):
            for cp in _split_copies(ins[at:at + n], ins[at + n:at + 2 * n], sem_refs[2 * gi], sem_refs[2 * gi + 1], scatter):
                cp.wait_send()
                cp.wait_recv()
            at += 2 * n

    outs = pl.pallas_call(
        body, name=name,
        out_shape=tuple(pltpu.HBM(a.shape, a.dtype) for a in flat),
        in_specs=(HBM,) * len(flat) + (SEM,) * len(sems) + (pl.BlockSpec(memory_space=pl.ANY),),
        out_specs=(HBM,) * len(flat),
        input_output_aliases={i: i for i in range(len(flat))},
        compiler_params=pltpu.CompilerParams(has_side_effects=pltpu.SideEffectType.DATAFLOW_SIDE_EFFECTING),
    )(*flat, *sems, after)
    done, at = [], 0
    for n in sizes:
        done.append((outs[at:at + n], outs[at + n:at + 2 * n]))
        at += 2 * n
    return done


def allgather_small(bufs):
    n = len(bufs)

    def body(*refs):
        srcs, outs = refs[:n], refs[n:2 * n]
        send_sems, recv_sems, local_sems = refs[2 * n:]
        x, y, c = _my_place()
        mine = _slot_of(x, y, c)
        local = [pltpu.make_async_copy(s, o.at[mine], local_sems.at[a]) for a, (s, o) in enumerate(zip(srcs, outs))]
        for cp in local:
            cp.start()
        copies = _split_copies(srcs, outs, send_sems, recv_sems, False)
        for cp in copies:
            cp.start()
        for cp in copies:
            cp.wait()
        for cp in local:
            cp.wait()

    return pl.pallas_call(
        body, name="allgather_small",
        out_shape=[jax.ShapeDtypeStruct((N_DEV,) + b.shape, b.dtype) for b in bufs],
        in_specs=[HBM] * n, out_specs=[HBM] * n,
        scratch_shapes=[pltpu.SemaphoreType.DMA((n * N_PEERS,)), pltpu.SemaphoreType.DMA((n * N_PEERS,)),
                        pltpu.SemaphoreType.DMA((n,))],
    )(*bufs)


def _adamw_math(g, w, m, v):
    c1 = 1.0 - ADAM_B1 ** ADAM_STEP
    c2 = 1.0 - ADAM_B2 ** ADAM_STEP
    nm = ADAM_B1 * m + (1.0 - ADAM_B1) * g
    nv = ADAM_B2 * v + (1.0 - ADAM_B2) * (g * g)
    delta = -ADAM_LR * ((nm / c1) / (jnp.sqrt(nv / c2) + ADAM_EPS) + ADAM_WD * w)
    return delta, nm, nv


def adamw_sharded(me, own, recv, w, m, v, *, name, tr):
    R, C = w.shape

    def body(me_ref, *refs):
        parts = refs[:N_DEV]
        w_ref, m_ref, v_ref, g_ref, d_ref, nm_ref, nv_ref = refs[N_DEV:]
        g = parts[0][...].astype(F32)
        for p in parts[1:]:
            g = g + p[...].astype(F32)
        g_ref[...] = g
        d_ref[...], nm_ref[...], nv_ref[...] = _adamw_math(g, w_ref[...], m_ref[...], v_ref[...])

    def slab(k):
        return pl.BlockSpec((None, tr, C), lambda i, me_ref: (me_ref[0] ^ k, i, 0))

    blk = pl.BlockSpec((tr, C), lambda i, me_ref: (i, 0))
    out = jax.ShapeDtypeStruct((R, C), F32)
    return pl.pallas_call(
        body, name=name,
        grid_spec=pltpu.PrefetchScalarGridSpec(
            num_scalar_prefetch=1, grid=(R // tr,),
            in_specs=[slab(k) for k in range(N_DEV)] + [blk, blk, blk],
            out_specs=[blk, blk, blk, blk]),
        out_shape=[out, out, out, out],
        compiler_params=_cparams(("parallel",)),
    )(me, own, *([recv] * N_PEERS), w, m, v)


def adamw_replicated(parts, ws, ms, vs, rows):
    n_buf, n_par = len(parts), len(ws)

    def body(*refs):
        p_refs = refs[:n_buf]
        w_refs = refs[n_buf:n_buf + n_par]
        m_refs = refs[n_buf + n_par:n_buf + 2 * n_par]
        v_refs = refs[n_buf + 2 * n_par:n_buf + 3 * n_par]
        outs = refs[n_buf + 3 * n_par:]
        sums = []
        for p in p_refs:
            g = p[0]
            for s in range(1, N_DEV):
                g = g + p[s]
            sums.append(g)
        for j, (b, r0, nr) in enumerate(rows):
            g = sums[b][r0:r0 + nr]
            delta, nm, nv = _adamw_math(g, w_refs[j][...], m_refs[j][...], v_refs[j][...])
            outs[j][...] = g
            outs[n_par + j][...] = delta
            outs[2 * n_par + j][...] = nm
            outs[3 * n_par + j][...] = nv

    shapes = [jax.ShapeDtypeStruct(w.shape, F32) for w in ws]
    outs = pl.pallas_call(
        body, name="adamw_replicated", out_shape=shapes * 4,
        compiler_params=pltpu.CompilerParams(vmem_limit_bytes=V7X_VMEM_LIMIT),
    )(*parts, *ws, *ms, *vs)
    return outs[:n_par], outs[n_par:2 * n_par], outs[2 * n_par:3 * n_par], outs[3 * n_par:]


BIG = ("w_in", "w_out", "xw_q", "xw_kv", "xw_o", "w_up", "w_down")
COL_SHARDED = ("w_in", "xw_kv", "w_up")
WEIGHTS = ("norm_mix", "w_in", "pool_w", "pool_scale", "lb_theta", "hgrn_norm", "w_out", "norm_xq",
           "norm_mem", "xw_q", "xw_kv", "xw_o", "norm_mlp", "w_up", "w_down", "norm_final")
SMALL = (("pool_w", (4 * HEAD_W, HEAD_W), 0, 0),
         ("norm_mix", (1, 1024), 1, 0), ("norm_xq", (1, 1024), 1, 1), ("norm_mem", (1, 1024), 1, 2),
         ("norm_mlp", (1, 1024), 1, 3), ("norm_final", (1, 1024), 1, 4),
         ("pool_scale", (1, 512), 2, 0), ("hgrn_norm", (1, 512), 2, 1), ("lb_theta", (2, 512), 2, 2))


def _pad_rows(a, rows):
    return jnp.concatenate([a, jnp.zeros((rows - a.shape[0], a.shape[1]), a.dtype)], axis=0)


def kernel(x, mem, norm_mix, w_in, pool_w, pool_scale, lb_theta, hgrn_norm, w_out, norm_xq, norm_mem, xw_q, xw_kv, xw_o, norm_mlp, w_up, w_down, norm_final, loss_target, m_norm_mix, m_w_in, m_pool_w, m_pool_scale, m_lb_theta, m_hgrn_norm, m_w_out, m_norm_xq, m_norm_mem, m_xw_q, m_xw_kv, m_xw_o, m_norm_mlp, m_w_up, m_w_down, m_norm_final, v_norm_mix, v_w_in, v_pool_w, v_pool_scale, v_lb_theta, v_hgrn_norm, v_w_out, v_norm_xq, v_norm_mem, v_xw_q, v_xw_kv, v_xw_o, v_norm_mlp, v_w_up, v_w_down, v_norm_final):
    w = dict(norm_mix=norm_mix, w_in=w_in, pool_w=pool_w, pool_scale=pool_scale, lb_theta=lb_theta,
             hgrn_norm=hgrn_norm, w_out=w_out, norm_xq=norm_xq, norm_mem=norm_mem, xw_q=xw_q, xw_kv=xw_kv,
             xw_o=xw_o, norm_mlp=norm_mlp, w_up=w_up, w_down=w_down, norm_final=norm_final)
    mom = dict(norm_mix=m_norm_mix, w_in=m_w_in, pool_w=m_pool_w, pool_scale=m_pool_scale, lb_theta=m_lb_theta,
               hgrn_norm=m_hgrn_norm, w_out=m_w_out, norm_xq=m_norm_xq, norm_mem=m_norm_mem, xw_q=m_xw_q,
               xw_kv=m_xw_kv, xw_o=m_xw_o, norm_mlp=m_norm_mlp, w_up=m_w_up, w_down=m_w_down,
               norm_final=m_norm_final)
    var = dict(norm_mix=v_norm_mix, w_in=v_w_in, pool_w=v_pool_w, pool_scale=v_pool_scale, lb_theta=v_lb_theta,
               hgrn_norm=v_hgrn_norm, w_out=v_w_out, norm_xq=v_norm_xq, norm_mem=v_norm_mem, xw_q=v_xw_q,
               xw_kv=v_xw_kv, xw_o=v_xw_o, norm_mlp=v_norm_mlp, w_up=v_w_up, w_down=v_w_down,
               norm_final=v_norm_final)

    seqs, seq_len, D = x.shape
    n_mem = mem.shape[1]
    T = seqs * seq_len
    W = HEAD_W
    x2 = x.reshape(T, D)
    mem2 = mem.reshape(seqs * n_mem, D)
    tgt2 = loss_target.reshape(T, D)
    tm_big = min(1024, T)
    tm_mid = min(512, T)
    tm_sq = min(1024, T)
    tm_mix = min(256, seq_len)
    tm_att = min(1024, seq_len)
    tkv = min(512, seqs * n_mem)
    px, py, pc = _my_place()
    me = _slot_of(px, py, pc).astype(jnp.int32)
    me1 = me.reshape(1)

    shard_bf = {n: w[n][0].astype(BF16) for n in BIG}

    def landing(n):
        zone = lax.empty((N_DEV,) + shard_bf[n].shape, BF16)
        return lax.dynamic_update_slice(zone, shard_bf[n][None], (me, 0, 0))

    ag_groups = (("w_in",), ("w_out", "xw_q", "xw_kv", "xw_o"), ("w_up", "w_down"))
    ag_started, _ = split_start([([shard_bf[n] for n in grp], [landing(n) for n in grp]) for grp in ag_groups],
                                name="weights_gather_start", scatter=False)

    pool_w_bf = pool_w[0].astype(BF16)
    scale4 = pool_scale.reshape(4, 1, W)
    gn4 = hgrn_norm.reshape(4, 1, W)
    theta4 = lb_theta.reshape(2, 4, W).transpose(1, 0, 2)
    g_final = norm_final.reshape(1, D)

    n1 = prenorm(x2, norm_mix, tm=tm_sq)
    (_, (wi3,)), = split_wait(ag_started[:1], n1, name="weights_gather_wait_in", scatter=False)
    full_w_in = wi3.transpose(1, 0, 2).reshape(D, -1)
    u5 = proj_plain(n1, full_w_in, name="in_proj", tm=tm_mid, tn=4 * W, out_dtype=F32, out_slabs=5)
    tri_bf, tri_f = chunk_triangles(tm_mix)
    y2, o_pre, st_prev = mixer_fwd(u5, pool_w_bf, scale4, theta4, gn4, tri_bf, tri_f, seqs=seqs, seq_len=seq_len,
                                   tm=tm_mix)
    (_, (wo3, wq3, wkv3, wao3)), = split_wait(ag_started[1:2], y2, name="weights_gather_wait_attn", scatter=False)
    full_w_out, full_xw_q, full_xw_o = wo3.reshape(D, D), wq3.reshape(D, D), wao3.reshape(D, D)
    tn = 4 * W
    h1, n2, q = proj_res_norm(y2, full_w_out, x2, norm_xq, full_xw_q, name="out_q_proj", tm=tm_sq, tn=tn)
    kv3, memn = proj_norm(mem2, norm_mem, wkv3, name="kv_proj", tm=tkv, tn=wkv3.shape[2], out_dtype=BF16,
                          out_slabs=2)
    o_att = attn_fwd(q, kv3, seqs=seqs, seq_len=seq_len, n_mem=n_mem, tm=tm_att)
    h2, n3 = proj_res_norm(o_att, full_xw_o, h1, norm_mlp, name="attn_out_proj", tm=tm_sq, tn=tn)
    (_, (wup3, wdn3)), = split_wait(ag_started[2:3], h2, name="weights_gather_wait_mlp", scatter=False)
    full_w_down = wdn3.reshape(-1, D)
    tn_up = wup3.shape[2]
    aa = proj_plain(n3, wup3, name="up_proj", tm=tm_mid, tn=tn_up, relu2=True)
    dh3, dh3b, sq_err, dg_final = proj_res_loss(aa, full_w_down, h2, g_final, tgt2, name="down_proj_loss",
                                                tm=tm_mid, tn=tn)

    def send(parts, name):
        srcs = [p.reshape((N_DEV, -1, p.shape[-1])) for p in parts]
        lands = [lax.empty(s.shape, BF16) for s in srcs]
        started, token = split_start([(srcs, lands)], name=name, scatter=True)
        return started[0], token

    gw_down = wgrad(aa, dh3b, name="down_proj_wgrad", tt=tm_mid, tn=tn)
    sent_down, tok = send([gw_down], "grads_send_down")
    dap = back_plain(dh3b, full_w_down, name="down_proj_bwd", tm=tm_mid, tn=tn, out_dtype=BF16, relu2_value=aa,
                     after=tok)
    gw_up = wgrad(n3, dap, name="up_proj_wgrad", tt=tm_mid, tn=tn_up, out_slabs=N_DEV)
    sent_up, tok = send([gw_up], "grads_send_up")
    dh2, dh2b, dg_mlp = back_norm(dap, wup3, h2, norm_mlp, dh3, name="up_proj_bwd", tm=tm_mid, tk=tn_up, after=tok)
    do_att = back_plain(dh2b, full_xw_o, name="attn_out_proj_bwd", tm=tm_sq, tn=tn, out_dtype=BF16)
    gxw_o = wgrad(o_att, dh2b, name="attn_out_proj_wgrad", tt=tm_sq, tn=tn)
    dq, dkv3 = attn_bwd(q, kv3, do_att, seqs=seqs, seq_len=seq_len, n_mem=n_mem, tm=tm_att)
    gxw_q = wgrad(n2, dq, name="q_proj_wgrad", tt=tm_sq, tn=tn)
    gxw_kv = wgrad(memn, dkv3, name="kv_proj_wgrad", tt=tkv, tn=wkv3.shape[2], out_slabs=N_DEV)
    sent_attn, tok = send([gxw_o, gxw_q, gxw_kv], "grads_send_attn")
    dg_mem = back_norm(dkv3, wkv3, mem2, norm_mem, None, name="kv_proj_bwd", tm=tkv, tk=wkv3.shape[2])
    dh1, dh1b, dg_xq = back_norm(dq, full_xw_q, h1, norm_xq, dh2, name="q_proj_bwd", tm=tm_sq, tk=D, after=tok)
    gw_out = wgrad(y2, dh1b, name="out_proj_wgrad", tt=tm_sq, tn=tn)
    sent_out, tok = send([gw_out], "grads_send_out")
    dy2 = back_plain(dh1b, full_w_out, name="out_proj_bwd", tm=tm_sq, tn=tn, out_dtype=F32, out_slabs=2, after=tok)
    du5, dpw, dsc, dlb, dgn = mixer_bwd(u5, dy2, o_pre, st_prev, pool_w_bf, scale4, theta4, gn4, tri_bf, tri_f,
                                        seqs=seqs, seq_len=seq_len, tm=tm_mix)
    gw_in = wgrad(n1, du5, name="in_proj_wgrad", tt=tm_mid, tn=tn)
    gw_in_slots = gw_in.reshape(D, N_DEV, -1).transpose(1, 0, 2)
    sent_in, tok = send([gw_in_slots], "grads_send_in")
    dx, dg_mix = back_norm(du5, full_w_in, x2, norm_mix, dh1, name="in_proj_bwd", tm=tm_mid, tk=tn, bf16_copy=False,
                           after=tok)

    dlb_row = dlb.reshape(1, 4 * W)
    buf_vec = _pad_rows(jnp.concatenate([dg_mix, dg_xq, dg_mem, dg_mlp, dg_final, sq_err], axis=0), 8)
    buf_half = _pad_rows(jnp.concatenate([dsc.reshape(1, 4 * W), dgn.reshape(1, 4 * W), dlb_row, -dlb_row], axis=0), 8)
    small_src = [dpw.reshape(4 * W, W), buf_vec, buf_half]
    small_land = [lax.dynamic_update_slice(lax.empty((N_DEV,) + b.shape, F32), b[None], (me, 0, 0))
                  for b in small_src]
    small_started, tok = split_start([(small_src, small_land)], name="small_grads_start", scatter=False)

    done = split_wait([sent_down, sent_up, sent_attn, sent_out, sent_in], tok, name="grads_wait", scatter=True)
    slots = dict(w_down=(0, 0), w_up=(1, 0), xw_o=(2, 0), xw_q=(2, 1), xw_kv=(2, 2), w_out=(3, 0), w_in=(4, 0))
    own = {n: done[gi][0][ai] for n, (gi, ai) in slots.items()}
    got = {n: done[gi][1][ai] for n, (gi, ai) in slots.items()}
    res = {}
    for n in BIG:
        shp = w[n].shape
        r = adamw_sharded(me1, own[n], got[n], w[n][0], mom[n][0], var[n][0], name="adamw_" + n,
                          tr=min(256, shp[1]))
        for kind, a in zip("gdmv", r):
            res[kind, n] = a.reshape(shp)
    (_, small_parts), = split_wait(small_started, res["g", BIG[-1]], name="small_grads_wait", scatter=False)
    loss = 0.5 * jnp.sum(small_parts[1][:, 5, :]) / D
    r = adamw_replicated(small_parts, [w[n].reshape(v2) for n, v2, _, _ in SMALL],
                         [mom[n].reshape(v2) for n, v2, _, _ in SMALL],
                         [var[n].reshape(v2) for n, v2, _, _ in SMALL],
                         [(b, r0, v2[0]) for _, v2, b, r0 in SMALL])
    for kind, arrs in zip("gdmv", r):
        for (n, _, _, _), a in zip(SMALL, arrs):
            res[kind, n] = a.reshape(w[n].shape)

    out = [loss, dx.reshape(x.shape)]
    for kind in "gdmv":
        out += [res[kind, n] for n in WEIGHTS]
    return tuple(out)
```

```python
import jax
import jax.numpy as jnp
from jax import lax
from jax.experimental import pallas as pl
from jax.experimental.pallas import tpu as pltpu

F32 = jnp.float32
BF16 = jnp.bfloat16
EPS = 1e-6
CHUNK = 64
POOL_HALO = 16
HEAD_W = 128
HEADS_PER_STEP = 4
XATTN_HEADS = 4
N_DEV = 8
N_PEERS = N_DEV - 1
ADAM_LR = 0.001
ADAM_B1 = 0.9
ADAM_B2 = 0.999
ADAM_EPS = 1e-08
ADAM_WD = 0.01
ADAM_STEP = 10
V7X_VMEM_LIMIT = 52 * 1024 * 1024
MESH = pl.DeviceIdType.MESH
HBM = pl.BlockSpec(memory_space=pltpu.HBM)
SEM = pl.BlockSpec(memory_space=pltpu.SEMAPHORE)


def _cparams(dims):
    return pltpu.CompilerParams(dimension_semantics=dims, vmem_limit_bytes=V7X_VMEM_LIMIT)


def _sigmoid(v):
    return 0.5 * jnp.tanh(0.5 * v) + 0.5


def _dot(a, b):
    return jnp.dot(a, b, preferred_element_type=F32)


def _dot_nt(a, b):
    return lax.dot_general(a, b, (((1,), (1,)), ((), ())), preferred_element_type=F32)


def _dot_tn(a, b):
    return lax.dot_general(a, b, (((0,), (0,)), ((), ())), preferred_element_type=F32)


def _split3(v):
    hi = v.astype(BF16)
    r1 = v - hi.astype(F32)
    mid = r1.astype(BF16)
    lo = (r1 - mid.astype(F32)).astype(BF16)
    return hi, mid, lo


def _tri_apply(tri, v):
    hi, mid, lo = _split3(v)
    return _dot(tri, hi) + _dot(tri, mid) + _dot(tri, lo)


def _mat_shape(a):
    return a.shape if a.ndim == 2 else (a.shape[1], a.shape[0] * a.shape[2])


def _tile_spec(a, rows, cols, row_of, col_of):
    if a.ndim == 2:
        return pl.BlockSpec((rows, cols), lambda *g: (row_of(*g), col_of(*g)))
    per = a.shape[2] // cols
    return pl.BlockSpec((None, rows, cols), lambda *g: (col_of(*g) // per, row_of(*g), col_of(*g) % per))


def _out_struct(rows, n, slabs, dtype):
    return jax.ShapeDtypeStruct((rows, n) if slabs is None else (slabs, rows, n // slabs), dtype)


def norm_mm(h, g, w, *, name, tm, tn, out_dtype, out_slabs=None):
    T, D = h.shape
    N = _mat_shape(w)[1]
    o_shape = _out_struct(T, N, out_slabs, out_dtype)

    def body(h_ref, g_ref, w_ref, o_ref, n_ref):
        @pl.when(pl.program_id(1) == 0)
        def _():
            x = h_ref[...]
            r = lax.rsqrt(jnp.mean(x * x, axis=-1, keepdims=True) + EPS)
            n_ref[...] = (x * r * g_ref[...]).astype(BF16)

        o_ref[...] = _dot(n_ref[...], w_ref[...]).astype(o_ref.dtype)

    return pl.pallas_call(
        body, name=name, grid=(T // tm, N // tn),
        in_specs=[pl.BlockSpec((tm, D), lambda i, j: (i, 0)),
                  pl.BlockSpec((1, D), lambda i, j: (0, 0)),
                  _tile_spec(w, D, tn, lambda i, j: 0, lambda i, j: j)],
        out_specs=[_tile_spec(o_shape, tm, tn, lambda i, j: i, lambda i, j: j),
                   pl.BlockSpec((tm, D), lambda i, j: (i, 0))],
        out_shape=[o_shape, jax.ShapeDtypeStruct((T, D), BF16)],
        compiler_params=_cparams(("parallel", "arbitrary")),
    )(h, g, w)


def mm_nn(a, w, res, *, name, tm, tn, tk, relu2=False):
    T, K = _mat_shape(a)
    N = w.shape[1]
    nk = K // tk

    def body(a_ref, w_ref, r_ref, o_ref, acc_ref):
        k = pl.program_id(2)
        av = a_ref[...]
        if relu2:
            av = jnp.maximum(av, 0.0)
            av = av * av
        part = _dot(av.astype(BF16), w_ref[...])

        @pl.when(k == 0)
        def _():
            acc_ref[...] = part

        @pl.when(k > 0)
        def _():
            acc_ref[...] += part

        @pl.when(k == nk - 1)
        def _():
            o_ref[...] = r_ref[...] + acc_ref[...]

    return pl.pallas_call(
        body, name=name, grid=(T // tm, N // tn, nk),
        in_specs=[_tile_spec(a, tm, tk, lambda i, j, k: i, lambda i, j, k: k),
                  pl.BlockSpec((tk, tn), lambda i, j, k: (k, j)),
                  pl.BlockSpec((tm, tn), lambda i, j, k: (i, j))],
        out_specs=pl.BlockSpec((tm, tn), lambda i, j, k: (i, j)),
        out_shape=jax.ShapeDtypeStruct((T, N), F32),
        scratch_shapes=[pltpu.VMEM((tm, tn), F32)],
        compiler_params=_cparams(("parallel", "parallel", "arbitrary")),
    )(a, w, res)


def mm_nt(a, w, *, name, tm, tn, tk, out_dtype, out_slabs=None, relu2_of=None, after=None):
    T, K = _mat_shape(a)
    nk = K // tk
    N = w.shape[0]
    has_z = relu2_of is not None
    o_shape = _out_struct(T, N, out_slabs, out_dtype)

    def body(*refs):
        a_ref, w_ref = refs[0], refs[1]
        z_ref = refs[2] if has_z else None
        o_ref, acc_ref = refs[-2], refs[-1]
        k = pl.program_id(2)
        part = _dot_nt(a_ref[...].astype(BF16), w_ref[...])

        @pl.when(k == 0)
        def _():
            acc_ref[...] = part

        @pl.when(k > 0)
        def _():
            acc_ref[...] += part

        @pl.when(k == nk - 1)
        def _():
            out = acc_ref[...]
            if has_z:
                out = out * (2.0 * jnp.maximum(z_ref[...], 0.0))
            o_ref[...] = out.astype(o_ref.dtype)

    in_specs = [_tile_spec(a, tm, tk, lambda i, j, k: i, lambda i, j, k: k),
                pl.BlockSpec((tn, tk), lambda i, j, k: (j, k))]
    args = [a, w]
    if has_z:
        in_specs.append(pl.BlockSpec((tm, tn), lambda i, j, k: (i, j)))
        args.append(relu2_of)
    if after is not None:
        in_specs.append(pl.BlockSpec(after.shape, lambda i, j, k: (0, 0)))
        args.append(after)
    return pl.pallas_call(
        body, name=name, grid=(T // tm, N // tn, nk),
        in_specs=in_specs,
        out_specs=_tile_spec(o_shape, tm, tn, lambda i, j, k: i, lambda i, j, k: j),
        out_shape=o_shape,
        scratch_shapes=[pltpu.VMEM((tm, tn), F32)],
        compiler_params=_cparams(("parallel", "parallel", "arbitrary")),
    )(*args)


def mm_nt_normbwd(a, w, h, g, dres, *, name, tm, tk, after=None):
    T, K = _mat_shape(a)
    nk = K // tk
    D = h.shape[1]
    with_dh = dres is not None

    def body(*refs):
        a_ref, w_ref, h_ref, g_ref = refs[:4]
        if with_dh:
            r_ref = refs[4]
            dh_ref, dhb_ref, dg_ref, acc_ref = refs[-4:]
        else:
            dg_ref, acc_ref = refs[-2:]
        i = pl.program_id(0)
        k = pl.program_id(1)
        part = _dot_nt(a_ref[...].astype(BF16), w_ref[...])

        @pl.when(k == 0)
        def _():
            acc_ref[...] = part

        @pl.when(k > 0)
        def _():
            acc_ref[...] += part

        @pl.when(k == nk - 1)
        def _():
            dn = acc_ref[...]
            x = h_ref[...]
            r = lax.rsqrt(jnp.mean(x * x, axis=-1, keepdims=True) + EPS)
            xr = x * r
            dgp = jnp.sum(dn * xr, axis=0, keepdims=True)

            @pl.when(i == 0)
            def _():
                dg_ref[...] = dgp

            @pl.when(i > 0)
            def _():
                dg_ref[...] += dgp

            if with_dh:
                dyg = dn * g_ref[...]
                dx = r * (dyg - xr * jnp.mean(dyg * xr, axis=-1, keepdims=True))
                out = r_ref[...] + dx
                dh_ref[...] = out
                dhb_ref[...] = out.astype(BF16)

    row = pl.BlockSpec((tm, D), lambda i, k: (i, 0))
    vec = pl.BlockSpec((1, D), lambda i, k: (0, 0))
    in_specs = [_tile_spec(a, tm, tk, lambda i, k: i, lambda i, k: k),
                _tile_spec(w, D, tk, lambda i, k: 0, lambda i, k: k), row, vec]
    args = [a, w, h, g]
    if with_dh:
        in_specs.append(row)
        args.append(dres)
        out_specs = [row, row, vec]
        out_shape = [jax.ShapeDtypeStruct((T, D), F32), jax.ShapeDtypeStruct((T, D), BF16),
                     jax.ShapeDtypeStruct((1, D), F32)]
    else:
        out_specs = vec
        out_shape = jax.ShapeDtypeStruct((1, D), F32)
    if after is not None:
        in_specs.append(pl.BlockSpec(after.shape, lambda i, k: (0, 0)))
        args.append(after)
    return pl.pallas_call(
        body, name=name, grid=(T // tm, nk),
        in_specs=in_specs, out_specs=out_specs, out_shape=out_shape,
        scratch_shapes=[pltpu.VMEM((tm, D), F32)],
        compiler_params=_cparams(("arbitrary", "arbitrary")),
    )(*args)


def mm_tn(a, b, *, name, tt, tko, tn, relu2=False, out_slabs=None):
    T, K = _mat_shape(a)
    N = _mat_shape(b)[1]
    nt = T // tt
    o_shape = _out_struct(K, N, out_slabs, BF16)

    def body(a_ref, b_ref, o_ref, acc_ref):
        t = pl.program_id(2)
        av = a_ref[...]
        if relu2:
            av = jnp.maximum(av, 0.0)
            av = av * av
        part = _dot_tn(av.astype(BF16), b_ref[...].astype(BF16))

        @pl.when(t == 0)
        def _():
            acc_ref[...] = part

        @pl.when(t > 0)
        def _():
            acc_ref[...] += part

        @pl.when(t == nt - 1)
        def _():
            o_ref[...] = acc_ref[...].astype(BF16)

    return pl.pallas_call(
        body, name=name, grid=(K // tko, N // tn, nt),
        in_specs=[_tile_spec(a, tt, tko, lambda kk, j, t: t, lambda kk, j, t: kk),
                  _tile_spec(b, tt, tn, lambda kk, j, t: t, lambda kk, j, t: j)],
        out_specs=_tile_spec(o_shape, tko, tn, lambda kk, j, t: kk, lambda kk, j, t: j),
        out_shape=o_shape,
        scratch_shapes=[pltpu.VMEM((tko, tn), F32)],
        compiler_params=_cparams(("parallel", "parallel", "arbitrary")),
    )(a, b)


def _resident(a):
    nd = a.ndim
    return pl.BlockSpec(a.shape, lambda i: (0,) * nd, pipeline_mode=pl.Buffered(1))


def _row_block(a, tm):
    if a.ndim == 2:
        return pl.BlockSpec((tm, a.shape[1]), lambda i: (i, 0))
    return pl.BlockSpec((a.shape[0], tm, a.shape[2]), lambda i: (0, i, 0))


def _cols(ref, c, width):
    if len(ref.shape) == 2:
        return ref[:, c * width:(c + 1) * width]
    per = ref.shape[2] // width
    if per == 1:
        return ref[c]
    return ref[c // per, :, (c % per) * width:(c % per + 1) * width]


def _set_cols(ref, c, width, val):
    if len(ref.shape) == 2:
        ref[:, c * width:(c + 1) * width] = val
        return
    per = ref.shape[2] // width
    if per == 1:
        ref[c] = val
    else:
        ref[c // per, :, (c % per) * width:(c % per + 1) * width] = val


def _all_cols(ref):
    if len(ref.shape) == 2:
        return ref[...]
    return jnp.concatenate([ref[s] for s in range(ref.shape[0])], axis=1)


def _rms(x):
    return lax.rsqrt(jnp.mean(x * x, axis=-1, keepdims=True) + EPS)


def _row_params():
    return _cparams(("arbitrary",))


def proj_norm(h, g, w, *, name, tm, tn, out_dtype, out_slabs=None):
    T, D = h.shape
    N = _mat_shape(w)[1]
    o_shape = _out_struct(T, N, out_slabs, out_dtype)

    def body(h_ref, g_ref, w_ref, o_ref, n_ref):
        x = h_ref[...]
        n = (x * _rms(x) * g_ref[...]).astype(BF16)
        n_ref[...] = n
        for c in range(N // tn):
            _set_cols(o_ref, c, tn, _dot(n, _cols(w_ref, c, tn)).astype(out_dtype))

    return pl.pallas_call(
        body, name=name, grid=(T // tm,),
        in_specs=[_row_block(h, tm), pl.BlockSpec((1, D), lambda i: (0, 0)), _resident(w)],
        out_specs=[_row_block(o_shape, tm), pl.BlockSpec((tm, D), lambda i: (i, 0))],
        out_shape=[o_shape, jax.ShapeDtypeStruct((T, D), BF16)],
        compiler_params=_row_params(),
    )(h, g, w)


def prenorm(h, g, *, tm):
    T, D = h.shape

    def body(h_ref, g_ref, n_ref):
        x = h_ref[...]
        n_ref[...] = (x * _rms(x) * g_ref[...]).astype(BF16)

    row = pl.BlockSpec((tm, D), lambda i: (i, 0))
    return pl.pallas_call(
        body, name="prenorm", grid=(T // tm,),
        in_specs=[row, pl.BlockSpec((1, D), lambda i: (0, 0))],
        out_specs=row, out_shape=jax.ShapeDtypeStruct((T, D), BF16),
        compiler_params=_row_params(),
    )(h, g)


def proj_plain(a, w, *, name, tm, tn, out_dtype=BF16, out_slabs=None, relu2=False):
    T = a.shape[0]
    N = _mat_shape(w)[1]

    def body(a_ref, w_ref, o_ref):
        av = a_ref[...]
        for c in range(N // tn):
            z = _dot(av, _cols(w_ref, c, tn))
            if relu2:
                z = jnp.maximum(z, 0.0)
                z = z * z
            _set_cols(o_ref, c, tn, z.astype(out_dtype))

    o_shape = _out_struct(T, N, out_slabs, out_dtype)
    return pl.pallas_call(
        body, name=name, grid=(T // tm,),
        in_specs=[_row_block(a, tm), _resident(w)],
        out_specs=_row_block(o_shape, tm), out_shape=o_shape,
        compiler_params=_row_params(),
    )(a, w)


def proj_res_norm(a, w, res, g, w_next=None, *, name, tm, tn):
    T = res.shape[0]
    D = w.shape[1]
    chained = w_next is not None

    def body(*refs):
        a_ref, w_ref, r_ref, g_ref = refs[:4]
        h_ref, n_ref = refs[4 + chained], refs[5 + chained]
        av = _all_cols(a_ref)
        for c in range(D // tn):
            sl = slice(c * tn, (c + 1) * tn)
            h_ref[:, sl] = r_ref[:, sl] + _dot(av, w_ref[:, sl])
        hv = h_ref[...]
        n = (hv * _rms(hv) * g_ref[...]).astype(BF16)
        n_ref[...] = n
        if chained:
            for c in range(D // tn):
                sl = slice(c * tn, (c + 1) * tn)
                refs[-1][:, sl] = _dot(n, refs[4][:, sl]).astype(BF16)

    row = pl.BlockSpec((tm, D), lambda i: (i, 0))
    half = jax.ShapeDtypeStruct((T, D), BF16)
    return pl.pallas_call(
        body, name=name, grid=(T // tm,),
        in_specs=[_row_block(a, tm), _resident(w), row, pl.BlockSpec((1, D), lambda i: (0, 0))]
        + ([_resident(w_next)] if chained else []),
        out_specs=[row, row] + ([row] if chained else []),
        out_shape=[jax.ShapeDtypeStruct((T, D), F32), half] + ([half] if chained else []),
        compiler_params=_row_params(),
    )(*([a, w, res, g] + ([w_next] if chained else [])))


def proj_res_loss(a, w, res, g, target, *, name, tm, tn):
    T = res.shape[0]
    D = w.shape[1]

    def body(a_ref, w_ref, r_ref, g_ref, t_ref, dh_ref, dhb_ref, ls_ref, dg_ref):
        i = pl.program_id(0)
        gv = g_ref[...]
        ls, dg = 0.0, 0.0
        halves = [slice(s * (tm // 2), (s + 1) * (tm // 2)) for s in range(2)]
        for rows in halves:
            av = a_ref[rows, :]
            for c in range(D // tn):
                sl = slice(c * tn, (c + 1) * tn)
                dh_ref[rows, sl] = r_ref[rows, sl] + _dot(av, w_ref[:, sl])
        for rows in halves:
            x = dh_ref[rows, :]
            r = _rms(x)
            xr = x * r
            d = xr * gv - t_ref[rows, :]
            dy = d * (1.0 / D)
            dyg = dy * gv
            dx = r * (dyg - xr * jnp.mean(dyg * xr, axis=-1, keepdims=True))
            dh_ref[rows, :] = dx
            dhb_ref[rows, :] = dx.astype(BF16)
            ls = ls + jnp.sum(d * d, axis=0, keepdims=True)
            dg = dg + jnp.sum(dy * xr, axis=0, keepdims=True)

        @pl.when(i == 0)
        def _():
            ls_ref[...] = ls
            dg_ref[...] = dg

        @pl.when(i > 0)
        def _():
            ls_ref[...] += ls
            dg_ref[...] += dg

    row = pl.BlockSpec((tm, D), lambda i: (i, 0))
    vec = pl.BlockSpec((1, D), lambda i: (0, 0))
    return pl.pallas_call(
        body, name=name, grid=(T // tm,),
        in_specs=[_row_block(a, tm), _resident(w), row, vec, row],
        out_specs=[row, row, vec, vec],
        out_shape=[jax.ShapeDtypeStruct((T, D), F32), jax.ShapeDtypeStruct((T, D), BF16),
                   jax.ShapeDtypeStruct((1, D), F32), jax.ShapeDtypeStruct((1, D), F32)],
        compiler_params=_row_params(),
    )(a, w, res, g, target)


def _anchor_spec(after):
    return pl.BlockSpec(after.shape, lambda i: (0, 0))


def back_plain(a, w, *, name, tm, tn, out_dtype, out_slabs=None, relu2_value=None, after=None):
    T = a.shape[0]
    N = w.shape[0]
    has_z = relu2_value is not None
    o_shape = _out_struct(T, N, out_slabs, out_dtype)

    def body(*refs):
        a_ref, w_ref = refs[0], refs[1]
        o_ref = refs[-1]
        av = a_ref[...]
        for c in range(N // tn):
            out = _dot_nt(av, w_ref[c * tn:(c + 1) * tn, :])
            if has_z:
                out = out * (2.0 * jnp.sqrt(refs[2][:, c * tn:(c + 1) * tn]).astype(F32))
            _set_cols(o_ref, c, tn, out.astype(out_dtype))

    in_specs, args = [_row_block(a, tm), _resident(w)], [a, w]
    if has_z:
        in_specs.append(_row_block(relu2_value, tm))
        args.append(relu2_value)
    if after is not None:
        in_specs.append(_anchor_spec(after))
        args.append(after)
    return pl.pallas_call(
        body, name=name, grid=(T // tm,),
        in_specs=in_specs, out_specs=_row_block(o_shape, tm), out_shape=o_shape,
        compiler_params=_row_params(),
    )(*args)


def back_norm(a, w, h, g, dres, *, name, tm, tk, bf16_copy=True, w_next=None, next_dtype=BF16, next_slabs=None,
              after=None):
    T, K = _mat_shape(a)
    D = h.shape[1]
    with_dh = dres is not None
    chained = w_next is not None
    n_in = 4 + with_dh + chained
    tn = 4 * HEAD_W

    def body(*refs):
        a_ref, w_ref, h_ref, g_ref = refs[:4]
        outs = refs[n_in + (after is not None):]
        i = pl.program_id(0)
        dn = None
        for kc in range(K // tk):
            part = _dot_nt(_cols(a_ref, kc, tk).astype(BF16), _cols(w_ref, kc, tk))
            dn = part if dn is None else dn + part
        x = h_ref[...]
        r = _rms(x)
        xr = x * r
        dgp = jnp.sum(dn * xr, axis=0, keepdims=True)
        dg_ref = outs[-1]

        @pl.when(i == 0)
        def _():
            dg_ref[...] = dgp

        @pl.when(i > 0)
        def _():
            dg_ref[...] += dgp

        if with_dh:
            dyg = dn * g_ref[...]
            out = refs[4][...] + r * (dyg - xr * jnp.mean(dyg * xr, axis=-1, keepdims=True))
            outs[0][...] = out
            outb = out.astype(BF16)
            if bf16_copy:
                outs[1][...] = outb
            if chained:
                wn_ref, nx_ref = refs[5], outs[-2]
                for c in range(wn_ref.shape[0] // tn):
                    _set_cols(nx_ref, c, tn, _dot_nt(outb, wn_ref[c * tn:(c + 1) * tn, :]).astype(next_dtype))

    row = pl.BlockSpec((tm, D), lambda i: (i, 0))
    vec = pl.BlockSpec((1, D), lambda i: (0, 0))
    in_specs, args = [_row_block(a, tm), _resident(w), row, vec], [a, w, h, g]
    out_specs, out_shape = [], []
    if with_dh:
        in_specs.append(row)
        args.append(dres)
        out_specs.append(row)
        out_shape.append(jax.ShapeDtypeStruct((T, D), F32))
        if bf16_copy:
            out_specs.append(row)
            out_shape.append(jax.ShapeDtypeStruct((T, D), BF16))
    if chained:
        in_specs.append(_resident(w_next))
        args.append(w_next)
        nx_shape = _out_struct(T, w_next.shape[0], next_slabs, next_dtype)
        out_specs.append(_row_block(nx_shape, tm))
        out_shape.append(nx_shape)
    out_specs.append(vec)
    out_shape.append(jax.ShapeDtypeStruct((1, D), F32))
    if after is not None:
        in_specs.append(_anchor_spec(after))
        args.append(after)
    outs = pl.pallas_call(
        body, name=name, grid=(T // tm,),
        in_specs=in_specs, out_specs=out_specs, out_shape=out_shape,
        compiler_params=_row_params(),
    )(*args)
    return outs if len(outs) > 1 else outs[0]


def wgrad(a, b, *, name, tt, tn, out_slabs=None):
    T, K = _mat_shape(a)
    N = _mat_shape(b)[1]
    nt = T // tt
    o_shape = _out_struct(K, N, out_slabs, BF16)

    flipped = K > N and out_slabs is None

    def body(a_ref, b_ref, o_ref, acc_ref):
        t = pl.program_id(0)

        @pl.when(t == 0)
        def _():
            acc_ref[...] = jnp.zeros_like(acc_ref)

        if flipped:
            bt = _all_cols(b_ref).astype(BF16).T
            for c in range(K // tn):
                acc_ref[:, c * tn:(c + 1) * tn] += _dot(bt, _cols(a_ref, c, tn).astype(BF16))
        else:
            at = _all_cols(a_ref).astype(BF16).T
            for c in range(N // tn):
                acc_ref[:, c * tn:(c + 1) * tn] += _dot(at, _cols(b_ref, c, tn).astype(BF16))

        @pl.when(t == nt - 1)
        def _():
            if flipped:
                for c in range(K // tn):
                    o_ref[c * tn:(c + 1) * tn, :] = acc_ref[:, c * tn:(c + 1) * tn].T.astype(BF16)
            else:
                for c in range(N // tn):
                    _set_cols(o_ref, c, tn, acc_ref[:, c * tn:(c + 1) * tn].astype(BF16))

    return pl.pallas_call(
        body, name=name, grid=(nt,),
        in_specs=[_row_block(a, tt), _row_block(b, tt)],
        out_specs=_resident(o_shape), out_shape=o_shape,
        scratch_shapes=[pltpu.VMEM((N, K) if flipped else (K, N), F32)],
        compiler_params=_row_params(),
    )(a, b)


def chunk_triangles(tm):
    r = lax.broadcasted_iota(jnp.int32, (tm, tm), 0)
    c = lax.broadcasted_iota(jnp.int32, (tm, tm), 1)
    same = (r // CHUNK) == (c // CHUNK)
    tri = jnp.stack([same & (c <= r), same & (c >= r)]).astype(F32)
    return tri.astype(BF16), tri


def _tri_spec(tm):
    return pl.BlockSpec((2, tm, tm), lambda g, s, i: (0, 0, 0))


def _chunk_row(v, r, nc):
    return jnp.concatenate([jnp.broadcast_to(v[c * CHUNK + r:c * CHUNK + r + 1], (CHUNK, v.shape[1]))
                            for c in range(nc)], axis=0)


def _block_diag(v, nc):
    chunk = lax.broadcasted_iota(jnp.int32, (v.shape[0], 1), 0) // CHUNK
    return jnp.concatenate([jnp.where(chunk == c, v, jnp.zeros_like(v)) for c in range(nc)], axis=1)


def _pool_windows_back(ext_ref, tm):
    n = tm + 32
    ext_ref[1, 8:n] = ext_ref[0, 8:n] + ext_ref[0, 7:n - 1]
    ext_ref[2, 16:n] = ext_ref[1, 16:n] + ext_ref[1, 14:n - 2]
    ext_ref[3, 24:n] = ext_ref[2, 24:n] + ext_ref[2, 20:n - 4]
    s2 = ext_ref[1, 32:n]
    s4 = ext_ref[2, 32:n]
    s8 = ext_ref[3, 32:n]
    s16 = s8 + ext_ref[3, 24:n - 8]
    return s2, s4, s8, s16


def _pool_windows_fwd(ext_ref, tm):
    n = tm + 32
    ext_ref[1, 0:n - 8] = ext_ref[0, 0:n - 8] + ext_ref[0, 1:n - 7]
    ext_ref[2, 0:n - 16] = ext_ref[1, 0:n - 16] + ext_ref[1, 2:n - 14]
    ext_ref[3, 0:n - 24] = ext_ref[2, 0:n - 24] + ext_ref[2, 4:n - 20]
    s2 = ext_ref[1, 0:tm]
    s4 = ext_ref[2, 0:tm]
    s8 = ext_ref[3, 0:tm]
    s16 = s8 + ext_ref[3, 8:tm + 8]
    return s2, s4, s8, s16


def _select_window(g, s2, s4, s8, s16):
    return jnp.where(g == 0, s2, jnp.where(g == 1, s4, jnp.where(g == 2, s8, s16)))


def _pool_count(g, pos):
    width = lax.shift_left(jnp.int32(2), g)
    return jnp.minimum(pos + 1, width).astype(F32)


def _hgrn_gates(zq, zf, th):
    lb = _sigmoid(th[0:1, :] - th[1:2, :])
    sig = _sigmoid(zf)
    f = lb + (1.0 - lb) * sig
    sq = _sigmoid(zq)
    return lb, sig, f, sq


def mixer_fwd(u5, pool_w_bf, scale4, theta4, gn4, tri_bf, tri_f, *, seqs, seq_len, tm):
    T = u5.shape[1]
    tps = seq_len // tm
    nc = tm // CHUNK
    W = HEAD_W

    H = HEADS_PER_STEP
    heads = range(H)

    def body(u_ref, pw_ref, sc_ref, th_ref, gn_ref, tri_ref, msk_ref, y_ref, o_ref, st_ref, halo_ref, ext_ref, s_ref):
        g = pl.program_id(0)
        i = pl.program_id(2)

        @pl.when(i == 0)
        def _():
            halo_ref[...] = jnp.zeros_like(halo_ref)
            s_ref[...] = jnp.zeros_like(s_ref)

        row = lax.broadcasted_iota(jnp.int32, (tm, 1), 0)
        cols = [slice(h * W, (h + 1) * W) for h in heads]

        for h in heads:
            grp = g * H + h
            up = u_ref[0, :, cols[h]]
            ext_ref[h, 0, 0:16] = jnp.zeros((16, W), F32)
            ext_ref[h, 0, 16:32] = halo_ref[h]
            ext_ref[h, 0, 32:32 + tm] = up
            win = _select_window(grp, *_pool_windows_back(ext_ref.at[h], tm))
            p = win * (1.0 / _pool_count(grp, i * tm + row)) - up
            halo_ref[h] = up[tm - POOL_HALO:tm]
            y_ref[0, :, cols[h]] = (_dot(p.astype(BF16), pw_ref[h]) * sc_ref[h]).astype(BF16)

        zq, zf, zi, zg = u_ref[1], u_ref[2], u_ref[3], u_ref[4]
        th = [th_ref[h] for h in heads]
        lb = jnp.concatenate([_sigmoid(t[0:1, :] - t[1:2, :]) for t in th], axis=1)
        f = lb + (1.0 - lb) * _sigmoid(zf)
        kk = 1.0 - f
        q = zq * _sigmoid(zq)
        G = _tri_apply(tri_ref[0], jnp.log(f))
        Gm, Gl = _chunk_row(G, CHUNK // 2 - 1, nc), _chunk_row(G, CHUNK - 1, nc)
        vb = zi.astype(BF16)
        qrb = (q * jnp.exp(G - Gm)).astype(BF16)
        krb = (kk * jnp.exp(Gm - G)).astype(BF16)
        keb = (kk * jnp.exp(Gl - G)).astype(BF16)
        qgb = (q * jnp.exp(G)).astype(BF16)
        mask = msk_ref[0] > 0.5
        a = [jnp.where(mask, _dot_nt(qrb[:, cols[h]], krb[:, cols[h]]), 0.0).astype(BF16) for h in heads]
        d_st = [_dot_tn(vb[:, cols[h]], _block_diag(keb[:, cols[h]], nc)) for h in heads]
        o_intra = [_dot(a[h], vb[:, cols[h]]) for h in heads]
        st_cat = []
        for h in heads:
            st = s_ref[h]
            states = []
            for c in range(nc):
                st_ref[c, h] = st
                states.append(st.astype(BF16))
                st = st * jnp.exp(G[(c + 1) * CHUNK - 1:(c + 1) * CHUNK, cols[h]]) + d_st[h][:, c * W:(c + 1) * W]
            s_ref[h] = st
            st_cat.append(jnp.concatenate(states, axis=1))
        o = [o_intra[h] + _dot_nt(_block_diag(qgb[:, cols[h]], nc), st_cat[h]) for h in heads]
        gate = zg * _sigmoid(zg)
        for h in heads:
            o_ref[:, cols[h]] = o[h]
            r = lax.rsqrt(jnp.mean(o[h] * o[h], axis=-1, keepdims=True) + EPS)
            y_ref[1, :, cols[h]] = (o[h] * r * gn_ref[h] * gate[:, cols[h]]).astype(BF16)

    def rb(s, i):
        return s * tps + i

    def per_head(*shape):
        return pl.BlockSpec((H,) + shape, lambda g, s, i: (g,) + (0,) * len(shape))

    return pl.pallas_call(
        body, name="mixer_fwd", grid=(4 // H, seqs, tps),
        in_specs=[pl.BlockSpec((5, tm, H * W), lambda g, s, i: (0, rb(s, i), g)),
                  per_head(W, W), per_head(1, W), per_head(2, W), per_head(1, W),
                  _tri_spec(tm), _tri_spec(tm)],
        out_specs=[pl.BlockSpec((2, tm, H * W), lambda g, s, i: (0, rb(s, i), g)),
                   pl.BlockSpec((tm, H * W), lambda g, s, i: (rb(s, i), g)),
                   pl.BlockSpec((nc, H, W, W), lambda g, s, i: (rb(s, i), g, 0, 0))],
        out_shape=[jax.ShapeDtypeStruct((2, T, 4 * W), BF16),
                   jax.ShapeDtypeStruct((T, 4 * W), F32),
                   jax.ShapeDtypeStruct((T // CHUNK, 4, W, W), F32)],
        scratch_shapes=[pltpu.VMEM((H, POOL_HALO, W), F32),
                        pltpu.VMEM((H, 4, tm + 32, W), F32),
                        pltpu.VMEM((H, W, W), F32)],
        compiler_params=_cparams(("arbitrary", "arbitrary", "arbitrary")),
    )(u5, pool_w_bf, scale4, theta4, gn4, tri_bf, tri_f)


def mixer_bwd(u5, dy2, o_pre, st_prev, pool_w_bf, scale4, theta4, gn4, tri_bf, tri_f, after, *, seqs, seq_len, tm):
    T = u5.shape[1]
    tps = seq_len // tm
    nc = tm // CHUNK
    W = HEAD_W
    hb = tm // POOL_HALO

    H = HEADS_PER_STEP
    heads = range(H)

    def body(u_ref, uh_ref, dy_ref, o_ref, st_ref, pw_ref, sc_ref, th_ref, gn_ref, tri_ref, msk_ref, _after_ref,
             du_ref, dpw_ref, dsc_ref, dlb_ref, dgn_ref, nxt_ref, ext_ref, ds_ref):
        g = pl.program_id(0)
        s = pl.program_id(1)
        i = pl.program_id(2)
        tile = tps - 1 - i
        first = (s == 0) & (i == 0)

        @pl.when(i == 0)
        def _():
            nxt_ref[...] = jnp.zeros_like(nxt_ref)
            ds_ref[...] = jnp.zeros_like(ds_ref)

        row = lax.broadcasted_iota(jnp.int32, (tm, 1), 0)
        cols = [slice(h * W, (h + 1) * W) for h in heads]

        def accumulate(ref, h, val):
            @pl.when(first)
            def _():
                ref[h] = val

            @pl.when(jnp.logical_not(first))
            def _():
                ref[h] += val

        def per_head(fn):
            return jnp.concatenate([jnp.broadcast_to(fn(cols[h]), (tm, W)) for h in heads], axis=1)

        for h in heads:
            grp = g * H + h
            inv_cnt = 1.0 / _pool_count(grp, tile * tm + row)
            ext = ext_ref.at[h]
            up = u_ref[0, :, cols[h]]
            ext[0, 0:16] = jnp.zeros((16, W), F32)
            ext[0, 16:32] = jnp.where(tile == 0, 0.0, uh_ref[:, cols[h]])
            ext[0, 32:32 + tm] = up
            win = _select_window(grp, *_pool_windows_back(ext, tm))
            pb = (win * inv_cnt - up).astype(BF16)
            dyp = dy_ref[0, :, cols[h]]
            z = _dot(pb, pw_ref[h])
            accumulate(dsc_ref, h, jnp.sum(dyp * z, axis=0, keepdims=True))
            dz = (dyp * sc_ref[h]).astype(BF16)
            accumulate(dpw_ref, h, _dot_tn(pb, dz))
            dp = _dot_nt(dz, pw_ref[h])
            e = dp * inv_cnt
            ext[0, 0:tm] = e
            ext[0, tm:tm + 16] = nxt_ref[h]
            ext[0, tm + 16:tm + 32] = jnp.zeros((16, W), F32)
            lead = _select_window(grp, *_pool_windows_fwd(ext, tm))
            nxt_ref[h] = e[0:POOL_HALO]
            du_ref[0, :, cols[h]] = (lead - dp).astype(BF16)

        zq, zf, zi, zg = u_ref[1], u_ref[2], u_ref[3], u_ref[4]
        lb = jnp.concatenate([_sigmoid(th_ref[h][0:1, :] - th_ref[h][1:2, :]) for h in heads], axis=1)
        gn = jnp.concatenate([gn_ref[h] for h in heads], axis=1)
        sig, sq, sg = _sigmoid(zf), _sigmoid(zq), _sigmoid(zg)
        f = lb + (1.0 - lb) * sig
        kk = 1.0 - f
        q = zq * sq
        G = _tri_apply(tri_ref[0], jnp.log(f))

        dyh = dy_ref[1]
        o = o_ref[...]
        sqr = o * o
        r = per_head(lambda cs: lax.rsqrt(jnp.mean(sqr[:, cs], axis=-1, keepdims=True) + EPS))
        orr = o * r
        du_ref[4] = (dyh * (orr * gn) * (sg * (1.0 + zg * (1.0 - sg)))).astype(BF16)
        don = dyh * (zg * sg)
        dgn = jnp.sum(don * orr, axis=0, keepdims=True)
        dog = don * gn
        dog_orr = dog * orr
        do = r * (dog - orr * per_head(lambda cs: jnp.mean(dog_orr[:, cs], axis=-1, keepdims=True)))

        Gm, Gl = _chunk_row(G, CHUNK // 2 - 1, nc), _chunk_row(G, CHUNK - 1, nc)
        e_q, e_k, e_e, e_g = jnp.exp(G - Gm), jnp.exp(Gm - G), jnp.exp(Gl - G), jnp.exp(G)
        qr, kr, ke, qg = q * e_q, kk * e_k, kk * e_e, q * e_g
        qrb, krb, keb, qgb = qr.astype(BF16), kr.astype(BF16), ke.astype(BF16), qg.astype(BF16)
        vb = zi.astype(BF16)
        dob = do.astype(BF16)
        lower, upper = msk_ref[0] > 0.5, msk_ref[1] > 0.5
        da = [jnp.where(lower, _dot_nt(dob[:, cs], vb[:, cs]), 0.0).astype(BF16) for cs in cols]
        a_t = [jnp.where(upper, _dot_nt(krb[:, cs], qrb[:, cs]), 0.0).astype(BF16) for cs in cols]
        da_t = [jnp.where(upper, _dot_nt(vb[:, cs], dob[:, cs]), 0.0).astype(BF16) for cs in cols]
        u_cat = [_dot_tn(dob[:, cs], _block_diag(qgb[:, cs], nc)) for cs in cols]
        dqr = [_dot(da[h], krb[:, cols[h]]) for h in heads]
        dkr = [_dot(da_t[h], qrb[:, cols[h]]) for h in heads]
        dv = [_dot(a_t[h], dob[:, cols[h]]) for h in heads]
        dsn_rows, dsn_cols, ddecay = [], [], [[None] * H for _ in range(nc)]
        for h in heads:
            dsn = ds_ref[h]
            dsn_b = [None] * nc
            for c in reversed(range(nc)):
                decay = jnp.exp(G[(c + 1) * CHUNK - 1:(c + 1) * CHUNK, cols[h]])
                dsn_b[c] = dsn.astype(BF16)
                ddecay[c][h] = jnp.sum(dsn * st_ref[c, h], axis=0, keepdims=True) * decay
                dsn = u_cat[h][:, c * W:(c + 1) * W] + dsn * decay
            ds_ref[h] = dsn
            dsn_rows.append(jnp.concatenate(dsn_b, axis=0))
            dsn_cols.append(jnp.concatenate(dsn_b, axis=1))
        st_rows = [jnp.concatenate([st_ref[c, h].astype(BF16) for c in range(nc)], axis=0) for h in heads]
        dqg = [_dot(_block_diag(dob[:, cols[h]], nc), st_rows[h]) for h in heads]
        dke = [_dot(_block_diag(vb[:, cols[h]], nc), dsn_rows[h]) for h in heads]
        dv = [dv[h] + _dot_nt(_block_diag(keb[:, cols[h]], nc), dsn_cols[h]) for h in heads]
        dqr, dkr, dqg, dke, dv = (jnp.concatenate(parts, axis=1) for parts in (dqr, dkr, dqg, dke, dv))
        t_mid, t_qg, t_ke = dkr * kr - dqr * qr, dqg * qg, dke * ke
        dq = dqr * e_q + dqg * e_g
        dk = dkr * e_k + dke * e_e
        crow = lax.broadcasted_iota(jnp.int32, (CHUNK, 1), 0)
        ends = []
        for c in range(nc):
            sl = slice(c * CHUNK, (c + 1) * CHUNK)
            dgm = jnp.sum(t_mid[sl], axis=0, keepdims=True)
            dgl = jnp.sum(t_ke[sl], axis=0, keepdims=True) + jnp.concatenate(ddecay[c], axis=1)
            ends.append(jnp.where(crow == CHUNK // 2 - 1, dgm, 0.0) + jnp.where(crow == CHUNK - 1, dgl, 0.0))
        dG = t_qg - t_ke - t_mid + jnp.concatenate(ends, axis=0)
        dlogf = _tri_apply(tri_ref[1], dG)
        df = dlogf / f - dk
        du_ref[1] = (dq * (sq * (1.0 + zq * (1.0 - sq)))).astype(BF16)
        du_ref[2] = (df * (1.0 - lb) * (sig * (1.0 - sig))).astype(BF16)
        du_ref[3] = dv.astype(BF16)
        dlb = jnp.sum(df * (1.0 - sig), axis=0, keepdims=True) * (lb * (1.0 - lb))
        for h in heads:
            accumulate(dgn_ref, h, dgn[:, cols[h]])
            accumulate(dlb_ref, h, dlb[:, cols[h]])

    def rb(s, i):
        return s * tps + (tps - 1 - i)

    def per_head_spec(*shape):
        return pl.BlockSpec((H,) + shape, lambda g, s, i: (g,) + (0,) * len(shape))

    vec, mat = per_head_spec(1, W), per_head_spec(W, W)
    return pl.pallas_call(
        body, name="mixer_bwd", grid=(4 // H, seqs, tps),
        in_specs=[pl.BlockSpec((5, tm, H * W), lambda g, s, i: (0, rb(s, i), g)),
                  pl.BlockSpec((None, POOL_HALO, H * W), lambda g, s, i: (0, jnp.maximum(rb(s, i) * hb - 1, 0), g)),
                  pl.BlockSpec((2, tm, H * W), lambda g, s, i: (0, rb(s, i), g)),
                  pl.BlockSpec((tm, H * W), lambda g, s, i: (rb(s, i), g)),
                  pl.BlockSpec((nc, H, W, W), lambda g, s, i: (rb(s, i), g, 0, 0)),
                  mat, vec, per_head_spec(2, W), vec, _tri_spec(tm), _tri_spec(tm),
                  pl.BlockSpec(after.shape, lambda g, s, i: (0, 0))],
        out_specs=[pl.BlockSpec((5, tm, H * W), lambda g, s, i: (0, rb(s, i), g)), mat, vec, vec, vec],
        out_shape=[jax.ShapeDtypeStruct((5, T, 4 * W), BF16),
                   jax.ShapeDtypeStruct((4, W, W), F32),
                   jax.ShapeDtypeStruct((4, 1, W), F32),
                   jax.ShapeDtypeStruct((4, 1, W), F32),
                   jax.ShapeDtypeStruct((4, 1, W), F32)],
        scratch_shapes=[pltpu.VMEM((H, POOL_HALO, W), F32),
                        pltpu.VMEM((H, 4, tm + 32, W), F32),
                        pltpu.VMEM((H, W, W), F32)],
        compiler_params=_cparams(("arbitrary", "arbitrary", "arbitrary")),
    )(u5, u5, dy2, o_pre, st_prev, pool_w_bf, scale4, theta4, gn4, tri_bf, tri_f, after)


def _attn_probs(q, k, hd):
    s = _dot_nt(q, k) * (1.0 / (hd ** 0.5))
    e = jnp.exp(s - jnp.max(s, axis=-1, keepdims=True))
    return e * (1.0 / jnp.sum(e, axis=-1, keepdims=True))


def attn_fwd(q, kv3, *, seqs, seq_len, n_mem, tm):
    T, D = q.shape
    hd = D // XATTN_HEADS
    tps = seq_len // tm

    cols = [slice(h * hd, (h + 1) * hd) for h in range(XATTN_HEADS)]

    def body(q_ref, kv_ref, o_ref):
        p = [_attn_probs(q_ref[:, cs], kv_ref[0, :, cs], hd) for cs in cols]
        for h, cs in enumerate(cols):
            o_ref[:, cs] = _dot(p[h].astype(BF16), kv_ref[1, :, cs]).astype(BF16)

    return pl.pallas_call(
        body, name="attn_fwd", grid=(seqs, tps),
        in_specs=[pl.BlockSpec((tm, D), lambda b, i: (b * tps + i, 0)),
                  pl.BlockSpec((2, n_mem, D), lambda b, i: (0, b, 0))],
        out_specs=pl.BlockSpec((tm, D), lambda b, i: (b * tps + i, 0)),
        out_shape=jax.ShapeDtypeStruct((T, D), BF16),
        compiler_params=_cparams(("parallel", "arbitrary")),
    )(q, kv3)


def attn_bwd(q, kv3, do, *, seqs, seq_len, n_mem, tm):
    T, D = q.shape
    hd = D // XATTN_HEADS
    tps = seq_len // tm

    cols = [slice(h * hd, (h + 1) * hd) for h in range(XATTN_HEADS)]

    def body(q_ref, kv_ref, do_ref, dq_ref, dkv_ref):
        i = pl.program_id(1)

        @pl.when(i == 0)
        def _():
            dkv_ref[...] = jnp.zeros_like(dkv_ref)

        p = [_attn_probs(q_ref[:, cs], kv_ref[0, :, cs], hd) for cs in cols]
        dp = [_dot_nt(do_ref[:, cs], kv_ref[1, :, cs]) for cs in cols]
        ds = [(p[h] * (dp[h] - jnp.sum(dp[h] * p[h], axis=-1, keepdims=True)) * (1.0 / (hd ** 0.5))).astype(BF16)
              for h in range(XATTN_HEADS)]
        for h, cs in enumerate(cols):
            dq_ref[:, cs] = _dot(ds[h], kv_ref[0, :, cs]).astype(BF16)
            dkv_ref[0, :, cs] += _dot_tn(ds[h], q_ref[:, cs])
            dkv_ref[1, :, cs] += _dot_tn(p[h].astype(BF16), do_ref[:, cs])

    qspec = pl.BlockSpec((tm, D), lambda b, i: (b * tps + i, 0))
    kvspec = pl.BlockSpec((2, n_mem, D), lambda b, i: (0, b, 0))
    return pl.pallas_call(
        body, name="attn_bwd", grid=(seqs, tps),
        in_specs=[qspec, kvspec, qspec],
        out_specs=[qspec, kvspec],
        out_shape=[jax.ShapeDtypeStruct((T, D), BF16), jax.ShapeDtypeStruct((2, seqs * n_mem, D), F32)],
        compiler_params=_cparams(("parallel", "arbitrary")),
    )(q, kv3, do)


def final_loss(h, g, target, *, tm):
    T, D = h.shape

    def body(h_ref, g_ref, t_ref, dh_ref, dhb_ref, ls_ref, dg_ref):
        i = pl.program_id(0)
        x = h_ref[...]
        gv = g_ref[...]
        r = lax.rsqrt(jnp.mean(x * x, axis=-1, keepdims=True) + EPS)
        xr = x * r
        d = xr * gv - t_ref[...]
        dy = d * (1.0 / D)
        dyg = dy * gv
        dx = r * (dyg - xr * jnp.mean(dyg * xr, axis=-1, keepdims=True))
        dh_ref[...] = dx
        dhb_ref[...] = dx.astype(BF16)
        ls = jnp.sum(d * d, axis=0, keepdims=True)
        dg = jnp.sum(dy * xr, axis=0, keepdims=True)

        @pl.when(i == 0)
        def _():
            ls_ref[...] = ls
            dg_ref[...] = dg

        @pl.when(i > 0)
        def _():
            ls_ref[...] += ls
            dg_ref[...] += dg

    row = pl.BlockSpec((tm, D), lambda i: (i, 0))
    vec = pl.BlockSpec((1, D), lambda i: (0, 0))
    return pl.pallas_call(
        body, name="final_loss", grid=(T // tm,),
        in_specs=[row, vec, row], out_specs=[row, row, vec, vec],
        out_shape=[jax.ShapeDtypeStruct((T, D), F32), jax.ShapeDtypeStruct((T, D), BF16),
                   jax.ShapeDtypeStruct((1, D), F32), jax.ShapeDtypeStruct((1, D), F32)],
        compiler_params=_cparams(("arbitrary",)),
    )(h, g, target)


def _my_place():
    return lax.axis_index("x"), lax.axis_index("y"), lax.axis_index("c")


def _slot_of(px, py, pc):
    return 4 * px + 2 * py + pc


def _peer(k, x, y, c):
    return (1 - x if (k >> 2) & 1 else x, 1 - y if (k >> 1) & 1 else y, 1 - c if k & 1 else c)


def _split_copies(src_refs, land_refs, send_sems, recv_sems, scatter):
    x, y, c = _my_place()
    mine = _slot_of(x, y, c)
    copies = []
    for a, (src, land) in enumerate(zip(src_refs, land_refs)):
        for k in range(1, N_DEV):
            peer = _peer(k, x, y, c)
            copies.append(pltpu.make_async_remote_copy(
                src_ref=src.at[_slot_of(*peer)] if scatter else src, dst_ref=land.at[mine],
                send_sem=send_sems.at[a * N_PEERS + k - 1], recv_sem=recv_sems.at[a * N_PEERS + k - 1],
                device_id=peer, device_id_type=MESH))
    return copies


def split_start(groups, *, name, scatter):
    sizes = [len(srcs) for srcs, _ in groups]
    n_arr = sum(sizes)
    flat = [a for srcs, lands in groups for a in list(srcs) + list(lands)]

    def body(*refs):
        ins = refs[:2 * n_arr]
        sems = refs[4 * n_arr:4 * n_arr + 2 * len(groups)]
        token = refs[-1]
        at = 0
        for gi, n in enumerate(sizes):
            for cp in _split_copies(ins[at:at + n], ins[at + n:at + 2 * n], sems[2 * gi], sems[2 * gi + 1], scatter):
                cp.start()
            at += 2 * n
        token[...] = jnp.zeros_like(token)

    sem_shapes = []
    for n in sizes:
        sem_shapes += [pltpu.SemaphoreType.DMA((n * N_PEERS,))] * 2
    outs = pl.pallas_call(
        body, name=name,
        out_shape=tuple(pltpu.HBM(a.shape, a.dtype) for a in flat) + tuple(sem_shapes)
        + (jax.ShapeDtypeStruct((8, 128), F32),),
        in_specs=(HBM,) * len(flat),
        out_specs=(HBM,) * len(flat) + (SEM,) * len(sem_shapes) + (pl.BlockSpec(memory_space=pltpu.VMEM),),
        input_output_aliases={i: i for i in range(len(flat))},
        compiler_params=pltpu.CompilerParams(has_side_effects=pltpu.SideEffectType.DATAFLOW_SIDE_EFFECTING),
    )(*[pltpu.with_memory_space_constraint(a, pltpu.HBM) for a in flat])
    thru, sems, token = outs[:len(flat)], outs[len(flat):-1], outs[-1]
    started, at = [], 0
    for gi, n in enumerate(sizes):
        started.append((sems[2 * gi], sems[2 * gi + 1], thru[at:at + n], thru[at + n:at + 2 * n]))
        at += 2 * n
    return started, token


def split_wait(started, after, *, name, scatter):
    sizes = [len(g[2]) for g in started]
    n_arr = sum(sizes)
    flat = [a for g in started for a in list(g[2]) + list(g[3])]
    sems = [s for g in started for s in g[:2]]

    def body(*refs):
        ins = refs[:2 * n_arr]
        sem_refs = refs[2 * n_arr:2 * n_arr + len(sems)]
        at = 0
        for gi, n in enumerate(sizes):
            for cp in _split_copies(ins[at:at + n], ins[at + n:at + 2 * n], sem_refs[2 * gi], sem_refs[2 * gi + 1], scatter):
                cp.wait_send()
                cp.wait_recv()
            at += 2 * n

    outs = pl.pallas_call(
        body, name=name,
        out_shape=tuple(pltpu.HBM(a.shape, a.dtype) for a in flat),
        in_specs=(HBM,) * len(flat) + (SEM,) * len(sems) + (pl.BlockSpec(memory_space=pl.ANY),),
        out_specs=(HBM,) * len(flat),
        input_output_aliases={i: i for i in range(len(flat))},
        compiler_params=pltpu.CompilerParams(has_side_effects=pltpu.SideEffectType.DATAFLOW_SIDE_EFFECTING),
    )(*flat, *sems, after)
    done, at = [], 0
    for n in sizes:
        done.append((outs[at:at + n], outs[at + n:at + 2 * n]))
        at += 2 * n
    return done


SIBLING = 1
CHIP_PEERS = (2, 4, 6)
_SIDE_EFFECTS = pltpu.CompilerParams(has_side_effects=pltpu.SideEffectType.DATAFLOW_SIDE_EFFECTING)


def _chip_level_copies(src, land, send_sems, recv_sems):
    x, y, c = _my_place()
    return [pltpu.make_async_remote_copy(
        src_ref=src, dst_ref=land.at[_slot_of(x, y, c)], send_sem=send_sems.at[j], recv_sem=recv_sems.at[j],
        device_id=_peer(k, x, y, c), device_id_type=MESH) for j, k in enumerate((SIBLING,) + CHIP_PEERS)]


def _pass_on_copies(land, send_sems, recv_sems, receiving):
    x, y, c = _my_place()
    copies = []
    for j, k in enumerate(CHIP_PEERS):
        slot = _slot_of(*_peer(k ^ SIBLING if receiving else k, x, y, c))
        copies.append(pltpu.make_async_remote_copy(
            src_ref=land.at[slot], dst_ref=land.at[slot], send_sem=send_sems.at[j], recv_sem=recv_sems.at[j],
            device_id=_peer(SIBLING, x, y, c), device_id_type=MESH))
    return copies


def gather2_start(src, land, *, name):
    def body(src_ref, land_ref, src_out, land_out, send_sems, recv_sems, token):
        for cp in _chip_level_copies(src_ref, land_ref, send_sems, recv_sems):
            cp.start()
        token[...] = jnp.zeros_like(token)

    n = 1 + len(CHIP_PEERS)
    src_t, land_t, send_sems, recv_sems, token = pl.pallas_call(
        body, name=name,
        out_shape=(pltpu.HBM(src.shape, src.dtype), pltpu.HBM(land.shape, land.dtype),
                   pltpu.SemaphoreType.DMA((n,)), pltpu.SemaphoreType.DMA((n,)), jax.ShapeDtypeStruct((8, 128), F32)),
        in_specs=(HBM, HBM), out_specs=(HBM, HBM, SEM, SEM, pl.BlockSpec(memory_space=pltpu.VMEM)),
        input_output_aliases={0: 0, 1: 1}, compiler_params=_SIDE_EFFECTS,
    )(pltpu.with_memory_space_constraint(src, pltpu.HBM), pltpu.with_memory_space_constraint(land, pltpu.HBM))
    return (src_t, land_t, send_sems, recv_sems), token


def gather2_pass_on(started, after, *, name):
    src, land, send_a, recv_a = started

    def body(src_ref, land_ref, send_a_ref, recv_a_ref, after_ref, land_out, send_b, recv_b):
        for cp in _chip_level_copies(src_ref, land_ref, send_a_ref, recv_a_ref):
            cp.wait_send()
            cp.wait_recv()
        for cp in _pass_on_copies(land_ref, send_b, recv_b, False):
            cp.start()

    n = len(CHIP_PEERS)
    land_t, send_b, recv_b = pl.pallas_call(
        body, name=name,
        out_shape=(pltpu.HBM(land.shape, land.dtype), pltpu.SemaphoreType.DMA((n,)), pltpu.SemaphoreType.DMA((n,))),
        in_specs=(HBM, HBM, SEM, SEM, pl.BlockSpec(memory_space=pl.ANY)), out_specs=(HBM, SEM, SEM),
        input_output_aliases={1: 0}, compiler_params=_SIDE_EFFECTS,
    )(src, land, send_a, recv_a, after)
    return land_t, send_b, recv_b


def gather2_wait(passed, *, name):
    land, send_b, recv_b = passed

    def body(land_ref, send_ref, recv_ref, land_out):
        for cp in _pass_on_copies(land_ref, send_ref, recv_ref, False):
            cp.wait_send()
        for cp in _pass_on_copies(land_ref, send_ref, recv_ref, True):
            cp.wait_recv()

    return pl.pallas_call(
        body, name=name, out_shape=pltpu.HBM(land.shape, land.dtype),
        in_specs=(HBM, SEM, SEM), out_specs=HBM,
        input_output_aliases={0: 0}, compiler_params=_SIDE_EFFECTS,
    )(land, send_b, recv_b)


def allgather_small(bufs):
    n = len(bufs)

    def body(*refs):
        srcs, outs = refs[:n], refs[n:2 * n]
        send_sems, recv_sems, local_sems = refs[2 * n:]
        x, y, c = _my_place()
        mine = _slot_of(x, y, c)
        local = [pltpu.make_async_copy(s, o.at[mine], local_sems.at[a]) for a, (s, o) in enumerate(zip(srcs, outs))]
        for cp in local:
            cp.start()
        copies = _split_copies(srcs, outs, send_sems, recv_sems, False)
        for cp in copies:
            cp.start()
        for cp in copies:
            cp.wait()
        for cp in local:
            cp.wait()

    return pl.pallas_call(
        body, name="allgather_small",
        out_shape=[jax.ShapeDtypeStruct((N_DEV,) + b.shape, b.dtype) for b in bufs],
        in_specs=[HBM] * n, out_specs=[HBM] * n,
        scratch_shapes=[pltpu.SemaphoreType.DMA((n * N_PEERS,)), pltpu.SemaphoreType.DMA((n * N_PEERS,)),
                        pltpu.SemaphoreType.DMA((n,))],
    )(*bufs)


def _adamw_math(g, w, m, v):
    c1 = 1.0 - ADAM_B1 ** ADAM_STEP
    c2 = 1.0 - ADAM_B2 ** ADAM_STEP
    nm = ADAM_B1 * m + (1.0 - ADAM_B1) * g
    nv = ADAM_B2 * v + (1.0 - ADAM_B2) * (g * g)
    delta = -ADAM_LR * ((nm / c1) / (jnp.sqrt(nv / c2) + ADAM_EPS) + ADAM_WD * w)
    return delta, nm, nv


def adamw_sharded(me, own, recv, w, m, v, *, name, tr):
    R, C = w.shape

    def body(me_ref, *refs):
        parts = refs[:N_DEV]
        w_ref, m_ref, v_ref, g_ref, d_ref, nm_ref, nv_ref = refs[N_DEV:]
        g = parts[0][...].astype(F32)
        for p in parts[1:]:
            g = g + p[...].astype(F32)
        g_ref[...] = g
        d_ref[...], nm_ref[...], nv_ref[...] = _adamw_math(g, w_ref[...], m_ref[...], v_ref[...])

    def slab(k):
        return pl.BlockSpec((None, tr, C), lambda i, me_ref: (me_ref[0] ^ k, i, 0))

    blk = pl.BlockSpec((tr, C), lambda i, me_ref: (i, 0))
    out = jax.ShapeDtypeStruct((R, C), F32)
    return pl.pallas_call(
        body, name=name,
        grid_spec=pltpu.PrefetchScalarGridSpec(
            num_scalar_prefetch=1, grid=(R // tr,),
            in_specs=[slab(k) for k in range(N_DEV)] + [blk, blk, blk],
            out_specs=[blk, blk, blk, blk]),
        out_shape=[out, out, out, out],
        compiler_params=_cparams(("parallel",)),
    )(me, own, *([recv] * N_PEERS), w, m, v)


def adamw_replicated(parts, ws, ms, vs, rows):
    n_buf, n_par = len(parts), len(ws)

    def body(*refs):
        p_refs = refs[:n_buf]
        w_refs = refs[n_buf:n_buf + n_par]
        m_refs = refs[n_buf + n_par:n_buf + 2 * n_par]
        v_refs = refs[n_buf + 2 * n_par:n_buf + 3 * n_par]
        outs = refs[n_buf + 3 * n_par:]
        sums = []
        for p in p_refs:
            g = p[0]
            for s in range(1, N_DEV):
                g = g + p[s]
            sums.append(g)
        for j, (b, r0, nr) in enumerate(rows):
            g = sums[b][r0:r0 + nr]
            delta, nm, nv = _adamw_math(g, w_refs[j][...], m_refs[j][...], v_refs[j][...])
            outs[j][...] = g
            outs[n_par + j][...] = delta
            outs[2 * n_par + j][...] = nm
            outs[3 * n_par + j][...] = nv

    shapes = [jax.ShapeDtypeStruct(w.shape, F32) for w in ws]
    outs = pl.pallas_call(
        body, name="adamw_replicated", out_shape=shapes * 4,
        compiler_params=pltpu.CompilerParams(vmem_limit_bytes=V7X_VMEM_LIMIT),
    )(*parts, *ws, *ms, *vs)
    return outs[:n_par], outs[n_par:2 * n_par], outs[2 * n_par:3 * n_par], outs[3 * n_par:]


BIG = ("w_in", "w_out", "xw_q", "xw_kv", "xw_o", "w_up", "w_down")
COL_SHARDED = ("w_in", "xw_kv", "w_up")
WEIGHTS = ("norm_mix", "w_in", "pool_w", "pool_scale", "lb_theta", "hgrn_norm", "w_out", "norm_xq",
           "norm_mem", "xw_q", "xw_kv", "xw_o", "norm_mlp", "w_up", "w_down", "norm_final")
SMALL = (("pool_w", (4 * HEAD_W, HEAD_W), 0, 0),
         ("norm_mix", (1, 1024), 1, 0), ("norm_xq", (1, 1024), 1, 1), ("norm_mem", (1, 1024), 1, 2),
         ("norm_mlp", (1, 1024), 1, 3), ("norm_final", (1, 1024), 1, 4),
         ("pool_scale", (1, 512), 2, 0), ("hgrn_norm", (1, 512), 2, 1), ("lb_theta", (2, 512), 2, 2))


def _pad_rows(a, rows):
    return jnp.concatenate([a, jnp.zeros((rows - a.shape[0], a.shape[1]), a.dtype)], axis=0)


def kernel(x, mem, norm_mix, w_in, pool_w, pool_scale, lb_theta, hgrn_norm, w_out, norm_xq, norm_mem, xw_q, xw_kv, xw_o, norm_mlp, w_up, w_down, norm_final, loss_target, m_norm_mix, m_w_in, m_pool_w, m_pool_scale, m_lb_theta, m_hgrn_norm, m_w_out, m_norm_xq, m_norm_mem, m_xw_q, m_xw_kv, m_xw_o, m_norm_mlp, m_w_up, m_w_down, m_norm_final, v_norm_mix, v_w_in, v_pool_w, v_pool_scale, v_lb_theta, v_hgrn_norm, v_w_out, v_norm_xq, v_norm_mem, v_xw_q, v_xw_kv, v_xw_o, v_norm_mlp, v_w_up, v_w_down, v_norm_final):
    w = dict(norm_mix=norm_mix, w_in=w_in, pool_w=pool_w, pool_scale=pool_scale, lb_theta=lb_theta,
             hgrn_norm=hgrn_norm, w_out=w_out, norm_xq=norm_xq, norm_mem=norm_mem, xw_q=xw_q, xw_kv=xw_kv,
             xw_o=xw_o, norm_mlp=norm_mlp, w_up=w_up, w_down=w_down, norm_final=norm_final)
    mom = dict(norm_mix=m_norm_mix, w_in=m_w_in, pool_w=m_pool_w, pool_scale=m_pool_scale, lb_theta=m_lb_theta,
               hgrn_norm=m_hgrn_norm, w_out=m_w_out, norm_xq=m_norm_xq, norm_mem=m_norm_mem, xw_q=m_xw_q,
               xw_kv=m_xw_kv, xw_o=m_xw_o, norm_mlp=m_norm_mlp, w_up=m_w_up, w_down=m_w_down,
               norm_final=m_norm_final)
    var = dict(norm_mix=v_norm_mix, w_in=v_w_in, pool_w=v_pool_w, pool_scale=v_pool_scale, lb_theta=v_lb_theta,
               hgrn_norm=v_hgrn_norm, w_out=v_w_out, norm_xq=v_norm_xq, norm_mem=v_norm_mem, xw_q=v_xw_q,
               xw_kv=v_xw_kv, xw_o=v_xw_o, norm_mlp=v_norm_mlp, w_up=v_w_up, w_down=v_w_down,
               norm_final=v_norm_final)

    seqs, seq_len, D = x.shape
    n_mem = mem.shape[1]
    T = seqs * seq_len
    W = HEAD_W
    x2 = x.reshape(T, D)
    mem2 = mem.reshape(seqs * n_mem, D)
    tgt2 = loss_target.reshape(T, D)
    tm_big = min(1024, T)
    tm_mid = min(512, T)
    tm_sq = min(1024, T)
    tm_mix = min(256, seq_len)
    tm_att = min(1024, seq_len)
    tkv = min(512, seqs * n_mem)
    px, py, pc = _my_place()
    me = _slot_of(px, py, pc).astype(jnp.int32)
    me1 = me.reshape(1)

    shard_bf = {n: w[n][0].astype(BF16) for n in BIG}

    def landing(n):
        zone = lax.empty((N_DEV,) + shard_bf[n].shape, BF16)
        return lax.dynamic_update_slice(zone, shard_bf[n][None], (me, 0, 0))

    w_in_started, tok = gather2_start(shard_bf["w_in"], landing("w_in"), name="w_in_gather_start")
    shard_bf["w_out"] = shard_bf["w_out"] + tok[0, 0].astype(BF16)
    ag_groups = (("w_out", "xw_q", "xw_kv", "xw_o"), ("w_up",), ("w_down",))
    ag_started, _ = split_start([([shard_bf[n] for n in grp], [landing(n) for n in grp]) for grp in ag_groups],
                                name="weights_gather_start", scatter=False)

    pool_w_bf = pool_w[0].astype(BF16)
    scale4 = pool_scale.reshape(4, 1, W)
    gn4 = hgrn_norm.reshape(4, 1, W)
    theta4 = lb_theta.reshape(2, 4, W).transpose(1, 0, 2)
    g_final = norm_final.reshape(1, D)

    n1 = prenorm(x2, norm_mix, tm=tm_sq)
    wi3 = gather2_wait(gather2_pass_on(w_in_started, n1, name="w_in_gather_pass_on"), name="w_in_gather_wait")
    full_w_in = wi3.transpose(1, 0, 2).reshape(D, -1)
    u5 = proj_plain(n1, full_w_in, name="in_proj", tm=tm_mid, tn=4 * W, out_dtype=F32, out_slabs=5)
    tri_bf, tri_f = chunk_triangles(tm_mix)
    y2, o_pre, st_prev = mixer_fwd(u5, pool_w_bf, scale4, theta4, gn4, tri_bf, tri_f, seqs=seqs, seq_len=seq_len,
                                   tm=tm_mix)
    (_, (wo3, wq3, wkv3, wao3)), = split_wait(ag_started[0:1], y2, name="weights_gather_wait_attn", scatter=False)
    full_w_out, full_xw_q, full_xw_o = wo3.reshape(D, D), wq3.reshape(D, D), wao3.reshape(D, D)
    tn = 4 * W
    h1, n2, q = proj_res_norm(y2, full_w_out, x2, norm_xq, full_xw_q, name="out_q_proj", tm=tm_sq, tn=tn)
    kv3, memn = proj_norm(mem2, norm_mem, wkv3, name="kv_proj", tm=tkv, tn=wkv3.shape[2], out_dtype=BF16,
                          out_slabs=2)
    o_att = attn_fwd(q, kv3, seqs=seqs, seq_len=seq_len, n_mem=n_mem, tm=tm_att)
    h2, n3 = proj_res_norm(o_att, full_xw_o, h1, norm_mlp, name="attn_out_proj", tm=tm_sq, tn=tn)
    (_, (wup3,)), = split_wait(ag_started[1:2], h2, name="weights_gather_wait_up", scatter=False)
    tn_up = wup3.shape[2]
    aa = proj_plain(n3, wup3, name="up_proj", tm=tm_mid, tn=tn_up, relu2=True)
    (_, (wdn3,)), = split_wait(ag_started[2:3], aa, name="weights_gather_wait_down", scatter=False)
    full_w_down = wdn3.reshape(-1, D)
    dh3, dh3b, sq_err, dg_final = proj_res_loss(aa, full_w_down, h2, g_final, tgt2, name="down_proj_loss",
                                                tm=tm_mid, tn=tn)

    def send(parts, name):
        srcs = [p.reshape((N_DEV, -1, p.shape[-1])) for p in parts]
        lands = [lax.empty(s.shape, BF16) for s in srcs]
        started, token = split_start([(srcs, lands)], name=name, scatter=True)
        return started[0], token

    gw_down = wgrad(aa, dh3b, name="down_proj_wgrad", tt=tm_mid, tn=tn)
    sent_down, tok = send([gw_down], "grads_send_down")
    dap = back_plain(dh3b, full_w_down, name="down_proj_bwd", tm=tm_mid, tn=tn, out_dtype=BF16, relu2_value=aa,
                     after=tok)
    gw_up = wgrad(n3, dap, name="up_proj_wgrad", tt=tm_mid, tn=tn_up, out_slabs=N_DEV)
    sent_up, tok = send([gw_up], "grads_send_up")
    dh2, dh2b, do_att, dg_mlp = back_norm(dap, wup3, h2, norm_mlp, dh3, name="up_proj_bwd", tm=tm_mid, tk=tn_up,
                                          w_next=full_xw_o, after=tok)
    gxw_o = wgrad(o_att, dh2b, name="attn_out_proj_wgrad", tt=tm_sq, tn=tn)
    dq, dkv3 = attn_bwd(q, kv3, do_att, seqs=seqs, seq_len=seq_len, n_mem=n_mem, tm=tm_att)
    gxw_q = wgrad(n2, dq, name="q_proj_wgrad", tt=tm_sq, tn=tn)
    gxw_kv = wgrad(memn, dkv3, name="kv_proj_wgrad", tt=tkv, tn=wkv3.shape[2], out_slabs=N_DEV)
    sent_attn, tok = send([gxw_o, gxw_q, gxw_kv], "grads_send_attn")
    dg_mem = back_norm(dkv3, wkv3, mem2, norm_mem, None, name="kv_proj_bwd", tm=tkv, tk=wkv3.shape[2])
    dh1, dh1b, dy2, dg_xq = back_norm(dq, full_xw_q, h1, norm_xq, dh2, name="q_proj_bwd", tm=tm_mid, tk=D,
                                      w_next=full_w_out, next_dtype=F32, next_slabs=2, after=tok)
    gw_out = wgrad(y2, dh1b, name="out_proj_wgrad", tt=tm_sq, tn=tn)
    sent_out, tok = send([gw_out], "grads_send_out")
    du5, dpw, dsc, dlb, dgn = mixer_bwd(u5, dy2, o_pre, st_prev, pool_w_bf, scale4, theta4, gn4, tri_bf, tri_f, tok,
                                        seqs=seqs, seq_len=seq_len, tm=tm_mix)
    gw_in = wgrad(n1, du5, name="in_proj_wgrad", tt=tm_mid, tn=tn)
    gw_in_slots = gw_in.reshape(D, N_DEV, -1).transpose(1, 0, 2)
    sent_in, tok = send([gw_in_slots], "grads_send_in")
    dx, dg_mix = back_norm(du5, full_w_in, x2, norm_mix, dh1, name="in_proj_bwd", tm=tm_mid, tk=tn, bf16_copy=False,
                           after=tok)

    dlb_row = dlb.reshape(1, 4 * W)
    buf_vec = _pad_rows(jnp.concatenate([dg_mix, dg_xq, dg_mem, dg_mlp, dg_final, sq_err], axis=0), 8)
    buf_half = _pad_rows(jnp.concatenate([dsc.reshape(1, 4 * W), dgn.reshape(1, 4 * W), dlb_row, -dlb_row], axis=0), 8)
    small_src = [dpw.reshape(4 * W, W), buf_vec, buf_half]
    small_land = [lax.dynamic_update_slice(lax.empty((N_DEV,) + b.shape, F32), b[None], (me, 0, 0))
                  for b in small_src]
    small_started, tok = split_start([(small_src, small_land)], name="small_grads_start", scatter=False)

    done = split_wait([sent_down, sent_up, sent_attn, sent_out, sent_in], tok, name="grads_wait", scatter=True)
    slots = dict(w_down=(0, 0), w_up=(1, 0), xw_o=(2, 0), xw_q=(2, 1), xw_kv=(2, 2), w_out=(3, 0), w_in=(4, 0))
    own = {n: done[gi][0][ai] for n, (gi, ai) in slots.items()}
    got = {n: done[gi][1][ai] for n, (gi, ai) in slots.items()}
    res = {}
    for n in BIG:
        shp = w[n].shape
        r = adamw_sharded(me1, own[n], got[n], w[n][0], mom[n][0], var[n][0], name="adamw_" + n,
                          tr=min(256, shp[1]))
        for kind, a in zip("gdmv", r):
            res[kind, n] = a.reshape(shp)
    (_, small_parts), = split_wait(small_started, res["g", BIG[-1]], name="small_grads_wait", scatter=False)
    loss = 0.5 * jnp.sum(small_parts[1][:, 5, :]) / D
    r = adamw_replicated(small_parts, [w[n].reshape(v2) for n, v2, _, _ in SMALL],
                         [mom[n].reshape(v2) for n, v2, _, _ in SMALL],
                         [var[n].reshape(v2) for n, v2, _, _ in SMALL],
                         [(b, r0, v2[0]) for _, v2, b, r0 in SMALL])
    for kind, arrs in zip("gdmv", r):
        for (n, _, _, _), a in zip(SMALL, arrs):
            res[kind, n] = a.reshape(w[n].shape)

    out = [loss, dx.reshape(x.shape)]
    for kind in "gdmv":
        out += [res[kind, n] for n in WEIGHTS]
    return tuple(out)
```

```python
import jax
import jax.numpy as jnp
from jax import lax
from jax.experimental import pallas as pl
from jax.experimental.pallas import tpu as pltpu

F32 = jnp.float32
BF16 = jnp.bfloat16
EPS = 1e-6
CHUNK = 64
POOL_HALO = 16
HEAD_W = 128
HEADS_PER_STEP = 4
XATTN_HEADS = 4
N_DEV = 8
N_PEERS = N_DEV - 1
ADAM_LR = 0.001
ADAM_B1 = 0.9
ADAM_B2 = 0.999
ADAM_EPS = 1e-08
ADAM_WD = 0.01
ADAM_STEP = 10
V7X_VMEM_LIMIT = 52 * 1024 * 1024
MESH = pl.DeviceIdType.MESH
HBM = pl.BlockSpec(memory_space=pltpu.HBM)
SEM = pl.BlockSpec(memory_space=pltpu.SEMAPHORE)


def _cparams(dims):
    return pltpu.CompilerParams(dimension_semantics=dims, vmem_limit_bytes=V7X_VMEM_LIMIT)


def _sigmoid(v):
    return 0.5 * jnp.tanh(0.5 * v) + 0.5


def _dot(a, b):
    return jnp.dot(a, b, preferred_element_type=F32)


def _dot_nt(a, b):
    return lax.dot_general(a, b, (((1,), (1,)), ((), ())), preferred_element_type=F32)


def _dot_tn(a, b):
    return lax.dot_general(a, b, (((0,), (0,)), ((), ())), preferred_element_type=F32)


def _split3(v):
    hi = v.astype(BF16)
    r1 = v - hi.astype(F32)
    mid = r1.astype(BF16)
    lo = (r1 - mid.astype(F32)).astype(BF16)
    return hi, mid, lo


def _tri_apply(tri, v):
    hi, mid, lo = _split3(v)
    return _dot(tri, hi) + _dot(tri, mid) + _dot(tri, lo)


def _mat_shape(a):
    return a.shape if a.ndim == 2 else (a.shape[1], a.shape[0] * a.shape[2])


def _tile_spec(a, rows, cols, row_of, col_of):
    if a.ndim == 2:
        return pl.BlockSpec((rows, cols), lambda *g: (row_of(*g), col_of(*g)))
    per = a.shape[2] // cols
    return pl.BlockSpec((None, rows, cols), lambda *g: (col_of(*g) // per, row_of(*g), col_of(*g) % per))


def _out_struct(rows, n, slabs, dtype):
    return jax.ShapeDtypeStruct((rows, n) if slabs is None else (slabs, rows, n // slabs), dtype)


def norm_mm(h, g, w, *, name, tm, tn, out_dtype, out_slabs=None):
    T, D = h.shape
    N = _mat_shape(w)[1]
    o_shape = _out_struct(T, N, out_slabs, out_dtype)

    def body(h_ref, g_ref, w_ref, o_ref, n_ref):
        @pl.when(pl.program_id(1) == 0)
        def _():
            x = h_ref[...]
            r = lax.rsqrt(jnp.mean(x * x, axis=-1, keepdims=True) + EPS)
            n_ref[...] = (x * r * g_ref[...]).astype(BF16)

        o_ref[...] = _dot(n_ref[...], w_ref[...]).astype(o_ref.dtype)

    return pl.pallas_call(
        body, name=name, grid=(T // tm, N // tn),
        in_specs=[pl.BlockSpec((tm, D), lambda i, j: (i, 0)),
                  pl.BlockSpec((1, D), lambda i, j: (0, 0)),
                  _tile_spec(w, D, tn, lambda i, j: 0, lambda i, j: j)],
        out_specs=[_tile_spec(o_shape, tm, tn, lambda i, j: i, lambda i, j: j),
                   pl.BlockSpec((tm, D), lambda i, j: (i, 0))],
        out_shape=[o_shape, jax.ShapeDtypeStruct((T, D), BF16)],
        compiler_params=_cparams(("parallel", "arbitrary")),
    )(h, g, w)


def mm_nn(a, w, res, *, name, tm, tn, tk, relu2=False):
    T, K = _mat_shape(a)
    N = w.shape[1]
    nk = K // tk

    def body(a_ref, w_ref, r_ref, o_ref, acc_ref):
        k = pl.program_id(2)
        av = a_ref[...]
        if relu2:
            av = jnp.maximum(av, 0.0)
            av = av * av
        part = _dot(av.astype(BF16), w_ref[...])

        @pl.when(k == 0)
        def _():
            acc_ref[...] = part

        @pl.when(k > 0)
        def _():
            acc_ref[...] += part

        @pl.when(k == nk - 1)
        def _():
            o_ref[...] = r_ref[...] + acc_ref[...]

    return pl.pallas_call(
        body, name=name, grid=(T // tm, N // tn, nk),
        in_specs=[_tile_spec(a, tm, tk, lambda i, j, k: i, lambda i, j, k: k),
                  pl.BlockSpec((tk, tn), lambda i, j, k: (k, j)),
                  pl.BlockSpec((tm, tn), lambda i, j, k: (i, j))],
        out_specs=pl.BlockSpec((tm, tn), lambda i, j, k: (i, j)),
        out_shape=jax.ShapeDtypeStruct((T, N), F32),
        scratch_shapes=[pltpu.VMEM((tm, tn), F32)],
        compiler_params=_cparams(("parallel", "parallel", "arbitrary")),
    )(a, w, res)


def mm_nt(a, w, *, name, tm, tn, tk, out_dtype, out_slabs=None, relu2_of=None, after=None):
    T, K = _mat_shape(a)
    nk = K // tk
    N = w.shape[0]
    has_z = relu2_of is not None
    o_shape = _out_struct(T, N, out_slabs, out_dtype)

    def body(*refs):
        a_ref, w_ref = refs[0], refs[1]
        z_ref = refs[2] if has_z else None
        o_ref, acc_ref = refs[-2], refs[-1]
        k = pl.program_id(2)
        part = _dot_nt(a_ref[...].astype(BF16), w_ref[...])

        @pl.when(k == 0)
        def _():
            acc_ref[...] = part

        @pl.when(k > 0)
        def _():
            acc_ref[...] += part

        @pl.when(k == nk - 1)
        def _():
            out = acc_ref[...]
            if has_z:
                out = out * (2.0 * jnp.maximum(z_ref[...], 0.0))
            o_ref[...] = out.astype(o_ref.dtype)

    in_specs = [_tile_spec(a, tm, tk, lambda i, j, k: i, lambda i, j, k: k),
                pl.BlockSpec((tn, tk), lambda i, j, k: (j, k))]
    args = [a, w]
    if has_z:
        in_specs.append(pl.BlockSpec((tm, tn), lambda i, j, k: (i, j)))
        args.append(relu2_of)
    if after is not None:
        in_specs.append(pl.BlockSpec(after.shape, lambda i, j, k: (0, 0)))
        args.append(after)
    return pl.pallas_call(
        body, name=name, grid=(T // tm, N // tn, nk),
        in_specs=in_specs,
        out_specs=_tile_spec(o_shape, tm, tn, lambda i, j, k: i, lambda i, j, k: j),
        out_shape=o_shape,
        scratch_shapes=[pltpu.VMEM((tm, tn), F32)],
        compiler_params=_cparams(("parallel", "parallel", "arbitrary")),
    )(*args)


def mm_nt_normbwd(a, w, h, g, dres, *, name, tm, tk, after=None):
    T, K = _mat_shape(a)
    nk = K // tk
    D = h.shape[1]
    with_dh = dres is not None

    def body(*refs):
        a_ref, w_ref, h_ref, g_ref = refs[:4]
        if with_dh:
            r_ref = refs[4]
            dh_ref, dhb_ref, dg_ref, acc_ref = refs[-4:]
        else:
            dg_ref, acc_ref = refs[-2:]
        i = pl.program_id(0)
        k = pl.program_id(1)
        part = _dot_nt(a_ref[...].astype(BF16), w_ref[...])

        @pl.when(k == 0)
        def _():
            acc_ref[...] = part

        @pl.when(k > 0)
        def _():
            acc_ref[...] += part

        @pl.when(k == nk - 1)
        def _():
            dn = acc_ref[...]
            x = h_ref[...]
            r = lax.rsqrt(jnp.mean(x * x, axis=-1, keepdims=True) + EPS)
            xr = x * r
            dgp = jnp.sum(dn * xr, axis=0, keepdims=True)

            @pl.when(i == 0)
            def _():
                dg_ref[...] = dgp

            @pl.when(i > 0)
            def _():
                dg_ref[...] += dgp

            if with_dh:
                dyg = dn * g_ref[...]
                dx = r * (dyg - xr * jnp.mean(dyg * xr, axis=-1, keepdims=True))
                out = r_ref[...] + dx
                dh_ref[...] = out
                dhb_ref[...] = out.astype(BF16)

    row = pl.BlockSpec((tm, D), lambda i, k: (i, 0))
    vec = pl.BlockSpec((1, D), lambda i, k: (0, 0))
    in_specs = [_tile_spec(a, tm, tk, lambda i, k: i, lambda i, k: k),
                _tile_spec(w, D, tk, lambda i, k: 0, lambda i, k: k), row, vec]
    args = [a, w, h, g]
    if with_dh:
        in_specs.append(row)
        args.append(dres)
        out_specs = [row, row, vec]
        out_shape = [jax.ShapeDtypeStruct((T, D), F32), jax.ShapeDtypeStruct((T, D), BF16),
                     jax.ShapeDtypeStruct((1, D), F32)]
    else:
        out_specs = vec
        out_shape = jax.ShapeDtypeStruct((1, D), F32)
    if after is not None:
        in_specs.append(pl.BlockSpec(after.shape, lambda i, k: (0, 0)))
        args.append(after)
    return pl.pallas_call(
        body, name=name, grid=(T // tm, nk),
        in_specs=in_specs, out_specs=out_specs, out_shape=out_shape,
        scratch_shapes=[pltpu.VMEM((tm, D), F32)],
        compiler_params=_cparams(("arbitrary", "arbitrary")),
    )(*args)


def mm_tn(a, b, *, name, tt, tko, tn, relu2=False, out_slabs=None):
    T, K = _mat_shape(a)
    N = _mat_shape(b)[1]
    nt = T // tt
    o_shape = _out_struct(K, N, out_slabs, BF16)

    def body(a_ref, b_ref, o_ref, acc_ref):
        t = pl.program_id(2)
        av = a_ref[...]
        if relu2:
            av = jnp.maximum(av, 0.0)
            av = av * av
        part = _dot_tn(av.astype(BF16), b_ref[...].astype(BF16))

        @pl.when(t == 0)
        def _():
            acc_ref[...] = part

        @pl.when(t > 0)
        def _():
            acc_ref[...] += part

        @pl.when(t == nt - 1)
        def _():
            o_ref[...] = acc_ref[...].astype(BF16)

    return pl.pallas_call(
        body, name=name, grid=(K // tko, N // tn, nt),
        in_specs=[_tile_spec(a, tt, tko, lambda kk, j, t: t, lambda kk, j, t: kk),
                  _tile_spec(b, tt, tn, lambda kk, j, t: t, lambda kk, j, t: j)],
        out_specs=_tile_spec(o_shape, tko, tn, lambda kk, j, t: kk, lambda kk, j, t: j),
        out_shape=o_shape,
        scratch_shapes=[pltpu.VMEM((tko, tn), F32)],
        compiler_params=_cparams(("parallel", "parallel", "arbitrary")),
    )(a, b)


def _resident(a):
    nd = a.ndim
    return pl.BlockSpec(a.shape, lambda i: (0,) * nd, pipeline_mode=pl.Buffered(1))


def _row_block(a, tm):
    if a.ndim == 2:
        return pl.BlockSpec((tm, a.shape[1]), lambda i: (i, 0))
    return pl.BlockSpec((a.shape[0], tm, a.shape[2]), lambda i: (0, i, 0))


def _cols(ref, c, width):
    if len(ref.shape) == 2:
        return ref[:, c * width:(c + 1) * width]
    per = ref.shape[2] // width
    if per == 1:
        return ref[c]
    return ref[c // per, :, (c % per) * width:(c % per + 1) * width]


def _set_cols(ref, c, width, val):
    if len(ref.shape) == 2:
        ref[:, c * width:(c + 1) * width] = val
        return
    per = ref.shape[2] // width
    if per == 1:
        ref[c] = val
    else:
        ref[c // per, :, (c % per) * width:(c % per + 1) * width] = val


def _all_cols(ref):
    if len(ref.shape) == 2:
        return ref[...]
    return jnp.concatenate([ref[s] for s in range(ref.shape[0])], axis=1)


def _rms(x):
    return lax.rsqrt(jnp.mean(x * x, axis=-1, keepdims=True) + EPS)


def _row_params():
    return _cparams(("arbitrary",))


def proj_norm(h, g, w, *, name, tm, tn, out_dtype, out_slabs=None):
    T, D = h.shape
    N = _mat_shape(w)[1]
    o_shape = _out_struct(T, N, out_slabs, out_dtype)

    def body(h_ref, g_ref, w_ref, o_ref, n_ref):
        x = h_ref[...]
        n = (x * _rms(x) * g_ref[...]).astype(BF16)
        n_ref[...] = n
        for c in range(N // tn):
            _set_cols(o_ref, c, tn, _dot(n, _cols(w_ref, c, tn)).astype(out_dtype))

    return pl.pallas_call(
        body, name=name, grid=(T // tm,),
        in_specs=[_row_block(h, tm), pl.BlockSpec((1, D), lambda i: (0, 0)), _resident(w)],
        out_specs=[_row_block(o_shape, tm), pl.BlockSpec((tm, D), lambda i: (i, 0))],
        out_shape=[o_shape, jax.ShapeDtypeStruct((T, D), BF16)],
        compiler_params=_row_params(),
    )(h, g, w)


def prenorm(h, g, after, *, tm):
    T, D = h.shape

    def body(h_ref, g_ref, _after_ref, n_ref):
        x = h_ref[...]
        n_ref[...] = (x * _rms(x) * g_ref[...]).astype(BF16)

    row = pl.BlockSpec((tm, D), lambda i: (i, 0))
    return pl.pallas_call(
        body, name="prenorm", grid=(T // tm,),
        in_specs=[row, pl.BlockSpec((1, D), lambda i: (0, 0)), _anchor_spec(after)],
        out_specs=row, out_shape=jax.ShapeDtypeStruct((T, D), BF16),
        compiler_params=_row_params(),
    )(h, g, after)


def proj_plain(a, w, *, name, tm, tn, out_dtype=BF16, out_slabs=None, relu2=False):
    T = a.shape[0]
    N = _mat_shape(w)[1]

    def body(a_ref, w_ref, o_ref):
        av = a_ref[...]
        for c in range(N // tn):
            z = _dot(av, _cols(w_ref, c, tn))
            if relu2:
                z = jnp.maximum(z, 0.0)
                z = z * z
            _set_cols(o_ref, c, tn, z.astype(out_dtype))

    o_shape = _out_struct(T, N, out_slabs, out_dtype)
    return pl.pallas_call(
        body, name=name, grid=(T // tm,),
        in_specs=[_row_block(a, tm), _resident(w)],
        out_specs=_row_block(o_shape, tm), out_shape=o_shape,
        compiler_params=_row_params(),
    )(a, w)


def proj_res_norm(a, w, res, g, w_next=None, *, name, tm, tn):
    T = res.shape[0]
    D = w.shape[1]
    chained = w_next is not None

    def body(*refs):
        a_ref, w_ref, r_ref, g_ref = refs[:4]
        h_ref, n_ref = refs[4 + chained], refs[5 + chained]
        av = _all_cols(a_ref)
        for c in range(D // tn):
            sl = slice(c * tn, (c + 1) * tn)
            h_ref[:, sl] = r_ref[:, sl] + _dot(av, w_ref[:, sl])
        hv = h_ref[...]
        n = (hv * _rms(hv) * g_ref[...]).astype(BF16)
        n_ref[...] = n
        if chained:
            for c in range(D // tn):
                sl = slice(c * tn, (c + 1) * tn)
                refs[-1][:, sl] = _dot(n, refs[4][:, sl]).astype(BF16)

    row = pl.BlockSpec((tm, D), lambda i: (i, 0))
    half = jax.ShapeDtypeStruct((T, D), BF16)
    return pl.pallas_call(
        body, name=name, grid=(T // tm,),
        in_specs=[_row_block(a, tm), _resident(w), row, pl.BlockSpec((1, D), lambda i: (0, 0))]
        + ([_resident(w_next)] if chained else []),
        out_specs=[row, row] + ([row] if chained else []),
        out_shape=[jax.ShapeDtypeStruct((T, D), F32), half] + ([half] if chained else []),
        compiler_params=_row_params(),
    )(*([a, w, res, g] + ([w_next] if chained else [])))


def proj_res_loss(a, w, res, g, target, *, name, tm, tn):
    T = res.shape[0]
    D = w.shape[1]

    def body(a_ref, w_ref, r_ref, g_ref, t_ref, dh_ref, dhb_ref, ls_ref, dg_ref):
        i = pl.program_id(0)
        gv = g_ref[...]
        ls, dg = 0.0, 0.0
        halves = [slice(s * (tm // 2), (s + 1) * (tm // 2)) for s in range(2)]
        for rows in halves:
            av = a_ref[rows, :]
            for c in range(D // tn):
                sl = slice(c * tn, (c + 1) * tn)
                dh_ref[rows, sl] = r_ref[rows, sl] + _dot(av, w_ref[:, sl])
        for rows in halves:
            x = dh_ref[rows, :]
            r = _rms(x)
            xr = x * r
            d = xr * gv - t_ref[rows, :]
            dy = d * (1.0 / D)
            dyg = dy * gv
            dx = r * (dyg - xr * jnp.mean(dyg * xr, axis=-1, keepdims=True))
            dh_ref[rows, :] = dx
            dhb_ref[rows, :] = dx.astype(BF16)
            ls = ls + jnp.sum(d * d, axis=0, keepdims=True)
            dg = dg + jnp.sum(dy * xr, axis=0, keepdims=True)

        @pl.when(i == 0)
        def _():
            ls_ref[...] = ls
            dg_ref[...] = dg

        @pl.when(i > 0)
        def _():
            ls_ref[...] += ls
            dg_ref[...] += dg

    row = pl.BlockSpec((tm, D), lambda i: (i, 0))
    vec = pl.BlockSpec((1, D), lambda i: (0, 0))
    return pl.pallas_call(
        body, name=name, grid=(T // tm,),
        in_specs=[_row_block(a, tm), _resident(w), row, vec, row],
        out_specs=[row, row, vec, vec],
        out_shape=[jax.ShapeDtypeStruct((T, D), F32), jax.ShapeDtypeStruct((T, D), BF16),
                   jax.ShapeDtypeStruct((1, D), F32), jax.ShapeDtypeStruct((1, D), F32)],
        compiler_params=_row_params(),
    )(a, w, res, g, target)


def _anchor_spec(after):
    return pl.BlockSpec(after.shape, lambda i: (0, 0))


def back_plain(a, w, *, name, tm, tn, out_dtype, out_slabs=None, relu2_value=None, after=None):
    T = a.shape[0]
    N = w.shape[0]
    has_z = relu2_value is not None
    o_shape = _out_struct(T, N, out_slabs, out_dtype)

    def body(*refs):
        a_ref, w_ref = refs[0], refs[1]
        o_ref = refs[-1]
        av = a_ref[...]
        for c in range(N // tn):
            out = _dot_nt(av, w_ref[c * tn:(c + 1) * tn, :])
            if has_z:
                out = out * (2.0 * jnp.sqrt(refs[2][:, c * tn:(c + 1) * tn]).astype(F32))
            _set_cols(o_ref, c, tn, out.astype(out_dtype))

    in_specs, args = [_row_block(a, tm), _resident(w)], [a, w]
    if has_z:
        in_specs.append(_row_block(relu2_value, tm))
        args.append(relu2_value)
    if after is not None:
        in_specs.append(_anchor_spec(after))
        args.append(after)
    return pl.pallas_call(
        body, name=name, grid=(T // tm,),
        in_specs=in_specs, out_specs=_row_block(o_shape, tm), out_shape=o_shape,
        compiler_params=_row_params(),
    )(*args)


def back_norm(a, w, h, g, dres, *, name, tm, tk, bf16_copy=True, w_next=None, next_dtype=BF16, next_slabs=None,
              after=None):
    T, K = _mat_shape(a)
    D = h.shape[1]
    with_dh = dres is not None
    chained = w_next is not None
    n_in = 4 + with_dh + chained
    tn = 4 * HEAD_W

    def body(*refs):
        a_ref, w_ref, h_ref, g_ref = refs[:4]
        outs = refs[n_in + (after is not None):]
        i = pl.program_id(0)
        dn = None
        for kc in range(K // tk):
            part = _dot_nt(_cols(a_ref, kc, tk).astype(BF16), _cols(w_ref, kc, tk))
            dn = part if dn is None else dn + part
        x = h_ref[...]
        r = _rms(x)
        xr = x * r
        dgp = jnp.sum(dn * xr, axis=0, keepdims=True)
        dg_ref = outs[-1]

        @pl.when(i == 0)
        def _():
            dg_ref[...] = dgp

        @pl.when(i > 0)
        def _():
            dg_ref[...] += dgp

        if with_dh:
            dyg = dn * g_ref[...]
            out = refs[4][...] + r * (dyg - xr * jnp.mean(dyg * xr, axis=-1, keepdims=True))
            outs[0][...] = out
            outb = out.astype(BF16)
            if bf16_copy:
                outs[1][...] = outb
            if chained:
                wn_ref, nx_ref = refs[5], outs[-2]
                for c in range(wn_ref.shape[0] // tn):
                    _set_cols(nx_ref, c, tn, _dot_nt(outb, wn_ref[c * tn:(c + 1) * tn, :]).astype(next_dtype))

    row = pl.BlockSpec((tm, D), lambda i: (i, 0))
    vec = pl.BlockSpec((1, D), lambda i: (0, 0))
    in_specs, args = [_row_block(a, tm), _resident(w), row, vec], [a, w, h, g]
    out_specs, out_shape = [], []
    if with_dh:
        in_specs.append(row)
        args.append(dres)
        out_specs.append(row)
        out_shape.append(jax.ShapeDtypeStruct((T, D), F32))
        if bf16_copy:
            out_specs.append(row)
            out_shape.append(jax.ShapeDtypeStruct((T, D), BF16))
    if chained:
        in_specs.append(_resident(w_next))
        args.append(w_next)
        nx_shape = _out_struct(T, w_next.shape[0], next_slabs, next_dtype)
        out_specs.append(_row_block(nx_shape, tm))
        out_shape.append(nx_shape)
    out_specs.append(vec)
    out_shape.append(jax.ShapeDtypeStruct((1, D), F32))
    if after is not None:
        in_specs.append(_anchor_spec(after))
        args.append(after)
    outs = pl.pallas_call(
        body, name=name, grid=(T // tm,),
        in_specs=in_specs, out_specs=out_specs, out_shape=out_shape,
        compiler_params=_row_params(),
    )(*args)
    return outs if len(outs) > 1 else outs[0]


def wgrad(a, b, *, name, tt, tn, out_slabs=None):
    T, K = _mat_shape(a)
    N = _mat_shape(b)[1]
    nt = T // tt
    o_shape = _out_struct(K, N, out_slabs, BF16)

    flipped = K > N and out_slabs is None

    def body(a_ref, b_ref, o_ref, acc_ref):
        t = pl.program_id(0)

        @pl.when(t == 0)
        def _():
            acc_ref[...] = jnp.zeros_like(acc_ref)

        if flipped:
            bt = _all_cols(b_ref).astype(BF16).T
            for c in range(K // tn):
                acc_ref[:, c * tn:(c + 1) * tn] += _dot(bt, _cols(a_ref, c, tn).astype(BF16))
        else:
            at = _all_cols(a_ref).astype(BF16).T
            for c in range(N // tn):
                acc_ref[:, c * tn:(c + 1) * tn] += _dot(at, _cols(b_ref, c, tn).astype(BF16))

        @pl.when(t == nt - 1)
        def _():
            if flipped:
                for c in range(K // tn):
                    o_ref[c * tn:(c + 1) * tn, :] = acc_ref[:, c * tn:(c + 1) * tn].T.astype(BF16)
            else:
                for c in range(N // tn):
                    _set_cols(o_ref, c, tn, acc_ref[:, c * tn:(c + 1) * tn].astype(BF16))

    return pl.pallas_call(
        body, name=name, grid=(nt,),
        in_specs=[_row_block(a, tt), _row_block(b, tt)],
        out_specs=_resident(o_shape), out_shape=o_shape,
        scratch_shapes=[pltpu.VMEM((N, K) if flipped else (K, N), F32)],
        compiler_params=_row_params(),
    )(a, b)


def chunk_triangles(tm):
    r = lax.broadcasted_iota(jnp.int32, (tm, tm), 0)
    c = lax.broadcasted_iota(jnp.int32, (tm, tm), 1)
    same = (r // CHUNK) == (c // CHUNK)
    tri = jnp.stack([same & (c <= r), same & (c >= r)]).astype(F32)
    return tri.astype(BF16), tri


def _tri_spec(tm):
    return pl.BlockSpec((2, tm, tm), lambda g, s, i: (0, 0, 0))


def _chunk_row(v, r, nc):
    return jnp.concatenate([jnp.broadcast_to(v[c * CHUNK + r:c * CHUNK + r + 1], (CHUNK, v.shape[1]))
                            for c in range(nc)], axis=0)


def _block_diag(v, nc):
    chunk = lax.broadcasted_iota(jnp.int32, (v.shape[0], 1), 0) // CHUNK
    return jnp.concatenate([jnp.where(chunk == c, v, jnp.zeros_like(v)) for c in range(nc)], axis=1)


def _pool_windows_back(ext_ref, tm):
    n = tm + 32
    ext_ref[1, 8:n] = ext_ref[0, 8:n] + ext_ref[0, 7:n - 1]
    ext_ref[2, 16:n] = ext_ref[1, 16:n] + ext_ref[1, 14:n - 2]
    ext_ref[3, 24:n] = ext_ref[2, 24:n] + ext_ref[2, 20:n - 4]
    s2 = ext_ref[1, 32:n]
    s4 = ext_ref[2, 32:n]
    s8 = ext_ref[3, 32:n]
    s16 = s8 + ext_ref[3, 24:n - 8]
    return s2, s4, s8, s16


def _pool_windows_fwd(ext_ref, tm):
    n = tm + 32
    ext_ref[1, 0:n - 8] = ext_ref[0, 0:n - 8] + ext_ref[0, 1:n - 7]
    ext_ref[2, 0:n - 16] = ext_ref[1, 0:n - 16] + ext_ref[1, 2:n - 14]
    ext_ref[3, 0:n - 24] = ext_ref[2, 0:n - 24] + ext_ref[2, 4:n - 20]
    s2 = ext_ref[1, 0:tm]
    s4 = ext_ref[2, 0:tm]
    s8 = ext_ref[3, 0:tm]
    s16 = s8 + ext_ref[3, 8:tm + 8]
    return s2, s4, s8, s16


def _select_window(g, s2, s4, s8, s16):
    return jnp.where(g == 0, s2, jnp.where(g == 1, s4, jnp.where(g == 2, s8, s16)))


def _pool_count(g, pos):
    width = lax.shift_left(jnp.int32(2), g)
    return jnp.minimum(pos + 1, width).astype(F32)


def _hgrn_gates(zq, zf, th):
    lb = _sigmoid(th[0:1, :] - th[1:2, :])
    sig = _sigmoid(zf)
    f = lb + (1.0 - lb) * sig
    sq = _sigmoid(zq)
    return lb, sig, f, sq


def mixer_fwd(u5, pool_w_bf, scale4, theta4, gn4, tri_bf, tri_f, *, seqs, seq_len, tm):
    T = u5.shape[1]
    tps = seq_len // tm
    nc = tm // CHUNK
    W = HEAD_W

    H = HEADS_PER_STEP
    heads = range(H)

    def body(u_ref, pw_ref, sc_ref, th_ref, gn_ref, tri_ref, msk_ref, y_ref, o_ref, st_ref, halo_ref, ext_ref, s_ref):
        g = pl.program_id(0)
        i = pl.program_id(2)

        @pl.when(i == 0)
        def _():
            halo_ref[...] = jnp.zeros_like(halo_ref)
            s_ref[...] = jnp.zeros_like(s_ref)

        row = lax.broadcasted_iota(jnp.int32, (tm, 1), 0)
        cols = [slice(h * W, (h + 1) * W) for h in heads]

        for h in heads:
            grp = g * H + h
            up = u_ref[0, :, cols[h]]
            ext_ref[h, 0, 0:16] = jnp.zeros((16, W), F32)
            ext_ref[h, 0, 16:32] = halo_ref[h]
            ext_ref[h, 0, 32:32 + tm] = up
            win = _select_window(grp, *_pool_windows_back(ext_ref.at[h], tm))
            p = win * (1.0 / _pool_count(grp, i * tm + row)) - up
            halo_ref[h] = up[tm - POOL_HALO:tm]
            y_ref[0, :, cols[h]] = (_dot(p.astype(BF16), pw_ref[h]) * sc_ref[h]).astype(BF16)

        zq, zf, zi, zg = u_ref[1], u_ref[2], u_ref[3], u_ref[4]
        th = [th_ref[h] for h in heads]
        lb = jnp.concatenate([_sigmoid(t[0:1, :] - t[1:2, :]) for t in th], axis=1)
        f = lb + (1.0 - lb) * _sigmoid(zf)
        kk = 1.0 - f
        q = zq * _sigmoid(zq)
        G = _tri_apply(tri_ref[0], jnp.log(f))
        Gm, Gl = _chunk_row(G, CHUNK // 2 - 1, nc), _chunk_row(G, CHUNK - 1, nc)
        vb = zi.astype(BF16)
        qrb = (q * jnp.exp(G - Gm)).astype(BF16)
        krb = (kk * jnp.exp(Gm - G)).astype(BF16)
        keb = (kk * jnp.exp(Gl - G)).astype(BF16)
        qgb = (q * jnp.exp(G)).astype(BF16)
        mask = msk_ref[0] > 0.5
        a = [jnp.where(mask, _dot_nt(qrb[:, cols[h]], krb[:, cols[h]]), 0.0).astype(BF16) for h in heads]
        d_st = [_dot_tn(vb[:, cols[h]], _block_diag(keb[:, cols[h]], nc)) for h in heads]
        o_intra = [_dot(a[h], vb[:, cols[h]]) for h in heads]
        st_cat = []
        for h in heads:
            st = s_ref[h]
            states = []
            for c in range(nc):
                st_ref[c, h] = st
                states.append(st.astype(BF16))
                st = st * jnp.exp(G[(c + 1) * CHUNK - 1:(c + 1) * CHUNK, cols[h]]) + d_st[h][:, c * W:(c + 1) * W]
            s_ref[h] = st
            st_cat.append(jnp.concatenate(states, axis=1))
        o = [o_intra[h] + _dot_nt(_block_diag(qgb[:, cols[h]], nc), st_cat[h]) for h in heads]
        gate = zg * _sigmoid(zg)
        for h in heads:
            o_ref[:, cols[h]] = o[h]
            r = lax.rsqrt(jnp.mean(o[h] * o[h], axis=-1, keepdims=True) + EPS)
            y_ref[1, :, cols[h]] = (o[h] * r * gn_ref[h] * gate[:, cols[h]]).astype(BF16)

    def rb(s, i):
        return s * tps + i

    def per_head(*shape):
        return pl.BlockSpec((H,) + shape, lambda g, s, i: (g,) + (0,) * len(shape))

    return pl.pallas_call(
        body, name="mixer_fwd", grid=(4 // H, seqs, tps),
        in_specs=[pl.BlockSpec((5, tm, H * W), lambda g, s, i: (0, rb(s, i), g)),
                  per_head(W, W), per_head(1, W), per_head(2, W), per_head(1, W),
                  _tri_spec(tm), _tri_spec(tm)],
        out_specs=[pl.BlockSpec((2, tm, H * W), lambda g, s, i: (0, rb(s, i), g)),
                   pl.BlockSpec((tm, H * W), lambda g, s, i: (rb(s, i), g)),
                   pl.BlockSpec((nc, H, W, W), lambda g, s, i: (rb(s, i), g, 0, 0))],
        out_shape=[jax.ShapeDtypeStruct((2, T, 4 * W), BF16),
                   jax.ShapeDtypeStruct((T, 4 * W), F32),
                   jax.ShapeDtypeStruct((T // CHUNK, 4, W, W), F32)],
        scratch_shapes=[pltpu.VMEM((H, POOL_HALO, W), F32),
                        pltpu.VMEM((H, 4, tm + 32, W), F32),
                        pltpu.VMEM((H, W, W), F32)],
        compiler_params=_cparams(("arbitrary", "arbitrary", "arbitrary")),
    )(u5, pool_w_bf, scale4, theta4, gn4, tri_bf, tri_f)


def mixer_bwd(u5, dy2, o_pre, st_prev, pool_w_bf, scale4, theta4, gn4, tri_bf, tri_f, after, *, seqs, seq_len, tm):
    T = u5.shape[1]
    tps = seq_len // tm
    nc = tm // CHUNK
    W = HEAD_W
    hb = tm // POOL_HALO

    H = HEADS_PER_STEP
    heads = range(H)

    def body(u_ref, uh_ref, dy_ref, o_ref, st_ref, pw_ref, sc_ref, th_ref, gn_ref, tri_ref, msk_ref, _after_ref,
             du_ref, dpw_ref, dsc_ref, dlb_ref, dgn_ref, nxt_ref, ext_ref, ds_ref):
        g = pl.program_id(0)
        s = pl.program_id(1)
        i = pl.program_id(2)
        tile = tps - 1 - i
        first = (s == 0) & (i == 0)

        @pl.when(i == 0)
        def _():
            nxt_ref[...] = jnp.zeros_like(nxt_ref)
            ds_ref[...] = jnp.zeros_like(ds_ref)

        row = lax.broadcasted_iota(jnp.int32, (tm, 1), 0)
        cols = [slice(h * W, (h + 1) * W) for h in heads]

        def accumulate(ref, h, val):
            @pl.when(first)
            def _():
                ref[h] = val

            @pl.when(jnp.logical_not(first))
            def _():
                ref[h] += val

        def per_head(fn):
            return jnp.concatenate([jnp.broadcast_to(fn(cols[h]), (tm, W)) for h in heads], axis=1)

        for h in heads:
            grp = g * H + h
            inv_cnt = 1.0 / _pool_count(grp, tile * tm + row)
            ext = ext_ref.at[h]
            up = u_ref[0, :, cols[h]]
            ext[0, 0:16] = jnp.zeros((16, W), F32)
            ext[0, 16:32] = jnp.where(tile == 0, 0.0, uh_ref[:, cols[h]])
            ext[0, 32:32 + tm] = up
            win = _select_window(grp, *_pool_windows_back(ext, tm))
            pb = (win * inv_cnt - up).astype(BF16)
            dyp = dy_ref[0, :, cols[h]]
            z = _dot(pb, pw_ref[h])
            accumulate(dsc_ref, h, jnp.sum(dyp * z, axis=0, keepdims=True))
            dz = (dyp * sc_ref[h]).astype(BF16)
            accumulate(dpw_ref, h, _dot_tn(pb, dz))
            dp = _dot_nt(dz, pw_ref[h])
            e = dp * inv_cnt
            ext[0, 0:tm] = e
            ext[0, tm:tm + 16] = nxt_ref[h]
            ext[0, tm + 16:tm + 32] = jnp.zeros((16, W), F32)
            lead = _select_window(grp, *_pool_windows_fwd(ext, tm))
            nxt_ref[h] = e[0:POOL_HALO]
            du_ref[0, :, cols[h]] = (lead - dp).astype(BF16)

        zq, zf, zi, zg = u_ref[1], u_ref[2], u_ref[3], u_ref[4]
        lb = jnp.concatenate([_sigmoid(th_ref[h][0:1, :] - th_ref[h][1:2, :]) for h in heads], axis=1)
        gn = jnp.concatenate([gn_ref[h] for h in heads], axis=1)
        sig, sq, sg = _sigmoid(zf), _sigmoid(zq), _sigmoid(zg)
        f = lb + (1.0 - lb) * sig
        kk = 1.0 - f
        q = zq * sq
        G = _tri_apply(tri_ref[0], jnp.log(f))

        dyh = dy_ref[1]
        o = o_ref[...]
        sqr = o * o
        r = per_head(lambda cs: lax.rsqrt(jnp.mean(sqr[:, cs], axis=-1, keepdims=True) + EPS))
        orr = o * r
        du_ref[4] = (dyh * (orr * gn) * (sg * (1.0 + zg * (1.0 - sg)))).astype(BF16)
        don = dyh * (zg * sg)
        dgn = jnp.sum(don * orr, axis=0, keepdims=True)
        dog = don * gn
        dog_orr = dog * orr
        do = r * (dog - orr * per_head(lambda cs: jnp.mean(dog_orr[:, cs], axis=-1, keepdims=True)))

        Gm, Gl = _chunk_row(G, CHUNK // 2 - 1, nc), _chunk_row(G, CHUNK - 1, nc)
        e_q, e_k, e_e, e_g = jnp.exp(G - Gm), jnp.exp(Gm - G), jnp.exp(Gl - G), jnp.exp(G)
        qr, kr, ke, qg = q * e_q, kk * e_k, kk * e_e, q * e_g
        qrb, krb, keb, qgb = qr.astype(BF16), kr.astype(BF16), ke.astype(BF16), qg.astype(BF16)
        vb = zi.astype(BF16)
        dob = do.astype(BF16)
        lower, upper = msk_ref[0] > 0.5, msk_ref[1] > 0.5
        da = [jnp.where(lower, _dot_nt(dob[:, cs], vb[:, cs]), 0.0).astype(BF16) for cs in cols]
        a_t = [jnp.where(upper, _dot_nt(krb[:, cs], qrb[:, cs]), 0.0).astype(BF16) for cs in cols]
        da_t = [jnp.where(upper, _dot_nt(vb[:, cs], dob[:, cs]), 0.0).astype(BF16) for cs in cols]
        u_cat = [_dot_tn(dob[:, cs], _block_diag(qgb[:, cs], nc)) for cs in cols]
        dqr = [_dot(da[h], krb[:, cols[h]]) for h in heads]
        dkr = [_dot(da_t[h], qrb[:, cols[h]]) for h in heads]
        dv = [_dot(a_t[h], dob[:, cols[h]]) for h in heads]
        dsn_rows, dsn_cols, ddecay = [], [], [[None] * H for _ in range(nc)]
        for h in heads:
            dsn = ds_ref[h]
            dsn_b = [None] * nc
            for c in reversed(range(nc)):
                decay = jnp.exp(G[(c + 1) * CHUNK - 1:(c + 1) * CHUNK, cols[h]])
                dsn_b[c] = dsn.astype(BF16)
                ddecay[c][h] = jnp.sum(dsn * st_ref[c, h], axis=0, keepdims=True) * decay
                dsn = u_cat[h][:, c * W:(c + 1) * W] + dsn * decay
            ds_ref[h] = dsn
            dsn_rows.append(jnp.concatenate(dsn_b, axis=0))
            dsn_cols.append(jnp.concatenate(dsn_b, axis=1))
        st_rows = [jnp.concatenate([st_ref[c, h].astype(BF16) for c in range(nc)], axis=0) for h in heads]
        dqg = [_dot(_block_diag(dob[:, cols[h]], nc), st_rows[h]) for h in heads]
        dke = [_dot(_block_diag(vb[:, cols[h]], nc), dsn_rows[h]) for h in heads]
        dv = [dv[h] + _dot_nt(_block_diag(keb[:, cols[h]], nc), dsn_cols[h]) for h in heads]
        dqr, dkr, dqg, dke, dv = (jnp.concatenate(parts, axis=1) for parts in (dqr, dkr, dqg, dke, dv))
        t_mid, t_qg, t_ke = dkr * kr - dqr * qr, dqg * qg, dke * ke
        dq = dqr * e_q + dqg * e_g
        dk = dkr * e_k + dke * e_e
        crow = lax.broadcasted_iota(jnp.int32, (CHUNK, 1), 0)
        ends = []
        for c in range(nc):
            sl = slice(c * CHUNK, (c + 1) * CHUNK)
            dgm = jnp.sum(t_mid[sl], axis=0, keepdims=True)
            dgl = jnp.sum(t_ke[sl], axis=0, keepdims=True) + jnp.concatenate(ddecay[c], axis=1)
            ends.append(jnp.where(crow == CHUNK // 2 - 1, dgm, 0.0) + jnp.where(crow == CHUNK - 1, dgl, 0.0))
        dG = t_qg - t_ke - t_mid + jnp.concatenate(ends, axis=0)
        dlogf = _tri_apply(tri_ref[1], dG)
        df = dlogf / f - dk
        du_ref[1] = (dq * (sq * (1.0 + zq * (1.0 - sq)))).astype(BF16)
        du_ref[2] = (df * (1.0 - lb) * (sig * (1.0 - sig))).astype(BF16)
        du_ref[3] = dv.astype(BF16)
        dlb = jnp.sum(df * (1.0 - sig), axis=0, keepdims=True) * (lb * (1.0 - lb))
        for h in heads:
            accumulate(dgn_ref, h, dgn[:, cols[h]])
            accumulate(dlb_ref, h, dlb[:, cols[h]])

    def rb(s, i):
        return s * tps + (tps - 1 - i)

    def per_head_spec(*shape):
        return pl.BlockSpec((H,) + shape, lambda g, s, i: (g,) + (0,) * len(shape))

    vec, mat = per_head_spec(1, W), per_head_spec(W, W)
    return pl.pallas_call(
        body, name="mixer_bwd", grid=(4 // H, seqs, tps),
        in_specs=[pl.BlockSpec((5, tm, H * W), lambda g, s, i: (0, rb(s, i), g)),
                  pl.BlockSpec((None, POOL_HALO, H * W), lambda g, s, i: (0, jnp.maximum(rb(s, i) * hb - 1, 0), g)),
                  pl.BlockSpec((2, tm, H * W), lambda g, s, i: (0, rb(s, i), g)),
                  pl.BlockSpec((tm, H * W), lambda g, s, i: (rb(s, i), g)),
                  pl.BlockSpec((nc, H, W, W), lambda g, s, i: (rb(s, i), g, 0, 0)),
                  mat, vec, per_head_spec(2, W), vec, _tri_spec(tm), _tri_spec(tm),
                  pl.BlockSpec(after.shape, lambda g, s, i: (0, 0))],
        out_specs=[pl.BlockSpec((5, tm, H * W), lambda g, s, i: (0, rb(s, i), g)), mat, vec, vec, vec],
        out_shape=[jax.ShapeDtypeStruct((5, T, 4 * W), BF16),
                   jax.ShapeDtypeStruct((4, W, W), F32),
                   jax.ShapeDtypeStruct((4, 1, W), F32),
                   jax.ShapeDtypeStruct((4, 1, W), F32),
                   jax.ShapeDtypeStruct((4, 1, W), F32)],
        scratch_shapes=[pltpu.VMEM((H, POOL_HALO, W), F32),
                        pltpu.VMEM((H, 4, tm + 32, W), F32),
                        pltpu.VMEM((H, W, W), F32)],
        compiler_params=_cparams(("arbitrary", "arbitrary", "arbitrary")),
    )(u5, u5, dy2, o_pre, st_prev, pool_w_bf, scale4, theta4, gn4, tri_bf, tri_f, after)


def _attn_probs(q, k, hd):
    s = _dot_nt(q, k) * (1.0 / (hd ** 0.5))
    e = jnp.exp(s - jnp.max(s, axis=-1, keepdims=True))
    return e * (1.0 / jnp.sum(e, axis=-1, keepdims=True))


def attn_fwd(q, kv3, *, seqs, seq_len, n_mem, tm):
    T, D = q.shape
    hd = D // XATTN_HEADS
    tps = seq_len // tm

    cols = [slice(h * hd, (h + 1) * hd) for h in range(XATTN_HEADS)]

    def body(q_ref, kv_ref, o_ref):
        p = [_attn_probs(q_ref[:, cs], kv_ref[0, :, cs], hd) for cs in cols]
        for h, cs in enumerate(cols):
            o_ref[:, cs] = _dot(p[h].astype(BF16), kv_ref[1, :, cs]).astype(BF16)

    return pl.pallas_call(
        body, name="attn_fwd", grid=(seqs, tps),
        in_specs=[pl.BlockSpec((tm, D), lambda b, i: (b * tps + i, 0)),
                  pl.BlockSpec((2, n_mem, D), lambda b, i: (0, b, 0))],
        out_specs=pl.BlockSpec((tm, D), lambda b, i: (b * tps + i, 0)),
        out_shape=jax.ShapeDtypeStruct((T, D), BF16),
        compiler_params=_cparams(("parallel", "arbitrary")),
    )(q, kv3)


def attn_bwd(q, kv3, do, *, seqs, seq_len, n_mem, tm):
    T, D = q.shape
    hd = D // XATTN_HEADS
    tps = seq_len // tm

    cols = [slice(h * hd, (h + 1) * hd) for h in range(XATTN_HEADS)]

    def body(q_ref, kv_ref, do_ref, dq_ref, dkv_ref):
        i = pl.program_id(1)

        @pl.when(i == 0)
        def _():
            dkv_ref[...] = jnp.zeros_like(dkv_ref)

        p = [_attn_probs(q_ref[:, cs], kv_ref[0, :, cs], hd) for cs in cols]
        dp = [_dot_nt(do_ref[:, cs], kv_ref[1, :, cs]) for cs in cols]
        ds = [(p[h] * (dp[h] - jnp.sum(dp[h] * p[h], axis=-1, keepdims=True)) * (1.0 / (hd ** 0.5))).astype(BF16)
              for h in range(XATTN_HEADS)]
        for h, cs in enumerate(cols):
            dq_ref[:, cs] = _dot(ds[h], kv_ref[0, :, cs]).astype(BF16)
            dkv_ref[0, :, cs] += _dot_tn(ds[h], q_ref[:, cs])
            dkv_ref[1, :, cs] += _dot_tn(p[h].astype(BF16), do_ref[:, cs])

    qspec = pl.BlockSpec((tm, D), lambda b, i: (b * tps + i, 0))
    kvspec = pl.BlockSpec((2, n_mem, D), lambda b, i: (0, b, 0))
    return pl.pallas_call(
        body, name="attn_bwd", grid=(seqs, tps),
        in_specs=[qspec, kvspec, qspec],
        out_specs=[qspec, kvspec],
        out_shape=[jax.ShapeDtypeStruct((T, D), BF16), jax.ShapeDtypeStruct((2, seqs * n_mem, D), F32)],
        compiler_params=_cparams(("parallel", "arbitrary")),
    )(q, kv3, do)


def final_loss(h, g, target, *, tm):
    T, D = h.shape

    def body(h_ref, g_ref, t_ref, dh_ref, dhb_ref, ls_ref, dg_ref):
        i = pl.program_id(0)
        x = h_ref[...]
        gv = g_ref[...]
        r = lax.rsqrt(jnp.mean(x * x, axis=-1, keepdims=True) + EPS)
        xr = x * r
        d = xr * gv - t_ref[...]
        dy = d * (1.0 / D)
        dyg = dy * gv
        dx = r * (dyg - xr * jnp.mean(dyg * xr, axis=-1, keepdims=True))
        dh_ref[...] = dx
        dhb_ref[...] = dx.astype(BF16)
        ls = jnp.sum(d * d, axis=0, keepdims=True)
        dg = jnp.sum(dy * xr, axis=0, keepdims=True)

        @pl.when(i == 0)
        def _():
            ls_ref[...] = ls
            dg_ref[...] = dg

        @pl.when(i > 0)
        def _():
            ls_ref[...] += ls
            dg_ref[...] += dg

    row = pl.BlockSpec((tm, D), lambda i: (i, 0))
    vec = pl.BlockSpec((1, D), lambda i: (0, 0))
    return pl.pallas_call(
        body, name="final_loss", grid=(T // tm,),
        in_specs=[row, vec, row], out_specs=[row, row, vec, vec],
        out_shape=[jax.ShapeDtypeStruct((T, D), F32), jax.ShapeDtypeStruct((T, D), BF16),
                   jax.ShapeDtypeStruct((1, D), F32), jax.ShapeDtypeStruct((1, D), F32)],
        compiler_params=_cparams(("arbitrary",)),
    )(h, g, target)


def _my_place():
    return lax.axis_index("x"), lax.axis_index("y"), lax.axis_index("c")


def _slot_of(px, py, pc):
    return 4 * px + 2 * py + pc


def _peer(k, x, y, c):
    return (1 - x if (k >> 2) & 1 else x, 1 - y if (k >> 1) & 1 else y, 1 - c if k & 1 else c)


def _split_copies(src_refs, land_refs, send_sems, recv_sems, scatter):
    x, y, c = _my_place()
    mine = _slot_of(x, y, c)
    copies = []
    for a, (src, land) in enumerate(zip(src_refs, land_refs)):
        for k in range(1, N_DEV):
            peer = _peer(k, x, y, c)
            copies.append(pltpu.make_async_remote_copy(
                src_ref=src.at[_slot_of(*peer)] if scatter else src, dst_ref=land.at[mine],
                send_sem=send_sems.at[a * N_PEERS + k - 1], recv_sem=recv_sems.at[a * N_PEERS + k - 1],
                device_id=peer, device_id_type=MESH))
    return copies


def split_start(groups, *, name, scatter):
    sizes = [len(srcs) for srcs, _ in groups]
    n_arr = sum(sizes)
    flat = [a for srcs, lands in groups for a in list(srcs) + list(lands)]

    def body(*refs):
        ins = refs[:2 * n_arr]
        sems = refs[4 * n_arr:4 * n_arr + 2 * len(groups)]
        token = refs[-1]
        at = 0
        for gi, n in enumerate(sizes):
            for cp in _split_copies(ins[at:at + n], ins[at + n:at + 2 * n], sems[2 * gi], sems[2 * gi + 1], scatter):
                cp.start()
            at += 2 * n
        token[...] = jnp.zeros_like(token)

    sem_shapes = []
    for n in sizes:
        sem_shapes += [pltpu.SemaphoreType.DMA((n * N_PEERS,))] * 2
    outs = pl.pallas_call(
        body, name=name,
        out_shape=tuple(pltpu.HBM(a.shape, a.dtype) for a in flat) + tuple(sem_shapes)
        + (jax.ShapeDtypeStruct((8, 128), F32),),
        in_specs=(HBM,) * len(flat),
        out_specs=(HBM,) * len(flat) + (SEM,) * len(sem_shapes) + (pl.BlockSpec(memory_space=pltpu.VMEM),),
        input_output_aliases={i: i for i in range(len(flat))},
        compiler_params=pltpu.CompilerParams(has_side_effects=pltpu.SideEffectType.DATAFLOW_SIDE_EFFECTING),
    )(*[pltpu.with_memory_space_constraint(a, pltpu.HBM) for a in flat])
    thru, sems, token = outs[:len(flat)], outs[len(flat):-1], outs[-1]
    started, at = [], 0
    for gi, n in enumerate(sizes):
        started.append((sems[2 * gi], sems[2 * gi + 1], thru[at:at + n], thru[at + n:at + 2 * n]))
        at += 2 * n
    return started, token


def split_wait(started, after, *, name, scatter):
    sizes = [len(g[2]) for g in started]
    n_arr = sum(sizes)
    flat = [a for g in started for a in list(g[2]) + list(g[3])]
    sems = [s for g in started for s in g[:2]]

    def body(*refs):
        ins = refs[:2 * n_arr]
        sem_refs = refs[2 * n_arr:2 * n_arr + len(sems)]
        at = 0
        for gi, n in enumerate(sizes):
            for cp in _split_copies(ins[at:at + n], ins[at + n:at + 2 * n], sem_refs[2 * gi], sem_refs[2 * gi + 1], scatter):
                cp.wait_send()
                cp.wait_recv()
            at += 2 * n

    outs = pl.pallas_call(
        body, name=name,
        out_shape=tuple(pltpu.HBM(a.shape, a.dtype) for a in flat),
        in_specs=(HBM,) * len(flat) + (SEM,) * len(sems) + (pl.BlockSpec(memory_space=pl.ANY),),
        out_specs=(HBM,) * len(flat),
        input_output_aliases={i: i for i in range(len(flat))},
        compiler_params=pltpu.CompilerParams(has_side_effects=pltpu.SideEffectType.DATAFLOW_SIDE_EFFECTING),
    )(*flat, *sems, after)
    done, at = [], 0
    for n in sizes:
        done.append((outs[at:at + n], outs[at + n:at + 2 * n]))
        at += 2 * n
    return done


SIBLING = 1
CHIP_PEERS = (2, 4, 6)
_SIDE_EFFECTS = pltpu.CompilerParams(has_side_effects=pltpu.SideEffectType.DATAFLOW_SIDE_EFFECTING)


def _chip_level_copies(src, land, send_sems, recv_sems):
    x, y, c = _my_place()
    return [pltpu.make_async_remote_copy(
        src_ref=src, dst_ref=land.at[_slot_of(x, y, c)], send_sem=send_sems.at[j], recv_sem=recv_sems.at[j],
        device_id=_peer(k, x, y, c), device_id_type=MESH) for j, k in enumerate((SIBLING,) + CHIP_PEERS)]


def _pass_on_copies(land, send_sems, recv_sems, receiving):
    x, y, c = _my_place()
    copies = []
    for j, k in enumerate(CHIP_PEERS):
        slot = _slot_of(*_peer(k ^ SIBLING if receiving else k, x, y, c))
        copies.append(pltpu.make_async_remote_copy(
            src_ref=land.at[slot], dst_ref=land.at[slot], send_sem=send_sems.at[j], recv_sem=recv_sems.at[j],
            device_id=_peer(SIBLING, x, y, c), device_id_type=MESH))
    return copies


def gather2_start(src, land, *, name):
    def body(src_ref, land_ref, src_out, land_out, send_sems, recv_sems, token):
        for cp in _chip_level_copies(src_ref, land_ref, send_sems, recv_sems):
            cp.start()
        token[...] = jnp.zeros_like(token)

    n = 1 + len(CHIP_PEERS)
    src_t, land_t, send_sems, recv_sems, token = pl.pallas_call(
        body, name=name,
        out_shape=(pltpu.HBM(src.shape, src.dtype), pltpu.HBM(land.shape, land.dtype),
                   pltpu.SemaphoreType.DMA((n,)), pltpu.SemaphoreType.DMA((n,)), jax.ShapeDtypeStruct((8, 128), F32)),
        in_specs=(HBM, HBM), out_specs=(HBM, HBM, SEM, SEM, pl.BlockSpec(memory_space=pltpu.VMEM)),
        input_output_aliases={0: 0, 1: 1}, compiler_params=_SIDE_EFFECTS,
    )(pltpu.with_memory_space_constraint(src, pltpu.HBM), pltpu.with_memory_space_constraint(land, pltpu.HBM))
    return (src_t, land_t, send_sems, recv_sems), token


def gather2_pass_on(started, after, *, name):
    src, land, send_a, recv_a = started

    def body(src_ref, land_ref, send_a_ref, recv_a_ref, after_ref, land_out, send_b, recv_b):
        for cp in _chip_level_copies(src_ref, land_ref, send_a_ref, recv_a_ref):
            cp.wait_send()
            cp.wait_recv()
        for cp in _pass_on_copies(land_ref, send_b, recv_b, False):
            cp.start()

    n = len(CHIP_PEERS)
    land_t, send_b, recv_b = pl.pallas_call(
        body, name=name,
        out_shape=(pltpu.HBM(land.shape, land.dtype), pltpu.SemaphoreType.DMA((n,)), pltpu.SemaphoreType.DMA((n,))),
        in_specs=(HBM, HBM, SEM, SEM, pl.BlockSpec(memory_space=pl.ANY)), out_specs=(HBM, SEM, SEM),
        input_output_aliases={1: 0}, compiler_params=_SIDE_EFFECTS,
    )(src, land, send_a, recv_a, after)
    return land_t, send_b, recv_b


def gather2_wait(passed, *, name):
    land, send_b, recv_b = passed

    def body(land_ref, send_ref, recv_ref, land_out):
        for cp in _pass_on_copies(land_ref, send_ref, recv_ref, False):
            cp.wait_send()
        for cp in _pass_on_copies(land_ref, send_ref, recv_ref, True):
            cp.wait_recv()

    return pl.pallas_call(
        body, name=name, out_shape=pltpu.HBM(land.shape, land.dtype),
        in_specs=(HBM, SEM, SEM), out_specs=HBM,
        input_output_aliases={0: 0}, compiler_params=_SIDE_EFFECTS,
    )(land, send_b, recv_b)


def allgather_small(bufs):
    n = len(bufs)

    def body(*refs):
        srcs, outs = refs[:n], refs[n:2 * n]
        send_sems, recv_sems, local_sems = refs[2 * n:]
        x, y, c = _my_place()
        mine = _slot_of(x, y, c)
        local = [pltpu.make_async_copy(s, o.at[mine], local_sems.at[a]) for a, (s, o) in enumerate(zip(srcs, outs))]
        for cp in local:
            cp.start()
        copies = _split_copies(srcs, outs, send_sems, recv_sems, False)
        for cp in copies:
            cp.start()
        for cp in copies:
            cp.wait()
        for cp in local:
            cp.wait()

    return pl.pallas_call(
        body, name="allgather_small",
        out_shape=[jax.ShapeDtypeStruct((N_DEV,) + b.shape, b.dtype) for b in bufs],
        in_specs=[HBM] * n, out_specs=[HBM] * n,
        scratch_shapes=[pltpu.SemaphoreType.DMA((n * N_PEERS,)), pltpu.SemaphoreType.DMA((n * N_PEERS,)),
                        pltpu.SemaphoreType.DMA((n,))],
    )(*bufs)


def _adamw_math(g, w, m, v):
    c1 = 1.0 - ADAM_B1 ** ADAM_STEP
    c2 = 1.0 - ADAM_B2 ** ADAM_STEP
    nm = ADAM_B1 * m + (1.0 - ADAM_B1) * g
    nv = ADAM_B2 * v + (1.0 - ADAM_B2) * (g * g)
    delta = -ADAM_LR * ((nm / c1) / (jnp.sqrt(nv / c2) + ADAM_EPS) + ADAM_WD * w)
    return delta, nm, nv


def adamw_sharded(me, own, recv, w, m, v, *, name, tr):
    R, C = w.shape

    def body(me_ref, *refs):
        parts = refs[:N_DEV]
        w_ref, m_ref, v_ref, g_ref, d_ref, nm_ref, nv_ref = refs[N_DEV:]
        g = parts[0][...].astype(F32)
        for p in parts[1:]:
            g = g + p[...].astype(F32)
        g_ref[...] = g
        d_ref[...], nm_ref[...], nv_ref[...] = _adamw_math(g, w_ref[...], m_ref[...], v_ref[...])

    def slab(k):
        return pl.BlockSpec((None, tr, C), lambda i, me_ref: (me_ref[0] ^ k, i, 0))

    blk = pl.BlockSpec((tr, C), lambda i, me_ref: (i, 0))
    out = jax.ShapeDtypeStruct((R, C), F32)
    return pl.pallas_call(
        body, name=name,
        grid_spec=pltpu.PrefetchScalarGridSpec(
            num_scalar_prefetch=1, grid=(R // tr,),
            in_specs=[slab(k) for k in range(N_DEV)] + [blk, blk, blk],
            out_specs=[blk, blk, blk, blk]),
        out_shape=[out, out, out, out],
        compiler_params=_cparams(("parallel",)),
    )(me, own, *([recv] * N_PEERS), w, m, v)


def adamw_replicated(parts, ws, ms, vs, rows):
    n_buf, n_par = len(parts), len(ws)

    def body(*refs):
        p_refs = refs[:n_buf]
        w_refs = refs[n_buf:n_buf + n_par]
        m_refs = refs[n_buf + n_par:n_buf + 2 * n_par]
        v_refs = refs[n_buf + 2 * n_par:n_buf + 3 * n_par]
        outs = refs[n_buf + 3 * n_par:]
        sums = []
        for p in p_refs:
            g = p[0]
            for s in range(1, N_DEV):
                g = g + p[s]
            sums.append(g)
        for j, (b, r0, nr) in enumerate(rows):
            g = sums[b][r0:r0 + nr]
            delta, nm, nv = _adamw_math(g, w_refs[j][...], m_refs[j][...], v_refs[j][...])
            outs[j][...] = g
            outs[n_par + j][...] = delta
            outs[2 * n_par + j][...] = nm
            outs[3 * n_par + j][...] = nv

    shapes = [jax.ShapeDtypeStruct(w.shape, F32) for w in ws]
    outs = pl.pallas_call(
        body, name="adamw_replicated", out_shape=shapes * 4,
        compiler_params=pltpu.CompilerParams(vmem_limit_bytes=V7X_VMEM_LIMIT),
    )(*parts, *ws, *ms, *vs)
    return outs[:n_par], outs[n_par:2 * n_par], outs[2 * n_par:3 * n_par], outs[3 * n_par:]


BIG = ("w_in", "w_out", "xw_q", "xw_kv", "xw_o", "w_up", "w_down")
COL_SHARDED = ("w_in", "xw_kv", "w_up")
WEIGHTS = ("norm_mix", "w_in", "pool_w", "pool_scale", "lb_theta", "hgrn_norm", "w_out", "norm_xq",
           "norm_mem", "xw_q", "xw_kv", "xw_o", "norm_mlp", "w_up", "w_down", "norm_final")
SMALL = (("pool_w", (4 * HEAD_W, HEAD_W), 0, 0),
         ("norm_mix", (1, 1024), 1, 0), ("norm_xq", (1, 1024), 1, 1), ("norm_mem", (1, 1024), 1, 2),
         ("norm_mlp", (1, 1024), 1, 3), ("norm_final", (1, 1024), 1, 4),
         ("pool_scale", (1, 512), 2, 0), ("hgrn_norm", (1, 512), 2, 1), ("lb_theta", (2, 512), 2, 2))


def _pad_rows(a, rows):
    return jnp.concatenate([a, jnp.zeros((rows - a.shape[0], a.shape[1]), a.dtype)], axis=0)


def kernel(x, mem, norm_mix, w_in, pool_w, pool_scale, lb_theta, hgrn_norm, w_out, norm_xq, norm_mem, xw_q, xw_kv, xw_o, norm_mlp, w_up, w_down, norm_final, loss_target, m_norm_mix, m_w_in, m_pool_w, m_pool_scale, m_lb_theta, m_hgrn_norm, m_w_out, m_norm_xq, m_norm_mem, m_xw_q, m_xw_kv, m_xw_o, m_norm_mlp, m_w_up, m_w_down, m_norm_final, v_norm_mix, v_w_in, v_pool_w, v_pool_scale, v_lb_theta, v_hgrn_norm, v_w_out, v_norm_xq, v_norm_mem, v_xw_q, v_xw_kv, v_xw_o, v_norm_mlp, v_w_up, v_w_down, v_norm_final):
    w = dict(norm_mix=norm_mix, w_in=w_in, pool_w=pool_w, pool_scale=pool_scale, lb_theta=lb_theta,
             hgrn_norm=hgrn_norm, w_out=w_out, norm_xq=norm_xq, norm_mem=norm_mem, xw_q=xw_q, xw_kv=xw_kv,
             xw_o=xw_o, norm_mlp=norm_mlp, w_up=w_up, w_down=w_down, norm_final=norm_final)
    mom = dict(norm_mix=m_norm_mix, w_in=m_w_in, pool_w=m_pool_w, pool_scale=m_pool_scale, lb_theta=m_lb_theta,
               hgrn_norm=m_hgrn_norm, w_out=m_w_out, norm_xq=m_norm_xq, norm_mem=m_norm_mem, xw_q=m_xw_q,
               xw_kv=m_xw_kv, xw_o=m_xw_o, norm_mlp=m_norm_mlp, w_up=m_w_up, w_down=m_w_down,
               norm_final=m_norm_final)
    var = dict(norm_mix=v_norm_mix, w_in=v_w_in, pool_w=v_pool_w, pool_scale=v_pool_scale, lb_theta=v_lb_theta,
               hgrn_norm=v_hgrn_norm, w_out=v_w_out, norm_xq=v_norm_xq, norm_mem=v_norm_mem, xw_q=v_xw_q,
               xw_kv=v_xw_kv, xw_o=v_xw_o, norm_mlp=v_norm_mlp, w_up=v_w_up, w_down=v_w_down,
               norm_final=v_norm_final)

    seqs, seq_len, D = x.shape
    n_mem = mem.shape[1]
    T = seqs * seq_len
    W = HEAD_W
    x2 = x.reshape(T, D)
    mem2 = mem.reshape(seqs * n_mem, D)
    tgt2 = loss_target.reshape(T, D)
    tm_big = min(1024, T)
    tm_mid = min(512, T)
    tm_sq = min(1024, T)
    tm_mix = min(256, seq_len)
    tm_att = min(1024, seq_len)
    tkv = min(512, seqs * n_mem)
    px, py, pc = _my_place()
    me = _slot_of(px, py, pc).astype(jnp.int32)
    me1 = me.reshape(1)

    shard_bf = {n: w[n][0].astype(BF16) for n in BIG}

    def landing(n):
        zone = lax.empty((N_DEV,) + shard_bf[n].shape, BF16)
        return lax.dynamic_update_slice(zone, shard_bf[n][None], (me, 0, 0))

    w_in_started, tok = gather2_start(shard_bf["w_in"], landing("w_in"), name="w_in_gather_start")
    shard_bf["w_out"] = shard_bf["w_out"] + tok[0, 0].astype(BF16)
    ag_groups = (("w_out", "xw_q", "xw_kv", "xw_o"), ("w_up",), ("w_down",))
    ag_started, tok = split_start([([shard_bf[n] for n in grp], [landing(n) for n in grp]) for grp in ag_groups],
                                name="weights_gather_start", scatter=False)

    pool_w_bf = pool_w[0].astype(BF16)
    scale4 = pool_scale.reshape(4, 1, W)
    gn4 = hgrn_norm.reshape(4, 1, W)
    theta4 = lb_theta.reshape(2, 4, W).transpose(1, 0, 2)
    g_final = norm_final.reshape(1, D)

    n1 = prenorm(x2, norm_mix, tok, tm=tm_sq)
    wi3 = gather2_wait(gather2_pass_on(w_in_started, n1, name="w_in_gather_pass_on"), name="w_in_gather_wait")
    full_w_in = wi3.transpose(1, 0, 2).reshape(D, -1)
    u5 = proj_plain(n1, full_w_in, name="in_proj", tm=tm_mid, tn=4 * W, out_dtype=F32, out_slabs=5)
    tri_bf, tri_f = chunk_triangles(tm_mix)
    y2, o_pre, st_prev = mixer_fwd(u5, pool_w_bf, scale4, theta4, gn4, tri_bf, tri_f, seqs=seqs, seq_len=seq_len,
                                   tm=tm_mix)
    (_, (wo3, wq3, wkv3, wao3)), = split_wait(ag_started[0:1], y2, name="weights_gather_wait_attn", scatter=False)
    full_w_out, full_xw_q, full_xw_o = wo3.reshape(D, D), wq3.reshape(D, D), wao3.reshape(D, D)
    tn = 4 * W
    h1, n2, q = proj_res_norm(y2, full_w_out, x2, norm_xq, full_xw_q, name="out_q_proj", tm=tm_sq, tn=tn)
    kv3, memn = proj_norm(mem2, norm_mem, wkv3, name="kv_proj", tm=tkv, tn=wkv3.shape[2], out_dtype=BF16,
                          out_slabs=2)
    o_att = attn_fwd(q, kv3, seqs=seqs, seq_len=seq_len, n_mem=n_mem, tm=tm_att)
    h2, n3 = proj_res_norm(o_att, full_xw_o, h1, norm_mlp, name="attn_out_proj", tm=tm_sq, tn=tn)
    (_, (wup3,)), = split_wait(ag_started[1:2], h2, name="weights_gather_wait_up", scatter=False)
    tn_up = wup3.shape[2]
    aa = proj_plain(n3, wup3, name="up_proj", tm=tm_mid, tn=tn_up, relu2=True)
    (_, (wdn3,)), = split_wait(ag_started[2:3], aa, name="weights_gather_wait_down", scatter=False)
    full_w_down = wdn3.reshape(-1, D)
    dh3, dh3b, sq_err, dg_final = proj_res_loss(aa, full_w_down, h2, g_final, tgt2, name="down_proj_loss",
                                                tm=tm_mid, tn=tn)

    def send(parts, name):
        srcs = [p.reshape((N_DEV, -1, p.shape[-1])) for p in parts]
        lands = [lax.empty(s.shape, BF16) for s in srcs]
        started, token = split_start([(srcs, lands)], name=name, scatter=True)
        return started[0], token

    gw_down = wgrad(aa, dh3b, name="down_proj_wgrad", tt=tm_mid, tn=tn)
    sent_down, tok = send([gw_down], "grads_send_down")
    dap = back_plain(dh3b, full_w_down, name="down_proj_bwd", tm=tm_mid, tn=tn, out_dtype=BF16, relu2_value=aa,
                     after=tok)
    gw_up = wgrad(n3, dap, name="up_proj_wgrad", tt=tm_mid, tn=tn_up, out_slabs=N_DEV)
    sent_up, tok = send([gw_up], "grads_send_up")
    dh2, dh2b, do_att, dg_mlp = back_norm(dap, wup3, h2, norm_mlp, dh3, name="up_proj_bwd", tm=tm_mid, tk=tn_up,
                                          w_next=full_xw_o, after=tok)
    gxw_o = wgrad(o_att, dh2b, name="attn_out_proj_wgrad", tt=tm_sq, tn=tn)
    dq, dkv3 = attn_bwd(q, kv3, do_att, seqs=seqs, seq_len=seq_len, n_mem=n_mem, tm=tm_att)
    gxw_q = wgrad(n2, dq, name="q_proj_wgrad", tt=tm_sq, tn=tn)
    gxw_kv = wgrad(memn, dkv3, name="kv_proj_wgrad", tt=tkv, tn=wkv3.shape[2], out_slabs=N_DEV)
    sent_attn, tok = send([gxw_o, gxw_q, gxw_kv], "grads_send_attn")
    dg_mem = back_norm(dkv3, wkv3, mem2, norm_mem, None, name="kv_proj_bwd", tm=tkv, tk=wkv3.shape[2])
    dh1, dh1b, dy2, dg_xq = back_norm(dq, full_xw_q, h1, norm_xq, dh2, name="q_proj_bwd", tm=tm_mid, tk=D,
                                      w_next=full_w_out, next_dtype=F32, next_slabs=2, after=tok)
    gw_out = wgrad(y2, dh1b, name="out_proj_wgrad", tt=tm_sq, tn=tn)
    sent_out, tok = send([gw_out], "grads_send_out")
    du5, dpw, dsc, dlb, dgn = mixer_bwd(u5, dy2, o_pre, st_prev, pool_w_bf, scale4, theta4, gn4, tri_bf, tri_f, tok,
                                        seqs=seqs, seq_len=seq_len, tm=tm_mix)
    gw_in = wgrad(n1, du5, name="in_proj_wgrad", tt=tm_mid, tn=tn)
    gw_in_slots = gw_in.reshape(D, N_DEV, -1).transpose(1, 0, 2)
    sent_in, tok = send([gw_in_slots], "grads_send_in")
    dx, dg_mix = back_norm(du5, full_w_in, x2, norm_mix, dh1, name="in_proj_bwd", tm=tm_mid, tk=tn, bf16_copy=False,
                           after=tok)

    dlb_row = dlb.reshape(1, 4 * W)
    buf_vec = _pad_rows(jnp.concatenate([dg_mix, dg_xq, dg_mem, dg_mlp, dg_final, sq_err], axis=0), 8)
    buf_half = _pad_rows(jnp.concatenate([dsc.reshape(1, 4 * W), dgn.reshape(1, 4 * W), dlb_row, -dlb_row], axis=0), 8)
    small_src = [dpw.reshape(4 * W, W), buf_vec, buf_half]
    small_land = [lax.dynamic_update_slice(lax.empty((N_DEV,) + b.shape, F32), b[None], (me, 0, 0))
                  for b in small_src]
    small_started, tok = split_start([(small_src, small_land)], name="small_grads_start", scatter=False)

    done = split_wait([sent_down, sent_up, sent_attn, sent_out, sent_in], tok, name="grads_wait", scatter=True)
    slots = dict(w_down=(0, 0), w_up=(1, 0), xw_o=(2, 0), xw_q=(2, 1), xw_kv=(2, 2), w_out=(3, 0), w_in=(4, 0))
    own = {n: done[gi][0][ai] for n, (gi, ai) in slots.items()}
    got = {n: done[gi][1][ai] for n, (gi, ai) in slots.items()}
    res = {}
    for n in BIG:
        shp = w[n].shape
        r = adamw_sharded(me1, own[n], got[n], w[n][0], mom[n][0], var[n][0], name="adamw_" + n,
                          tr=min(256, shp[1]))
        for kind, a in zip("gdmv", r):
            res[kind, n] = a.reshape(shp)
    (_, small_parts), = split_wait(small_started, res["g", BIG[-1]], name="small_grads_wait", scatter=False)
    loss = 0.5 * jnp.sum(small_parts[1][:, 5, :]) / D
    r = adamw_replicated(small_parts, [w[n].reshape(v2) for n, v2, _, _ in SMALL],
                         [mom[n].reshape(v2) for n, v2, _, _ in SMALL],
                         [var[n].reshape(v2) for n, v2, _, _ in SMALL],
                         [(b, r0, v2[0]) for _, v2, b, r0 in SMALL])
    for kind, arrs in zip("gdmv", r):
        for (n, _, _, _), a in zip(SMALL, arrs):
            res[kind, n] = a.reshape(w[n].shape)

    out = [loss, dx.reshape(x.shape)]
    for kind in "gdmv":
        out += [res[kind, n] for n in WEIGHTS]
    return tuple(out)
```

```python
import jax
import jax.numpy as jnp
from jax import lax
from jax.experimental import pallas as pl
from jax.experimental.pallas import tpu as pltpu

F32 = jnp.float32
BF16 = jnp.bfloat16
EPS = 1e-6
CHUNK = 64
POOL_HALO = 16
HEAD_W = 128
HEADS_PER_STEP = 4
XATTN_HEADS = 4
N_DEV = 8
N_PEERS = N_DEV - 1
ADAM_LR = 0.001
ADAM_B1 = 0.9
ADAM_B2 = 0.999
ADAM_EPS = 1e-08
ADAM_WD = 0.01
ADAM_STEP = 10
V7X_VMEM_LIMIT = 52 * 1024 * 1024
MESH = pl.DeviceIdType.MESH
HBM = pl.BlockSpec(memory_space=pltpu.HBM)
SEM = pl.BlockSpec(memory_space=pltpu.SEMAPHORE)


def _cparams(dims):
    return pltpu.CompilerParams(dimension_semantics=dims, vmem_limit_bytes=V7X_VMEM_LIMIT)


def _sigmoid(v):
    return 0.5 * jnp.tanh(0.5 * v) + 0.5


def _dot(a, b):
    return jnp.dot(a, b, preferred_element_type=F32)


def _dot_nt(a, b):
    return lax.dot_general(a, b, (((1,), (1,)), ((), ())), preferred_element_type=F32)


def _dot_tn(a, b):
    return lax.dot_general(a, b, (((0,), (0,)), ((), ())), preferred_element_type=F32)


def _split3(v):
    hi = v.astype(BF16)
    r1 = v - hi.astype(F32)
    mid = r1.astype(BF16)
    lo = (r1 - mid.astype(F32)).astype(BF16)
    return hi, mid, lo


def _tri_apply(tri, v):
    hi, mid, lo = _split3(v)
    return _dot(tri, hi) + _dot(tri, mid) + _dot(tri, lo)


def _mat_shape(a):
    return a.shape if a.ndim == 2 else (a.shape[1], a.shape[0] * a.shape[2])


def _tile_spec(a, rows, cols, row_of, col_of):
    if a.ndim == 2:
        return pl.BlockSpec((rows, cols), lambda *g: (row_of(*g), col_of(*g)))
    per = a.shape[2] // cols
    return pl.BlockSpec((None, rows, cols), lambda *g: (col_of(*g) // per, row_of(*g), col_of(*g) % per))


def _out_struct(rows, n, slabs, dtype):
    return jax.ShapeDtypeStruct((rows, n) if slabs is None else (slabs, rows, n // slabs), dtype)


def norm_mm(h, g, w, *, name, tm, tn, out_dtype, out_slabs=None):
    T, D = h.shape
    N = _mat_shape(w)[1]
    o_shape = _out_struct(T, N, out_slabs, out_dtype)

    def body(h_ref, g_ref, w_ref, o_ref, n_ref):
        @pl.when(pl.program_id(1) == 0)
        def _():
            x = h_ref[...]
            r = lax.rsqrt(jnp.mean(x * x, axis=-1, keepdims=True) + EPS)
            n_ref[...] = (x * r * g_ref[...]).astype(BF16)

        o_ref[...] = _dot(n_ref[...], w_ref[...]).astype(o_ref.dtype)

    return pl.pallas_call(
        body, name=name, grid=(T // tm, N // tn),
        in_specs=[pl.BlockSpec((tm, D), lambda i, j: (i, 0)),
                  pl.BlockSpec((1, D), lambda i, j: (0, 0)),
                  _tile_spec(w, D, tn, lambda i, j: 0, lambda i, j: j)],
        out_specs=[_tile_spec(o_shape, tm, tn, lambda i, j: i, lambda i, j: j),
                   pl.BlockSpec((tm, D), lambda i, j: (i, 0))],
        out_shape=[o_shape, jax.ShapeDtypeStruct((T, D), BF16)],
        compiler_params=_cparams(("parallel", "arbitrary")),
    )(h, g, w)


def mm_nn(a, w, res, *, name, tm, tn, tk, relu2=False):
    T, K = _mat_shape(a)
    N = w.shape[1]
    nk = K // tk

    def body(a_ref, w_ref, r_ref, o_ref, acc_ref):
        k = pl.program_id(2)
        av = a_ref[...]
        if relu2:
            av = jnp.maximum(av, 0.0)
            av = av * av
        part = _dot(av.astype(BF16), w_ref[...])

        @pl.when(k == 0)
        def _():
            acc_ref[...] = part

        @pl.when(k > 0)
        def _():
            acc_ref[...] += part

        @pl.when(k == nk - 1)
        def _():
            o_ref[...] = r_ref[...] + acc_ref[...]

    return pl.pallas_call(
        body, name=name, grid=(T // tm, N // tn, nk),
        in_specs=[_tile_spec(a, tm, tk, lambda i, j, k: i, lambda i, j, k: k),
                  pl.BlockSpec((tk, tn), lambda i, j, k: (k, j)),
                  pl.BlockSpec((tm, tn), lambda i, j, k: (i, j))],
        out_specs=pl.BlockSpec((tm, tn), lambda i, j, k: (i, j)),
        out_shape=jax.ShapeDtypeStruct((T, N), F32),
        scratch_shapes=[pltpu.VMEM((tm, tn), F32)],
        compiler_params=_cparams(("parallel", "parallel", "arbitrary")),
    )(a, w, res)


def mm_nt(a, w, *, name, tm, tn, tk, out_dtype, out_slabs=None, relu2_of=None, after=None):
    T, K = _mat_shape(a)
    nk = K // tk
    N = w.shape[0]
    has_z = relu2_of is not None
    o_shape = _out_struct(T, N, out_slabs, out_dtype)

    def body(*refs):
        a_ref, w_ref = refs[0], refs[1]
        z_ref = refs[2] if has_z else None
        o_ref, acc_ref = refs[-2], refs[-1]
        k = pl.program_id(2)
        part = _dot_nt(a_ref[...].astype(BF16), w_ref[...])

        @pl.when(k == 0)
        def _():
            acc_ref[...] = part

        @pl.when(k > 0)
        def _():
            acc_ref[...] += part

        @pl.when(k == nk - 1)
        def _():
            out = acc_ref[...]
            if has_z:
                out = out * (2.0 * jnp.maximum(z_ref[...], 0.0))
            o_ref[...] = out.astype(o_ref.dtype)

    in_specs = [_tile_spec(a, tm, tk, lambda i, j, k: i, lambda i, j, k: k),
                pl.BlockSpec((tn, tk), lambda i, j, k: (j, k))]
    args = [a, w]
    if has_z:
        in_specs.append(pl.BlockSpec((tm, tn), lambda i, j, k: (i, j)))
        args.append(relu2_of)
    if after is not None:
        in_specs.append(pl.BlockSpec(after.shape, lambda i, j, k: (0, 0)))
        args.append(after)
    return pl.pallas_call(
        body, name=name, grid=(T // tm, N // tn, nk),
        in_specs=in_specs,
        out_specs=_tile_spec(o_shape, tm, tn, lambda i, j, k: i, lambda i, j, k: j),
        out_shape=o_shape,
        scratch_shapes=[pltpu.VMEM((tm, tn), F32)],
        compiler_params=_cparams(("parallel", "parallel", "arbitrary")),
    )(*args)


def mm_nt_normbwd(a, w, h, g, dres, *, name, tm, tk, after=None):
    T, K = _mat_shape(a)
    nk = K // tk
    D = h.shape[1]
    with_dh = dres is not None

    def body(*refs):
        a_ref, w_ref, h_ref, g_ref = refs[:4]
        if with_dh:
            r_ref = refs[4]
            dh_ref, dhb_ref, dg_ref, acc_ref = refs[-4:]
        else:
            dg_ref, acc_ref = refs[-2:]
        i = pl.program_id(0)
        k = pl.program_id(1)
        part = _dot_nt(a_ref[...].astype(BF16), w_ref[...])

        @pl.when(k == 0)
        def _():
            acc_ref[...] = part

        @pl.when(k > 0)
        def _():
            acc_ref[...] += part

        @pl.when(k == nk - 1)
        def _():
            dn = acc_ref[...]
            x = h_ref[...]
            r = lax.rsqrt(jnp.mean(x * x, axis=-1, keepdims=True) + EPS)
            xr = x * r
            dgp = jnp.sum(dn * xr, axis=0, keepdims=True)

            @pl.when(i == 0)
            def _():
                dg_ref[...] = dgp

            @pl.when(i > 0)
            def _():
                dg_ref[...] += dgp

            if with_dh:
                dyg = dn * g_ref[...]
                dx = r * (dyg - xr * jnp.mean(dyg * xr, axis=-1, keepdims=True))
                out = r_ref[...] + dx
                dh_ref[...] = out
                dhb_ref[...] = out.astype(BF16)

    row = pl.BlockSpec((tm, D), lambda i, k: (i, 0))
    vec = pl.BlockSpec((1, D), lambda i, k: (0, 0))
    in_specs = [_tile_spec(a, tm, tk, lambda i, k: i, lambda i, k: k),
                _tile_spec(w, D, tk, lambda i, k: 0, lambda i, k: k), row, vec]
    args = [a, w, h, g]
    if with_dh:
        in_specs.append(row)
        args.append(dres)
        out_specs = [row, row, vec]
        out_shape = [jax.ShapeDtypeStruct((T, D), F32), jax.ShapeDtypeStruct((T, D), BF16),
                     jax.ShapeDtypeStruct((1, D), F32)]
    else:
        out_specs = vec
        out_shape = jax.ShapeDtypeStruct((1, D), F32)
    if after is not None:
        in_specs.append(pl.BlockSpec(after.shape, lambda i, k: (0, 0)))
        args.append(after)
    return pl.pallas_call(
        body, name=name, grid=(T // tm, nk),
        in_specs=in_specs, out_specs=out_specs, out_shape=out_shape,
        scratch_shapes=[pltpu.VMEM((tm, D), F32)],
        compiler_params=_cparams(("arbitrary", "arbitrary")),
    )(*args)


def mm_tn(a, b, *, name, tt, tko, tn, relu2=False, out_slabs=None):
    T, K = _mat_shape(a)
    N = _mat_shape(b)[1]
    nt = T // tt
    o_shape = _out_struct(K, N, out_slabs, BF16)

    def body(a_ref, b_ref, o_ref, acc_ref):
        t = pl.program_id(2)
        av = a_ref[...]
        if relu2:
            av = jnp.maximum(av, 0.0)
            av = av * av
        part = _dot_tn(av.astype(BF16), b_ref[...].astype(BF16))

        @pl.when(t == 0)
        def _():
            acc_ref[...] = part

        @pl.when(t > 0)
        def _():
            acc_ref[...] += part

        @pl.when(t == nt - 1)
        def _():
            o_ref[...] = acc_ref[...].astype(BF16)

    return pl.pallas_call(
        body, name=name, grid=(K // tko, N // tn, nt),
        in_specs=[_tile_spec(a, tt, tko, lambda kk, j, t: t, lambda kk, j, t: kk),
                  _tile_spec(b, tt, tn, lambda kk, j, t: t, lambda kk, j, t: j)],
        out_specs=_tile_spec(o_shape, tko, tn, lambda kk, j, t: kk, lambda kk, j, t: j),
        out_shape=o_shape,
        scratch_shapes=[pltpu.VMEM((tko, tn), F32)],
        compiler_params=_cparams(("parallel", "parallel", "arbitrary")),
    )(a, b)


def _resident(a):
    nd = a.ndim
    return pl.BlockSpec(a.shape, lambda i: (0,) * nd, pipeline_mode=pl.Buffered(1))


def _row_block(a, tm):
    if a.ndim == 2:
        return pl.BlockSpec((tm, a.shape[1]), lambda i: (i, 0))
    return pl.BlockSpec((a.shape[0], tm, a.shape[2]), lambda i: (0, i, 0))


def _cols(ref, c, width):
    if len(ref.shape) == 2:
        return ref[:, c * width:(c + 1) * width]
    per = ref.shape[2] // width
    if per == 1:
        return ref[c]
    return ref[c // per, :, (c % per) * width:(c % per + 1) * width]


def _set_cols(ref, c, width, val):
    if len(ref.shape) == 2:
        ref[:, c * width:(c + 1) * width] = val
        return
    per = ref.shape[2] // width
    if per == 1:
        ref[c] = val
    else:
        ref[c // per, :, (c % per) * width:(c % per + 1) * width] = val


def _all_cols(ref):
    if len(ref.shape) == 2:
        return ref[...]
    return jnp.concatenate([ref[s] for s in range(ref.shape[0])], axis=1)


def _rms(x):
    return lax.rsqrt(jnp.mean(x * x, axis=-1, keepdims=True) + EPS)


def _row_params():
    return _cparams(("arbitrary",))


def proj_norm(h, g, w, *, name, tm, tn, out_dtype, out_slabs=None):
    T, D = h.shape
    N = _mat_shape(w)[1]
    o_shape = _out_struct(T, N, out_slabs, out_dtype)

    def body(h_ref, g_ref, w_ref, o_ref, n_ref):
        x = h_ref[...]
        n = (x * _rms(x) * g_ref[...]).astype(BF16)
        n_ref[...] = n
        for c in range(N // tn):
            _set_cols(o_ref, c, tn, _dot(n, _cols(w_ref, c, tn)).astype(out_dtype))

    return pl.pallas_call(
        body, name=name, grid=(T // tm,),
        in_specs=[_row_block(h, tm), pl.BlockSpec((1, D), lambda i: (0, 0)), _resident(w)],
        out_specs=[_row_block(o_shape, tm), pl.BlockSpec((tm, D), lambda i: (i, 0))],
        out_shape=[o_shape, jax.ShapeDtypeStruct((T, D), BF16)],
        compiler_params=_row_params(),
    )(h, g, w)


def prenorm(h, g, after, *, tm):
    T, D = h.shape

    def body(h_ref, g_ref, _after_ref, n_ref):
        x = h_ref[...]
        n_ref[...] = (x * _rms(x) * g_ref[...]).astype(BF16)

    row = pl.BlockSpec((tm, D), lambda i: (i, 0))
    return pl.pallas_call(
        body, name="prenorm", grid=(T // tm,),
        in_specs=[row, pl.BlockSpec((1, D), lambda i: (0, 0)), _anchor_spec(after)],
        out_specs=row, out_shape=jax.ShapeDtypeStruct((T, D), BF16),
        compiler_params=_row_params(),
    )(h, g, after)


def proj_plain(a, w, *, name, tm, tn, out_dtype=BF16, out_slabs=None, relu2=False):
    T = a.shape[0]
    N = _mat_shape(w)[1]

    def body(a_ref, w_ref, o_ref):
        av = a_ref[...]
        for c in range(N // tn):
            z = _dot(av, _cols(w_ref, c, tn))
            if relu2:
                z = jnp.maximum(z, 0.0)
                z = z * z
            _set_cols(o_ref, c, tn, z.astype(out_dtype))

    o_shape = _out_struct(T, N, out_slabs, out_dtype)
    return pl.pallas_call(
        body, name=name, grid=(T // tm,),
        in_specs=[_row_block(a, tm), _resident(w)],
        out_specs=_row_block(o_shape, tm), out_shape=o_shape,
        compiler_params=_row_params(),
    )(a, w)


def proj_res_norm(a, w, res, g, w_next=None, *, name, tm, tn):
    T = res.shape[0]
    D = w.shape[1]
    chained = w_next is not None

    def body(*refs):
        a_ref, w_ref, r_ref, g_ref = refs[:4]
        h_ref, n_ref = refs[4 + chained], refs[5 + chained]
        av = _all_cols(a_ref)
        for c in range(D // tn):
            sl = slice(c * tn, (c + 1) * tn)
            h_ref[:, sl] = r_ref[:, sl] + _dot(av, w_ref[:, sl])
        hv = h_ref[...]
        n = (hv * _rms(hv) * g_ref[...]).astype(BF16)
        n_ref[...] = n
        if chained:
            for c in range(D // tn):
                sl = slice(c * tn, (c + 1) * tn)
                refs[-1][:, sl] = _dot(n, refs[4][:, sl]).astype(BF16)

    row = pl.BlockSpec((tm, D), lambda i: (i, 0))
    half = jax.ShapeDtypeStruct((T, D), BF16)
    return pl.pallas_call(
        body, name=name, grid=(T // tm,),
        in_specs=[_row_block(a, tm), _resident(w), row, pl.BlockSpec((1, D), lambda i: (0, 0))]
        + ([_resident(w_next)] if chained else []),
        out_specs=[row, row] + ([row] if chained else []),
        out_shape=[jax.ShapeDtypeStruct((T, D), F32), half] + ([half] if chained else []),
        compiler_params=_row_params(),
    )(*([a, w, res, g] + ([w_next] if chained else [])))


def proj_res_loss(a, w, res, g, target, *, name, tm, tn):
    T = res.shape[0]
    D = w.shape[1]

    def body(a_ref, w_ref, r_ref, g_ref, t_ref, dh_ref, dhb_ref, ls_ref, dg_ref):
        i = pl.program_id(0)
        gv = g_ref[...]
        ls, dg = 0.0, 0.0
        halves = [slice(s * (tm // 2), (s + 1) * (tm // 2)) for s in range(2)]
        for rows in halves:
            av = a_ref[rows, :]
            for c in range(D // tn):
                sl = slice(c * tn, (c + 1) * tn)
                dh_ref[rows, sl] = r_ref[rows, sl] + _dot(av, w_ref[:, sl])
        for rows in halves:
            x = dh_ref[rows, :]
            r = _rms(x)
            xr = x * r
            d = xr * gv - t_ref[rows, :]
            dy = d * (1.0 / D)
            dyg = dy * gv
            dx = r * (dyg - xr * jnp.mean(dyg * xr, axis=-1, keepdims=True))
            dh_ref[rows, :] = dx
            dhb_ref[rows, :] = dx.astype(BF16)
            ls = ls + jnp.sum(d * d, axis=0, keepdims=True)
            dg = dg + jnp.sum(dy * xr, axis=0, keepdims=True)

        @pl.when(i == 0)
        def _():
            ls_ref[...] = ls
            dg_ref[...] = dg

        @pl.when(i > 0)
        def _():
            ls_ref[...] += ls
            dg_ref[...] += dg

    row = pl.BlockSpec((tm, D), lambda i: (i, 0))
    vec = pl.BlockSpec((1, D), lambda i: (0, 0))
    return pl.pallas_call(
        body, name=name, grid=(T // tm,),
        in_specs=[_row_block(a, tm), _resident(w), row, vec, row],
        out_specs=[row, row, vec, vec],
        out_shape=[jax.ShapeDtypeStruct((T, D), F32), jax.ShapeDtypeStruct((T, D), BF16),
                   jax.ShapeDtypeStruct((1, D), F32), jax.ShapeDtypeStruct((1, D), F32)],
        compiler_params=_row_params(),
    )(a, w, res, g, target)


def _anchor_spec(after):
    return pl.BlockSpec(after.shape, lambda i: (0, 0))


def back_plain(a, w, *, name, tm, tn, out_dtype, out_slabs=None, relu2_value=None, after=None):
    T = a.shape[0]
    N = w.shape[0]
    has_z = relu2_value is not None
    o_shape = _out_struct(T, N, out_slabs, out_dtype)

    def body(*refs):
        a_ref, w_ref = refs[0], refs[1]
        o_ref = refs[-1]
        av = a_ref[...]
        for c in range(N // tn):
            out = _dot_nt(av, w_ref[c * tn:(c + 1) * tn, :])
            if has_z:
                out = out * (2.0 * jnp.sqrt(refs[2][:, c * tn:(c + 1) * tn]).astype(F32))
            _set_cols(o_ref, c, tn, out.astype(out_dtype))

    in_specs, args = [_row_block(a, tm), _resident(w)], [a, w]
    if has_z:
        in_specs.append(_row_block(relu2_value, tm))
        args.append(relu2_value)
    if after is not None:
        in_specs.append(_anchor_spec(after))
        args.append(after)
    return pl.pallas_call(
        body, name=name, grid=(T // tm,),
        in_specs=in_specs, out_specs=_row_block(o_shape, tm), out_shape=o_shape,
        compiler_params=_row_params(),
    )(*args)


def back_norm(a, w, h, g, dres, *, name, tm, tk, bf16_copy=True, w_next=None, next_dtype=BF16, next_slabs=None,
              after=None):
    T, K = _mat_shape(a)
    D = h.shape[1]
    with_dh = dres is not None
    chained = w_next is not None
    n_in = 4 + with_dh + chained
    tn = 4 * HEAD_W

    def body(*refs):
        a_ref, w_ref, h_ref, g_ref = refs[:4]
        outs = refs[n_in + (after is not None):]
        i = pl.program_id(0)
        dn = None
        for kc in range(K // tk):
            part = _dot_nt(_cols(a_ref, kc, tk).astype(BF16), _cols(w_ref, kc, tk))
            dn = part if dn is None else dn + part
        x = h_ref[...]
        r = _rms(x)
        xr = x * r
        dgp = jnp.sum(dn * xr, axis=0, keepdims=True)
        dg_ref = outs[-1]

        @pl.when(i == 0)
        def _():
            dg_ref[...] = dgp

        @pl.when(i > 0)
        def _():
            dg_ref[...] += dgp

        if with_dh:
            dyg = dn * g_ref[...]
            out = refs[4][...] + r * (dyg - xr * jnp.mean(dyg * xr, axis=-1, keepdims=True))
            outs[0][...] = out
            outb = out.astype(BF16)
            if bf16_copy:
                outs[1][...] = outb
            if chained:
                wn_ref, nx_ref = refs[5], outs[-2]
                for c in range(wn_ref.shape[0] // tn):
                    _set_cols(nx_ref, c, tn, _dot_nt(outb, wn_ref[c * tn:(c + 1) * tn, :]).astype(next_dtype))

    row = pl.BlockSpec((tm, D), lambda i: (i, 0))
    vec = pl.BlockSpec((1, D), lambda i: (0, 0))
    in_specs, args = [_row_block(a, tm), _resident(w), row, vec], [a, w, h, g]
    out_specs, out_shape = [], []
    if with_dh:
        in_specs.append(row)
        args.append(dres)
        out_specs.append(row)
        out_shape.append(jax.ShapeDtypeStruct((T, D), F32))
        if bf16_copy:
            out_specs.append(row)
            out_shape.append(jax.ShapeDtypeStruct((T, D), BF16))
    if chained:
        in_specs.append(_resident(w_next))
        args.append(w_next)
        nx_shape = _out_struct(T, w_next.shape[0], next_slabs, next_dtype)
        out_specs.append(_row_block(nx_shape, tm))
        out_shape.append(nx_shape)
    out_specs.append(vec)
    out_shape.append(jax.ShapeDtypeStruct((1, D), F32))
    if after is not None:
        in_specs.append(_anchor_spec(after))
        args.append(after)
    outs = pl.pallas_call(
        body, name=name, grid=(T // tm,),
        in_specs=in_specs, out_specs=out_specs, out_shape=out_shape,
        compiler_params=_row_params(),
    )(*args)
    return outs if len(outs) > 1 else outs[0]


def wgrad(a, b, *, name, tt, tn, out_slabs=None):
    T, K = _mat_shape(a)
    N = _mat_shape(b)[1]
    nt = T // tt
    o_shape = _out_struct(K, N, out_slabs, BF16)

    flipped = K > N and out_slabs is None

    def body(a_ref, b_ref, o_ref, acc_ref):
        t = pl.program_id(0)

        @pl.when(t == 0)
        def _():
            acc_ref[...] = jnp.zeros_like(acc_ref)

        if flipped:
            bt = _all_cols(b_ref).astype(BF16).T
            for c in range(K // tn):
                acc_ref[:, c * tn:(c + 1) * tn] += _dot(bt, _cols(a_ref, c, tn).astype(BF16))
        else:
            at = _all_cols(a_ref).astype(BF16).T
            for c in range(N // tn):
                acc_ref[:, c * tn:(c + 1) * tn] += _dot(at, _cols(b_ref, c, tn).astype(BF16))

        @pl.when(t == nt - 1)
        def _():
            if flipped:
                for c in range(K // tn):
                    o_ref[c * tn:(c + 1) * tn, :] = acc_ref[:, c * tn:(c + 1) * tn].T.astype(BF16)
            else:
                for c in range(N // tn):
                    _set_cols(o_ref, c, tn, acc_ref[:, c * tn:(c + 1) * tn].astype(BF16))

    return pl.pallas_call(
        body, name=name, grid=(nt,),
        in_specs=[_row_block(a, tt), _row_block(b, tt)],
        out_specs=_resident(o_shape), out_shape=o_shape,
        scratch_shapes=[pltpu.VMEM((N, K) if flipped else (K, N), F32)],
        compiler_params=_row_params(),
    )(a, b)


def chunk_triangles(tm):
    r = lax.broadcasted_iota(jnp.int32, (tm, tm), 0)
    c = lax.broadcasted_iota(jnp.int32, (tm, tm), 1)
    same = (r // CHUNK) == (c // CHUNK)
    tri = jnp.stack([same & (c <= r), same & (c >= r)]).astype(F32)
    return tri.astype(BF16), tri


def _tri_spec(tm):
    return pl.BlockSpec((2, tm, tm), lambda g, s, i: (0, 0, 0))


def _chunk_row(v, r, nc):
    return jnp.concatenate([jnp.broadcast_to(v[c * CHUNK + r:c * CHUNK + r + 1], (CHUNK, v.shape[1]))
                            for c in range(nc)], axis=0)


def _block_diag(v, nc):
    chunk = lax.broadcasted_iota(jnp.int32, (v.shape[0], 1), 0) // CHUNK
    return jnp.concatenate([jnp.where(chunk == c, v, jnp.zeros_like(v)) for c in range(nc)], axis=1)


def _pool_windows_back(ext_ref, tm):
    n = tm + 32
    ext_ref[1, 8:n] = ext_ref[0, 8:n] + ext_ref[0, 7:n - 1]
    ext_ref[2, 16:n] = ext_ref[1, 16:n] + ext_ref[1, 14:n - 2]
    ext_ref[3, 24:n] = ext_ref[2, 24:n] + ext_ref[2, 20:n - 4]
    s2 = ext_ref[1, 32:n]
    s4 = ext_ref[2, 32:n]
    s8 = ext_ref[3, 32:n]
    s16 = s8 + ext_ref[3, 24:n - 8]
    return s2, s4, s8, s16


def _pool_windows_fwd(ext_ref, tm):
    n = tm + 32
    ext_ref[1, 0:n - 8] = ext_ref[0, 0:n - 8] + ext_ref[0, 1:n - 7]
    ext_ref[2, 0:n - 16] = ext_ref[1, 0:n - 16] + ext_ref[1, 2:n - 14]
    ext_ref[3, 0:n - 24] = ext_ref[2, 0:n - 24] + ext_ref[2, 4:n - 20]
    s2 = ext_ref[1, 0:tm]
    s4 = ext_ref[2, 0:tm]
    s8 = ext_ref[3, 0:tm]
    s16 = s8 + ext_ref[3, 8:tm + 8]
    return s2, s4, s8, s16


def _select_window(g, s2, s4, s8, s16):
    return jnp.where(g == 0, s2, jnp.where(g == 1, s4, jnp.where(g == 2, s8, s16)))


def _pool_count(g, pos):
    width = lax.shift_left(jnp.int32(2), g)
    return jnp.minimum(pos + 1, width).astype(F32)


def _hgrn_gates(zq, zf, th):
    lb = _sigmoid(th[0:1, :] - th[1:2, :])
    sig = _sigmoid(zf)
    f = lb + (1.0 - lb) * sig
    sq = _sigmoid(zq)
    return lb, sig, f, sq


def mixer_fwd(u5, pool_w_bf, scale4, theta4, gn4, tri_bf, tri_f, *, seqs, seq_len, tm):
    T = u5.shape[1]
    tps = seq_len // tm
    nc = tm // CHUNK
    W = HEAD_W

    H = HEADS_PER_STEP
    heads = range(H)

    def body(u_ref, pw_ref, sc_ref, th_ref, gn_ref, tri_ref, msk_ref, y_ref, o_ref, st_ref, halo_ref, ext_ref, s_ref):
        g = pl.program_id(0)
        i = pl.program_id(2)

        @pl.when(i == 0)
        def _():
            halo_ref[...] = jnp.zeros_like(halo_ref)
            s_ref[...] = jnp.zeros_like(s_ref)

        row = lax.broadcasted_iota(jnp.int32, (tm, 1), 0)
        cols = [slice(h * W, (h + 1) * W) for h in heads]

        pooled = []
        for h in heads:
            grp = g * H + h
            up = u_ref[0, :, cols[h]]
            ext_ref[h, 0, 0:16] = jnp.zeros((16, W), F32)
            ext_ref[h, 0, 16:32] = halo_ref[h]
            ext_ref[h, 0, 32:32 + tm] = up
            win = _select_window(grp, *_pool_windows_back(ext_ref.at[h], tm))
            pooled.append((win * (1.0 / _pool_count(grp, i * tm + row)) - up).astype(BF16))
            halo_ref[h] = up[tm - POOL_HALO:tm]
        mixed = [_dot(pooled[h], pw_ref[h]) for h in heads]
        for h in heads:
            y_ref[0, :, cols[h]] = (mixed[h] * sc_ref[h]).astype(BF16)

        zq, zf, zi, zg = u_ref[1], u_ref[2], u_ref[3], u_ref[4]
        th = [th_ref[h] for h in heads]
        lb = jnp.concatenate([_sigmoid(t[0:1, :] - t[1:2, :]) for t in th], axis=1)
        f = lb + (1.0 - lb) * _sigmoid(zf)
        kk = 1.0 - f
        q = zq * _sigmoid(zq)
        G = _tri_apply(tri_ref[0], jnp.log(f))
        Gm, Gl = _chunk_row(G, CHUNK // 2 - 1, nc), _chunk_row(G, CHUNK - 1, nc)
        vb = zi.astype(BF16)
        qrb = (q * jnp.exp(G - Gm)).astype(BF16)
        krb = (kk * jnp.exp(Gm - G)).astype(BF16)
        keb = (kk * jnp.exp(Gl - G)).astype(BF16)
        qgb = (q * jnp.exp(G)).astype(BF16)
        mask = msk_ref[0] > 0.5
        a = [jnp.where(mask, _dot_nt(qrb[:, cols[h]], krb[:, cols[h]]), 0.0).astype(BF16) for h in heads]
        d_st = [_dot_tn(vb[:, cols[h]], _block_diag(keb[:, cols[h]], nc)) for h in heads]
        o_intra = [_dot(a[h], vb[:, cols[h]]) for h in heads]
        st_cat = []
        for h in heads:
            st = s_ref[h]
            states = []
            for c in range(nc):
                st_ref[c, h] = st
                states.append(st.astype(BF16))
                st = st * jnp.exp(G[(c + 1) * CHUNK - 1:(c + 1) * CHUNK, cols[h]]) + d_st[h][:, c * W:(c + 1) * W]
            s_ref[h] = st
            st_cat.append(jnp.concatenate(states, axis=1))
        o = [o_intra[h] + _dot_nt(_block_diag(qgb[:, cols[h]], nc), st_cat[h]) for h in heads]
        gate = zg * _sigmoid(zg)
        for h in heads:
            o_ref[:, cols[h]] = o[h]
            r = lax.rsqrt(jnp.mean(o[h] * o[h], axis=-1, keepdims=True) + EPS)
            y_ref[1, :, cols[h]] = (o[h] * r * gn_ref[h] * gate[:, cols[h]]).astype(BF16)

    def rb(s, i):
        return s * tps + i

    def per_head(*shape):
        return pl.BlockSpec((H,) + shape, lambda g, s, i: (g,) + (0,) * len(shape))

    return pl.pallas_call(
        body, name="mixer_fwd", grid=(4 // H, seqs, tps),
        in_specs=[pl.BlockSpec((5, tm, H * W), lambda g, s, i: (0, rb(s, i), g)),
                  per_head(W, W), per_head(1, W), per_head(2, W), per_head(1, W),
                  _tri_spec(tm), _tri_spec(tm)],
        out_specs=[pl.BlockSpec((2, tm, H * W), lambda g, s, i: (0, rb(s, i), g)),
                   pl.BlockSpec((tm, H * W), lambda g, s, i: (rb(s, i), g)),
                   pl.BlockSpec((nc, H, W, W), lambda g, s, i: (rb(s, i), g, 0, 0))],
        out_shape=[jax.ShapeDtypeStruct((2, T, 4 * W), BF16),
                   jax.ShapeDtypeStruct((T, 4 * W), F32),
                   jax.ShapeDtypeStruct((T // CHUNK, 4, W, W), F32)],
        scratch_shapes=[pltpu.VMEM((H, POOL_HALO, W), F32),
                        pltpu.VMEM((H, 4, tm + 32, W), F32),
                        pltpu.VMEM((H, W, W), F32)],
        compiler_params=_cparams(("arbitrary", "arbitrary", "arbitrary")),
    )(u5, pool_w_bf, scale4, theta4, gn4, tri_bf, tri_f)


def mixer_bwd(u5, dy2, o_pre, st_prev, pool_w_bf, scale4, theta4, gn4, tri_bf, tri_f, after, *, seqs, seq_len, tm):
    T = u5.shape[1]
    tps = seq_len // tm
    nc = tm // CHUNK
    W = HEAD_W
    hb = tm // POOL_HALO

    H = HEADS_PER_STEP
    heads = range(H)

    def body(u_ref, uh_ref, dy_ref, o_ref, st_ref, pw_ref, sc_ref, th_ref, gn_ref, tri_ref, msk_ref, _after_ref,
             du_ref, dpw_ref, dsc_ref, dlb_ref, dgn_ref, nxt_ref, ext_ref, ds_ref):
        g = pl.program_id(0)
        s = pl.program_id(1)
        i = pl.program_id(2)
        tile = tps - 1 - i
        first = (s == 0) & (i == 0)

        @pl.when(i == 0)
        def _():
            nxt_ref[...] = jnp.zeros_like(nxt_ref)
            ds_ref[...] = jnp.zeros_like(ds_ref)

        row = lax.broadcasted_iota(jnp.int32, (tm, 1), 0)
        cols = [slice(h * W, (h + 1) * W) for h in heads]

        def accumulate(ref, h, val):
            @pl.when(first)
            def _():
                ref[h] = val

            @pl.when(jnp.logical_not(first))
            def _():
                ref[h] += val

        def per_head(fn):
            return jnp.concatenate([jnp.broadcast_to(fn(cols[h]), (tm, W)) for h in heads], axis=1)

        inv_cnt, pb, dz = [], [], []
        for h in heads:
            grp = g * H + h
            inv_cnt.append(1.0 / _pool_count(grp, tile * tm + row))
            ext = ext_ref.at[h]
            up = u_ref[0, :, cols[h]]
            ext[0, 0:16] = jnp.zeros((16, W), F32)
            ext[0, 16:32] = jnp.where(tile == 0, 0.0, uh_ref[:, cols[h]])
            ext[0, 32:32 + tm] = up
            win = _select_window(grp, *_pool_windows_back(ext, tm))
            pb.append((win * inv_cnt[h] - up).astype(BF16))
            dz.append((dy_ref[0, :, cols[h]] * sc_ref[h]).astype(BF16))
        z = [_dot(pb[h], pw_ref[h]) for h in heads]
        dp = [_dot_nt(dz[h], pw_ref[h]) for h in heads]
        dpw = [_dot_tn(pb[h], dz[h]) for h in heads]
        for h in heads:
            accumulate(dsc_ref, h, jnp.sum(dy_ref[0, :, cols[h]] * z[h], axis=0, keepdims=True))
            accumulate(dpw_ref, h, dpw[h])
            ext = ext_ref.at[h]
            e = dp[h] * inv_cnt[h]
            ext[0, 0:tm] = e
            ext[0, tm:tm + 16] = nxt_ref[h]
            ext[0, tm + 16:tm + 32] = jnp.zeros((16, W), F32)
            lead = _select_window(g * H + h, *_pool_windows_fwd(ext, tm))
            nxt_ref[h] = e[0:POOL_HALO]
            du_ref[0, :, cols[h]] = (lead - dp[h]).astype(BF16)

        zq, zf, zi, zg = u_ref[1], u_ref[2], u_ref[3], u_ref[4]
        lb = jnp.concatenate([_sigmoid(th_ref[h][0:1, :] - th_ref[h][1:2, :]) for h in heads], axis=1)
        gn = jnp.concatenate([gn_ref[h] for h in heads], axis=1)
        sig, sq, sg = _sigmoid(zf), _sigmoid(zq), _sigmoid(zg)
        f = lb + (1.0 - lb) * sig
        kk = 1.0 - f
        q = zq * sq
        G = _tri_apply(tri_ref[0], jnp.log(f))

        dyh = dy_ref[1]
        o = o_ref[...]
        sqr = o * o
        r = per_head(lambda cs: lax.rsqrt(jnp.mean(sqr[:, cs], axis=-1, keepdims=True) + EPS))
        orr = o * r
        du_ref[4] = (dyh * (orr * gn) * (sg * (1.0 + zg * (1.0 - sg)))).astype(BF16)
        don = dyh * (zg * sg)
        dgn = jnp.sum(don * orr, axis=0, keepdims=True)
        dog = don * gn
        dog_orr = dog * orr
        do = r * (dog - orr * per_head(lambda cs: jnp.mean(dog_orr[:, cs], axis=-1, keepdims=True)))

        Gm, Gl = _chunk_row(G, CHUNK // 2 - 1, nc), _chunk_row(G, CHUNK - 1, nc)
        e_q, e_k, e_e, e_g = jnp.exp(G - Gm), jnp.exp(Gm - G), jnp.exp(Gl - G), jnp.exp(G)
        qr, kr, ke, qg = q * e_q, kk * e_k, kk * e_e, q * e_g
        qrb, krb, keb, qgb = qr.astype(BF16), kr.astype(BF16), ke.astype(BF16), qg.astype(BF16)
        vb = zi.astype(BF16)
        dob = do.astype(BF16)
        lower, upper = msk_ref[0] > 0.5, msk_ref[1] > 0.5
        da = [jnp.where(lower, _dot_nt(dob[:, cs], vb[:, cs]), 0.0).astype(BF16) for cs in cols]
        a_t = [jnp.where(upper, _dot_nt(krb[:, cs], qrb[:, cs]), 0.0).astype(BF16) for cs in cols]
        da_t = [jnp.where(upper, _dot_nt(vb[:, cs], dob[:, cs]), 0.0).astype(BF16) for cs in cols]
        u_cat = [_dot_tn(dob[:, cs], _block_diag(qgb[:, cs], nc)) for cs in cols]
        dqr = [_dot(da[h], krb[:, cols[h]]) for h in heads]
        dkr = [_dot(da_t[h], qrb[:, cols[h]]) for h in heads]
        dv = [_dot(a_t[h], dob[:, cols[h]]) for h in heads]
        dsn_rows, dsn_cols, ddecay = [], [], [[None] * H for _ in range(nc)]
        for h in heads:
            dsn = ds_ref[h]
            dsn_b = [None] * nc
            for c in reversed(range(nc)):
                decay = jnp.exp(G[(c + 1) * CHUNK - 1:(c + 1) * CHUNK, cols[h]])
                dsn_b[c] = dsn.astype(BF16)
                ddecay[c][h] = jnp.sum(dsn * st_ref[c, h], axis=0, keepdims=True) * decay
                dsn = u_cat[h][:, c * W:(c + 1) * W] + dsn * decay
            ds_ref[h] = dsn
            dsn_rows.append(jnp.concatenate(dsn_b, axis=0))
            dsn_cols.append(jnp.concatenate(dsn_b, axis=1))
        st_rows = [jnp.concatenate([st_ref[c, h].astype(BF16) for c in range(nc)], axis=0) for h in heads]
        dqg = [_dot(_block_diag(dob[:, cols[h]], nc), st_rows[h]) for h in heads]
        dke = [_dot(_block_diag(vb[:, cols[h]], nc), dsn_rows[h]) for h in heads]
        dv = [dv[h] + _dot_nt(_block_diag(keb[:, cols[h]], nc), dsn_cols[h]) for h in heads]
        dqr, dkr, dqg, dke, dv = (jnp.concatenate(parts, axis=1) for parts in (dqr, dkr, dqg, dke, dv))
        t_mid, t_qg, t_ke = dkr * kr - dqr * qr, dqg * qg, dke * ke
        dq = dqr * e_q + dqg * e_g
        dk = dkr * e_k + dke * e_e
        crow = lax.broadcasted_iota(jnp.int32, (CHUNK, 1), 0)
        ends = []
        for c in range(nc):
            sl = slice(c * CHUNK, (c + 1) * CHUNK)
            dgm = jnp.sum(t_mid[sl], axis=0, keepdims=True)
            dgl = jnp.sum(t_ke[sl], axis=0, keepdims=True) + jnp.concatenate(ddecay[c], axis=1)
            ends.append(jnp.where(crow == CHUNK // 2 - 1, dgm, 0.0) + jnp.where(crow == CHUNK - 1, dgl, 0.0))
        dG = t_qg - t_ke - t_mid + jnp.concatenate(ends, axis=0)
        dlogf = _tri_apply(tri_ref[1], dG)
        df = dlogf / f - dk
        du_ref[1] = (dq * (sq * (1.0 + zq * (1.0 - sq)))).astype(BF16)
        du_ref[2] = (df * (1.0 - lb) * (sig * (1.0 - sig))).astype(BF16)
        du_ref[3] = dv.astype(BF16)
        dlb = jnp.sum(df * (1.0 - sig), axis=0, keepdims=True) * (lb * (1.0 - lb))
        for h in heads:
            accumulate(dgn_ref, h, dgn[:, cols[h]])
            accumulate(dlb_ref, h, dlb[:, cols[h]])

    def rb(s, i):
        return s * tps + (tps - 1 - i)

    def per_head_spec(*shape):
        return pl.BlockSpec((H,) + shape, lambda g, s, i: (g,) + (0,) * len(shape))

    vec, mat = per_head_spec(1, W), per_head_spec(W, W)
    return pl.pallas_call(
        body, name="mixer_bwd", grid=(4 // H, seqs, tps),
        in_specs=[pl.BlockSpec((5, tm, H * W), lambda g, s, i: (0, rb(s, i), g)),
                  pl.BlockSpec((None, POOL_HALO, H * W), lambda g, s, i: (0, jnp.maximum(rb(s, i) * hb - 1, 0), g)),
                  pl.BlockSpec((2, tm, H * W), lambda g, s, i: (0, rb(s, i), g)),
                  pl.BlockSpec((tm, H * W), lambda g, s, i: (rb(s, i), g)),
                  pl.BlockSpec((nc, H, W, W), lambda g, s, i: (rb(s, i), g, 0, 0)),
                  mat, vec, per_head_spec(2, W), vec, _tri_spec(tm), _tri_spec(tm),
                  pl.BlockSpec(after.shape, lambda g, s, i: (0, 0))],
        out_specs=[pl.BlockSpec((5, tm, H * W), lambda g, s, i: (0, rb(s, i), g)), mat, vec, vec, vec],
        out_shape=[jax.ShapeDtypeStruct((5, T, 4 * W), BF16),
                   jax.ShapeDtypeStruct((4, W, W), F32),
                   jax.ShapeDtypeStruct((4, 1, W), F32),
                   jax.ShapeDtypeStruct((4, 1, W), F32),
                   jax.ShapeDtypeStruct((4, 1, W), F32)],
        scratch_shapes=[pltpu.VMEM((H, POOL_HALO, W), F32),
                        pltpu.VMEM((H, 4, tm + 32, W), F32),
                        pltpu.VMEM((H, W, W), F32)],
        compiler_params=_cparams(("arbitrary", "arbitrary", "arbitrary")),
    )(u5, u5, dy2, o_pre, st_prev, pool_w_bf, scale4, theta4, gn4, tri_bf, tri_f, after)


def _attn_probs(q, k, hd):
    s = _dot_nt(q, k) * (1.0 / (hd ** 0.5))
    e = jnp.exp(s - jnp.max(s, axis=-1, keepdims=True))
    return e * (1.0 / jnp.sum(e, axis=-1, keepdims=True))


def attn_fwd(q, kv3, *, seqs, seq_len, n_mem, tm):
    T, D = q.shape
    hd = D // XATTN_HEADS
    tps = seq_len // tm

    cols = [slice(h * hd, (h + 1) * hd) for h in range(XATTN_HEADS)]

    def body(q_ref, kv_ref, o_ref):
        p = [_attn_probs(q_ref[:, cs], kv_ref[0, :, cs], hd) for cs in cols]
        for h, cs in enumerate(cols):
            o_ref[:, cs] = _dot(p[h].astype(BF16), kv_ref[1, :, cs]).astype(BF16)

    return pl.pallas_call(
        body, name="attn_fwd", grid=(seqs, tps),
        in_specs=[pl.BlockSpec((tm, D), lambda b, i: (b * tps + i, 0)),
                  pl.BlockSpec((2, n_mem, D), lambda b, i: (0, b, 0))],
        out_specs=pl.BlockSpec((tm, D), lambda b, i: (b * tps + i, 0)),
        out_shape=jax.ShapeDtypeStruct((T, D), BF16),
        compiler_params=_cparams(("parallel", "arbitrary")),
    )(q, kv3)


def attn_bwd(q, kv3, do, *, seqs, seq_len, n_mem, tm):
    T, D = q.shape
    hd = D // XATTN_HEADS
    tps = seq_len // tm

    cols = [slice(h * hd, (h + 1) * hd) for h in range(XATTN_HEADS)]

    def body(q_ref, kv_ref, do_ref, dq_ref, dkv_ref):
        i = pl.program_id(1)

        @pl.when(i == 0)
        def _():
            dkv_ref[...] = jnp.zeros_like(dkv_ref)

        p = [_attn_probs(q_ref[:, cs], kv_ref[0, :, cs], hd) for cs in cols]
        dp = [_dot_nt(do_ref[:, cs], kv_ref[1, :, cs]) for cs in cols]
        ds = [(p[h] * (dp[h] - jnp.sum(dp[h] * p[h], axis=-1, keepdims=True)) * (1.0 / (hd ** 0.5))).astype(BF16)
              for h in range(XATTN_HEADS)]
        for h, cs in enumerate(cols):
            dq_ref[:, cs] = _dot(ds[h], kv_ref[0, :, cs]).astype(BF16)
            dkv_ref[0, :, cs] += _dot_tn(ds[h], q_ref[:, cs])
            dkv_ref[1, :, cs] += _dot_tn(p[h].astype(BF16), do_ref[:, cs])

    qspec = pl.BlockSpec((tm, D), lambda b, i: (b * tps + i, 0))
    kvspec = pl.BlockSpec((2, n_mem, D), lambda b, i: (0, b, 0))
    return pl.pallas_call(
        body, name="attn_bwd", grid=(seqs, tps),
        in_specs=[qspec, kvspec, qspec],
        out_specs=[qspec, kvspec],
        out_shape=[jax.ShapeDtypeStruct((T, D), BF16), jax.ShapeDtypeStruct((2, seqs * n_mem, D), F32)],
        compiler_params=_cparams(("parallel", "arbitrary")),
    )(q, kv3, do)


def final_loss(h, g, target, *, tm):
    T, D = h.shape

    def body(h_ref, g_ref, t_ref, dh_ref, dhb_ref, ls_ref, dg_ref):
        i = pl.program_id(0)
        x = h_ref[...]
        gv = g_ref[...]
        r = lax.rsqrt(jnp.mean(x * x, axis=-1, keepdims=True) + EPS)
        xr = x * r
        d = xr * gv - t_ref[...]
        dy = d * (1.0 / D)
        dyg = dy * gv
        dx = r * (dyg - xr * jnp.mean(dyg * xr, axis=-1, keepdims=True))
        dh_ref[...] = dx
        dhb_ref[...] = dx.astype(BF16)
        ls = jnp.sum(d * d, axis=0, keepdims=True)
        dg = jnp.sum(dy * xr, axis=0, keepdims=True)

        @pl.when(i == 0)
        def _():
            ls_ref[...] = ls
            dg_ref[...] = dg

        @pl.when(i > 0)
        def _():
            ls_ref[...] += ls
            dg_ref[...] += dg

    row = pl.BlockSpec((tm, D), lambda i: (i, 0))
    vec = pl.BlockSpec((1, D), lambda i: (0, 0))
    return pl.pallas_call(
        body, name="final_loss", grid=(T // tm,),
        in_specs=[row, vec, row], out_specs=[row, row, vec, vec],
        out_shape=[jax.ShapeDtypeStruct((T, D), F32), jax.ShapeDtypeStruct((T, D), BF16),
                   jax.ShapeDtypeStruct((1, D), F32), jax.ShapeDtypeStruct((1, D), F32)],
        compiler_params=_cparams(("arbitrary",)),
    )(h, g, target)


def _my_place():
    return lax.axis_index("x"), lax.axis_index("y"), lax.axis_index("c")


def _slot_of(px, py, pc):
    return 4 * px + 2 * py + pc


def _peer(k, x, y, c):
    return (1 - x if (k >> 2) & 1 else x, 1 - y if (k >> 1) & 1 else y, 1 - c if k & 1 else c)


def _split_copies(src_refs, land_refs, send_sems, recv_sems, scatter):
    x, y, c = _my_place()
    mine = _slot_of(x, y, c)
    copies = []
    for a, (src, land) in enumerate(zip(src_refs, land_refs)):
        for k in range(1, N_DEV):
            peer = _peer(k, x, y, c)
            copies.append(pltpu.make_async_remote_copy(
                src_ref=src.at[_slot_of(*peer)] if scatter else src, dst_ref=land.at[mine],
                send_sem=send_sems.at[a * N_PEERS + k - 1], recv_sem=recv_sems.at[a * N_PEERS + k - 1],
                device_id=peer, device_id_type=MESH))
    return copies


def split_start(groups, *, name, scatter):
    sizes = [len(srcs) for srcs, _ in groups]
    n_arr = sum(sizes)
    flat = [a for srcs, lands in groups for a in list(srcs) + list(lands)]

    def body(*refs):
        ins = refs[:2 * n_arr]
        sems = refs[4 * n_arr:4 * n_arr + 2 * len(groups)]
        token = refs[-1]
        at = 0
        for gi, n in enumerate(sizes):
            for cp in _split_copies(ins[at:at + n], ins[at + n:at + 2 * n], sems[2 * gi], sems[2 * gi + 1], scatter):
                cp.start()
            at += 2 * n
        token[...] = jnp.zeros_like(token)

    sem_shapes = []
    for n in sizes:
        sem_shapes += [pltpu.SemaphoreType.DMA((n * N_PEERS,))] * 2
    outs = pl.pallas_call(
        body, name=name,
        out_shape=tuple(pltpu.HBM(a.shape, a.dtype) for a in flat) + tuple(sem_shapes)
        + (jax.ShapeDtypeStruct((8, 128), F32),),
        in_specs=(HBM,) * len(flat),
        out_specs=(HBM,) * len(flat) + (SEM,) * len(sem_shapes) + (pl.BlockSpec(memory_space=pltpu.VMEM),),
        input_output_aliases={i: i for i in range(len(flat))},
        compiler_params=pltpu.CompilerParams(has_side_effects=pltpu.SideEffectType.DATAFLOW_SIDE_EFFECTING),
    )(*[pltpu.with_memory_space_constraint(a, pltpu.HBM) for a in flat])
    thru, sems, token = outs[:len(flat)], outs[len(flat):-1], outs[-1]
    started, at = [], 0
    for gi, n in enumerate(sizes):
        started.append((sems[2 * gi], sems[2 * gi + 1], thru[at:at + n], thru[at + n:at + 2 * n]))
        at += 2 * n
    return started, token


def split_wait(started, after, *, name, scatter):
    sizes = [len(g[2]) for g in started]
    n_arr = sum(sizes)
    flat = [a for g in started for a in list(g[2]) + list(g[3])]
    sems = [s for g in started for s in g[:2]]

    def body(*refs):
        ins = refs[:2 * n_arr]
        sem_refs = refs[2 * n_arr:2 * n_arr + len(sems)]
        at = 0
        for gi, n in enumerate(sizes):
            for cp in _split_copies(ins[at:at + n], ins[at + n:at + 2 * n], sem_refs[2 * gi], sem_refs[2 * gi + 1], scatter):
                cp.wait_send()
                cp.wait_recv()
            at += 2 * n

    outs = pl.pallas_call(
        body, name=name,
        out_shape=tuple(pltpu.HBM(a.shape, a.dtype) for a in flat),
        in_specs=(HBM,) * len(flat) + (SEM,) * len(sems) + (pl.BlockSpec(memory_space=pl.ANY),),
        out_specs=(HBM,) * len(flat),
        input_output_aliases={i: i for i in range(len(flat))},
        compiler_params=pltpu.CompilerParams(has_side_effects=pltpu.SideEffectType.DATAFLOW_SIDE_EFFECTING),
    )(*flat, *sems, after)
    done, at = [], 0
    for n in sizes:
        done.append((outs[at:at + n], outs[at + n:at + 2 * n]))
        at += 2 * n
    return done


SIBLING = 1
CHIP_PEERS = (2, 4, 6)
_SIDE_EFFECTS = pltpu.CompilerParams(has_side_effects=pltpu.SideEffectType.DATAFLOW_SIDE_EFFECTING)


def _chip_level_copies(src, land, send_sems, recv_sems):
    x, y, c = _my_place()
    return [pltpu.make_async_remote_copy(
        src_ref=src, dst_ref=land.at[_slot_of(x, y, c)], send_sem=send_sems.at[j], recv_sem=recv_sems.at[j],
        device_id=_peer(k, x, y, c), device_id_type=MESH) for j, k in enumerate((SIBLING,) + CHIP_PEERS)]


def _pass_on_copies(land, send_sems, recv_sems, receiving):
    x, y, c = _my_place()
    copies = []
    for j, k in enumerate(CHIP_PEERS):
        slot = _slot_of(*_peer(k ^ SIBLING if receiving else k, x, y, c))
        copies.append(pltpu.make_async_remote_copy(
            src_ref=land.at[slot], dst_ref=land.at[slot], send_sem=send_sems.at[j], recv_sem=recv_sems.at[j],
            device_id=_peer(SIBLING, x, y, c), device_id_type=MESH))
    return copies


def gather2_start(src, land, *, name):
    def body(src_ref, land_ref, src_out, land_out, send_sems, recv_sems, token):
        for cp in _chip_level_copies(src_ref, land_ref, send_sems, recv_sems):
            cp.start()
        token[...] = jnp.zeros_like(token)

    n = 1 + len(CHIP_PEERS)
    src_t, land_t, send_sems, recv_sems, token = pl.pallas_call(
        body, name=name,
        out_shape=(pltpu.HBM(src.shape, src.dtype), pltpu.HBM(land.shape, land.dtype),
                   pltpu.SemaphoreType.DMA((n,)), pltpu.SemaphoreType.DMA((n,)), jax.ShapeDtypeStruct((8, 128), F32)),
        in_specs=(HBM, HBM), out_specs=(HBM, HBM, SEM, SEM, pl.BlockSpec(memory_space=pltpu.VMEM)),
        input_output_aliases={0: 0, 1: 1}, compiler_params=_SIDE_EFFECTS,
    )(pltpu.with_memory_space_constraint(src, pltpu.HBM), pltpu.with_memory_space_constraint(land, pltpu.HBM))
    return (src_t, land_t, send_sems, recv_sems), token


def gather2_pass_on(started, after, *, name):
    src, land, send_a, recv_a = started

    def body(src_ref, land_ref, send_a_ref, recv_a_ref, after_ref, land_out, send_b, recv_b):
        for cp in _chip_level_copies(src_ref, land_ref, send_a_ref, recv_a_ref):
            cp.wait_send()
            cp.wait_recv()
        for cp in _pass_on_copies(land_ref, send_b, recv_b, False):
            cp.start()

    n = len(CHIP_PEERS)
    land_t, send_b, recv_b = pl.pallas_call(
        body, name=name,
        out_shape=(pltpu.HBM(land.shape, land.dtype), pltpu.SemaphoreType.DMA((n,)), pltpu.SemaphoreType.DMA((n,))),
        in_specs=(HBM, HBM, SEM, SEM, pl.BlockSpec(memory_space=pl.ANY)), out_specs=(HBM, SEM, SEM),
        input_output_aliases={1: 0}, compiler_params=_SIDE_EFFECTS,
    )(src, land, send_a, recv_a, after)
    return land_t, send_b, recv_b


def gather2_wait(passed, *, name):
    land, send_b, recv_b = passed

    def body(land_ref, send_ref, recv_ref, land_out):
        for cp in _pass_on_copies(land_ref, send_ref, recv_ref, False):
            cp.wait_send()
        for cp in _pass_on_copies(land_ref, send_ref, recv_ref, True):
            cp.wait_recv()

    return pl.pallas_call(
        body, name=name, out_shape=pltpu.HBM(land.shape, land.dtype),
        in_specs=(HBM, SEM, SEM), out_specs=HBM,
        input_output_aliases={0: 0}, compiler_params=_SIDE_EFFECTS,
    )(land, send_b, recv_b)


def allgather_small(bufs):
    n = len(bufs)

    def body(*refs):
        srcs, outs = refs[:n], refs[n:2 * n]
        send_sems, recv_sems, local_sems = refs[2 * n:]
        x, y, c = _my_place()
        mine = _slot_of(x, y, c)
        local = [pltpu.make_async_copy(s, o.at[mine], local_sems.at[a]) for a, (s, o) in enumerate(zip(srcs, outs))]
        for cp in local:
            cp.start()
        copies = _split_copies(srcs, outs, send_sems, recv_sems, False)
        for cp in copies:
            cp.start()
        for cp in copies:
            cp.wait()
        for cp in local:
            cp.wait()

    return pl.pallas_call(
        body, name="allgather_small",
        out_shape=[jax.ShapeDtypeStruct((N_DEV,) + b.shape, b.dtype) for b in bufs],
        in_specs=[HBM] * n, out_specs=[HBM] * n,
        scratch_shapes=[pltpu.SemaphoreType.DMA((n * N_PEERS,)), pltpu.SemaphoreType.DMA((n * N_PEERS,)),
                        pltpu.SemaphoreType.DMA((n,))],
    )(*bufs)


def _adamw_math(g, w, m, v):
    c1 = 1.0 - ADAM_B1 ** ADAM_STEP
    c2 = 1.0 - ADAM_B2 ** ADAM_STEP
    nm = ADAM_B1 * m + (1.0 - ADAM_B1) * g
    nv = ADAM_B2 * v + (1.0 - ADAM_B2) * (g * g)
    delta = -ADAM_LR * ((nm / c1) / (jnp.sqrt(nv / c2) + ADAM_EPS) + ADAM_WD * w)
    return delta, nm, nv


def adamw_sharded(me, own, recv, w, m, v, *, name, tr):
    R, C = w.shape

    def body(me_ref, *refs):
        parts = refs[:N_DEV]
        w_ref, m_ref, v_ref, g_ref, d_ref, nm_ref, nv_ref = refs[N_DEV:]
        g = parts[0][...].astype(F32)
        for p in parts[1:]:
            g = g + p[...].astype(F32)
        g_ref[...] = g
        d_ref[...], nm_ref[...], nv_ref[...] = _adamw_math(g, w_ref[...], m_ref[...], v_ref[...])

    def slab(k):
        return pl.BlockSpec((None, tr, C), lambda i, me_ref: (me_ref[0] ^ k, i, 0))

    blk = pl.BlockSpec((tr, C), lambda i, me_ref: (i, 0))
    out = jax.ShapeDtypeStruct((R, C), F32)
    return pl.pallas_call(
        body, name=name,
        grid_spec=pltpu.PrefetchScalarGridSpec(
            num_scalar_prefetch=1, grid=(R // tr,),
            in_specs=[slab(k) for k in range(N_DEV)] + [blk, blk, blk],
            out_specs=[blk, blk, blk, blk]),
        out_shape=[out, out, out, out],
        compiler_params=_cparams(("parallel",)),
    )(me, own, *([recv] * N_PEERS), w, m, v)


def adamw_replicated(parts, ws, ms, vs, rows):
    n_buf, n_par = len(parts), len(ws)

    def body(*refs):
        p_refs = refs[:n_buf]
        w_refs = refs[n_buf:n_buf + n_par]
        m_refs = refs[n_buf + n_par:n_buf + 2 * n_par]
        v_refs = refs[n_buf + 2 * n_par:n_buf + 3 * n_par]
        outs = refs[n_buf + 3 * n_par:]
        sums = []
        for p in p_refs:
            g = p[0]
            for s in range(1, N_DEV):
                g = g + p[s]
            sums.append(g)
        for j, (b, r0, nr) in enumerate(rows):
            g = sums[b][r0:r0 + nr]
            delta, nm, nv = _adamw_math(g, w_refs[j][...], m_refs[j][...], v_refs[j][...])
            outs[j][...] = g
            outs[n_par + j][...] = delta
            outs[2 * n_par + j][...] = nm
            outs[3 * n_par + j][...] = nv

    shapes = [jax.ShapeDtypeStruct(w.shape, F32) for w in ws]
    outs = pl.pallas_call(
        body, name="adamw_replicated", out_shape=shapes * 4,
        compiler_params=pltpu.CompilerParams(vmem_limit_bytes=V7X_VMEM_LIMIT),
    )(*parts, *ws, *ms, *vs)
    return outs[:n_par], outs[n_par:2 * n_par], outs[2 * n_par:3 * n_par], outs[3 * n_par:]


BIG = ("w_in", "w_out", "xw_q", "xw_kv", "xw_o", "w_up", "w_down")
COL_SHARDED = ("w_in", "xw_kv", "w_up")
WEIGHTS = ("norm_mix", "w_in", "pool_w", "pool_scale", "lb_theta", "hgrn_norm", "w_out", "norm_xq",
           "norm_mem", "xw_q", "xw_kv", "xw_o", "norm_mlp", "w_up", "w_down", "norm_final")
SMALL = (("pool_w", (4 * HEAD_W, HEAD_W), 0, 0),
         ("norm_mix", (1, 1024), 1, 0), ("norm_xq", (1, 1024), 1, 1), ("norm_mem", (1, 1024), 1, 2),
         ("norm_mlp", (1, 1024), 1, 3), ("norm_final", (1, 1024), 1, 4),
         ("pool_scale", (1, 512), 2, 0), ("hgrn_norm", (1, 512), 2, 1), ("lb_theta", (2, 512), 2, 2))


def _pad_rows(a, rows):
    return jnp.concatenate([a, jnp.zeros((rows - a.shape[0], a.shape[1]), a.dtype)], axis=0)


def kernel(x, mem, norm_mix, w_in, pool_w, pool_scale, lb_theta, hgrn_norm, w_out, norm_xq, norm_mem, xw_q, xw_kv, xw_o, norm_mlp, w_up, w_down, norm_final, loss_target, m_norm_mix, m_w_in, m_pool_w, m_pool_scale, m_lb_theta, m_hgrn_norm, m_w_out, m_norm_xq, m_norm_mem, m_xw_q, m_xw_kv, m_xw_o, m_norm_mlp, m_w_up, m_w_down, m_norm_final, v_norm_mix, v_w_in, v_pool_w, v_pool_scale, v_lb_theta, v_hgrn_norm, v_w_out, v_norm_xq, v_norm_mem, v_xw_q, v_xw_kv, v_xw_o, v_norm_mlp, v_w_up, v_w_down, v_norm_final):
    w = dict(norm_mix=norm_mix, w_in=w_in, pool_w=pool_w, pool_scale=pool_scale, lb_theta=lb_theta,
             hgrn_norm=hgrn_norm, w_out=w_out, norm_xq=norm_xq, norm_mem=norm_mem, xw_q=xw_q, xw_kv=xw_kv,
             xw_o=xw_o, norm_mlp=norm_mlp, w_up=w_up, w_down=w_down, norm_final=norm_final)
    mom = dict(norm_mix=m_norm_mix, w_in=m_w_in, pool_w=m_pool_w, pool_scale=m_pool_scale, lb_theta=m_lb_theta,
               hgrn_norm=m_hgrn_norm, w_out=m_w_out, norm_xq=m_norm_xq, norm_mem=m_norm_mem, xw_q=m_xw_q,
               xw_kv=m_xw_kv, xw_o=m_xw_o, norm_mlp=m_norm_mlp, w_up=m_w_up, w_down=m_w_down,
               norm_final=m_norm_final)
    var = dict(norm_mix=v_norm_mix, w_in=v_w_in, pool_w=v_pool_w, pool_scale=v_pool_scale, lb_theta=v_lb_theta,
               hgrn_norm=v_hgrn_norm, w_out=v_w_out, norm_xq=v_norm_xq, norm_mem=v_norm_mem, xw_q=v_xw_q,
               xw_kv=v_xw_kv, xw_o=v_xw_o, norm_mlp=v_norm_mlp, w_up=v_w_up, w_down=v_w_down,
               norm_final=v_norm_final)

    seqs, seq_len, D = x.shape
    n_mem = mem.shape[1]
    T = seqs * seq_len
    W = HEAD_W
    x2 = x.reshape(T, D)
    mem2 = mem.reshape(seqs * n_mem, D)
    tgt2 = loss_target.reshape(T, D)
    tm_big = min(1024, T)
    tm_mid = min(512, T)
    tm_sq = min(1024, T)
    tm_mix = min(256, seq_len)
    tm_att = min(1024, seq_len)
    tkv = min(512, seqs * n_mem)
    px, py, pc = _my_place()
    me = _slot_of(px, py, pc).astype(jnp.int32)
    me1 = me.reshape(1)

    shard_bf = {n: w[n][0].astype(BF16) for n in BIG}

    def landing(n):
        zone = lax.empty((N_DEV,) + shard_bf[n].shape, BF16)
        return lax.dynamic_update_slice(zone, shard_bf[n][None], (me, 0, 0))

    w_in_started, tok = gather2_start(shard_bf["w_in"], landing("w_in"), name="w_in_gather_start")
    shard_bf["w_out"] = shard_bf["w_out"] + tok[0, 0].astype(BF16)
    ag_groups = (("w_out", "xw_q", "xw_kv", "xw_o"), ("w_up",), ("w_down",))
    ag_started, tok = split_start([([shard_bf[n] for n in grp], [landing(n) for n in grp]) for grp in ag_groups],
                                name="weights_gather_start", scatter=False)

    pool_w_bf = pool_w[0].astype(BF16)
    scale4 = pool_scale.reshape(4, 1, W)
    gn4 = hgrn_norm.reshape(4, 1, W)
    theta4 = lb_theta.reshape(2, 4, W).transpose(1, 0, 2)
    g_final = norm_final.reshape(1, D)

    n1 = prenorm(x2, norm_mix, tok, tm=tm_sq)
    wi3 = gather2_wait(gather2_pass_on(w_in_started, n1, name="w_in_gather_pass_on"), name="w_in_gather_wait")
    full_w_in = wi3.transpose(1, 0, 2).reshape(D, -1)
    u5 = proj_plain(n1, full_w_in, name="in_proj", tm=tm_mid, tn=4 * W, out_dtype=F32, out_slabs=5)
    tri_bf, tri_f = chunk_triangles(tm_mix)
    y2, o_pre, st_prev = mixer_fwd(u5, pool_w_bf, scale4, theta4, gn4, tri_bf, tri_f, seqs=seqs, seq_len=seq_len,
                                   tm=tm_mix)
    (_, (wo3, wq3, wkv3, wao3)), = split_wait(ag_started[0:1], y2, name="weights_gather_wait_attn", scatter=False)
    full_w_out, full_xw_q, full_xw_o = wo3.reshape(D, D), wq3.reshape(D, D), wao3.reshape(D, D)
    tn = 4 * W
    h1, n2, q = proj_res_norm(y2, full_w_out, x2, norm_xq, full_xw_q, name="out_q_proj", tm=tm_sq, tn=tn)
    kv3, memn = proj_norm(mem2, norm_mem, wkv3, name="kv_proj", tm=tkv, tn=wkv3.shape[2], out_dtype=BF16,
                          out_slabs=2)
    o_att = attn_fwd(q, kv3, seqs=seqs, seq_len=seq_len, n_mem=n_mem, tm=tm_att)
    h2, n3 = proj_res_norm(o_att, full_xw_o, h1, norm_mlp, name="attn_out_proj", tm=tm_sq, tn=tn)
    (_, (wup3,)), = split_wait(ag_started[1:2], h2, name="weights_gather_wait_up", scatter=False)
    tn_up = wup3.shape[2]
    aa = proj_plain(n3, wup3, name="up_proj", tm=tm_mid, tn=tn_up, relu2=True)
    (_, (wdn3,)), = split_wait(ag_started[2:3], aa, name="weights_gather_wait_down", scatter=False)
    full_w_down = wdn3.reshape(-1, D)
    dh3, dh3b, sq_err, dg_final = proj_res_loss(aa, full_w_down, h2, g_final, tgt2, name="down_proj_loss",
                                                tm=tm_mid, tn=tn)

    def send(parts, name):
        srcs = [p.reshape((N_DEV, -1, p.shape[-1])) for p in parts]
        lands = [lax.empty(s.shape, BF16) for s in srcs]
        started, token = split_start([(srcs, lands)], name=name, scatter=True)
        return started[0], token

    gw_down = wgrad(aa, dh3b, name="down_proj_wgrad", tt=tm_mid, tn=tn)
    sent_down, tok = send([gw_down], "grads_send_down")
    dap = back_plain(dh3b, full_w_down, name="down_proj_bwd", tm=tm_mid, tn=tn, out_dtype=BF16, relu2_value=aa,
                     after=tok)
    gw_up = wgrad(n3, dap, name="up_proj_wgrad", tt=tm_mid, tn=tn_up, out_slabs=N_DEV)
    sent_up, tok = send([gw_up], "grads_send_up")
    dh2, dh2b, do_att, dg_mlp = back_norm(dap, wup3, h2, norm_mlp, dh3, name="up_proj_bwd", tm=tm_mid, tk=tn_up,
                                          w_next=full_xw_o, after=tok)
    gxw_o = wgrad(o_att, dh2b, name="attn_out_proj_wgrad", tt=tm_sq, tn=tn)
    dq, dkv3 = attn_bwd(q, kv3, do_att, seqs=seqs, seq_len=seq_len, n_mem=n_mem, tm=tm_att)
    gxw_q = wgrad(n2, dq, name="q_proj_wgrad", tt=tm_sq, tn=tn)
    gxw_kv = wgrad(memn, dkv3, name="kv_proj_wgrad", tt=tkv, tn=wkv3.shape[2], out_slabs=N_DEV)
    sent_attn, tok = send([gxw_o, gxw_q, gxw_kv], "grads_send_attn")
    dg_mem = back_norm(dkv3, wkv3, mem2, norm_mem, None, name="kv_proj_bwd", tm=tkv, tk=wkv3.shape[2])
    dh1, dh1b, dy2, dg_xq = back_norm(dq, full_xw_q, h1, norm_xq, dh2, name="q_proj_bwd", tm=tm_mid, tk=D,
                                      w_next=full_w_out, next_dtype=F32, next_slabs=2, after=tok)
    gw_out = wgrad(y2, dh1b, name="out_proj_wgrad", tt=tm_sq, tn=tn)
    sent_out, tok = send([gw_out], "grads_send_out")
    du5, dpw, dsc, dlb, dgn = mixer_bwd(u5, dy2, o_pre, st_prev, pool_w_bf, scale4, theta4, gn4, tri_bf, tri_f, tok,
                                        seqs=seqs, seq_len=seq_len, tm=tm_mix)
    gw_in = wgrad(n1, du5, name="in_proj_wgrad", tt=tm_mid, tn=tn)
    gw_in_slots = gw_in.reshape(D, N_DEV, -1).transpose(1, 0, 2)
    sent_in, tok = send([gw_in_slots], "grads_send_in")
    dx, dg_mix = back_norm(du5, full_w_in, x2, norm_mix, dh1, name="in_proj_bwd", tm=tm_mid, tk=tn, bf16_copy=False,
                           after=tok)

    dlb_row = dlb.reshape(1, 4 * W)
    buf_vec = _pad_rows(jnp.concatenate([dg_mix, dg_xq, dg_mem, dg_mlp, dg_final, sq_err], axis=0), 8)
    buf_half = _pad_rows(jnp.concatenate([dsc.reshape(1, 4 * W), dgn.reshape(1, 4 * W), dlb_row, -dlb_row], axis=0), 8)
    small_src = [dpw.reshape(4 * W, W), buf_vec, buf_half]
    small_land = [lax.dynamic_update_slice(lax.empty((N_DEV,) + b.shape, F32), b[None], (me, 0, 0))
                  for b in small_src]
    small_started, tok = split_start([(small_src, small_land)], name="small_grads_start", scatter=False)

    done = split_wait([sent_down, sent_up, sent_attn, sent_out, sent_in], tok, name="grads_wait", scatter=True)
    slots = dict(w_down=(0, 0), w_up=(1, 0), xw_o=(2, 0), xw_q=(2, 1), xw_kv=(2, 2), w_out=(3, 0), w_in=(4, 0))
    own = {n: done[gi][0][ai] for n, (gi, ai) in slots.items()}
    got = {n: done[gi][1][ai] for n, (gi, ai) in slots.items()}
    res = {}
    for n in BIG:
        shp = w[n].shape
        r = adamw_sharded(me1, own[n], got[n], w[n][0], mom[n][0], var[n][0], name="adamw_" + n,
                          tr=min(256, shp[1]))
        for kind, a in zip("gdmv", r):
            res[kind, n] = a.reshape(shp)
    (_, small_parts), = split_wait(small_started, res["g", BIG[-1]], name="small_grads_wait", scatter=False)
    loss = 0.5 * jnp.sum(small_parts[1][:, 5, :]) / D
    r = adamw_replicated(small_parts, [w[n].reshape(v2) for n, v2, _, _ in SMALL],
                         [mom[n].reshape(v2) for n, v2, _, _ in SMALL],
                         [var[n].reshape(v2) for n, v2, _, _ in SMALL],
                         [(b, r0, v2[0]) for _, v2, b, r0 in SMALL])
    for kind, arrs in zip("gdmv", r):
        for (n, _, _, _), a in zip(SMALL, arrs):
            res[kind, n] = a.reshape(w[n].shape)

    out = [loss, dx.reshape(x.shape)]
    for kind in "gdmv":
        out += [res[kind, n] for n in WEIGHTS]
    return tuple(out)
```

```python
import jax
import jax.numpy as jnp
from jax import lax
from jax.experimental import pallas as pl
from jax.experimental.pallas import tpu as pltpu

F32 = jnp.float32
BF16 = jnp.bfloat16
EPS = 1e-6
CHUNK = 64
POOL_HALO = 16
HEAD_W = 128
HEADS_PER_STEP = 4
XATTN_HEADS = 4
N_DEV = 8
N_PEERS = N_DEV - 1
ADAM_LR = 0.001
ADAM_B1 = 0.9
ADAM_B2 = 0.999
ADAM_EPS = 1e-08
ADAM_WD = 0.01
ADAM_STEP = 10
V7X_VMEM_LIMIT = 52 * 1024 * 1024
MESH = pl.DeviceIdType.MESH
HBM = pl.BlockSpec(memory_space=pltpu.HBM)
SEM = pl.BlockSpec(memory_space=pltpu.SEMAPHORE)


def _cparams(dims):
    return pltpu.CompilerParams(dimension_semantics=dims, vmem_limit_bytes=V7X_VMEM_LIMIT)


def _sigmoid(v):
    return 0.5 * jnp.tanh(0.5 * v) + 0.5


def _dot(a, b):
    return jnp.dot(a, b, preferred_element_type=F32)


def _dot_nt(a, b):
    return lax.dot_general(a, b, (((1,), (1,)), ((), ())), preferred_element_type=F32)


def _dot_tn(a, b):
    return lax.dot_general(a, b, (((0,), (0,)), ((), ())), preferred_element_type=F32)


def _tri_apply(tri, v):
    hi = v.astype(BF16)
    lo = (v - hi.astype(F32)).astype(BF16)
    return _dot(tri, hi) + _dot(tri, lo)


def _mat_shape(a):
    return a.shape if a.ndim == 2 else (a.shape[1], a.shape[0] * a.shape[2])


def _tile_spec(a, rows, cols, row_of, col_of):
    if a.ndim == 2:
        return pl.BlockSpec((rows, cols), lambda *g: (row_of(*g), col_of(*g)))
    per = a.shape[2] // cols
    return pl.BlockSpec((None, rows, cols), lambda *g: (col_of(*g) // per, row_of(*g), col_of(*g) % per))


def _out_struct(rows, n, slabs, dtype):
    return jax.ShapeDtypeStruct((rows, n) if slabs is None else (slabs, rows, n // slabs), dtype)


def norm_mm(h, g, w, *, name, tm, tn, out_dtype, out_slabs=None):
    T, D = h.shape
    N = _mat_shape(w)[1]
    o_shape = _out_struct(T, N, out_slabs, out_dtype)

    def body(h_ref, g_ref, w_ref, o_ref, n_ref):
        @pl.when(pl.program_id(1) == 0)
        def _():
            x = h_ref[...]
            r = lax.rsqrt(jnp.mean(x * x, axis=-1, keepdims=True) + EPS)
            n_ref[...] = (x * r * g_ref[...]).astype(BF16)

        o_ref[...] = _dot(n_ref[...], w_ref[...]).astype(o_ref.dtype)

    return pl.pallas_call(
        body, name=name, grid=(T // tm, N // tn),
        in_specs=[pl.BlockSpec((tm, D), lambda i, j: (i, 0)),
                  pl.BlockSpec((1, D), lambda i, j: (0, 0)),
                  _tile_spec(w, D, tn, lambda i, j: 0, lambda i, j: j)],
        out_specs=[_tile_spec(o_shape, tm, tn, lambda i, j: i, lambda i, j: j),
                   pl.BlockSpec((tm, D), lambda i, j: (i, 0))],
        out_shape=[o_shape, jax.ShapeDtypeStruct((T, D), BF16)],
        compiler_params=_cparams(("parallel", "arbitrary")),
    )(h, g, w)


def mm_nn(a, w, res, *, name, tm, tn, tk, relu2=False):
    T, K = _mat_shape(a)
    N = w.shape[1]
    nk = K // tk

    def body(a_ref, w_ref, r_ref, o_ref, acc_ref):
        k = pl.program_id(2)
        av = a_ref[...]
        if relu2:
            av = jnp.maximum(av, 0.0)
            av = av * av
        part = _dot(av.astype(BF16), w_ref[...])

        @pl.when(k == 0)
        def _():
            acc_ref[...] = part

        @pl.when(k > 0)
        def _():
            acc_ref[...] += part

        @pl.when(k == nk - 1)
        def _():
            o_ref[...] = r_ref[...] + acc_ref[...]

    return pl.pallas_call(
        body, name=name, grid=(T // tm, N // tn, nk),
        in_specs=[_tile_spec(a, tm, tk, lambda i, j, k: i, lambda i, j, k: k),
                  pl.BlockSpec((tk, tn), lambda i, j, k: (k, j)),
                  pl.BlockSpec((tm, tn), lambda i, j, k: (i, j))],
        out_specs=pl.BlockSpec((tm, tn), lambda i, j, k: (i, j)),
        out_shape=jax.ShapeDtypeStruct((T, N), F32),
        scratch_shapes=[pltpu.VMEM((tm, tn), F32)],
        compiler_params=_cparams(("parallel", "parallel", "arbitrary")),
    )(a, w, res)


def mm_nt(a, w, *, name, tm, tn, tk, out_dtype, out_slabs=None, relu2_of=None, after=None):
    T, K = _mat_shape(a)
    nk = K // tk
    N = w.shape[0]
    has_z = relu2_of is not None
    o_shape = _out_struct(T, N, out_slabs, out_dtype)

    def body(*refs):
        a_ref, w_ref = refs[0], refs[1]
        z_ref = refs[2] if has_z else None
        o_ref, acc_ref = refs[-2], refs[-1]
        k = pl.program_id(2)
        part = _dot_nt(a_ref[...].astype(BF16), w_ref[...])

        @pl.when(k == 0)
        def _():
            acc_ref[...] = part

        @pl.when(k > 0)
        def _():
            acc_ref[...] += part

        @pl.when(k == nk - 1)
        def _():
            out = acc_ref[...]
            if has_z:
                out = out * (2.0 * jnp.maximum(z_ref[...], 0.0))
            o_ref[...] = out.astype(o_ref.dtype)

    in_specs = [_tile_spec(a, tm, tk, lambda i, j, k: i, lambda i, j, k: k),
                pl.BlockSpec((tn, tk), lambda i, j, k: (j, k))]
    args = [a, w]
    if has_z:
        in_specs.append(pl.BlockSpec((tm, tn), lambda i, j, k: (i, j)))
        args.append(relu2_of)
    if after is not None:
        in_specs.append(pl.BlockSpec(after.shape, lambda i, j, k: (0, 0)))
        args.append(after)
    return pl.pallas_call(
        body, name=name, grid=(T // tm, N // tn, nk),
        in_specs=in_specs,
        out_specs=_tile_spec(o_shape, tm, tn, lambda i, j, k: i, lambda i, j, k: j),
        out_shape=o_shape,
        scratch_shapes=[pltpu.VMEM((tm, tn), F32)],
        compiler_params=_cparams(("parallel", "parallel", "arbitrary")),
    )(*args)


def mm_nt_normbwd(a, w, h, g, dres, *, name, tm, tk, after=None):
    T, K = _mat_shape(a)
    nk = K // tk
    D = h.shape[1]
    with_dh = dres is not None

    def body(*refs):
        a_ref, w_ref, h_ref, g_ref = refs[:4]
        if with_dh:
            r_ref = refs[4]
            dh_ref, dhb_ref, dg_ref, acc_ref = refs[-4:]
        else:
            dg_ref, acc_ref = refs[-2:]
        i = pl.program_id(0)
        k = pl.program_id(1)
        part = _dot_nt(a_ref[...].astype(BF16), w_ref[...])

        @pl.when(k == 0)
        def _():
            acc_ref[...] = part

        @pl.when(k > 0)
        def _():
            acc_ref[...] += part

        @pl.when(k == nk - 1)
        def _():
            dn = acc_ref[...]
            x = h_ref[...]
            r = lax.rsqrt(jnp.mean(x * x, axis=-1, keepdims=True) + EPS)
            xr = x * r
            dgp = jnp.sum(dn * xr, axis=0, keepdims=True)

            @pl.when(i == 0)
            def _():
                dg_ref[...] = dgp

            @pl.when(i > 0)
            def _():
                dg_ref[...] += dgp

            if with_dh:
                dyg = dn * g_ref[...]
                dx = r * (dyg - xr * jnp.mean(dyg * xr, axis=-1, keepdims=True))
                out = r_ref[...] + dx
                dh_ref[...] = out
                dhb_ref[...] = out.astype(BF16)

    row = pl.BlockSpec((tm, D), lambda i, k: (i, 0))
    vec = pl.BlockSpec((1, D), lambda i, k: (0, 0))
    in_specs = [_tile_spec(a, tm, tk, lambda i, k: i, lambda i, k: k),
                _tile_spec(w, D, tk, lambda i, k: 0, lambda i, k: k), row, vec]
    args = [a, w, h, g]
    if with_dh:
        in_specs.append(row)
        args.append(dres)
        out_specs = [row, row, vec]
        out_shape = [jax.ShapeDtypeStruct((T, D), F32), jax.ShapeDtypeStruct((T, D), BF16),
                     jax.ShapeDtypeStruct((1, D), F32)]
    else:
        out_specs = vec
        out_shape = jax.ShapeDtypeStruct((1, D), F32)
    if after is not None:
        in_specs.append(pl.BlockSpec(after.shape, lambda i, k: (0, 0)))
        args.append(after)
    return pl.pallas_call(
        body, name=name, grid=(T // tm, nk),
        in_specs=in_specs, out_specs=out_specs, out_shape=out_shape,
        scratch_shapes=[pltpu.VMEM((tm, D), F32)],
        compiler_params=_cparams(("arbitrary", "arbitrary")),
    )(*args)


def mm_tn(a, b, *, name, tt, tko, tn, relu2=False, out_slabs=None):
    T, K = _mat_shape(a)
    N = _mat_shape(b)[1]
    nt = T // tt
    o_shape = _out_struct(K, N, out_slabs, BF16)

    def body(a_ref, b_ref, o_ref, acc_ref):
        t = pl.program_id(2)
        av = a_ref[...]
        if relu2:
            av = jnp.maximum(av, 0.0)
            av = av * av
        part = _dot_tn(av.astype(BF16), b_ref[...].astype(BF16))

        @pl.when(t == 0)
        def _():
            acc_ref[...] = part

        @pl.when(t > 0)
        def _():
            acc_ref[...] += part

        @pl.when(t == nt - 1)
        def _():
            o_ref[...] = acc_ref[...].astype(BF16)

    return pl.pallas_call(
        body, name=name, grid=(K // tko, N // tn, nt),
        in_specs=[_tile_spec(a, tt, tko, lambda kk, j, t: t, lambda kk, j, t: kk),
                  _tile_spec(b, tt, tn, lambda kk, j, t: t, lambda kk, j, t: j)],
        out_specs=_tile_spec(o_shape, tko, tn, lambda kk, j, t: kk, lambda kk, j, t: j),
        out_shape=o_shape,
        scratch_shapes=[pltpu.VMEM((tko, tn), F32)],
        compiler_params=_cparams(("parallel", "parallel", "arbitrary")),
    )(a, b)


def _resident(a):
    nd = a.ndim
    return pl.BlockSpec(a.shape, lambda i: (0,) * nd, pipeline_mode=pl.Buffered(1))


def _row_block(a, tm):
    if a.ndim == 2:
        return pl.BlockSpec((tm, a.shape[1]), lambda i: (i, 0))
    return pl.BlockSpec((a.shape[0], tm, a.shape[2]), lambda i: (0, i, 0))


def _cols(ref, c, width):
    if len(ref.shape) == 2:
        return ref[:, c * width:(c + 1) * width]
    per = ref.shape[2] // width
    if per == 1:
        return ref[c]
    return ref[c // per, :, (c % per) * width:(c % per + 1) * width]


def _set_cols(ref, c, width, val):
    if len(ref.shape) == 2:
        ref[:, c * width:(c + 1) * width] = val
        return
    per = ref.shape[2] // width
    if per == 1:
        ref[c] = val
    else:
        ref[c // per, :, (c % per) * width:(c % per + 1) * width] = val


def _all_cols(ref):
    if len(ref.shape) == 2:
        return ref[...]
    return jnp.concatenate([ref[s] for s in range(ref.shape[0])], axis=1)


def _rms(x):
    return lax.rsqrt(jnp.mean(x * x, axis=-1, keepdims=True) + EPS)


def _row_params():
    return _cparams(("arbitrary",))


def proj_norm(h, g, w, *, name, tm, tn, out_dtype, out_slabs=None):
    T, D = h.shape
    N = _mat_shape(w)[1]
    o_shape = _out_struct(T, N, out_slabs, out_dtype)

    def body(h_ref, g_ref, w_ref, o_ref, n_ref):
        x = h_ref[...]
        n = (x * _rms(x) * g_ref[...]).astype(BF16)
        n_ref[...] = n
        for c in range(N // tn):
            _set_cols(o_ref, c, tn, _dot(n, _cols(w_ref, c, tn)).astype(out_dtype))

    return pl.pallas_call(
        body, name=name, grid=(T // tm,),
        in_specs=[_row_block(h, tm), pl.BlockSpec((1, D), lambda i: (0, 0)), _resident(w)],
        out_specs=[_row_block(o_shape, tm), pl.BlockSpec((tm, D), lambda i: (i, 0))],
        out_shape=[o_shape, jax.ShapeDtypeStruct((T, D), BF16)],
        compiler_params=_row_params(),
    )(h, g, w)


def prenorm(h, g, after, *, tm):
    T, D = h.shape

    def body(h_ref, g_ref, _after_ref, n_ref):
        x = h_ref[...]
        n_ref[...] = (x * _rms(x) * g_ref[...]).astype(BF16)

    row = pl.BlockSpec((tm, D), lambda i: (i, 0))
    return pl.pallas_call(
        body, name="prenorm", grid=(T // tm,),
        in_specs=[row, pl.BlockSpec((1, D), lambda i: (0, 0)), _anchor_spec(after)],
        out_specs=row, out_shape=jax.ShapeDtypeStruct((T, D), BF16),
        compiler_params=_row_params(),
    )(h, g, after)


def proj_plain(a, w, *, name, tm, tn, out_dtype=BF16, out_slabs=None, relu2=False):
    T = a.shape[0]
    N = _mat_shape(w)[1]

    def body(a_ref, w_ref, o_ref):
        av = a_ref[...]
        for c in range(N // tn):
            z = _dot(av, _cols(w_ref, c, tn))
            if relu2:
                z = jnp.maximum(z, 0.0)
                z = z * z
            _set_cols(o_ref, c, tn, z.astype(out_dtype))

    o_shape = _out_struct(T, N, out_slabs, out_dtype)
    return pl.pallas_call(
        body, name=name, grid=(T // tm,),
        in_specs=[_row_block(a, tm), _resident(w)],
        out_specs=_row_block(o_shape, tm), out_shape=o_shape,
        compiler_params=_row_params(),
    )(a, w)


def proj_res_norm(a, w, res, g, w_next=None, *, name, tm, tn):
    T = res.shape[0]
    D = w.shape[1]
    chained = w_next is not None

    def body(*refs):
        a_ref, w_ref, r_ref, g_ref = refs[:4]
        h_ref, n_ref = refs[4 + chained], refs[5 + chained]
        av = _all_cols(a_ref)
        for c in range(D // tn):
            sl = slice(c * tn, (c + 1) * tn)
            h_ref[:, sl] = r_ref[:, sl] + _dot(av, w_ref[:, sl])
        hv = h_ref[...]
        n = (hv * _rms(hv) * g_ref[...]).astype(BF16)
        n_ref[...] = n
        if chained:
            for c in range(D // tn):
                sl = slice(c * tn, (c + 1) * tn)
                refs[-1][:, sl] = _dot(n, refs[4][:, sl]).astype(BF16)

    row = pl.BlockSpec((tm, D), lambda i: (i, 0))
    half = jax.ShapeDtypeStruct((T, D), BF16)
    return pl.pallas_call(
        body, name=name, grid=(T // tm,),
        in_specs=[_row_block(a, tm), _resident(w), row, pl.BlockSpec((1, D), lambda i: (0, 0))]
        + ([_resident(w_next)] if chained else []),
        out_specs=[row, row] + ([row] if chained else []),
        out_shape=[jax.ShapeDtypeStruct((T, D), F32), half] + ([half] if chained else []),
        compiler_params=_row_params(),
    )(*([a, w, res, g] + ([w_next] if chained else [])))


def proj_res_loss(a, w, res, g, target, *, name, tm, tn):
    T = res.shape[0]
    D = w.shape[1]

    def body(a_ref, w_ref, r_ref, g_ref, t_ref, dh_ref, dhb_ref, ls_ref, dg_ref):
        i = pl.program_id(0)
        gv = g_ref[...]
        ls, dg = 0.0, 0.0
        halves = [slice(s * (tm // 2), (s + 1) * (tm // 2)) for s in range(2)]
        for rows in halves:
            av = a_ref[rows, :]
            for c in range(D // tn):
                sl = slice(c * tn, (c + 1) * tn)
                dh_ref[rows, sl] = r_ref[rows, sl] + _dot(av, w_ref[:, sl])
        for rows in halves:
            x = dh_ref[rows, :]
            r = _rms(x)
            xr = x * r
            d = xr * gv - t_ref[rows, :]
            dy = d * (1.0 / D)
            dyg = dy * gv
            dx = r * (dyg - xr * jnp.mean(dyg * xr, axis=-1, keepdims=True))
            dh_ref[rows, :] = dx
            dhb_ref[rows, :] = dx.astype(BF16)
            ls = ls + jnp.sum(d * d, axis=0, keepdims=True)
            dg = dg + jnp.sum(dy * xr, axis=0, keepdims=True)

        @pl.when(i == 0)
        def _():
            ls_ref[...] = ls
            dg_ref[...] = dg

        @pl.when(i > 0)
        def _():
            ls_ref[...] += ls
            dg_ref[...] += dg

    row = pl.BlockSpec((tm, D), lambda i: (i, 0))
    vec = pl.BlockSpec((1, D), lambda i: (0, 0))
    return pl.pallas_call(
        body, name=name, grid=(T // tm,),
        in_specs=[_row_block(a, tm), _resident(w), row, vec, row],
        out_specs=[row, row, vec, vec],
        out_shape=[jax.ShapeDtypeStruct((T, D), F32), jax.ShapeDtypeStruct((T, D), BF16),
                   jax.ShapeDtypeStruct((1, D), F32), jax.ShapeDtypeStruct((1, D), F32)],
        compiler_params=_row_params(),
    )(a, w, res, g, target)


def _anchor_spec(after):
    return pl.BlockSpec(after.shape, lambda i: (0, 0))


def back_plain(a, w, *, name, tm, tn, out_dtype, out_slabs=None, relu2_value=None, after=None):
    T = a.shape[0]
    N = w.shape[0]
    has_z = relu2_value is not None
    o_shape = _out_struct(T, N, out_slabs, out_dtype)

    def body(*refs):
        a_ref, w_ref = refs[0], refs[1]
        o_ref = refs[-1]
        av = a_ref[...]
        for c in range(N // tn):
            out = _dot_nt(av, w_ref[c * tn:(c + 1) * tn, :])
            if has_z:
                out = out * (2.0 * jnp.sqrt(refs[2][:, c * tn:(c + 1) * tn]).astype(F32))
            _set_cols(o_ref, c, tn, out.astype(out_dtype))

    in_specs, args = [_row_block(a, tm), _resident(w)], [a, w]
    if has_z:
        in_specs.append(_row_block(relu2_value, tm))
        args.append(relu2_value)
    if after is not None:
        in_specs.append(_anchor_spec(after))
        args.append(after)
    return pl.pallas_call(
        body, name=name, grid=(T // tm,),
        in_specs=in_specs, out_specs=_row_block(o_shape, tm), out_shape=o_shape,
        compiler_params=_row_params(),
    )(*args)


def back_norm(a, w, h, g, dres, *, name, tm, tk, bf16_copy=True, w_next=None, next_dtype=BF16, next_slabs=None,
              after=None):
    T, K = _mat_shape(a)
    D = h.shape[1]
    with_dh = dres is not None
    chained = w_next is not None
    n_in = 4 + with_dh + chained
    tn = 4 * HEAD_W

    def body(*refs):
        a_ref, w_ref, h_ref, g_ref = refs[:4]
        outs = refs[n_in + (after is not None):]
        i = pl.program_id(0)
        if len(w_ref.shape) == 2:
            dn = _dot_nt(_all_cols(a_ref).astype(BF16), w_ref[...])
        else:
            dn = None
            for kc in range(K // tk):
                part = _dot_nt(_cols(a_ref, kc, tk).astype(BF16), _cols(w_ref, kc, tk))
                dn = part if dn is None else dn + part
        x = h_ref[...]
        r = _rms(x)
        xr = x * r
        dgp = jnp.sum(dn * xr, axis=0, keepdims=True)
        dg_ref = outs[-1]

        @pl.when(i == 0)
        def _():
            dg_ref[...] = dgp

        @pl.when(i > 0)
        def _():
            dg_ref[...] += dgp

        if with_dh:
            dyg = dn * g_ref[...]
            out = refs[4][...] + r * (dyg - xr * jnp.mean(dyg * xr, axis=-1, keepdims=True))
            outs[0][...] = out
            outb = out.astype(BF16)
            if bf16_copy:
                outs[1][...] = outb
            if chained:
                wn_ref, nx_ref = refs[5], outs[-2]
                for c in range(wn_ref.shape[0] // tn):
                    _set_cols(nx_ref, c, tn, _dot_nt(outb, wn_ref[c * tn:(c + 1) * tn, :]).astype(next_dtype))

    row = pl.BlockSpec((tm, D), lambda i: (i, 0))
    vec = pl.BlockSpec((1, D), lambda i: (0, 0))
    in_specs, args = [_row_block(a, tm), _resident(w), row, vec], [a, w, h, g]
    out_specs, out_shape = [], []
    if with_dh:
        in_specs.append(row)
        args.append(dres)
        out_specs.append(row)
        out_shape.append(jax.ShapeDtypeStruct((T, D), F32))
        if bf16_copy:
            out_specs.append(row)
            out_shape.append(jax.ShapeDtypeStruct((T, D), BF16))
    if chained:
        in_specs.append(_resident(w_next))
        args.append(w_next)
        nx_shape = _out_struct(T, w_next.shape[0], next_slabs, next_dtype)
        out_specs.append(_row_block(nx_shape, tm))
        out_shape.append(nx_shape)
    out_specs.append(vec)
    out_shape.append(jax.ShapeDtypeStruct((1, D), F32))
    if after is not None:
        in_specs.append(_anchor_spec(after))
        args.append(after)
    outs = pl.pallas_call(
        body, name=name, grid=(T // tm,),
        in_specs=in_specs, out_specs=out_specs, out_shape=out_shape,
        compiler_params=_row_params(),
    )(*args)
    return outs if len(outs) > 1 else outs[0]


def wgrad(a, b, *, name, tt, tn, out_slabs=None):
    T, K = _mat_shape(a)
    N = _mat_shape(b)[1]
    nt = T // tt
    o_shape = _out_struct(K, N, out_slabs, BF16)

    flipped = K > N and out_slabs is None

    def body(a_ref, b_ref, o_ref, acc_ref):
        t = pl.program_id(0)

        @pl.when(t == 0)
        def _():
            acc_ref[...] = jnp.zeros_like(acc_ref)

        if flipped:
            bt = _all_cols(b_ref).astype(BF16).T
            for c in range(K // tn):
                acc_ref[:, c * tn:(c + 1) * tn] += _dot(bt, _cols(a_ref, c, tn).astype(BF16))
        else:
            at = _all_cols(a_ref).astype(BF16).T
            for c in range(N // tn):
                acc_ref[:, c * tn:(c + 1) * tn] += _dot(at, _cols(b_ref, c, tn).astype(BF16))

        @pl.when(t == nt - 1)
        def _():
            if flipped:
                for c in range(K // tn):
                    o_ref[c * tn:(c + 1) * tn, :] = acc_ref[:, c * tn:(c + 1) * tn].T.astype(BF16)
            else:
                for c in range(N // tn):
                    _set_cols(o_ref, c, tn, acc_ref[:, c * tn:(c + 1) * tn].astype(BF16))

    return pl.pallas_call(
        body, name=name, grid=(nt,),
        in_specs=[_row_block(a, tt), _row_block(b, tt)],
        out_specs=_resident(o_shape), out_shape=o_shape,
        scratch_shapes=[pltpu.VMEM((N, K) if flipped else (K, N), F32)],
        compiler_params=_row_params(),
    )(a, b)


def chunk_triangles(tm):
    r = lax.broadcasted_iota(jnp.int32, (tm, tm), 0)
    c = lax.broadcasted_iota(jnp.int32, (tm, tm), 1)
    same = (r // CHUNK) == (c // CHUNK)
    tri = jnp.stack([same & (c <= r), same & (c >= r)]).astype(F32)
    return tri.astype(BF16), tri


def _tri_spec(tm):
    return pl.BlockSpec((2, tm, tm), lambda g, s, i: (0, 0, 0))


def _chunk_row(v, r, nc):
    return jnp.concatenate([jnp.broadcast_to(v[c * CHUNK + r:c * CHUNK + r + 1], (CHUNK, v.shape[1]))
                            for c in range(nc)], axis=0)


def _block_diag(v, nc):
    chunk = lax.broadcasted_iota(jnp.int32, (v.shape[0], 1), 0) // CHUNK
    return jnp.concatenate([jnp.where(chunk == c, v, jnp.zeros_like(v)) for c in range(nc)], axis=1)


def _pool_windows_back(ext_ref, tm):
    n = tm + 32
    ext_ref[1, 8:n] = ext_ref[0, 8:n] + ext_ref[0, 7:n - 1]
    ext_ref[2, 16:n] = ext_ref[1, 16:n] + ext_ref[1, 14:n - 2]
    ext_ref[3, 24:n] = ext_ref[2, 24:n] + ext_ref[2, 20:n - 4]
    s2 = ext_ref[1, 32:n]
    s4 = ext_ref[2, 32:n]
    s8 = ext_ref[3, 32:n]
    s16 = s8 + ext_ref[3, 24:n - 8]
    return s2, s4, s8, s16


def _pool_windows_fwd(ext_ref, tm):
    n = tm + 32
    ext_ref[1, 0:n - 8] = ext_ref[0, 0:n - 8] + ext_ref[0, 1:n - 7]
    ext_ref[2, 0:n - 16] = ext_ref[1, 0:n - 16] + ext_ref[1, 2:n - 14]
    ext_ref[3, 0:n - 24] = ext_ref[2, 0:n - 24] + ext_ref[2, 4:n - 20]
    s2 = ext_ref[1, 0:tm]
    s4 = ext_ref[2, 0:tm]
    s8 = ext_ref[3, 0:tm]
    s16 = s8 + ext_ref[3, 8:tm + 8]
    return s2, s4, s8, s16


def _select_window(g, s2, s4, s8, s16):
    return jnp.where(g == 0, s2, jnp.where(g == 1, s4, jnp.where(g == 2, s8, s16)))


def _pool_count(g, pos):
    width = lax.shift_left(jnp.int32(2), g)
    return jnp.minimum(pos + 1, width).astype(F32)


def _hgrn_gates(zq, zf, th):
    lb = _sigmoid(th[0:1, :] - th[1:2, :])
    sig = _sigmoid(zf)
    f = lb + (1.0 - lb) * sig
    sq = _sigmoid(zq)
    return lb, sig, f, sq


def mixer_fwd(u5, pool_w_bf, scale4, theta4, gn4, tri_bf, tri_f, *, seqs, seq_len, tm):
    T = u5.shape[1]
    tps = seq_len // tm
    nc = tm // CHUNK
    W = HEAD_W

    H = HEADS_PER_STEP
    heads = range(H)

    def body(u_ref, pw_ref, sc_ref, th_ref, gn_ref, tri_ref, msk_ref, y_ref, o_ref, st_ref, halo_ref, ext_ref, s_ref):
        g = pl.program_id(0)
        i = pl.program_id(2)

        @pl.when(i == 0)
        def _():
            halo_ref[...] = jnp.zeros_like(halo_ref)
            s_ref[...] = jnp.zeros_like(s_ref)

        row = lax.broadcasted_iota(jnp.int32, (tm, 1), 0)
        cols = [slice(h * W, (h + 1) * W) for h in heads]

        pooled = []
        for h in heads:
            grp = g * H + h
            up = u_ref[0, :, cols[h]]
            ext_ref[h, 0, 0:16] = jnp.zeros((16, W), F32)
            ext_ref[h, 0, 16:32] = halo_ref[h]
            ext_ref[h, 0, 32:32 + tm] = up
            win = _select_window(grp, *_pool_windows_back(ext_ref.at[h], tm))
            pooled.append((win * (1.0 / _pool_count(grp, i * tm + row)) - up).astype(BF16))
            halo_ref[h] = up[tm - POOL_HALO:tm]
        mixed = [_dot(pooled[h], pw_ref[h]) for h in heads]
        for h in heads:
            y_ref[0, :, cols[h]] = (mixed[h] * sc_ref[h]).astype(BF16)

        zq, zf, zi, zg = u_ref[1], u_ref[2], u_ref[3], u_ref[4]
        th = [th_ref[h] for h in heads]
        lb = jnp.concatenate([_sigmoid(t[0:1, :] - t[1:2, :]) for t in th], axis=1)
        f = lb + (1.0 - lb) * _sigmoid(zf)
        kk = 1.0 - f
        q = zq * _sigmoid(zq)
        G = _tri_apply(tri_ref[0], jnp.log(f))
        Gm, Gl = _chunk_row(G, CHUNK // 2 - 1, nc), _chunk_row(G, CHUNK - 1, nc)
        vb = zi.astype(BF16)
        qrb = (q * jnp.exp(G - Gm)).astype(BF16)
        krb = (kk * jnp.exp(Gm - G)).astype(BF16)
        keb = (kk * jnp.exp(Gl - G)).astype(BF16)
        qgb = (q * jnp.exp(G)).astype(BF16)
        mask = msk_ref[0] > 0.5
        a = [jnp.where(mask, _dot_nt(qrb[:, cols[h]], krb[:, cols[h]]), 0.0).astype(BF16) for h in heads]
        d_st = [_dot_tn(vb[:, cols[h]], _block_diag(keb[:, cols[h]], nc)) for h in heads]
        o_intra = [_dot(a[h], vb[:, cols[h]]) for h in heads]
        st_cat = []
        for h in heads:
            st = s_ref[h]
            states = []
            for c in range(nc):
                st_ref[c, h] = st
                states.append(st.astype(BF16))
                st = st * jnp.exp(G[(c + 1) * CHUNK - 1:(c + 1) * CHUNK, cols[h]]) + d_st[h][:, c * W:(c + 1) * W]
            s_ref[h] = st
            st_cat.append(jnp.concatenate(states, axis=1))
        o = [o_intra[h] + _dot_nt(_block_diag(qgb[:, cols[h]], nc), st_cat[h]) for h in heads]
        gate = zg * _sigmoid(zg)
        for h in heads:
            o_ref[:, cols[h]] = o[h]
            r = lax.rsqrt(jnp.mean(o[h] * o[h], axis=-1, keepdims=True) + EPS)
            y_ref[1, :, cols[h]] = (o[h] * r * gn_ref[h] * gate[:, cols[h]]).astype(BF16)

    def rb(s, i):
        return s * tps + i

    def per_head(*shape):
        return pl.BlockSpec((H,) + shape, lambda g, s, i: (g,) + (0,) * len(shape))

    return pl.pallas_call(
        body, name="mixer_fwd", grid=(4 // H, seqs, tps),
        in_specs=[pl.BlockSpec((5, tm, H * W), lambda g, s, i: (0, rb(s, i), g)),
                  per_head(W, W), per_head(1, W), per_head(2, W), per_head(1, W),
                  _tri_spec(tm), _tri_spec(tm)],
        out_specs=[pl.BlockSpec((2, tm, H * W), lambda g, s, i: (0, rb(s, i), g)),
                   pl.BlockSpec((tm, H * W), lambda g, s, i: (rb(s, i), g)),
                   pl.BlockSpec((nc, H, W, W), lambda g, s, i: (rb(s, i), g, 0, 0))],
        out_shape=[jax.ShapeDtypeStruct((2, T, 4 * W), BF16),
                   jax.ShapeDtypeStruct((T, 4 * W), F32),
                   jax.ShapeDtypeStruct((T // CHUNK, 4, W, W), F32)],
        scratch_shapes=[pltpu.VMEM((H, POOL_HALO, W), F32),
                        pltpu.VMEM((H, 4, tm + 32, W), F32),
                        pltpu.VMEM((H, W, W), F32)],
        compiler_params=_cparams(("arbitrary", "arbitrary", "arbitrary")),
    )(u5, pool_w_bf, scale4, theta4, gn4, tri_bf, tri_f)


def mixer_bwd(u5, dy2, o_pre, st_prev, pool_w_bf, scale4, theta4, gn4, tri_bf, tri_f, after, *, seqs, seq_len, tm):
    T = u5.shape[1]
    tps = seq_len // tm
    nc = tm // CHUNK
    W = HEAD_W
    hb = tm // POOL_HALO

    H = HEADS_PER_STEP
    heads = range(H)

    def body(u_ref, uh_ref, dy_ref, o_ref, st_ref, pw_ref, sc_ref, th_ref, gn_ref, tri_ref, msk_ref, _after_ref,
             du_ref, dpw_ref, dsc_ref, dlb_ref, dgn_ref, nxt_ref, ext_ref, ds_ref):
        g = pl.program_id(0)
        s = pl.program_id(1)
        i = pl.program_id(2)
        tile = tps - 1 - i
        first = (s == 0) & (i == 0)

        @pl.when(i == 0)
        def _():
            nxt_ref[...] = jnp.zeros_like(nxt_ref)
            ds_ref[...] = jnp.zeros_like(ds_ref)

        row = lax.broadcasted_iota(jnp.int32, (tm, 1), 0)
        cols = [slice(h * W, (h + 1) * W) for h in heads]

        def accumulate(ref, h, val):
            @pl.when(first)
            def _():
                ref[h] = val

            @pl.when(jnp.logical_not(first))
            def _():
                ref[h] += val

        def per_head(fn):
            return jnp.concatenate([jnp.broadcast_to(fn(cols[h]), (tm, W)) for h in heads], axis=1)

        inv_cnt, pb, dz = [], [], []
        for h in heads:
            grp = g * H + h
            inv_cnt.append(1.0 / _pool_count(grp, tile * tm + row))
            ext = ext_ref.at[h]
            up = u_ref[0, :, cols[h]]
            ext[0, 0:16] = jnp.zeros((16, W), F32)
            ext[0, 16:32] = jnp.where(tile == 0, 0.0, uh_ref[:, cols[h]])
            ext[0, 32:32 + tm] = up
            win = _select_window(grp, *_pool_windows_back(ext, tm))
            pb.append((win * inv_cnt[h] - up).astype(BF16))
            dz.append((dy_ref[0, :, cols[h]] * sc_ref[h]).astype(BF16))
        z = [_dot(pb[h], pw_ref[h]) for h in heads]
        dp = [_dot_nt(dz[h], pw_ref[h]) for h in heads]
        dpw = [_dot_tn(pb[h], dz[h]) for h in heads]
        for h in heads:
            accumulate(dsc_ref, h, jnp.sum(dy_ref[0, :, cols[h]] * z[h], axis=0, keepdims=True))
            accumulate(dpw_ref, h, dpw[h])
            ext = ext_ref.at[h]
            e = dp[h] * inv_cnt[h]
            ext[0, 0:tm] = e
            ext[0, tm:tm + 16] = nxt_ref[h]
            ext[0, tm + 16:tm + 32] = jnp.zeros((16, W), F32)
            lead = _select_window(g * H + h, *_pool_windows_fwd(ext, tm))
            nxt_ref[h] = e[0:POOL_HALO]
            du_ref[0, :, cols[h]] = (lead - dp[h]).astype(BF16)

        zq, zf, zi, zg = u_ref[1], u_ref[2], u_ref[3], u_ref[4]
        lb = jnp.concatenate([_sigmoid(th_ref[h][0:1, :] - th_ref[h][1:2, :]) for h in heads], axis=1)
        gn = jnp.concatenate([gn_ref[h] for h in heads], axis=1)
        sig, sq, sg = _sigmoid(zf), _sigmoid(zq), _sigmoid(zg)
        f = lb + (1.0 - lb) * sig
        kk = 1.0 - f
        q = zq * sq
        G = _tri_apply(tri_ref[0], jnp.log(f))

        dyh = dy_ref[1]
        o = o_ref[...]
        sqr = o * o
        r = per_head(lambda cs: lax.rsqrt(jnp.mean(sqr[:, cs], axis=-1, keepdims=True) + EPS))
        orr = o * r
        du_ref[4] = (dyh * (orr * gn) * (sg * (1.0 + zg * (1.0 - sg)))).astype(BF16)
        don = dyh * (zg * sg)
        dgn = jnp.sum(don * orr, axis=0, keepdims=True)
        dog = don * gn
        dog_orr = dog * orr
        do = r * (dog - orr * per_head(lambda cs: jnp.mean(dog_orr[:, cs], axis=-1, keepdims=True)))

        Gm, Gl = _chunk_row(G, CHUNK // 2 - 1, nc), _chunk_row(G, CHUNK - 1, nc)
        e_q, e_k, e_e, e_g = jnp.exp(G - Gm), jnp.exp(Gm - G), jnp.exp(Gl - G), jnp.exp(G)
        qr, kr, ke, qg = q * e_q, kk * e_k, kk * e_e, q * e_g
        qrb, krb, keb, qgb = qr.astype(BF16), kr.astype(BF16), ke.astype(BF16), qg.astype(BF16)
        vb = zi.astype(BF16)
        dob = do.astype(BF16)
        lower, upper = msk_ref[0] > 0.5, msk_ref[1] > 0.5
        da = [jnp.where(lower, _dot_nt(dob[:, cs], vb[:, cs]), 0.0).astype(BF16) for cs in cols]
        a_t = [jnp.where(upper, _dot_nt(krb[:, cs], qrb[:, cs]), 0.0).astype(BF16) for cs in cols]
        da_t = [jnp.where(upper, _dot_nt(vb[:, cs], dob[:, cs]), 0.0).astype(BF16) for cs in cols]
        u_cat = [_dot_tn(dob[:, cs], _block_diag(qgb[:, cs], nc)) for cs in cols]
        dqr = [_dot(da[h], krb[:, cols[h]]) for h in heads]
        dkr = [_dot(da_t[h], qrb[:, cols[h]]) for h in heads]
        dv = [_dot(a_t[h], dob[:, cols[h]]) for h in heads]
        dsn_rows, dsn_cols, ddecay = [], [], [[None] * H for _ in range(nc)]
        for h in heads:
            dsn = ds_ref[h]
            dsn_b = [None] * nc
            for c in reversed(range(nc)):
                decay = jnp.exp(G[(c + 1) * CHUNK - 1:(c + 1) * CHUNK, cols[h]])
                dsn_b[c] = dsn.astype(BF16)
                ddecay[c][h] = jnp.sum(dsn * st_ref[c, h], axis=0, keepdims=True) * decay
                dsn = u_cat[h][:, c * W:(c + 1) * W] + dsn * decay
            ds_ref[h] = dsn
            dsn_rows.append(jnp.concatenate(dsn_b, axis=0))
            dsn_cols.append(jnp.concatenate(dsn_b, axis=1))
        st_rows = [jnp.concatenate([st_ref[c, h].astype(BF16) for c in range(nc)], axis=0) for h in heads]
        dqg = [_dot(_block_diag(dob[:, cols[h]], nc), st_rows[h]) for h in heads]
        dke = [_dot(_block_diag(vb[:, cols[h]], nc), dsn_rows[h]) for h in heads]
        dv = [dv[h] + _dot_nt(_block_diag(keb[:, cols[h]], nc), dsn_cols[h]) for h in heads]
        dqr, dkr, dqg, dke, dv = (jnp.concatenate(parts, axis=1) for parts in (dqr, dkr, dqg, dke, dv))
        t_mid, t_qg, t_ke = dkr * kr - dqr * qr, dqg * qg, dke * ke
        dq = dqr * e_q + dqg * e_g
        dk = dkr * e_k + dke * e_e
        crow = lax.broadcasted_iota(jnp.int32, (CHUNK, 1), 0)
        ends = []
        for c in range(nc):
            sl = slice(c * CHUNK, (c + 1) * CHUNK)
            dgm = jnp.sum(t_mid[sl], axis=0, keepdims=True)
            dgl = jnp.sum(t_ke[sl], axis=0, keepdims=True) + jnp.concatenate(ddecay[c], axis=1)
            ends.append(jnp.where(crow == CHUNK // 2 - 1, dgm, 0.0) + jnp.where(crow == CHUNK - 1, dgl, 0.0))
        dG = t_qg - t_ke - t_mid + jnp.concatenate(ends, axis=0)
        dlogf = _tri_apply(tri_ref[1], dG)
        df = dlogf / f - dk
        du_ref[1] = (dq * (sq * (1.0 + zq * (1.0 - sq)))).astype(BF16)
        du_ref[2] = (df * (1.0 - lb) * (sig * (1.0 - sig))).astype(BF16)
        du_ref[3] = dv.astype(BF16)
        dlb = jnp.sum(df * (1.0 - sig), axis=0, keepdims=True) * (lb * (1.0 - lb))
        for h in heads:
            accumulate(dgn_ref, h, dgn[:, cols[h]])
            accumulate(dlb_ref, h, dlb[:, cols[h]])

    def rb(s, i):
        return s * tps + (tps - 1 - i)

    def per_head_spec(*shape):
        return pl.BlockSpec((H,) + shape, lambda g, s, i: (g,) + (0,) * len(shape))

    vec, mat = per_head_spec(1, W), per_head_spec(W, W)
    return pl.pallas_call(
        body, name="mixer_bwd", grid=(4 // H, seqs, tps),
        in_specs=[pl.BlockSpec((5, tm, H * W), lambda g, s, i: (0, rb(s, i), g)),
                  pl.BlockSpec((None, POOL_HALO, H * W), lambda g, s, i: (0, jnp.maximum(rb(s, i) * hb - 1, 0), g)),
                  pl.BlockSpec((2, tm, H * W), lambda g, s, i: (0, rb(s, i), g)),
                  pl.BlockSpec((tm, H * W), lambda g, s, i: (rb(s, i), g)),
                  pl.BlockSpec((nc, H, W, W), lambda g, s, i: (rb(s, i), g, 0, 0)),
                  mat, vec, per_head_spec(2, W), vec, _tri_spec(tm), _tri_spec(tm),
                  pl.BlockSpec(after.shape, lambda g, s, i: (0, 0))],
        out_specs=[pl.BlockSpec((5, tm, H * W), lambda g, s, i: (0, rb(s, i), g)), mat, vec, vec, vec],
        out_shape=[jax.ShapeDtypeStruct((5, T, 4 * W), BF16),
                   jax.ShapeDtypeStruct((4, W, W), F32),
                   jax.ShapeDtypeStruct((4, 1, W), F32),
                   jax.ShapeDtypeStruct((4, 1, W), F32),
                   jax.ShapeDtypeStruct((4, 1, W), F32)],
        scratch_shapes=[pltpu.VMEM((H, POOL_HALO, W), F32),
                        pltpu.VMEM((H, 4, tm + 32, W), F32),
                        pltpu.VMEM((H, W, W), F32)],
        compiler_params=_cparams(("arbitrary", "arbitrary", "arbitrary")),
    )(u5, u5, dy2, o_pre, st_prev, pool_w_bf, scale4, theta4, gn4, tri_bf, tri_f, after)


def _attn_probs(q, k, hd):
    s = _dot_nt(q, k) * (1.0 / (hd ** 0.5))
    e = jnp.exp(s - jnp.max(s, axis=-1, keepdims=True))
    return e * (1.0 / jnp.sum(e, axis=-1, keepdims=True))


def attn_fwd(q, kv3, *, seqs, seq_len, n_mem, tm):
    T, D = q.shape
    hd = D // XATTN_HEADS
    tps = seq_len // tm

    cols = [slice(h * hd, (h + 1) * hd) for h in range(XATTN_HEADS)]

    def body(q_ref, kv_ref, o_ref):
        p = [_attn_probs(q_ref[:, cs], kv_ref[0, :, cs], hd) for cs in cols]
        for h, cs in enumerate(cols):
            o_ref[:, cs] = _dot(p[h].astype(BF16), kv_ref[1, :, cs]).astype(BF16)

    return pl.pallas_call(
        body, name="attn_fwd", grid=(seqs, tps),
        in_specs=[pl.BlockSpec((tm, D), lambda b, i: (b * tps + i, 0)),
                  pl.BlockSpec((2, n_mem, D), lambda b, i: (0, b, 0))],
        out_specs=pl.BlockSpec((tm, D), lambda b, i: (b * tps + i, 0)),
        out_shape=jax.ShapeDtypeStruct((T, D), BF16),
        compiler_params=_cparams(("parallel", "arbitrary")),
    )(q, kv3)


def attn_bwd(q, kv3, do, *, seqs, seq_len, n_mem, tm):
    T, D = q.shape
    hd = D // XATTN_HEADS
    tps = seq_len // tm

    cols = [slice(h * hd, (h + 1) * hd) for h in range(XATTN_HEADS)]

    def body(q_ref, kv_ref, do_ref, dq_ref, dkv_ref):
        i = pl.program_id(1)

        @pl.when(i == 0)
        def _():
            dkv_ref[...] = jnp.zeros_like(dkv_ref)

        p = [_attn_probs(q_ref[:, cs], kv_ref[0, :, cs], hd) for cs in cols]
        dp = [_dot_nt(do_ref[:, cs], kv_ref[1, :, cs]) for cs in cols]
        ds = [(p[h] * (dp[h] - jnp.sum(dp[h] * p[h], axis=-1, keepdims=True)) * (1.0 / (hd ** 0.5))).astype(BF16)
              for h in range(XATTN_HEADS)]
        for h, cs in enumerate(cols):
            dq_ref[:, cs] = _dot(ds[h], kv_ref[0, :, cs]).astype(BF16)
            dkv_ref[0, :, cs] += _dot_tn(ds[h], q_ref[:, cs])
            dkv_ref[1, :, cs] += _dot_tn(p[h].astype(BF16), do_ref[:, cs])

    qspec = pl.BlockSpec((tm, D), lambda b, i: (b * tps + i, 0))
    kvspec = pl.BlockSpec((2, n_mem, D), lambda b, i: (0, b, 0))
    return pl.pallas_call(
        body, name="attn_bwd", grid=(seqs, tps),
        in_specs=[qspec, kvspec, qspec],
        out_specs=[qspec, kvspec],
        out_shape=[jax.ShapeDtypeStruct((T, D), BF16), jax.ShapeDtypeStruct((2, seqs * n_mem, D), F32)],
        compiler_params=_cparams(("parallel", "arbitrary")),
    )(q, kv3, do)


def final_loss(h, g, target, *, tm):
    T, D = h.shape

    def body(h_ref, g_ref, t_ref, dh_ref, dhb_ref, ls_ref, dg_ref):
        i = pl.program_id(0)
        x = h_ref[...]
        gv = g_ref[...]
        r = lax.rsqrt(jnp.mean(x * x, axis=-1, keepdims=True) + EPS)
        xr = x * r
        d = xr * gv - t_ref[...]
        dy = d * (1.0 / D)
        dyg = dy * gv
        dx = r * (dyg - xr * jnp.mean(dyg * xr, axis=-1, keepdims=True))
        dh_ref[...] = dx
        dhb_ref[...] = dx.astype(BF16)
        ls = jnp.sum(d * d, axis=0, keepdims=True)
        dg = jnp.sum(dy * xr, axis=0, keepdims=True)

        @pl.when(i == 0)
        def _():
            ls_ref[...] = ls
            dg_ref[...] = dg

        @pl.when(i > 0)
        def _():
            ls_ref[...] += ls
            dg_ref[...] += dg

    row = pl.BlockSpec((tm, D), lambda i: (i, 0))
    vec = pl.BlockSpec((1, D), lambda i: (0, 0))
    return pl.pallas_call(
        body, name="final_loss", grid=(T // tm,),
        in_specs=[row, vec, row], out_specs=[row, row, vec, vec],
        out_shape=[jax.ShapeDtypeStruct((T, D), F32), jax.ShapeDtypeStruct((T, D), BF16),
                   jax.ShapeDtypeStruct((1, D), F32), jax.ShapeDtypeStruct((1, D), F32)],
        compiler_params=_cparams(("arbitrary",)),
    )(h, g, target)


def _my_place():
    return lax.axis_index("x"), lax.axis_index("y"), lax.axis_index("c")


def _slot_of(px, py, pc):
    return 4 * px + 2 * py + pc


def _peer(k, x, y, c):
    return (1 - x if (k >> 2) & 1 else x, 1 - y if (k >> 1) & 1 else y, 1 - c if k & 1 else c)


def _split_copies(src_refs, land_refs, send_sems, recv_sems, scatter):
    x, y, c = _my_place()
    mine = _slot_of(x, y, c)
    copies = []
    for a, (src, land) in enumerate(zip(src_refs, land_refs)):
        for k in range(1, N_DEV):
            peer = _peer(k, x, y, c)
            copies.append(pltpu.make_async_remote_copy(
                src_ref=src.at[_slot_of(*peer)] if scatter else src, dst_ref=land.at[mine],
                send_sem=send_sems.at[a * N_PEERS + k - 1], recv_sem=recv_sems.at[a * N_PEERS + k - 1],
                device_id=peer, device_id_type=MESH))
    return copies


def split_start(groups, *, name, scatter):
    sizes = [len(srcs) for srcs, _ in groups]
    n_arr = sum(sizes)
    flat = [a for srcs, lands in groups for a in list(srcs) + list(lands)]

    def body(*refs):
        ins = refs[:2 * n_arr]
        sems = refs[4 * n_arr:4 * n_arr + 2 * len(groups)]
        token = refs[-1]
        at = 0
        for gi, n in enumerate(sizes):
            for cp in _split_copies(ins[at:at + n], ins[at + n:at + 2 * n], sems[2 * gi], sems[2 * gi + 1], scatter):
                cp.start()
            at += 2 * n
        token[...] = jnp.zeros_like(token)

    sem_shapes = []
    for n in sizes:
        sem_shapes += [pltpu.SemaphoreType.DMA((n * N_PEERS,))] * 2
    outs = pl.pallas_call(
        body, name=name,
        out_shape=tuple(pltpu.HBM(a.shape, a.dtype) for a in flat) + tuple(sem_shapes)
        + (jax.ShapeDtypeStruct((8, 128), F32),),
        in_specs=(HBM,) * len(flat),
        out_specs=(HBM,) * len(flat) + (SEM,) * len(sem_shapes) + (pl.BlockSpec(memory_space=pltpu.VMEM),),
        input_output_aliases={i: i for i in range(len(flat))},
        compiler_params=pltpu.CompilerParams(has_side_effects=pltpu.SideEffectType.DATAFLOW_SIDE_EFFECTING),
    )(*[pltpu.with_memory_space_constraint(a, pltpu.HBM) for a in flat])
    thru, sems, token = outs[:len(flat)], outs[len(flat):-1], outs[-1]
    started, at = [], 0
    for gi, n in enumerate(sizes):
        started.append((sems[2 * gi], sems[2 * gi + 1], thru[at:at + n], thru[at + n:at + 2 * n]))
        at += 2 * n
    return started, token


def split_wait(started, after, *, name, scatter):
    sizes = [len(g[2]) for g in started]
    n_arr = sum(sizes)
    flat = [a for g in started for a in list(g[2]) + list(g[3])]
    sems = [s for g in started for s in g[:2]]

    def body(*refs):
        ins = refs[:2 * n_arr]
        sem_refs = refs[2 * n_arr:2 * n_arr + len(sems)]
        at = 0
        for gi, n in enumerate(sizes):
            for cp in _split_copies(ins[at:at + n], ins[at + n:at + 2 * n], sem_refs[2 * gi], sem_refs[2 * gi + 1], scatter):
                cp.wait_send()
                cp.wait_recv()
            at += 2 * n

    outs = pl.pallas_call(
        body, name=name,
        out_shape=tuple(pltpu.HBM(a.shape, a.dtype) for a in flat),
        in_specs=(HBM,) * len(flat) + (SEM,) * len(sems) + (pl.BlockSpec(memory_space=pl.ANY),),
        out_specs=(HBM,) * len(flat),
        input_output_aliases={i: i for i in range(len(flat))},
        compiler_params=pltpu.CompilerParams(has_side_effects=pltpu.SideEffectType.DATAFLOW_SIDE_EFFECTING),
    )(*flat, *sems, after)
    done, at = [], 0
    for n in sizes:
        done.append((outs[at:at + n], outs[at + n:at + 2 * n]))
        at += 2 * n
    return done


SIBLING = 1
CHIP_PEERS = (2, 4, 6)
_SIDE_EFFECTS = pltpu.CompilerParams(has_side_effects=pltpu.SideEffectType.DATAFLOW_SIDE_EFFECTING)


def _chip_level_copies(src, land, send_sems, recv_sems):
    x, y, c = _my_place()
    return [pltpu.make_async_remote_copy(
        src_ref=src, dst_ref=land.at[_slot_of(x, y, c)], send_sem=send_sems.at[j], recv_sem=recv_sems.at[j],
        device_id=_peer(k, x, y, c), device_id_type=MESH) for j, k in enumerate((SIBLING,) + CHIP_PEERS)]


def _pass_on_copies(land, send_sems, recv_sems, receiving):
    x, y, c = _my_place()
    copies = []
    for j, k in enumerate(CHIP_PEERS):
        slot = _slot_of(*_peer(k ^ SIBLING if receiving else k, x, y, c))
        copies.append(pltpu.make_async_remote_copy(
            src_ref=land.at[slot], dst_ref=land.at[slot], send_sem=send_sems.at[j], recv_sem=recv_sems.at[j],
            device_id=_peer(SIBLING, x, y, c), device_id_type=MESH))
    return copies


def gather2_start(src, land, *, name):
    def body(src_ref, land_ref, src_out, land_out, send_sems, recv_sems, token):
        for cp in _chip_level_copies(src_ref, land_ref, send_sems, recv_sems):
            cp.start()
        token[...] = jnp.zeros_like(token)

    n = 1 + len(CHIP_PEERS)
    src_t, land_t, send_sems, recv_sems, token = pl.pallas_call(
        body, name=name,
        out_shape=(pltpu.HBM(src.shape, src.dtype), pltpu.HBM(land.shape, land.dtype),
                   pltpu.SemaphoreType.DMA((n,)), pltpu.SemaphoreType.DMA((n,)), jax.ShapeDtypeStruct((8, 128), F32)),
        in_specs=(HBM, HBM), out_specs=(HBM, HBM, SEM, SEM, pl.BlockSpec(memory_space=pltpu.VMEM)),
        input_output_aliases={0: 0, 1: 1}, compiler_params=_SIDE_EFFECTS,
    )(pltpu.with_memory_space_constraint(src, pltpu.HBM), pltpu.with_memory_space_constraint(land, pltpu.HBM))
    return (src_t, land_t, send_sems, recv_sems), token


def gather2_pass_on(started, after, *, name):
    src, land, send_a, recv_a = started

    def body(src_ref, land_ref, send_a_ref, recv_a_ref, after_ref, land_out, send_b, recv_b):
        for cp in _chip_level_copies(src_ref, land_ref, send_a_ref, recv_a_ref):
            cp.wait_send()
            cp.wait_recv()
        for cp in _pass_on_copies(land_ref, send_b, recv_b, False):
            cp.start()

    n = len(CHIP_PEERS)
    land_t, send_b, recv_b = pl.pallas_call(
        body, name=name,
        out_shape=(pltpu.HBM(land.shape, land.dtype), pltpu.SemaphoreType.DMA((n,)), pltpu.SemaphoreType.DMA((n,))),
        in_specs=(HBM, HBM, SEM, SEM, pl.BlockSpec(memory_space=pl.ANY)), out_specs=(HBM, SEM, SEM),
        input_output_aliases={1: 0}, compiler_params=_SIDE_EFFECTS,
    )(src, land, send_a, recv_a, after)
    return land_t, send_b, recv_b


def gather2_wait(passed, *, name):
    land, send_b, recv_b = passed

    def body(land_ref, send_ref, recv_ref, land_out):
        for cp in _pass_on_copies(land_ref, send_ref, recv_ref, False):
            cp.wait_send()
        for cp in _pass_on_copies(land_ref, send_ref, recv_ref, True):
            cp.wait_recv()

    return pl.pallas_call(
        body, name=name, out_shape=pltpu.HBM(land.shape, land.dtype),
        in_specs=(HBM, SEM, SEM), out_specs=HBM,
        input_output_aliases={0: 0}, compiler_params=_SIDE_EFFECTS,
    )(land, send_b, recv_b)


def allgather_small(bufs):
    n = len(bufs)

    def body(*refs):
        srcs, outs = refs[:n], refs[n:2 * n]
        send_sems, recv_sems, local_sems = refs[2 * n:]
        x, y, c = _my_place()
        mine = _slot_of(x, y, c)
        local = [pltpu.make_async_copy(s, o.at[mine], local_sems.at[a]) for a, (s, o) in enumerate(zip(srcs, outs))]
        for cp in local:
            cp.start()
        copies = _split_copies(srcs, outs, send_sems, recv_sems, False)
        for cp in copies:
            cp.start()
        for cp in copies:
            cp.wait()
        for cp in local:
            cp.wait()

    return pl.pallas_call(
        body, name="allgather_small",
        out_shape=[jax.ShapeDtypeStruct((N_DEV,) + b.shape, b.dtype) for b in bufs],
        in_specs=[HBM] * n, out_specs=[HBM] * n,
        scratch_shapes=[pltpu.SemaphoreType.DMA((n * N_PEERS,)), pltpu.SemaphoreType.DMA((n * N_PEERS,)),
                        pltpu.SemaphoreType.DMA((n,))],
    )(*bufs)


def _adamw_math(g, w, m, v):
    c1 = 1.0 - ADAM_B1 ** ADAM_STEP
    c2 = 1.0 - ADAM_B2 ** ADAM_STEP
    nm = ADAM_B1 * m + (1.0 - ADAM_B1) * g
    nv = ADAM_B2 * v + (1.0 - ADAM_B2) * (g * g)
    delta = -ADAM_LR * ((nm / c1) / (jnp.sqrt(nv / c2) + ADAM_EPS) + ADAM_WD * w)
    return delta, nm, nv


def adamw_sharded(me, own, recv, w, m, v, *, name, tr):
    R, C = w.shape

    def body(me_ref, *refs):
        parts = refs[:N_DEV]
        w_ref, m_ref, v_ref, g_ref, d_ref, nm_ref, nv_ref = refs[N_DEV:]
        g = parts[0][...].astype(F32)
        for p in parts[1:]:
            g = g + p[...].astype(F32)
        g_ref[...] = g
        d_ref[...], nm_ref[...], nv_ref[...] = _adamw_math(g, w_ref[...], m_ref[...], v_ref[...])

    def slab(k):
        return pl.BlockSpec((None, tr, C), lambda i, me_ref: (me_ref[0] ^ k, i, 0))

    blk = pl.BlockSpec((tr, C), lambda i, me_ref: (i, 0))
    out = jax.ShapeDtypeStruct((R, C), F32)
    return pl.pallas_call(
        body, name=name,
        grid_spec=pltpu.PrefetchScalarGridSpec(
            num_scalar_prefetch=1, grid=(R // tr,),
            in_specs=[slab(k) for k in range(N_DEV)] + [blk, blk, blk],
            out_specs=[blk, blk, blk, blk]),
        out_shape=[out, out, out, out],
        compiler_params=_cparams(("parallel",)),
    )(me, own, *([recv] * N_PEERS), w, m, v)


def adamw_replicated(parts, ws, ms, vs, rows):
    n_buf, n_par = len(parts), len(ws)

    def body(*refs):
        p_refs = refs[:n_buf]
        w_refs = refs[n_buf:n_buf + n_par]
        m_refs = refs[n_buf + n_par:n_buf + 2 * n_par]
        v_refs = refs[n_buf + 2 * n_par:n_buf + 3 * n_par]
        outs = refs[n_buf + 3 * n_par:]
        sums = []
        for p in p_refs:
            g = p[0]
            for s in range(1, N_DEV):
                g = g + p[s]
            sums.append(g)
        for j, (b, r0, nr) in enumerate(rows):
            g = sums[b][r0:r0 + nr]
            delta, nm, nv = _adamw_math(g, w_refs[j][...], m_refs[j][...], v_refs[j][...])
            outs[j][...] = g
            outs[n_par + j][...] = delta
            outs[2 * n_par + j][...] = nm
            outs[3 * n_par + j][...] = nv

    shapes = [jax.ShapeDtypeStruct(w.shape, F32) for w in ws]
    outs = pl.pallas_call(
        body, name="adamw_replicated", out_shape=shapes * 4,
        compiler_params=pltpu.CompilerParams(vmem_limit_bytes=V7X_VMEM_LIMIT),
    )(*parts, *ws, *ms, *vs)
    return outs[:n_par], outs[n_par:2 * n_par], outs[2 * n_par:3 * n_par], outs[3 * n_par:]


BIG = ("w_in", "w_out", "xw_q", "xw_kv", "xw_o", "w_up", "w_down")
COL_SHARDED = ("w_in", "xw_kv", "w_up")
WEIGHTS = ("norm_mix", "w_in", "pool_w", "pool_scale", "lb_theta", "hgrn_norm", "w_out", "norm_xq",
           "norm_mem", "xw_q", "xw_kv", "xw_o", "norm_mlp", "w_up", "w_down", "norm_final")
SMALL = (("pool_w", (4 * HEAD_W, HEAD_W), 0, 0),
         ("norm_mix", (1, 1024), 1, 0), ("norm_xq", (1, 1024), 1, 1), ("norm_mem", (1, 1024), 1, 2),
         ("norm_mlp", (1, 1024), 1, 3), ("norm_final", (1, 1024), 1, 4),
         ("pool_scale", (1, 512), 2, 0), ("hgrn_norm", (1, 512), 2, 1), ("lb_theta", (2, 512), 2, 2))


def _pad_rows(a, rows):
    return jnp.concatenate([a, jnp.zeros((rows - a.shape[0], a.shape[1]), a.dtype)], axis=0)


def kernel(x, mem, norm_mix, w_in, pool_w, pool_scale, lb_theta, hgrn_norm, w_out, norm_xq, norm_mem, xw_q, xw_kv, xw_o, norm_mlp, w_up, w_down, norm_final, loss_target, m_norm_mix, m_w_in, m_pool_w, m_pool_scale, m_lb_theta, m_hgrn_norm, m_w_out, m_norm_xq, m_norm_mem, m_xw_q, m_xw_kv, m_xw_o, m_norm_mlp, m_w_up, m_w_down, m_norm_final, v_norm_mix, v_w_in, v_pool_w, v_pool_scale, v_lb_theta, v_hgrn_norm, v_w_out, v_norm_xq, v_norm_mem, v_xw_q, v_xw_kv, v_xw_o, v_norm_mlp, v_w_up, v_w_down, v_norm_final):
    w = dict(norm_mix=norm_mix, w_in=w_in, pool_w=pool_w, pool_scale=pool_scale, lb_theta=lb_theta,
             hgrn_norm=hgrn_norm, w_out=w_out, norm_xq=norm_xq, norm_mem=norm_mem, xw_q=xw_q, xw_kv=xw_kv,
             xw_o=xw_o, norm_mlp=norm_mlp, w_up=w_up, w_down=w_down, norm_final=norm_final)
    mom = dict(norm_mix=m_norm_mix, w_in=m_w_in, pool_w=m_pool_w, pool_scale=m_pool_scale, lb_theta=m_lb_theta,
               hgrn_norm=m_hgrn_norm, w_out=m_w_out, norm_xq=m_norm_xq, norm_mem=m_norm_mem, xw_q=m_xw_q,
               xw_kv=m_xw_kv, xw_o=m_xw_o, norm_mlp=m_norm_mlp, w_up=m_w_up, w_down=m_w_down,
               norm_final=m_norm_final)
    var = dict(norm_mix=v_norm_mix, w_in=v_w_in, pool_w=v_pool_w, pool_scale=v_pool_scale, lb_theta=v_lb_theta,
               hgrn_norm=v_hgrn_norm, w_out=v_w_out, norm_xq=v_norm_xq, norm_mem=v_norm_mem, xw_q=v_xw_q,
               xw_kv=v_xw_kv, xw_o=v_xw_o, norm_mlp=v_norm_mlp, w_up=v_w_up, w_down=v_w_down,
               norm_final=v_norm_final)

    seqs, seq_len, D = x.shape
    n_mem = mem.shape[1]
    T = seqs * seq_len
    W = HEAD_W
    x2 = x.reshape(T, D)
    mem2 = mem.reshape(seqs * n_mem, D)
    tgt2 = loss_target.reshape(T, D)
    tm_big = min(1024, T)
    tm_mid = min(512, T)
    tm_sq = min(1024, T)
    tm_mix = min(256, seq_len)
    tm_att = min(1024, seq_len)
    tkv = min(512, seqs * n_mem)
    px, py, pc = _my_place()
    me = _slot_of(px, py, pc).astype(jnp.int32)
    me1 = me.reshape(1)

    shard_bf = {n: w[n][0].astype(BF16) for n in BIG}

    def landing(n):
        zone = lax.empty((N_DEV,) + shard_bf[n].shape, BF16)
        return lax.dynamic_update_slice(zone, shard_bf[n][None], (me, 0, 0))

    w_in_started, tok = gather2_start(shard_bf["w_in"], landing("w_in"), name="w_in_gather_start")
    shard_bf["w_out"] = shard_bf["w_out"] + tok[0, 0].astype(BF16)
    ag_groups = (("w_out", "xw_q", "xw_kv", "xw_o"), ("w_up",), ("w_down",))
    ag_started, tok = split_start([([shard_bf[n] for n in grp], [landing(n) for n in grp]) for grp in ag_groups],
                                name="weights_gather_start", scatter=False)

    pool_w_bf = pool_w[0].astype(BF16)
    scale4 = pool_scale.reshape(4, 1, W)
    gn4 = hgrn_norm.reshape(4, 1, W)
    theta4 = lb_theta.reshape(2, 4, W).transpose(1, 0, 2)
    g_final = norm_final.reshape(1, D)

    n1 = prenorm(x2, norm_mix, tok, tm=tm_sq)
    wi3 = gather2_wait(gather2_pass_on(w_in_started, n1, name="w_in_gather_pass_on"), name="w_in_gather_wait")
    full_w_in = wi3.transpose(1, 0, 2).reshape(D, -1)
    u5 = proj_plain(n1, full_w_in, name="in_proj", tm=tm_mid, tn=4 * W, out_dtype=F32, out_slabs=5)
    tri_bf, tri_f = chunk_triangles(tm_mix)
    y2, o_pre, st_prev = mixer_fwd(u5, pool_w_bf, scale4, theta4, gn4, tri_bf, tri_f, seqs=seqs, seq_len=seq_len,
                                   tm=tm_mix)
    (_, (wo3, wq3, wkv3, wao3)), = split_wait(ag_started[0:1], y2, name="weights_gather_wait_attn", scatter=False)
    full_w_out, full_xw_q, full_xw_o = wo3.reshape(D, D), wq3.reshape(D, D), wao3.reshape(D, D)
    tn = 4 * W
    h1, n2, q = proj_res_norm(y2, full_w_out, x2, norm_xq, full_xw_q, name="out_q_proj", tm=tm_sq, tn=tn)
    kv3, memn = proj_norm(mem2, norm_mem, wkv3, name="kv_proj", tm=tkv, tn=wkv3.shape[2], out_dtype=BF16,
                          out_slabs=2)
    o_att = attn_fwd(q, kv3, seqs=seqs, seq_len=seq_len, n_mem=n_mem, tm=tm_att)
    h2, n3 = proj_res_norm(o_att, full_xw_o, h1, norm_mlp, name="attn_out_proj", tm=tm_sq, tn=tn)
    (_, (wup3,)), = split_wait(ag_started[1:2], h2, name="weights_gather_wait_up", scatter=False)
    tn_up = wup3.shape[2]
    aa = proj_plain(n3, wup3, name="up_proj", tm=tm_mid, tn=tn_up, relu2=True)
    (_, (wdn3,)), = split_wait(ag_started[2:3], aa, name="weights_gather_wait_down", scatter=False)
    full_w_down = wdn3.reshape(-1, D)
    dh3, dh3b, sq_err, dg_final = proj_res_loss(aa, full_w_down, h2, g_final, tgt2, name="down_proj_loss",
                                                tm=tm_mid, tn=tn)

    def send(parts, name):
        srcs = [p.reshape((N_DEV, -1, p.shape[-1])) for p in parts]
        lands = [lax.empty(s.shape, BF16) for s in srcs]
        started, token = split_start([(srcs, lands)], name=name, scatter=True)
        return started[0], token

    gw_down = wgrad(aa, dh3b, name="down_proj_wgrad", tt=tm_mid, tn=tn)
    dap = back_plain(dh3b, full_w_down, name="down_proj_bwd", tm=tm_mid, tn=tn, out_dtype=BF16, relu2_value=aa)
    gw_up = wgrad(n3, dap, name="up_proj_wgrad", tt=tm_mid, tn=tn_up, out_slabs=N_DEV)
    sent_mlp, tok = send([gw_down, gw_up], "grads_send_mlp")
    dh2, dh2b, do_att, dg_mlp = back_norm(dap, wup3, h2, norm_mlp, dh3, name="up_proj_bwd", tm=tm_mid, tk=tn_up,
                                          w_next=full_xw_o, after=tok)
    gxw_o = wgrad(o_att, dh2b, name="attn_out_proj_wgrad", tt=tm_sq, tn=tn)
    dq, dkv3 = attn_bwd(q, kv3, do_att, seqs=seqs, seq_len=seq_len, n_mem=n_mem, tm=tm_att)
    gxw_q = wgrad(n2, dq, name="q_proj_wgrad", tt=tm_sq, tn=tn)
    gxw_kv = wgrad(memn, dkv3, name="kv_proj_wgrad", tt=tkv, tn=wkv3.shape[2], out_slabs=N_DEV)
    dg_mem = back_norm(dkv3, wkv3, mem2, norm_mem, None, name="kv_proj_bwd", tm=tkv, tk=wkv3.shape[2])
    dh1, dh1b, dy2, dg_xq = back_norm(dq, full_xw_q, h1, norm_xq, dh2, name="q_proj_bwd", tm=tm_mid, tk=D,
                                      w_next=full_w_out, next_dtype=F32, next_slabs=2)
    gw_out = wgrad(y2, dh1b, name="out_proj_wgrad", tt=tm_sq, tn=tn)
    sent_attn, tok = send([gxw_o, gxw_q, gxw_kv, gw_out], "grads_send_attn")
    du5, dpw, dsc, dlb, dgn = mixer_bwd(u5, dy2, o_pre, st_prev, pool_w_bf, scale4, theta4, gn4, tri_bf, tri_f, tok,
                                        seqs=seqs, seq_len=seq_len, tm=tm_mix)
    gw_in = wgrad(n1, du5, name="in_proj_wgrad", tt=tm_mid, tn=tn)
    gw_in_slots = gw_in.reshape(D, N_DEV, -1).transpose(1, 0, 2)
    sent_in, tok = send([gw_in_slots], "grads_send_in")
    dx, dg_mix = back_norm(du5, full_w_in, x2, norm_mix, dh1, name="in_proj_bwd", tm=tm_mid, tk=tn, bf16_copy=False,
                           after=tok)

    dlb_row = dlb.reshape(1, 4 * W)
    buf_vec = _pad_rows(jnp.concatenate([dg_mix, dg_xq, dg_mem, dg_mlp, dg_final, sq_err], axis=0), 8)
    buf_half = _pad_rows(jnp.concatenate([dsc.reshape(1, 4 * W), dgn.reshape(1, 4 * W), dlb_row, -dlb_row], axis=0), 8)
    small_src = [dpw.reshape(4 * W, W), buf_vec, buf_half]
    small_land = [lax.dynamic_update_slice(lax.empty((N_DEV,) + b.shape, F32), b[None], (me, 0, 0))
                  for b in small_src]
    small_started, tok = split_start([(small_src, small_land)], name="small_grads_start", scatter=False)

    done = split_wait([sent_mlp, sent_attn, sent_in], tok, name="grads_wait", scatter=True)
    slots = dict(w_down=(0, 0), w_up=(0, 1), xw_o=(1, 0), xw_q=(1, 1), xw_kv=(1, 2), w_out=(1, 3), w_in=(2, 0))
    own = {n: done[gi][0][ai] for n, (gi, ai) in slots.items()}
    got = {n: done[gi][1][ai] for n, (gi, ai) in slots.items()}
    res = {}
    for n in BIG:
        shp = w[n].shape
        r = adamw_sharded(me1, own[n], got[n], w[n][0], mom[n][0], var[n][0], name="adamw_" + n,
                          tr=min(256, shp[1]))
        for kind, a in zip("gdmv", r):
            res[kind, n] = a.reshape(shp)
    (_, small_parts), = split_wait(small_started, res["g", BIG[-1]], name="small_grads_wait", scatter=False)
    loss = 0.5 * jnp.sum(small_parts[1][:, 5, :]) / D
    r = adamw_replicated(small_parts, [w[n].reshape(v2) for n, v2, _, _ in SMALL],
                         [mom[n].reshape(v2) for n, v2, _, _ in SMALL],
                         [var[n].reshape(v2) for n, v2, _, _ in SMALL],
                         [(b, r0, v2[0]) for _, v2, b, r0 in SMALL])
    for kind, arrs in zip("gdmv", r):
        for (n, _, _, _), a in zip(SMALL, arrs):
            res[kind, n] = a.reshape(w[n].shape)

    out = [loss, dx.reshape(x.shape)]
    for kind in "gdmv":
        out += [res[kind, n] for n in WEIGHTS]
    return tuple(out)
```

```python
import jax
import jax.numpy as jnp
from jax import lax
from jax.experimental import pallas as pl
from jax.experimental.pallas import tpu as pltpu

F32 = jnp.float32
BF16 = jnp.bfloat16
EPS = 1e-6
CHUNK = 64
POOL_HALO = 16
HEAD_W = 128
HEADS_PER_STEP = 4
XATTN_HEADS = 4
N_DEV = 8
N_PEERS = N_DEV - 1
ADAM_LR = 0.001
ADAM_B1 = 0.9
ADAM_B2 = 0.999
ADAM_EPS = 1e-08
ADAM_WD = 0.01
ADAM_STEP = 10
V7X_VMEM_LIMIT = 52 * 1024 * 1024
MESH = pl.DeviceIdType.MESH
HBM = pl.BlockSpec(memory_space=pltpu.HBM)
SEM = pl.BlockSpec(memory_space=pltpu.SEMAPHORE)


def _cparams(dims):
    return pltpu.CompilerParams(dimension_semantics=dims, vmem_limit_bytes=V7X_VMEM_LIMIT)


def _sigmoid(v):
    return 0.5 * jnp.tanh(0.5 * v) + 0.5


def _dot(a, b):
    return jnp.dot(a, b, preferred_element_type=F32)


def _dot_nt(a, b):
    return lax.dot_general(a, b, (((1,), (1,)), ((), ())), preferred_element_type=F32)


def _dot_tn(a, b):
    return lax.dot_general(a, b, (((0,), (0,)), ((), ())), preferred_element_type=F32)


def _tri_apply(tri, v):
    hi = v.astype(BF16)
    lo = (v - hi.astype(F32)).astype(BF16)
    return _dot(tri, hi) + _dot(tri, lo)


def _mat_shape(a):
    return a.shape if a.ndim == 2 else (a.shape[1], a.shape[0] * a.shape[2])


def _tile_spec(a, rows, cols, row_of, col_of):
    if a.ndim == 2:
        return pl.BlockSpec((rows, cols), lambda *g: (row_of(*g), col_of(*g)))
    per = a.shape[2] // cols
    return pl.BlockSpec((None, rows, cols), lambda *g: (col_of(*g) // per, row_of(*g), col_of(*g) % per))


def _out_struct(rows, n, slabs, dtype):
    return jax.ShapeDtypeStruct((rows, n) if slabs is None else (slabs, rows, n // slabs), dtype)


def norm_mm(h, g, w, *, name, tm, tn, out_dtype, out_slabs=None):
    T, D = h.shape
    N = _mat_shape(w)[1]
    o_shape = _out_struct(T, N, out_slabs, out_dtype)

    def body(h_ref, g_ref, w_ref, o_ref, n_ref):
        @pl.when(pl.program_id(1) == 0)
        def _():
            x = h_ref[...]
            r = lax.rsqrt(jnp.mean(x * x, axis=-1, keepdims=True) + EPS)
            n_ref[...] = (x * r * g_ref[...]).astype(BF16)

        o_ref[...] = _dot(n_ref[...], w_ref[...]).astype(o_ref.dtype)

    return pl.pallas_call(
        body, name=name, grid=(T // tm, N // tn),
        in_specs=[pl.BlockSpec((tm, D), lambda i, j: (i, 0)),
                  pl.BlockSpec((1, D), lambda i, j: (0, 0)),
                  _tile_spec(w, D, tn, lambda i, j: 0, lambda i, j: j)],
        out_specs=[_tile_spec(o_shape, tm, tn, lambda i, j: i, lambda i, j: j),
                   pl.BlockSpec((tm, D), lambda i, j: (i, 0))],
        out_shape=[o_shape, jax.ShapeDtypeStruct((T, D), BF16)],
        compiler_params=_cparams(("parallel", "arbitrary")),
    )(h, g, w)


def mm_nn(a, w, res, *, name, tm, tn, tk, relu2=False):
    T, K = _mat_shape(a)
    N = w.shape[1]
    nk = K // tk

    def body(a_ref, w_ref, r_ref, o_ref, acc_ref):
        k = pl.program_id(2)
        av = a_ref[...]
        if relu2:
            av = jnp.maximum(av, 0.0)
            av = av * av
        part = _dot(av.astype(BF16), w_ref[...])

        @pl.when(k == 0)
        def _():
            acc_ref[...] = part

        @pl.when(k > 0)
        def _():
            acc_ref[...] += part

        @pl.when(k == nk - 1)
        def _():
            o_ref[...] = r_ref[...] + acc_ref[...]

    return pl.pallas_call(
        body, name=name, grid=(T // tm, N // tn, nk),
        in_specs=[_tile_spec(a, tm, tk, lambda i, j, k: i, lambda i, j, k: k),
                  pl.BlockSpec((tk, tn), lambda i, j, k: (k, j)),
                  pl.BlockSpec((tm, tn), lambda i, j, k: (i, j))],
        out_specs=pl.BlockSpec((tm, tn), lambda i, j, k: (i, j)),
        out_shape=jax.ShapeDtypeStruct((T, N), F32),
        scratch_shapes=[pltpu.VMEM((tm, tn), F32)],
        compiler_params=_cparams(("parallel", "parallel", "arbitrary")),
    )(a, w, res)


def mm_nt(a, w, *, name, tm, tn, tk, out_dtype, out_slabs=None, relu2_of=None, after=None):
    T, K = _mat_shape(a)
    nk = K // tk
    N = w.shape[0]
    has_z = relu2_of is not None
    o_shape = _out_struct(T, N, out_slabs, out_dtype)

    def body(*refs):
        a_ref, w_ref = refs[0], refs[1]
        z_ref = refs[2] if has_z else None
        o_ref, acc_ref = refs[-2], refs[-1]
        k = pl.program_id(2)
        part = _dot_nt(a_ref[...].astype(BF16), w_ref[...])

        @pl.when(k == 0)
        def _():
            acc_ref[...] = part

        @pl.when(k > 0)
        def _():
            acc_ref[...] += part

        @pl.when(k == nk - 1)
        def _():
            out = acc_ref[...]
            if has_z:
                out = out * (2.0 * jnp.maximum(z_ref[...], 0.0))
            o_ref[...] = out.astype(o_ref.dtype)

    in_specs = [_tile_spec(a, tm, tk, lambda i, j, k: i, lambda i, j, k: k),
                pl.BlockSpec((tn, tk), lambda i, j, k: (j, k))]
    args = [a, w]
    if has_z:
        in_specs.append(pl.BlockSpec((tm, tn), lambda i, j, k: (i, j)))
        args.append(relu2_of)
    if after is not None:
        in_specs.append(pl.BlockSpec(after.shape, lambda i, j, k: (0, 0)))
        args.append(after)
    return pl.pallas_call(
        body, name=name, grid=(T // tm, N // tn, nk),
        in_specs=in_specs,
        out_specs=_tile_spec(o_shape, tm, tn, lambda i, j, k: i, lambda i, j, k: j),
        out_shape=o_shape,
        scratch_shapes=[pltpu.VMEM((tm, tn), F32)],
        compiler_params=_cparams(("parallel", "parallel", "arbitrary")),
    )(*args)


def mm_nt_normbwd(a, w, h, g, dres, *, name, tm, tk, after=None):
    T, K = _mat_shape(a)
    nk = K // tk
    D = h.shape[1]
    with_dh = dres is not None

    def body(*refs):
        a_ref, w_ref, h_ref, g_ref = refs[:4]
        if with_dh:
            r_ref = refs[4]
            dh_ref, dhb_ref, dg_ref, acc_ref = refs[-4:]
        else:
            dg_ref, acc_ref = refs[-2:]
        i = pl.program_id(0)
        k = pl.program_id(1)
        part = _dot_nt(a_ref[...].astype(BF16), w_ref[...])

        @pl.when(k == 0)
        def _():
            acc_ref[...] = part

        @pl.when(k > 0)
        def _():
            acc_ref[...] += part

        @pl.when(k == nk - 1)
        def _():
            dn = acc_ref[...]
            x = h_ref[...]
            r = lax.rsqrt(jnp.mean(x * x, axis=-1, keepdims=True) + EPS)
            xr = x * r
            dgp = jnp.sum(dn * xr, axis=0, keepdims=True)

            @pl.when(i == 0)
            def _():
                dg_ref[...] = dgp

            @pl.when(i > 0)
            def _():
                dg_ref[...] += dgp

            if with_dh:
                dyg = dn * g_ref[...]
                dx = r * (dyg - xr * jnp.mean(dyg * xr, axis=-1, keepdims=True))
                out = r_ref[...] + dx
                dh_ref[...] = out
                dhb_ref[...] = out.astype(BF16)

    row = pl.BlockSpec((tm, D), lambda i, k: (i, 0))
    vec = pl.BlockSpec((1, D), lambda i, k: (0, 0))
    in_specs = [_tile_spec(a, tm, tk, lambda i, k: i, lambda i, k: k),
                _tile_spec(w, D, tk, lambda i, k: 0, lambda i, k: k), row, vec]
    args = [a, w, h, g]
    if with_dh:
        in_specs.append(row)
        args.append(dres)
        out_specs = [row, row, vec]
        out_shape = [jax.ShapeDtypeStruct((T, D), F32), jax.ShapeDtypeStruct((T, D), BF16),
                     jax.ShapeDtypeStruct((1, D), F32)]
    else:
        out_specs = vec
        out_shape = jax.ShapeDtypeStruct((1, D), F32)
    if after is not None:
        in_specs.append(pl.BlockSpec(after.shape, lambda i, k: (0, 0)))
        args.append(after)
    return pl.pallas_call(
        body, name=name, grid=(T // tm, nk),
        in_specs=in_specs, out_specs=out_specs, out_shape=out_shape,
        scratch_shapes=[pltpu.VMEM((tm, D), F32)],
        compiler_params=_cparams(("arbitrary", "arbitrary")),
    )(*args)


def mm_tn(a, b, *, name, tt, tko, tn, relu2=False, out_slabs=None):
    T, K = _mat_shape(a)
    N = _mat_shape(b)[1]
    nt = T // tt
    o_shape = _out_struct(K, N, out_slabs, BF16)

    def body(a_ref, b_ref, o_ref, acc_ref):
        t = pl.program_id(2)
        av = a_ref[...]
        if relu2:
            av = jnp.maximum(av, 0.0)
            av = av * av
        part = _dot_tn(av.astype(BF16), b_ref[...].astype(BF16))

        @pl.when(t == 0)
        def _():
            acc_ref[...] = part

        @pl.when(t > 0)
        def _():
            acc_ref[...] += part

        @pl.when(t == nt - 1)
        def _():
            o_ref[...] = acc_ref[...].astype(BF16)

    return pl.pallas_call(
        body, name=name, grid=(K // tko, N // tn, nt),
        in_specs=[_tile_spec(a, tt, tko, lambda kk, j, t: t, lambda kk, j, t: kk),
                  _tile_spec(b, tt, tn, lambda kk, j, t: t, lambda kk, j, t: j)],
        out_specs=_tile_spec(o_shape, tko, tn, lambda kk, j, t: kk, lambda kk, j, t: j),
        out_shape=o_shape,
        scratch_shapes=[pltpu.VMEM((tko, tn), F32)],
        compiler_params=_cparams(("parallel", "parallel", "arbitrary")),
    )(a, b)


def _resident(a):
    nd = a.ndim
    return pl.BlockSpec(a.shape, lambda i: (0,) * nd, pipeline_mode=pl.Buffered(1))


def _row_block(a, tm):
    if a.ndim == 2:
        return pl.BlockSpec((tm, a.shape[1]), lambda i: (i, 0))
    return pl.BlockSpec((a.shape[0], tm, a.shape[2]), lambda i: (0, i, 0))


def _cols(ref, c, width):
    if len(ref.shape) == 2:
        return ref[:, c * width:(c + 1) * width]
    per = ref.shape[2] // width
    if per == 1:
        return ref[c]
    return ref[c // per, :, (c % per) * width:(c % per + 1) * width]


def _set_cols(ref, c, width, val):
    if len(ref.shape) == 2:
        ref[:, c * width:(c + 1) * width] = val
        return
    per = ref.shape[2] // width
    if per == 1:
        ref[c] = val
    else:
        ref[c // per, :, (c % per) * width:(c % per + 1) * width] = val


def _all_cols(ref):
    if len(ref.shape) == 2:
        return ref[...]
    return jnp.concatenate([ref[s] for s in range(ref.shape[0])], axis=1)


def _rms(x):
    return lax.rsqrt(jnp.mean(x * x, axis=-1, keepdims=True) + EPS)


def _row_params():
    return _cparams(("arbitrary",))


def proj_norm(h, g, w, *, name, tm, tn, out_dtype, out_slabs=None):
    T, D = h.shape
    N = _mat_shape(w)[1]
    o_shape = _out_struct(T, N, out_slabs, out_dtype)

    def body(h_ref, g_ref, w_ref, o_ref, n_ref):
        x = h_ref[...]
        n = (x * _rms(x) * g_ref[...]).astype(BF16)
        n_ref[...] = n
        for c in range(N // tn):
            _set_cols(o_ref, c, tn, _dot(n, _cols(w_ref, c, tn)).astype(out_dtype))

    return pl.pallas_call(
        body, name=name, grid=(T // tm,),
        in_specs=[_row_block(h, tm), pl.BlockSpec((1, D), lambda i: (0, 0)), _resident(w)],
        out_specs=[_row_block(o_shape, tm), pl.BlockSpec((tm, D), lambda i: (i, 0))],
        out_shape=[o_shape, jax.ShapeDtypeStruct((T, D), BF16)],
        compiler_params=_row_params(),
    )(h, g, w)


def prenorm(h, g, after, *, tm):
    T, D = h.shape

    def body(h_ref, g_ref, _after_ref, n_ref):
        x = h_ref[...]
        n_ref[...] = (x * _rms(x) * g_ref[...]).astype(BF16)

    row = pl.BlockSpec((tm, D), lambda i: (i, 0))
    return pl.pallas_call(
        body, name="prenorm", grid=(T // tm,),
        in_specs=[row, pl.BlockSpec((1, D), lambda i: (0, 0)), _anchor_spec(after)],
        out_specs=row, out_shape=jax.ShapeDtypeStruct((T, D), BF16),
        compiler_params=_row_params(),
    )(h, g, after)


def proj_plain(a, w, *, name, tm, tn, out_dtype=BF16, out_slabs=None, relu2=False):
    T = a.shape[0]
    N = _mat_shape(w)[1]

    def body(a_ref, w_ref, o_ref):
        av = a_ref[...]
        for c in range(N // tn):
            z = _dot(av, _cols(w_ref, c, tn))
            if relu2:
                z = jnp.maximum(z, 0.0)
                z = z * z
            _set_cols(o_ref, c, tn, z.astype(out_dtype))

    o_shape = _out_struct(T, N, out_slabs, out_dtype)
    return pl.pallas_call(
        body, name=name, grid=(T // tm,),
        in_specs=[_row_block(a, tm), _resident(w)],
        out_specs=_row_block(o_shape, tm), out_shape=o_shape,
        compiler_params=_row_params(),
    )(a, w)


def proj_res_norm(a, w, res, g, w_next=None, *, name, tm, tn):
    T = res.shape[0]
    D = w.shape[1]
    chained = w_next is not None

    def body(*refs):
        a_ref, w_ref, r_ref, g_ref = refs[:4]
        h_ref, n_ref = refs[4 + chained], refs[5 + chained]
        av = _all_cols(a_ref)
        for c in range(D // tn):
            sl = slice(c * tn, (c + 1) * tn)
            h_ref[:, sl] = r_ref[:, sl] + _dot(av, w_ref[:, sl])
        hv = h_ref[...]
        n = (hv * _rms(hv) * g_ref[...]).astype(BF16)
        n_ref[...] = n
        if chained:
            for c in range(D // tn):
                sl = slice(c * tn, (c + 1) * tn)
                refs[-1][:, sl] = _dot(n, refs[4][:, sl]).astype(BF16)

    row = pl.BlockSpec((tm, D), lambda i: (i, 0))
    half = jax.ShapeDtypeStruct((T, D), BF16)
    return pl.pallas_call(
        body, name=name, grid=(T // tm,),
        in_specs=[_row_block(a, tm), _resident(w), row, pl.BlockSpec((1, D), lambda i: (0, 0))]
        + ([_resident(w_next)] if chained else []),
        out_specs=[row, row] + ([row] if chained else []),
        out_shape=[jax.ShapeDtypeStruct((T, D), F32), half] + ([half] if chained else []),
        compiler_params=_row_params(),
    )(*([a, w, res, g] + ([w_next] if chained else [])))


def proj_res_loss(a, w, res, g, target, *, name, tm, tn):
    T = res.shape[0]
    D = w.shape[1]

    def body(a_ref, w_ref, r_ref, g_ref, t_ref, dh_ref, dhb_ref, ls_ref, dg_ref):
        i = pl.program_id(0)
        gv = g_ref[...]
        ls, dg = 0.0, 0.0
        halves = [slice(s * (tm // 2), (s + 1) * (tm // 2)) for s in range(2)]
        for rows in halves:
            av = a_ref[rows, :]
            for c in range(D // tn):
                sl = slice(c * tn, (c + 1) * tn)
                dh_ref[rows, sl] = r_ref[rows, sl] + _dot(av, w_ref[:, sl])
        for rows in halves:
            x = dh_ref[rows, :]
            r = _rms(x)
            xr = x * r
            d = xr * gv - t_ref[rows, :]
            dy = d * (1.0 / D)
            dyg = dy * gv
            dx = r * (dyg - xr * jnp.mean(dyg * xr, axis=-1, keepdims=True))
            dh_ref[rows, :] = dx
            dhb_ref[rows, :] = dx.astype(BF16)
            ls = ls + jnp.sum(d * d, axis=0, keepdims=True)
            dg = dg + jnp.sum(dy * xr, axis=0, keepdims=True)

        @pl.when(i == 0)
        def _():
            ls_ref[...] = ls
            dg_ref[...] = dg

        @pl.when(i > 0)
        def _():
            ls_ref[...] += ls
            dg_ref[...] += dg

    row = pl.BlockSpec((tm, D), lambda i: (i, 0))
    vec = pl.BlockSpec((1, D), lambda i: (0, 0))
    return pl.pallas_call(
        body, name=name, grid=(T // tm,),
        in_specs=[_row_block(a, tm), _resident(w), row, vec, row],
        out_specs=[row, row, vec, vec],
        out_shape=[jax.ShapeDtypeStruct((T, D), F32), jax.ShapeDtypeStruct((T, D), BF16),
                   jax.ShapeDtypeStruct((1, D), F32), jax.ShapeDtypeStruct((1, D), F32)],
        compiler_params=_row_params(),
    )(a, w, res, g, target)


def _anchor_spec(after):
    return pl.BlockSpec(after.shape, lambda i: (0, 0))


def back_plain(a, w, *, name, tm, tn, out_dtype, out_slabs=None, relu2_value=None, after=None):
    T = a.shape[0]
    N = w.shape[0]
    has_z = relu2_value is not None
    o_shape = _out_struct(T, N, out_slabs, out_dtype)

    def body(*refs):
        a_ref, w_ref = refs[0], refs[1]
        o_ref = refs[-1]
        av = a_ref[...]
        for c in range(N // tn):
            out = _dot_nt(av, w_ref[c * tn:(c + 1) * tn, :])
            if has_z:
                out = out * (2.0 * jnp.sqrt(refs[2][:, c * tn:(c + 1) * tn]).astype(F32))
            _set_cols(o_ref, c, tn, out.astype(out_dtype))

    in_specs, args = [_row_block(a, tm), _resident(w)], [a, w]
    if has_z:
        in_specs.append(_row_block(relu2_value, tm))
        args.append(relu2_value)
    if after is not None:
        in_specs.append(_anchor_spec(after))
        args.append(after)
    return pl.pallas_call(
        body, name=name, grid=(T // tm,),
        in_specs=in_specs, out_specs=_row_block(o_shape, tm), out_shape=o_shape,
        compiler_params=_row_params(),
    )(*args)


def back_norm(a, w, h, g, dres, *, name, tm, tk, bf16_copy=True, w_next=None, next_dtype=BF16, next_slabs=None,
              after=None):
    T, K = _mat_shape(a)
    D = h.shape[1]
    with_dh = dres is not None
    chained = w_next is not None
    n_in = 4 + with_dh + chained
    tn = 4 * HEAD_W

    def body(*refs):
        a_ref, w_ref, h_ref, g_ref = refs[:4]
        outs = refs[n_in + (after is not None):]
        i = pl.program_id(0)
        if len(w_ref.shape) == 2:
            dn = _dot_nt(_all_cols(a_ref).astype(BF16), w_ref[...])
        else:
            dn = None
            for kc in range(K // tk):
                part = _dot_nt(_cols(a_ref, kc, tk).astype(BF16), _cols(w_ref, kc, tk))
                dn = part if dn is None else dn + part
        x = h_ref[...]
        r = _rms(x)
        xr = x * r
        dgp = jnp.sum(dn * xr, axis=0, keepdims=True)
        dg_ref = outs[-1]

        @pl.when(i == 0)
        def _():
            dg_ref[...] = dgp

        @pl.when(i > 0)
        def _():
            dg_ref[...] += dgp

        if with_dh:
            dyg = dn * g_ref[...]
            out = refs[4][...] + r * (dyg - xr * jnp.mean(dyg * xr, axis=-1, keepdims=True))
            outs[0][...] = out
            outb = out.astype(BF16)
            if bf16_copy:
                outs[1][...] = outb
            if chained:
                wn_ref, nx_ref = refs[5], outs[-2]
                for c in range(wn_ref.shape[0] // tn):
                    _set_cols(nx_ref, c, tn, _dot_nt(outb, wn_ref[c * tn:(c + 1) * tn, :]).astype(next_dtype))

    row = pl.BlockSpec((tm, D), lambda i: (i, 0))
    vec = pl.BlockSpec((1, D), lambda i: (0, 0))
    in_specs, args = [_row_block(a, tm), _resident(w), row, vec], [a, w, h, g]
    out_specs, out_shape = [], []
    if with_dh:
        in_specs.append(row)
        args.append(dres)
        out_specs.append(row)
        out_shape.append(jax.ShapeDtypeStruct((T, D), F32))
        if bf16_copy:
            out_specs.append(row)
            out_shape.append(jax.ShapeDtypeStruct((T, D), BF16))
    if chained:
        in_specs.append(_resident(w_next))
        args.append(w_next)
        nx_shape = _out_struct(T, w_next.shape[0], next_slabs, next_dtype)
        out_specs.append(_row_block(nx_shape, tm))
        out_shape.append(nx_shape)
    out_specs.append(vec)
    out_shape.append(jax.ShapeDtypeStruct((1, D), F32))
    if after is not None:
        in_specs.append(_anchor_spec(after))
        args.append(after)
    outs = pl.pallas_call(
        body, name=name, grid=(T // tm,),
        in_specs=in_specs, out_specs=out_specs, out_shape=out_shape,
        compiler_params=_row_params(),
    )(*args)
    return outs if len(outs) > 1 else outs[0]


def wgrad(a, b, *, name, tt, tn, out_slabs=None):
    T, K = _mat_shape(a)
    N = _mat_shape(b)[1]
    nt = T // tt
    o_shape = _out_struct(K, N, out_slabs, BF16)

    flipped = K > N and out_slabs is None

    def body(a_ref, b_ref, o_ref, acc_ref):
        t = pl.program_id(0)

        @pl.when(t == 0)
        def _():
            acc_ref[...] = jnp.zeros_like(acc_ref)

        if flipped:
            bt = _all_cols(b_ref).astype(BF16).T
            for c in range(K // tn):
                acc_ref[:, c * tn:(c + 1) * tn] += _dot(bt, _cols(a_ref, c, tn).astype(BF16))
        else:
            at = _all_cols(a_ref).astype(BF16).T
            for c in range(N // tn):
                acc_ref[:, c * tn:(c + 1) * tn] += _dot(at, _cols(b_ref, c, tn).astype(BF16))

        @pl.when(t == nt - 1)
        def _():
            if flipped:
                for c in range(K // tn):
                    o_ref[c * tn:(c + 1) * tn, :] = acc_ref[:, c * tn:(c + 1) * tn].T.astype(BF16)
            else:
                for c in range(N // tn):
                    _set_cols(o_ref, c, tn, acc_ref[:, c * tn:(c + 1) * tn].astype(BF16))

    return pl.pallas_call(
        body, name=name, grid=(nt,),
        in_specs=[_row_block(a, tt), _row_block(b, tt)],
        out_specs=_resident(o_shape), out_shape=o_shape,
        scratch_shapes=[pltpu.VMEM((N, K) if flipped else (K, N), F32)],
        compiler_params=_row_params(),
    )(a, b)


def chunk_triangles(tm):
    r = lax.broadcasted_iota(jnp.int32, (tm, tm), 0)
    c = lax.broadcasted_iota(jnp.int32, (tm, tm), 1)
    same = (r // CHUNK) == (c // CHUNK)
    tri = jnp.stack([same & (c <= r), same & (c >= r)]).astype(F32)
    return tri.astype(BF16), tri


def _tri_spec(tm):
    return pl.BlockSpec((2, tm, tm), lambda g, s, i: (0, 0, 0))


def _chunk_row(v, r, nc):
    return jnp.concatenate([jnp.broadcast_to(v[c * CHUNK + r:c * CHUNK + r + 1], (CHUNK, v.shape[1]))
                            for c in range(nc)], axis=0)


def _block_diag(v, nc):
    chunk = lax.broadcasted_iota(jnp.int32, (v.shape[0], 1), 0) // CHUNK
    return jnp.concatenate([jnp.where(chunk == c, v, jnp.zeros_like(v)) for c in range(nc)], axis=1)


def _pool_windows_back(ext_ref, tm):
    n = tm + 32
    ext_ref[1, 8:n] = ext_ref[0, 8:n] + ext_ref[0, 7:n - 1]
    ext_ref[2, 16:n] = ext_ref[1, 16:n] + ext_ref[1, 14:n - 2]
    ext_ref[3, 24:n] = ext_ref[2, 24:n] + ext_ref[2, 20:n - 4]
    s2 = ext_ref[1, 32:n]
    s4 = ext_ref[2, 32:n]
    s8 = ext_ref[3, 32:n]
    s16 = s8 + ext_ref[3, 24:n - 8]
    return s2, s4, s8, s16


def _pool_windows_fwd(ext_ref, tm):
    n = tm + 32
    ext_ref[1, 0:n - 8] = ext_ref[0, 0:n - 8] + ext_ref[0, 1:n - 7]
    ext_ref[2, 0:n - 16] = ext_ref[1, 0:n - 16] + ext_ref[1, 2:n - 14]
    ext_ref[3, 0:n - 24] = ext_ref[2, 0:n - 24] + ext_ref[2, 4:n - 20]
    s2 = ext_ref[1, 0:tm]
    s4 = ext_ref[2, 0:tm]
    s8 = ext_ref[3, 0:tm]
    s16 = s8 + ext_ref[3, 8:tm + 8]
    return s2, s4, s8, s16


def _select_window(g, s2, s4, s8, s16):
    return jnp.where(g == 0, s2, jnp.where(g == 1, s4, jnp.where(g == 2, s8, s16)))


def _pool_count(g, pos):
    width = lax.shift_left(jnp.int32(2), g)
    return jnp.minimum(pos + 1, width).astype(F32)


def _hgrn_gates(zq, zf, th):
    lb = _sigmoid(th[0:1, :] - th[1:2, :])
    sig = _sigmoid(zf)
    f = lb + (1.0 - lb) * sig
    sq = _sigmoid(zq)
    return lb, sig, f, sq


def mixer_fwd(u5, pool_w_bf, scale4, theta4, gn4, tri_bf, tri_f, *, seqs, seq_len, tm):
    T = u5.shape[1]
    tps = seq_len // tm
    nc = tm // CHUNK
    W = HEAD_W

    H = HEADS_PER_STEP
    heads = range(H)

    def body(u_ref, pw_ref, sc_ref, th_ref, gn_ref, tri_ref, msk_ref, y_ref, o_ref, st_ref, halo_ref, ext_ref, s_ref):
        g = pl.program_id(0)
        i = pl.program_id(2)

        @pl.when(i == 0)
        def _():
            halo_ref[...] = jnp.zeros_like(halo_ref)
            s_ref[...] = jnp.zeros_like(s_ref)

        row = lax.broadcasted_iota(jnp.int32, (tm, 1), 0)
        cols = [slice(h * W, (h + 1) * W) for h in heads]

        pooled = []
        for h in heads:
            grp = g * H + h
            up = u_ref[0, :, cols[h]]
            ext_ref[h, 0, 0:16] = jnp.zeros((16, W), F32)
            ext_ref[h, 0, 16:32] = halo_ref[h]
            ext_ref[h, 0, 32:32 + tm] = up
            win = _select_window(grp, *_pool_windows_back(ext_ref.at[h], tm))
            pooled.append((win * (1.0 / _pool_count(grp, i * tm + row)) - up).astype(BF16))
            halo_ref[h] = up[tm - POOL_HALO:tm]
        mixed = [_dot(pooled[h], pw_ref[h]) for h in heads]
        for h in heads:
            y_ref[0, :, cols[h]] = (mixed[h] * sc_ref[h]).astype(BF16)

        zq, zf, zi, zg = u_ref[1], u_ref[2], u_ref[3], u_ref[4]
        th = [th_ref[h] for h in heads]
        lb = jnp.concatenate([_sigmoid(t[0:1, :] - t[1:2, :]) for t in th], axis=1)
        f = lb + (1.0 - lb) * _sigmoid(zf)
        kk = 1.0 - f
        q = zq * _sigmoid(zq)
        G = _tri_apply(tri_ref[0], jnp.log(f))
        Gm, Gl = _chunk_row(G, CHUNK // 2 - 1, nc), _chunk_row(G, CHUNK - 1, nc)
        vb = zi.astype(BF16)
        qrb = (q * jnp.exp(G - Gm)).astype(BF16)
        krb = (kk * jnp.exp(Gm - G)).astype(BF16)
        keb = (kk * jnp.exp(Gl - G)).astype(BF16)
        qgb = (q * jnp.exp(G)).astype(BF16)
        mask = msk_ref[0] > 0.5
        a = [jnp.where(mask, _dot_nt(qrb[:, cols[h]], krb[:, cols[h]]), 0.0).astype(BF16) for h in heads]
        d_st = [_dot_tn(vb[:, cols[h]], _block_diag(keb[:, cols[h]], nc)) for h in heads]
        o_intra = [_dot(a[h], vb[:, cols[h]]) for h in heads]
        st_cat = []
        for h in heads:
            st = s_ref[h]
            states = []
            for c in range(nc):
                states.append(st.astype(BF16))
                st_ref[c, h] = states[-1]
                st = st * jnp.exp(G[(c + 1) * CHUNK - 1:(c + 1) * CHUNK, cols[h]]) + d_st[h][:, c * W:(c + 1) * W]
            s_ref[h] = st
            st_cat.append(jnp.concatenate(states, axis=1))
        o = [o_intra[h] + _dot_nt(_block_diag(qgb[:, cols[h]], nc), st_cat[h]) for h in heads]
        gate = zg * _sigmoid(zg)
        for h in heads:
            o_ref[:, cols[h]] = o[h]
            r = lax.rsqrt(jnp.mean(o[h] * o[h], axis=-1, keepdims=True) + EPS)
            y_ref[1, :, cols[h]] = (o[h] * r * gn_ref[h] * gate[:, cols[h]]).astype(BF16)

    def rb(s, i):
        return s * tps + i

    def per_head(*shape):
        return pl.BlockSpec((H,) + shape, lambda g, s, i: (g,) + (0,) * len(shape))

    return pl.pallas_call(
        body, name="mixer_fwd", grid=(4 // H, seqs, tps),
        in_specs=[pl.BlockSpec((5, tm, H * W), lambda g, s, i: (0, rb(s, i), g)),
                  per_head(W, W), per_head(1, W), per_head(2, W), per_head(1, W),
                  _tri_spec(tm), _tri_spec(tm)],
        out_specs=[pl.BlockSpec((2, tm, H * W), lambda g, s, i: (0, rb(s, i), g)),
                   pl.BlockSpec((tm, H * W), lambda g, s, i: (rb(s, i), g)),
                   pl.BlockSpec((nc, H, W, W), lambda g, s, i: (rb(s, i), g, 0, 0))],
        out_shape=[jax.ShapeDtypeStruct((2, T, 4 * W), BF16),
                   jax.ShapeDtypeStruct((T, 4 * W), F32),
                   jax.ShapeDtypeStruct((T // CHUNK, 4, W, W), BF16)],
        scratch_shapes=[pltpu.VMEM((H, POOL_HALO, W), F32),
                        pltpu.VMEM((H, 4, tm + 32, W), F32),
                        pltpu.VMEM((H, W, W), F32)],
        compiler_params=_cparams(("arbitrary", "arbitrary", "arbitrary")),
    )(u5, pool_w_bf, scale4, theta4, gn4, tri_bf, tri_f)


def mixer_bwd(u5, dy2, o_pre, st_prev, pool_w_bf, scale4, theta4, gn4, tri_bf, tri_f, after, *, seqs, seq_len, tm):
    T = u5.shape[1]
    tps = seq_len // tm
    nc = tm // CHUNK
    W = HEAD_W
    hb = tm // POOL_HALO

    H = HEADS_PER_STEP
    heads = range(H)

    def body(u_ref, uh_ref, dy_ref, o_ref, st_ref, pw_ref, sc_ref, th_ref, gn_ref, tri_ref, msk_ref, _after_ref,
             du_ref, dpw_ref, dsc_ref, dlb_ref, dgn_ref, nxt_ref, ext_ref, ds_ref):
        g = pl.program_id(0)
        s = pl.program_id(1)
        i = pl.program_id(2)
        tile = tps - 1 - i
        first = (s == 0) & (i == 0)

        @pl.when(i == 0)
        def _():
            nxt_ref[...] = jnp.zeros_like(nxt_ref)
            ds_ref[...] = jnp.zeros_like(ds_ref)

        row = lax.broadcasted_iota(jnp.int32, (tm, 1), 0)
        cols = [slice(h * W, (h + 1) * W) for h in heads]

        def accumulate(ref, h, val):
            @pl.when(first)
            def _():
                ref[h] = val

            @pl.when(jnp.logical_not(first))
            def _():
                ref[h] += val

        def per_head(fn):
            return jnp.concatenate([jnp.broadcast_to(fn(cols[h]), (tm, W)) for h in heads], axis=1)

        inv_cnt, pb, dz = [], [], []
        for h in heads:
            grp = g * H + h
            inv_cnt.append(1.0 / _pool_count(grp, tile * tm + row))
            ext = ext_ref.at[h]
            up = u_ref[0, :, cols[h]]
            ext[0, 0:16] = jnp.zeros((16, W), F32)
            ext[0, 16:32] = jnp.where(tile == 0, 0.0, uh_ref[:, cols[h]])
            ext[0, 32:32 + tm] = up
            win = _select_window(grp, *_pool_windows_back(ext, tm))
            pb.append((win * inv_cnt[h] - up).astype(BF16))
            dz.append((dy_ref[0, :, cols[h]].astype(F32) * sc_ref[h]).astype(BF16))
        z = [_dot(pb[h], pw_ref[h]) for h in heads]
        dp = [_dot_nt(dz[h], pw_ref[h]) for h in heads]
        dpw = [_dot_tn(pb[h], dz[h]) for h in heads]
        for h in heads:
            accumulate(dsc_ref, h, jnp.sum(dy_ref[0, :, cols[h]].astype(F32) * z[h], axis=0, keepdims=True))
            accumulate(dpw_ref, h, dpw[h])
            ext = ext_ref.at[h]
            e = dp[h] * inv_cnt[h]
            ext[0, 0:tm] = e
            ext[0, tm:tm + 16] = nxt_ref[h]
            ext[0, tm + 16:tm + 32] = jnp.zeros((16, W), F32)
            lead = _select_window(g * H + h, *_pool_windows_fwd(ext, tm))
            nxt_ref[h] = e[0:POOL_HALO]
            du_ref[0, :, cols[h]] = (lead - dp[h]).astype(BF16)

        zq, zf, zi, zg = u_ref[1], u_ref[2], u_ref[3], u_ref[4]
        lb = jnp.concatenate([_sigmoid(th_ref[h][0:1, :] - th_ref[h][1:2, :]) for h in heads], axis=1)
        gn = jnp.concatenate([gn_ref[h] for h in heads], axis=1)
        sig, sq, sg = _sigmoid(zf), _sigmoid(zq), _sigmoid(zg)
        f = lb + (1.0 - lb) * sig
        kk = 1.0 - f
        q = zq * sq
        G = _tri_apply(tri_ref[0], jnp.log(f))

        dyh = dy_ref[1].astype(F32)
        o = o_ref[...]
        sqr = o * o
        r = per_head(lambda cs: lax.rsqrt(jnp.mean(sqr[:, cs], axis=-1, keepdims=True) + EPS))
        orr = o * r
        du_ref[4] = (dyh * (orr * gn) * (sg * (1.0 + zg * (1.0 - sg)))).astype(BF16)
        don = dyh * (zg * sg)
        dgn = jnp.sum(don * orr, axis=0, keepdims=True)
        dog = don * gn
        dog_orr = dog * orr
        do = r * (dog - orr * per_head(lambda cs: jnp.mean(dog_orr[:, cs], axis=-1, keepdims=True)))

        Gm, Gl = _chunk_row(G, CHUNK // 2 - 1, nc), _chunk_row(G, CHUNK - 1, nc)
        e_q, e_k, e_e, e_g = jnp.exp(G - Gm), jnp.exp(Gm - G), jnp.exp(Gl - G), jnp.exp(G)
        qr, kr, ke, qg = q * e_q, kk * e_k, kk * e_e, q * e_g
        qrb, krb, keb, qgb = qr.astype(BF16), kr.astype(BF16), ke.astype(BF16), qg.astype(BF16)
        vb = zi.astype(BF16)
        dob = do.astype(BF16)
        lower, upper = msk_ref[0] > 0.5, msk_ref[1] > 0.5
        da = [jnp.where(lower, _dot_nt(dob[:, cs], vb[:, cs]), 0.0).astype(BF16) for cs in cols]
        a_t = [jnp.where(upper, _dot_nt(krb[:, cs], qrb[:, cs]), 0.0).astype(BF16) for cs in cols]
        da_t = [jnp.where(upper, _dot_nt(vb[:, cs], dob[:, cs]), 0.0).astype(BF16) for cs in cols]
        u_cat = [_dot_tn(dob[:, cs], _block_diag(qgb[:, cs], nc)) for cs in cols]
        dqr = [_dot(da[h], krb[:, cols[h]]) for h in heads]
        dkr = [_dot(da_t[h], qrb[:, cols[h]]) for h in heads]
        dv = [_dot(a_t[h], dob[:, cols[h]]) for h in heads]
        dsn_rows, dsn_cols, ddecay = [], [], [[None] * H for _ in range(nc)]
        for h in heads:
            dsn = ds_ref[h]
            dsn_b = [None] * nc
            for c in reversed(range(nc)):
                decay = jnp.exp(G[(c + 1) * CHUNK - 1:(c + 1) * CHUNK, cols[h]])
                dsn_b[c] = dsn.astype(BF16)
                ddecay[c][h] = jnp.sum(dsn * st_ref[c, h].astype(F32), axis=0, keepdims=True) * decay
                dsn = u_cat[h][:, c * W:(c + 1) * W] + dsn * decay
            ds_ref[h] = dsn
            dsn_rows.append(jnp.concatenate(dsn_b, axis=0))
            dsn_cols.append(jnp.concatenate(dsn_b, axis=1))
        st_rows = [jnp.concatenate([st_ref[c, h] for c in range(nc)], axis=0) for h in heads]
        dqg = [_dot(_block_diag(dob[:, cols[h]], nc), st_rows[h]) for h in heads]
        dke = [_dot(_block_diag(vb[:, cols[h]], nc), dsn_rows[h]) for h in heads]
        dv = [dv[h] + _dot_nt(_block_diag(keb[:, cols[h]], nc), dsn_cols[h]) for h in heads]
        dqr, dkr, dqg, dke, dv = (jnp.concatenate(parts, axis=1) for parts in (dqr, dkr, dqg, dke, dv))
        t_mid, t_qg, t_ke = dkr * kr - dqr * qr, dqg * qg, dke * ke
        dq = dqr * e_q + dqg * e_g
        dk = dkr * e_k + dke * e_e
        crow = lax.broadcasted_iota(jnp.int32, (CHUNK, 1), 0)
        ends = []
        for c in range(nc):
            sl = slice(c * CHUNK, (c + 1) * CHUNK)
            dgm = jnp.sum(t_mid[sl], axis=0, keepdims=True)
            dgl = jnp.sum(t_ke[sl], axis=0, keepdims=True) + jnp.concatenate(ddecay[c], axis=1)
            ends.append(jnp.where(crow == CHUNK // 2 - 1, dgm, 0.0) + jnp.where(crow == CHUNK - 1, dgl, 0.0))
        dG = t_qg - t_ke - t_mid + jnp.concatenate(ends, axis=0)
        dlogf = _tri_apply(tri_ref[1], dG)
        df = dlogf / f - dk
        du_ref[1] = (dq * (sq * (1.0 + zq * (1.0 - sq)))).astype(BF16)
        du_ref[2] = (df * (1.0 - lb) * (sig * (1.0 - sig))).astype(BF16)
        du_ref[3] = dv.astype(BF16)
        dlb = jnp.sum(df * (1.0 - sig), axis=0, keepdims=True) * (lb * (1.0 - lb))
        for h in heads:
            accumulate(dgn_ref, h, dgn[:, cols[h]])
            accumulate(dlb_ref, h, dlb[:, cols[h]])

    def rb(s, i):
        return s * tps + (tps - 1 - i)

    def per_head_spec(*shape):
        return pl.BlockSpec((H,) + shape, lambda g, s, i: (g,) + (0,) * len(shape))

    vec, mat = per_head_spec(1, W), per_head_spec(W, W)
    return pl.pallas_call(
        body, name="mixer_bwd", grid=(4 // H, seqs, tps),
        in_specs=[pl.BlockSpec((5, tm, H * W), lambda g, s, i: (0, rb(s, i), g)),
                  pl.BlockSpec((None, POOL_HALO, H * W), lambda g, s, i: (0, jnp.maximum(rb(s, i) * hb - 1, 0), g)),
                  pl.BlockSpec((2, tm, H * W), lambda g, s, i: (0, rb(s, i), g)),
                  pl.BlockSpec((tm, H * W), lambda g, s, i: (rb(s, i), g)),
                  pl.BlockSpec((nc, H, W, W), lambda g, s, i: (rb(s, i), g, 0, 0)),
                  mat, vec, per_head_spec(2, W), vec, _tri_spec(tm), _tri_spec(tm),
                  pl.BlockSpec(after.shape, lambda g, s, i: (0, 0))],
        out_specs=[pl.BlockSpec((5, tm, H * W), lambda g, s, i: (0, rb(s, i), g)), mat, vec, vec, vec],
        out_shape=[jax.ShapeDtypeStruct((5, T, 4 * W), BF16),
                   jax.ShapeDtypeStruct((4, W, W), F32),
                   jax.ShapeDtypeStruct((4, 1, W), F32),
                   jax.ShapeDtypeStruct((4, 1, W), F32),
                   jax.ShapeDtypeStruct((4, 1, W), F32)],
        scratch_shapes=[pltpu.VMEM((H, POOL_HALO, W), F32),
                        pltpu.VMEM((H, 4, tm + 32, W), F32),
                        pltpu.VMEM((H, W, W), F32)],
        compiler_params=_cparams(("arbitrary", "arbitrary", "arbitrary")),
    )(u5, u5, dy2, o_pre, st_prev, pool_w_bf, scale4, theta4, gn4, tri_bf, tri_f, after)


def _attn_probs(q, k, hd):
    s = _dot_nt(q, k) * (1.0 / (hd ** 0.5))
    e = jnp.exp(s - jnp.max(s, axis=-1, keepdims=True))
    return e * (1.0 / jnp.sum(e, axis=-1, keepdims=True))


def attn_fwd(q, kv3, *, seqs, seq_len, n_mem, tm):
    T, D = q.shape
    hd = D // XATTN_HEADS
    tps = seq_len // tm

    cols = [slice(h * hd, (h + 1) * hd) for h in range(XATTN_HEADS)]

    def body(q_ref, kv_ref, o_ref):
        p = [_attn_probs(q_ref[:, cs], kv_ref[0, :, cs], hd) for cs in cols]
        for h, cs in enumerate(cols):
            o_ref[:, cs] = _dot(p[h].astype(BF16), kv_ref[1, :, cs]).astype(BF16)

    return pl.pallas_call(
        body, name="attn_fwd", grid=(seqs, tps),
        in_specs=[pl.BlockSpec((tm, D), lambda b, i: (b * tps + i, 0)),
                  pl.BlockSpec((2, n_mem, D), lambda b, i: (0, b, 0))],
        out_specs=pl.BlockSpec((tm, D), lambda b, i: (b * tps + i, 0)),
        out_shape=jax.ShapeDtypeStruct((T, D), BF16),
        compiler_params=_cparams(("parallel", "arbitrary")),
    )(q, kv3)


def attn_bwd(q, kv3, do, *, seqs, seq_len, n_mem, tm):
    T, D = q.shape
    hd = D // XATTN_HEADS
    tps = seq_len // tm

    cols = [slice(h * hd, (h + 1) * hd) for h in range(XATTN_HEADS)]

    def body(q_ref, kv_ref, do_ref, dq_ref, dkv_ref):
        i = pl.program_id(1)

        @pl.when(i == 0)
        def _():
            dkv_ref[...] = jnp.zeros_like(dkv_ref)

        p = [_attn_probs(q_ref[:, cs], kv_ref[0, :, cs], hd) for cs in cols]
        dp = [_dot_nt(do_ref[:, cs], kv_ref[1, :, cs]) for cs in cols]
        ds = [(p[h] * (dp[h] - jnp.sum(dp[h] * p[h], axis=-1, keepdims=True)) * (1.0 / (hd ** 0.5))).astype(BF16)
              for h in range(XATTN_HEADS)]
        for h, cs in enumerate(cols):
            dq_ref[:, cs] = _dot(ds[h], kv_ref[0, :, cs]).astype(BF16)
            dkv_ref[0, :, cs] += _dot_tn(ds[h], q_ref[:, cs])
            dkv_ref[1, :, cs] += _dot_tn(p[h].astype(BF16), do_ref[:, cs])

    qspec = pl.BlockSpec((tm, D), lambda b, i: (b * tps + i, 0))
    kvspec = pl.BlockSpec((2, n_mem, D), lambda b, i: (0, b, 0))
    return pl.pallas_call(
        body, name="attn_bwd", grid=(seqs, tps),
        in_specs=[qspec, kvspec, qspec],
        out_specs=[qspec, kvspec],
        out_shape=[jax.ShapeDtypeStruct((T, D), BF16), jax.ShapeDtypeStruct((2, seqs * n_mem, D), F32)],
        compiler_params=_cparams(("parallel", "arbitrary")),
    )(q, kv3, do)


def final_loss(h, g, target, *, tm):
    T, D = h.shape

    def body(h_ref, g_ref, t_ref, dh_ref, dhb_ref, ls_ref, dg_ref):
        i = pl.program_id(0)
        x = h_ref[...]
        gv = g_ref[...]
        r = lax.rsqrt(jnp.mean(x * x, axis=-1, keepdims=True) + EPS)
        xr = x * r
        d = xr * gv - t_ref[...]
        dy = d * (1.0 / D)
        dyg = dy * gv
        dx = r * (dyg - xr * jnp.mean(dyg * xr, axis=-1, keepdims=True))
        dh_ref[...] = dx
        dhb_ref[...] = dx.astype(BF16)
        ls = jnp.sum(d * d, axis=0, keepdims=True)
        dg = jnp.sum(dy * xr, axis=0, keepdims=True)

        @pl.when(i == 0)
        def _():
            ls_ref[...] = ls
            dg_ref[...] = dg

        @pl.when(i > 0)
        def _():
            ls_ref[...] += ls
            dg_ref[...] += dg

    row = pl.BlockSpec((tm, D), lambda i: (i, 0))
    vec = pl.BlockSpec((1, D), lambda i: (0, 0))
    return pl.pallas_call(
        body, name="final_loss", grid=(T // tm,),
        in_specs=[row, vec, row], out_specs=[row, row, vec, vec],
        out_shape=[jax.ShapeDtypeStruct((T, D), F32), jax.ShapeDtypeStruct((T, D), BF16),
                   jax.ShapeDtypeStruct((1, D), F32), jax.ShapeDtypeStruct((1, D), F32)],
        compiler_params=_cparams(("arbitrary",)),
    )(h, g, target)


def _my_place():
    return lax.axis_index("x"), lax.axis_index("y"), lax.axis_index("c")


def _slot_of(px, py, pc):
    return 4 * px + 2 * py + pc


def _peer(k, x, y, c):
    return (1 - x if (k >> 2) & 1 else x, 1 - y if (k >> 1) & 1 else y, 1 - c if k & 1 else c)


def _split_copies(src_refs, land_refs, send_sems, recv_sems, scatter):
    x, y, c = _my_place()
    mine = _slot_of(x, y, c)
    copies = []
    for a, (src, land) in enumerate(zip(src_refs, land_refs)):
        for k in range(1, N_DEV):
            peer = _peer(k, x, y, c)
            copies.append(pltpu.make_async_remote_copy(
                src_ref=src.at[_slot_of(*peer)] if scatter else src, dst_ref=land.at[mine],
                send_sem=send_sems.at[a * N_PEERS + k - 1], recv_sem=recv_sems.at[a * N_PEERS + k - 1],
                device_id=peer, device_id_type=MESH))
    return copies


def split_start(groups, *, name, scatter):
    sizes = [len(srcs) for srcs, _ in groups]
    n_arr = sum(sizes)
    flat = [a for srcs, lands in groups for a in list(srcs) + list(lands)]

    def body(*refs):
        ins = refs[:2 * n_arr]
        sems = refs[4 * n_arr:4 * n_arr + 2 * len(groups)]
        token = refs[-1]
        at = 0
        for gi, n in enumerate(sizes):
            for cp in _split_copies(ins[at:at + n], ins[at + n:at + 2 * n], sems[2 * gi], sems[2 * gi + 1], scatter):
                cp.start()
            at += 2 * n
        token[...] = jnp.zeros_like(token)

    sem_shapes = []
    for n in sizes:
        sem_shapes += [pltpu.SemaphoreType.DMA((n * N_PEERS,))] * 2
    outs = pl.pallas_call(
        body, name=name,
        out_shape=tuple(pltpu.HBM(a.shape, a.dtype) for a in flat) + tuple(sem_shapes)
        + (jax.ShapeDtypeStruct((8, 128), F32),),
        in_specs=(HBM,) * len(flat),
        out_specs=(HBM,) * len(flat) + (SEM,) * len(sem_shapes) + (pl.BlockSpec(memory_space=pltpu.VMEM),),
        input_output_aliases={i: i for i in range(len(flat))},
        compiler_params=pltpu.CompilerParams(has_side_effects=pltpu.SideEffectType.DATAFLOW_SIDE_EFFECTING),
    )(*[pltpu.with_memory_space_constraint(a, pltpu.HBM) for a in flat])
    thru, sems, token = outs[:len(flat)], outs[len(flat):-1], outs[-1]
    started, at = [], 0
    for gi, n in enumerate(sizes):
        started.append((sems[2 * gi], sems[2 * gi + 1], thru[at:at + n], thru[at + n:at + 2 * n]))
        at += 2 * n
    return started, token


def split_wait(started, after, *, name, scatter):
    sizes = [len(g[2]) for g in started]
    n_arr = sum(sizes)
    flat = [a for g in started for a in list(g[2]) + list(g[3])]
    sems = [s for g in started for s in g[:2]]

    def body(*refs):
        ins = refs[:2 * n_arr]
        sem_refs = refs[2 * n_arr:2 * n_arr + len(sems)]
        at = 0
        for gi, n in enumerate(sizes):
            for cp in _split_copies(ins[at:at + n], ins[at + n:at + 2 * n], sem_refs[2 * gi], sem_refs[2 * gi + 1], scatter):
                cp.wait_send()
                cp.wait_recv()
            at += 2 * n

    outs = pl.pallas_call(
        body, name=name,
        out_shape=tuple(pltpu.HBM(a.shape, a.dtype) for a in flat),
        in_specs=(HBM,) * len(flat) + (SEM,) * len(sems) + (pl.BlockSpec(memory_space=pl.ANY),),
        out_specs=(HBM,) * len(flat),
        input_output_aliases={i: i for i in range(len(flat))},
        compiler_params=pltpu.CompilerParams(has_side_effects=pltpu.SideEffectType.DATAFLOW_SIDE_EFFECTING),
    )(*flat, *sems, after)
    done, at = [], 0
    for n in sizes:
        done.append((outs[at:at + n], outs[at + n:at + 2 * n]))
        at += 2 * n
    return done


SIBLING = 1
CHIP_PEERS = (2, 4, 6)
_SIDE_EFFECTS = pltpu.CompilerParams(has_side_effects=pltpu.SideEffectType.DATAFLOW_SIDE_EFFECTING)


def _chip_level_copies(src, land, send_sems, recv_sems):
    x, y, c = _my_place()
    return [pltpu.make_async_remote_copy(
        src_ref=src, dst_ref=land.at[_slot_of(x, y, c)], send_sem=send_sems.at[j], recv_sem=recv_sems.at[j],
        device_id=_peer(k, x, y, c), device_id_type=MESH) for j, k in enumerate((SIBLING,) + CHIP_PEERS)]


def _pass_on_copies(land, send_sems, recv_sems, receiving):
    x, y, c = _my_place()
    copies = []
    for j, k in enumerate(CHIP_PEERS):
        slot = _slot_of(*_peer(k ^ SIBLING if receiving else k, x, y, c))
        copies.append(pltpu.make_async_remote_copy(
            src_ref=land.at[slot], dst_ref=land.at[slot], send_sem=send_sems.at[j], recv_sem=recv_sems.at[j],
            device_id=_peer(SIBLING, x, y, c), device_id_type=MESH))
    return copies


def gather2_start(src, land, *, name):
    def body(src_ref, land_ref, src_out, land_out, send_sems, recv_sems, token):
        for cp in _chip_level_copies(src_ref, land_ref, send_sems, recv_sems):
            cp.start()
        token[...] = jnp.zeros_like(token)

    n = 1 + len(CHIP_PEERS)
    src_t, land_t, send_sems, recv_sems, token = pl.pallas_call(
        body, name=name,
        out_shape=(pltpu.HBM(src.shape, src.dtype), pltpu.HBM(land.shape, land.dtype),
                   pltpu.SemaphoreType.DMA((n,)), pltpu.SemaphoreType.DMA((n,)), jax.ShapeDtypeStruct((8, 128), F32)),
        in_specs=(HBM, HBM), out_specs=(HBM, HBM, SEM, SEM, pl.BlockSpec(memory_space=pltpu.VMEM)),
        input_output_aliases={0: 0, 1: 1}, compiler_params=_SIDE_EFFECTS,
    )(pltpu.with_memory_space_constraint(src, pltpu.HBM), pltpu.with_memory_space_constraint(land, pltpu.HBM))
    return (src_t, land_t, send_sems, recv_sems), token


def gather2_pass_on(started, after, *, name):
    src, land, send_a, recv_a = started

    def body(src_ref, land_ref, send_a_ref, recv_a_ref, after_ref, land_out, send_b, recv_b):
        for cp in _chip_level_copies(src_ref, land_ref, send_a_ref, recv_a_ref):
            cp.wait_send()
            cp.wait_recv()
        for cp in _pass_on_copies(land_ref, send_b, recv_b, False):
            cp.start()

    n = len(CHIP_PEERS)
    land_t, send_b, recv_b = pl.pallas_call(
        body, name=name,
        out_shape=(pltpu.HBM(land.shape, land.dtype), pltpu.SemaphoreType.DMA((n,)), pltpu.SemaphoreType.DMA((n,))),
        in_specs=(HBM, HBM, SEM, SEM, pl.BlockSpec(memory_space=pl.ANY)), out_specs=(HBM, SEM, SEM),
        input_output_aliases={1: 0}, compiler_params=_SIDE_EFFECTS,
    )(src, land, send_a, recv_a, after)
    return land_t, send_b, recv_b


def gather2_wait(passed, *, name):
    land, send_b, recv_b = passed

    def body(land_ref, send_ref, recv_ref, land_out):
        for cp in _pass_on_copies(land_ref, send_ref, recv_ref, False):
            cp.wait_send()
        for cp in _pass_on_copies(land_ref, send_ref, recv_ref, True):
            cp.wait_recv()

    return pl.pallas_call(
        body, name=name, out_shape=pltpu.HBM(land.shape, land.dtype),
        in_specs=(HBM, SEM, SEM), out_specs=HBM,
        input_output_aliases={0: 0}, compiler_params=_SIDE_EFFECTS,
    )(land, send_b, recv_b)


def allgather_small(bufs):
    n = len(bufs)

    def body(*refs):
        srcs, outs = refs[:n], refs[n:2 * n]
        send_sems, recv_sems, local_sems = refs[2 * n:]
        x, y, c = _my_place()
        mine = _slot_of(x, y, c)
        local = [pltpu.make_async_copy(s, o.at[mine], local_sems.at[a]) for a, (s, o) in enumerate(zip(srcs, outs))]
        for cp in local:
            cp.start()
        copies = _split_copies(srcs, outs, send_sems, recv_sems, False)
        for cp in copies:
            cp.start()
        for cp in copies:
            cp.wait()
        for cp in local:
            cp.wait()

    return pl.pallas_call(
        body, name="allgather_small",
        out_shape=[jax.ShapeDtypeStruct((N_DEV,) + b.shape, b.dtype) for b in bufs],
        in_specs=[HBM] * n, out_specs=[HBM] * n,
        scratch_shapes=[pltpu.SemaphoreType.DMA((n * N_PEERS,)), pltpu.SemaphoreType.DMA((n * N_PEERS,)),
                        pltpu.SemaphoreType.DMA((n,))],
    )(*bufs)


def _adamw_math(g, w, m, v):
    c1 = 1.0 - ADAM_B1 ** ADAM_STEP
    c2 = 1.0 - ADAM_B2 ** ADAM_STEP
    nm = ADAM_B1 * m + (1.0 - ADAM_B1) * g
    nv = ADAM_B2 * v + (1.0 - ADAM_B2) * (g * g)
    delta = -ADAM_LR * ((nm / c1) / (jnp.sqrt(nv / c2) + ADAM_EPS) + ADAM_WD * w)
    return delta, nm, nv


def adamw_sharded(me, own, recv, w, m, v, *, name, tr):
    R, C = w.shape

    def body(me_ref, *refs):
        parts = refs[:N_DEV]
        w_ref, m_ref, v_ref, g_ref, d_ref, nm_ref, nv_ref = refs[N_DEV:]
        g = parts[0][...].astype(F32)
        for p in parts[1:]:
            g = g + p[...].astype(F32)
        g_ref[...] = g
        d_ref[...], nm_ref[...], nv_ref[...] = _adamw_math(g, w_ref[...], m_ref[...], v_ref[...])

    def slab(k):
        return pl.BlockSpec((None, tr, C), lambda i, me_ref: (me_ref[0] ^ k, i, 0))

    blk = pl.BlockSpec((tr, C), lambda i, me_ref: (i, 0))
    out = jax.ShapeDtypeStruct((R, C), F32)
    return pl.pallas_call(
        body, name=name,
        grid_spec=pltpu.PrefetchScalarGridSpec(
            num_scalar_prefetch=1, grid=(R // tr,),
            in_specs=[slab(k) for k in range(N_DEV)] + [blk, blk, blk],
            out_specs=[blk, blk, blk, blk]),
        out_shape=[out, out, out, out],
        compiler_params=_cparams(("parallel",)),
    )(me, own, *([recv] * N_PEERS), w, m, v)


def adamw_replicated(parts, ws, ms, vs, rows):
    n_buf, n_par = len(parts), len(ws)

    def body(*refs):
        p_refs = refs[:n_buf]
        w_refs = refs[n_buf:n_buf + n_par]
        m_refs = refs[n_buf + n_par:n_buf + 2 * n_par]
        v_refs = refs[n_buf + 2 * n_par:n_buf + 3 * n_par]
        outs = refs[n_buf + 3 * n_par:]
        sums = []
        for p in p_refs:
            g = p[0]
            for s in range(1, N_DEV):
                g = g + p[s]
            sums.append(g)
        for j, (b, r0, nr) in enumerate(rows):
            g = sums[b][r0:r0 + nr]
            delta, nm, nv = _adamw_math(g, w_refs[j][...], m_refs[j][...], v_refs[j][...])
            outs[j][...] = g
            outs[n_par + j][...] = delta
            outs[2 * n_par + j][...] = nm
            outs[3 * n_par + j][...] = nv

    shapes = [jax.ShapeDtypeStruct(w.shape, F32) for w in ws]
    outs = pl.pallas_call(
        body, name="adamw_replicated", out_shape=shapes * 4,
        compiler_params=pltpu.CompilerParams(vmem_limit_bytes=V7X_VMEM_LIMIT),
    )(*parts, *ws, *ms, *vs)
    return outs[:n_par], outs[n_par:2 * n_par], outs[2 * n_par:3 * n_par], outs[3 * n_par:]


BIG = ("w_in", "w_out", "xw_q", "xw_kv", "xw_o", "w_up", "w_down")
COL_SHARDED = ("w_in", "xw_kv", "w_up")
WEIGHTS = ("norm_mix", "w_in", "pool_w", "pool_scale", "lb_theta", "hgrn_norm", "w_out", "norm_xq",
           "norm_mem", "xw_q", "xw_kv", "xw_o", "norm_mlp", "w_up", "w_down", "norm_final")
SMALL = (("pool_w", (4 * HEAD_W, HEAD_W), 0, 0),
         ("norm_mix", (1, 1024), 1, 0), ("norm_xq", (1, 1024), 1, 1), ("norm_mem", (1, 1024), 1, 2),
         ("norm_mlp", (1, 1024), 1, 3), ("norm_final", (1, 1024), 1, 4),
         ("pool_scale", (1, 512), 2, 0), ("hgrn_norm", (1, 512), 2, 1), ("lb_theta", (2, 512), 2, 2))


def _pad_rows(a, rows):
    return jnp.concatenate([a, jnp.zeros((rows - a.shape[0], a.shape[1]), a.dtype)], axis=0)


def kernel(x, mem, norm_mix, w_in, pool_w, pool_scale, lb_theta, hgrn_norm, w_out, norm_xq, norm_mem, xw_q, xw_kv, xw_o, norm_mlp, w_up, w_down, norm_final, loss_target, m_norm_mix, m_w_in, m_pool_w, m_pool_scale, m_lb_theta, m_hgrn_norm, m_w_out, m_norm_xq, m_norm_mem, m_xw_q, m_xw_kv, m_xw_o, m_norm_mlp, m_w_up, m_w_down, m_norm_final, v_norm_mix, v_w_in, v_pool_w, v_pool_scale, v_lb_theta, v_hgrn_norm, v_w_out, v_norm_xq, v_norm_mem, v_xw_q, v_xw_kv, v_xw_o, v_norm_mlp, v_w_up, v_w_down, v_norm_final):
    w = dict(norm_mix=norm_mix, w_in=w_in, pool_w=pool_w, pool_scale=pool_scale, lb_theta=lb_theta,
             hgrn_norm=hgrn_norm, w_out=w_out, norm_xq=norm_xq, norm_mem=norm_mem, xw_q=xw_q, xw_kv=xw_kv,
             xw_o=xw_o, norm_mlp=norm_mlp, w_up=w_up, w_down=w_down, norm_final=norm_final)
    mom = dict(norm_mix=m_norm_mix, w_in=m_w_in, pool_w=m_pool_w, pool_scale=m_pool_scale, lb_theta=m_lb_theta,
               hgrn_norm=m_hgrn_norm, w_out=m_w_out, norm_xq=m_norm_xq, norm_mem=m_norm_mem, xw_q=m_xw_q,
               xw_kv=m_xw_kv, xw_o=m_xw_o, norm_mlp=m_norm_mlp, w_up=m_w_up, w_down=m_w_down,
               norm_final=m_norm_final)
    var = dict(norm_mix=v_norm_mix, w_in=v_w_in, pool_w=v_pool_w, pool_scale=v_pool_scale, lb_theta=v_lb_theta,
               hgrn_norm=v_hgrn_norm, w_out=v_w_out, norm_xq=v_norm_xq, norm_mem=v_norm_mem, xw_q=v_xw_q,
               xw_kv=v_xw_kv, xw_o=v_xw_o, norm_mlp=v_norm_mlp, w_up=v_w_up, w_down=v_w_down,
               norm_final=v_norm_final)

    seqs, seq_len, D = x.shape
    n_mem = mem.shape[1]
    T = seqs * seq_len
    W = HEAD_W
    x2 = x.reshape(T, D)
    mem2 = mem.reshape(seqs * n_mem, D)
    tgt2 = loss_target.reshape(T, D)
    tm_big = min(1024, T)
    tm_mid = min(512, T)
    tm_sq = min(1024, T)
    tm_mix = min(256, seq_len)
    tm_att = min(1024, seq_len)
    tkv = min(512, seqs * n_mem)
    px, py, pc = _my_place()
    me = _slot_of(px, py, pc).astype(jnp.int32)
    me1 = me.reshape(1)

    shard_bf = {n: w[n][0].astype(BF16) for n in BIG}

    def landing(n):
        zone = lax.empty((N_DEV,) + shard_bf[n].shape, BF16)
        return lax.dynamic_update_slice(zone, shard_bf[n][None], (me, 0, 0))

    w_in_started, tok = gather2_start(shard_bf["w_in"], landing("w_in"), name="w_in_gather_start")
    shard_bf["w_out"] = shard_bf["w_out"] + tok[0, 0].astype(BF16)
    ag_groups = (("w_out", "xw_q", "xw_kv", "xw_o"), ("w_up",), ("w_down",))
    ag_started, tok = split_start([([shard_bf[n] for n in grp], [landing(n) for n in grp]) for grp in ag_groups],
                                name="weights_gather_start", scatter=False)

    pool_w_bf = pool_w[0].astype(BF16)
    scale4 = pool_scale.reshape(4, 1, W)
    gn4 = hgrn_norm.reshape(4, 1, W)
    theta4 = lb_theta.reshape(2, 4, W).transpose(1, 0, 2)
    g_final = norm_final.reshape(1, D)

    n1 = prenorm(x2, norm_mix, tok, tm=tm_sq)
    wi3 = gather2_wait(gather2_pass_on(w_in_started, n1, name="w_in_gather_pass_on"), name="w_in_gather_wait")
    full_w_in = wi3.transpose(1, 0, 2).reshape(D, -1)
    u5 = proj_plain(n1, full_w_in, name="in_proj", tm=tm_sq, tn=4 * W, out_dtype=F32, out_slabs=5)
    tri_bf, tri_f = chunk_triangles(tm_mix)
    y2, o_pre, st_prev = mixer_fwd(u5, pool_w_bf, scale4, theta4, gn4, tri_bf, tri_f, seqs=seqs, seq_len=seq_len,
                                   tm=tm_mix)
    (_, (wo3, wq3, wkv3, wao3)), = split_wait(ag_started[0:1], y2, name="weights_gather_wait_attn", scatter=False)
    full_w_out, full_xw_q, full_xw_o = wo3.reshape(D, D), wq3.reshape(D, D), wao3.reshape(D, D)
    tn = 4 * W
    h1, n2, q = proj_res_norm(y2, full_w_out, x2, norm_xq, full_xw_q, name="out_q_proj", tm=tm_sq, tn=tn)
    kv3, memn = proj_norm(mem2, norm_mem, wkv3, name="kv_proj", tm=tkv, tn=wkv3.shape[2], out_dtype=BF16,
                          out_slabs=2)
    o_att = attn_fwd(q, kv3, seqs=seqs, seq_len=seq_len, n_mem=n_mem, tm=tm_att)
    h2, n3 = proj_res_norm(o_att, full_xw_o, h1, norm_mlp, name="attn_out_proj", tm=tm_sq, tn=tn)
    (_, (wup3,)), = split_wait(ag_started[1:2], h2, name="weights_gather_wait_up", scatter=False)
    tn_up = wup3.shape[2]
    aa = proj_plain(n3, wup3, name="up_proj", tm=tm_mid, tn=tn_up, relu2=True)
    (_, (wdn3,)), = split_wait(ag_started[2:3], aa, name="weights_gather_wait_down", scatter=False)
    full_w_down = wdn3.reshape(-1, D)
    dh3, dh3b, sq_err, dg_final = proj_res_loss(aa, full_w_down, h2, g_final, tgt2, name="down_proj_loss",
                                                tm=tm_mid, tn=tn)

    def send(parts, name):
        srcs = [p.reshape((N_DEV, -1, p.shape[-1])) for p in parts]
        lands = [lax.empty(s.shape, BF16) for s in srcs]
        started, token = split_start([(srcs, lands)], name=name, scatter=True)
        return started[0], token

    gw_down = wgrad(aa, dh3b, name="down_proj_wgrad", tt=tm_mid, tn=tn)
    dap = back_plain(dh3b, full_w_down, name="down_proj_bwd", tm=tm_mid, tn=tn, out_dtype=BF16, relu2_value=aa)
    gw_up = wgrad(n3, dap, name="up_proj_wgrad", tt=tm_mid, tn=tn_up, out_slabs=N_DEV)
    sent_mlp, tok = send([gw_down, gw_up], "grads_send_mlp")
    dh2, dh2b, do_att, dg_mlp = back_norm(dap, wup3, h2, norm_mlp, dh3, name="up_proj_bwd", tm=tm_mid, tk=tn_up,
                                          w_next=full_xw_o, after=tok)
    gxw_o = wgrad(o_att, dh2b, name="attn_out_proj_wgrad", tt=tm_sq, tn=tn)
    dq, dkv3 = attn_bwd(q, kv3, do_att, seqs=seqs, seq_len=seq_len, n_mem=n_mem, tm=tm_att)
    gxw_q = wgrad(n2, dq, name="q_proj_wgrad", tt=tm_sq, tn=tn)
    gxw_kv = wgrad(memn, dkv3, name="kv_proj_wgrad", tt=tkv, tn=wkv3.shape[2], out_slabs=N_DEV)
    dg_mem = back_norm(dkv3, wkv3, mem2, norm_mem, None, name="kv_proj_bwd", tm=tkv, tk=wkv3.shape[2])
    dh1, dh1b, dy2, dg_xq = back_norm(dq, full_xw_q, h1, norm_xq, dh2, name="q_proj_bwd", tm=tm_sq, tk=D,
                                      w_next=full_w_out, next_slabs=2)
    gw_out = wgrad(y2, dh1b, name="out_proj_wgrad", tt=tm_sq, tn=tn)
    sent_attn, tok = send([gxw_o, gxw_q, gxw_kv, gw_out], "grads_send_attn")
    du5, dpw, dsc, dlb, dgn = mixer_bwd(u5, dy2, o_pre, st_prev, pool_w_bf, scale4, theta4, gn4, tri_bf, tri_f, tok,
                                        seqs=seqs, seq_len=seq_len, tm=tm_mix)
    gw_in = wgrad(n1, du5, name="in_proj_wgrad", tt=tm_sq, tn=tn)
    gw_in_slots = gw_in.reshape(D, N_DEV, -1).transpose(1, 0, 2)
    sent_in, tok = send([gw_in_slots], "grads_send_in")
    dx, dg_mix = back_norm(du5, full_w_in, x2, norm_mix, dh1, name="in_proj_bwd", tm=tm_sq, tk=tn, bf16_copy=False,
                           after=tok)

    dlb_row = dlb.reshape(1, 4 * W)
    buf_vec = _pad_rows(jnp.concatenate([dg_mix, dg_xq, dg_mem, dg_mlp, dg_final, sq_err], axis=0), 8)
    buf_half = _pad_rows(jnp.concatenate([dsc.reshape(1, 4 * W), dgn.reshape(1, 4 * W), dlb_row, -dlb_row], axis=0), 8)
    small_src = [dpw.reshape(4 * W, W), buf_vec, buf_half]
    small_land = [lax.dynamic_update_slice(lax.empty((N_DEV,) + b.shape, F32), b[None], (me, 0, 0))
                  for b in small_src]
    small_started, tok = split_start([(small_src, small_land)], name="small_grads_start", scatter=False)

    done = split_wait([sent_mlp, sent_attn, sent_in], tok, name="grads_wait", scatter=True)
    slots = dict(w_down=(0, 0), w_up=(0, 1), xw_o=(1, 0), xw_q=(1, 1), xw_kv=(1, 2), w_out=(1, 3), w_in=(2, 0))
    own = {n: done[gi][0][ai] for n, (gi, ai) in slots.items()}
    got = {n: done[gi][1][ai] for n, (gi, ai) in slots.items()}
    res = {}
    for n in BIG:
        shp = w[n].shape
        r = adamw_sharded(me1, own[n], got[n], w[n][0], mom[n][0], var[n][0], name="adamw_" + n,
                          tr=min(256, shp[1]))
        for kind, a in zip("gdmv", r):
            res[kind, n] = a.reshape(shp)
    (_, small_parts), = split_wait(small_started, res["g", BIG[-1]], name="small_grads_wait", scatter=False)
    loss = 0.5 * jnp.sum(small_parts[1][:, 5, :]) / D
    r = adamw_replicated(small_parts, [w[n].reshape(v2) for n, v2, _, _ in SMALL],
                         [mom[n].reshape(v2) for n, v2, _, _ in SMALL],
                         [var[n].reshape(v2) for n, v2, _, _ in SMALL],
                         [(b, r0, v2[0]) for _, v2, b, r0 in SMALL])
    for kind, arrs in zip("gdmv", r):
        for (n, _, _, _), a in zip(SMALL, arrs):
            res[kind, n] = a.reshape(w[n].shape)

    out = [loss, dx.reshape(x.shape)]
    for kind in "gdmv":
        out += [res[kind, n] for n in WEIGHTS]
    return tuple(out)
```

```python
import jax
import jax.numpy as jnp
from jax import lax
from jax.experimental import pallas as pl
from jax.experimental.pallas import tpu as pltpu

F32 = jnp.float32
BF16 = jnp.bfloat16
EPS = 1e-6
CHUNK = 64
POOL_HALO = 16
HEAD_W = 128
HEADS_PER_STEP = 4
XATTN_HEADS = 4
N_DEV = 8
N_PEERS = N_DEV - 1
ADAM_LR = 0.001
ADAM_B1 = 0.9
ADAM_B2 = 0.999
ADAM_EPS = 1e-08
ADAM_WD = 0.01
ADAM_STEP = 10
V7X_VMEM_LIMIT = 52 * 1024 * 1024
MESH = pl.DeviceIdType.MESH
HBM = pl.BlockSpec(memory_space=pltpu.HBM)
SEM = pl.BlockSpec(memory_space=pltpu.SEMAPHORE)


def _cparams(dims):
    return pltpu.CompilerParams(dimension_semantics=dims, vmem_limit_bytes=V7X_VMEM_LIMIT)


def _sigmoid(v):
    return 0.5 * jnp.tanh(0.5 * v) + 0.5


def _dot(a, b):
    return jnp.dot(a, b, preferred_element_type=F32)


def _dot_nt(a, b):
    return lax.dot_general(a, b, (((1,), (1,)), ((), ())), preferred_element_type=F32)


def _dot_tn(a, b):
    return lax.dot_general(a, b, (((0,), (0,)), ((), ())), preferred_element_type=F32)


def _tri_apply(tri, v):
    hi = v.astype(BF16)
    lo = (v - hi.astype(F32)).astype(BF16)
    return _dot(tri, hi) + _dot(tri, lo)


def _mat_shape(a):
    return a.shape if a.ndim == 2 else (a.shape[1], a.shape[0] * a.shape[2])


def _out_struct(rows, n, slabs, dtype):
    return jax.ShapeDtypeStruct((rows, n) if slabs is None else (slabs, rows, n // slabs), dtype)


def _resident(a):
    nd = a.ndim
    return pl.BlockSpec(a.shape, lambda i: (0,) * nd, pipeline_mode=pl.Buffered(1))


def _row_block(a, tm):
    if a.ndim == 2:
        return pl.BlockSpec((tm, a.shape[1]), lambda i: (i, 0))
    return pl.BlockSpec((a.shape[0], tm, a.shape[2]), lambda i: (0, i, 0))


def _cols(ref, c, width):
    if len(ref.shape) == 2:
        return ref[:, c * width:(c + 1) * width]
    per = ref.shape[2] // width
    if per == 1:
        return ref[c]
    return ref[c // per, :, (c % per) * width:(c % per + 1) * width]


def _set_cols(ref, c, width, val):
    if len(ref.shape) == 2:
        ref[:, c * width:(c + 1) * width] = val
        return
    per = ref.shape[2] // width
    if per == 1:
        ref[c] = val
    else:
        ref[c // per, :, (c % per) * width:(c % per + 1) * width] = val


def _all_cols(ref):
    if len(ref.shape) == 2:
        return ref[...]
    return jnp.concatenate([ref[s] for s in range(ref.shape[0])], axis=1)


def _rms(x):
    return lax.rsqrt(jnp.mean(x * x, axis=-1, keepdims=True) + EPS)


def _row_params():
    return _cparams(("arbitrary",))


def proj_norm(h, g, w, *, name, tm, tn, out_dtype, out_slabs=None):
    T, D = h.shape
    N = _mat_shape(w)[1]
    o_shape = _out_struct(T, N, out_slabs, out_dtype)

    def body(h_ref, g_ref, w_ref, o_ref, n_ref):
        x = h_ref[...]
        n = (x * _rms(x) * g_ref[...]).astype(BF16)
        n_ref[...] = n
        for c in range(N // tn):
            _set_cols(o_ref, c, tn, _dot(n, _cols(w_ref, c, tn)).astype(out_dtype))

    return pl.pallas_call(
        body, name=name, grid=(T // tm,),
        in_specs=[_row_block(h, tm), pl.BlockSpec((1, D), lambda i: (0, 0)), _resident(w)],
        out_specs=[_row_block(o_shape, tm), pl.BlockSpec((tm, D), lambda i: (i, 0))],
        out_shape=[o_shape, jax.ShapeDtypeStruct((T, D), BF16)],
        compiler_params=_row_params(),
    )(h, g, w)


def prenorm(h, g, after, *, tm):
    T, D = h.shape

    def body(h_ref, g_ref, _after_ref, n_ref):
        x = h_ref[...]
        n_ref[...] = (x * _rms(x) * g_ref[...]).astype(BF16)

    row = pl.BlockSpec((tm, D), lambda i: (i, 0))
    return pl.pallas_call(
        body, name="prenorm", grid=(T // tm,),
        in_specs=[row, pl.BlockSpec((1, D), lambda i: (0, 0)), _anchor_spec(after)],
        out_specs=row, out_shape=jax.ShapeDtypeStruct((T, D), BF16),
        compiler_params=_row_params(),
    )(h, g, after)


def proj_plain(a, w, *, name, tm, tn, out_dtype=BF16, out_slabs=None, relu2=False):
    T = a.shape[0]
    N = _mat_shape(w)[1]

    def body(a_ref, w_ref, o_ref):
        av = a_ref[...]
        for c in range(N // tn):
            z = _dot(av, _cols(w_ref, c, tn))
            if relu2:
                z = jnp.maximum(z, 0.0)
                z = z * z
            _set_cols(o_ref, c, tn, z.astype(out_dtype))

    o_shape = _out_struct(T, N, out_slabs, out_dtype)
    return pl.pallas_call(
        body, name=name, grid=(T // tm,),
        in_specs=[_row_block(a, tm), _resident(w)],
        out_specs=_row_block(o_shape, tm), out_shape=o_shape,
        compiler_params=_row_params(),
    )(a, w)


def proj_res_norm(a, w, res, g, w_next=None, *, name, tm, tn):
    T = res.shape[0]
    D = w.shape[1]
    chained = w_next is not None

    def body(*refs):
        a_ref, w_ref, r_ref, g_ref = refs[:4]
        h_ref, n_ref = refs[4 + chained], refs[5 + chained]
        av = _all_cols(a_ref)
        for c in range(D // tn):
            sl = slice(c * tn, (c + 1) * tn)
            h_ref[:, sl] = r_ref[:, sl] + _dot(av, w_ref[:, sl])
        hv = h_ref[...]
        n = (hv * _rms(hv) * g_ref[...]).astype(BF16)
        n_ref[...] = n
        if chained:
            for c in range(D // tn):
                sl = slice(c * tn, (c + 1) * tn)
                refs[-1][:, sl] = _dot(n, refs[4][:, sl]).astype(BF16)

    row = pl.BlockSpec((tm, D), lambda i: (i, 0))
    half = jax.ShapeDtypeStruct((T, D), BF16)
    return pl.pallas_call(
        body, name=name, grid=(T // tm,),
        in_specs=[_row_block(a, tm), _resident(w), row, pl.BlockSpec((1, D), lambda i: (0, 0))]
        + ([_resident(w_next)] if chained else []),
        out_specs=[row, row] + ([row] if chained else []),
        out_shape=[jax.ShapeDtypeStruct((T, D), F32), half] + ([half] if chained else []),
        compiler_params=_row_params(),
    )(*([a, w, res, g] + ([w_next] if chained else [])))


def proj_res_loss(a, w, res, g, target, *, name, tm, tn):
    T = res.shape[0]
    D = w.shape[1]

    def body(a_ref, w_ref, r_ref, g_ref, t_ref, dh_ref, dhb_ref, ls_ref, dg_ref):
        i = pl.program_id(0)
        gv = g_ref[...]
        ls, dg = 0.0, 0.0
        halves = [slice(s * (tm // 2), (s + 1) * (tm // 2)) for s in range(2)]
        for rows in halves:
            av = a_ref[rows, :]
            for c in range(D // tn):
                sl = slice(c * tn, (c + 1) * tn)
                dh_ref[rows, sl] = r_ref[rows, sl] + _dot(av, w_ref[:, sl])
        for rows in halves:
            x = dh_ref[rows, :]
            r = _rms(x)
            xr = x * r
            d = xr * gv - t_ref[rows, :]
            dy = d * (1.0 / D)
            dyg = dy * gv
            dx = r * (dyg - xr * jnp.mean(dyg * xr, axis=-1, keepdims=True))
            dh_ref[rows, :] = dx
            dhb_ref[rows, :] = dx.astype(BF16)
            ls = ls + jnp.sum(d * d, axis=0, keepdims=True)
            dg = dg + jnp.sum(dy * xr, axis=0, keepdims=True)

        @pl.when(i == 0)
        def _():
            ls_ref[...] = ls
            dg_ref[...] = dg

        @pl.when(i > 0)
        def _():
            ls_ref[...] += ls
            dg_ref[...] += dg

    row = pl.BlockSpec((tm, D), lambda i: (i, 0))
    vec = pl.BlockSpec((1, D), lambda i: (0, 0))
    return pl.pallas_call(
        body, name=name, grid=(T // tm,),
        in_specs=[_row_block(a, tm), _resident(w), row, vec, row],
        out_specs=[row, row, vec, vec],
        out_shape=[jax.ShapeDtypeStruct((T, D), F32), jax.ShapeDtypeStruct((T, D), BF16),
                   jax.ShapeDtypeStruct((1, D), F32), jax.ShapeDtypeStruct((1, D), F32)],
        compiler_params=_row_params(),
    )(a, w, res, g, target)


def _anchor_spec(after):
    return pl.BlockSpec(after.shape, lambda i: (0, 0))


def back_plain(a, w, *, name, tm, tn, out_dtype, out_slabs=None, relu2_value=None, after=None):
    T = a.shape[0]
    N = w.shape[0]
    has_z = relu2_value is not None
    o_shape = _out_struct(T, N, out_slabs, out_dtype)

    def body(*refs):
        a_ref, w_ref = refs[0], refs[1]
        o_ref = refs[-1]
        av = a_ref[...]
        for c in range(N // tn):
            out = _dot_nt(av, w_ref[c * tn:(c + 1) * tn, :])
            if has_z:
                out = out * (2.0 * jnp.sqrt(refs[2][:, c * tn:(c + 1) * tn]).astype(F32))
            _set_cols(o_ref, c, tn, out.astype(out_dtype))

    in_specs, args = [_row_block(a, tm), _resident(w)], [a, w]
    if has_z:
        in_specs.append(_row_block(relu2_value, tm))
        args.append(relu2_value)
    if after is not None:
        in_specs.append(_anchor_spec(after))
        args.append(after)
    return pl.pallas_call(
        body, name=name, grid=(T // tm,),
        in_specs=in_specs, out_specs=_row_block(o_shape, tm), out_shape=o_shape,
        compiler_params=_row_params(),
    )(*args)


def back_norm(a, w, h, g, dres, *, name, tm, tk, bf16_copy=True, w_next=None, next_dtype=BF16, next_slabs=None,
              after=None):
    T, K = _mat_shape(a)
    D = h.shape[1]
    with_dh = dres is not None
    chained = w_next is not None
    n_in = 4 + with_dh + chained
    tn = 4 * HEAD_W

    def body(*refs):
        a_ref, w_ref, h_ref, g_ref = refs[:4]
        outs = refs[n_in + (after is not None):]
        i = pl.program_id(0)
        if len(w_ref.shape) == 2:
            dn = _dot_nt(_all_cols(a_ref).astype(BF16), w_ref[...])
        else:
            dn = None
            for kc in range(K // tk):
                part = _dot_nt(_cols(a_ref, kc, tk).astype(BF16), _cols(w_ref, kc, tk))
                dn = part if dn is None else dn + part
        x = h_ref[...]
        r = _rms(x)
        xr = x * r
        dgp = jnp.sum(dn * xr, axis=0, keepdims=True)
        dg_ref = outs[-1]

        @pl.when(i == 0)
        def _():
            dg_ref[...] = dgp

        @pl.when(i > 0)
        def _():
            dg_ref[...] += dgp

        if with_dh:
            dyg = dn * g_ref[...]
            out = refs[4][...] + r * (dyg - xr * jnp.mean(dyg * xr, axis=-1, keepdims=True))
            outs[0][...] = out
            outb = out.astype(BF16)
            if bf16_copy:
                outs[1][...] = outb
            if chained:
                wn_ref, nx_ref = refs[5], outs[-2]
                for c in range(wn_ref.shape[0] // tn):
                    _set_cols(nx_ref, c, tn, _dot_nt(outb, wn_ref[c * tn:(c + 1) * tn, :]).astype(next_dtype))

    row = pl.BlockSpec((tm, D), lambda i: (i, 0))
    vec = pl.BlockSpec((1, D), lambda i: (0, 0))
    in_specs, args = [_row_block(a, tm), _resident(w), row, vec], [a, w, h, g]
    out_specs, out_shape = [], []
    if with_dh:
        in_specs.append(row)
        args.append(dres)
        out_specs.append(row)
        out_shape.append(jax.ShapeDtypeStruct((T, D), F32))
        if bf16_copy:
            out_specs.append(row)
            out_shape.append(jax.ShapeDtypeStruct((T, D), BF16))
    if chained:
        in_specs.append(_resident(w_next))
        args.append(w_next)
        nx_shape = _out_struct(T, w_next.shape[0], next_slabs, next_dtype)
        out_specs.append(_row_block(nx_shape, tm))
        out_shape.append(nx_shape)
    out_specs.append(vec)
    out_shape.append(jax.ShapeDtypeStruct((1, D), F32))
    if after is not None:
        in_specs.append(_anchor_spec(after))
        args.append(after)
    outs = pl.pallas_call(
        body, name=name, grid=(T // tm,),
        in_specs=in_specs, out_specs=out_specs, out_shape=out_shape,
        compiler_params=_row_params(),
    )(*args)
    return outs if len(outs) > 1 else outs[0]


def wgrad(a, b, *, name, tt, tn, out_slabs=None):
    T, K = _mat_shape(a)
    N = _mat_shape(b)[1]
    nt = T // tt
    o_shape = _out_struct(K, N, out_slabs, BF16)

    flipped = K > N and out_slabs is None

    def body(a_ref, b_ref, o_ref, acc_ref):
        t = pl.program_id(0)

        @pl.when(t == 0)
        def _():
            acc_ref[...] = jnp.zeros_like(acc_ref)

        if flipped:
            bt = _all_cols(b_ref).astype(BF16).T
            for c in range(K // tn):
                acc_ref[:, c * tn:(c + 1) * tn] += _dot(bt, _cols(a_ref, c, tn).astype(BF16))
        else:
            at = _all_cols(a_ref).astype(BF16).T
            for c in range(N // tn):
                acc_ref[:, c * tn:(c + 1) * tn] += _dot(at, _cols(b_ref, c, tn).astype(BF16))

        @pl.when(t == nt - 1)
        def _():
            if flipped:
                for c in range(K // tn):
                    o_ref[c * tn:(c + 1) * tn, :] = acc_ref[:, c * tn:(c + 1) * tn].T.astype(BF16)
            else:
                for c in range(N // tn):
                    _set_cols(o_ref, c, tn, acc_ref[:, c * tn:(c + 1) * tn].astype(BF16))

    return pl.pallas_call(
        body, name=name, grid=(nt,),
        in_specs=[_row_block(a, tt), _row_block(b, tt)],
        out_specs=_resident(o_shape), out_shape=o_shape,
        scratch_shapes=[pltpu.VMEM((N, K) if flipped else (K, N), F32)],
        compiler_params=_row_params(),
    )(a, b)


def chunk_triangles(tm):
    r = lax.broadcasted_iota(jnp.int32, (tm, tm), 0)
    c = lax.broadcasted_iota(jnp.int32, (tm, tm), 1)
    same = (r // CHUNK) == (c // CHUNK)
    tri = jnp.stack([same & (c <= r), same & (c >= r)]).astype(F32)
    return tri.astype(BF16), tri


def _tri_spec(tm):
    return pl.BlockSpec((2, tm, tm), lambda g, s, i: (0, 0, 0))


def _chunk_row(v, r, nc):
    return jnp.concatenate([jnp.broadcast_to(v[c * CHUNK + r:c * CHUNK + r + 1], (CHUNK, v.shape[1]))
                            for c in range(nc)], axis=0)


def _block_diag(v, nc):
    chunk = lax.broadcasted_iota(jnp.int32, (v.shape[0], 1), 0) // CHUNK
    return jnp.concatenate([jnp.where(chunk == c, v, jnp.zeros_like(v)) for c in range(nc)], axis=1)


def _pool_windows_back(ext_ref, tm):
    n = tm + 32
    ext_ref[1, 8:n] = ext_ref[0, 8:n] + ext_ref[0, 7:n - 1]
    ext_ref[2, 16:n] = ext_ref[1, 16:n] + ext_ref[1, 14:n - 2]
    ext_ref[3, 24:n] = ext_ref[2, 24:n] + ext_ref[2, 20:n - 4]
    s2 = ext_ref[1, 32:n]
    s4 = ext_ref[2, 32:n]
    s8 = ext_ref[3, 32:n]
    s16 = s8 + ext_ref[3, 24:n - 8]
    return s2, s4, s8, s16


def _pool_windows_fwd(ext_ref, tm):
    n = tm + 32
    ext_ref[1, 0:n - 8] = ext_ref[0, 0:n - 8] + ext_ref[0, 1:n - 7]
    ext_ref[2, 0:n - 16] = ext_ref[1, 0:n - 16] + ext_ref[1, 2:n - 14]
    ext_ref[3, 0:n - 24] = ext_ref[2, 0:n - 24] + ext_ref[2, 4:n - 20]
    s2 = ext_ref[1, 0:tm]
    s4 = ext_ref[2, 0:tm]
    s8 = ext_ref[3, 0:tm]
    s16 = s8 + ext_ref[3, 8:tm + 8]
    return s2, s4, s8, s16


def _select_window(g, s2, s4, s8, s16):
    return jnp.where(g == 0, s2, jnp.where(g == 1, s4, jnp.where(g == 2, s8, s16)))


def _pool_count(g, pos):
    width = lax.shift_left(jnp.int32(2), g)
    return jnp.minimum(pos + 1, width).astype(F32)


def mixer_fwd(u5, pool_w_bf, scale4, theta4, gn4, tri_bf, tri_f, *, seqs, seq_len, tm):
    T = u5.shape[1]
    tps = seq_len // tm
    nc = tm // CHUNK
    W = HEAD_W

    H = HEADS_PER_STEP
    heads = range(H)

    def body(u_ref, pw_ref, sc_ref, th_ref, gn_ref, tri_ref, msk_ref, y_ref, o_ref, st_ref, halo_ref, ext_ref, s_ref):
        g = pl.program_id(0)
        i = pl.program_id(2)

        @pl.when(i == 0)
        def _():
            halo_ref[...] = jnp.zeros_like(halo_ref)
            s_ref[...] = jnp.zeros_like(s_ref)

        row = lax.broadcasted_iota(jnp.int32, (tm, 1), 0)
        cols = [slice(h * W, (h + 1) * W) for h in heads]

        pooled = []
        for h in heads:
            grp = g * H + h
            up = u_ref[0, :, cols[h]]
            ext_ref[h, 0, 0:16] = jnp.zeros((16, W), F32)
            ext_ref[h, 0, 16:32] = halo_ref[h]
            ext_ref[h, 0, 32:32 + tm] = up
            win = _select_window(grp, *_pool_windows_back(ext_ref.at[h], tm))
            pooled.append((win * (1.0 / _pool_count(grp, i * tm + row)) - up).astype(BF16))
            halo_ref[h] = up[tm - POOL_HALO:tm]
        mixed = [_dot(pooled[h], pw_ref[h]) for h in heads]
        for h in heads:
            y_ref[0, :, cols[h]] = (mixed[h] * sc_ref[h]).astype(BF16)

        zq, zf, zi, zg = u_ref[1], u_ref[2], u_ref[3], u_ref[4]
        th = [th_ref[h] for h in heads]
        lb = jnp.concatenate([_sigmoid(t[0:1, :] - t[1:2, :]) for t in th], axis=1)
        f = lb + (1.0 - lb) * _sigmoid(zf)
        kk = 1.0 - f
        q = zq * _sigmoid(zq)
        G = _tri_apply(tri_ref[0], jnp.log(f))
        Gm, Gl = _chunk_row(G, CHUNK // 2 - 1, nc), _chunk_row(G, CHUNK - 1, nc)
        vb = zi.astype(BF16)
        qrb = (q * jnp.exp(G - Gm)).astype(BF16)
        krb = (kk * jnp.exp(Gm - G)).astype(BF16)
        keb = (kk * jnp.exp(Gl - G)).astype(BF16)
        qgb = (q * jnp.exp(G)).astype(BF16)
        mask = msk_ref[0] > 0.5
        a = [jnp.where(mask, _dot_nt(qrb[:, cols[h]], krb[:, cols[h]]), 0.0).astype(BF16) for h in heads]
        d_st = [_dot_tn(vb[:, cols[h]], _block_diag(keb[:, cols[h]], nc)) for h in heads]
        o_intra = [_dot(a[h], vb[:, cols[h]]) for h in heads]
        st_cat = []
        for h in heads:
            st = s_ref[h]
            states = []
            for c in range(nc):
                states.append(st.astype(BF16))
                st_ref[c, h] = states[-1]
                st = st * jnp.exp(G[(c + 1) * CHUNK - 1:(c + 1) * CHUNK, cols[h]]) + d_st[h][:, c * W:(c + 1) * W]
            s_ref[h] = st
            st_cat.append(jnp.concatenate(states, axis=1))
        o = [o_intra[h] + _dot_nt(_block_diag(qgb[:, cols[h]], nc), st_cat[h]) for h in heads]
        gate = zg * _sigmoid(zg)
        for h in heads:
            o_ref[:, cols[h]] = o[h]
            r = lax.rsqrt(jnp.mean(o[h] * o[h], axis=-1, keepdims=True) + EPS)
            y_ref[1, :, cols[h]] = (o[h] * r * gn_ref[h] * gate[:, cols[h]]).astype(BF16)

    def rb(s, i):
        return s * tps + i

    def per_head(*shape):
        return pl.BlockSpec((H,) + shape, lambda g, s, i: (g,) + (0,) * len(shape))

    return pl.pallas_call(
        body, name="mixer_fwd", grid=(4 // H, seqs, tps),
        in_specs=[pl.BlockSpec((5, tm, H * W), lambda g, s, i: (0, rb(s, i), g)),
                  per_head(W, W), per_head(1, W), per_head(2, W), per_head(1, W),
                  _tri_spec(tm), _tri_spec(tm)],
        out_specs=[pl.BlockSpec((2, tm, H * W), lambda g, s, i: (0, rb(s, i), g)),
                   pl.BlockSpec((tm, H * W), lambda g, s, i: (rb(s, i), g)),
                   pl.BlockSpec((nc, H, W, W), lambda g, s, i: (rb(s, i), g, 0, 0))],
        out_shape=[jax.ShapeDtypeStruct((2, T, 4 * W), BF16),
                   jax.ShapeDtypeStruct((T, 4 * W), F32),
                   jax.ShapeDtypeStruct((T // CHUNK, 4, W, W), BF16)],
        scratch_shapes=[pltpu.VMEM((H, POOL_HALO, W), F32),
                        pltpu.VMEM((H, 4, tm + 32, W), F32),
                        pltpu.VMEM((H, W, W), F32)],
        compiler_params=_cparams(("arbitrary", "arbitrary", "arbitrary")),
    )(u5, pool_w_bf, scale4, theta4, gn4, tri_bf, tri_f)


def mixer_bwd(u5, dy2, o_pre, st_prev, pool_w_bf, scale4, theta4, gn4, tri_bf, tri_f, after, *, seqs, seq_len, tm):
    T = u5.shape[1]
    tps = seq_len // tm
    nc = tm // CHUNK
    W = HEAD_W
    hb = tm // POOL_HALO

    H = HEADS_PER_STEP
    heads = range(H)

    def body(u_ref, uh_ref, dy_ref, o_ref, st_ref, pw_ref, sc_ref, th_ref, gn_ref, tri_ref, msk_ref, _after_ref,
             du_ref, dpw_ref, dsc_ref, dlb_ref, dgn_ref, nxt_ref, ext_ref, ds_ref):
        g = pl.program_id(0)
        s = pl.program_id(1)
        i = pl.program_id(2)
        tile = tps - 1 - i
        first = (s == 0) & (i == 0)

        @pl.when(i == 0)
        def _():
            nxt_ref[...] = jnp.zeros_like(nxt_ref)
            ds_ref[...] = jnp.zeros_like(ds_ref)

        row = lax.broadcasted_iota(jnp.int32, (tm, 1), 0)
        cols = [slice(h * W, (h + 1) * W) for h in heads]

        def accumulate(ref, h, val):
            @pl.when(first)
            def _():
                ref[h] = val

            @pl.when(jnp.logical_not(first))
            def _():
                ref[h] += val

        def per_head(fn):
            return jnp.concatenate([jnp.broadcast_to(fn(cols[h]), (tm, W)) for h in heads], axis=1)

        inv_cnt, pb, dz = [], [], []
        for h in heads:
            grp = g * H + h
            inv_cnt.append(1.0 / _pool_count(grp, tile * tm + row))
            ext = ext_ref.at[h]
            up = u_ref[0, :, cols[h]]
            ext[0, 0:16] = jnp.zeros((16, W), F32)
            ext[0, 16:32] = jnp.where(tile == 0, 0.0, uh_ref[:, cols[h]])
            ext[0, 32:32 + tm] = up
            win = _select_window(grp, *_pool_windows_back(ext, tm))
            pb.append((win * inv_cnt[h] - up).astype(BF16))
            dz.append((dy_ref[0, :, cols[h]].astype(F32) * sc_ref[h]).astype(BF16))
        z = [_dot(pb[h], pw_ref[h]) for h in heads]
        dp = [_dot_nt(dz[h], pw_ref[h]) for h in heads]
        dpw = [_dot_tn(pb[h], dz[h]) for h in heads]
        for h in heads:
            accumulate(dsc_ref, h, jnp.sum(dy_ref[0, :, cols[h]].astype(F32) * z[h], axis=0, keepdims=True))
            accumulate(dpw_ref, h, dpw[h])
            ext = ext_ref.at[h]
            e = dp[h] * inv_cnt[h]
            ext[0, 0:tm] = e
            ext[0, tm:tm + 16] = nxt_ref[h]
            ext[0, tm + 16:tm + 32] = jnp.zeros((16, W), F32)
            lead = _select_window(g * H + h, *_pool_windows_fwd(ext, tm))
            nxt_ref[h] = e[0:POOL_HALO]
            du_ref[0, :, cols[h]] = (lead - dp[h]).astype(BF16)

        zq, zf, zi, zg = u_ref[1], u_ref[2], u_ref[3], u_ref[4]
        lb = jnp.concatenate([_sigmoid(th_ref[h][0:1, :] - th_ref[h][1:2, :]) for h in heads], axis=1)
        gn = jnp.concatenate([gn_ref[h] for h in heads], axis=1)
        sig, sq, sg = _sigmoid(zf), _sigmoid(zq), _sigmoid(zg)
        f = lb + (1.0 - lb) * sig
        kk = 1.0 - f
        q = zq * sq
        G = _tri_apply(tri_ref[0], jnp.log(f))

        dyh = dy_ref[1].astype(F32)
        o = o_ref[...]
        sqr = o * o
        r = per_head(lambda cs: lax.rsqrt(jnp.mean(sqr[:, cs], axis=-1, keepdims=True) + EPS))
        orr = o * r
        du_ref[4] = (dyh * (orr * gn) * (sg * (1.0 + zg * (1.0 - sg)))).astype(BF16)
        don = dyh * (zg * sg)
        dgn = jnp.sum(don * orr, axis=0, keepdims=True)
        dog = don * gn
        dog_orr = dog * orr
        do = r * (dog - orr * per_head(lambda cs: jnp.mean(dog_orr[:, cs], axis=-1, keepdims=True)))

        Gm, Gl = _chunk_row(G, CHUNK // 2 - 1, nc), _chunk_row(G, CHUNK - 1, nc)
        e_q, e_k, e_e, e_g = jnp.exp(G - Gm), jnp.exp(Gm - G), jnp.exp(Gl - G), jnp.exp(G)
        qr, kr, ke, qg = q * e_q, kk * e_k, kk * e_e, q * e_g
        qrb, krb, keb, qgb = qr.astype(BF16), kr.astype(BF16), ke.astype(BF16), qg.astype(BF16)
        vb = zi.astype(BF16)
        dob = do.astype(BF16)
        lower, upper = msk_ref[0] > 0.5, msk_ref[1] > 0.5
        da = [jnp.where(lower, _dot_nt(dob[:, cs], vb[:, cs]), 0.0).astype(BF16) for cs in cols]
        a_t = [jnp.where(upper, _dot_nt(krb[:, cs], qrb[:, cs]), 0.0).astype(BF16) for cs in cols]
        da_t = [jnp.where(upper, _dot_nt(vb[:, cs], dob[:, cs]), 0.0).astype(BF16) for cs in cols]
        u_cat = [_dot_tn(dob[:, cs], _block_diag(qgb[:, cs], nc)) for cs in cols]
        dqr = [_dot(da[h], krb[:, cols[h]]) for h in heads]
        dkr = [_dot(da_t[h], qrb[:, cols[h]]) for h in heads]
        dv = [_dot(a_t[h], dob[:, cols[h]]) for h in heads]
        dsn_rows, dsn_cols, ddecay = [], [], [[None] * H for _ in range(nc)]
        for h in heads:
            dsn = ds_ref[h]
            dsn_b = [None] * nc
            for c in reversed(range(nc)):
                decay = jnp.exp(G[(c + 1) * CHUNK - 1:(c + 1) * CHUNK, cols[h]])
                dsn_b[c] = dsn.astype(BF16)
                ddecay[c][h] = jnp.sum(dsn * st_ref[c, h].astype(F32), axis=0, keepdims=True) * decay
                dsn = u_cat[h][:, c * W:(c + 1) * W] + dsn * decay
            ds_ref[h] = dsn
            dsn_rows.append(jnp.concatenate(dsn_b, axis=0))
            dsn_cols.append(jnp.concatenate(dsn_b, axis=1))
        st_rows = [jnp.concatenate([st_ref[c, h] for c in range(nc)], axis=0) for h in heads]
        dqg = [_dot(_block_diag(dob[:, cols[h]], nc), st_rows[h]) for h in heads]
        dke = [_dot(_block_diag(vb[:, cols[h]], nc), dsn_rows[h]) for h in heads]
        dv = [dv[h] + _dot_nt(_block_diag(keb[:, cols[h]], nc), dsn_cols[h]) for h in heads]
        dqr, dkr, dqg, dke, dv = (jnp.concatenate(parts, axis=1) for parts in (dqr, dkr, dqg, dke, dv))
        t_mid, t_qg, t_ke = dkr * kr - dqr * qr, dqg * qg, dke * ke
        dq = dqr * e_q + dqg * e_g
        dk = dkr * e_k + dke * e_e
        crow = lax.broadcasted_iota(jnp.int32, (CHUNK, 1), 0)
        ends = []
        for c in range(nc):
            sl = slice(c * CHUNK, (c + 1) * CHUNK)
            dgm = jnp.sum(t_mid[sl], axis=0, keepdims=True)
            dgl = jnp.sum(t_ke[sl], axis=0, keepdims=True) + jnp.concatenate(ddecay[c], axis=1)
            ends.append(jnp.where(crow == CHUNK // 2 - 1, dgm, 0.0) + jnp.where(crow == CHUNK - 1, dgl, 0.0))
        dG = t_qg - t_ke - t_mid + jnp.concatenate(ends, axis=0)
        dlogf = _tri_apply(tri_ref[1], dG)
        df = dlogf / f - dk
        du_ref[1] = (dq * (sq * (1.0 + zq * (1.0 - sq)))).astype(BF16)
        du_ref[2] = (df * (1.0 - lb) * (sig * (1.0 - sig))).astype(BF16)
        du_ref[3] = dv.astype(BF16)
        dlb = jnp.sum(df * (1.0 - sig), axis=0, keepdims=True) * (lb * (1.0 - lb))
        for h in heads:
            accumulate(dgn_ref, h, dgn[:, cols[h]])
            accumulate(dlb_ref, h, dlb[:, cols[h]])

    def rb(s, i):
        return s * tps + (tps - 1 - i)

    def per_head_spec(*shape):
        return pl.BlockSpec((H,) + shape, lambda g, s, i: (g,) + (0,) * len(shape))

    vec, mat = per_head_spec(1, W), per_head_spec(W, W)
    return pl.pallas_call(
        body, name="mixer_bwd", grid=(4 // H, seqs, tps),
        in_specs=[pl.BlockSpec((5, tm, H * W), lambda g, s, i: (0, rb(s, i), g)),
                  pl.BlockSpec((None, POOL_HALO, H * W), lambda g, s, i: (0, jnp.maximum(rb(s, i) * hb - 1, 0), g)),
                  pl.BlockSpec((2, tm, H * W), lambda g, s, i: (0, rb(s, i), g)),
                  pl.BlockSpec((tm, H * W), lambda g, s, i: (rb(s, i), g)),
                  pl.BlockSpec((nc, H, W, W), lambda g, s, i: (rb(s, i), g, 0, 0)),
                  mat, vec, per_head_spec(2, W), vec, _tri_spec(tm), _tri_spec(tm),
                  pl.BlockSpec(after.shape, lambda g, s, i: (0, 0))],
        out_specs=[pl.BlockSpec((5, tm, H * W), lambda g, s, i: (0, rb(s, i), g)), mat, vec, vec, vec],
        out_shape=[jax.ShapeDtypeStruct((5, T, 4 * W), BF16),
                   jax.ShapeDtypeStruct((4, W, W), F32),
                   jax.ShapeDtypeStruct((4, 1, W), F32),
                   jax.ShapeDtypeStruct((4, 1, W), F32),
                   jax.ShapeDtypeStruct((4, 1, W), F32)],
        scratch_shapes=[pltpu.VMEM((H, POOL_HALO, W), F32),
                        pltpu.VMEM((H, 4, tm + 32, W), F32),
                        pltpu.VMEM((H, W, W), F32)],
        compiler_params=_cparams(("arbitrary", "arbitrary", "arbitrary")),
    )(u5, u5, dy2, o_pre, st_prev, pool_w_bf, scale4, theta4, gn4, tri_bf, tri_f, after)


def _attn_probs(q, k, hd):
    s = _dot_nt(q, k) * (1.0 / (hd ** 0.5))
    e = jnp.exp(s - jnp.max(s, axis=-1, keepdims=True))
    return e * (1.0 / jnp.sum(e, axis=-1, keepdims=True))


def attn_fwd(q, kv3, *, seqs, seq_len, n_mem, tm):
    T, D = q.shape
    hd = D // XATTN_HEADS
    tps = seq_len // tm

    cols = [slice(h * hd, (h + 1) * hd) for h in range(XATTN_HEADS)]

    def body(q_ref, kv_ref, o_ref):
        p = [_attn_probs(q_ref[:, cs], kv_ref[0, :, cs], hd) for cs in cols]
        for h, cs in enumerate(cols):
            o_ref[:, cs] = _dot(p[h].astype(BF16), kv_ref[1, :, cs]).astype(BF16)

    return pl.pallas_call(
        body, name="attn_fwd", grid=(seqs, tps),
        in_specs=[pl.BlockSpec((tm, D), lambda b, i: (b * tps + i, 0)),
                  pl.BlockSpec((2, n_mem, D), lambda b, i: (0, b, 0))],
        out_specs=pl.BlockSpec((tm, D), lambda b, i: (b * tps + i, 0)),
        out_shape=jax.ShapeDtypeStruct((T, D), BF16),
        compiler_params=_cparams(("parallel", "arbitrary")),
    )(q, kv3)


def attn_bwd(q, kv3, do, *, seqs, seq_len, n_mem, tm):
    T, D = q.shape
    hd = D // XATTN_HEADS
    tps = seq_len // tm

    cols = [slice(h * hd, (h + 1) * hd) for h in range(XATTN_HEADS)]

    def body(q_ref, kv_ref, do_ref, dq_ref, dkv_ref):
        i = pl.program_id(1)

        @pl.when(i == 0)
        def _():
            dkv_ref[...] = jnp.zeros_like(dkv_ref)

        p = [_attn_probs(q_ref[:, cs], kv_ref[0, :, cs], hd) for cs in cols]
        dp = [_dot_nt(do_ref[:, cs], kv_ref[1, :, cs]) for cs in cols]
        ds = [(p[h] * (dp[h] - jnp.sum(dp[h] * p[h], axis=-1, keepdims=True)) * (1.0 / (hd ** 0.5))).astype(BF16)
              for h in range(XATTN_HEADS)]
        for h, cs in enumerate(cols):
            dq_ref[:, cs] = _dot(ds[h], kv_ref[0, :, cs]).astype(BF16)
            dkv_ref[0, :, cs] += _dot_tn(ds[h], q_ref[:, cs])
            dkv_ref[1, :, cs] += _dot_tn(p[h].astype(BF16), do_ref[:, cs])

    qspec = pl.BlockSpec((tm, D), lambda b, i: (b * tps + i, 0))
    kvspec = pl.BlockSpec((2, n_mem, D), lambda b, i: (0, b, 0))
    return pl.pallas_call(
        body, name="attn_bwd", grid=(seqs, tps),
        in_specs=[qspec, kvspec, qspec],
        out_specs=[qspec, kvspec],
        out_shape=[jax.ShapeDtypeStruct((T, D), BF16), jax.ShapeDtypeStruct((2, seqs * n_mem, D), F32)],
        compiler_params=_cparams(("parallel", "arbitrary")),
    )(q, kv3, do)


def _my_place():
    return lax.axis_index("x"), lax.axis_index("y"), lax.axis_index("c")


def _slot_of(px, py, pc):
    return 4 * px + 2 * py + pc


def _peer(k, x, y, c):
    return (1 - x if (k >> 2) & 1 else x, 1 - y if (k >> 1) & 1 else y, 1 - c if k & 1 else c)


def _split_copies(src_refs, land_refs, send_sems, recv_sems, scatter):
    x, y, c = _my_place()
    mine = _slot_of(x, y, c)
    copies = []
    for a, (src, land) in enumerate(zip(src_refs, land_refs)):
        for k in range(1, N_DEV):
            peer = _peer(k, x, y, c)
            copies.append(pltpu.make_async_remote_copy(
                src_ref=src.at[_slot_of(*peer)] if scatter else src, dst_ref=land.at[mine],
                send_sem=send_sems.at[a * N_PEERS + k - 1], recv_sem=recv_sems.at[a * N_PEERS + k - 1],
                device_id=peer, device_id_type=MESH))
    return copies


def split_start(groups, *, name, scatter):
    sizes = [len(srcs) for srcs, _ in groups]
    n_arr = sum(sizes)
    flat = [a for srcs, lands in groups for a in list(srcs) + list(lands)]

    def body(*refs):
        ins = refs[:2 * n_arr]
        sems = refs[4 * n_arr:4 * n_arr + 2 * len(groups)]
        token = refs[-1]
        at = 0
        for gi, n in enumerate(sizes):
            for cp in _split_copies(ins[at:at + n], ins[at + n:at + 2 * n], sems[2 * gi], sems[2 * gi + 1], scatter):
                cp.start()
            at += 2 * n
        token[...] = jnp.zeros_like(token)

    sem_shapes = []
    for n in sizes:
        sem_shapes += [pltpu.SemaphoreType.DMA((n * N_PEERS,))] * 2
    outs = pl.pallas_call(
        body, name=name,
        out_shape=tuple(pltpu.HBM(a.shape, a.dtype) for a in flat) + tuple(sem_shapes)
        + (jax.ShapeDtypeStruct((8, 128), F32),),
        in_specs=(HBM,) * len(flat),
        out_specs=(HBM,) * len(flat) + (SEM,) * len(sem_shapes) + (pl.BlockSpec(memory_space=pltpu.VMEM),),
        input_output_aliases={i: i for i in range(len(flat))},
        compiler_params=pltpu.CompilerParams(has_side_effects=pltpu.SideEffectType.DATAFLOW_SIDE_EFFECTING),
    )(*[pltpu.with_memory_space_constraint(a, pltpu.HBM) for a in flat])
    thru, sems, token = outs[:len(flat)], outs[len(flat):-1], outs[-1]
    started, at = [], 0
    for gi, n in enumerate(sizes):
        started.append((sems[2 * gi], sems[2 * gi + 1], thru[at:at + n], thru[at + n:at + 2 * n]))
        at += 2 * n
    return started, token


def split_wait(started, after, *, name, scatter):
    sizes = [len(g[2]) for g in started]
    n_arr = sum(sizes)
    flat = [a for g in started for a in list(g[2]) + list(g[3])]
    sems = [s for g in started for s in g[:2]]

    def body(*refs):
        ins = refs[:2 * n_arr]
        sem_refs = refs[2 * n_arr:2 * n_arr + len(sems)]
        at = 0
        for gi, n in enumerate(sizes):
            for cp in _split_copies(ins[at:at + n], ins[at + n:at + 2 * n], sem_refs[2 * gi], sem_refs[2 * gi + 1], scatter):
                cp.wait_send()
                cp.wait_recv()
            at += 2 * n

    outs = pl.pallas_call(
        body, name=name,
        out_shape=tuple(pltpu.HBM(a.shape, a.dtype) for a in flat),
        in_specs=(HBM,) * len(flat) + (SEM,) * len(sems) + (pl.BlockSpec(memory_space=pl.ANY),),
        out_specs=(HBM,) * len(flat),
        input_output_aliases={i: i for i in range(len(flat))},
        compiler_params=pltpu.CompilerParams(has_side_effects=pltpu.SideEffectType.DATAFLOW_SIDE_EFFECTING),
    )(*flat, *sems, after)
    done, at = [], 0
    for n in sizes:
        done.append((outs[at:at + n], outs[at + n:at + 2 * n]))
        at += 2 * n
    return done


SIBLING = 1
CHIP_PEERS = (2, 4, 6)
_SIDE_EFFECTS = pltpu.CompilerParams(has_side_effects=pltpu.SideEffectType.DATAFLOW_SIDE_EFFECTING)


def _chip_level_copies(src, land, send_sems, recv_sems):
    x, y, c = _my_place()
    return [pltpu.make_async_remote_copy(
        src_ref=src, dst_ref=land.at[_slot_of(x, y, c)], send_sem=send_sems.at[j], recv_sem=recv_sems.at[j],
        device_id=_peer(k, x, y, c), device_id_type=MESH) for j, k in enumerate((SIBLING,) + CHIP_PEERS)]


def _pass_on_copies(land, send_sems, recv_sems, receiving):
    x, y, c = _my_place()
    copies = []
    for j, k in enumerate(CHIP_PEERS):
        slot = _slot_of(*_peer(k ^ SIBLING if receiving else k, x, y, c))
        copies.append(pltpu.make_async_remote_copy(
            src_ref=land.at[slot], dst_ref=land.at[slot], send_sem=send_sems.at[j], recv_sem=recv_sems.at[j],
            device_id=_peer(SIBLING, x, y, c), device_id_type=MESH))
    return copies


def gather2_start(src, land, *, name):
    def body(src_ref, land_ref, src_out, land_out, send_sems, recv_sems, token):
        for cp in _chip_level_copies(src_ref, land_ref, send_sems, recv_sems):
            cp.start()
        token[...] = jnp.zeros_like(token)

    n = 1 + len(CHIP_PEERS)
    src_t, land_t, send_sems, recv_sems, token = pl.pallas_call(
        body, name=name,
        out_shape=(pltpu.HBM(src.shape, src.dtype), pltpu.HBM(land.shape, land.dtype),
                   pltpu.SemaphoreType.DMA((n,)), pltpu.SemaphoreType.DMA((n,)), jax.ShapeDtypeStruct((8, 128), F32)),
        in_specs=(HBM, HBM), out_specs=(HBM, HBM, SEM, SEM, pl.BlockSpec(memory_space=pltpu.VMEM)),
        input_output_aliases={0: 0, 1: 1}, compiler_params=_SIDE_EFFECTS,
    )(pltpu.with_memory_space_constraint(src, pltpu.HBM), pltpu.with_memory_space_constraint(land, pltpu.HBM))
    return (src_t, land_t, send_sems, recv_sems), token


def gather2_pass_on(started, after, *, name):
    src, land, send_a, recv_a = started

    def body(src_ref, land_ref, send_a_ref, recv_a_ref, after_ref, land_out, send_b, recv_b):
        for cp in _chip_level_copies(src_ref, land_ref, send_a_ref, recv_a_ref):
            cp.wait_send()
            cp.wait_recv()
        for cp in _pass_on_copies(land_ref, send_b, recv_b, False):
            cp.start()

    n = len(CHIP_PEERS)
    land_t, send_b, recv_b = pl.pallas_call(
        body, name=name,
        out_shape=(pltpu.HBM(land.shape, land.dtype), pltpu.SemaphoreType.DMA((n,)), pltpu.SemaphoreType.DMA((n,))),
        in_specs=(HBM, HBM, SEM, SEM, pl.BlockSpec(memory_space=pl.ANY)), out_specs=(HBM, SEM, SEM),
        input_output_aliases={1: 0}, compiler_params=_SIDE_EFFECTS,
    )(src, land, send_a, recv_a, after)
    return land_t, send_b, recv_b


def gather2_wait(passed, *, name):
    land, send_b, recv_b = passed

    def body(land_ref, send_ref, recv_ref, land_out):
        for cp in _pass_on_copies(land_ref, send_ref, recv_ref, False):
            cp.wait_send()
        for cp in _pass_on_copies(land_ref, send_ref, recv_ref, True):
            cp.wait_recv()

    return pl.pallas_call(
        body, name=name, out_shape=pltpu.HBM(land.shape, land.dtype),
        in_specs=(HBM, SEM, SEM), out_specs=HBM,
        input_output_aliases={0: 0}, compiler_params=_SIDE_EFFECTS,
    )(land, send_b, recv_b)


def _adamw_math(g, w, m, v):
    c1 = 1.0 - ADAM_B1 ** ADAM_STEP
    c2 = 1.0 - ADAM_B2 ** ADAM_STEP
    nm = ADAM_B1 * m + (1.0 - ADAM_B1) * g
    nv = ADAM_B2 * v + (1.0 - ADAM_B2) * (g * g)
    delta = -ADAM_LR * ((nm / c1) / (jnp.sqrt(nv / c2) + ADAM_EPS) + ADAM_WD * w)
    return delta, nm, nv


def adamw_sharded(me, owns, recvs, ws, ms, vs, *, name, tr):
    n = len(ws)
    R, C = ws[0].shape

    def body(me_ref, *refs):
        for t in range(n):
            parts = refs[t * N_DEV:(t + 1) * N_DEV]
            w_ref, m_ref, v_ref = refs[n * N_DEV + 3 * t:n * N_DEV + 3 * t + 3]
            g_ref, d_ref, nm_ref, nv_ref = refs[n * (N_DEV + 3) + 4 * t:n * (N_DEV + 3) + 4 * t + 4]
            g = parts[0][...].astype(F32)
            for p in parts[1:]:
                g = g + p[...].astype(F32)
            g_ref[...] = g
            d_ref[...], nm_ref[...], nv_ref[...] = _adamw_math(g, w_ref[...], m_ref[...], v_ref[...])

    def slab(k):
        return pl.BlockSpec((None, tr, C), lambda i, me_ref: (me_ref[0] ^ k, i, 0))

    blk = pl.BlockSpec((tr, C), lambda i, me_ref: (i, 0))
    out = jax.ShapeDtypeStruct((R, C), F32)
    args = [me]
    for t in range(n):
        args += [owns[t]] + [recvs[t]] * N_PEERS
    for t in range(n):
        args += [ws[t], ms[t], vs[t]]
    outs = pl.pallas_call(
        body, name=name,
        grid_spec=pltpu.PrefetchScalarGridSpec(
            num_scalar_prefetch=1, grid=(R // tr,),
            in_specs=[slab(k) for _ in range(n) for k in range(N_DEV)] + [blk] * (3 * n),
            out_specs=[blk] * (4 * n)),
        out_shape=[out] * (4 * n),
        compiler_params=_cparams(("parallel",)),
    )(*args)
    return [outs[4 * t:4 * t + 4] for t in range(n)]


def adamw_replicated(parts, ws, ms, vs, rows):
    n_buf, n_par = len(parts), len(ws)

    def body(*refs):
        p_refs = refs[:n_buf]
        w_refs = refs[n_buf:n_buf + n_par]
        m_refs = refs[n_buf + n_par:n_buf + 2 * n_par]
        v_refs = refs[n_buf + 2 * n_par:n_buf + 3 * n_par]
        outs = refs[n_buf + 3 * n_par:]
        sums = []
        for p in p_refs:
            g = p[0]
            for s in range(1, N_DEV):
                g = g + p[s]
            sums.append(g)
        for j, (b, r0, nr) in enumerate(rows):
            g = sums[b][r0:r0 + nr]
            delta, nm, nv = _adamw_math(g, w_refs[j][...], m_refs[j][...], v_refs[j][...])
            outs[j][...] = g
            outs[n_par + j][...] = delta
            outs[2 * n_par + j][...] = nm
            outs[3 * n_par + j][...] = nv

    shapes = [jax.ShapeDtypeStruct(w.shape, F32) for w in ws]
    outs = pl.pallas_call(
        body, name="adamw_replicated", out_shape=shapes * 4,
        compiler_params=pltpu.CompilerParams(vmem_limit_bytes=V7X_VMEM_LIMIT),
    )(*parts, *ws, *ms, *vs)
    return outs[:n_par], outs[n_par:2 * n_par], outs[2 * n_par:3 * n_par], outs[3 * n_par:]


BIG = ("w_in", "w_out", "xw_q", "xw_kv", "xw_o", "w_up", "w_down")
ADAMW_CALLS = (("w_in",), ("w_out", "xw_q", "xw_o"), ("xw_kv",), ("w_up",), ("w_down",))
WEIGHTS = ("norm_mix", "w_in", "pool_w", "pool_scale", "lb_theta", "hgrn_norm", "w_out", "norm_xq",
           "norm_mem", "xw_q", "xw_kv", "xw_o", "norm_mlp", "w_up", "w_down", "norm_final")
SMALL = (("pool_w", (4 * HEAD_W, HEAD_W), 0, 0),
         ("norm_mix", (1, 1024), 1, 0), ("norm_xq", (1, 1024), 1, 1), ("norm_mem", (1, 1024), 1, 2),
         ("norm_mlp", (1, 1024), 1, 3), ("norm_final", (1, 1024), 1, 4),
         ("pool_scale", (1, 512), 2, 0), ("hgrn_norm", (1, 512), 2, 1), ("lb_theta", (2, 512), 2, 2))


def _pad_rows(a, rows):
    return jnp.concatenate([a, jnp.zeros((rows - a.shape[0], a.shape[1]), a.dtype)], axis=0)


def kernel(x, mem, norm_mix, w_in, pool_w, pool_scale, lb_theta, hgrn_norm, w_out, norm_xq, norm_mem, xw_q, xw_kv, xw_o, norm_mlp, w_up, w_down, norm_final, loss_target, m_norm_mix, m_w_in, m_pool_w, m_pool_scale, m_lb_theta, m_hgrn_norm, m_w_out, m_norm_xq, m_norm_mem, m_xw_q, m_xw_kv, m_xw_o, m_norm_mlp, m_w_up, m_w_down, m_norm_final, v_norm_mix, v_w_in, v_pool_w, v_pool_scale, v_lb_theta, v_hgrn_norm, v_w_out, v_norm_xq, v_norm_mem, v_xw_q, v_xw_kv, v_xw_o, v_norm_mlp, v_w_up, v_w_down, v_norm_final):
    w = dict(norm_mix=norm_mix, w_in=w_in, pool_w=pool_w, pool_scale=pool_scale, lb_theta=lb_theta,
             hgrn_norm=hgrn_norm, w_out=w_out, norm_xq=norm_xq, norm_mem=norm_mem, xw_q=xw_q, xw_kv=xw_kv,
             xw_o=xw_o, norm_mlp=norm_mlp, w_up=w_up, w_down=w_down, norm_final=norm_final)
    mom = dict(norm_mix=m_norm_mix, w_in=m_w_in, pool_w=m_pool_w, pool_scale=m_pool_scale, lb_theta=m_lb_theta,
               hgrn_norm=m_hgrn_norm, w_out=m_w_out, norm_xq=m_norm_xq, norm_mem=m_norm_mem, xw_q=m_xw_q,
               xw_kv=m_xw_kv, xw_o=m_xw_o, norm_mlp=m_norm_mlp, w_up=m_w_up, w_down=m_w_down,
               norm_final=m_norm_final)
    var = dict(norm_mix=v_norm_mix, w_in=v_w_in, pool_w=v_pool_w, pool_scale=v_pool_scale, lb_theta=v_lb_theta,
               hgrn_norm=v_hgrn_norm, w_out=v_w_out, norm_xq=v_norm_xq, norm_mem=v_norm_mem, xw_q=v_xw_q,
               xw_kv=v_xw_kv, xw_o=v_xw_o, norm_mlp=v_norm_mlp, w_up=v_w_up, w_down=v_w_down,
               norm_final=v_norm_final)

    seqs, seq_len, D = x.shape
    n_mem = mem.shape[1]
    T = seqs * seq_len
    W = HEAD_W
    x2 = x.reshape(T, D)
    mem2 = mem.reshape(seqs * n_mem, D)
    tgt2 = loss_target.reshape(T, D)
    tm_big = min(1024, T)
    tm_mid = min(512, T)
    tm_sq = min(1024, T)
    tm_mix = min(256, seq_len)
    tm_att = min(1024, seq_len)
    tkv = min(512, seqs * n_mem)
    px, py, pc = _my_place()
    me = _slot_of(px, py, pc).astype(jnp.int32)
    me1 = me.reshape(1)

    shard_bf = {n: w[n][0].astype(BF16) for n in BIG}

    def landing(n):
        zone = lax.empty((N_DEV,) + shard_bf[n].shape, BF16)
        return lax.dynamic_update_slice(zone, shard_bf[n][None], (me, 0, 0))

    w_in_started, tok = gather2_start(shard_bf["w_in"], landing("w_in"), name="w_in_gather_start")
    shard_bf["w_out"] = shard_bf["w_out"] + tok[0, 0].astype(BF16)
    ag_groups = (("w_out", "xw_q", "xw_kv", "xw_o"), ("w_up",), ("w_down",))
    ag_started, tok = split_start([([shard_bf[n] for n in grp], [landing(n) for n in grp]) for grp in ag_groups],
                                name="weights_gather_start", scatter=False)

    pool_w_bf = pool_w[0].astype(BF16)
    scale4 = pool_scale.reshape(4, 1, W)
    gn4 = hgrn_norm.reshape(4, 1, W)
    theta4 = lb_theta.reshape(2, 4, W).transpose(1, 0, 2)
    g_final = norm_final.reshape(1, D)

    n1 = prenorm(x2, norm_mix, tok, tm=tm_sq)
    wi3 = gather2_wait(gather2_pass_on(w_in_started, n1, name="w_in_gather_pass_on"), name="w_in_gather_wait")
    full_w_in = wi3.transpose(1, 0, 2).reshape(D, -1)
    u5 = proj_plain(n1, full_w_in, name="in_proj", tm=tm_sq, tn=4 * W, out_dtype=F32, out_slabs=5)
    tri_bf, tri_f = chunk_triangles(tm_mix)
    y2, o_pre, st_prev = mixer_fwd(u5, pool_w_bf, scale4, theta4, gn4, tri_bf, tri_f, seqs=seqs, seq_len=seq_len,
                                   tm=tm_mix)
    (_, (wo3, wq3, wkv3, wao3)), = split_wait(ag_started[0:1], y2, name="weights_gather_wait_attn", scatter=False)
    full_w_out, full_xw_q, full_xw_o = wo3.reshape(D, D), wq3.reshape(D, D), wao3.reshape(D, D)
    tn = 4 * W
    h1, n2, q = proj_res_norm(y2, full_w_out, x2, norm_xq, full_xw_q, name="out_q_proj", tm=tm_sq, tn=tn)
    kv3, memn = proj_norm(mem2, norm_mem, wkv3, name="kv_proj", tm=tkv, tn=wkv3.shape[2], out_dtype=BF16,
                          out_slabs=2)
    o_att = attn_fwd(q, kv3, seqs=seqs, seq_len=seq_len, n_mem=n_mem, tm=tm_att)
    h2, n3 = proj_res_norm(o_att, full_xw_o, h1, norm_mlp, name="attn_out_proj", tm=tm_sq, tn=tn)
    (_, (wup3,)), = split_wait(ag_started[1:2], h2, name="weights_gather_wait_up", scatter=False)
    tn_up = wup3.shape[2]
    aa = proj_plain(n3, wup3, name="up_proj", tm=tm_mid, tn=tn_up, relu2=True)
    (_, (wdn3,)), = split_wait(ag_started[2:3], aa, name="weights_gather_wait_down", scatter=False)
    full_w_down = wdn3.reshape(-1, D)
    dh3, dh3b, sq_err, dg_final = proj_res_loss(aa, full_w_down, h2, g_final, tgt2, name="down_proj_loss",
                                                tm=tm_mid, tn=tn)

    def send(parts, name):
        srcs = [p.reshape((N_DEV, -1, p.shape[-1])) for p in parts]
        lands = [lax.empty(s.shape, BF16) for s in srcs]
        started, token = split_start([(srcs, lands)], name=name, scatter=True)
        return started[0], token

    gw_down = wgrad(aa, dh3b, name="down_proj_wgrad", tt=tm_mid, tn=tn)
    dap = back_plain(dh3b, full_w_down, name="down_proj_bwd", tm=tm_mid, tn=tn, out_dtype=BF16, relu2_value=aa)
    gw_up = wgrad(n3, dap, name="up_proj_wgrad", tt=tm_mid, tn=tn_up, out_slabs=N_DEV)
    sent_mlp, tok = send([gw_down, gw_up], "grads_send_mlp")
    dh2, dh2b, do_att, dg_mlp = back_norm(dap, wup3, h2, norm_mlp, dh3, name="up_proj_bwd", tm=tm_mid, tk=tn_up,
                                          w_next=full_xw_o, after=tok)
    gxw_o = wgrad(o_att, dh2b, name="attn_out_proj_wgrad", tt=tm_sq, tn=tn)
    dq, dkv3 = attn_bwd(q, kv3, do_att, seqs=seqs, seq_len=seq_len, n_mem=n_mem, tm=tm_att)
    gxw_q = wgrad(n2, dq, name="q_proj_wgrad", tt=tm_sq, tn=tn)
    gxw_kv = wgrad(memn, dkv3, name="kv_proj_wgrad", tt=tkv, tn=wkv3.shape[2], out_slabs=N_DEV)
    dg_mem = back_norm(dkv3, wkv3, mem2, norm_mem, None, name="kv_proj_bwd", tm=tkv, tk=wkv3.shape[2])
    dh1, dh1b, dy2, dg_xq = back_norm(dq, full_xw_q, h1, norm_xq, dh2, name="q_proj_bwd", tm=tm_sq, tk=D,
                                      w_next=full_w_out, next_slabs=2)
    gw_out = wgrad(y2, dh1b, name="out_proj_wgrad", tt=tm_sq, tn=tn)
    sent_attn, tok = send([gxw_o, gxw_q, gxw_kv, gw_out], "grads_send_attn")
    du5, dpw, dsc, dlb, dgn = mixer_bwd(u5, dy2, o_pre, st_prev, pool_w_bf, scale4, theta4, gn4, tri_bf, tri_f, tok,
                                        seqs=seqs, seq_len=seq_len, tm=tm_mix)
    gw_in = wgrad(n1, du5, name="in_proj_wgrad", tt=tm_sq, tn=tn)
    gw_in_slots = gw_in.reshape(D, N_DEV, -1).transpose(1, 0, 2)
    sent_in, tok = send([gw_in_slots], "grads_send_in")
    dx, dg_mix = back_norm(du5, full_w_in, x2, norm_mix, dh1, name="in_proj_bwd", tm=tm_sq, tk=tn, bf16_copy=False,
                           after=tok)

    dlb_row = dlb.reshape(1, 4 * W)
    buf_vec = _pad_rows(jnp.concatenate([dg_mix, dg_xq, dg_mem, dg_mlp, dg_final, sq_err], axis=0), 8)
    buf_half = _pad_rows(jnp.concatenate([dsc.reshape(1, 4 * W), dgn.reshape(1, 4 * W), dlb_row, -dlb_row], axis=0), 8)
    small_src = [dpw.reshape(4 * W, W), buf_vec, buf_half]
    small_land = [lax.dynamic_update_slice(lax.empty((N_DEV,) + b.shape, F32), b[None], (me, 0, 0))
                  for b in small_src]
    small_started, tok = split_start([(small_src, small_land)], name="small_grads_start", scatter=False)

    done = split_wait([sent_mlp, sent_attn, sent_in], tok, name="grads_wait", scatter=True)
    slots = dict(w_down=(0, 0), w_up=(0, 1), xw_o=(1, 0), xw_q=(1, 1), xw_kv=(1, 2), w_out=(1, 3), w_in=(2, 0))
    own = {n: done[gi][0][ai] for n, (gi, ai) in slots.items()}
    got = {n: done[gi][1][ai] for n, (gi, ai) in slots.items()}
    res = {}
    for names in ADAMW_CALLS:
        shp = w[names[0]].shape
        r = adamw_sharded(me1, [own[n] for n in names], [got[n] for n in names], [w[n][0] for n in names],
                          [mom[n][0] for n in names], [var[n][0] for n in names], name="adamw_" + names[0],
                          tr=min(256, shp[1]))
        for n, outs in zip(names, r):
            for kind, a in zip("gdmv", outs):
                res[kind, n] = a.reshape(shp)
    (_, small_parts), = split_wait(small_started, res["g", BIG[-1]], name="small_grads_wait", scatter=False)
    loss = 0.5 * jnp.sum(small_parts[1][:, 5, :]) / D
    r = adamw_replicated(small_parts, [w[n].reshape(v2) for n, v2, _, _ in SMALL],
                         [mom[n].reshape(v2) for n, v2, _, _ in SMALL],
                         [var[n].reshape(v2) for n, v2, _, _ in SMALL],
                         [(b, r0, v2[0]) for _, v2, b, r0 in SMALL])
    for kind, arrs in zip("gdmv", r):
        for (n, _, _, _), a in zip(SMALL, arrs):
            res[kind, n] = a.reshape(w[n].shape)

    out = [loss, dx.reshape(x.shape)]
    for kind in "gdmv":
        out += [res[kind, n] for n in WEIGHTS]
    return tuple(out)
```

```python
import jax
import jax.numpy as jnp
from jax import lax
from jax.experimental import pallas as pl
from jax.experimental.pallas import tpu as pltpu

F32 = jnp.float32
BF16 = jnp.bfloat16
EPS = 1e-6
CHUNK = 64
POOL_HALO = 16
HEAD_W = 128
HEADS_PER_STEP = 4
XATTN_HEADS = 4
N_DEV = 8
N_PEERS = N_DEV - 1
ADAM_LR = 0.001
ADAM_B1 = 0.9
ADAM_B2 = 0.999
ADAM_EPS = 1e-08
ADAM_WD = 0.01
ADAM_STEP = 10
V7X_VMEM_LIMIT = 52 * 1024 * 1024
MESH = pl.DeviceIdType.MESH
HBM = pl.BlockSpec(memory_space=pltpu.HBM)
SEM = pl.BlockSpec(memory_space=pltpu.SEMAPHORE)


def _cparams(dims):
    return pltpu.CompilerParams(dimension_semantics=dims, vmem_limit_bytes=V7X_VMEM_LIMIT)


def _sigmoid(v):
    return 0.5 * jnp.tanh(0.5 * v) + 0.5


def _dot(a, b):
    return jnp.dot(a, b, preferred_element_type=F32)


def _dot_nt(a, b):
    return lax.dot_general(a, b, (((1,), (1,)), ((), ())), preferred_element_type=F32)


def _dot_tn(a, b):
    return lax.dot_general(a, b, (((0,), (0,)), ((), ())), preferred_element_type=F32)


def _tri_apply(tri, v):
    hi = v.astype(BF16)
    lo = (v - hi.astype(F32)).astype(BF16)
    return _dot(tri, hi) + _dot(tri, lo)


def _mat_shape(a):
    return a.shape if a.ndim == 2 else (a.shape[1], a.shape[0] * a.shape[2])


def _out_struct(rows, n, slabs, dtype):
    return jax.ShapeDtypeStruct((rows, n) if slabs is None else (slabs, rows, n // slabs), dtype)


def _resident(a):
    nd = a.ndim
    return pl.BlockSpec(a.shape, lambda i: (0,) * nd, pipeline_mode=pl.Buffered(1))


def _row_block(a, tm):
    if a.ndim == 2:
        return pl.BlockSpec((tm, a.shape[1]), lambda i: (i, 0))
    return pl.BlockSpec((a.shape[0], tm, a.shape[2]), lambda i: (0, i, 0))


def _cols(ref, c, width):
    if len(ref.shape) == 2:
        return ref[:, c * width:(c + 1) * width]
    per = ref.shape[2] // width
    if per == 1:
        return ref[c]
    return ref[c // per, :, (c % per) * width:(c % per + 1) * width]


def _set_cols(ref, c, width, val):
    if len(ref.shape) == 2:
        ref[:, c * width:(c + 1) * width] = val
        return
    per = ref.shape[2] // width
    if per == 1:
        ref[c] = val
    else:
        ref[c // per, :, (c % per) * width:(c % per + 1) * width] = val


def _all_cols(ref):
    if len(ref.shape) == 2:
        return ref[...]
    return jnp.concatenate([ref[s] for s in range(ref.shape[0])], axis=1)


def _rms(x):
    return lax.rsqrt(jnp.mean(x * x, axis=-1, keepdims=True) + EPS)


def _row_params():
    return _cparams(("arbitrary",))


def proj_norm(h, g, w, *, name, tm, tn, out_dtype, out_slabs=None):
    T, D = h.shape
    N = _mat_shape(w)[1]
    o_shape = _out_struct(T, N, out_slabs, out_dtype)

    def body(h_ref, g_ref, w_ref, o_ref, n_ref):
        x = h_ref[...]
        n = (x * _rms(x) * g_ref[...]).astype(BF16)
        n_ref[...] = n
        for c in range(N // tn):
            _set_cols(o_ref, c, tn, _dot(n, _cols(w_ref, c, tn)).astype(out_dtype))

    return pl.pallas_call(
        body, name=name, grid=(T // tm,),
        in_specs=[_row_block(h, tm), pl.BlockSpec((1, D), lambda i: (0, 0)), _resident(w)],
        out_specs=[_row_block(o_shape, tm), pl.BlockSpec((tm, D), lambda i: (i, 0))],
        out_shape=[o_shape, jax.ShapeDtypeStruct((T, D), BF16)],
        compiler_params=_row_params(),
    )(h, g, w)


def prenorm(h, g, after, *, tm):
    T, D = h.shape

    def body(h_ref, g_ref, _after_ref, n_ref):
        x = h_ref[...]
        n_ref[...] = (x * _rms(x) * g_ref[...]).astype(BF16)

    row = pl.BlockSpec((tm, D), lambda i: (i, 0))
    return pl.pallas_call(
        body, name="prenorm", grid=(T // tm,),
        in_specs=[row, pl.BlockSpec((1, D), lambda i: (0, 0)), _anchor_spec(after)],
        out_specs=row, out_shape=jax.ShapeDtypeStruct((T, D), BF16),
        compiler_params=_row_params(),
    )(h, g, after)


def proj_plain(a, w, *, name, tm, tn, out_dtype=BF16, out_slabs=None, relu2=False):
    T = a.shape[0]
    N = _mat_shape(w)[1]

    def body(a_ref, w_ref, o_ref, *d_ref):
        av = a_ref[...]
        for c in range(N // tn):
            z = _dot(av, _cols(w_ref, c, tn))
            if relu2:
                z = jnp.maximum(z, 0.0)
                _set_cols(d_ref[0], c, tn, (z + z).astype(out_dtype))
                z = z * z
            _set_cols(o_ref, c, tn, z.astype(out_dtype))

    o_shape = _out_struct(T, N, out_slabs, out_dtype)
    n_out = 2 if relu2 else 1
    outs = pl.pallas_call(
        body, name=name, grid=(T // tm,),
        in_specs=[_row_block(a, tm), _resident(w)],
        out_specs=[_row_block(o_shape, tm)] * n_out, out_shape=[o_shape] * n_out,
        compiler_params=_row_params(),
    )(a, w)
    return outs if relu2 else outs[0]


def proj_res_norm(a, w, res, g, w_next=None, *, name, tm, tn):
    T = res.shape[0]
    D = w.shape[1]
    chained = w_next is not None

    def body(*refs):
        a_ref, w_ref, r_ref, g_ref = refs[:4]
        h_ref, n_ref = refs[4 + chained], refs[5 + chained]
        av = _all_cols(a_ref)
        for c in range(D // tn):
            sl = slice(c * tn, (c + 1) * tn)
            h_ref[:, sl] = r_ref[:, sl] + _dot(av, w_ref[:, sl])
        hv = h_ref[...]
        n = (hv * _rms(hv) * g_ref[...]).astype(BF16)
        n_ref[...] = n
        if chained:
            for c in range(D // tn):
                sl = slice(c * tn, (c + 1) * tn)
                refs[-1][:, sl] = _dot(n, refs[4][:, sl]).astype(BF16)

    row = pl.BlockSpec((tm, D), lambda i: (i, 0))
    half = jax.ShapeDtypeStruct((T, D), BF16)
    return pl.pallas_call(
        body, name=name, grid=(T // tm,),
        in_specs=[_row_block(a, tm), _resident(w), row, pl.BlockSpec((1, D), lambda i: (0, 0))]
        + ([_resident(w_next)] if chained else []),
        out_specs=[row, row] + ([row] if chained else []),
        out_shape=[jax.ShapeDtypeStruct((T, D), F32), half] + ([half] if chained else []),
        compiler_params=_row_params(),
    )(*([a, w, res, g] + ([w_next] if chained else [])))


def proj_res_loss(a, w, res, g, target, *, name, tm, tn):
    T = res.shape[0]
    D = w.shape[1]

    def body(a_ref, w_ref, r_ref, g_ref, t_ref, dh_ref, dhb_ref, ls_ref, dg_ref):
        i = pl.program_id(0)
        gv = g_ref[...]
        ls, dg = 0.0, 0.0
        halves = [slice(s * (tm // 2), (s + 1) * (tm // 2)) for s in range(2)]
        for rows in halves:
            av = a_ref[rows, :]
            for c in range(D // tn):
                sl = slice(c * tn, (c + 1) * tn)
                dh_ref[rows, sl] = r_ref[rows, sl] + _dot(av, w_ref[:, sl])
        for rows in halves:
            x = dh_ref[rows, :]
            r = _rms(x)
            xr = x * r
            d = xr * gv - t_ref[rows, :]
            dy = d * (1.0 / D)
            dyg = dy * gv
            dx = r * (dyg - xr * jnp.mean(dyg * xr, axis=-1, keepdims=True))
            dh_ref[rows, :] = dx
            dhb_ref[rows, :] = dx.astype(BF16)
            ls = ls + jnp.sum(d * d, axis=0, keepdims=True)
            dg = dg + jnp.sum(dy * xr, axis=0, keepdims=True)

        @pl.when(i == 0)
        def _():
            ls_ref[...] = ls
            dg_ref[...] = dg

        @pl.when(i > 0)
        def _():
            ls_ref[...] += ls
            dg_ref[...] += dg

    row = pl.BlockSpec((tm, D), lambda i: (i, 0))
    vec = pl.BlockSpec((1, D), lambda i: (0, 0))
    return pl.pallas_call(
        body, name=name, grid=(T // tm,),
        in_specs=[_row_block(a, tm), _resident(w), row, vec, row],
        out_specs=[row, row, vec, vec],
        out_shape=[jax.ShapeDtypeStruct((T, D), F32), jax.ShapeDtypeStruct((T, D), BF16),
                   jax.ShapeDtypeStruct((1, D), F32), jax.ShapeDtypeStruct((1, D), F32)],
        compiler_params=_row_params(),
    )(a, w, res, g, target)


def _anchor_spec(after):
    return pl.BlockSpec(after.shape, lambda i: (0, 0))


def back_plain(a, w, *, name, tm, tn, out_dtype, out_slabs=None, factor=None, after=None):
    T = a.shape[0]
    N = w.shape[0]
    has_z = factor is not None
    o_shape = _out_struct(T, N, out_slabs, out_dtype)

    def body(*refs):
        a_ref, w_ref = refs[0], refs[1]
        o_ref = refs[-1]
        av = a_ref[...]
        for c in range(N // tn):
            out = _dot_nt(av, w_ref[c * tn:(c + 1) * tn, :])
            if has_z:
                out = out * refs[2][:, c * tn:(c + 1) * tn].astype(F32)
            _set_cols(o_ref, c, tn, out.astype(out_dtype))

    in_specs, args = [_row_block(a, tm), _resident(w)], [a, w]
    if has_z:
        in_specs.append(_row_block(factor, tm))
        args.append(factor)
    if after is not None:
        in_specs.append(_anchor_spec(after))
        args.append(after)
    return pl.pallas_call(
        body, name=name, grid=(T // tm,),
        in_specs=in_specs, out_specs=_row_block(o_shape, tm), out_shape=o_shape,
        compiler_params=_row_params(),
    )(*args)


def back_norm(a, w, h, g, dres, *, name, tm, tk, bf16_copy=True, w_next=None, next_dtype=BF16, next_slabs=None,
              after=None):
    T, K = _mat_shape(a)
    D = h.shape[1]
    with_dh = dres is not None
    chained = w_next is not None
    n_in = 4 + with_dh + chained
    tn = 4 * HEAD_W

    def body(*refs):
        a_ref, w_ref, h_ref, g_ref = refs[:4]
        outs = refs[n_in + (after is not None):]
        i = pl.program_id(0)
        if len(w_ref.shape) == 2:
            dn = _dot_nt(_all_cols(a_ref).astype(BF16), w_ref[...])
        else:
            dn = None
            for kc in range(K // tk):
                part = _dot_nt(_cols(a_ref, kc, tk).astype(BF16), _cols(w_ref, kc, tk))
                dn = part if dn is None else dn + part
        x = h_ref[...]
        r = _rms(x)
        xr = x * r
        dgp = jnp.sum(dn * xr, axis=0, keepdims=True)
        dg_ref = outs[-1]

        @pl.when(i == 0)
        def _():
            dg_ref[...] = dgp

        @pl.when(i > 0)
        def _():
            dg_ref[...] += dgp

        if with_dh:
            dyg = dn * g_ref[...]
            out = refs[4][...] + r * (dyg - xr * jnp.mean(dyg * xr, axis=-1, keepdims=True))
            outs[0][...] = out
            outb = out.astype(BF16)
            if bf16_copy:
                outs[1][...] = outb
            if chained:
                wn_ref, nx_ref = refs[5], outs[-2]
                for c in range(wn_ref.shape[0] // tn):
                    _set_cols(nx_ref, c, tn, _dot_nt(outb, wn_ref[c * tn:(c + 1) * tn, :]).astype(next_dtype))

    row = pl.BlockSpec((tm, D), lambda i: (i, 0))
    vec = pl.BlockSpec((1, D), lambda i: (0, 0))
    in_specs, args = [_row_block(a, tm), _resident(w), row, vec], [a, w, h, g]
    out_specs, out_shape = [], []
    if with_dh:
        in_specs.append(row)
        args.append(dres)
        out_specs.append(row)
        out_shape.append(jax.ShapeDtypeStruct((T, D), F32))
        if bf16_copy:
            out_specs.append(row)
            out_shape.append(jax.ShapeDtypeStruct((T, D), BF16))
    if chained:
        in_specs.append(_resident(w_next))
        args.append(w_next)
        nx_shape = _out_struct(T, w_next.shape[0], next_slabs, next_dtype)
        out_specs.append(_row_block(nx_shape, tm))
        out_shape.append(nx_shape)
    out_specs.append(vec)
    out_shape.append(jax.ShapeDtypeStruct((1, D), F32))
    if after is not None:
        in_specs.append(_anchor_spec(after))
        args.append(after)
    outs = pl.pallas_call(
        body, name=name, grid=(T // tm,),
        in_specs=in_specs, out_specs=out_specs, out_shape=out_shape,
        compiler_params=_row_params(),
    )(*args)
    return outs if len(outs) > 1 else outs[0]


def wgrad(a, b, *, name, tt, tn, out_slabs=None):
    T, K = _mat_shape(a)
    N = _mat_shape(b)[1]
    nt = T // tt
    o_shape = _out_struct(K, N, out_slabs, BF16)

    flipped = K > N and out_slabs is None

    def body(a_ref, b_ref, o_ref, acc_ref):
        t = pl.program_id(0)

        @pl.when(t == 0)
        def _():
            acc_ref[...] = jnp.zeros_like(acc_ref)

        if flipped:
            bt = _all_cols(b_ref).astype(BF16).T
            for c in range(K // tn):
                acc_ref[:, c * tn:(c + 1) * tn] += _dot(bt, _cols(a_ref, c, tn).astype(BF16))
        else:
            at = _all_cols(a_ref).astype(BF16).T
            for c in range(N // tn):
                acc_ref[:, c * tn:(c + 1) * tn] += _dot(at, _cols(b_ref, c, tn).astype(BF16))

        @pl.when(t == nt - 1)
        def _():
            if flipped:
                for c in range(K // tn):
                    o_ref[c * tn:(c + 1) * tn, :] = acc_ref[:, c * tn:(c + 1) * tn].T.astype(BF16)
            else:
                for c in range(N // tn):
                    _set_cols(o_ref, c, tn, acc_ref[:, c * tn:(c + 1) * tn].astype(BF16))

    return pl.pallas_call(
        body, name=name, grid=(nt,),
        in_specs=[_row_block(a, tt), _row_block(b, tt)],
        out_specs=_resident(o_shape), out_shape=o_shape,
        scratch_shapes=[pltpu.VMEM((N, K) if flipped else (K, N), F32)],
        compiler_params=_row_params(),
    )(a, b)


def chunk_triangles(tm):
    r = lax.broadcasted_iota(jnp.int32, (tm, tm), 0)
    c = lax.broadcasted_iota(jnp.int32, (tm, tm), 1)
    same = (r // CHUNK) == (c // CHUNK)
    tri = jnp.stack([same & (c <= r), same & (c >= r)]).astype(F32)
    return tri.astype(BF16), tri


def _tri_spec(tm):
    return pl.BlockSpec((2, tm, tm), lambda g, s, i: (0, 0, 0))


def _chunk_row(v, r, nc):
    return jnp.concatenate([jnp.broadcast_to(v[c * CHUNK + r:c * CHUNK + r + 1], (CHUNK, v.shape[1]))
                            for c in range(nc)], axis=0)


def _block_diag(v, nc):
    chunk = lax.broadcasted_iota(jnp.int32, (v.shape[0], 1), 0) // CHUNK
    return jnp.concatenate([jnp.where(chunk == c, v, jnp.zeros_like(v)) for c in range(nc)], axis=1)


def _pool_windows_back(ext_ref, tm):
    n = tm + 32
    ext_ref[1, 8:n] = ext_ref[0, 8:n] + ext_ref[0, 7:n - 1]
    ext_ref[2, 16:n] = ext_ref[1, 16:n] + ext_ref[1, 14:n - 2]
    ext_ref[3, 24:n] = ext_ref[2, 24:n] + ext_ref[2, 20:n - 4]
    s2 = ext_ref[1, 32:n]
    s4 = ext_ref[2, 32:n]
    s8 = ext_ref[3, 32:n]
    s16 = s8 + ext_ref[3, 24:n - 8]
    return s2, s4, s8, s16


def _pool_windows_fwd(ext_ref, tm):
    n = tm + 32
    ext_ref[1, 0:n - 8] = ext_ref[0, 0:n - 8] + ext_ref[0, 1:n - 7]
    ext_ref[2, 0:n - 16] = ext_ref[1, 0:n - 16] + ext_ref[1, 2:n - 14]
    ext_ref[3, 0:n - 24] = ext_ref[2, 0:n - 24] + ext_ref[2, 4:n - 20]
    s2 = ext_ref[1, 0:tm]
    s4 = ext_ref[2, 0:tm]
    s8 = ext_ref[3, 0:tm]
    s16 = s8 + ext_ref[3, 8:tm + 8]
    return s2, s4, s8, s16


def _select_window(g, s2, s4, s8, s16):
    return jnp.where(g == 0, s2, jnp.where(g == 1, s4, jnp.where(g == 2, s8, s16)))


def _pool_count(g, pos):
    width = lax.shift_left(jnp.int32(2), g)
    return jnp.minimum(pos + 1, width).astype(F32)


def mixer_fwd(u5, pool_w_bf, scale4, theta4, gn4, tri_bf, tri_f, *, seqs, seq_len, tm):
    T = u5.shape[1]
    tps = seq_len // tm
    nc = tm // CHUNK
    W = HEAD_W

    H = HEADS_PER_STEP
    heads = range(H)

    def body(u_ref, pw_ref, sc_ref, th_ref, gn_ref, tri_ref, msk_ref, y_ref, o_ref, st_ref, halo_ref, ext_ref, s_ref):
        g = pl.program_id(0)
        i = pl.program_id(2)

        @pl.when(i == 0)
        def _():
            halo_ref[...] = jnp.zeros_like(halo_ref)
            s_ref[...] = jnp.zeros_like(s_ref)

        row = lax.broadcasted_iota(jnp.int32, (tm, 1), 0)
        cols = [slice(h * W, (h + 1) * W) for h in heads]

        pooled = []
        for h in heads:
            grp = g * H + h
            up = u_ref[0, :, cols[h]]
            ext_ref[h, 0, 0:16] = jnp.zeros((16, W), F32)
            ext_ref[h, 0, 16:32] = halo_ref[h]
            ext_ref[h, 0, 32:32 + tm] = up
            win = _select_window(grp, *_pool_windows_back(ext_ref.at[h], tm))
            pooled.append((win * (1.0 / _pool_count(grp, i * tm + row)) - up).astype(BF16))
            halo_ref[h] = up[tm - POOL_HALO:tm]
        mixed = [_dot(pooled[h], pw_ref[h]) for h in heads]
        for h in heads:
            y_ref[0, :, cols[h]] = (mixed[h] * sc_ref[h]).astype(BF16)

        zq, zf, zi, zg = u_ref[1], u_ref[2], u_ref[3], u_ref[4]
        th = [th_ref[h] for h in heads]
        lb = jnp.concatenate([_sigmoid(t[0:1, :] - t[1:2, :]) for t in th], axis=1)
        f = lb + (1.0 - lb) * _sigmoid(zf)
        kk = 1.0 - f
        q = zq * _sigmoid(zq)
        G = _tri_apply(tri_ref[0], jnp.log(f))
        Gm, Gl = _chunk_row(G, CHUNK // 2 - 1, nc), _chunk_row(G, CHUNK - 1, nc)
        vb = zi.astype(BF16)
        qrb = (q * jnp.exp(G - Gm)).astype(BF16)
        krb = (kk * jnp.exp(Gm - G)).astype(BF16)
        keb = (kk * jnp.exp(Gl - G)).astype(BF16)
        qgb = (q * jnp.exp(G)).astype(BF16)
        mask = msk_ref[0] > 0.5
        a = [jnp.where(mask, _dot_nt(qrb[:, cols[h]], krb[:, cols[h]]), 0.0).astype(BF16) for h in heads]
        d_st = [_dot_tn(vb[:, cols[h]], _block_diag(keb[:, cols[h]], nc)) for h in heads]
        o_intra = [_dot(a[h], vb[:, cols[h]]) for h in heads]
        st_cat = []
        for h in heads:
            st = s_ref[h]
            states = []
            for c in range(nc):
                states.append(st.astype(BF16))
                st_ref[c, h] = states[-1]
                st = st * jnp.exp(G[(c + 1) * CHUNK - 1:(c + 1) * CHUNK, cols[h]]) + d_st[h][:, c * W:(c + 1) * W]
            s_ref[h] = st
            st_cat.append(jnp.concatenate(states, axis=1))
        o = [o_intra[h] + _dot_nt(_block_diag(qgb[:, cols[h]], nc), st_cat[h]) for h in heads]
        gate = zg * _sigmoid(zg)
        for h in heads:
            o_ref[:, cols[h]] = o[h]
            r = lax.rsqrt(jnp.mean(o[h] * o[h], axis=-1, keepdims=True) + EPS)
            y_ref[1, :, cols[h]] = (o[h] * r * gn_ref[h] * gate[:, cols[h]]).astype(BF16)

    def rb(s, i):
        return s * tps + i

    def per_head(*shape):
        return pl.BlockSpec((H,) + shape, lambda g, s, i: (g,) + (0,) * len(shape))

    return pl.pallas_call(
        body, name="mixer_fwd", grid=(4 // H, seqs, tps),
        in_specs=[pl.BlockSpec((5, tm, H * W), lambda g, s, i: (0, rb(s, i), g)),
                  per_head(W, W), per_head(1, W), per_head(2, W), per_head(1, W),
                  _tri_spec(tm), _tri_spec(tm)],
        out_specs=[pl.BlockSpec((2, tm, H * W), lambda g, s, i: (0, rb(s, i), g)),
                   pl.BlockSpec((tm, H * W), lambda g, s, i: (rb(s, i), g)),
                   pl.BlockSpec((nc, H, W, W), lambda g, s, i: (rb(s, i), g, 0, 0))],
        out_shape=[jax.ShapeDtypeStruct((2, T, 4 * W), BF16),
                   jax.ShapeDtypeStruct((T, 4 * W), F32),
                   jax.ShapeDtypeStruct((T // CHUNK, 4, W, W), BF16)],
        scratch_shapes=[pltpu.VMEM((H, POOL_HALO, W), F32),
                        pltpu.VMEM((H, 4, tm + 32, W), F32),
                        pltpu.VMEM((H, W, W), F32)],
        compiler_params=_cparams(("arbitrary", "arbitrary", "arbitrary")),
    )(u5, pool_w_bf, scale4, theta4, gn4, tri_bf, tri_f)


def mixer_bwd(u5, dy2, o_pre, st_prev, pool_w_bf, scale4, theta4, gn4, tri_bf, tri_f, after, *, seqs, seq_len, tm):
    T = u5.shape[1]
    tps = seq_len // tm
    nc = tm // CHUNK
    W = HEAD_W
    hb = tm // POOL_HALO

    H = HEADS_PER_STEP
    heads = range(H)

    def body(u_ref, uh_ref, dy_ref, o_ref, st_ref, pw_ref, sc_ref, th_ref, gn_ref, tri_ref, msk_ref, _after_ref,
             du_ref, dpw_ref, dsc_ref, dlb_ref, dgn_ref, nxt_ref, ext_ref, ds_ref):
        g = pl.program_id(0)
        s = pl.program_id(1)
        i = pl.program_id(2)
        tile = tps - 1 - i
        first = (s == 0) & (i == 0)

        @pl.when(i == 0)
        def _():
            nxt_ref[...] = jnp.zeros_like(nxt_ref)
            ds_ref[...] = jnp.zeros_like(ds_ref)

        row = lax.broadcasted_iota(jnp.int32, (tm, 1), 0)
        cols = [slice(h * W, (h + 1) * W) for h in heads]

        def accumulate(ref, h, val):
            @pl.when(first)
            def _():
                ref[h] = val

            @pl.when(jnp.logical_not(first))
            def _():
                ref[h] += val

        def per_head(fn):
            return jnp.concatenate([jnp.broadcast_to(fn(cols[h]), (tm, W)) for h in heads], axis=1)

        inv_cnt, pb, dz = [], [], []
        for h in heads:
            grp = g * H + h
            inv_cnt.append(1.0 / _pool_count(grp, tile * tm + row))
            ext = ext_ref.at[h]
            up = u_ref[0, :, cols[h]]
            ext[0, 0:16] = jnp.zeros((16, W), F32)
            ext[0, 16:32] = jnp.where(tile == 0, 0.0, uh_ref[:, cols[h]])
            ext[0, 32:32 + tm] = up
            win = _select_window(grp, *_pool_windows_back(ext, tm))
            pb.append((win * inv_cnt[h] - up).astype(BF16))
            dz.append((dy_ref[0, :, cols[h]].astype(F32) * sc_ref[h]).astype(BF16))
        z = [_dot(pb[h], pw_ref[h]) for h in heads]
        dp = [_dot_nt(dz[h], pw_ref[h]) for h in heads]
        dpw = [_dot_tn(pb[h], dz[h]) for h in heads]
        for h in heads:
            accumulate(dsc_ref, h, jnp.sum(dy_ref[0, :, cols[h]].astype(F32) * z[h], axis=0, keepdims=True))
            accumulate(dpw_ref, h, dpw[h])
            ext = ext_ref.at[h]
            e = dp[h] * inv_cnt[h]
            ext[0, 0:tm] = e
            ext[0, tm:tm + 16] = nxt_ref[h]
            ext[0, tm + 16:tm + 32] = jnp.zeros((16, W), F32)
            lead = _select_window(g * H + h, *_pool_windows_fwd(ext, tm))
            nxt_ref[h] = e[0:POOL_HALO]
            du_ref[0, :, cols[h]] = (lead - dp[h]).astype(BF16)

        zq, zf, zi, zg = u_ref[1], u_ref[2], u_ref[3], u_ref[4]
        lb = jnp.concatenate([_sigmoid(th_ref[h][0:1, :] - th_ref[h][1:2, :]) for h in heads], axis=1)
        gn = jnp.concatenate([gn_ref[h] for h in heads], axis=1)
        sig, sq, sg = _sigmoid(zf), _sigmoid(zq), _sigmoid(zg)
        f = lb + (1.0 - lb) * sig
        kk = 1.0 - f
        q = zq * sq
        G = _tri_apply(tri_ref[0], jnp.log(f))

        dyh = dy_ref[1].astype(F32)
        o = o_ref[...]
        sqr = o * o
        r = per_head(lambda cs: lax.rsqrt(jnp.mean(sqr[:, cs], axis=-1, keepdims=True) + EPS))
        orr = o * r
        du_ref[4] = (dyh * (orr * gn) * (sg * (1.0 + zg * (1.0 - sg)))).astype(BF16)
        don = dyh * (zg * sg)
        dgn = jnp.sum(don * orr, axis=0, keepdims=True)
        dog = don * gn
        dog_orr = dog * orr
        do = r * (dog - orr * per_head(lambda cs: jnp.mean(dog_orr[:, cs], axis=-1, keepdims=True)))

        Gm, Gl = _chunk_row(G, CHUNK // 2 - 1, nc), _chunk_row(G, CHUNK - 1, nc)
        e_q, e_k, e_e, e_g = jnp.exp(G - Gm), jnp.exp(Gm - G), jnp.exp(Gl - G), jnp.exp(G)
        qr, kr, ke, qg = q * e_q, kk * e_k, kk * e_e, q * e_g
        qrb, krb, keb, qgb = qr.astype(BF16), kr.astype(BF16), ke.astype(BF16), qg.astype(BF16)
        vb = zi.astype(BF16)
        dob = do.astype(BF16)
        lower, upper = msk_ref[0] > 0.5, msk_ref[1] > 0.5
        da = [jnp.where(lower, _dot_nt(dob[:, cs], vb[:, cs]), 0.0).astype(BF16) for cs in cols]
        a_t = [jnp.where(upper, _dot_nt(krb[:, cs], qrb[:, cs]), 0.0).astype(BF16) for cs in cols]
        da_t = [jnp.where(upper, _dot_nt(vb[:, cs], dob[:, cs]), 0.0).astype(BF16) for cs in cols]
        u_cat = [_dot_tn(dob[:, cs], _block_diag(qgb[:, cs], nc)) for cs in cols]
        dqr = [_dot(da[h], krb[:, cols[h]]) for h in heads]
        dkr = [_dot(da_t[h], qrb[:, cols[h]]) for h in heads]
        dv = [_dot(a_t[h], dob[:, cols[h]]) for h in heads]
        dsn_rows, dsn_cols, ddecay = [], [], [[None] * H for _ in range(nc)]
        for h in heads:
            dsn = ds_ref[h]
            dsn_b = [None] * nc
            for c in reversed(range(nc)):
                decay = jnp.exp(G[(c + 1) * CHUNK - 1:(c + 1) * CHUNK, cols[h]])
                dsn_b[c] = dsn.astype(BF16)
                ddecay[c][h] = jnp.sum(dsn * st_ref[c, h].astype(F32), axis=0, keepdims=True) * decay
                dsn = u_cat[h][:, c * W:(c + 1) * W] + dsn * decay
            ds_ref[h] = dsn
            dsn_rows.append(jnp.concatenate(dsn_b, axis=0))
            dsn_cols.append(jnp.concatenate(dsn_b, axis=1))
        st_rows = [jnp.concatenate([st_ref[c, h] for c in range(nc)], axis=0) for h in heads]
        dqg = [_dot(_block_diag(dob[:, cols[h]], nc), st_rows[h]) for h in heads]
        dke = [_dot(_block_diag(vb[:, cols[h]], nc), dsn_rows[h]) for h in heads]
        dv = [dv[h] + _dot_nt(_block_diag(keb[:, cols[h]], nc), dsn_cols[h]) for h in heads]
        dqr, dkr, dqg, dke, dv = (jnp.concatenate(parts, axis=1) for parts in (dqr, dkr, dqg, dke, dv))
        t_mid, t_qg, t_ke = dkr * kr - dqr * qr, dqg * qg, dke * ke
        dq = dqr * e_q + dqg * e_g
        dk = dkr * e_k + dke * e_e
        crow = lax.broadcasted_iota(jnp.int32, (CHUNK, 1), 0)
        ends = []
        for c in range(nc):
            sl = slice(c * CHUNK, (c + 1) * CHUNK)
            dgm = jnp.sum(t_mid[sl], axis=0, keepdims=True)
            dgl = jnp.sum(t_ke[sl], axis=0, keepdims=True) + jnp.concatenate(ddecay[c], axis=1)
            ends.append(jnp.where(crow == CHUNK // 2 - 1, dgm, 0.0) + jnp.where(crow == CHUNK - 1, dgl, 0.0))
        dG = t_qg - t_ke - t_mid + jnp.concatenate(ends, axis=0)
        dlogf = _tri_apply(tri_ref[1], dG)
        df = dlogf / f - dk
        du_ref[1] = (dq * (sq * (1.0 + zq * (1.0 - sq)))).astype(BF16)
        du_ref[2] = (df * (1.0 - lb) * (sig * (1.0 - sig))).astype(BF16)
        du_ref[3] = dv.astype(BF16)
        dlb = jnp.sum(df * (1.0 - sig), axis=0, keepdims=True) * (lb * (1.0 - lb))
        for h in heads:
            accumulate(dgn_ref, h, dgn[:, cols[h]])
            accumulate(dlb_ref, h, dlb[:, cols[h]])

    def rb(s, i):
        return s * tps + (tps - 1 - i)

    def per_head_spec(*shape):
        return pl.BlockSpec((H,) + shape, lambda g, s, i: (g,) + (0,) * len(shape))

    vec, mat = per_head_spec(1, W), per_head_spec(W, W)
    return pl.pallas_call(
        body, name="mixer_bwd", grid=(4 // H, seqs, tps),
        in_specs=[pl.BlockSpec((5, tm, H * W), lambda g, s, i: (0, rb(s, i), g)),
                  pl.BlockSpec((None, POOL_HALO, H * W), lambda g, s, i: (0, jnp.maximum(rb(s, i) * hb - 1, 0), g)),
                  pl.BlockSpec((2, tm, H * W), lambda g, s, i: (0, rb(s, i), g)),
                  pl.BlockSpec((tm, H * W), lambda g, s, i: (rb(s, i), g)),
                  pl.BlockSpec((nc, H, W, W), lambda g, s, i: (rb(s, i), g, 0, 0)),
                  mat, vec, per_head_spec(2, W), vec, _tri_spec(tm), _tri_spec(tm),
                  pl.BlockSpec(after.shape, lambda g, s, i: (0, 0))],
        out_specs=[pl.BlockSpec((5, tm, H * W), lambda g, s, i: (0, rb(s, i), g)), mat, vec, vec, vec],
        out_shape=[jax.ShapeDtypeStruct((5, T, 4 * W), BF16),
                   jax.ShapeDtypeStruct((4, W, W), F32),
                   jax.ShapeDtypeStruct((4, 1, W), F32),
                   jax.ShapeDtypeStruct((4, 1, W), F32),
                   jax.ShapeDtypeStruct((4, 1, W), F32)],
        scratch_shapes=[pltpu.VMEM((H, POOL_HALO, W), F32),
                        pltpu.VMEM((H, 4, tm + 32, W), F32),
                        pltpu.VMEM((H, W, W), F32)],
        compiler_params=_cparams(("arbitrary", "arbitrary", "arbitrary")),
    )(u5, u5, dy2, o_pre, st_prev, pool_w_bf, scale4, theta4, gn4, tri_bf, tri_f, after)


def _attn_probs(q, k, hd):
    s = _dot_nt(q, k) * (1.0 / (hd ** 0.5))
    e = jnp.exp(s - jnp.max(s, axis=-1, keepdims=True))
    return e * (1.0 / jnp.sum(e, axis=-1, keepdims=True))


def attn_fwd(q, kv3, *, seqs, seq_len, n_mem, tm):
    T, D = q.shape
    hd = D // XATTN_HEADS
    tps = seq_len // tm

    cols = [slice(h * hd, (h + 1) * hd) for h in range(XATTN_HEADS)]

    def body(q_ref, kv_ref, o_ref):
        p = [_attn_probs(q_ref[:, cs], kv_ref[0, :, cs], hd) for cs in cols]
        for h, cs in enumerate(cols):
            o_ref[:, cs] = _dot(p[h].astype(BF16), kv_ref[1, :, cs]).astype(BF16)

    return pl.pallas_call(
        body, name="attn_fwd", grid=(seqs, tps),
        in_specs=[pl.BlockSpec((tm, D), lambda b, i: (b * tps + i, 0)),
                  pl.BlockSpec((2, n_mem, D), lambda b, i: (0, b, 0))],
        out_specs=pl.BlockSpec((tm, D), lambda b, i: (b * tps + i, 0)),
        out_shape=jax.ShapeDtypeStruct((T, D), BF16),
        compiler_params=_cparams(("parallel", "arbitrary")),
    )(q, kv3)


def attn_bwd(q, kv3, do, *, seqs, seq_len, n_mem, tm):
    T, D = q.shape
    hd = D // XATTN_HEADS
    tps = seq_len // tm

    cols = [slice(h * hd, (h + 1) * hd) for h in range(XATTN_HEADS)]

    def body(q_ref, kv_ref, do_ref, dq_ref, dkv_ref):
        i = pl.program_id(1)

        @pl.when(i == 0)
        def _():
            dkv_ref[...] = jnp.zeros_like(dkv_ref)

        p = [_attn_probs(q_ref[:, cs], kv_ref[0, :, cs], hd) for cs in cols]
        dp = [_dot_nt(do_ref[:, cs], kv_ref[1, :, cs]) for cs in cols]
        ds = [(p[h] * (dp[h] - jnp.sum(dp[h] * p[h], axis=-1, keepdims=True)) * (1.0 / (hd ** 0.5))).astype(BF16)
              for h in range(XATTN_HEADS)]
        for h, cs in enumerate(cols):
            dq_ref[:, cs] = _dot(ds[h], kv_ref[0, :, cs]).astype(BF16)
            dkv_ref[0, :, cs] += _dot_tn(ds[h], q_ref[:, cs])
            dkv_ref[1, :, cs] += _dot_tn(p[h].astype(BF16), do_ref[:, cs])

    qspec = pl.BlockSpec((tm, D), lambda b, i: (b * tps + i, 0))
    kvspec = pl.BlockSpec((2, n_mem, D), lambda b, i: (0, b, 0))
    return pl.pallas_call(
        body, name="attn_bwd", grid=(seqs, tps),
        in_specs=[qspec, kvspec, qspec],
        out_specs=[qspec, kvspec],
        out_shape=[jax.ShapeDtypeStruct((T, D), BF16), jax.ShapeDtypeStruct((2, seqs * n_mem, D), F32)],
        compiler_params=_cparams(("parallel", "arbitrary")),
    )(q, kv3, do)


def _my_place():
    return lax.axis_index("x"), lax.axis_index("y"), lax.axis_index("c")


def _slot_of(px, py, pc):
    return 4 * px + 2 * py + pc


def _peer(k, x, y, c):
    return (1 - x if (k >> 2) & 1 else x, 1 - y if (k >> 1) & 1 else y, 1 - c if k & 1 else c)


def _split_copies(src_refs, land_refs, send_sems, recv_sems, scatter):
    x, y, c = _my_place()
    mine = _slot_of(x, y, c)

    def slot(ref, s, like):
        if len(ref.shape) > len(like.shape):
            return ref.at[s]
        width = like.shape[-1]
        return ref.at[:, pl.ds(pl.multiple_of(s * width, width), width)]

    copies = []
    for a, (src, land) in enumerate(zip(src_refs, land_refs)):
        for k in range(1, N_DEV):
            peer = _peer(k, x, y, c)
            if scatter:
                src_k, dst = slot(src, _slot_of(*peer), land.at[0]), land.at[mine]
            else:
                src_k, dst = src, slot(land, mine, src)
            copies.append(pltpu.make_async_remote_copy(
                src_ref=src_k, dst_ref=dst,
                send_sem=send_sems.at[a * N_PEERS + k - 1], recv_sem=recv_sems.at[a * N_PEERS + k - 1],
                device_id=peer, device_id_type=MESH))
    return copies


def split_start(groups, *, name, scatter):
    sizes = [len(srcs) for srcs, _ in groups]
    n_arr = sum(sizes)
    flat = [a for srcs, lands in groups for a in list(srcs) + list(lands)]

    def body(*refs):
        ins = refs[:2 * n_arr]
        sems = refs[4 * n_arr:4 * n_arr + 2 * len(groups)]
        token = refs[-1]
        at = 0
        for gi, n in enumerate(sizes):
            for cp in _split_copies(ins[at:at + n], ins[at + n:at + 2 * n], sems[2 * gi], sems[2 * gi + 1], scatter):
                cp.start()
            at += 2 * n
        token[...] = jnp.zeros_like(token)

    sem_shapes = []
    for n in sizes:
        sem_shapes += [pltpu.SemaphoreType.DMA((n * N_PEERS,))] * 2
    outs = pl.pallas_call(
        body, name=name,
        out_shape=tuple(pltpu.HBM(a.shape, a.dtype) for a in flat) + tuple(sem_shapes)
        + (jax.ShapeDtypeStruct((8, 128), F32),),
        in_specs=(HBM,) * len(flat),
        out_specs=(HBM,) * len(flat) + (SEM,) * len(sem_shapes) + (pl.BlockSpec(memory_space=pltpu.VMEM),),
        input_output_aliases={i: i for i in range(len(flat))},
        compiler_params=pltpu.CompilerParams(has_side_effects=pltpu.SideEffectType.DATAFLOW_SIDE_EFFECTING),
    )(*[pltpu.with_memory_space_constraint(a, pltpu.HBM) for a in flat])
    thru, sems, token = outs[:len(flat)], outs[len(flat):-1], outs[-1]
    started, at = [], 0
    for gi, n in enumerate(sizes):
        started.append((sems[2 * gi], sems[2 * gi + 1], thru[at:at + n], thru[at + n:at + 2 * n]))
        at += 2 * n
    return started, token


def split_wait(started, after, *, name, scatter):
    sizes = [len(g[2]) for g in started]
    n_arr = sum(sizes)
    flat = [a for g in started for a in list(g[2]) + list(g[3])]
    sems = [s for g in started for s in g[:2]]

    def body(*refs):
        ins = refs[:2 * n_arr]
        sem_refs = refs[2 * n_arr:2 * n_arr + len(sems)]
        at = 0
        for gi, n in enumerate(sizes):
            for cp in _split_copies(ins[at:at + n], ins[at + n:at + 2 * n], sem_refs[2 * gi], sem_refs[2 * gi + 1], scatter):
                cp.wait_send()
                cp.wait_recv()
            at += 2 * n

    outs = pl.pallas_call(
        body, name=name,
        out_shape=tuple(pltpu.HBM(a.shape, a.dtype) for a in flat),
        in_specs=(HBM,) * len(flat) + (SEM,) * len(sems) + (pl.BlockSpec(memory_space=pl.ANY),),
        out_specs=(HBM,) * len(flat),
        input_output_aliases={i: i for i in range(len(flat))},
        compiler_params=pltpu.CompilerParams(has_side_effects=pltpu.SideEffectType.DATAFLOW_SIDE_EFFECTING),
    )(*flat, *sems, after)
    done, at = [], 0
    for n in sizes:
        done.append((outs[at:at + n], outs[at + n:at + 2 * n]))
        at += 2 * n
    return done


SIBLING = 1
CHIP_PEERS = (2, 4, 6)
_SIDE_EFFECTS = pltpu.CompilerParams(has_side_effects=pltpu.SideEffectType.DATAFLOW_SIDE_EFFECTING)


def _chip_level_copies(src, land, send_sems, recv_sems):
    x, y, c = _my_place()
    return [pltpu.make_async_remote_copy(
        src_ref=src, dst_ref=land.at[_slot_of(x, y, c)], send_sem=send_sems.at[j], recv_sem=recv_sems.at[j],
        device_id=_peer(k, x, y, c), device_id_type=MESH) for j, k in enumerate((SIBLING,) + CHIP_PEERS)]


def _pass_on_copies(land, send_sems, recv_sems, receiving):
    x, y, c = _my_place()
    copies = []
    for j, k in enumerate(CHIP_PEERS):
        slot = _slot_of(*_peer(k ^ SIBLING if receiving else k, x, y, c))
        copies.append(pltpu.make_async_remote_copy(
            src_ref=land.at[slot], dst_ref=land.at[slot], send_sem=send_sems.at[j], recv_sem=recv_sems.at[j],
            device_id=_peer(SIBLING, x, y, c), device_id_type=MESH))
    return copies


def gather2_start(src, land, *, name):
    def body(src_ref, land_ref, src_out, land_out, send_sems, recv_sems, token):
        for cp in _chip_level_copies(src_ref, land_ref, send_sems, recv_sems):
            cp.start()
        token[...] = jnp.zeros_like(token)

    n = 1 + len(CHIP_PEERS)
    src_t, land_t, send_sems, recv_sems, token = pl.pallas_call(
        body, name=name,
        out_shape=(pltpu.HBM(src.shape, src.dtype), pltpu.HBM(land.shape, land.dtype),
                   pltpu.SemaphoreType.DMA((n,)), pltpu.SemaphoreType.DMA((n,)), jax.ShapeDtypeStruct((8, 128), F32)),
        in_specs=(HBM, HBM), out_specs=(HBM, HBM, SEM, SEM, pl.BlockSpec(memory_space=pltpu.VMEM)),
        input_output_aliases={0: 0, 1: 1}, compiler_params=_SIDE_EFFECTS,
    )(pltpu.with_memory_space_constraint(src, pltpu.HBM), pltpu.with_memory_space_constraint(land, pltpu.HBM))
    return (src_t, land_t, send_sems, recv_sems), token


def gather2_pass_on(started, after, *, name):
    src, land, send_a, recv_a = started

    def body(src_ref, land_ref, send_a_ref, recv_a_ref, after_ref, land_out, send_b, recv_b):
        for cp in _chip_level_copies(src_ref, land_ref, send_a_ref, recv_a_ref):
            cp.wait_send()
            cp.wait_recv()
        for cp in _pass_on_copies(land_ref, send_b, recv_b, False):
            cp.start()

    n = len(CHIP_PEERS)
    land_t, send_b, recv_b = pl.pallas_call(
        body, name=name,
        out_shape=(pltpu.HBM(land.shape, land.dtype), pltpu.SemaphoreType.DMA((n,)), pltpu.SemaphoreType.DMA((n,))),
        in_specs=(HBM, HBM, SEM, SEM, pl.BlockSpec(memory_space=pl.ANY)), out_specs=(HBM, SEM, SEM),
        input_output_aliases={1: 0}, compiler_params=_SIDE_EFFECTS,
    )(src, land, send_a, recv_a, after)
    return land_t, send_b, recv_b


def gather2_wait(passed, *, name):
    land, send_b, recv_b = passed

    def body(land_ref, send_ref, recv_ref, land_out):
        for cp in _pass_on_copies(land_ref, send_ref, recv_ref, False):
            cp.wait_send()
        for cp in _pass_on_copies(land_ref, send_ref, recv_ref, True):
            cp.wait_recv()

    return pl.pallas_call(
        body, name=name, out_shape=pltpu.HBM(land.shape, land.dtype),
        in_specs=(HBM, SEM, SEM), out_specs=HBM,
        input_output_aliases={0: 0}, compiler_params=_SIDE_EFFECTS,
    )(land, send_b, recv_b)


def _adamw_math(g, w, m, v):
    c1 = 1.0 - ADAM_B1 ** ADAM_STEP
    c2 = 1.0 - ADAM_B2 ** ADAM_STEP
    nm = ADAM_B1 * m + (1.0 - ADAM_B1) * g
    nv = ADAM_B2 * v + (1.0 - ADAM_B2) * (g * g)
    delta = -ADAM_LR * ((nm / c1) / (jnp.sqrt(nv / c2) + ADAM_EPS) + ADAM_WD * w)
    return delta, nm, nv


def adamw_sharded(me, owns, recvs, ws, ms, vs, *, name, tr):
    n = len(ws)
    R, C = ws[0].shape

    def body(me_ref, *refs):
        for t in range(n):
            parts = refs[t * N_DEV:(t + 1) * N_DEV]
            w_ref, m_ref, v_ref = refs[n * N_DEV + 3 * t:n * N_DEV + 3 * t + 3]
            g_ref, d_ref, nm_ref, nv_ref = refs[n * (N_DEV + 3) + 4 * t:n * (N_DEV + 3) + 4 * t + 4]
            g = parts[0][...].astype(F32)
            for p in parts[1:]:
                g = g + p[...].astype(F32)
            g_ref[...] = g
            d_ref[...], nm_ref[...], nv_ref[...] = _adamw_math(g, w_ref[...], m_ref[...], v_ref[...])

    def slab(k):
        return pl.BlockSpec((None, tr, C), lambda i, me_ref: (me_ref[0] ^ k, i, 0))

    def own_spec(own):
        return slab(0) if own.ndim == 3 else pl.BlockSpec((tr, C), lambda i, me_ref: (i, me_ref[0]))

    blk = pl.BlockSpec((tr, C), lambda i, me_ref: (i, 0))
    out = jax.ShapeDtypeStruct((R, C), F32)
    args = [me]
    for t in range(n):
        args += [owns[t]] + [recvs[t]] * N_PEERS
    for t in range(n):
        args += [ws[t], ms[t], vs[t]]
    outs = pl.pallas_call(
        body, name=name,
        grid_spec=pltpu.PrefetchScalarGridSpec(
            num_scalar_prefetch=1, grid=(R // tr,),
            in_specs=[spec for t in range(n) for spec in [own_spec(owns[t])] + [slab(k) for k in range(1, N_DEV)]]
            + [blk] * (3 * n),
            out_specs=[blk] * (4 * n)),
        out_shape=[out] * (4 * n),
        compiler_params=_cparams(("parallel",)),
    )(*args)
    return [outs[4 * t:4 * t + 4] for t in range(n)]


def adamw_replicated(parts, ws, ms, vs, rows):
    n_buf, n_par = len(parts), len(ws)

    def body(*refs):
        p_refs = refs[:n_buf]
        w_refs = refs[n_buf:n_buf + n_par]
        m_refs = refs[n_buf + n_par:n_buf + 2 * n_par]
        v_refs = refs[n_buf + 2 * n_par:n_buf + 3 * n_par]
        outs = refs[n_buf + 3 * n_par:]
        sums = []
        for p in p_refs:
            g = p[0]
            for s in range(1, N_DEV):
                g = g + p[s]
            sums.append(g)
        for j, (b, r0, nr) in enumerate(rows):
            g = sums[b][r0:r0 + nr]
            delta, nm, nv = _adamw_math(g, w_refs[j][...], m_refs[j][...], v_refs[j][...])
            outs[j][...] = g
            outs[n_par + j][...] = delta
            outs[2 * n_par + j][...] = nm
            outs[3 * n_par + j][...] = nv

    shapes = [jax.ShapeDtypeStruct(w.shape, F32) for w in ws]
    outs = pl.pallas_call(
        body, name="adamw_replicated", out_shape=shapes * 4,
        compiler_params=pltpu.CompilerParams(vmem_limit_bytes=V7X_VMEM_LIMIT),
    )(*parts, *ws, *ms, *vs)
    return outs[:n_par], outs[n_par:2 * n_par], outs[2 * n_par:3 * n_par], outs[3 * n_par:]


BIG = ("w_in", "w_out", "xw_q", "xw_kv", "xw_o", "w_up", "w_down")
ADAMW_CALLS = (("w_in",), ("w_out", "xw_q", "xw_o"), ("xw_kv",), ("w_up",), ("w_down",))
WEIGHTS = ("norm_mix", "w_in", "pool_w", "pool_scale", "lb_theta", "hgrn_norm", "w_out", "norm_xq",
           "norm_mem", "xw_q", "xw_kv", "xw_o", "norm_mlp", "w_up", "w_down", "norm_final")
SMALL = (("pool_w", (4 * HEAD_W, HEAD_W), 0, 0),
         ("norm_mix", (1, 1024), 1, 0), ("norm_xq", (1, 1024), 1, 1), ("norm_mem", (1, 1024), 1, 2),
         ("norm_mlp", (1, 1024), 1, 3), ("norm_final", (1, 1024), 1, 4),
         ("pool_scale", (1, 512), 2, 0), ("hgrn_norm", (1, 512), 2, 1), ("lb_theta", (2, 512), 2, 2))


def _pad_rows(a, rows):
    return jnp.concatenate([a, jnp.zeros((rows - a.shape[0], a.shape[1]), a.dtype)], axis=0)


def kernel(x, mem, norm_mix, w_in, pool_w, pool_scale, lb_theta, hgrn_norm, w_out, norm_xq, norm_mem, xw_q, xw_kv, xw_o, norm_mlp, w_up, w_down, norm_final, loss_target, m_norm_mix, m_w_in, m_pool_w, m_pool_scale, m_lb_theta, m_hgrn_norm, m_w_out, m_norm_xq, m_norm_mem, m_xw_q, m_xw_kv, m_xw_o, m_norm_mlp, m_w_up, m_w_down, m_norm_final, v_norm_mix, v_w_in, v_pool_w, v_pool_scale, v_lb_theta, v_hgrn_norm, v_w_out, v_norm_xq, v_norm_mem, v_xw_q, v_xw_kv, v_xw_o, v_norm_mlp, v_w_up, v_w_down, v_norm_final):
    w = dict(norm_mix=norm_mix, w_in=w_in, pool_w=pool_w, pool_scale=pool_scale, lb_theta=lb_theta,
             hgrn_norm=hgrn_norm, w_out=w_out, norm_xq=norm_xq, norm_mem=norm_mem, xw_q=xw_q, xw_kv=xw_kv,
             xw_o=xw_o, norm_mlp=norm_mlp, w_up=w_up, w_down=w_down, norm_final=norm_final)
    mom = dict(norm_mix=m_norm_mix, w_in=m_w_in, pool_w=m_pool_w, pool_scale=m_pool_scale, lb_theta=m_lb_theta,
               hgrn_norm=m_hgrn_norm, w_out=m_w_out, norm_xq=m_norm_xq, norm_mem=m_norm_mem, xw_q=m_xw_q,
               xw_kv=m_xw_kv, xw_o=m_xw_o, norm_mlp=m_norm_mlp, w_up=m_w_up, w_down=m_w_down,
               norm_final=m_norm_final)
    var = dict(norm_mix=v_norm_mix, w_in=v_w_in, pool_w=v_pool_w, pool_scale=v_pool_scale, lb_theta=v_lb_theta,
               hgrn_norm=v_hgrn_norm, w_out=v_w_out, norm_xq=v_norm_xq, norm_mem=v_norm_mem, xw_q=v_xw_q,
               xw_kv=v_xw_kv, xw_o=v_xw_o, norm_mlp=v_norm_mlp, w_up=v_w_up, w_down=v_w_down,
               norm_final=v_norm_final)

    seqs, seq_len, D = x.shape
    n_mem = mem.shape[1]
    T = seqs * seq_len
    W = HEAD_W
    x2 = x.reshape(T, D)
    mem2 = mem.reshape(seqs * n_mem, D)
    tgt2 = loss_target.reshape(T, D)
    tm_big = min(1024, T)
    tm_mid = min(512, T)
    tm_sq = min(1024, T)
    tm_mix = min(256, seq_len)
    tm_att = min(1024, seq_len)
    tkv = min(512, seqs * n_mem)
    px, py, pc = _my_place()
    me = _slot_of(px, py, pc).astype(jnp.int32)
    me1 = me.reshape(1)

    shard_bf = {n: w[n][0].astype(BF16) for n in BIG}

    def landing(n):
        rows, cols = shard_bf[n].shape
        if n in ("xw_kv", "w_up"):
            return lax.dynamic_update_slice(lax.empty((rows, N_DEV * cols), BF16), shard_bf[n], (0, me * cols))
        return lax.dynamic_update_slice(lax.empty((N_DEV, rows, cols), BF16), shard_bf[n][None], (me, 0, 0))

    w_in_started, tok = gather2_start(shard_bf["w_in"], landing("w_in"), name="w_in_gather_start")
    shard_bf["w_out"] = shard_bf["w_out"] + tok[0, 0].astype(BF16)
    ag_groups = (("w_out", "xw_q", "xw_kv", "xw_o"), ("w_up",), ("w_down",))
    ag_started, tok = split_start([([shard_bf[n] for n in grp], [landing(n) for n in grp]) for grp in ag_groups],
                                name="weights_gather_start", scatter=False)

    pool_w_bf = pool_w[0].astype(BF16)
    scale4 = pool_scale.reshape(4, 1, W)
    gn4 = hgrn_norm.reshape(4, 1, W)
    theta4 = lb_theta.reshape(2, 4, W).transpose(1, 0, 2)
    g_final = norm_final.reshape(1, D)

    n1 = prenorm(x2, norm_mix, tok, tm=tm_sq)
    wi3 = gather2_wait(gather2_pass_on(w_in_started, n1, name="w_in_gather_pass_on"), name="w_in_gather_wait")
    full_w_in = wi3.transpose(1, 0, 2).reshape(D, -1)
    u5 = proj_plain(n1, full_w_in, name="in_proj", tm=tm_sq, tn=4 * W, out_dtype=F32, out_slabs=5)
    tri_bf, tri_f = chunk_triangles(tm_mix)
    y2, o_pre, st_prev = mixer_fwd(u5, pool_w_bf, scale4, theta4, gn4, tri_bf, tri_f, seqs=seqs, seq_len=seq_len,
                                   tm=tm_mix)
    (_, (wo3, wq3, wkv3, wao3)), = split_wait(ag_started[0:1], y2, name="weights_gather_wait_attn", scatter=False)
    full_w_out, full_xw_q, full_xw_o = wo3.reshape(D, D), wq3.reshape(D, D), wao3.reshape(D, D)
    tn = 4 * W
    h1, n2, q = proj_res_norm(y2, full_w_out, x2, norm_xq, full_xw_q, name="out_q_proj", tm=tm_sq, tn=tn)
    tn_kv, tn_up = xw_kv.shape[2], w_up.shape[2]
    kv3, memn = proj_norm(mem2, norm_mem, wkv3, name="kv_proj", tm=tkv, tn=tn_kv, out_dtype=BF16, out_slabs=2)
    o_att = attn_fwd(q, kv3, seqs=seqs, seq_len=seq_len, n_mem=n_mem, tm=tm_att)
    h2, n3 = proj_res_norm(o_att, full_xw_o, h1, norm_mlp, name="attn_out_proj", tm=tm_sq, tn=tn)
    (_, (wup3,)), = split_wait(ag_started[1:2], h2, name="weights_gather_wait_up", scatter=False)
    aa, da = proj_plain(n3, wup3, name="up_proj", tm=tm_mid, tn=tn_up, relu2=True)
    (_, (wdn3,)), = split_wait(ag_started[2:3], aa, name="weights_gather_wait_down", scatter=False)
    full_w_down = wdn3.reshape(-1, D)
    dh3, dh3b, sq_err, dg_final = proj_res_loss(aa, full_w_down, h2, g_final, tgt2, name="down_proj_loss",
                                                tm=tm_mid, tn=tn)

    def send(names, parts, name):
        srcs = [p if n in ("xw_kv", "w_up") or p.ndim == 3 else p.reshape((N_DEV, -1, p.shape[-1]))
                for n, p in zip(names, parts)]
        lands = [lax.empty((N_DEV,) + w[n].shape[1:], BF16) for n in names]
        started, token = split_start([(srcs, lands)], name=name, scatter=True)
        return started[0], token

    gw_down = wgrad(aa, dh3b, name="down_proj_wgrad", tt=tm_mid, tn=tn)
    dap = back_plain(dh3b, full_w_down, name="down_proj_bwd", tm=tm_mid, tn=tn, out_dtype=BF16, factor=da)
    gw_up = wgrad(n3, dap, name="up_proj_wgrad", tt=tm_mid, tn=tn_up)
    sent_mlp, tok = send(["w_down", "w_up"], [gw_down, gw_up], "grads_send_mlp")
    dh2, dh2b, do_att, dg_mlp = back_norm(dap, wup3, h2, norm_mlp, dh3, name="up_proj_bwd", tm=tm_mid, tk=tn_up,
                                          w_next=full_xw_o, after=tok)
    gxw_o = wgrad(o_att, dh2b, name="attn_out_proj_wgrad", tt=tm_sq, tn=tn)
    dq, dkv3 = attn_bwd(q, kv3, do_att, seqs=seqs, seq_len=seq_len, n_mem=n_mem, tm=tm_att)
    gxw_q = wgrad(n2, dq, name="q_proj_wgrad", tt=tm_sq, tn=tn)
    gxw_kv = wgrad(memn, dkv3, name="kv_proj_wgrad", tt=tkv, tn=tn_kv)
    dg_mem = back_norm(dkv3, wkv3, mem2, norm_mem, None, name="kv_proj_bwd", tm=tkv, tk=tn_kv)
    dh1, dh1b, dy2, dg_xq = back_norm(dq, full_xw_q, h1, norm_xq, dh2, name="q_proj_bwd", tm=tm_sq, tk=D,
                                      w_next=full_w_out, next_slabs=2)
    gw_out = wgrad(y2, dh1b, name="out_proj_wgrad", tt=tm_sq, tn=tn)
    sent_attn, tok = send(["xw_o", "xw_q", "xw_kv", "w_out"], [gxw_o, gxw_q, gxw_kv, gw_out], "grads_send_attn")
    du5, dpw, dsc, dlb, dgn = mixer_bwd(u5, dy2, o_pre, st_prev, pool_w_bf, scale4, theta4, gn4, tri_bf, tri_f, tok,
                                        seqs=seqs, seq_len=seq_len, tm=tm_mix)
    gw_in = wgrad(n1, du5, name="in_proj_wgrad", tt=tm_sq, tn=tn)
    gw_in_slots = gw_in.reshape(D, N_DEV, -1).transpose(1, 0, 2)
    sent_in, tok = send(["w_in"], [gw_in_slots], "grads_send_in")
    dx, dg_mix = back_norm(du5, full_w_in, x2, norm_mix, dh1, name="in_proj_bwd", tm=tm_sq, tk=tn, bf16_copy=False,
                           after=tok)

    dlb_row = dlb.reshape(1, 4 * W)
    buf_vec = _pad_rows(jnp.concatenate([dg_mix, dg_xq, dg_mem, dg_mlp, dg_final, sq_err], axis=0), 8)
    buf_half = _pad_rows(jnp.concatenate([dsc.reshape(1, 4 * W), dgn.reshape(1, 4 * W), dlb_row, -dlb_row], axis=0), 8)
    small_src = [dpw.reshape(4 * W, W), buf_vec, buf_half]
    small_land = [lax.dynamic_update_slice(lax.empty((N_DEV,) + b.shape, F32), b[None], (me, 0, 0))
                  for b in small_src]
    small_started, tok = split_start([(small_src, small_land)], name="small_grads_start", scatter=False)

    done = split_wait([sent_mlp, sent_attn, sent_in], tok, name="grads_wait", scatter=True)
    slots = dict(w_down=(0, 0), w_up=(0, 1), xw_o=(1, 0), xw_q=(1, 1), xw_kv=(1, 2), w_out=(1, 3), w_in=(2, 0))
    own = {n: done[gi][0][ai] for n, (gi, ai) in slots.items()}
    got = {n: done[gi][1][ai] for n, (gi, ai) in slots.items()}
    res = {}
    for names in ADAMW_CALLS:
        shp = w[names[0]].shape
        r = adamw_sharded(me1, [own[n] for n in names], [got[n] for n in names], [w[n][0] for n in names],
                          [mom[n][0] for n in names], [var[n][0] for n in names], name="adamw_" + names[0],
                          tr=min(256, shp[1]))
        for n, outs in zip(names, r):
            for kind, a in zip("gdmv", outs):
                res[kind, n] = a.reshape(shp)
    (_, small_parts), = split_wait(small_started, res["g", BIG[-1]], name="small_grads_wait", scatter=False)
    loss = 0.5 * jnp.sum(small_parts[1][:, 5, :]) / D
    r = adamw_replicated(small_parts, [w[n].reshape(v2) for n, v2, _, _ in SMALL],
                         [mom[n].reshape(v2) for n, v2, _, _ in SMALL],
                         [var[n].reshape(v2) for n, v2, _, _ in SMALL],
                         [(b, r0, v2[0]) for _, v2, b, r0 in SMALL])
    for kind, arrs in zip("gdmv", r):
        for (n, _, _, _), a in zip(SMALL, arrs):
            res[kind, n] = a.reshape(w[n].shape)

    out = [loss, dx.reshape(x.shape)]
    for kind in "gdmv":
        out += [res[kind, n] for n in WEIGHTS]
    return tuple(out)
```

```python
import jax
import jax.numpy as jnp
from jax import lax
from jax.experimental import pallas as pl
from jax.experimental.pallas import tpu as pltpu

F32 = jnp.float32
BF16 = jnp.bfloat16
EPS = 1e-6
CHUNK = 64
POOL_HALO = 16
HEAD_W = 128
HEADS_PER_STEP = 4
XATTN_HEADS = 4
N_DEV = 8
N_PEERS = N_DEV - 1
ADAM_LR = 0.001
ADAM_B1 = 0.9
ADAM_B2 = 0.999
ADAM_EPS = 1e-08
ADAM_WD = 0.01
ADAM_STEP = 10
V7X_VMEM_LIMIT = 52 * 1024 * 1024
MESH = pl.DeviceIdType.MESH
HBM = pl.BlockSpec(memory_space=pltpu.HBM)
SEM = pl.BlockSpec(memory_space=pltpu.SEMAPHORE)


def _cparams(dims):
    return pltpu.CompilerParams(dimension_semantics=dims, vmem_limit_bytes=V7X_VMEM_LIMIT)


def _sigmoid(v):
    return 0.5 * jnp.tanh(0.5 * v) + 0.5


def _dot(a, b):
    return jnp.dot(a, b, preferred_element_type=F32)


def _dot_nt(a, b):
    return lax.dot_general(a, b, (((1,), (1,)), ((), ())), preferred_element_type=F32)


def _dot_tn(a, b):
    return lax.dot_general(a, b, (((0,), (0,)), ((), ())), preferred_element_type=F32)


def _tri_apply(tri, v):
    hi = v.astype(BF16)
    lo = (v - hi.astype(F32)).astype(BF16)
    return _dot(tri, hi) + _dot(tri, lo)


def _mat_shape(a):
    return a.shape if a.ndim == 2 else (a.shape[1], a.shape[0] * a.shape[2])


def _out_struct(rows, n, slabs, dtype):
    return jax.ShapeDtypeStruct((rows, n) if slabs is None else (slabs, rows, n // slabs), dtype)


def _resident(a):
    nd = a.ndim
    return pl.BlockSpec(a.shape, lambda i: (0,) * nd, pipeline_mode=pl.Buffered(1))


def _row_block(a, tm):
    if a.ndim == 2:
        return pl.BlockSpec((tm, a.shape[1]), lambda i: (i, 0))
    return pl.BlockSpec((a.shape[0], tm, a.shape[2]), lambda i: (0, i, 0))


def _cols(ref, c, width):
    if len(ref.shape) == 2:
        return ref[:, c * width:(c + 1) * width]
    per = ref.shape[2] // width
    if per == 1:
        return ref[c]
    return ref[c // per, :, (c % per) * width:(c % per + 1) * width]


def _set_cols(ref, c, width, val):
    if len(ref.shape) == 2:
        ref[:, c * width:(c + 1) * width] = val
        return
    per = ref.shape[2] // width
    if per == 1:
        ref[c] = val
    else:
        ref[c // per, :, (c % per) * width:(c % per + 1) * width] = val


def _all_cols(ref):
    if len(ref.shape) == 2:
        return ref[...]
    return jnp.concatenate([ref[s] for s in range(ref.shape[0])], axis=1)


def _rms(x):
    return lax.rsqrt(jnp.mean(x * x, axis=-1, keepdims=True) + EPS)


def _row_params():
    return _cparams(("arbitrary",))


def proj_norm(h, g, w, *, name, tm, tn, out_dtype, out_slabs=None):
    T, D = h.shape
    N = _mat_shape(w)[1]
    o_shape = _out_struct(T, N, out_slabs, out_dtype)

    def body(h_ref, g_ref, w_ref, o_ref, n_ref):
        x = h_ref[...]
        n = (x * _rms(x) * g_ref[...]).astype(BF16)
        n_ref[...] = n
        for c in range(N // tn):
            _set_cols(o_ref, c, tn, _dot(n, _cols(w_ref, c, tn)).astype(out_dtype))

    return pl.pallas_call(
        body, name=name, grid=(T // tm,),
        in_specs=[_row_block(h, tm), pl.BlockSpec((1, D), lambda i: (0, 0)), _resident(w)],
        out_specs=[_row_block(o_shape, tm), pl.BlockSpec((tm, D), lambda i: (i, 0))],
        out_shape=[o_shape, jax.ShapeDtypeStruct((T, D), BF16)],
        compiler_params=_row_params(),
    )(h, g, w)


def prenorm(h, g, after, *, tm):
    T, D = h.shape

    def body(h_ref, g_ref, _after_ref, n_ref):
        x = h_ref[...]
        n_ref[...] = (x * _rms(x) * g_ref[...]).astype(BF16)

    row = pl.BlockSpec((tm, D), lambda i: (i, 0))
    return pl.pallas_call(
        body, name="prenorm", grid=(T // tm,),
        in_specs=[row, pl.BlockSpec((1, D), lambda i: (0, 0)), _anchor_spec(after)],
        out_specs=row, out_shape=jax.ShapeDtypeStruct((T, D), BF16),
        compiler_params=_row_params(),
    )(h, g, after)


def proj_plain(a, w, *, name, tm, tn, out_dtype=BF16, out_slabs=None, relu2=False):
    T = a.shape[0]
    N = _mat_shape(w)[1]

    def body(a_ref, w_ref, o_ref, *d_ref):
        av = a_ref[...]
        for c in range(N // tn):
            z = _dot(av, _cols(w_ref, c, tn))
            if relu2:
                z = jnp.maximum(z, 0.0)
                _set_cols(d_ref[0], c, tn, (z + z).astype(out_dtype))
                z = z * z
            _set_cols(o_ref, c, tn, z.astype(out_dtype))

    o_shape = _out_struct(T, N, out_slabs, out_dtype)
    n_out = 2 if relu2 else 1
    outs = pl.pallas_call(
        body, name=name, grid=(T // tm,),
        in_specs=[_row_block(a, tm), _resident(w)],
        out_specs=[_row_block(o_shape, tm)] * n_out, out_shape=[o_shape] * n_out,
        compiler_params=_row_params(),
    )(a, w)
    return outs if relu2 else outs[0]


def proj_res_norm(a, w, res, g, w_next=None, *, name, tm, tn):
    T = res.shape[0]
    D = w.shape[1]
    chained = w_next is not None

    def body(*refs):
        a_ref, w_ref, r_ref, g_ref = refs[:4]
        h_ref, n_ref = refs[4 + chained], refs[5 + chained]
        av = _all_cols(a_ref)
        for c in range(D // tn):
            sl = slice(c * tn, (c + 1) * tn)
            h_ref[:, sl] = r_ref[:, sl] + _dot(av, w_ref[:, sl])
        hv = h_ref[...]
        n = (hv * _rms(hv) * g_ref[...]).astype(BF16)
        n_ref[...] = n
        if chained:
            for c in range(D // tn):
                sl = slice(c * tn, (c + 1) * tn)
                refs[-1][:, sl] = _dot(n, refs[4][:, sl]).astype(BF16)

    row = pl.BlockSpec((tm, D), lambda i: (i, 0))
    half = jax.ShapeDtypeStruct((T, D), BF16)
    return pl.pallas_call(
        body, name=name, grid=(T // tm,),
        in_specs=[_row_block(a, tm), _resident(w), row, pl.BlockSpec((1, D), lambda i: (0, 0))]
        + ([_resident(w_next)] if chained else []),
        out_specs=[row, row] + ([row] if chained else []),
        out_shape=[jax.ShapeDtypeStruct((T, D), F32), half] + ([half] if chained else []),
        compiler_params=_row_params(),
    )(*([a, w, res, g] + ([w_next] if chained else [])))


def proj_res_loss(a, w, res, g, target, *, name, tm, tn):
    T = res.shape[0]
    D = w.shape[1]

    def body(a_ref, w_ref, r_ref, g_ref, t_ref, dh_ref, dhb_ref, ls_ref, dg_ref):
        i = pl.program_id(0)
        gv = g_ref[...]
        ls, dg = 0.0, 0.0
        halves = [slice(s * (tm // 2), (s + 1) * (tm // 2)) for s in range(2)]
        for rows in halves:
            av = a_ref[rows, :]
            for c in range(D // tn):
                sl = slice(c * tn, (c + 1) * tn)
                dh_ref[rows, sl] = r_ref[rows, sl] + _dot(av, w_ref[:, sl])
        for rows in halves:
            x = dh_ref[rows, :]
            r = _rms(x)
            xr = x * r
            d = xr * gv - t_ref[rows, :]
            dy = d * (1.0 / D)
            dyg = dy * gv
            dx = r * (dyg - xr * jnp.mean(dyg * xr, axis=-1, keepdims=True))
            dh_ref[rows, :] = dx
            dhb_ref[rows, :] = dx.astype(BF16)
            ls = ls + jnp.sum(d * d, axis=0, keepdims=True)
            dg = dg + jnp.sum(dy * xr, axis=0, keepdims=True)

        @pl.when(i == 0)
        def _():
            ls_ref[...] = ls
            dg_ref[...] = dg

        @pl.when(i > 0)
        def _():
            ls_ref[...] += ls
            dg_ref[...] += dg

    row = pl.BlockSpec((tm, D), lambda i: (i, 0))
    vec = pl.BlockSpec((1, D), lambda i: (0, 0))
    return pl.pallas_call(
        body, name=name, grid=(T // tm,),
        in_specs=[_row_block(a, tm), _resident(w), row, vec, row],
        out_specs=[row, row, vec, vec],
        out_shape=[jax.ShapeDtypeStruct((T, D), F32), jax.ShapeDtypeStruct((T, D), BF16),
                   jax.ShapeDtypeStruct((1, D), F32), jax.ShapeDtypeStruct((1, D), F32)],
        compiler_params=_row_params(),
    )(a, w, res, g, target)


def _anchor_spec(after):
    return pl.BlockSpec(after.shape, lambda i: (0, 0))


def back_plain(a, w, *, name, tm, tn, out_dtype, out_slabs=None, factor=None, after=None):
    T = a.shape[0]
    N = w.shape[0]
    has_z = factor is not None
    o_shape = _out_struct(T, N, out_slabs, out_dtype)

    def body(*refs):
        a_ref, w_ref = refs[0], refs[1]
        o_ref = refs[-1]
        av = a_ref[...]
        for c in range(N // tn):
            out = _dot_nt(av, w_ref[c * tn:(c + 1) * tn, :])
            if has_z:
                out = out * refs[2][:, c * tn:(c + 1) * tn].astype(F32)
            _set_cols(o_ref, c, tn, out.astype(out_dtype))

    in_specs, args = [_row_block(a, tm), _resident(w)], [a, w]
    if has_z:
        in_specs.append(_row_block(factor, tm))
        args.append(factor)
    if after is not None:
        in_specs.append(_anchor_spec(after))
        args.append(after)
    return pl.pallas_call(
        body, name=name, grid=(T // tm,),
        in_specs=in_specs, out_specs=_row_block(o_shape, tm), out_shape=o_shape,
        compiler_params=_row_params(),
    )(*args)


def back_norm(a, w, h, g, dres, *, name, tm, tk, bf16_copy=True, w_next=None, next_dtype=BF16, next_slabs=None,
              after=None):
    T, K = _mat_shape(a)
    D = h.shape[1]
    with_dh = dres is not None
    chained = w_next is not None
    n_in = 4 + with_dh + chained
    tn = 4 * HEAD_W

    def body(*refs):
        a_ref, w_ref, h_ref, g_ref = refs[:4]
        outs = refs[n_in + (after is not None):]
        i = pl.program_id(0)
        if len(w_ref.shape) == 2:
            dn = _dot_nt(_all_cols(a_ref).astype(BF16), w_ref[...])
        else:
            dn = None
            for kc in range(K // tk):
                part = _dot_nt(_cols(a_ref, kc, tk).astype(BF16), _cols(w_ref, kc, tk))
                dn = part if dn is None else dn + part
        x = h_ref[...]
        r = _rms(x)
        xr = x * r
        dgp = jnp.sum(dn * xr, axis=0, keepdims=True)
        dg_ref = outs[-1]

        @pl.when(i == 0)
        def _():
            dg_ref[...] = dgp

        @pl.when(i > 0)
        def _():
            dg_ref[...] += dgp

        if with_dh:
            dyg = dn * g_ref[...]
            out = refs[4][...] + r * (dyg - xr * jnp.mean(dyg * xr, axis=-1, keepdims=True))
            outs[0][...] = out
            outb = out.astype(BF16)
            if bf16_copy:
                outs[1][...] = outb
            if chained:
                wn_ref, nx_ref = refs[5], outs[-2]
                for c in range(wn_ref.shape[0] // tn):
                    _set_cols(nx_ref, c, tn, _dot_nt(outb, wn_ref[c * tn:(c + 1) * tn, :]).astype(next_dtype))

    row = pl.BlockSpec((tm, D), lambda i: (i, 0))
    vec = pl.BlockSpec((1, D), lambda i: (0, 0))
    in_specs, args = [_row_block(a, tm), _resident(w), row, vec], [a, w, h, g]
    out_specs, out_shape = [], []
    if with_dh:
        in_specs.append(row)
        args.append(dres)
        out_specs.append(row)
        out_shape.append(jax.ShapeDtypeStruct((T, D), F32))
        if bf16_copy:
            out_specs.append(row)
            out_shape.append(jax.ShapeDtypeStruct((T, D), BF16))
    if chained:
        in_specs.append(_resident(w_next))
        args.append(w_next)
        nx_shape = _out_struct(T, w_next.shape[0], next_slabs, next_dtype)
        out_specs.append(_row_block(nx_shape, tm))
        out_shape.append(nx_shape)
    out_specs.append(vec)
    out_shape.append(jax.ShapeDtypeStruct((1, D), F32))
    if after is not None:
        in_specs.append(_anchor_spec(after))
        args.append(after)
    outs = pl.pallas_call(
        body, name=name, grid=(T // tm,),
        in_specs=in_specs, out_specs=out_specs, out_shape=out_shape,
        compiler_params=_row_params(),
    )(*args)
    return outs if len(outs) > 1 else outs[0]


def wgrad(a, b, *, name, tt, tn, out_slabs=None):
    T, K = _mat_shape(a)
    N = _mat_shape(b)[1]
    nt = T // tt
    o_shape = _out_struct(K, N, out_slabs, BF16)

    flipped = K > N and out_slabs is None

    def body(a_ref, b_ref, o_ref, acc_ref):
        t = pl.program_id(0)

        @pl.when(t == 0)
        def _():
            acc_ref[...] = jnp.zeros_like(acc_ref)

        if flipped:
            bt = _all_cols(b_ref).astype(BF16).T
            for c in range(K // tn):
                acc_ref[:, c * tn:(c + 1) * tn] += _dot(bt, _cols(a_ref, c, tn).astype(BF16))
        else:
            at = _all_cols(a_ref).astype(BF16).T
            for c in range(N // tn):
                acc_ref[:, c * tn:(c + 1) * tn] += _dot(at, _cols(b_ref, c, tn).astype(BF16))

        @pl.when(t == nt - 1)
        def _():
            if flipped:
                for c in range(K // tn):
                    o_ref[c * tn:(c + 1) * tn, :] = acc_ref[:, c * tn:(c + 1) * tn].T.astype(BF16)
            else:
                for c in range(N // tn):
                    _set_cols(o_ref, c, tn, acc_ref[:, c * tn:(c + 1) * tn].astype(BF16))

    return pl.pallas_call(
        body, name=name, grid=(nt,),
        in_specs=[_row_block(a, tt), _row_block(b, tt)],
        out_specs=_resident(o_shape), out_shape=o_shape,
        scratch_shapes=[pltpu.VMEM((N, K) if flipped else (K, N), F32)],
        compiler_params=_row_params(),
    )(a, b)


def chunk_triangles(tm):
    r = lax.broadcasted_iota(jnp.int32, (tm, tm), 0)
    c = lax.broadcasted_iota(jnp.int32, (tm, tm), 1)
    same = (r // CHUNK) == (c // CHUNK)
    tri = jnp.stack([same & (c <= r), same & (c >= r)]).astype(F32)
    return tri.astype(BF16), tri


def _tri_spec(tm):
    return pl.BlockSpec((2, tm, tm), lambda g, s, i: (0, 0, 0))


def _chunk_row(v, r, nc):
    return jnp.concatenate([jnp.broadcast_to(v[c * CHUNK + r:c * CHUNK + r + 1], (CHUNK, v.shape[1]))
                            for c in range(nc)], axis=0)


def _block_diag(v, nc):
    chunk = lax.broadcasted_iota(jnp.int32, (v.shape[0], 1), 0) // CHUNK
    return jnp.concatenate([jnp.where(chunk == c, v, jnp.zeros_like(v)) for c in range(nc)], axis=1)


def _pool_windows_back(ext_ref, tm):
    n = tm + 32
    ext_ref[1, 8:n] = ext_ref[0, 8:n] + ext_ref[0, 7:n - 1]
    ext_ref[2, 16:n] = ext_ref[1, 16:n] + ext_ref[1, 14:n - 2]
    ext_ref[3, 24:n] = ext_ref[2, 24:n] + ext_ref[2, 20:n - 4]
    s2 = ext_ref[1, 32:n]
    s4 = ext_ref[2, 32:n]
    s8 = ext_ref[3, 32:n]
    s16 = s8 + ext_ref[3, 24:n - 8]
    return s2, s4, s8, s16


def _pool_windows_fwd(ext_ref, tm):
    n = tm + 32
    ext_ref[1, 0:n - 8] = ext_ref[0, 0:n - 8] + ext_ref[0, 1:n - 7]
    ext_ref[2, 0:n - 16] = ext_ref[1, 0:n - 16] + ext_ref[1, 2:n - 14]
    ext_ref[3, 0:n - 24] = ext_ref[2, 0:n - 24] + ext_ref[2, 4:n - 20]
    s2 = ext_ref[1, 0:tm]
    s4 = ext_ref[2, 0:tm]
    s8 = ext_ref[3, 0:tm]
    s16 = s8 + ext_ref[3, 8:tm + 8]
    return s2, s4, s8, s16


def _select_window(g, s2, s4, s8, s16):
    return jnp.where(g == 0, s2, jnp.where(g == 1, s4, jnp.where(g == 2, s8, s16)))


def _pool_count(g, pos):
    width = lax.shift_left(jnp.int32(2), g)
    return jnp.minimum(pos + 1, width).astype(F32)


def mixer_fwd(u5, pool_w_bf, scale4, theta4, gn4, tri_bf, tri_f, *, seqs, seq_len, tm):
    T = u5.shape[1]
    tps = seq_len // tm
    nc = tm // CHUNK
    W = HEAD_W

    H = HEADS_PER_STEP
    heads = range(H)

    def body(u_ref, pw_ref, sc_ref, th_ref, gn_ref, tri_ref, msk_ref, y_ref, o_ref, st_ref, halo_ref, ext_ref, s_ref):
        g = pl.program_id(0)
        i = pl.program_id(2)

        @pl.when(i == 0)
        def _():
            halo_ref[...] = jnp.zeros_like(halo_ref)
            s_ref[...] = jnp.zeros_like(s_ref)

        row = lax.broadcasted_iota(jnp.int32, (tm, 1), 0)
        cols = [slice(h * W, (h + 1) * W) for h in heads]

        pooled = []
        for h in heads:
            grp = g * H + h
            up = u_ref[0, :, cols[h]]
            ext_ref[h, 0, 0:16] = jnp.zeros((16, W), F32)
            ext_ref[h, 0, 16:32] = halo_ref[h]
            ext_ref[h, 0, 32:32 + tm] = up
            win = _select_window(grp, *_pool_windows_back(ext_ref.at[h], tm))
            pooled.append((win * (1.0 / _pool_count(grp, i * tm + row)) - up).astype(BF16))
            halo_ref[h] = up[tm - POOL_HALO:tm]
        mixed = [_dot(pooled[h], pw_ref[h]) for h in heads]
        for h in heads:
            y_ref[0, :, cols[h]] = (mixed[h] * sc_ref[h]).astype(BF16)

        zq, zf, zi, zg = u_ref[1], u_ref[2], u_ref[3], u_ref[4]
        th = [th_ref[h] for h in heads]
        lb = jnp.concatenate([_sigmoid(t[0:1, :] - t[1:2, :]) for t in th], axis=1)
        f = lb + (1.0 - lb) * _sigmoid(zf)
        kk = 1.0 - f
        q = zq * _sigmoid(zq)
        G = _tri_apply(tri_ref[0], jnp.log(f))
        Gm, Gl = _chunk_row(G, CHUNK // 2 - 1, nc), _chunk_row(G, CHUNK - 1, nc)
        vb = zi.astype(BF16)
        qrb = (q * jnp.exp(G - Gm)).astype(BF16)
        krb = (kk * jnp.exp(Gm - G)).astype(BF16)
        keb = (kk * jnp.exp(Gl - G)).astype(BF16)
        qgb = (q * jnp.exp(G)).astype(BF16)
        mask = msk_ref[0] > 0.5
        a = [jnp.where(mask, _dot_nt(qrb[:, cols[h]], krb[:, cols[h]]), 0.0).astype(BF16) for h in heads]
        d_st = [_dot_tn(vb[:, cols[h]], _block_diag(keb[:, cols[h]], nc)) for h in heads]
        o_intra = [_dot(a[h], vb[:, cols[h]]) for h in heads]
        st_cat = []
        for h in heads:
            st = s_ref[h]
            states = []
            for c in range(nc):
                states.append(st.astype(BF16))
                st_ref[c, h] = states[-1]
                st = st * jnp.exp(G[(c + 1) * CHUNK - 1:(c + 1) * CHUNK, cols[h]]) + d_st[h][:, c * W:(c + 1) * W]
            s_ref[h] = st
            st_cat.append(jnp.concatenate(states, axis=1))
        o = [o_intra[h] + _dot_nt(_block_diag(qgb[:, cols[h]], nc), st_cat[h]) for h in heads]
        gate = zg * _sigmoid(zg)
        for h in heads:
            o_ref[:, cols[h]] = o[h]
            r = lax.rsqrt(jnp.mean(o[h] * o[h], axis=-1, keepdims=True) + EPS)
            y_ref[1, :, cols[h]] = (o[h] * r * gn_ref[h] * gate[:, cols[h]]).astype(BF16)

    def rb(s, i):
        return s * tps + i

    def per_head(*shape):
        return pl.BlockSpec((H,) + shape, lambda g, s, i: (g,) + (0,) * len(shape))

    return pl.pallas_call(
        body, name="mixer_fwd", grid=(4 // H, seqs, tps),
        in_specs=[pl.BlockSpec((5, tm, H * W), lambda g, s, i: (0, rb(s, i), g)),
                  per_head(W, W), per_head(1, W), per_head(2, W), per_head(1, W),
                  _tri_spec(tm), _tri_spec(tm)],
        out_specs=[pl.BlockSpec((2, tm, H * W), lambda g, s, i: (0, rb(s, i), g)),
                   pl.BlockSpec((tm, H * W), lambda g, s, i: (rb(s, i), g)),
                   pl.BlockSpec((nc, H, W, W), lambda g, s, i: (rb(s, i), g, 0, 0))],
        out_shape=[jax.ShapeDtypeStruct((2, T, 4 * W), BF16),
                   jax.ShapeDtypeStruct((T, 4 * W), F32),
                   jax.ShapeDtypeStruct((T // CHUNK, 4, W, W), BF16)],
        scratch_shapes=[pltpu.VMEM((H, POOL_HALO, W), F32),
                        pltpu.VMEM((H, 4, tm + 32, W), F32),
                        pltpu.VMEM((H, W, W), F32)],
        compiler_params=_cparams(("arbitrary", "arbitrary", "arbitrary")),
    )(u5, pool_w_bf, scale4, theta4, gn4, tri_bf, tri_f)


def mixer_bwd(u5, dy2, o_pre, st_prev, pool_w_bf, scale4, theta4, gn4, tri_bf, tri_f, after, *, seqs, seq_len, tm):
    T = u5.shape[1]
    tps = seq_len // tm
    nc = tm // CHUNK
    W = HEAD_W
    hb = tm // POOL_HALO

    H = HEADS_PER_STEP
    heads = range(H)

    def body(u_ref, uh_ref, dy_ref, o_ref, st_ref, pw_ref, sc_ref, th_ref, gn_ref, tri_ref, msk_ref, _after_ref,
             du_ref, dpw_ref, dsc_ref, dlb_ref, dgn_ref, nxt_ref, ext_ref, ds_ref):
        g = pl.program_id(0)
        s = pl.program_id(1)
        i = pl.program_id(2)
        tile = tps - 1 - i
        first = (s == 0) & (i == 0)

        @pl.when(i == 0)
        def _():
            nxt_ref[...] = jnp.zeros_like(nxt_ref)
            ds_ref[...] = jnp.zeros_like(ds_ref)

        row = lax.broadcasted_iota(jnp.int32, (tm, 1), 0)
        cols = [slice(h * W, (h + 1) * W) for h in heads]

        def accumulate(ref, h, val):
            @pl.when(first)
            def _():
                ref[h] = val

            @pl.when(jnp.logical_not(first))
            def _():
                ref[h] += val

        def per_head(fn):
            return jnp.concatenate([jnp.broadcast_to(fn(cols[h]), (tm, W)) for h in heads], axis=1)

        inv_cnt, pb, dz = [], [], []
        for h in heads:
            grp = g * H + h
            inv_cnt.append(1.0 / _pool_count(grp, tile * tm + row))
            ext = ext_ref.at[h]
            up = u_ref[0, :, cols[h]]
            ext[0, 0:16] = jnp.zeros((16, W), F32)
            ext[0, 16:32] = jnp.where(tile == 0, 0.0, uh_ref[:, cols[h]])
            ext[0, 32:32 + tm] = up
            win = _select_window(grp, *_pool_windows_back(ext, tm))
            pb.append((win * inv_cnt[h] - up).astype(BF16))
            dz.append((dy_ref[0, :, cols[h]].astype(F32) * sc_ref[h]).astype(BF16))
        z = [_dot(pb[h], pw_ref[h]) for h in heads]
        dp = [_dot_nt(dz[h], pw_ref[h]) for h in heads]
        dpw = [_dot_tn(pb[h], dz[h]) for h in heads]
        for h in heads:
            accumulate(dsc_ref, h, jnp.sum(dy_ref[0, :, cols[h]].astype(F32) * z[h], axis=0, keepdims=True))
            accumulate(dpw_ref, h, dpw[h])
            ext = ext_ref.at[h]
            e = dp[h] * inv_cnt[h]
            ext[0, 0:tm] = e
            ext[0, tm:tm + 16] = nxt_ref[h]
            ext[0, tm + 16:tm + 32] = jnp.zeros((16, W), F32)
            lead = _select_window(g * H + h, *_pool_windows_fwd(ext, tm))
            nxt_ref[h] = e[0:POOL_HALO]
            du_ref[0, :, cols[h]] = (lead - dp[h]).astype(BF16)

        zq, zf, zi, zg = u_ref[1], u_ref[2], u_ref[3], u_ref[4]
        lb = jnp.concatenate([_sigmoid(th_ref[h][0:1, :] - th_ref[h][1:2, :]) for h in heads], axis=1)
        gn = jnp.concatenate([gn_ref[h] for h in heads], axis=1)
        sig, sq, sg = _sigmoid(zf), _sigmoid(zq), _sigmoid(zg)
        f = lb + (1.0 - lb) * sig
        kk = 1.0 - f
        q = zq * sq
        G = _tri_apply(tri_ref[0], jnp.log(f))

        dyh = dy_ref[1].astype(F32)
        o = o_ref[...]
        sqr = o * o
        r = per_head(lambda cs: lax.rsqrt(jnp.mean(sqr[:, cs], axis=-1, keepdims=True) + EPS))
        orr = o * r
        du_ref[4] = (dyh * (orr * gn) * (sg * (1.0 + zg * (1.0 - sg)))).astype(BF16)
        don = dyh * (zg * sg)
        dgn = jnp.sum(don * orr, axis=0, keepdims=True)
        dog = don * gn
        dog_orr = dog * orr
        do = r * (dog - orr * per_head(lambda cs: jnp.mean(dog_orr[:, cs], axis=-1, keepdims=True)))

        Gm, Gl = _chunk_row(G, CHUNK // 2 - 1, nc), _chunk_row(G, CHUNK - 1, nc)
        e_q, e_k, e_e, e_g = jnp.exp(G - Gm), jnp.exp(Gm - G), jnp.exp(Gl - G), jnp.exp(G)
        qr, kr, ke, qg = q * e_q, kk * e_k, kk * e_e, q * e_g
        qrb, krb, keb, qgb = qr.astype(BF16), kr.astype(BF16), ke.astype(BF16), qg.astype(BF16)
        vb = zi.astype(BF16)
        dob = do.astype(BF16)
        lower, upper = msk_ref[0] > 0.5, msk_ref[1] > 0.5
        da = [jnp.where(lower, _dot_nt(dob[:, cs], vb[:, cs]), 0.0).astype(BF16) for cs in cols]
        a_t = [jnp.where(upper, _dot_nt(krb[:, cs], qrb[:, cs]), 0.0).astype(BF16) for cs in cols]
        da_t = [jnp.where(upper, _dot_nt(vb[:, cs], dob[:, cs]), 0.0).astype(BF16) for cs in cols]
        u_cat = [_dot_tn(dob[:, cs], _block_diag(qgb[:, cs], nc)) for cs in cols]
        dqr = [_dot(da[h], krb[:, cols[h]]) for h in heads]
        dkr = [_dot(da_t[h], qrb[:, cols[h]]) for h in heads]
        dv = [_dot(a_t[h], dob[:, cols[h]]) for h in heads]
        dsn_rows, dsn_cols, ddecay = [], [], [[None] * H for _ in range(nc)]
        for h in heads:
            dsn = ds_ref[h]
            dsn_b = [None] * nc
            for c in reversed(range(nc)):
                decay = jnp.exp(G[(c + 1) * CHUNK - 1:(c + 1) * CHUNK, cols[h]])
                dsn_b[c] = dsn.astype(BF16)
                ddecay[c][h] = jnp.sum(dsn * st_ref[c, h].astype(F32), axis=0, keepdims=True) * decay
                dsn = u_cat[h][:, c * W:(c + 1) * W] + dsn * decay
            ds_ref[h] = dsn
            dsn_rows.append(jnp.concatenate(dsn_b, axis=0))
            dsn_cols.append(jnp.concatenate(dsn_b, axis=1))
        st_rows = [jnp.concatenate([st_ref[c, h] for c in range(nc)], axis=0) for h in heads]
        dqg = [_dot(_block_diag(dob[:, cols[h]], nc), st_rows[h]) for h in heads]
        dke = [_dot(_block_diag(vb[:, cols[h]], nc), dsn_rows[h]) for h in heads]
        dv = [dv[h] + _dot_nt(_block_diag(keb[:, cols[h]], nc), dsn_cols[h]) for h in heads]
        dqr, dkr, dqg, dke, dv = (jnp.concatenate(parts, axis=1) for parts in (dqr, dkr, dqg, dke, dv))
        t_mid, t_qg, t_ke = dkr * kr - dqr * qr, dqg * qg, dke * ke
        dq = dqr * e_q + dqg * e_g
        dk = dkr * e_k + dke * e_e
        crow = lax.broadcasted_iota(jnp.int32, (CHUNK, 1), 0)
        ends = []
        for c in range(nc):
            sl = slice(c * CHUNK, (c + 1) * CHUNK)
            dgm = jnp.sum(t_mid[sl], axis=0, keepdims=True)
            dgl = jnp.sum(t_ke[sl], axis=0, keepdims=True) + jnp.concatenate(ddecay[c], axis=1)
            ends.append(jnp.where(crow == CHUNK // 2 - 1, dgm, 0.0) + jnp.where(crow == CHUNK - 1, dgl, 0.0))
        dG = t_qg - t_ke - t_mid + jnp.concatenate(ends, axis=0)
        dlogf = _tri_apply(tri_ref[1], dG)
        df = dlogf / f - dk
        du_ref[1] = (dq * (sq * (1.0 + zq * (1.0 - sq)))).astype(BF16)
        du_ref[2] = (df * (1.0 - lb) * (sig * (1.0 - sig))).astype(BF16)
        du_ref[3] = dv.astype(BF16)
        dlb = jnp.sum(df * (1.0 - sig), axis=0, keepdims=True) * (lb * (1.0 - lb))
        for h in heads:
            accumulate(dgn_ref, h, dgn[:, cols[h]])
            accumulate(dlb_ref, h, dlb[:, cols[h]])

    def rb(s, i):
        return s * tps + (tps - 1 - i)

    def per_head_spec(*shape):
        return pl.BlockSpec((H,) + shape, lambda g, s, i: (g,) + (0,) * len(shape))

    vec, mat = per_head_spec(1, W), per_head_spec(W, W)
    return pl.pallas_call(
        body, name="mixer_bwd", grid=(4 // H, seqs, tps),
        in_specs=[pl.BlockSpec((5, tm, H * W), lambda g, s, i: (0, rb(s, i), g)),
                  pl.BlockSpec((None, POOL_HALO, H * W), lambda g, s, i: (0, jnp.maximum(rb(s, i) * hb - 1, 0), g)),
                  pl.BlockSpec((2, tm, H * W), lambda g, s, i: (0, rb(s, i), g)),
                  pl.BlockSpec((tm, H * W), lambda g, s, i: (rb(s, i), g)),
                  pl.BlockSpec((nc, H, W, W), lambda g, s, i: (rb(s, i), g, 0, 0)),
                  mat, vec, per_head_spec(2, W), vec, _tri_spec(tm), _tri_spec(tm),
                  pl.BlockSpec(after.shape, lambda g, s, i: (0, 0))],
        out_specs=[pl.BlockSpec((5, tm, H * W), lambda g, s, i: (0, rb(s, i), g)), mat, vec, vec, vec],
        out_shape=[jax.ShapeDtypeStruct((5, T, 4 * W), BF16),
                   jax.ShapeDtypeStruct((4, W, W), F32),
                   jax.ShapeDtypeStruct((4, 1, W), F32),
                   jax.ShapeDtypeStruct((4, 1, W), F32),
                   jax.ShapeDtypeStruct((4, 1, W), F32)],
        scratch_shapes=[pltpu.VMEM((H, POOL_HALO, W), F32),
                        pltpu.VMEM((H, 4, tm + 32, W), F32),
                        pltpu.VMEM((H, W, W), F32)],
        compiler_params=_cparams(("arbitrary", "arbitrary", "arbitrary")),
    )(u5, u5, dy2, o_pre, st_prev, pool_w_bf, scale4, theta4, gn4, tri_bf, tri_f, after)


def _attn_probs(q, k, hd):
    s = _dot_nt(q, k) * (1.0 / (hd ** 0.5))
    e = jnp.exp(s - jnp.max(s, axis=-1, keepdims=True))
    return e * (1.0 / jnp.sum(e, axis=-1, keepdims=True))


def attn_fwd(q, kv3, *, seqs, seq_len, n_mem, tm):
    T, D = q.shape
    hd = D // XATTN_HEADS
    tps = seq_len // tm

    cols = [slice(h * hd, (h + 1) * hd) for h in range(XATTN_HEADS)]

    def body(q_ref, kv_ref, o_ref):
        p = [_attn_probs(q_ref[:, cs], kv_ref[0, :, cs], hd) for cs in cols]
        for h, cs in enumerate(cols):
            o_ref[:, cs] = _dot(p[h].astype(BF16), kv_ref[1, :, cs]).astype(BF16)

    return pl.pallas_call(
        body, name="attn_fwd", grid=(seqs, tps),
        in_specs=[pl.BlockSpec((tm, D), lambda b, i: (b * tps + i, 0)),
                  pl.BlockSpec((2, n_mem, D), lambda b, i: (0, b, 0))],
        out_specs=pl.BlockSpec((tm, D), lambda b, i: (b * tps + i, 0)),
        out_shape=jax.ShapeDtypeStruct((T, D), BF16),
        compiler_params=_cparams(("parallel", "arbitrary")),
    )(q, kv3)


def attn_bwd(q, kv3, do, *, seqs, seq_len, n_mem, tm):
    T, D = q.shape
    hd = D // XATTN_HEADS
    tps = seq_len // tm

    cols = [slice(h * hd, (h + 1) * hd) for h in range(XATTN_HEADS)]

    def body(q_ref, kv_ref, do_ref, dq_ref, dkv_ref):
        i = pl.program_id(1)

        @pl.when(i == 0)
        def _():
            dkv_ref[...] = jnp.zeros_like(dkv_ref)

        p = [_attn_probs(q_ref[:, cs], kv_ref[0, :, cs], hd) for cs in cols]
        dp = [_dot_nt(do_ref[:, cs], kv_ref[1, :, cs]) for cs in cols]
        ds = [(p[h] * (dp[h] - jnp.sum(dp[h] * p[h], axis=-1, keepdims=True)) * (1.0 / (hd ** 0.5))).astype(BF16)
              for h in range(XATTN_HEADS)]
        for h, cs in enumerate(cols):
            dq_ref[:, cs] = _dot(ds[h], kv_ref[0, :, cs]).astype(BF16)
            dkv_ref[0, :, cs] += _dot_tn(ds[h], q_ref[:, cs])
            dkv_ref[1, :, cs] += _dot_tn(p[h].astype(BF16), do_ref[:, cs])

    qspec = pl.BlockSpec((tm, D), lambda b, i: (b * tps + i, 0))
    kvspec = pl.BlockSpec((2, n_mem, D), lambda b, i: (0, b, 0))
    return pl.pallas_call(
        body, name="attn_bwd", grid=(seqs, tps),
        in_specs=[qspec, kvspec, qspec],
        out_specs=[qspec, kvspec],
        out_shape=[jax.ShapeDtypeStruct((T, D), BF16), jax.ShapeDtypeStruct((2, seqs * n_mem, D), F32)],
        compiler_params=_cparams(("parallel", "arbitrary")),
    )(q, kv3, do)


def _my_place():
    return lax.axis_index("x"), lax.axis_index("y"), lax.axis_index("c")


def _slot_of(px, py, pc):
    return 4 * px + 2 * py + pc


def _peer(k, x, y, c):
    return (1 - x if (k >> 2) & 1 else x, 1 - y if (k >> 1) & 1 else y, 1 - c if k & 1 else c)


def _split_copies(src_refs, land_refs, send_sems, recv_sems, scatter):
    x, y, c = _my_place()
    mine = _slot_of(x, y, c)
    copies = []
    for a, (src, land) in enumerate(zip(src_refs, land_refs)):
        for k in range(1, N_DEV):
            peer = _peer(k, x, y, c)
            copies.append(pltpu.make_async_remote_copy(
                src_ref=src.at[_slot_of(*peer)] if scatter else src, dst_ref=land.at[mine],
                send_sem=send_sems.at[a * N_PEERS + k - 1], recv_sem=recv_sems.at[a * N_PEERS + k - 1],
                device_id=peer, device_id_type=MESH))
    return copies


def split_start(groups, *, name, scatter):
    sizes = [len(srcs) for srcs, _ in groups]
    n_arr = sum(sizes)
    flat = [a for srcs, lands in groups for a in list(srcs) + list(lands)]

    def body(*refs):
        ins = refs[:2 * n_arr]
        sems = refs[4 * n_arr:4 * n_arr + 2 * len(groups)]
        token = refs[-1]
        at = 0
        for gi, n in enumerate(sizes):
            for cp in _split_copies(ins[at:at + n], ins[at + n:at + 2 * n], sems[2 * gi], sems[2 * gi + 1], scatter):
                cp.start()
            at += 2 * n
        token[...] = jnp.zeros_like(token)

    sem_shapes = []
    for n in sizes:
        sem_shapes += [pltpu.SemaphoreType.DMA((n * N_PEERS,))] * 2
    outs = pl.pallas_call(
        body, name=name,
        out_shape=tuple(pltpu.HBM(a.shape, a.dtype) for a in flat) + tuple(sem_shapes)
        + (jax.ShapeDtypeStruct((8, 128), F32),),
        in_specs=(HBM,) * len(flat),
        out_specs=(HBM,) * len(flat) + (SEM,) * len(sem_shapes) + (pl.BlockSpec(memory_space=pltpu.VMEM),),
        input_output_aliases={i: i for i in range(len(flat))},
        compiler_params=pltpu.CompilerParams(has_side_effects=pltpu.SideEffectType.DATAFLOW_SIDE_EFFECTING),
    )(*[pltpu.with_memory_space_constraint(a, pltpu.HBM) for a in flat])
    thru, sems, token = outs[:len(flat)], outs[len(flat):-1], outs[-1]
    started, at = [], 0
    for gi, n in enumerate(sizes):
        started.append((sems[2 * gi], sems[2 * gi + 1], thru[at:at + n], thru[at + n:at + 2 * n]))
        at += 2 * n
    return started, token


def split_wait(started, after, *, name, scatter):
    sizes = [len(g[2]) for g in started]
    n_arr = sum(sizes)
    flat = [a for g in started for a in list(g[2]) + list(g[3])]
    sems = [s for g in started for s in g[:2]]

    def body(*refs):
        ins = refs[:2 * n_arr]
        sem_refs = refs[2 * n_arr:2 * n_arr + len(sems)]
        at = 0
        for gi, n in enumerate(sizes):
            for cp in _split_copies(ins[at:at + n], ins[at + n:at + 2 * n], sem_refs[2 * gi], sem_refs[2 * gi + 1], scatter):
                cp.wait_send()
                cp.wait_recv()
            at += 2 * n

    outs = pl.pallas_call(
        body, name=name,
        out_shape=tuple(pltpu.HBM(a.shape, a.dtype) for a in flat),
        in_specs=(HBM,) * len(flat) + (SEM,) * len(sems) + (pl.BlockSpec(memory_space=pl.ANY),),
        out_specs=(HBM,) * len(flat),
        input_output_aliases={i: i for i in range(len(flat))},
        compiler_params=pltpu.CompilerParams(has_side_effects=pltpu.SideEffectType.DATAFLOW_SIDE_EFFECTING),
    )(*flat, *sems, after)
    done, at = [], 0
    for n in sizes:
        done.append((outs[at:at + n], outs[at + n:at + 2 * n]))
        at += 2 * n
    return done


SIBLING = 1
CHIP_PEERS = (2, 4, 6)
_SIDE_EFFECTS = pltpu.CompilerParams(has_side_effects=pltpu.SideEffectType.DATAFLOW_SIDE_EFFECTING)


def _chip_level_copies(src, land, send_sems, recv_sems):
    x, y, c = _my_place()
    return [pltpu.make_async_remote_copy(
        src_ref=src, dst_ref=land.at[_slot_of(x, y, c)], send_sem=send_sems.at[j], recv_sem=recv_sems.at[j],
        device_id=_peer(k, x, y, c), device_id_type=MESH) for j, k in enumerate((SIBLING,) + CHIP_PEERS)]


def _pass_on_copies(land, send_sems, recv_sems, receiving):
    x, y, c = _my_place()
    copies = []
    for j, k in enumerate(CHIP_PEERS):
        slot = _slot_of(*_peer(k ^ SIBLING if receiving else k, x, y, c))
        copies.append(pltpu.make_async_remote_copy(
            src_ref=land.at[slot], dst_ref=land.at[slot], send_sem=send_sems.at[j], recv_sem=recv_sems.at[j],
            device_id=_peer(SIBLING, x, y, c), device_id_type=MESH))
    return copies


def gather2_start(src, land, *, name):
    def body(src_ref, land_ref, src_out, land_out, send_sems, recv_sems, token):
        for cp in _chip_level_copies(src_ref, land_ref, send_sems, recv_sems):
            cp.start()
        token[...] = jnp.zeros_like(token)

    n = 1 + len(CHIP_PEERS)
    src_t, land_t, send_sems, recv_sems, token = pl.pallas_call(
        body, name=name,
        out_shape=(pltpu.HBM(src.shape, src.dtype), pltpu.HBM(land.shape, land.dtype),
                   pltpu.SemaphoreType.DMA((n,)), pltpu.SemaphoreType.DMA((n,)), jax.ShapeDtypeStruct((8, 128), F32)),
        in_specs=(HBM, HBM), out_specs=(HBM, HBM, SEM, SEM, pl.BlockSpec(memory_space=pltpu.VMEM)),
        input_output_aliases={0: 0, 1: 1}, compiler_params=_SIDE_EFFECTS,
    )(pltpu.with_memory_space_constraint(src, pltpu.HBM), pltpu.with_memory_space_constraint(land, pltpu.HBM))
    return (src_t, land_t, send_sems, recv_sems), token


def gather2_pass_on(started, after, *, name):
    src, land, send_a, recv_a = started

    def body(src_ref, land_ref, send_a_ref, recv_a_ref, after_ref, land_out, send_b, recv_b):
        for cp in _chip_level_copies(src_ref, land_ref, send_a_ref, recv_a_ref):
            cp.wait_send()
            cp.wait_recv()
        for cp in _pass_on_copies(land_ref, send_b, recv_b, False):
            cp.start()

    n = len(CHIP_PEERS)
    land_t, send_b, recv_b = pl.pallas_call(
        body, name=name,
        out_shape=(pltpu.HBM(land.shape, land.dtype), pltpu.SemaphoreType.DMA((n,)), pltpu.SemaphoreType.DMA((n,))),
        in_specs=(HBM, HBM, SEM, SEM, pl.BlockSpec(memory_space=pl.ANY)), out_specs=(HBM, SEM, SEM),
        input_output_aliases={1: 0}, compiler_params=_SIDE_EFFECTS,
    )(src, land, send_a, recv_a, after)
    return land_t, send_b, recv_b


def gather2_wait(passed, *, name):
    land, send_b, recv_b = passed

    def body(land_ref, send_ref, recv_ref, land_out):
        for cp in _pass_on_copies(land_ref, send_ref, recv_ref, False):
            cp.wait_send()
        for cp in _pass_on_copies(land_ref, send_ref, recv_ref, True):
            cp.wait_recv()

    return pl.pallas_call(
        body, name=name, out_shape=pltpu.HBM(land.shape, land.dtype),
        in_specs=(HBM, SEM, SEM), out_specs=HBM,
        input_output_aliases={0: 0}, compiler_params=_SIDE_EFFECTS,
    )(land, send_b, recv_b)


def _adamw_math(g, w, m, v):
    c1 = 1.0 - ADAM_B1 ** ADAM_STEP
    c2 = 1.0 - ADAM_B2 ** ADAM_STEP
    nm = ADAM_B1 * m + (1.0 - ADAM_B1) * g
    nv = ADAM_B2 * v + (1.0 - ADAM_B2) * (g * g)
    delta = -ADAM_LR * ((nm / c1) / (jnp.sqrt(nv / c2) + ADAM_EPS) + ADAM_WD * w)
    return delta, nm, nv


def adamw_sharded(me, owns, recvs, ws, ms, vs, *, name, tr):
    n = len(ws)
    R, C = ws[0].shape

    def body(me_ref, *refs):
        for t in range(n):
            parts = refs[t * N_DEV:(t + 1) * N_DEV]
            w_ref, m_ref, v_ref = refs[n * N_DEV + 3 * t:n * N_DEV + 3 * t + 3]
            g_ref, d_ref, nm_ref, nv_ref = refs[n * (N_DEV + 3) + 4 * t:n * (N_DEV + 3) + 4 * t + 4]
            g = parts[0][...].astype(F32)
            for p in parts[1:]:
                g = g + p[...].astype(F32)
            g_ref[...] = g
            d_ref[...], nm_ref[...], nv_ref[...] = _adamw_math(g, w_ref[...], m_ref[...], v_ref[...])

    def slab(k):
        return pl.BlockSpec((None, tr, C), lambda i, me_ref: (me_ref[0] ^ k, i, 0))

    blk = pl.BlockSpec((tr, C), lambda i, me_ref: (i, 0))
    out = jax.ShapeDtypeStruct((R, C), F32)
    args = [me]
    for t in range(n):
        args += [owns[t]] + [recvs[t]] * N_PEERS
    for t in range(n):
        args += [ws[t], ms[t], vs[t]]
    outs = pl.pallas_call(
        body, name=name,
        grid_spec=pltpu.PrefetchScalarGridSpec(
            num_scalar_prefetch=1, grid=(R // tr,),
            in_specs=[slab(k) for _ in range(n) for k in range(N_DEV)] + [blk] * (3 * n),
            out_specs=[blk] * (4 * n)),
        out_shape=[out] * (4 * n),
        compiler_params=_cparams(("parallel",)),
    )(*args)
    return [outs[4 * t:4 * t + 4] for t in range(n)]


def adamw_replicated(parts, ws, ms, vs, rows):
    n_buf, n_par = len(parts), len(ws)

    def body(*refs):
        p_refs = refs[:n_buf]
        w_refs = refs[n_buf:n_buf + n_par]
        m_refs = refs[n_buf + n_par:n_buf + 2 * n_par]
        v_refs = refs[n_buf + 2 * n_par:n_buf + 3 * n_par]
        outs = refs[n_buf + 3 * n_par:]
        sums = []
        for p in p_refs:
            g = p[0]
            for s in range(1, N_DEV):
                g = g + p[s]
            sums.append(g)
        for j, (b, r0, nr) in enumerate(rows):
            g = sums[b][r0:r0 + nr]
            delta, nm, nv = _adamw_math(g, w_refs[j][...], m_refs[j][...], v_refs[j][...])
            outs[j][...] = g
            outs[n_par + j][...] = delta
            outs[2 * n_par + j][...] = nm
            outs[3 * n_par + j][...] = nv

    shapes = [jax.ShapeDtypeStruct(w.shape, F32) for w in ws]
    outs = pl.pallas_call(
        body, name="adamw_replicated", out_shape=shapes * 4,
        compiler_params=pltpu.CompilerParams(vmem_limit_bytes=V7X_VMEM_LIMIT),
    )(*parts, *ws, *ms, *vs)
    return outs[:n_par], outs[n_par:2 * n_par], outs[2 * n_par:3 * n_par], outs[3 * n_par:]


BIG = ("w_in", "w_out", "xw_q", "xw_kv", "xw_o", "w_up", "w_down")
ADAMW_CALLS = (("w_in",), ("w_out", "xw_q", "xw_o"), ("xw_kv",), ("w_up",), ("w_down",))
WEIGHTS = ("norm_mix", "w_in", "pool_w", "pool_scale", "lb_theta", "hgrn_norm", "w_out", "norm_xq",
           "norm_mem", "xw_q", "xw_kv", "xw_o", "norm_mlp", "w_up", "w_down", "norm_final")
SMALL = (("pool_w", (4 * HEAD_W, HEAD_W), 0, 0),
         ("norm_mix", (1, 1024), 1, 0), ("norm_xq", (1, 1024), 1, 1), ("norm_mem", (1, 1024), 1, 2),
         ("norm_mlp", (1, 1024), 1, 3), ("norm_final", (1, 1024), 1, 4),
         ("pool_scale", (1, 512), 2, 0), ("hgrn_norm", (1, 512), 2, 1), ("lb_theta", (2, 512), 2, 2))


def _pad_rows(a, rows):
    return jnp.concatenate([a, jnp.zeros((rows - a.shape[0], a.shape[1]), a.dtype)], axis=0)


def kernel(x, mem, norm_mix, w_in, pool_w, pool_scale, lb_theta, hgrn_norm, w_out, norm_xq, norm_mem, xw_q, xw_kv, xw_o, norm_mlp, w_up, w_down, norm_final, loss_target, m_norm_mix, m_w_in, m_pool_w, m_pool_scale, m_lb_theta, m_hgrn_norm, m_w_out, m_norm_xq, m_norm_mem, m_xw_q, m_xw_kv, m_xw_o, m_norm_mlp, m_w_up, m_w_down, m_norm_final, v_norm_mix, v_w_in, v_pool_w, v_pool_scale, v_lb_theta, v_hgrn_norm, v_w_out, v_norm_xq, v_norm_mem, v_xw_q, v_xw_kv, v_xw_o, v_norm_mlp, v_w_up, v_w_down, v_norm_final):
    w = dict(norm_mix=norm_mix, w_in=w_in, pool_w=pool_w, pool_scale=pool_scale, lb_theta=lb_theta,
             hgrn_norm=hgrn_norm, w_out=w_out, norm_xq=norm_xq, norm_mem=norm_mem, xw_q=xw_q, xw_kv=xw_kv,
             xw_o=xw_o, norm_mlp=norm_mlp, w_up=w_up, w_down=w_down, norm_final=norm_final)
    mom = dict(norm_mix=m_norm_mix, w_in=m_w_in, pool_w=m_pool_w, pool_scale=m_pool_scale, lb_theta=m_lb_theta,
               hgrn_norm=m_hgrn_norm, w_out=m_w_out, norm_xq=m_norm_xq, norm_mem=m_norm_mem, xw_q=m_xw_q,
               xw_kv=m_xw_kv, xw_o=m_xw_o, norm_mlp=m_norm_mlp, w_up=m_w_up, w_down=m_w_down,
               norm_final=m_norm_final)
    var = dict(norm_mix=v_norm_mix, w_in=v_w_in, pool_w=v_pool_w, pool_scale=v_pool_scale, lb_theta=v_lb_theta,
               hgrn_norm=v_hgrn_norm, w_out=v_w_out, norm_xq=v_norm_xq, norm_mem=v_norm_mem, xw_q=v_xw_q,
               xw_kv=v_xw_kv, xw_o=v_xw_o, norm_mlp=v_norm_mlp, w_up=v_w_up, w_down=v_w_down,
               norm_final=v_norm_final)

    seqs, seq_len, D = x.shape
    n_mem = mem.shape[1]
    T = seqs * seq_len
    W = HEAD_W
    x2 = x.reshape(T, D)
    mem2 = mem.reshape(seqs * n_mem, D)
    tgt2 = loss_target.reshape(T, D)
    tm_big = min(1024, T)
    tm_mid = min(512, T)
    tm_sq = min(1024, T)
    tm_mix = min(256, seq_len)
    tm_att = min(1024, seq_len)
    tkv = min(512, seqs * n_mem)
    px, py, pc = _my_place()
    me = _slot_of(px, py, pc).astype(jnp.int32)
    me1 = me.reshape(1)

    shard_bf = {n: w[n][0].astype(BF16) for n in BIG}

    def landing(n):
        zone = lax.empty((N_DEV,) + shard_bf[n].shape, BF16)
        return lax.dynamic_update_slice(zone, shard_bf[n][None], (me, 0, 0))

    w_in_started, tok = gather2_start(shard_bf["w_in"], landing("w_in"), name="w_in_gather_start")
    shard_bf["w_out"] = shard_bf["w_out"] + tok[0, 0].astype(BF16)
    ag_groups = (("w_out", "xw_q", "xw_kv", "xw_o"), ("w_up",), ("w_down",))
    ag_started, tok = split_start([([shard_bf[n] for n in grp], [landing(n) for n in grp]) for grp in ag_groups],
                                name="weights_gather_start", scatter=False)

    pool_w_bf = pool_w[0].astype(BF16)
    scale4 = pool_scale.reshape(4, 1, W)
    gn4 = hgrn_norm.reshape(4, 1, W)
    theta4 = lb_theta.reshape(2, 4, W).transpose(1, 0, 2)
    g_final = norm_final.reshape(1, D)

    n1 = prenorm(x2, norm_mix, tok, tm=tm_sq)
    wi3 = gather2_wait(gather2_pass_on(w_in_started, n1, name="w_in_gather_pass_on"), name="w_in_gather_wait")
    full_w_in = wi3.transpose(1, 0, 2).reshape(D, -1)
    u5 = proj_plain(n1, full_w_in, name="in_proj", tm=tm_sq, tn=4 * W, out_dtype=F32, out_slabs=5)
    tri_bf, tri_f = chunk_triangles(tm_mix)
    y2, o_pre, st_prev = mixer_fwd(u5, pool_w_bf, scale4, theta4, gn4, tri_bf, tri_f, seqs=seqs, seq_len=seq_len,
                                   tm=tm_mix)
    (_, (wo3, wq3, wkv3, wao3)), = split_wait(ag_started[0:1], y2, name="weights_gather_wait_attn", scatter=False)
    full_w_out, full_xw_q, full_xw_o = wo3.reshape(D, D), wq3.reshape(D, D), wao3.reshape(D, D)
    tn = 4 * W
    h1, n2, q = proj_res_norm(y2, full_w_out, x2, norm_xq, full_xw_q, name="out_q_proj", tm=tm_sq, tn=tn)
    tn_kv, tn_up = xw_kv.shape[2], w_up.shape[2]
    kv3, memn = proj_norm(mem2, norm_mem, wkv3, name="kv_proj", tm=tkv, tn=tn_kv, out_dtype=BF16, out_slabs=2)
    o_att = attn_fwd(q, kv3, seqs=seqs, seq_len=seq_len, n_mem=n_mem, tm=tm_att)
    h2, n3 = proj_res_norm(o_att, full_xw_o, h1, norm_mlp, name="attn_out_proj", tm=tm_sq, tn=tn)
    (_, (wup3,)), = split_wait(ag_started[1:2], h2, name="weights_gather_wait_up", scatter=False)
    aa, da = proj_plain(n3, wup3, name="up_proj", tm=tm_mid, tn=tn_up, relu2=True)
    (_, (wdn3,)), = split_wait(ag_started[2:3], aa, name="weights_gather_wait_down", scatter=False)
    full_w_down = wdn3.reshape(-1, D)
    dh3, dh3b, sq_err, dg_final = proj_res_loss(aa, full_w_down, h2, g_final, tgt2, name="down_proj_loss",
                                                tm=tm_mid, tn=tn)

    def send(parts, name):
        srcs = [p.reshape((N_DEV, -1, p.shape[-1])) for p in parts]
        lands = [lax.empty(s.shape, BF16) for s in srcs]
        started, token = split_start([(srcs, lands)], name=name, scatter=True)
        return started[0], token

    gw_down = wgrad(aa, dh3b, name="down_proj_wgrad", tt=tm_mid, tn=tn)
    dap = back_plain(dh3b, full_w_down, name="down_proj_bwd", tm=tm_mid, tn=tn, out_dtype=BF16, factor=da)
    gw_up = wgrad(n3, dap, name="up_proj_wgrad", tt=tm_mid, tn=tn_up, out_slabs=N_DEV)
    sent_mlp, tok = send([gw_down, gw_up], "grads_send_mlp")
    dh2, dh2b, do_att, dg_mlp = back_norm(dap, wup3, h2, norm_mlp, dh3, name="up_proj_bwd", tm=tm_mid, tk=tn_up,
                                          w_next=full_xw_o, after=tok)
    gxw_o = wgrad(o_att, dh2b, name="attn_out_proj_wgrad", tt=tm_sq, tn=tn)
    dq, dkv3 = attn_bwd(q, kv3, do_att, seqs=seqs, seq_len=seq_len, n_mem=n_mem, tm=tm_att)
    gxw_q = wgrad(n2, dq, name="q_proj_wgrad", tt=tm_sq, tn=tn)
    gxw_kv = wgrad(memn, dkv3, name="kv_proj_wgrad", tt=tkv, tn=tn_kv, out_slabs=N_DEV)
    dg_mem = back_norm(dkv3, wkv3, mem2, norm_mem, None, name="kv_proj_bwd", tm=tkv, tk=tn_kv)
    dh1, dh1b, dy2, dg_xq = back_norm(dq, full_xw_q, h1, norm_xq, dh2, name="q_proj_bwd", tm=tm_sq, tk=D,
                                      w_next=full_w_out, next_slabs=2)
    gw_out = wgrad(y2, dh1b, name="out_proj_wgrad", tt=tm_sq, tn=tn)
    sent_attn, tok = send([gxw_o, gxw_q, gxw_kv, gw_out], "grads_send_attn")
    du5, dpw, dsc, dlb, dgn = mixer_bwd(u5, dy2, o_pre, st_prev, pool_w_bf, scale4, theta4, gn4, tri_bf, tri_f, tok,
                                        seqs=seqs, seq_len=seq_len, tm=tm_mix)
    gw_in = wgrad(n1, du5, name="in_proj_wgrad", tt=tm_sq, tn=tn)
    gw_in_slots = gw_in.reshape(D, N_DEV, -1).transpose(1, 0, 2)
    sent_in, tok = send([gw_in_slots], "grads_send_in")
    dx, dg_mix = back_norm(du5, full_w_in, x2, norm_mix, dh1, name="in_proj_bwd", tm=tm_sq, tk=tn, bf16_copy=False,
                           after=tok)

    dlb_row = dlb.reshape(1, 4 * W)
    buf_vec = _pad_rows(jnp.concatenate([dg_mix, dg_xq, dg_mem, dg_mlp, dg_final, sq_err], axis=0), 8)
    buf_half = _pad_rows(jnp.concatenate([dsc.reshape(1, 4 * W), dgn.reshape(1, 4 * W), dlb_row, -dlb_row], axis=0), 8)
    small_src = [dpw.reshape(4 * W, W), buf_vec, buf_half]
    small_land = [lax.dynamic_update_slice(lax.empty((N_DEV,) + b.shape, F32), b[None], (me, 0, 0))
                  for b in small_src]
    small_started, tok = split_start([(small_src, small_land)], name="small_grads_start", scatter=False)

    done = split_wait([sent_mlp, sent_attn, sent_in], tok, name="grads_wait", scatter=True)
    slots = dict(w_down=(0, 0), w_up=(0, 1), xw_o=(1, 0), xw_q=(1, 1), xw_kv=(1, 2), w_out=(1, 3), w_in=(2, 0))
    own = {n: done[gi][0][ai] for n, (gi, ai) in slots.items()}
    got = {n: done[gi][1][ai] for n, (gi, ai) in slots.items()}
    res = {}
    for names in ADAMW_CALLS:
        shp = w[names[0]].shape
        r = adamw_sharded(me1, [own[n] for n in names], [got[n] for n in names], [w[n][0] for n in names],
                          [mom[n][0] for n in names], [var[n][0] for n in names], name="adamw_" + names[0],
                          tr=min(256, shp[1]))
        for n, outs in zip(names, r):
            for kind, a in zip("gdmv", outs):
                res[kind, n] = a.reshape(shp)
    (_, small_parts), = split_wait(small_started, res["g", BIG[-1]], name="small_grads_wait", scatter=False)
    loss = 0.5 * jnp.sum(small_parts[1][:, 5, :]) / D
    r = adamw_replicated(small_parts, [w[n].reshape(v2) for n, v2, _, _ in SMALL],
                         [mom[n].reshape(v2) for n, v2, _, _ in SMALL],
                         [var[n].reshape(v2) for n, v2, _, _ in SMALL],
                         [(b, r0, v2[0]) for _, v2, b, r0 in SMALL])
    for kind, arrs in zip("gdmv", r):
        for (n, _, _, _), a in zip(SMALL, arrs):
            res[kind, n] = a.reshape(w[n].shape)

    out = [loss, dx.reshape(x.shape)]
    for kind in "gdmv":
        out += [res[kind, n] for n in WEIGHTS]
    return tuple(out)
```

```python
import jax
import jax.numpy as jnp
from jax import lax
from jax.experimental import pallas as pl
from jax.experimental.pallas import tpu as pltpu

F32 = jnp.float32
BF16 = jnp.bfloat16
EPS = 1e-6
CHUNK = 64
POOL_HALO = 16
HEAD_W = 128
HEADS_PER_STEP = 4
XATTN_HEADS = 4
N_DEV = 8
N_PEERS = N_DEV - 1
ADAM_LR = 0.001
ADAM_B1 = 0.9
ADAM_B2 = 0.999
ADAM_EPS = 1e-08
ADAM_WD = 0.01
ADAM_STEP = 10
V7X_VMEM_LIMIT = 52 * 1024 * 1024
MESH = pl.DeviceIdType.MESH
HBM = pl.BlockSpec(memory_space=pltpu.HBM)
SEM = pl.BlockSpec(memory_space=pltpu.SEMAPHORE)


def _cparams(dims):
    return pltpu.CompilerParams(dimension_semantics=dims, vmem_limit_bytes=V7X_VMEM_LIMIT)


def _sigmoid(v):
    return 0.5 * jnp.tanh(0.5 * v) + 0.5


def _dot(a, b):
    return jnp.dot(a, b, preferred_element_type=F32)


def _dot_nt(a, b):
    return lax.dot_general(a, b, (((1,), (1,)), ((), ())), preferred_element_type=F32)


def _dot_tn(a, b):
    return lax.dot_general(a, b, (((0,), (0,)), ((), ())), preferred_element_type=F32)


def _tri_apply(tri, v):
    hi = v.astype(BF16)
    lo = (v - hi.astype(F32)).astype(BF16)
    return _dot(tri, hi) + _dot(tri, lo)


def _mat_shape(a):
    return a.shape if a.ndim == 2 else (a.shape[1], a.shape[0] * a.shape[2])


def _out_struct(rows, n, slabs, dtype):
    return jax.ShapeDtypeStruct((rows, n) if slabs is None else (slabs, rows, n // slabs), dtype)


def _resident(a):
    nd = a.ndim
    return pl.BlockSpec(a.shape, lambda i: (0,) * nd, pipeline_mode=pl.Buffered(1))


def _row_block(a, tm):
    if a.ndim == 2:
        return pl.BlockSpec((tm, a.shape[1]), lambda i: (i, 0))
    return pl.BlockSpec((a.shape[0], tm, a.shape[2]), lambda i: (0, i, 0))


def _cols(ref, c, width):
    if len(ref.shape) == 2:
        return ref[:, c * width:(c + 1) * width]
    per = ref.shape[2] // width
    if per == 1:
        return ref[c]
    return ref[c // per, :, (c % per) * width:(c % per + 1) * width]


def _set_cols(ref, c, width, val):
    if len(ref.shape) == 2:
        ref[:, c * width:(c + 1) * width] = val
        return
    per = ref.shape[2] // width
    if per == 1:
        ref[c] = val
    else:
        ref[c // per, :, (c % per) * width:(c % per + 1) * width] = val


def _all_cols(ref):
    if len(ref.shape) == 2:
        return ref[...]
    return jnp.concatenate([ref[s] for s in range(ref.shape[0])], axis=1)


def _rms(x):
    return lax.rsqrt(jnp.mean(x * x, axis=-1, keepdims=True) + EPS)


def _row_params():
    return _cparams(("arbitrary",))


def proj_norm(h, g, w, *, name, tm, tn, out_dtype, out_slabs=None):
    T, D = h.shape
    N = _mat_shape(w)[1]
    o_shape = _out_struct(T, N, out_slabs, out_dtype)

    def body(h_ref, g_ref, w_ref, o_ref, n_ref):
        x = h_ref[...]
        n = (x * _rms(x) * g_ref[...]).astype(BF16)
        n_ref[...] = n
        for c in range(N // tn):
            _set_cols(o_ref, c, tn, _dot(n, _cols(w_ref, c, tn)).astype(out_dtype))

    return pl.pallas_call(
        body, name=name, grid=(T // tm,),
        in_specs=[_row_block(h, tm), pl.BlockSpec((1, D), lambda i: (0, 0)), _resident(w)],
        out_specs=[_row_block(o_shape, tm), pl.BlockSpec((tm, D), lambda i: (i, 0))],
        out_shape=[o_shape, jax.ShapeDtypeStruct((T, D), BF16)],
        compiler_params=_row_params(),
    )(h, g, w)


def prenorm(h, g, after, *, tm):
    T, D = h.shape

    def body(h_ref, g_ref, _after_ref, n_ref):
        x = h_ref[...]
        n_ref[...] = (x * _rms(x) * g_ref[...]).astype(BF16)

    row = pl.BlockSpec((tm, D), lambda i: (i, 0))
    return pl.pallas_call(
        body, name="prenorm", grid=(T // tm,),
        in_specs=[row, pl.BlockSpec((1, D), lambda i: (0, 0)), _anchor_spec(after)],
        out_specs=row, out_shape=jax.ShapeDtypeStruct((T, D), BF16),
        compiler_params=_row_params(),
    )(h, g, after)


def proj_plain(a, w, *, name, tm, tn, out_dtype=BF16, out_slabs=None, relu2=False):
    T = a.shape[0]
    N = _mat_shape(w)[1]

    def body(a_ref, w_ref, o_ref, *d_ref):
        av = a_ref[...]
        for c in range(N // tn):
            z = _dot(av, _cols(w_ref, c, tn))
            if relu2:
                z = jnp.maximum(z, 0.0)
                _set_cols(d_ref[0], c, tn, (z + z).astype(out_dtype))
                z = z * z
            _set_cols(o_ref, c, tn, z.astype(out_dtype))

    o_shape = _out_struct(T, N, out_slabs, out_dtype)
    n_out = 2 if relu2 else 1
    outs = pl.pallas_call(
        body, name=name, grid=(T // tm,),
        in_specs=[_row_block(a, tm), _resident(w)],
        out_specs=[_row_block(o_shape, tm)] * n_out, out_shape=[o_shape] * n_out,
        compiler_params=_row_params(),
    )(a, w)
    return outs if relu2 else outs[0]


def proj_res_norm(a, w, res, g, w_next=None, *, name, tm, tn):
    T = res.shape[0]
    D = w.shape[1]
    chained = w_next is not None

    def body(*refs):
        a_ref, w_ref, r_ref, g_ref = refs[:4]
        h_ref, n_ref = refs[4 + chained], refs[5 + chained]
        av = _all_cols(a_ref)
        for c in range(D // tn):
            sl = slice(c * tn, (c + 1) * tn)
            h_ref[:, sl] = r_ref[:, sl] + _dot(av, w_ref[:, sl])
        hv = h_ref[...]
        n = (hv * _rms(hv) * g_ref[...]).astype(BF16)
        n_ref[...] = n
        if chained:
            for c in range(D // tn):
                sl = slice(c * tn, (c + 1) * tn)
                refs[-1][:, sl] = _dot(n, refs[4][:, sl]).astype(BF16)

    row = pl.BlockSpec((tm, D), lambda i: (i, 0))
    half = jax.ShapeDtypeStruct((T, D), BF16)
    return pl.pallas_call(
        body, name=name, grid=(T // tm,),
        in_specs=[_row_block(a, tm), _resident(w), row, pl.BlockSpec((1, D), lambda i: (0, 0))]
        + ([_resident(w_next)] if chained else []),
        out_specs=[row, row] + ([row] if chained else []),
        out_shape=[jax.ShapeDtypeStruct((T, D), F32), half] + ([half] if chained else []),
        compiler_params=_row_params(),
    )(*([a, w, res, g] + ([w_next] if chained else [])))


def proj_res_loss(a, w, res, g, target, *, name, tm, tn):
    T = res.shape[0]
    D = w.shape[1]

    def body(a_ref, w_ref, r_ref, g_ref, t_ref, dh_ref, dhb_ref, ls_ref, dg_ref):
        i = pl.program_id(0)
        gv = g_ref[...]
        ls, dg = 0.0, 0.0
        halves = [slice(s * (tm // 2), (s + 1) * (tm // 2)) for s in range(2)]
        for rows in halves:
            av = a_ref[rows, :]
            for c in range(D // tn):
                sl = slice(c * tn, (c + 1) * tn)
                dh_ref[rows, sl] = r_ref[rows, sl] + _dot(av, w_ref[:, sl])
        for rows in halves:
            x = dh_ref[rows, :]
            r = _rms(x)
            xr = x * r
            d = xr * gv - t_ref[rows, :]
            dy = d * (1.0 / D)
            dyg = dy * gv
            dx = r * (dyg - xr * jnp.mean(dyg * xr, axis=-1, keepdims=True))
            dh_ref[rows, :] = dx
            dhb_ref[rows, :] = dx.astype(BF16)
            ls = ls + jnp.sum(d * d, axis=0, keepdims=True)
            dg = dg + jnp.sum(dy * xr, axis=0, keepdims=True)

        @pl.when(i == 0)
        def _():
            ls_ref[...] = ls
            dg_ref[...] = dg

        @pl.when(i > 0)
        def _():
            ls_ref[...] += ls
            dg_ref[...] += dg

    row = pl.BlockSpec((tm, D), lambda i: (i, 0))
    vec = pl.BlockSpec((1, D), lambda i: (0, 0))
    return pl.pallas_call(
        body, name=name, grid=(T // tm,),
        in_specs=[_row_block(a, tm), _resident(w), row, vec, row],
        out_specs=[row, row, vec, vec],
        out_shape=[jax.ShapeDtypeStruct((T, D), F32), jax.ShapeDtypeStruct((T, D), BF16),
                   jax.ShapeDtypeStruct((1, D), F32), jax.ShapeDtypeStruct((1, D), F32)],
        compiler_params=_row_params(),
    )(a, w, res, g, target)


def _anchor_spec(after):
    return pl.BlockSpec(after.shape, lambda i: (0, 0))


def back_plain(a, w, *, name, tm, tn, out_dtype, out_slabs=None, factor=None, after=None):
    T = a.shape[0]
    N = w.shape[0]
    has_z = factor is not None
    o_shape = _out_struct(T, N, out_slabs, out_dtype)

    def body(*refs):
        a_ref, w_ref = refs[0], refs[1]
        o_ref = refs[-1]
        av = a_ref[...]
        for c in range(N // tn):
            out = _dot_nt(av, w_ref[c * tn:(c + 1) * tn, :])
            if has_z:
                out = out * refs[2][:, c * tn:(c + 1) * tn].astype(F32)
            _set_cols(o_ref, c, tn, out.astype(out_dtype))

    in_specs, args = [_row_block(a, tm), _resident(w)], [a, w]
    if has_z:
        in_specs.append(_row_block(factor, tm))
        args.append(factor)
    if after is not None:
        in_specs.append(_anchor_spec(after))
        args.append(after)
    return pl.pallas_call(
        body, name=name, grid=(T // tm,),
        in_specs=in_specs, out_specs=_row_block(o_shape, tm), out_shape=o_shape,
        compiler_params=_row_params(),
    )(*args)


def back_norm(a, w, h, g, dres, *, name, tm, tk, bf16_copy=True, w_next=None, next_dtype=BF16, next_slabs=None,
              after=None):
    T, K = _mat_shape(a)
    D = h.shape[1]
    with_dh = dres is not None
    chained = w_next is not None
    n_in = 4 + with_dh + chained
    tn = 4 * HEAD_W

    def body(*refs):
        a_ref, w_ref, h_ref, g_ref = refs[:4]
        outs = refs[n_in + (after is not None):]
        i = pl.program_id(0)
        if len(w_ref.shape) == 2:
            dn = _dot_nt(_all_cols(a_ref).astype(BF16), w_ref[...])
        else:
            dn = None
            for kc in range(K // tk):
                part = _dot_nt(_cols(a_ref, kc, tk).astype(BF16), _cols(w_ref, kc, tk))
                dn = part if dn is None else dn + part
        x = h_ref[...]
        r = _rms(x)
        xr = x * r
        dgp = jnp.sum(dn * xr, axis=0, keepdims=True)
        dg_ref = outs[-1]

        @pl.when(i == 0)
        def _():
            dg_ref[...] = dgp

        @pl.when(i > 0)
        def _():
            dg_ref[...] += dgp

        if with_dh:
            dyg = dn * g_ref[...]
            out = refs[4][...] + r * (dyg - xr * jnp.mean(dyg * xr, axis=-1, keepdims=True))
            outs[0][...] = out
            outb = out.astype(BF16)
            if bf16_copy:
                outs[1][...] = outb
            if chained:
                wn_ref, nx_ref = refs[5], outs[-2]
                for c in range(wn_ref.shape[0] // tn):
                    _set_cols(nx_ref, c, tn, _dot_nt(outb, wn_ref[c * tn:(c + 1) * tn, :]).astype(next_dtype))

    row = pl.BlockSpec((tm, D), lambda i: (i, 0))
    vec = pl.BlockSpec((1, D), lambda i: (0, 0))
    in_specs, args = [_row_block(a, tm), _resident(w), row, vec], [a, w, h, g]
    out_specs, out_shape = [], []
    if with_dh:
        in_specs.append(row)
        args.append(dres)
        out_specs.append(row)
        out_shape.append(jax.ShapeDtypeStruct((T, D), F32))
        if bf16_copy:
            out_specs.append(row)
            out_shape.append(jax.ShapeDtypeStruct((T, D), BF16))
    if chained:
        in_specs.append(_resident(w_next))
        args.append(w_next)
        nx_shape = _out_struct(T, w_next.shape[0], next_slabs, next_dtype)
        out_specs.append(_row_block(nx_shape, tm))
        out_shape.append(nx_shape)
    out_specs.append(vec)
    out_shape.append(jax.ShapeDtypeStruct((1, D), F32))
    if after is not None:
        in_specs.append(_anchor_spec(after))
        args.append(after)
    outs = pl.pallas_call(
        body, name=name, grid=(T // tm,),
        in_specs=in_specs, out_specs=out_specs, out_shape=out_shape,
        compiler_params=_row_params(),
    )(*args)
    return outs if len(outs) > 1 else outs[0]


def wgrad(a, b, *, name, tt, tn, out_slabs=None):
    T, K = _mat_shape(a)
    N = _mat_shape(b)[1]
    nt = T // tt
    o_shape = _out_struct(K, N, out_slabs, BF16)

    flipped = K > N and out_slabs is None

    def body(a_ref, b_ref, o_ref, acc_ref):
        t = pl.program_id(0)

        @pl.when(t == 0)
        def _():
            acc_ref[...] = jnp.zeros_like(acc_ref)

        if flipped:
            bt = _all_cols(b_ref).astype(BF16).T
            for c in range(K // tn):
                acc_ref[:, c * tn:(c + 1) * tn] += _dot(bt, _cols(a_ref, c, tn).astype(BF16))
        else:
            at = _all_cols(a_ref).astype(BF16).T
            for c in range(N // tn):
                acc_ref[:, c * tn:(c + 1) * tn] += _dot(at, _cols(b_ref, c, tn).astype(BF16))

        @pl.when(t == nt - 1)
        def _():
            if flipped:
                for c in range(K // tn):
                    o_ref[c * tn:(c + 1) * tn, :] = acc_ref[:, c * tn:(c + 1) * tn].T.astype(BF16)
            else:
                for c in range(N // tn):
                    _set_cols(o_ref, c, tn, acc_ref[:, c * tn:(c + 1) * tn].astype(BF16))

    return pl.pallas_call(
        body, name=name, grid=(nt,),
        in_specs=[_row_block(a, tt), _row_block(b, tt)],
        out_specs=_resident(o_shape), out_shape=o_shape,
        scratch_shapes=[pltpu.VMEM((N, K) if flipped else (K, N), F32)],
        compiler_params=_row_params(),
    )(a, b)


def chunk_triangles(tm):
    r = lax.broadcasted_iota(jnp.int32, (tm, tm), 0)
    c = lax.broadcasted_iota(jnp.int32, (tm, tm), 1)
    same = (r // CHUNK) == (c // CHUNK)
    tri = jnp.stack([same & (c <= r), same & (c >= r)]).astype(F32)
    return tri.astype(BF16), tri


def _tri_spec(tm):
    return pl.BlockSpec((2, tm, tm), lambda g, s, i: (0, 0, 0))


def _chunk_row(v, r, nc):
    return jnp.concatenate([jnp.broadcast_to(v[c * CHUNK + r:c * CHUNK + r + 1], (CHUNK, v.shape[1]))
                            for c in range(nc)], axis=0)


def _block_diag(v, nc):
    chunk = lax.broadcasted_iota(jnp.int32, (v.shape[0], 1), 0) // CHUNK
    return jnp.concatenate([jnp.where(chunk == c, v, jnp.zeros_like(v)) for c in range(nc)], axis=1)


def _pool_windows_back(ext_ref, tm):
    n = tm + 32
    ext_ref[1, 8:n] = ext_ref[0, 8:n] + ext_ref[0, 7:n - 1]
    ext_ref[2, 16:n] = ext_ref[1, 16:n] + ext_ref[1, 14:n - 2]
    ext_ref[3, 24:n] = ext_ref[2, 24:n] + ext_ref[2, 20:n - 4]
    s2 = ext_ref[1, 32:n]
    s4 = ext_ref[2, 32:n]
    s8 = ext_ref[3, 32:n]
    s16 = s8 + ext_ref[3, 24:n - 8]
    return s2, s4, s8, s16


def _pool_windows_fwd(ext_ref, tm):
    n = tm + 32
    ext_ref[1, 0:n - 8] = ext_ref[0, 0:n - 8] + ext_ref[0, 1:n - 7]
    ext_ref[2, 0:n - 16] = ext_ref[1, 0:n - 16] + ext_ref[1, 2:n - 14]
    ext_ref[3, 0:n - 24] = ext_ref[2, 0:n - 24] + ext_ref[2, 4:n - 20]
    s2 = ext_ref[1, 0:tm]
    s4 = ext_ref[2, 0:tm]
    s8 = ext_ref[3, 0:tm]
    s16 = s8 + ext_ref[3, 8:tm + 8]
    return s2, s4, s8, s16


def _select_window(g, s2, s4, s8, s16):
    return jnp.where(g == 0, s2, jnp.where(g == 1, s4, jnp.where(g == 2, s8, s16)))


def _pool_count(g, pos):
    width = lax.shift_left(jnp.int32(2), g)
    return jnp.minimum(pos + 1, width).astype(F32)


def mixer_fwd(u5, pool_w_bf, scale4, theta4, gn4, tri_bf, tri_f, *, seqs, seq_len, tm):
    T = u5.shape[1]
    tps = seq_len // tm
    nc = tm // CHUNK
    W = HEAD_W

    H = HEADS_PER_STEP
    heads = range(H)

    def body(u_ref, pw_ref, sc_ref, th_ref, gn_ref, tri_ref, msk_ref, y_ref, o_ref, st_ref, halo_ref, ext_ref, s_ref):
        g = pl.program_id(0)
        i = pl.program_id(2)

        @pl.when(i == 0)
        def _():
            halo_ref[...] = jnp.zeros_like(halo_ref)
            s_ref[...] = jnp.zeros_like(s_ref)

        row = lax.broadcasted_iota(jnp.int32, (tm, 1), 0)
        cols = [slice(h * W, (h + 1) * W) for h in heads]

        pooled = []
        for h in heads:
            grp = g * H + h
            up = u_ref[0, :, cols[h]]
            ext_ref[h, 0, 0:16] = jnp.zeros((16, W), F32)
            ext_ref[h, 0, 16:32] = halo_ref[h]
            ext_ref[h, 0, 32:32 + tm] = up
            win = _select_window(grp, *_pool_windows_back(ext_ref.at[h], tm))
            pooled.append((win * (1.0 / _pool_count(grp, i * tm + row)) - up).astype(BF16))
            halo_ref[h] = up[tm - POOL_HALO:tm]
        mixed = [_dot(pooled[h], pw_ref[h]) for h in heads]
        for h in heads:
            y_ref[0, :, cols[h]] = (mixed[h] * sc_ref[h]).astype(BF16)

        zq, zf, zi, zg = u_ref[1], u_ref[2], u_ref[3], u_ref[4]
        th = [th_ref[h] for h in heads]
        lb = jnp.concatenate([_sigmoid(t[0:1, :] - t[1:2, :]) for t in th], axis=1)
        f = lb + (1.0 - lb) * _sigmoid(zf)
        kk = 1.0 - f
        q = zq * _sigmoid(zq)
        G = _tri_apply(tri_ref[0], jnp.log(f))
        Gm, Gl = _chunk_row(G, CHUNK // 2 - 1, nc), _chunk_row(G, CHUNK - 1, nc)
        vb = zi.astype(BF16)
        qrb = (q * jnp.exp(G - Gm)).astype(BF16)
        krb = (kk * jnp.exp(Gm - G)).astype(BF16)
        keb = (kk * jnp.exp(Gl - G)).astype(BF16)
        qgb = (q * jnp.exp(G)).astype(BF16)
        mask = msk_ref[0] > 0.5
        a = [jnp.where(mask, _dot_nt(qrb[:, cols[h]], krb[:, cols[h]]), 0.0).astype(BF16) for h in heads]
        d_st = [_dot_tn(vb[:, cols[h]], _block_diag(keb[:, cols[h]], nc)) for h in heads]
        o_intra = [_dot(a[h], vb[:, cols[h]]) for h in heads]
        st_cat = []
        for h in heads:
            st = s_ref[h]
            states = []
            for c in range(nc):
                states.append(st.astype(BF16))
                st_ref[c, h] = states[-1]
                st = st * jnp.exp(G[(c + 1) * CHUNK - 1:(c + 1) * CHUNK, cols[h]]) + d_st[h][:, c * W:(c + 1) * W]
            s_ref[h] = st
            st_cat.append(jnp.concatenate(states, axis=1))
        o = [o_intra[h] + _dot_nt(_block_diag(qgb[:, cols[h]], nc), st_cat[h]) for h in heads]
        gate = zg * _sigmoid(zg)
        for h in heads:
            o_ref[:, cols[h]] = o[h]
            r = lax.rsqrt(jnp.mean(o[h] * o[h], axis=-1, keepdims=True) + EPS)
            y_ref[1, :, cols[h]] = (o[h] * r * gn_ref[h] * gate[:, cols[h]]).astype(BF16)

    def rb(s, i):
        return s * tps + i

    def per_head(*shape):
        return pl.BlockSpec((H,) + shape, lambda g, s, i: (g,) + (0,) * len(shape))

    return pl.pallas_call(
        body, name="mixer_fwd", grid=(4 // H, seqs, tps),
        in_specs=[pl.BlockSpec((5, tm, H * W), lambda g, s, i: (0, rb(s, i), g)),
                  per_head(W, W), per_head(1, W), per_head(2, W), per_head(1, W),
                  _tri_spec(tm), _tri_spec(tm)],
        out_specs=[pl.BlockSpec((2, tm, H * W), lambda g, s, i: (0, rb(s, i), g)),
                   pl.BlockSpec((tm, H * W), lambda g, s, i: (rb(s, i), g)),
                   pl.BlockSpec((nc, H, W, W), lambda g, s, i: (rb(s, i), g, 0, 0))],
        out_shape=[jax.ShapeDtypeStruct((2, T, 4 * W), BF16),
                   jax.ShapeDtypeStruct((T, 4 * W), F32),
                   jax.ShapeDtypeStruct((T // CHUNK, 4, W, W), BF16)],
        scratch_shapes=[pltpu.VMEM((H, POOL_HALO, W), F32),
                        pltpu.VMEM((H, 4, tm + 32, W), F32),
                        pltpu.VMEM((H, W, W), F32)],
        compiler_params=_cparams(("arbitrary", "arbitrary", "arbitrary")),
    )(u5, pool_w_bf, scale4, theta4, gn4, tri_bf, tri_f)


def mixer_bwd(u5, dy2, o_pre, st_prev, pool_w_bf, scale4, theta4, gn4, tri_bf, tri_f, after, *, seqs, seq_len, tm):
    T = u5.shape[1]
    tps = seq_len // tm
    nc = tm // CHUNK
    W = HEAD_W
    hb = tm // POOL_HALO

    H = HEADS_PER_STEP
    heads = range(H)

    def body(u_ref, uh_ref, dy_ref, o_ref, st_ref, pw_ref, sc_ref, th_ref, gn_ref, tri_ref, msk_ref, _after_ref,
             du_ref, dpw_ref, dsc_ref, dlb_ref, dgn_ref, nxt_ref, ext_ref, ds_ref):
        g = pl.program_id(0)
        s = pl.program_id(1)
        i = pl.program_id(2)
        tile = tps - 1 - i
        first = (s == 0) & (i == 0)

        @pl.when(i == 0)
        def _():
            nxt_ref[...] = jnp.zeros_like(nxt_ref)
            ds_ref[...] = jnp.zeros_like(ds_ref)

        row = lax.broadcasted_iota(jnp.int32, (tm, 1), 0)
        cols = [slice(h * W, (h + 1) * W) for h in heads]

        def accumulate(ref, h, val):
            @pl.when(first)
            def _():
                ref[h] = val

            @pl.when(jnp.logical_not(first))
            def _():
                ref[h] += val

        def per_head(fn):
            return jnp.concatenate([jnp.broadcast_to(fn(cols[h]), (tm, W)) for h in heads], axis=1)

        inv_cnt, pb, dz = [], [], []
        for h in heads:
            grp = g * H + h
            inv_cnt.append(1.0 / _pool_count(grp, tile * tm + row))
            ext = ext_ref.at[h]
            up = u_ref[0, :, cols[h]]
            ext[0, 0:16] = jnp.zeros((16, W), F32)
            ext[0, 16:32] = jnp.where(tile == 0, 0.0, uh_ref[:, cols[h]])
            ext[0, 32:32 + tm] = up
            win = _select_window(grp, *_pool_windows_back(ext, tm))
            pb.append((win * inv_cnt[h] - up).astype(BF16))
            dz.append((dy_ref[0, :, cols[h]].astype(F32) * sc_ref[h]).astype(BF16))
        z = [_dot(pb[h], pw_ref[h]) for h in heads]
        dp = [_dot_nt(dz[h], pw_ref[h]) for h in heads]
        dpw = [_dot_tn(pb[h], dz[h]) for h in heads]
        for h in heads:
            accumulate(dsc_ref, h, jnp.sum(dy_ref[0, :, cols[h]].astype(F32) * z[h], axis=0, keepdims=True))
            accumulate(dpw_ref, h, dpw[h])
            ext = ext_ref.at[h]
            e = dp[h] * inv_cnt[h]
            ext[0, 0:tm] = e
            ext[0, tm:tm + 16] = nxt_ref[h]
            ext[0, tm + 16:tm + 32] = jnp.zeros((16, W), F32)
            lead = _select_window(g * H + h, *_pool_windows_fwd(ext, tm))
            nxt_ref[h] = e[0:POOL_HALO]
            du_ref[0, :, cols[h]] = (lead - dp[h]).astype(BF16)

        zq, zf, zi, zg = u_ref[1], u_ref[2], u_ref[3], u_ref[4]
        lb = jnp.concatenate([_sigmoid(th_ref[h][0:1, :] - th_ref[h][1:2, :]) for h in heads], axis=1)
        gn = jnp.concatenate([gn_ref[h] for h in heads], axis=1)
        sig, sq, sg = _sigmoid(zf), _sigmoid(zq), _sigmoid(zg)
        f = lb + (1.0 - lb) * sig
        kk = 1.0 - f
        q = zq * sq
        G = _tri_apply(tri_ref[0], jnp.log(f))

        dyh = dy_ref[1].astype(F32)
        o = o_ref[...]
        sqr = o * o
        r = per_head(lambda cs: lax.rsqrt(jnp.mean(sqr[:, cs], axis=-1, keepdims=True) + EPS))
        orr = o * r
        du_ref[4] = (dyh * (orr * gn) * (sg * (1.0 + zg * (1.0 - sg)))).astype(BF16)
        don = dyh * (zg * sg)
        dgn = jnp.sum(don * orr, axis=0, keepdims=True)
        dog = don * gn
        dog_orr = dog * orr
        do = r * (dog - orr * per_head(lambda cs: jnp.mean(dog_orr[:, cs], axis=-1, keepdims=True)))

        Gm, Gl = _chunk_row(G, CHUNK // 2 - 1, nc), _chunk_row(G, CHUNK - 1, nc)
        e_q, e_k, e_e, e_g = jnp.exp(G - Gm), jnp.exp(Gm - G), jnp.exp(Gl - G), jnp.exp(G)
        qr, kr, ke, qg = q * e_q, kk * e_k, kk * e_e, q * e_g
        qrb, krb, keb, qgb = qr.astype(BF16), kr.astype(BF16), ke.astype(BF16), qg.astype(BF16)
        vb = zi.astype(BF16)
        dob = do.astype(BF16)
        lower, upper = msk_ref[0] > 0.5, msk_ref[1] > 0.5
        da = [jnp.where(lower, _dot_nt(dob[:, cs], vb[:, cs]), 0.0).astype(BF16) for cs in cols]
        a_t = [jnp.where(upper, _dot_nt(krb[:, cs], qrb[:, cs]), 0.0).astype(BF16) for cs in cols]
        da_t = [jnp.where(upper, _dot_nt(vb[:, cs], dob[:, cs]), 0.0).astype(BF16) for cs in cols]
        u_cat = [_dot_tn(dob[:, cs], _block_diag(qgb[:, cs], nc)) for cs in cols]
        dqr = [_dot(da[h], krb[:, cols[h]]) for h in heads]
        dkr = [_dot(da_t[h], qrb[:, cols[h]]) for h in heads]
        dv = [_dot(a_t[h], dob[:, cols[h]]) for h in heads]
        dsn_rows, dsn_cols, ddecay = [], [], [[None] * H for _ in range(nc)]
        for h in heads:
            dsn = ds_ref[h]
            dsn_b = [None] * nc
            for c in reversed(range(nc)):
                decay = jnp.exp(G[(c + 1) * CHUNK - 1:(c + 1) * CHUNK, cols[h]])
                dsn_b[c] = dsn.astype(BF16)
                ddecay[c][h] = jnp.sum(dsn * st_ref[c, h].astype(F32), axis=0, keepdims=True) * decay
                dsn = u_cat[h][:, c * W:(c + 1) * W] + dsn * decay
            ds_ref[h] = dsn
            dsn_rows.append(jnp.concatenate(dsn_b, axis=0))
            dsn_cols.append(jnp.concatenate(dsn_b, axis=1))
        st_rows = [jnp.concatenate([st_ref[c, h] for c in range(nc)], axis=0) for h in heads]
        dqg = [_dot(_block_diag(dob[:, cols[h]], nc), st_rows[h]) for h in heads]
        dke = [_dot(_block_diag(vb[:, cols[h]], nc), dsn_rows[h]) for h in heads]
        dv = [dv[h] + _dot_nt(_block_diag(keb[:, cols[h]], nc), dsn_cols[h]) for h in heads]
        dqr, dkr, dqg, dke, dv = (jnp.concatenate(parts, axis=1) for parts in (dqr, dkr, dqg, dke, dv))
        t_mid, t_qg, t_ke = dkr * kr - dqr * qr, dqg * qg, dke * ke
        dq = dqr * e_q + dqg * e_g
        dk = dkr * e_k + dke * e_e
        crow = lax.broadcasted_iota(jnp.int32, (CHUNK, 1), 0)
        ends = []
        for c in range(nc):
            sl = slice(c * CHUNK, (c + 1) * CHUNK)
            dgm = jnp.sum(t_mid[sl], axis=0, keepdims=True)
            dgl = jnp.sum(t_ke[sl], axis=0, keepdims=True) + jnp.concatenate(ddecay[c], axis=1)
            ends.append(jnp.where(crow == CHUNK // 2 - 1, dgm, 0.0) + jnp.where(crow == CHUNK - 1, dgl, 0.0))
        dG = t_qg - t_ke - t_mid + jnp.concatenate(ends, axis=0)
        dlogf = _tri_apply(tri_ref[1], dG)
        df = dlogf / f - dk
        du_ref[1] = (dq * (sq * (1.0 + zq * (1.0 - sq)))).astype(BF16)
        du_ref[2] = (df * (1.0 - lb) * (sig * (1.0 - sig))).astype(BF16)
        du_ref[3] = dv.astype(BF16)
        dlb = jnp.sum(df * (1.0 - sig), axis=0, keepdims=True) * (lb * (1.0 - lb))
        for h in heads:
            accumulate(dgn_ref, h, dgn[:, cols[h]])
            accumulate(dlb_ref, h, dlb[:, cols[h]])

    def rb(s, i):
        return s * tps + (tps - 1 - i)

    def per_head_spec(*shape):
        return pl.BlockSpec((H,) + shape, lambda g, s, i: (g,) + (0,) * len(shape))

    vec, mat = per_head_spec(1, W), per_head_spec(W, W)
    return pl.pallas_call(
        body, name="mixer_bwd", grid=(4 // H, seqs, tps),
        in_specs=[pl.BlockSpec((5, tm, H * W), lambda g, s, i: (0, rb(s, i), g)),
                  pl.BlockSpec((None, POOL_HALO, H * W), lambda g, s, i: (0, jnp.maximum(rb(s, i) * hb - 1, 0), g)),
                  pl.BlockSpec((2, tm, H * W), lambda g, s, i: (0, rb(s, i), g)),
                  pl.BlockSpec((tm, H * W), lambda g, s, i: (rb(s, i), g)),
                  pl.BlockSpec((nc, H, W, W), lambda g, s, i: (rb(s, i), g, 0, 0)),
                  mat, vec, per_head_spec(2, W), vec, _tri_spec(tm), _tri_spec(tm),
                  pl.BlockSpec(after.shape, lambda g, s, i: (0, 0))],
        out_specs=[pl.BlockSpec((5, tm, H * W), lambda g, s, i: (0, rb(s, i), g)), mat, vec, vec, vec],
        out_shape=[jax.ShapeDtypeStruct((5, T, 4 * W), BF16),
                   jax.ShapeDtypeStruct((4, W, W), F32),
                   jax.ShapeDtypeStruct((4, 1, W), F32),
                   jax.ShapeDtypeStruct((4, 1, W), F32),
                   jax.ShapeDtypeStruct((4, 1, W), F32)],
        scratch_shapes=[pltpu.VMEM((H, POOL_HALO, W), F32),
                        pltpu.VMEM((H, 4, tm + 32, W), F32),
                        pltpu.VMEM((H, W, W), F32)],
        compiler_params=_cparams(("arbitrary", "arbitrary", "arbitrary")),
    )(u5, u5, dy2, o_pre, st_prev, pool_w_bf, scale4, theta4, gn4, tri_bf, tri_f, after)


def _attn_probs(q, k, hd):
    s = _dot_nt(q, k) * (1.0 / (hd ** 0.5))
    e = jnp.exp(s - jnp.max(s, axis=-1, keepdims=True))
    return e * (1.0 / jnp.sum(e, axis=-1, keepdims=True))


def attn_fwd(q, kv3, *, seqs, seq_len, n_mem, tm):
    T, D = q.shape
    hd = D // XATTN_HEADS
    tps = seq_len // tm

    cols = [slice(h * hd, (h + 1) * hd) for h in range(XATTN_HEADS)]

    def body(q_ref, kv_ref, o_ref):
        p = [_attn_probs(q_ref[:, cs], kv_ref[0, :, cs], hd) for cs in cols]
        for h, cs in enumerate(cols):
            o_ref[:, cs] = _dot(p[h].astype(BF16), kv_ref[1, :, cs]).astype(BF16)

    return pl.pallas_call(
        body, name="attn_fwd", grid=(seqs, tps),
        in_specs=[pl.BlockSpec((tm, D), lambda b, i: (b * tps + i, 0)),
                  pl.BlockSpec((2, n_mem, D), lambda b, i: (0, b, 0))],
        out_specs=pl.BlockSpec((tm, D), lambda b, i: (b * tps + i, 0)),
        out_shape=jax.ShapeDtypeStruct((T, D), BF16),
        compiler_params=_cparams(("parallel", "arbitrary")),
    )(q, kv3)


def attn_bwd(q, kv3, do, *, seqs, seq_len, n_mem, tm):
    T, D = q.shape
    hd = D // XATTN_HEADS
    tps = seq_len // tm

    cols = [slice(h * hd, (h + 1) * hd) for h in range(XATTN_HEADS)]

    def body(q_ref, kv_ref, do_ref, dq_ref, dkv_ref):
        i = pl.program_id(1)

        @pl.when(i == 0)
        def _():
            dkv_ref[...] = jnp.zeros_like(dkv_ref)

        p = [_attn_probs(q_ref[:, cs], kv_ref[0, :, cs], hd) for cs in cols]
        dp = [_dot_nt(do_ref[:, cs], kv_ref[1, :, cs]) for cs in cols]
        ds = [(p[h] * (dp[h] - jnp.sum(dp[h] * p[h], axis=-1, keepdims=True)) * (1.0 / (hd ** 0.5))).astype(BF16)
              for h in range(XATTN_HEADS)]
        for h, cs in enumerate(cols):
            dq_ref[:, cs] = _dot(ds[h], kv_ref[0, :, cs]).astype(BF16)
            dkv_ref[0, :, cs] += _dot_tn(ds[h], q_ref[:, cs])
            dkv_ref[1, :, cs] += _dot_tn(p[h].astype(BF16), do_ref[:, cs])

    qspec = pl.BlockSpec((tm, D), lambda b, i: (b * tps + i, 0))
    kvspec = pl.BlockSpec((2, n_mem, D), lambda b, i: (0, b, 0))
    return pl.pallas_call(
        body, name="attn_bwd", grid=(seqs, tps),
        in_specs=[qspec, kvspec, qspec],
        out_specs=[qspec, kvspec],
        out_shape=[jax.ShapeDtypeStruct((T, D), BF16), jax.ShapeDtypeStruct((2, seqs * n_mem, D), F32)],
        compiler_params=_cparams(("parallel", "arbitrary")),
    )(q, kv3, do)


def _my_place():
    return lax.axis_index("x"), lax.axis_index("y"), lax.axis_index("c")


def _slot_of(px, py, pc):
    return 4 * px + 2 * py + pc


def _peer(k, x, y, c):
    return (1 - x if (k >> 2) & 1 else x, 1 - y if (k >> 1) & 1 else y, 1 - c if k & 1 else c)


def _split_copies(src_refs, land_refs, send_sems, recv_sems, scatter):
    x, y, c = _my_place()
    mine = _slot_of(x, y, c)
    copies = []
    for a, (src, land) in enumerate(zip(src_refs, land_refs)):
        for k in range(1, N_DEV):
            peer = _peer(k, x, y, c)
            copies.append(pltpu.make_async_remote_copy(
                src_ref=src.at[_slot_of(*peer)] if scatter else src, dst_ref=land.at[mine],
                send_sem=send_sems.at[a * N_PEERS + k - 1], recv_sem=recv_sems.at[a * N_PEERS + k - 1],
                device_id=peer, device_id_type=MESH))
    return copies


def split_start(groups, *, name, scatter):
    sizes = [len(srcs) for srcs, _ in groups]
    n_arr = sum(sizes)
    flat = [a for srcs, lands in groups for a in list(srcs) + list(lands)]

    def body(*refs):
        ins = refs[:2 * n_arr]
        sems = refs[4 * n_arr:4 * n_arr + 2 * len(groups)]
        token = refs[-1]
        at = 0
        for gi, n in enumerate(sizes):
            for cp in _split_copies(ins[at:at + n], ins[at + n:at + 2 * n], sems[2 * gi], sems[2 * gi + 1], scatter):
                cp.start()
            at += 2 * n
        token[...] = jnp.zeros_like(token)

    sem_shapes = []
    for n in sizes:
        sem_shapes += [pltpu.SemaphoreType.DMA((n * N_PEERS,))] * 2
    outs = pl.pallas_call(
        body, name=name,
        out_shape=tuple(pltpu.HBM(a.shape, a.dtype) for a in flat) + tuple(sem_shapes)
        + (jax.ShapeDtypeStruct((8, 128), F32),),
        in_specs=(HBM,) * len(flat),
        out_specs=(HBM,) * len(flat) + (SEM,) * len(sem_shapes) + (pl.BlockSpec(memory_space=pltpu.VMEM),),
        input_output_aliases={i: i for i in range(len(flat))},
        compiler_params=pltpu.CompilerParams(has_side_effects=pltpu.SideEffectType.DATAFLOW_SIDE_EFFECTING),
    )(*[pltpu.with_memory_space_constraint(a, pltpu.HBM) for a in flat])
    thru, sems, token = outs[:len(flat)], outs[len(flat):-1], outs[-1]
    started, at = [], 0
    for gi, n in enumerate(sizes):
        started.append((sems[2 * gi], sems[2 * gi + 1], thru[at:at + n], thru[at + n:at + 2 * n]))
        at += 2 * n
    return started, token


def split_wait(started, after, *, name, scatter):
    sizes = [len(g[2]) for g in started]
    n_arr = sum(sizes)
    flat = [a for g in started for a in list(g[2]) + list(g[3])]
    sems = [s for g in started for s in g[:2]]

    def body(*refs):
        ins = refs[:2 * n_arr]
        sem_refs = refs[2 * n_arr:2 * n_arr + len(sems)]
        at = 0
        for gi, n in enumerate(sizes):
            for cp in _split_copies(ins[at:at + n], ins[at + n:at + 2 * n], sem_refs[2 * gi], sem_refs[2 * gi + 1], scatter):
                cp.wait_send()
                cp.wait_recv()
            at += 2 * n

    outs = pl.pallas_call(
        body, name=name,
        out_shape=tuple(pltpu.HBM(a.shape, a.dtype) for a in flat),
        in_specs=(HBM,) * len(flat) + (SEM,) * len(sems) + (pl.BlockSpec(memory_space=pl.ANY),),
        out_specs=(HBM,) * len(flat),
        input_output_aliases={i: i for i in range(len(flat))},
        compiler_params=pltpu.CompilerParams(has_side_effects=pltpu.SideEffectType.DATAFLOW_SIDE_EFFECTING),
    )(*flat, *sems, after)
    done, at = [], 0
    for n in sizes:
        done.append((outs[at:at + n], outs[at + n:at + 2 * n]))
        at += 2 * n
    return done


SIBLING = 1
CHIP_PEERS = (2, 4, 6)
_SIDE_EFFECTS = pltpu.CompilerParams(has_side_effects=pltpu.SideEffectType.DATAFLOW_SIDE_EFFECTING)


def _chip_level_copies(src, land, send_sems, recv_sems):
    x, y, c = _my_place()
    return [pltpu.make_async_remote_copy(
        src_ref=src, dst_ref=land.at[_slot_of(x, y, c)], send_sem=send_sems.at[j], recv_sem=recv_sems.at[j],
        device_id=_peer(k, x, y, c), device_id_type=MESH) for j, k in enumerate((SIBLING,) + CHIP_PEERS)]


def _pass_on_copies(land, send_sems, recv_sems, receiving):
    x, y, c = _my_place()
    copies = []
    for j, k in enumerate(CHIP_PEERS):
        slot = _slot_of(*_peer(k ^ SIBLING if receiving else k, x, y, c))
        copies.append(pltpu.make_async_remote_copy(
            src_ref=land.at[slot], dst_ref=land.at[slot], send_sem=send_sems.at[j], recv_sem=recv_sems.at[j],
            device_id=_peer(SIBLING, x, y, c), device_id_type=MESH))
    return copies


def gather2_start(src, land, *, name):
    def body(src_ref, land_ref, src_out, land_out, send_sems, recv_sems, token):
        for cp in _chip_level_copies(src_ref, land_ref, send_sems, recv_sems):
            cp.start()
        token[...] = jnp.zeros_like(token)

    n = 1 + len(CHIP_PEERS)
    src_t, land_t, send_sems, recv_sems, token = pl.pallas_call(
        body, name=name,
        out_shape=(pltpu.HBM(src.shape, src.dtype), pltpu.HBM(land.shape, land.dtype),
                   pltpu.SemaphoreType.DMA((n,)), pltpu.SemaphoreType.DMA((n,)), jax.ShapeDtypeStruct((8, 128), F32)),
        in_specs=(HBM, HBM), out_specs=(HBM, HBM, SEM, SEM, pl.BlockSpec(memory_space=pltpu.VMEM)),
        input_output_aliases={0: 0, 1: 1}, compiler_params=_SIDE_EFFECTS,
    )(pltpu.with_memory_space_constraint(src, pltpu.HBM), pltpu.with_memory_space_constraint(land, pltpu.HBM))
    return (src_t, land_t, send_sems, recv_sems), token


def gather2_pass_on(started, after, *, name):
    src, land, send_a, recv_a = started

    def body(src_ref, land_ref, send_a_ref, recv_a_ref, after_ref, land_out, send_b, recv_b):
        for cp in _chip_level_copies(src_ref, land_ref, send_a_ref, recv_a_ref):
            cp.wait_send()
            cp.wait_recv()
        for cp in _pass_on_copies(land_ref, send_b, recv_b, False):
            cp.start()

    n = len(CHIP_PEERS)
    land_t, send_b, recv_b = pl.pallas_call(
        body, name=name,
        out_shape=(pltpu.HBM(land.shape, land.dtype), pltpu.SemaphoreType.DMA((n,)), pltpu.SemaphoreType.DMA((n,))),
        in_specs=(HBM, HBM, SEM, SEM, pl.BlockSpec(memory_space=pl.ANY)), out_specs=(HBM, SEM, SEM),
        input_output_aliases={1: 0}, compiler_params=_SIDE_EFFECTS,
    )(src, land, send_a, recv_a, after)
    return land_t, send_b, recv_b


def gather2_wait(passed, *, name):
    land, send_b, recv_b = passed

    def body(land_ref, send_ref, recv_ref, land_out):
        for cp in _pass_on_copies(land_ref, send_ref, recv_ref, False):
            cp.wait_send()
        for cp in _pass_on_copies(land_ref, send_ref, recv_ref, True):
            cp.wait_recv()

    return pl.pallas_call(
        body, name=name, out_shape=pltpu.HBM(land.shape, land.dtype),
        in_specs=(HBM, SEM, SEM), out_specs=HBM,
        input_output_aliases={0: 0}, compiler_params=_SIDE_EFFECTS,
    )(land, send_b, recv_b)


def _adamw_math(g, w, m, v):
    c1 = 1.0 - ADAM_B1 ** ADAM_STEP
    c2 = 1.0 - ADAM_B2 ** ADAM_STEP
    nm = ADAM_B1 * m + (1.0 - ADAM_B1) * g
    nv = ADAM_B2 * v + (1.0 - ADAM_B2) * (g * g)
    delta = -ADAM_LR * ((nm / c1) / (jnp.sqrt(nv / c2) + ADAM_EPS) + ADAM_WD * w)
    return delta, nm, nv


def adamw_sharded(me, owns, recvs, ws, ms, vs, *, name, tr):
    n = len(ws)
    R, C = ws[0].shape

    def body(me_ref, *refs):
        for t in range(n):
            parts = refs[t * N_DEV:(t + 1) * N_DEV]
            w_ref, m_ref, v_ref = refs[n * N_DEV + 3 * t:n * N_DEV + 3 * t + 3]
            g_ref, d_ref, nm_ref, nv_ref = refs[n * (N_DEV + 3) + 4 * t:n * (N_DEV + 3) + 4 * t + 4]
            g = parts[0][...].astype(F32)
            for p in parts[1:]:
                g = g + p[...].astype(F32)
            g_ref[...] = g
            d_ref[...], nm_ref[...], nv_ref[...] = _adamw_math(g, w_ref[...], m_ref[...], v_ref[...])

    def slab(k):
        return pl.BlockSpec((None, tr, C), lambda i, me_ref: (me_ref[0] ^ k, i, 0))

    blk = pl.BlockSpec((tr, C), lambda i, me_ref: (i, 0))
    out = jax.ShapeDtypeStruct((R, C), F32)
    args = [me]
    for t in range(n):
        args += [owns[t]] + [recvs[t]] * N_PEERS
    for t in range(n):
        args += [ws[t], ms[t], vs[t]]
    outs = pl.pallas_call(
        body, name=name,
        grid_spec=pltpu.PrefetchScalarGridSpec(
            num_scalar_prefetch=1, grid=(R // tr,),
            in_specs=[slab(k) for _ in range(n) for k in range(N_DEV)] + [blk] * (3 * n),
            out_specs=[blk] * (4 * n)),
        out_shape=[out] * (4 * n),
        compiler_params=_cparams(("parallel",)),
    )(*args)
    return [outs[4 * t:4 * t + 4] for t in range(n)]


def adamw_replicated(parts, ws, ms, vs, rows):
    n_buf, n_par = len(parts), len(ws)

    def body(*refs):
        p_refs = refs[:n_buf]
        w_refs = refs[n_buf:n_buf + n_par]
        m_refs = refs[n_buf + n_par:n_buf + 2 * n_par]
        v_refs = refs[n_buf + 2 * n_par:n_buf + 3 * n_par]
        outs = refs[n_buf + 3 * n_par:]
        sums = []
        for p in p_refs:
            g = p[0]
            for s in range(1, N_DEV):
                g = g + p[s]
            sums.append(g)
        for j, (b, r0, nr) in enumerate(rows):
            g = sums[b][r0:r0 + nr]
            delta, nm, nv = _adamw_math(g, w_refs[j][...], m_refs[j][...], v_refs[j][...])
            outs[j][...] = g
            outs[n_par + j][...] = delta
            outs[2 * n_par + j][...] = nm
            outs[3 * n_par + j][...] = nv

    shapes = [jax.ShapeDtypeStruct(w.shape, F32) for w in ws]
    outs = pl.pallas_call(
        body, name="adamw_replicated", out_shape=shapes * 4,
        compiler_params=pltpu.CompilerParams(vmem_limit_bytes=V7X_VMEM_LIMIT),
    )(*parts, *ws, *ms, *vs)
    return outs[:n_par], outs[n_par:2 * n_par], outs[2 * n_par:3 * n_par], outs[3 * n_par:]


BIG = ("w_in", "w_out", "xw_q", "xw_kv", "xw_o", "w_up", "w_down")
ADAMW_CALLS = (("w_out", "xw_q", "xw_o"), ("xw_kv",), ("w_up",), ("w_down",), ("w_in",))
WEIGHTS = ("norm_mix", "w_in", "pool_w", "pool_scale", "lb_theta", "hgrn_norm", "w_out", "norm_xq",
           "norm_mem", "xw_q", "xw_kv", "xw_o", "norm_mlp", "w_up", "w_down", "norm_final")
SMALL = (("pool_w", (4 * HEAD_W, HEAD_W), 0, 0),
         ("norm_mix", (1, 1024), 1, 0), ("norm_xq", (1, 1024), 1, 1), ("norm_mem", (1, 1024), 1, 2),
         ("norm_mlp", (1, 1024), 1, 3), ("norm_final", (1, 1024), 1, 4),
         ("pool_scale", (1, 512), 2, 0), ("hgrn_norm", (1, 512), 2, 1), ("lb_theta", (2, 512), 2, 2))


def _pad_rows(a, rows):
    return jnp.concatenate([a, jnp.zeros((rows - a.shape[0], a.shape[1]), a.dtype)], axis=0)


def kernel(x, mem, norm_mix, w_in, pool_w, pool_scale, lb_theta, hgrn_norm, w_out, norm_xq, norm_mem, xw_q, xw_kv, xw_o, norm_mlp, w_up, w_down, norm_final, loss_target, m_norm_mix, m_w_in, m_pool_w, m_pool_scale, m_lb_theta, m_hgrn_norm, m_w_out, m_norm_xq, m_norm_mem, m_xw_q, m_xw_kv, m_xw_o, m_norm_mlp, m_w_up, m_w_down, m_norm_final, v_norm_mix, v_w_in, v_pool_w, v_pool_scale, v_lb_theta, v_hgrn_norm, v_w_out, v_norm_xq, v_norm_mem, v_xw_q, v_xw_kv, v_xw_o, v_norm_mlp, v_w_up, v_w_down, v_norm_final):
    w = dict(norm_mix=norm_mix, w_in=w_in, pool_w=pool_w, pool_scale=pool_scale, lb_theta=lb_theta,
             hgrn_norm=hgrn_norm, w_out=w_out, norm_xq=norm_xq, norm_mem=norm_mem, xw_q=xw_q, xw_kv=xw_kv,
             xw_o=xw_o, norm_mlp=norm_mlp, w_up=w_up, w_down=w_down, norm_final=norm_final)
    mom = dict(norm_mix=m_norm_mix, w_in=m_w_in, pool_w=m_pool_w, pool_scale=m_pool_scale, lb_theta=m_lb_theta,
               hgrn_norm=m_hgrn_norm, w_out=m_w_out, norm_xq=m_norm_xq, norm_mem=m_norm_mem, xw_q=m_xw_q,
               xw_kv=m_xw_kv, xw_o=m_xw_o, norm_mlp=m_norm_mlp, w_up=m_w_up, w_down=m_w_down,
               norm_final=m_norm_final)
    var = dict(norm_mix=v_norm_mix, w_in=v_w_in, pool_w=v_pool_w, pool_scale=v_pool_scale, lb_theta=v_lb_theta,
               hgrn_norm=v_hgrn_norm, w_out=v_w_out, norm_xq=v_norm_xq, norm_mem=v_norm_mem, xw_q=v_xw_q,
               xw_kv=v_xw_kv, xw_o=v_xw_o, norm_mlp=v_norm_mlp, w_up=v_w_up, w_down=v_w_down,
               norm_final=v_norm_final)

    seqs, seq_len, D = x.shape
    n_mem = mem.shape[1]
    T = seqs * seq_len
    W = HEAD_W
    x2 = x.reshape(T, D)
    mem2 = mem.reshape(seqs * n_mem, D)
    tgt2 = loss_target.reshape(T, D)
    tm_big = min(1024, T)
    tm_mid = min(512, T)
    tm_sq = min(1024, T)
    tm_mix = min(256, seq_len)
    tm_att = min(1024, seq_len)
    tkv = min(512, seqs * n_mem)
    px, py, pc = _my_place()
    me = _slot_of(px, py, pc).astype(jnp.int32)
    me1 = me.reshape(1)

    shard_bf = {n: w[n][0].astype(BF16) for n in BIG}

    def landing(n):
        zone = lax.empty((N_DEV,) + shard_bf[n].shape, BF16)
        return lax.dynamic_update_slice(zone, shard_bf[n][None], (me, 0, 0))

    w_in_started, tok = gather2_start(shard_bf["w_in"], landing("w_in"), name="w_in_gather_start")
    shard_bf["w_out"] = shard_bf["w_out"] + tok[0, 0].astype(BF16)
    ag_groups = (("w_out", "xw_q", "xw_kv", "xw_o"), ("w_up",), ("w_down",))
    ag_started, tok = split_start([([shard_bf[n] for n in grp], [landing(n) for n in grp]) for grp in ag_groups],
                                name="weights_gather_start", scatter=False)

    pool_w_bf = pool_w[0].astype(BF16)
    scale4 = pool_scale.reshape(4, 1, W)
    gn4 = hgrn_norm.reshape(4, 1, W)
    theta4 = lb_theta.reshape(2, 4, W).transpose(1, 0, 2)
    g_final = norm_final.reshape(1, D)

    n1 = prenorm(x2, norm_mix, tok, tm=tm_sq)
    wi3 = gather2_wait(gather2_pass_on(w_in_started, n1, name="w_in_gather_pass_on"), name="w_in_gather_wait")
    full_w_in = wi3.transpose(1, 0, 2).reshape(D, -1)
    u5 = proj_plain(n1, full_w_in, name="in_proj", tm=tm_sq, tn=4 * W, out_dtype=F32, out_slabs=5)
    tri_bf, tri_f = chunk_triangles(tm_mix)
    y2, o_pre, st_prev = mixer_fwd(u5, pool_w_bf, scale4, theta4, gn4, tri_bf, tri_f, seqs=seqs, seq_len=seq_len,
                                   tm=tm_mix)
    (_, (wo3, wq3, wkv3, wao3)), = split_wait(ag_started[0:1], y2, name="weights_gather_wait_attn", scatter=False)
    full_w_out, full_xw_q, full_xw_o = wo3.reshape(D, D), wq3.reshape(D, D), wao3.reshape(D, D)
    tn = 4 * W
    h1, n2, q = proj_res_norm(y2, full_w_out, x2, norm_xq, full_xw_q, name="out_q_proj", tm=tm_sq, tn=tn)
    tn_kv, tn_up = xw_kv.shape[2], w_up.shape[2]
    kv3, memn = proj_norm(mem2, norm_mem, wkv3, name="kv_proj", tm=tkv, tn=tn_kv, out_dtype=BF16, out_slabs=2)
    o_att = attn_fwd(q, kv3, seqs=seqs, seq_len=seq_len, n_mem=n_mem, tm=tm_att)
    h2, n3 = proj_res_norm(o_att, full_xw_o, h1, norm_mlp, name="attn_out_proj", tm=tm_sq, tn=tn)
    (_, (wup3,)), = split_wait(ag_started[1:2], h2, name="weights_gather_wait_up", scatter=False)
    aa, da = proj_plain(n3, wup3, name="up_proj", tm=tm_mid, tn=tn_up, relu2=True)
    (_, (wdn3,)), = split_wait(ag_started[2:3], aa, name="weights_gather_wait_down", scatter=False)
    full_w_down = wdn3.reshape(-1, D)
    dh3, dh3b, sq_err, dg_final = proj_res_loss(aa, full_w_down, h2, g_final, tgt2, name="down_proj_loss",
                                                tm=tm_mid, tn=tn)

    def send(parts, name):
        srcs = [p.reshape((N_DEV, -1, p.shape[-1])) for p in parts]
        lands = [lax.empty(s.shape, BF16) for s in srcs]
        started, token = split_start([(srcs, lands)], name=name, scatter=True)
        return started[0], token

    gw_down = wgrad(aa, dh3b, name="down_proj_wgrad", tt=tm_mid, tn=tn)
    dap = back_plain(dh3b, full_w_down, name="down_proj_bwd", tm=tm_mid, tn=tn, out_dtype=BF16, factor=da)
    gw_up = wgrad(n3, dap, name="up_proj_wgrad", tt=tm_mid, tn=tn_up, out_slabs=N_DEV)
    sent_mlp, tok = send([gw_down, gw_up], "grads_send_mlp")
    dh2, dh2b, do_att, dg_mlp = back_norm(dap, wup3, h2, norm_mlp, dh3, name="up_proj_bwd", tm=tm_mid, tk=tn_up,
                                          w_next=full_xw_o, after=tok)
    gxw_o = wgrad(o_att, dh2b, name="attn_out_proj_wgrad", tt=tm_sq, tn=tn)
    dq, dkv3 = attn_bwd(q, kv3, do_att, seqs=seqs, seq_len=seq_len, n_mem=n_mem, tm=tm_att)
    gxw_q = wgrad(n2, dq, name="q_proj_wgrad", tt=tm_sq, tn=tn)
    gxw_kv = wgrad(memn, dkv3, name="kv_proj_wgrad", tt=tkv, tn=tn_kv, out_slabs=N_DEV)
    dg_mem = back_norm(dkv3, wkv3, mem2, norm_mem, None, name="kv_proj_bwd", tm=tkv, tk=tn_kv)
    dh1, dh1b, dy2, dg_xq = back_norm(dq, full_xw_q, h1, norm_xq, dh2, name="q_proj_bwd", tm=tm_sq, tk=D,
                                      w_next=full_w_out, next_slabs=2)
    gw_out = wgrad(y2, dh1b, name="out_proj_wgrad", tt=tm_sq, tn=tn)
    sent_attn, tok = send([gxw_o, gxw_q, gxw_kv, gw_out], "grads_send_attn")
    du5, dpw, dsc, dlb, dgn = mixer_bwd(u5, dy2, o_pre, st_prev, pool_w_bf, scale4, theta4, gn4, tri_bf, tri_f, tok,
                                        seqs=seqs, seq_len=seq_len, tm=tm_mix)
    gw_in = wgrad(n1, du5, name="in_proj_wgrad", tt=tm_sq, tn=tn)
    gw_in_slots = gw_in.reshape(D, N_DEV, -1).transpose(1, 0, 2)
    sent_in, tok = send([gw_in_slots], "grads_send_in")
    dx, dg_mix = back_norm(du5, full_w_in, x2, norm_mix, dh1, name="in_proj_bwd", tm=tm_sq, tk=tn, bf16_copy=False,
                           after=tok)

    dlb_row = dlb.reshape(1, 4 * W)
    buf_vec = _pad_rows(jnp.concatenate([dg_mix, dg_xq, dg_mem, dg_mlp, dg_final, sq_err], axis=0), 8)
    buf_half = _pad_rows(jnp.concatenate([dsc.reshape(1, 4 * W), dgn.reshape(1, 4 * W), dlb_row, -dlb_row], axis=0), 8)
    small_src = [dpw.reshape(4 * W, W), buf_vec, buf_half]
    small_land = [lax.dynamic_update_slice(lax.empty((N_DEV,) + b.shape, F32), b[None], (me, 0, 0))
                  for b in small_src]
    small_started, tok = split_start([(small_src, small_land)], name="small_grads_start", scatter=False)

    done = split_wait([sent_mlp, sent_attn], tok, name="grads_wait", scatter=True)
    slots = dict(w_down=(0, 0), w_up=(0, 1), xw_o=(1, 0), xw_q=(1, 1), xw_kv=(1, 2), w_out=(1, 3))
    own = {n: done[gi][0][ai] for n, (gi, ai) in slots.items()}
    got = {n: done[gi][1][ai] for n, (gi, ai) in slots.items()}
    res = {}
    for names in ADAMW_CALLS:
        if names == ("w_in",):
            ((own["w_in"],), (got["w_in"],)), = split_wait([sent_in], res["g", "w_down"], name="grads_wait_in",
                                                           scatter=True)
        shp = w[names[0]].shape
        r = adamw_sharded(me1, [own[n] for n in names], [got[n] for n in names], [w[n][0] for n in names],
                          [mom[n][0] for n in names], [var[n][0] for n in names], name="adamw_" + names[0],
                          tr=min(256, shp[1]))
        for n, outs in zip(names, r):
            for kind, a in zip("gdmv", outs):
                res[kind, n] = a.reshape(shp)
    (_, small_parts), = split_wait(small_started, res["g", "w_in"], name="small_grads_wait", scatter=False)
    loss = 0.5 * jnp.sum(small_parts[1][:, 5, :]) / D
    r = adamw_replicated(small_parts, [w[n].reshape(v2) for n, v2, _, _ in SMALL],
                         [mom[n].reshape(v2) for n, v2, _, _ in SMALL],
                         [var[n].reshape(v2) for n, v2, _, _ in SMALL],
                         [(b, r0, v2[0]) for _, v2, b, r0 in SMALL])
    for kind, arrs in zip("gdmv", r):
        for (n, _, _, _), a in zip(SMALL, arrs):
            res[kind, n] = a.reshape(w[n].shape)

    out = [loss, dx.reshape(x.shape)]
    for kind in "gdmv":
        out += [res[kind, n] for n in WEIGHTS]
    return tuple(out)
```

```python
import jax
import jax.numpy as jnp
from jax import lax
from jax.experimental import pallas as pl
from jax.experimental.pallas import tpu as pltpu

F32 = jnp.float32
BF16 = jnp.bfloat16
EPS = 1e-6
CHUNK = 64
POOL_HALO = 16
HEAD_W = 128
HEADS_PER_STEP = 4
XATTN_HEADS = 4
N_DEV = 8
N_PEERS = N_DEV - 1
ADAM_LR = 0.001
ADAM_B1 = 0.9
ADAM_B2 = 0.999
ADAM_EPS = 1e-08
ADAM_WD = 0.01
ADAM_STEP = 10
V7X_VMEM_LIMIT = 52 * 1024 * 1024
MESH = pl.DeviceIdType.MESH
HBM = pl.BlockSpec(memory_space=pltpu.HBM)
SEM = pl.BlockSpec(memory_space=pltpu.SEMAPHORE)


def _cparams(dims):
    return pltpu.CompilerParams(dimension_semantics=dims, vmem_limit_bytes=V7X_VMEM_LIMIT)


def _sigmoid(v):
    return 0.5 * jnp.tanh(0.5 * v) + 0.5


def _dot(a, b):
    return jnp.dot(a, b, preferred_element_type=F32)


def _dot_nt(a, b):
    return lax.dot_general(a, b, (((1,), (1,)), ((), ())), preferred_element_type=F32)


def _dot_tn(a, b):
    return lax.dot_general(a, b, (((0,), (0,)), ((), ())), preferred_element_type=F32)


def _tri_apply(tri, v):
    hi = v.astype(BF16)
    lo = (v - hi.astype(F32)).astype(BF16)
    return _dot(tri, hi) + _dot(tri, lo)


def _mat_shape(a):
    return a.shape if a.ndim == 2 else (a.shape[1], a.shape[0] * a.shape[2])


def _out_struct(rows, n, slabs, dtype):
    return jax.ShapeDtypeStruct((rows, n) if slabs is None else (slabs, rows, n // slabs), dtype)


def _resident(a):
    nd = a.ndim
    return pl.BlockSpec(a.shape, lambda i: (0,) * nd, pipeline_mode=pl.Buffered(1))


def _row_block(a, tm):
    if a.ndim == 2:
        return pl.BlockSpec((tm, a.shape[1]), lambda i: (i, 0))
    return pl.BlockSpec((a.shape[0], tm, a.shape[2]), lambda i: (0, i, 0))


def _cols(ref, c, width):
    if len(ref.shape) == 2:
        return ref[:, c * width:(c + 1) * width]
    per = ref.shape[2] // width
    if per == 1:
        return ref[c]
    return ref[c // per, :, (c % per) * width:(c % per + 1) * width]


def _set_cols(ref, c, width, val):
    if len(ref.shape) == 2:
        ref[:, c * width:(c + 1) * width] = val
        return
    per = ref.shape[2] // width
    if per == 1:
        ref[c] = val
    else:
        ref[c // per, :, (c % per) * width:(c % per + 1) * width] = val


def _all_cols(ref):
    if len(ref.shape) == 2:
        return ref[...]
    return jnp.concatenate([ref[s] for s in range(ref.shape[0])], axis=1)


def _rms(x):
    return lax.rsqrt(jnp.mean(x * x, axis=-1, keepdims=True) + EPS)


def _row_params():
    return _cparams(("arbitrary",))


def proj_norm(h, g, w, *, name, tm, tn, out_dtype, out_slabs=None):
    T, D = h.shape
    N = _mat_shape(w)[1]
    o_shape = _out_struct(T, N, out_slabs, out_dtype)

    def body(h_ref, g_ref, w_ref, o_ref, n_ref):
        x = h_ref[...]
        n = (x * _rms(x) * g_ref[...]).astype(BF16)
        n_ref[...] = n
        for c in range(N // tn):
            _set_cols(o_ref, c, tn, _dot(n, _cols(w_ref, c, tn)).astype(out_dtype))

    return pl.pallas_call(
        body, name=name, grid=(T // tm,),
        in_specs=[_row_block(h, tm), pl.BlockSpec((1, D), lambda i: (0, 0)), _resident(w)],
        out_specs=[_row_block(o_shape, tm), pl.BlockSpec((tm, D), lambda i: (i, 0))],
        out_shape=[o_shape, jax.ShapeDtypeStruct((T, D), BF16)],
        compiler_params=_row_params(),
    )(h, g, w)


def prenorm(h, g, after, *, tm):
    T, D = h.shape

    def body(h_ref, g_ref, _after_ref, n_ref):
        x = h_ref[...]
        n_ref[...] = (x * _rms(x) * g_ref[...]).astype(BF16)

    row = pl.BlockSpec((tm, D), lambda i: (i, 0))
    return pl.pallas_call(
        body, name="prenorm", grid=(T // tm,),
        in_specs=[row, pl.BlockSpec((1, D), lambda i: (0, 0)), _anchor_spec(after)],
        out_specs=row, out_shape=jax.ShapeDtypeStruct((T, D), BF16),
        compiler_params=_row_params(),
    )(h, g, after)


def proj_plain(a, w, *, name, tm, tn, out_dtype=BF16, out_slabs=None, relu2=False):
    T = a.shape[0]
    N = _mat_shape(w)[1]

    def body(a_ref, w_ref, o_ref):
        av = a_ref[...]
        for c in range(N // tn):
            z = _dot(av, _cols(w_ref, c, tn))
            if relu2:
                z = jnp.maximum(z, 0.0)
                z = z * z
            _set_cols(o_ref, c, tn, z.astype(out_dtype))

    o_shape = _out_struct(T, N, out_slabs, out_dtype)
    return pl.pallas_call(
        body, name=name, grid=(T // tm,),
        in_specs=[_row_block(a, tm), _resident(w)],
        out_specs=_row_block(o_shape, tm), out_shape=o_shape,
        compiler_params=_row_params(),
    )(a, w)


def proj_res_norm(a, w, res, g, w_next=None, *, name, tm, tn):
    T = res.shape[0]
    D = w.shape[1]
    chained = w_next is not None

    def body(*refs):
        a_ref, w_ref, r_ref, g_ref = refs[:4]
        h_ref, n_ref = refs[4 + chained], refs[5 + chained]
        av = _all_cols(a_ref)
        for c in range(D // tn):
            sl = slice(c * tn, (c + 1) * tn)
            h_ref[:, sl] = r_ref[:, sl] + _dot(av, w_ref[:, sl])
        hv = h_ref[...]
        n = (hv * _rms(hv) * g_ref[...]).astype(BF16)
        n_ref[...] = n
        if chained:
            for c in range(D // tn):
                sl = slice(c * tn, (c + 1) * tn)
                refs[-1][:, sl] = _dot(n, refs[4][:, sl]).astype(BF16)

    row = pl.BlockSpec((tm, D), lambda i: (i, 0))
    half = jax.ShapeDtypeStruct((T, D), BF16)
    return pl.pallas_call(
        body, name=name, grid=(T // tm,),
        in_specs=[_row_block(a, tm), _resident(w), row, pl.BlockSpec((1, D), lambda i: (0, 0))]
        + ([_resident(w_next)] if chained else []),
        out_specs=[row, row] + ([row] if chained else []),
        out_shape=[jax.ShapeDtypeStruct((T, D), F32), half] + ([half] if chained else []),
        compiler_params=_row_params(),
    )(*([a, w, res, g] + ([w_next] if chained else [])))


def proj_res_loss(a, w, res, g, target, *, name, tm, tn):
    T = res.shape[0]
    D = w.shape[1]

    def body(a_ref, w_ref, r_ref, g_ref, t_ref, dh_ref, dhb_ref, ls_ref, dg_ref):
        i = pl.program_id(0)
        gv = g_ref[...]
        ls, dg = 0.0, 0.0
        halves = [slice(s * (tm // 2), (s + 1) * (tm // 2)) for s in range(2)]
        for rows in halves:
            av = a_ref[rows, :]
            for c in range(D // tn):
                sl = slice(c * tn, (c + 1) * tn)
                dh_ref[rows, sl] = r_ref[rows, sl] + _dot(av, w_ref[:, sl])
        for rows in halves:
            x = dh_ref[rows, :]
            r = _rms(x)
            xr = x * r
            d = xr * gv - t_ref[rows, :]
            dy = d * (1.0 / D)
            dyg = dy * gv
            dx = r * (dyg - xr * jnp.mean(dyg * xr, axis=-1, keepdims=True))
            dh_ref[rows, :] = dx
            dhb_ref[rows, :] = dx.astype(BF16)
            ls = ls + jnp.sum(d * d, axis=0, keepdims=True)
            dg = dg + jnp.sum(dy * xr, axis=0, keepdims=True)

        @pl.when(i == 0)
        def _():
            ls_ref[...] = ls
            dg_ref[...] = dg

        @pl.when(i > 0)
        def _():
            ls_ref[...] += ls
            dg_ref[...] += dg

    row = pl.BlockSpec((tm, D), lambda i: (i, 0))
    vec = pl.BlockSpec((1, D), lambda i: (0, 0))
    return pl.pallas_call(
        body, name=name, grid=(T // tm,),
        in_specs=[_row_block(a, tm), _resident(w), row, vec, row],
        out_specs=[row, row, vec, vec],
        out_shape=[jax.ShapeDtypeStruct((T, D), F32), jax.ShapeDtypeStruct((T, D), BF16),
                   jax.ShapeDtypeStruct((1, D), F32), jax.ShapeDtypeStruct((1, D), F32)],
        compiler_params=_row_params(),
    )(a, w, res, g, target)


def _anchor_spec(after):
    return pl.BlockSpec(after.shape, lambda i: (0, 0))


def back_plain(a, w, *, name, tm, tn, out_dtype, out_slabs=None, relu2_value=None, after=None):
    T = a.shape[0]
    N = w.shape[0]
    has_z = relu2_value is not None
    o_shape = _out_struct(T, N, out_slabs, out_dtype)

    def body(*refs):
        a_ref, w_ref = refs[0], refs[1]
        o_ref = refs[-1]
        av = a_ref[...]
        for c in range(N // tn):
            out = _dot_nt(av, w_ref[c * tn:(c + 1) * tn, :])
            if has_z:
                out = out * (2.0 * jnp.sqrt(refs[2][:, c * tn:(c + 1) * tn]).astype(F32))
            _set_cols(o_ref, c, tn, out.astype(out_dtype))

    in_specs, args = [_row_block(a, tm), _resident(w)], [a, w]
    if has_z:
        in_specs.append(_row_block(relu2_value, tm))
        args.append(relu2_value)
    if after is not None:
        in_specs.append(_anchor_spec(after))
        args.append(after)
    return pl.pallas_call(
        body, name=name, grid=(T // tm,),
        in_specs=in_specs, out_specs=_row_block(o_shape, tm), out_shape=o_shape,
        compiler_params=_row_params(),
    )(*args)


def back_norm(a, w, h, g, dres, *, name, tm, tk, bf16_copy=True, w_next=None, next_dtype=BF16, next_slabs=None,
              after=None):
    T, K = _mat_shape(a)
    D = h.shape[1]
    with_dh = dres is not None
    chained = w_next is not None
    n_in = 4 + with_dh + chained
    tn = 4 * HEAD_W

    def body(*refs):
        a_ref, w_ref, h_ref, g_ref = refs[:4]
        outs = refs[n_in + (after is not None):]
        i = pl.program_id(0)
        if len(w_ref.shape) == 2:
            dn = _dot_nt(_all_cols(a_ref).astype(BF16), w_ref[...])
        else:
            dn = None
            for kc in range(K // tk):
                part = _dot_nt(_cols(a_ref, kc, tk).astype(BF16), _cols(w_ref, kc, tk))
                dn = part if dn is None else dn + part
        x = h_ref[...]
        r = _rms(x)
        xr = x * r
        dgp = jnp.sum(dn * xr, axis=0, keepdims=True)
        dg_ref = outs[-1]

        @pl.when(i == 0)
        def _():
            dg_ref[...] = dgp

        @pl.when(i > 0)
        def _():
            dg_ref[...] += dgp

        if with_dh:
            dyg = dn * g_ref[...]
            out = refs[4][...] + r * (dyg - xr * jnp.mean(dyg * xr, axis=-1, keepdims=True))
            outs[0][...] = out
            outb = out.astype(BF16)
            if bf16_copy:
                outs[1][...] = outb
            if chained:
                wn_ref, nx_ref = refs[5], outs[-2]
                for c in range(wn_ref.shape[0] // tn):
                    _set_cols(nx_ref, c, tn, _dot_nt(outb, wn_ref[c * tn:(c + 1) * tn, :]).astype(next_dtype))

    row = pl.BlockSpec((tm, D), lambda i: (i, 0))
    vec = pl.BlockSpec((1, D), lambda i: (0, 0))
    in_specs, args = [_row_block(a, tm), _resident(w), row, vec], [a, w, h, g]
    out_specs, out_shape = [], []
    if with_dh:
        in_specs.append(row)
        args.append(dres)
        out_specs.append(row)
        out_shape.append(jax.ShapeDtypeStruct((T, D), F32))
        if bf16_copy:
            out_specs.append(row)
            out_shape.append(jax.ShapeDtypeStruct((T, D), BF16))
    if chained:
        in_specs.append(_resident(w_next))
        args.append(w_next)
        nx_shape = _out_struct(T, w_next.shape[0], next_slabs, next_dtype)
        out_specs.append(_row_block(nx_shape, tm))
        out_shape.append(nx_shape)
    out_specs.append(vec)
    out_shape.append(jax.ShapeDtypeStruct((1, D), F32))
    if after is not None:
        in_specs.append(_anchor_spec(after))
        args.append(after)
    outs = pl.pallas_call(
        body, name=name, grid=(T // tm,),
        in_specs=in_specs, out_specs=out_specs, out_shape=out_shape,
        compiler_params=_row_params(),
    )(*args)
    return outs if len(outs) > 1 else outs[0]


def wgrad(a, b, *, name, tt, tn, out_slabs=None):
    T, K = _mat_shape(a)
    N = _mat_shape(b)[1]
    nt = T // tt
    o_shape = _out_struct(K, N, out_slabs, BF16)

    flipped = K > N and out_slabs is None

    def body(a_ref, b_ref, o_ref, acc_ref):
        t = pl.program_id(0)

        @pl.when(t == 0)
        def _():
            acc_ref[...] = jnp.zeros_like(acc_ref)

        if flipped:
            bt = _all_cols(b_ref).astype(BF16).T
            for c in range(K // tn):
                acc_ref[:, c * tn:(c + 1) * tn] += _dot(bt, _cols(a_ref, c, tn).astype(BF16))
        else:
            at = _all_cols(a_ref).astype(BF16).T
            for c in range(N // tn):
                acc_ref[:, c * tn:(c + 1) * tn] += _dot(at, _cols(b_ref, c, tn).astype(BF16))

        @pl.when(t == nt - 1)
        def _():
            if flipped:
                for c in range(K // tn):
                    o_ref[c * tn:(c + 1) * tn, :] = acc_ref[:, c * tn:(c + 1) * tn].T.astype(BF16)
            else:
                for c in range(N // tn):
                    _set_cols(o_ref, c, tn, acc_ref[:, c * tn:(c + 1) * tn].astype(BF16))

    return pl.pallas_call(
        body, name=name, grid=(nt,),
        in_specs=[_row_block(a, tt), _row_block(b, tt)],
        out_specs=_resident(o_shape), out_shape=o_shape,
        scratch_shapes=[pltpu.VMEM((N, K) if flipped else (K, N), F32)],
        compiler_params=_row_params(),
    )(a, b)


def chunk_triangles(tm):
    r = lax.broadcasted_iota(jnp.int32, (tm, tm), 0)
    c = lax.broadcasted_iota(jnp.int32, (tm, tm), 1)
    same = (r // CHUNK) == (c // CHUNK)
    tri = jnp.stack([same & (c <= r), same & (c >= r)]).astype(F32)
    return tri.astype(BF16), tri


def _tri_spec(tm):
    return pl.BlockSpec((2, tm, tm), lambda g, s, i: (0, 0, 0))


def _chunk_row(v, r, nc):
    return jnp.concatenate([jnp.broadcast_to(v[c * CHUNK + r:c * CHUNK + r + 1], (CHUNK, v.shape[1]))
                            for c in range(nc)], axis=0)


def _block_diag(v, nc):
    chunk = lax.broadcasted_iota(jnp.int32, (v.shape[0], 1), 0) // CHUNK
    return jnp.concatenate([jnp.where(chunk == c, v, jnp.zeros_like(v)) for c in range(nc)], axis=1)


def _pool_windows_back(ext_ref, tm):
    n = tm + 32
    ext_ref[1, 8:n] = ext_ref[0, 8:n] + ext_ref[0, 7:n - 1]
    ext_ref[2, 16:n] = ext_ref[1, 16:n] + ext_ref[1, 14:n - 2]
    ext_ref[3, 24:n] = ext_ref[2, 24:n] + ext_ref[2, 20:n - 4]
    s2 = ext_ref[1, 32:n]
    s4 = ext_ref[2, 32:n]
    s8 = ext_ref[3, 32:n]
    s16 = s8 + ext_ref[3, 24:n - 8]
    return s2, s4, s8, s16


def _pool_windows_fwd(ext_ref, tm):
    n = tm + 32
    ext_ref[1, 0:n - 8] = ext_ref[0, 0:n - 8] + ext_ref[0, 1:n - 7]
    ext_ref[2, 0:n - 16] = ext_ref[1, 0:n - 16] + ext_ref[1, 2:n - 14]
    ext_ref[3, 0:n - 24] = ext_ref[2, 0:n - 24] + ext_ref[2, 4:n - 20]
    s2 = ext_ref[1, 0:tm]
    s4 = ext_ref[2, 0:tm]
    s8 = ext_ref[3, 0:tm]
    s16 = s8 + ext_ref[3, 8:tm + 8]
    return s2, s4, s8, s16


def _select_window(g, s2, s4, s8, s16):
    return jnp.where(g == 0, s2, jnp.where(g == 1, s4, jnp.where(g == 2, s8, s16)))


def _pool_count(g, pos):
    width = lax.shift_left(jnp.int32(2), g)
    return jnp.minimum(pos + 1, width).astype(F32)


def mixer_fwd(u5, pool_w_bf, scale4, theta4, gn4, tri_bf, tri_f, *, seqs, seq_len, tm):
    T = u5.shape[1]
    tps = seq_len // tm
    nc = tm // CHUNK
    W = HEAD_W

    H = HEADS_PER_STEP
    heads = range(H)

    def body(u_ref, pw_ref, sc_ref, th_ref, gn_ref, tri_ref, msk_ref, y_ref, o_ref, st_ref, halo_ref, ext_ref, s_ref):
        g = pl.program_id(0)
        i = pl.program_id(2)

        @pl.when(i == 0)
        def _():
            halo_ref[...] = jnp.zeros_like(halo_ref)
            s_ref[...] = jnp.zeros_like(s_ref)

        row = lax.broadcasted_iota(jnp.int32, (tm, 1), 0)
        cols = [slice(h * W, (h + 1) * W) for h in heads]

        pooled = []
        for h in heads:
            grp = g * H + h
            up = u_ref[0, :, cols[h]]
            ext_ref[h, 0, 0:16] = jnp.zeros((16, W), F32)
            ext_ref[h, 0, 16:32] = halo_ref[h]
            ext_ref[h, 0, 32:32 + tm] = up
            win = _select_window(grp, *_pool_windows_back(ext_ref.at[h], tm))
            pooled.append((win * (1.0 / _pool_count(grp, i * tm + row)) - up).astype(BF16))
            halo_ref[h] = up[tm - POOL_HALO:tm]
        mixed = [_dot(pooled[h], pw_ref[h]) for h in heads]
        for h in heads:
            y_ref[0, :, cols[h]] = (mixed[h] * sc_ref[h]).astype(BF16)

        zq, zf, zi, zg = u_ref[1], u_ref[2], u_ref[3], u_ref[4]
        th = [th_ref[h] for h in heads]
        lb = jnp.concatenate([_sigmoid(t[0:1, :] - t[1:2, :]) for t in th], axis=1)
        f = lb + (1.0 - lb) * _sigmoid(zf)
        kk = 1.0 - f
        q = zq * _sigmoid(zq)
        G = _tri_apply(tri_ref[0], jnp.log(f))
        Gm, Gl = _chunk_row(G, CHUNK // 2 - 1, nc), _chunk_row(G, CHUNK - 1, nc)
        vb = zi.astype(BF16)
        qrb = (q * jnp.exp(G - Gm)).astype(BF16)
        krb = (kk * jnp.exp(Gm - G)).astype(BF16)
        keb = (kk * jnp.exp(Gl - G)).astype(BF16)
        qgb = (q * jnp.exp(G)).astype(BF16)
        mask = msk_ref[0] > 0.5
        a = [jnp.where(mask, _dot_nt(qrb[:, cols[h]], krb[:, cols[h]]), 0.0).astype(BF16) for h in heads]
        d_st = [_dot_tn(vb[:, cols[h]], _block_diag(keb[:, cols[h]], nc)) for h in heads]
        o_intra = [_dot(a[h], vb[:, cols[h]]) for h in heads]
        st_cat = []
        for h in heads:
            st = s_ref[h]
            states = []
            for c in range(nc):
                states.append(st.astype(BF16))
                st_ref[c, h] = states[-1]
                st = st * jnp.exp(G[(c + 1) * CHUNK - 1:(c + 1) * CHUNK, cols[h]]) + d_st[h][:, c * W:(c + 1) * W]
            s_ref[h] = st
            st_cat.append(jnp.concatenate(states, axis=1))
        o = [o_intra[h] + _dot_nt(_block_diag(qgb[:, cols[h]], nc), st_cat[h]) for h in heads]
        gate = zg * _sigmoid(zg)
        for h in heads:
            o_ref[:, cols[h]] = o[h]
            r = lax.rsqrt(jnp.mean(o[h] * o[h], axis=-1, keepdims=True) + EPS)
            y_ref[1, :, cols[h]] = (o[h] * r * gn_ref[h] * gate[:, cols[h]]).astype(BF16)

    def rb(s, i):
        return s * tps + i

    def per_head(*shape):
        return pl.BlockSpec((H,) + shape, lambda g, s, i: (g,) + (0,) * len(shape))

    return pl.pallas_call(
        body, name="mixer_fwd", grid=(4 // H, seqs, tps),
        in_specs=[pl.BlockSpec((5, tm, H * W), lambda g, s, i: (0, rb(s, i), g)),
                  per_head(W, W), per_head(1, W), per_head(2, W), per_head(1, W),
                  _tri_spec(tm), _tri_spec(tm)],
        out_specs=[pl.BlockSpec((2, tm, H * W), lambda g, s, i: (0, rb(s, i), g)),
                   pl.BlockSpec((tm, H * W), lambda g, s, i: (rb(s, i), g)),
                   pl.BlockSpec((nc, H, W, W), lambda g, s, i: (rb(s, i), g, 0, 0))],
        out_shape=[jax.ShapeDtypeStruct((2, T, 4 * W), BF16),
                   jax.ShapeDtypeStruct((T, 4 * W), F32),
                   jax.ShapeDtypeStruct((T // CHUNK, 4, W, W), BF16)],
        scratch_shapes=[pltpu.VMEM((H, POOL_HALO, W), F32),
                        pltpu.VMEM((H, 4, tm + 32, W), F32),
                        pltpu.VMEM((H, W, W), F32)],
        compiler_params=_cparams(("arbitrary", "arbitrary", "arbitrary")),
    )(u5, pool_w_bf, scale4, theta4, gn4, tri_bf, tri_f)


def mixer_bwd(u5, dy2, o_pre, st_prev, pool_w_bf, scale4, theta4, gn4, tri_bf, tri_f, after, *, seqs, seq_len, tm):
    T = u5.shape[1]
    tps = seq_len // tm
    nc = tm // CHUNK
    W = HEAD_W
    hb = tm // POOL_HALO

    H = HEADS_PER_STEP
    heads = range(H)

    def body(u_ref, uh_ref, dy_ref, o_ref, st_ref, pw_ref, sc_ref, th_ref, gn_ref, tri_ref, msk_ref, _after_ref,
             du_ref, dpw_ref, dsc_ref, dlb_ref, dgn_ref, nxt_ref, ext_ref, ds_ref):
        g = pl.program_id(0)
        s = pl.program_id(1)
        i = pl.program_id(2)
        tile = tps - 1 - i
        first = (s == 0) & (i == 0)

        @pl.when(i == 0)
        def _():
            nxt_ref[...] = jnp.zeros_like(nxt_ref)
            ds_ref[...] = jnp.zeros_like(ds_ref)

        row = lax.broadcasted_iota(jnp.int32, (tm, 1), 0)
        cols = [slice(h * W, (h + 1) * W) for h in heads]

        def accumulate(ref, h, val):
            @pl.when(first)
            def _():
                ref[h] = val

            @pl.when(jnp.logical_not(first))
            def _():
                ref[h] += val

        def per_head(fn):
            return jnp.concatenate([jnp.broadcast_to(fn(cols[h]), (tm, W)) for h in heads], axis=1)

        inv_cnt, pb, dz = [], [], []
        for h in heads:
            grp = g * H + h
            inv_cnt.append(1.0 / _pool_count(grp, tile * tm + row))
            ext = ext_ref.at[h]
            up = u_ref[0, :, cols[h]]
            ext[0, 0:16] = jnp.zeros((16, W), F32)
            ext[0, 16:32] = jnp.where(tile == 0, 0.0, uh_ref[:, cols[h]])
            ext[0, 32:32 + tm] = up
            win = _select_window(grp, *_pool_windows_back(ext, tm))
            pb.append((win * inv_cnt[h] - up).astype(BF16))
            dz.append((dy_ref[0, :, cols[h]].astype(F32) * sc_ref[h]).astype(BF16))
        z = [_dot(pb[h], pw_ref[h]) for h in heads]
        dp = [_dot_nt(dz[h], pw_ref[h]) for h in heads]
        dpw = [_dot_tn(pb[h], dz[h]) for h in heads]
        for h in heads:
            accumulate(dsc_ref, h, jnp.sum(dy_ref[0, :, cols[h]].astype(F32) * z[h], axis=0, keepdims=True))
            accumulate(dpw_ref, h, dpw[h])
            ext = ext_ref.at[h]
            e = dp[h] * inv_cnt[h]
            ext[0, 0:tm] = e
            ext[0, tm:tm + 16] = nxt_ref[h]
            ext[0, tm + 16:tm + 32] = jnp.zeros((16, W), F32)
            lead = _select_window(g * H + h, *_pool_windows_fwd(ext, tm))
            nxt_ref[h] = e[0:POOL_HALO]
            du_ref[0, :, cols[h]] = (lead - dp[h]).astype(BF16)

        zq, zf, zi, zg = u_ref[1], u_ref[2], u_ref[3], u_ref[4]
        lb = jnp.concatenate([_sigmoid(th_ref[h][0:1, :] - th_ref[h][1:2, :]) for h in heads], axis=1)
        gn = jnp.concatenate([gn_ref[h] for h in heads], axis=1)
        sig, sq, sg = _sigmoid(zf), _sigmoid(zq), _sigmoid(zg)
        f = lb + (1.0 - lb) * sig
        kk = 1.0 - f
        q = zq * sq
        G = _tri_apply(tri_ref[0], jnp.log(f))

        dyh = dy_ref[1].astype(F32)
        o = o_ref[...]
        sqr = o * o
        r = per_head(lambda cs: lax.rsqrt(jnp.mean(sqr[:, cs], axis=-1, keepdims=True) + EPS))
        orr = o * r
        du_ref[4] = (dyh * (orr * gn) * (sg * (1.0 + zg * (1.0 - sg)))).astype(BF16)
        don = dyh * (zg * sg)
        dgn = jnp.sum(don * orr, axis=0, keepdims=True)
        dog = don * gn
        dog_orr = dog * orr
        do = r * (dog - orr * per_head(lambda cs: jnp.mean(dog_orr[:, cs], axis=-1, keepdims=True)))

        Gm, Gl = _chunk_row(G, CHUNK // 2 - 1, nc), _chunk_row(G, CHUNK - 1, nc)
        e_q, e_k, e_e, e_g = jnp.exp(G - Gm), jnp.exp(Gm - G), jnp.exp(Gl - G), jnp.exp(G)
        qr, kr, ke, qg = q * e_q, kk * e_k, kk * e_e, q * e_g
        qrb, krb, keb, qgb = qr.astype(BF16), kr.astype(BF16), ke.astype(BF16), qg.astype(BF16)
        vb = zi.astype(BF16)
        dob = do.astype(BF16)
        lower, upper = msk_ref[0] > 0.5, msk_ref[1] > 0.5
        da = [jnp.where(lower, _dot_nt(dob[:, cs], vb[:, cs]), 0.0).astype(BF16) for cs in cols]
        a_t = [jnp.where(upper, _dot_nt(krb[:, cs], qrb[:, cs]), 0.0).astype(BF16) for cs in cols]
        da_t = [jnp.where(upper, _dot_nt(vb[:, cs], dob[:, cs]), 0.0).astype(BF16) for cs in cols]
        u_cat = [_dot_tn(dob[:, cs], _block_diag(qgb[:, cs], nc)) for cs in cols]
        dqr = [_dot(da[h], krb[:, cols[h]]) for h in heads]
        dkr = [_dot(da_t[h], qrb[:, cols[h]]) for h in heads]
        dv = [_dot(a_t[h], dob[:, cols[h]]) for h in heads]
        dsn_rows, dsn_cols, ddecay = [], [], [[None] * H for _ in range(nc)]
        for h in heads:
            dsn = ds_ref[h]
            dsn_b = [None] * nc
            for c in reversed(range(nc)):
                decay = jnp.exp(G[(c + 1) * CHUNK - 1:(c + 1) * CHUNK, cols[h]])
                dsn_b[c] = dsn.astype(BF16)
                ddecay[c][h] = jnp.sum(dsn * st_ref[c, h].astype(F32), axis=0, keepdims=True) * decay
                dsn = u_cat[h][:, c * W:(c + 1) * W] + dsn * decay
            ds_ref[h] = dsn
            dsn_rows.append(jnp.concatenate(dsn_b, axis=0))
            dsn_cols.append(jnp.concatenate(dsn_b, axis=1))
        st_rows = [jnp.concatenate([st_ref[c, h] for c in range(nc)], axis=0) for h in heads]
        dqg = [_dot(_block_diag(dob[:, cols[h]], nc), st_rows[h]) for h in heads]
        dke = [_dot(_block_diag(vb[:, cols[h]], nc), dsn_rows[h]) for h in heads]
        dv = [dv[h] + _dot_nt(_block_diag(keb[:, cols[h]], nc), dsn_cols[h]) for h in heads]
        dqr, dkr, dqg, dke, dv = (jnp.concatenate(parts, axis=1) for parts in (dqr, dkr, dqg, dke, dv))
        t_mid, t_qg, t_ke = dkr * kr - dqr * qr, dqg * qg, dke * ke
        dq = dqr * e_q + dqg * e_g
        dk = dkr * e_k + dke * e_e
        crow = lax.broadcasted_iota(jnp.int32, (CHUNK, 1), 0)
        ends = []
        for c in range(nc):
            sl = slice(c * CHUNK, (c + 1) * CHUNK)
            dgm = jnp.sum(t_mid[sl], axis=0, keepdims=True)
            dgl = jnp.sum(t_ke[sl], axis=0, keepdims=True) + jnp.concatenate(ddecay[c], axis=1)
            ends.append(jnp.where(crow == CHUNK // 2 - 1, dgm, 0.0) + jnp.where(crow == CHUNK - 1, dgl, 0.0))
        dG = t_qg - t_ke - t_mid + jnp.concatenate(ends, axis=0)
        dlogf = _tri_apply(tri_ref[1], dG)
        df = dlogf / f - dk
        du_ref[1] = (dq * (sq * (1.0 + zq * (1.0 - sq)))).astype(BF16)
        du_ref[2] = (df * (1.0 - lb) * (sig * (1.0 - sig))).astype(BF16)
        du_ref[3] = dv.astype(BF16)
        dlb = jnp.sum(df * (1.0 - sig), axis=0, keepdims=True) * (lb * (1.0 - lb))
        for h in heads:
            accumulate(dgn_ref, h, dgn[:, cols[h]])
            accumulate(dlb_ref, h, dlb[:, cols[h]])

    def rb(s, i):
        return s * tps + (tps - 1 - i)

    def per_head_spec(*shape):
        return pl.BlockSpec((H,) + shape, lambda g, s, i: (g,) + (0,) * len(shape))

    vec, mat = per_head_spec(1, W), per_head_spec(W, W)
    return pl.pallas_call(
        body, name="mixer_bwd", grid=(4 // H, seqs, tps),
        in_specs=[pl.BlockSpec((5, tm, H * W), lambda g, s, i: (0, rb(s, i), g)),
                  pl.BlockSpec((None, POOL_HALO, H * W), lambda g, s, i: (0, jnp.maximum(rb(s, i) * hb - 1, 0), g)),
                  pl.BlockSpec((2, tm, H * W), lambda g, s, i: (0, rb(s, i), g)),
                  pl.BlockSpec((tm, H * W), lambda g, s, i: (rb(s, i), g)),
                  pl.BlockSpec((nc, H, W, W), lambda g, s, i: (rb(s, i), g, 0, 0)),
                  mat, vec, per_head_spec(2, W), vec, _tri_spec(tm), _tri_spec(tm),
                  pl.BlockSpec(after.shape, lambda g, s, i: (0, 0))],
        out_specs=[pl.BlockSpec((5, tm, H * W), lambda g, s, i: (0, rb(s, i), g)), mat, vec, vec, vec],
        out_shape=[jax.ShapeDtypeStruct((5, T, 4 * W), BF16),
                   jax.ShapeDtypeStruct((4, W, W), F32),
                   jax.ShapeDtypeStruct((4, 1, W), F32),
                   jax.ShapeDtypeStruct((4, 1, W), F32),
                   jax.ShapeDtypeStruct((4, 1, W), F32)],
        scratch_shapes=[pltpu.VMEM((H, POOL_HALO, W), F32),
                        pltpu.VMEM((H, 4, tm + 32, W), F32),
                        pltpu.VMEM((H, W, W), F32)],
        compiler_params=_cparams(("arbitrary", "arbitrary", "arbitrary")),
    )(u5, u5, dy2, o_pre, st_prev, pool_w_bf, scale4, theta4, gn4, tri_bf, tri_f, after)


def _attn_probs(q, k, hd):
    s = _dot_nt(q, k) * (1.0 / (hd ** 0.5))
    e = jnp.exp(s - jnp.max(s, axis=-1, keepdims=True))
    return e * (1.0 / jnp.sum(e, axis=-1, keepdims=True))


def attn_fwd(q, kv3, *, seqs, seq_len, n_mem, tm):
    T, D = q.shape
    hd = D // XATTN_HEADS
    tps = seq_len // tm

    cols = [slice(h * hd, (h + 1) * hd) for h in range(XATTN_HEADS)]

    def body(q_ref, kv_ref, o_ref):
        p = [_attn_probs(q_ref[:, cs], kv_ref[0, :, cs], hd) for cs in cols]
        for h, cs in enumerate(cols):
            o_ref[:, cs] = _dot(p[h].astype(BF16), kv_ref[1, :, cs]).astype(BF16)

    return pl.pallas_call(
        body, name="attn_fwd", grid=(seqs, tps),
        in_specs=[pl.BlockSpec((tm, D), lambda b, i: (b * tps + i, 0)),
                  pl.BlockSpec((2, n_mem, D), lambda b, i: (0, b, 0))],
        out_specs=pl.BlockSpec((tm, D), lambda b, i: (b * tps + i, 0)),
        out_shape=jax.ShapeDtypeStruct((T, D), BF16),
        compiler_params=_cparams(("parallel", "arbitrary")),
    )(q, kv3)


def attn_bwd(q, kv3, do, *, seqs, seq_len, n_mem, tm):
    T, D = q.shape
    hd = D // XATTN_HEADS
    tps = seq_len // tm

    cols = [slice(h * hd, (h + 1) * hd) for h in range(XATTN_HEADS)]

    def body(q_ref, kv_ref, do_ref, dq_ref, dkv_ref):
        i = pl.program_id(1)

        @pl.when(i == 0)
        def _():
            dkv_ref[...] = jnp.zeros_like(dkv_ref)

        p = [_attn_probs(q_ref[:, cs], kv_ref[0, :, cs], hd) for cs in cols]
        dp = [_dot_nt(do_ref[:, cs], kv_ref[1, :, cs]) for cs in cols]
        ds = [(p[h] * (dp[h] - jnp.sum(dp[h] * p[h], axis=-1, keepdims=True)) * (1.0 / (hd ** 0.5))).astype(BF16)
              for h in range(XATTN_HEADS)]
        for h, cs in enumerate(cols):
            dq_ref[:, cs] = _dot(ds[h], kv_ref[0, :, cs]).astype(BF16)
            dkv_ref[0, :, cs] += _dot_tn(ds[h], q_ref[:, cs])
            dkv_ref[1, :, cs] += _dot_tn(p[h].astype(BF16), do_ref[:, cs])

    qspec = pl.BlockSpec((tm, D), lambda b, i: (b * tps + i, 0))
    kvspec = pl.BlockSpec((2, n_mem, D), lambda b, i: (0, b, 0))
    return pl.pallas_call(
        body, name="attn_bwd", grid=(seqs, tps),
        in_specs=[qspec, kvspec, qspec],
        out_specs=[qspec, kvspec],
        out_shape=[jax.ShapeDtypeStruct((T, D), BF16), jax.ShapeDtypeStruct((2, seqs * n_mem, D), F32)],
        compiler_params=_cparams(("parallel", "arbitrary")),
    )(q, kv3, do)


def _my_place():
    return lax.axis_index("x"), lax.axis_index("y"), lax.axis_index("c")


def _slot_of(px, py, pc):
    return 4 * px + 2 * py + pc


def _peer(k, x, y, c):
    return (1 - x if (k >> 2) & 1 else x, 1 - y if (k >> 1) & 1 else y, 1 - c if k & 1 else c)


def _split_copies(src_refs, land_refs, send_sems, recv_sems, scatter):
    x, y, c = _my_place()
    mine = _slot_of(x, y, c)
    copies = []
    for a, (src, land) in enumerate(zip(src_refs, land_refs)):
        for k in range(1, N_DEV):
            peer = _peer(k, x, y, c)
            copies.append(pltpu.make_async_remote_copy(
                src_ref=src.at[_slot_of(*peer)] if scatter else src, dst_ref=land.at[mine],
                send_sem=send_sems.at[a * N_PEERS + k - 1], recv_sem=recv_sems.at[a * N_PEERS + k - 1],
                device_id=peer, device_id_type=MESH))
    return copies


def split_start(groups, *, name, scatter):
    sizes = [len(srcs) for srcs, _ in groups]
    n_arr = sum(sizes)
    flat = [a for srcs, lands in groups for a in list(srcs) + list(lands)]

    def body(*refs):
        ins = refs[:2 * n_arr]
        sems = refs[4 * n_arr:4 * n_arr + 2 * len(groups)]
        token = refs[-1]
        at = 0
        for gi, n in enumerate(sizes):
            for cp in _split_copies(ins[at:at + n], ins[at + n:at + 2 * n], sems[2 * gi], sems[2 * gi + 1], scatter):
                cp.start()
            at += 2 * n
        token[...] = jnp.zeros_like(token)

    sem_shapes = []
    for n in sizes:
        sem_shapes += [pltpu.SemaphoreType.DMA((n * N_PEERS,))] * 2
    outs = pl.pallas_call(
        body, name=name,
        out_shape=tuple(pltpu.HBM(a.shape, a.dtype) for a in flat) + tuple(sem_shapes)
        + (jax.ShapeDtypeStruct((8, 128), F32),),
        in_specs=(HBM,) * len(flat),
        out_specs=(HBM,) * len(flat) + (SEM,) * len(sem_shapes) + (pl.BlockSpec(memory_space=pltpu.VMEM),),
        input_output_aliases={i: i for i in range(len(flat))},
        compiler_params=pltpu.CompilerParams(has_side_effects=pltpu.SideEffectType.DATAFLOW_SIDE_EFFECTING),
    )(*[pltpu.with_memory_space_constraint(a, pltpu.HBM) for a in flat])
    thru, sems, token = outs[:len(flat)], outs[len(flat):-1], outs[-1]
    started, at = [], 0
    for gi, n in enumerate(sizes):
        started.append((sems[2 * gi], sems[2 * gi + 1], thru[at:at + n], thru[at + n:at + 2 * n]))
        at += 2 * n
    return started, token


def split_wait(started, after, *, name, scatter):
    sizes = [len(g[2]) for g in started]
    n_arr = sum(sizes)
    flat = [a for g in started for a in list(g[2]) + list(g[3])]
    sems = [s for g in started for s in g[:2]]

    def body(*refs):
        ins = refs[:2 * n_arr]
        sem_refs = refs[2 * n_arr:2 * n_arr + len(sems)]
        at = 0
        for gi, n in enumerate(sizes):
            for cp in _split_copies(ins[at:at + n], ins[at + n:at + 2 * n], sem_refs[2 * gi], sem_refs[2 * gi + 1], scatter):
                cp.wait_send()
                cp.wait_recv()
            at += 2 * n

    outs = pl.pallas_call(
        body, name=name,
        out_shape=tuple(pltpu.HBM(a.shape, a.dtype) for a in flat),
        in_specs=(HBM,) * len(flat) + (SEM,) * len(sems) + (pl.BlockSpec(memory_space=pl.ANY),),
        out_specs=(HBM,) * len(flat),
        input_output_aliases={i: i for i in range(len(flat))},
        compiler_params=pltpu.CompilerParams(has_side_effects=pltpu.SideEffectType.DATAFLOW_SIDE_EFFECTING),
    )(*flat, *sems, after)
    done, at = [], 0
    for n in sizes:
        done.append((outs[at:at + n], outs[at + n:at + 2 * n]))
        at += 2 * n
    return done


SIBLING = 1
CHIP_PEERS = (2, 4, 6)
_SIDE_EFFECTS = pltpu.CompilerParams(has_side_effects=pltpu.SideEffectType.DATAFLOW_SIDE_EFFECTING)


def _chip_level_copies(src, land, send_sems, recv_sems):
    x, y, c = _my_place()
    return [pltpu.make_async_remote_copy(
        src_ref=src, dst_ref=land.at[_slot_of(x, y, c)], send_sem=send_sems.at[j], recv_sem=recv_sems.at[j],
        device_id=_peer(k, x, y, c), device_id_type=MESH) for j, k in enumerate((SIBLING,) + CHIP_PEERS)]


def _pass_on_copies(land, send_sems, recv_sems, receiving):
    x, y, c = _my_place()
    copies = []
    for j, k in enumerate(CHIP_PEERS):
        slot = _slot_of(*_peer(k ^ SIBLING if receiving else k, x, y, c))
        copies.append(pltpu.make_async_remote_copy(
            src_ref=land.at[slot], dst_ref=land.at[slot], send_sem=send_sems.at[j], recv_sem=recv_sems.at[j],
            device_id=_peer(SIBLING, x, y, c), device_id_type=MESH))
    return copies


def gather2_start(src, land, *, name):
    def body(src_ref, land_ref, src_out, land_out, send_sems, recv_sems, token):
        for cp in _chip_level_copies(src_ref, land_ref, send_sems, recv_sems):
            cp.start()
        token[...] = jnp.zeros_like(token)

    n = 1 + len(CHIP_PEERS)
    src_t, land_t, send_sems, recv_sems, token = pl.pallas_call(
        body, name=name,
        out_shape=(pltpu.HBM(src.shape, src.dtype), pltpu.HBM(land.shape, land.dtype),
                   pltpu.SemaphoreType.DMA((n,)), pltpu.SemaphoreType.DMA((n,)), jax.ShapeDtypeStruct((8, 128), F32)),
        in_specs=(HBM, HBM), out_specs=(HBM, HBM, SEM, SEM, pl.BlockSpec(memory_space=pltpu.VMEM)),
        input_output_aliases={0: 0, 1: 1}, compiler_params=_SIDE_EFFECTS,
    )(pltpu.with_memory_space_constraint(src, pltpu.HBM), pltpu.with_memory_space_constraint(land, pltpu.HBM))
    return (src_t, land_t, send_sems, recv_sems), token


def gather2_pass_on(started, after, *, name):
    src, land, send_a, recv_a = started

    def body(src_ref, land_ref, send_a_ref, recv_a_ref, after_ref, land_out, send_b, recv_b):
        for cp in _chip_level_copies(src_ref, land_ref, send_a_ref, recv_a_ref):
            cp.wait_send()
            cp.wait_recv()
        for cp in _pass_on_copies(land_ref, send_b, recv_b, False):
            cp.start()

    n = len(CHIP_PEERS)
    land_t, send_b, recv_b = pl.pallas_call(
        body, name=name,
        out_shape=(pltpu.HBM(land.shape, land.dtype), pltpu.SemaphoreType.DMA((n,)), pltpu.SemaphoreType.DMA((n,))),
        in_specs=(HBM, HBM, SEM, SEM, pl.BlockSpec(memory_space=pl.ANY)), out_specs=(HBM, SEM, SEM),
        input_output_aliases={1: 0}, compiler_params=_SIDE_EFFECTS,
    )(src, land, send_a, recv_a, after)
    return land_t, send_b, recv_b


def gather2_wait(passed, *, name):
    land, send_b, recv_b = passed

    def body(land_ref, send_ref, recv_ref, land_out):
        for cp in _pass_on_copies(land_ref, send_ref, recv_ref, False):
            cp.wait_send()
        for cp in _pass_on_copies(land_ref, send_ref, recv_ref, True):
            cp.wait_recv()

    return pl.pallas_call(
        body, name=name, out_shape=pltpu.HBM(land.shape, land.dtype),
        in_specs=(HBM, SEM, SEM), out_specs=HBM,
        input_output_aliases={0: 0}, compiler_params=_SIDE_EFFECTS,
    )(land, send_b, recv_b)


def _adamw_math(g, w, m, v):
    c1 = 1.0 - ADAM_B1 ** ADAM_STEP
    c2 = 1.0 - ADAM_B2 ** ADAM_STEP
    nm = ADAM_B1 * m + (1.0 - ADAM_B1) * g
    nv = ADAM_B2 * v + (1.0 - ADAM_B2) * (g * g)
    delta = -ADAM_LR * ((nm / c1) / (jnp.sqrt(nv / c2) + ADAM_EPS) + ADAM_WD * w)
    return delta, nm, nv


def adamw_sharded(me, owns, recvs, ws, ms, vs, *, name, steps):
    n = len(ws)

    def body(me_ref, *refs):
        for t in range(n):
            parts = refs[t * N_DEV:(t + 1) * N_DEV]
            w_ref, m_ref, v_ref = refs[n * N_DEV + 3 * t:n * N_DEV + 3 * t + 3]
            g_ref, d_ref, nm_ref, nv_ref = refs[n * (N_DEV + 3) + 4 * t:n * (N_DEV + 3) + 4 * t + 4]
            g = parts[0][...].astype(F32)
            for p in parts[1:]:
                g = g + p[...].astype(F32)
            g_ref[...] = g
            d_ref[...], nm_ref[...], nv_ref[...] = _adamw_math(g, w_ref[...], m_ref[...], v_ref[...])

    def slab(t, k):
        R, C = ws[t].shape
        return pl.BlockSpec((None, R // steps, C), lambda i, me_ref: (me_ref[0] ^ k, i, 0))

    def blk(t):
        R, C = ws[t].shape
        return pl.BlockSpec((R // steps, C), lambda i, me_ref: (i, 0))

    args = [me]
    for t in range(n):
        args += [owns[t]] + [recvs[t]] * N_PEERS
    for t in range(n):
        args += [ws[t], ms[t], vs[t]]
    outs = pl.pallas_call(
        body, name=name,
        grid_spec=pltpu.PrefetchScalarGridSpec(
            num_scalar_prefetch=1, grid=(steps,),
            in_specs=[slab(t, k) for t in range(n) for k in range(N_DEV)] + [blk(t) for t in range(n) for _ in range(3)],
            out_specs=[blk(t) for t in range(n) for _ in range(4)]),
        out_shape=[jax.ShapeDtypeStruct(ws[t].shape, F32) for t in range(n) for _ in range(4)],
        compiler_params=_cparams(("parallel",)),
    )(*args)
    return [outs[4 * t:4 * t + 4] for t in range(n)]


def adamw_replicated(parts, ws, ms, vs, rows):
    n_buf, n_par = len(parts), len(ws)

    def body(*refs):
        p_refs = refs[:n_buf]
        w_refs = refs[n_buf:n_buf + n_par]
        m_refs = refs[n_buf + n_par:n_buf + 2 * n_par]
        v_refs = refs[n_buf + 2 * n_par:n_buf + 3 * n_par]
        outs = refs[n_buf + 3 * n_par:]
        sums = []
        for p in p_refs:
            g = p[0]
            for s in range(1, N_DEV):
                g = g + p[s]
            sums.append(g)
        for j, (b, r0, nr) in enumerate(rows):
            g = sums[b][r0:r0 + nr]
            delta, nm, nv = _adamw_math(g, w_refs[j][...], m_refs[j][...], v_refs[j][...])
            outs[j][...] = g
            outs[n_par + j][...] = delta
            outs[2 * n_par + j][...] = nm
            outs[3 * n_par + j][...] = nv

    shapes = [jax.ShapeDtypeStruct(w.shape, F32) for w in ws]
    outs = pl.pallas_call(
        body, name="adamw_replicated", out_shape=shapes * 4,
        compiler_params=pltpu.CompilerParams(vmem_limit_bytes=V7X_VMEM_LIMIT),
    )(*parts, *ws, *ms, *vs)
    return outs[:n_par], outs[n_par:2 * n_par], outs[2 * n_par:3 * n_par], outs[3 * n_par:]


BIG = ("w_in", "w_out", "xw_q", "xw_kv", "xw_o", "w_up", "w_down")
ADAMW_STEPS = 4
WEIGHTS = ("norm_mix", "w_in", "pool_w", "pool_scale", "lb_theta", "hgrn_norm", "w_out", "norm_xq",
           "norm_mem", "xw_q", "xw_kv", "xw_o", "norm_mlp", "w_up", "w_down", "norm_final")
SMALL = (("pool_w", (4 * HEAD_W, HEAD_W), 0, 0),
         ("norm_mix", (1, 1024), 1, 0), ("norm_xq", (1, 1024), 1, 1), ("norm_mem", (1, 1024), 1, 2),
         ("norm_mlp", (1, 1024), 1, 3), ("norm_final", (1, 1024), 1, 4),
         ("pool_scale", (1, 512), 2, 0), ("hgrn_norm", (1, 512), 2, 1), ("lb_theta", (2, 512), 2, 2))


def _pad_rows(a, rows):
    return jnp.concatenate([a, jnp.zeros((rows - a.shape[0], a.shape[1]), a.dtype)], axis=0)


def kernel(x, mem, norm_mix, w_in, pool_w, pool_scale, lb_theta, hgrn_norm, w_out, norm_xq, norm_mem, xw_q, xw_kv, xw_o, norm_mlp, w_up, w_down, norm_final, loss_target, m_norm_mix, m_w_in, m_pool_w, m_pool_scale, m_lb_theta, m_hgrn_norm, m_w_out, m_norm_xq, m_norm_mem, m_xw_q, m_xw_kv, m_xw_o, m_norm_mlp, m_w_up, m_w_down, m_norm_final, v_norm_mix, v_w_in, v_pool_w, v_pool_scale, v_lb_theta, v_hgrn_norm, v_w_out, v_norm_xq, v_norm_mem, v_xw_q, v_xw_kv, v_xw_o, v_norm_mlp, v_w_up, v_w_down, v_norm_final):
    w = dict(norm_mix=norm_mix, w_in=w_in, pool_w=pool_w, pool_scale=pool_scale, lb_theta=lb_theta,
             hgrn_norm=hgrn_norm, w_out=w_out, norm_xq=norm_xq, norm_mem=norm_mem, xw_q=xw_q, xw_kv=xw_kv,
             xw_o=xw_o, norm_mlp=norm_mlp, w_up=w_up, w_down=w_down, norm_final=norm_final)
    mom = dict(norm_mix=m_norm_mix, w_in=m_w_in, pool_w=m_pool_w, pool_scale=m_pool_scale, lb_theta=m_lb_theta,
               hgrn_norm=m_hgrn_norm, w_out=m_w_out, norm_xq=m_norm_xq, norm_mem=m_norm_mem, xw_q=m_xw_q,
               xw_kv=m_xw_kv, xw_o=m_xw_o, norm_mlp=m_norm_mlp, w_up=m_w_up, w_down=m_w_down,
               norm_final=m_norm_final)
    var = dict(norm_mix=v_norm_mix, w_in=v_w_in, pool_w=v_pool_w, pool_scale=v_pool_scale, lb_theta=v_lb_theta,
               hgrn_norm=v_hgrn_norm, w_out=v_w_out, norm_xq=v_norm_xq, norm_mem=v_norm_mem, xw_q=v_xw_q,
               xw_kv=v_xw_kv, xw_o=v_xw_o, norm_mlp=v_norm_mlp, w_up=v_w_up, w_down=v_w_down,
               norm_final=v_norm_final)

    seqs, seq_len, D = x.shape
    n_mem = mem.shape[1]
    T = seqs * seq_len
    W = HEAD_W
    x2 = x.reshape(T, D)
    mem2 = mem.reshape(seqs * n_mem, D)
    tgt2 = loss_target.reshape(T, D)
    tm_big = min(1024, T)
    tm_mid = min(512, T)
    tm_sq = min(1024, T)
    tm_mix = min(256, seq_len)
    tm_att = min(1024, seq_len)
    tkv = min(512, seqs * n_mem)
    px, py, pc = _my_place()
    me = _slot_of(px, py, pc).astype(jnp.int32)
    me1 = me.reshape(1)

    shard_bf = {n: w[n][0].astype(BF16) for n in BIG}

    def landing(n):
        zone = lax.empty((N_DEV,) + shard_bf[n].shape, BF16)
        return lax.dynamic_update_slice(zone, shard_bf[n][None], (me, 0, 0))

    w_in_started, tok = gather2_start(shard_bf["w_in"], landing("w_in"), name="w_in_gather_start")
    shard_bf["w_out"] = shard_bf["w_out"] + tok[0, 0].astype(BF16)
    ag_groups = (("w_out", "xw_q", "xw_kv", "xw_o"), ("w_up",), ("w_down",))
    ag_started, tok = split_start([([shard_bf[n] for n in grp], [landing(n) for n in grp]) for grp in ag_groups],
                                name="weights_gather_start", scatter=False)

    pool_w_bf = pool_w[0].astype(BF16)
    scale4 = pool_scale.reshape(4, 1, W)
    gn4 = hgrn_norm.reshape(4, 1, W)
    theta4 = lb_theta.reshape(2, 4, W).transpose(1, 0, 2)
    g_final = norm_final.reshape(1, D)

    n1 = prenorm(x2, norm_mix, tok, tm=tm_sq)
    wi3 = gather2_wait(gather2_pass_on(w_in_started, n1, name="w_in_gather_pass_on"), name="w_in_gather_wait")
    full_w_in = wi3.transpose(1, 0, 2).reshape(D, -1)
    u5 = proj_plain(n1, full_w_in, name="in_proj", tm=tm_sq, tn=4 * W, out_dtype=F32, out_slabs=5)
    tri_bf, tri_f = chunk_triangles(tm_mix)
    y2, o_pre, st_prev = mixer_fwd(u5, pool_w_bf, scale4, theta4, gn4, tri_bf, tri_f, seqs=seqs, seq_len=seq_len,
                                   tm=tm_mix)
    (_, (wo3, wq3, wkv3, wao3)), = split_wait(ag_started[0:1], y2, name="weights_gather_wait_attn", scatter=False)
    full_w_out, full_xw_q, full_xw_o = wo3.reshape(D, D), wq3.reshape(D, D), wao3.reshape(D, D)
    tn = 4 * W
    h1, n2, q = proj_res_norm(y2, full_w_out, x2, norm_xq, full_xw_q, name="out_q_proj", tm=tm_sq, tn=tn)
    kv3, memn = proj_norm(mem2, norm_mem, wkv3, name="kv_proj", tm=tkv, tn=wkv3.shape[2], out_dtype=BF16,
                          out_slabs=2)
    o_att = attn_fwd(q, kv3, seqs=seqs, seq_len=seq_len, n_mem=n_mem, tm=tm_att)
    h2, n3 = proj_res_norm(o_att, full_xw_o, h1, norm_mlp, name="attn_out_proj", tm=tm_sq, tn=tn)
    (_, (wup3,)), = split_wait(ag_started[1:2], h2, name="weights_gather_wait_up", scatter=False)
    tn_up = wup3.shape[2]
    aa = proj_plain(n3, wup3, name="up_proj", tm=tm_mid, tn=tn_up, relu2=True)
    (_, (wdn3,)), = split_wait(ag_started[2:3], aa, name="weights_gather_wait_down", scatter=False)
    full_w_down = wdn3.reshape(-1, D)
    dh3, dh3b, sq_err, dg_final = proj_res_loss(aa, full_w_down, h2, g_final, tgt2, name="down_proj_loss",
                                                tm=tm_mid, tn=tn)

    def send(parts, name):
        srcs = [p.reshape((N_DEV, -1, p.shape[-1])) for p in parts]
        lands = [lax.empty(s.shape, BF16) for s in srcs]
        started, token = split_start([(srcs, lands)], name=name, scatter=True)
        return started[0], token

    gw_down = wgrad(aa, dh3b, name="down_proj_wgrad", tt=tm_mid, tn=tn)
    dap = back_plain(dh3b, full_w_down, name="down_proj_bwd", tm=tm_mid, tn=tn, out_dtype=BF16, relu2_value=aa)
    gw_up = wgrad(n3, dap, name="up_proj_wgrad", tt=tm_mid, tn=tn_up, out_slabs=N_DEV)
    sent_mlp, tok = send([gw_down, gw_up], "grads_send_mlp")
    dh2, dh2b, do_att, dg_mlp = back_norm(dap, wup3, h2, norm_mlp, dh3, name="up_proj_bwd", tm=tm_mid, tk=tn_up,
                                          w_next=full_xw_o, after=tok)
    gxw_o = wgrad(o_att, dh2b, name="attn_out_proj_wgrad", tt=tm_sq, tn=tn)
    dq, dkv3 = attn_bwd(q, kv3, do_att, seqs=seqs, seq_len=seq_len, n_mem=n_mem, tm=tm_att)
    gxw_q = wgrad(n2, dq, name="q_proj_wgrad", tt=tm_sq, tn=tn)
    gxw_kv = wgrad(memn, dkv3, name="kv_proj_wgrad", tt=tkv, tn=wkv3.shape[2], out_slabs=N_DEV)
    dg_mem = back_norm(dkv3, wkv3, mem2, norm_mem, None, name="kv_proj_bwd", tm=tkv, tk=wkv3.shape[2])
    dh1, dh1b, dy2, dg_xq = back_norm(dq, full_xw_q, h1, norm_xq, dh2, name="q_proj_bwd", tm=tm_sq, tk=D,
                                      w_next=full_w_out, next_slabs=2)
    gw_out = wgrad(y2, dh1b, name="out_proj_wgrad", tt=tm_sq, tn=tn)
    sent_attn, tok = send([gxw_o, gxw_q, gxw_kv, gw_out], "grads_send_attn")
    du5, dpw, dsc, dlb, dgn = mixer_bwd(u5, dy2, o_pre, st_prev, pool_w_bf, scale4, theta4, gn4, tri_bf, tri_f, tok,
                                        seqs=seqs, seq_len=seq_len, tm=tm_mix)
    gw_in = wgrad(n1, du5, name="in_proj_wgrad", tt=tm_sq, tn=tn)
    gw_in_slots = gw_in.reshape(D, N_DEV, -1).transpose(1, 0, 2)
    sent_in, tok = send([gw_in_slots], "grads_send_in")
    dx, dg_mix = back_norm(du5, full_w_in, x2, norm_mix, dh1, name="in_proj_bwd", tm=tm_sq, tk=tn, bf16_copy=False,
                           after=tok)

    dlb_row = dlb.reshape(1, 4 * W)
    buf_vec = _pad_rows(jnp.concatenate([dg_mix, dg_xq, dg_mem, dg_mlp, dg_final, sq_err], axis=0), 8)
    buf_half = _pad_rows(jnp.concatenate([dsc.reshape(1, 4 * W), dgn.reshape(1, 4 * W), dlb_row, -dlb_row], axis=0), 8)
    small_src = [dpw.reshape(4 * W, W), buf_vec, buf_half]
    small_land = [lax.dynamic_update_slice(lax.empty((N_DEV,) + b.shape, F32), b[None], (me, 0, 0))
                  for b in small_src]
    small_started, tok = split_start([(small_src, small_land)], name="small_grads_start", scatter=False)

    done = split_wait([sent_mlp, sent_attn, sent_in], tok, name="grads_wait", scatter=True)
    slots = dict(w_down=(0, 0), w_up=(0, 1), xw_o=(1, 0), xw_q=(1, 1), xw_kv=(1, 2), w_out=(1, 3), w_in=(2, 0))
    own = {n: done[gi][0][ai] for n, (gi, ai) in slots.items()}
    got = {n: done[gi][1][ai] for n, (gi, ai) in slots.items()}
    res = {}
    r = adamw_sharded(me1, [own[n] for n in BIG], [got[n] for n in BIG], [w[n][0] for n in BIG],
                      [mom[n][0] for n in BIG], [var[n][0] for n in BIG], name="adamw_shards", steps=ADAMW_STEPS)
    for n, outs in zip(BIG, r):
        for kind, a in zip("gdmv", outs):
            res[kind, n] = a.reshape(w[n].shape)
    (_, small_parts), = split_wait(small_started, res["g", BIG[-1]], name="small_grads_wait", scatter=False)
    loss = 0.5 * jnp.sum(small_parts[1][:, 5, :]) / D
    r = adamw_replicated(small_parts, [w[n].reshape(v2) for n, v2, _, _ in SMALL],
                         [mom[n].reshape(v2) for n, v2, _, _ in SMALL],
                         [var[n].reshape(v2) for n, v2, _, _ in SMALL],
                         [(b, r0, v2[0]) for _, v2, b, r0 in SMALL])
    for kind, arrs in zip("gdmv", r):
        for (n, _, _, _), a in zip(SMALL, arrs):
            res[kind, n] = a.reshape(w[n].shape)

    out = [loss, dx.reshape(x.shape)]
    for kind in "gdmv":
        out += [res[kind, n] for n in WEIGHTS]
    return tuple(out)
```
